```python
import math
import jax, jax.numpy as jnp
from jax import lax
import numpy as np

D_MODEL = 1024
BATCH = 8
SEQ = 4096
DEPTH = 1

N_HEADS = 8
QK_NOPE_DIM = 64
QK_ROPE_DIM = 32
QK_HEAD_DIM = QK_NOPE_DIM + QK_ROPE_DIM
V_HEAD_DIM = 64
Q_LORA_RANK = 384
KV_LORA_RANK = 256
ROPE_THETA = 10000.0
Q_BLOCK = 128
SSM_WIDTH = 512
SSM_GROUP = 16
SSM_GROUPS = SSM_WIDTH // SSM_GROUP
SSM_STATE = 64
DT_MIN = 0.001
DT_MAX = 0.1
N_BRANCHES = 2
D_FF = 2816
CONV_WIDTH = 3
EPS = 1e-6

OFF_CQ = 0
OFF_CKV = OFF_CQ + Q_LORA_RANK
OFF_KR = OFF_CKV + KV_LORA_RANK
OFF_U = OFF_KR + QK_ROPE_DIM
OFF_GATE = OFF_U + SSM_WIDTH
D_IN = OFF_GATE + N_BRANCHES * D_MODEL

kernel_name = "hybrid_mla_s5_gated_convffn"


def _rmsnorm(x, g):
    xf = x.astype(jnp.float32)
    y = xf * lax.rsqrt(jnp.mean(xf * xf, axis=-1, keepdims=True) + EPS)
    return (y * g.astype(jnp.float32)).astype(x.dtype)


def _rope(x, cos, sin):
    half = x.shape[-1] // 2
    x1, x2 = x[..., :half], x[..., half:]
    return jnp.concatenate([x1 * cos - x2 * sin, x2 * cos + x1 * sin], axis=-1)


def _rope_tables(positions, dtype):
    inv_freq = ROPE_THETA ** (-jnp.arange(0, QK_ROPE_DIM, 2, dtype=jnp.float32) / QK_ROPE_DIM)
    ang = positions.astype(jnp.float32)[..., None] * inv_freq
    return jnp.cos(ang).astype(dtype), jnp.sin(ang).astype(dtype)


def _mla(c_q, c_kv, k_r, cos, sin, q_norm, w_uq, kv_norm, w_uk, w_uv):
    B, L, _ = c_q.shape
    q = (_rmsnorm(c_q, q_norm) @ w_uq).reshape(B, L, N_HEADS, QK_HEAD_DIM)
    q_nope = q[..., :QK_NOPE_DIM]
    q_rope = _rope(q[..., QK_NOPE_DIM:], cos[:, :, None, :], sin[:, :, None, :])
    ckv = _rmsnorm(c_kv, kv_norm)
    k_nope = (ckv @ w_uk).reshape(B, L, N_HEADS, QK_NOPE_DIM)
    v = (ckv @ w_uv).reshape(B, L, N_HEADS, V_HEAD_DIM)
    k_rope = _rope(k_r, cos, sin)
    scale = 1.0 / math.sqrt(QK_HEAD_DIM)
    nblk = L // Q_BLOCK
    qn_blk = q_nope.reshape(B, nblk, Q_BLOCK, N_HEADS, QK_NOPE_DIM).transpose(1, 0, 2, 3, 4)
    qr_blk = q_rope.reshape(B, nblk, Q_BLOCK, N_HEADS, QK_ROPE_DIM).transpose(1, 0, 2, 3, 4)
    k_idx = jnp.arange(L)

    def block(args):
        qn, qr, i = args
        s = jnp.einsum('bqhd,bkhd->bhqk', qn, k_nope) + jnp.einsum('bqhd,bkd->bhqk', qr, k_rope)
        s = s.astype(jnp.float32) * scale
        q_idx = i * Q_BLOCK + jnp.arange(Q_BLOCK)
        s = jnp.where(k_idx[None, :] <= q_idx[:, None], s, -jnp.inf)
        p = jax.nn.softmax(s, axis=-1).astype(v.dtype)
        return jnp.einsum('bhqk,bkhd->bqhd', p, v)

    out = lax.map(block, (qn_blk, qr_blk, jnp.arange(nblk)))
    return out.transpose(1, 0, 2, 3, 4).reshape(B, L, N_HEADS * V_HEAD_DIM)


def _ssm_combine(e1, e2):
    a1r, a1i, b1r, b1i = e1
    a2r, a2i, b2r, b2i = e2
    return (a2r * a1r - a2i * a1i,
            a2r * a1i + a2i * a1r,
            a2r * b1r - a2i * b1i + b2r,
            a2r * b1i + a2i * b1r + b2i)


def _s5(u, lam_re, lam_im, log_dt, b_re, b_im, c_re, c_im, d_skip, w_glu, b_glu):
    B, L, _ = u.shape
    uf = u.astype(jnp.float32).reshape(B, L, SSM_GROUPS, SSM_GROUP)
    dt = jnp.exp(log_dt.astype(jnp.float32))[:, None]
    lr = lam_re.astype(jnp.float32)
    li = lam_im.astype(jnp.float32)
    mag = jnp.exp(lr * dt)
    ang = li * dt
    a_re, a_im = mag * jnp.cos(ang), mag * jnp.sin(ang)
    den = lr * lr + li * li
    n_re, n_im = a_re - 1.0, a_im
    z_re = (n_re * lr + n_im * li) / den
    z_im = (n_im * lr - n_re * li) / den
    br, bi = b_re.astype(jnp.float32), b_im.astype(jnp.float32)
    bb_re = z_re[..., None] * br - z_im[..., None] * bi
    bb_im = z_re[..., None] * bi + z_im[..., None] * br
    bu_re = jnp.einsum('blgh,gph->blgp', uf, bb_re)
    bu_im = jnp.einsum('blgh,gph->blgp', uf, bb_im)
    a_seq_re = jnp.broadcast_to(a_re, (L, SSM_GROUPS, SSM_STATE))
    a_seq_im = jnp.broadcast_to(a_im, (L, SSM_GROUPS, SSM_STATE))

    def scan_one(bur, bui):
        _, _, sr, si = lax.associative_scan(_ssm_combine, (a_seq_re, a_seq_im, bur, bui), axis=0)
        return sr, si

    s_re, s_im = jax.vmap(scan_one)(bu_re, bu_im)
    y = (jnp.einsum('blgp,ghp->blgh', s_re, c_re.astype(jnp.float32))
         - jnp.einsum('blgp,ghp->blgh', s_im, c_im.astype(jnp.float32))
         + d_skip.astype(jnp.float32) * uf)
    y = jax.nn.gelu(y.reshape(B, L, SSM_WIDTH), approximate=True)
    y = y * jax.nn.sigmoid(y @ w_glu.astype(jnp.float32) + b_glu.astype(jnp.float32))
    return y.astype(u.dtype)


def _causal_dwconv(h, w, b):
    L = h.shape[1]
    hp = jnp.pad(h, ((0, 0), (CONV_WIDTH - 1, 0), (0, 0)))
    out = b
    for k in range(CONV_WIDTH):
        out = out + w[k] * hp[:, k:k + L, :]
    return out


def _conv_ffn(x, w_up, conv_w, conv_b, w_down):
    h = _causal_dwconv(x @ w_up, conv_w, conv_b)
    gate, val = h[..., :D_FF], h[..., D_FF:]
    return (jax.nn.gelu(gate, approximate=True) * val) @ w_down


def _fwd_setup_inputs(seed: int = 0) -> dict:
    key = jax.random.key(seed)
    ks = iter(jax.random.split(key, 40))

    def nrm(shape, scale):
        return jax.random.normal(next(ks), shape, jnp.float32) * scale

    def gain(n):
        return 1.0 + nrm((DEPTH, n), 0.02)

    x = jax.random.normal(next(ks), (BATCH, SEQ, D_MODEL), jnp.float32)
    offs = jax.random.randint(next(ks), (BATCH, 1), 0, 1024, dtype=jnp.int32)
    positions = offs + jnp.arange(SEQ, dtype=jnp.int32)[None, :]
    lam_im0 = math.pi * jnp.arange(SSM_STATE, dtype=jnp.float32)
    return {
        "x": x,
        "positions": positions,
        "mix_norm_pre": gain(D_MODEL),
        "w_in": nrm((DEPTH, D_MODEL, D_IN), D_MODEL ** -0.5),
        "q_norm": gain(Q_LORA_RANK),
        "w_uq": nrm((DEPTH, Q_LORA_RANK, N_HEADS * QK_HEAD_DIM), Q_LORA_RANK ** -0.5),
        "kv_norm": gain(KV_LORA_RANK),
        "w_uk": nrm((DEPTH, KV_LORA_RANK, N_HEADS * QK_NOPE_DIM), KV_LORA_RANK ** -0.5),
        "w_uv": nrm((DEPTH, KV_LORA_RANK, N_HEADS * V_HEAD_DIM), KV_LORA_RANK ** -0.5),
        "ssm_lambda_re": -0.5 + nrm((DEPTH, SSM_GROUPS, SSM_STATE), 0.01),
        "ssm_lambda_im": lam_im0 + nrm((DEPTH, SSM_GROUPS, SSM_STATE), 0.01),
        "ssm_log_dt": jax.random.uniform(next(ks), (DEPTH, SSM_GROUPS), jnp.float32,
                                          math.log(DT_MIN), math.log(DT_MAX)),
        "ssm_b_re": nrm((DEPTH, SSM_GROUPS, SSM_STATE, SSM_GROUP), (2.0 * SSM_GROUP) ** -0.5),
        "ssm_b_im": nrm((DEPTH, SSM_GROUPS, SSM_STATE, SSM_GROUP), (2.0 * SSM_GROUP) ** -0.5),
        "ssm_c_re": nrm((DEPTH, SSM_GROUPS, SSM_GROUP, SSM_STATE), (2.0 * SSM_STATE) ** -0.5),
        "ssm_c_im": nrm((DEPTH, SSM_GROUPS, SSM_GROUP, SSM_STATE), (2.0 * SSM_STATE) ** -0.5),
        "ssm_d": nrm((DEPTH, SSM_GROUPS, SSM_GROUP), 1.0),
        "w_glu": nrm((DEPTH, SSM_WIDTH, SSM_WIDTH), SSM_WIDTH ** -0.5),
        "b_glu": nrm((DEPTH, SSM_WIDTH), 0.01),
        "w_branch_attn": nrm((DEPTH, N_HEADS * V_HEAD_DIM, D_MODEL), (N_HEADS * V_HEAD_DIM) ** -0.5),
        "w_branch_ssm": nrm((DEPTH, SSM_WIDTH, D_MODEL), SSM_WIDTH ** -0.5),
        "b_gate": nrm((DEPTH, N_BRANCHES * D_MODEL), 0.01),
        "w_out": nrm((DEPTH, D_MODEL, D_MODEL), D_MODEL ** -0.5),
        "mix_norm_post": gain(D_MODEL),
        "ffn_norm_pre": gain(D_MODEL),
        "w_up": nrm((DEPTH, D_MODEL, 2 * D_FF), D_MODEL ** -0.5),
        "conv_w": nrm((DEPTH, CONV_WIDTH, 2 * D_FF), CONV_WIDTH ** -0.5),
        "conv_b": nrm((DEPTH, 2 * D_FF), 0.01),
        "w_down": nrm((DEPTH, D_FF, D_MODEL), D_FF ** -0.5),
        "ffn_norm_post": gain(D_MODEL),
    }


def _fwd_reference(x, positions, mix_norm_pre, w_in, q_norm, w_uq, kv_norm, w_uk, w_uv,
              ssm_lambda_re, ssm_lambda_im, ssm_log_dt, ssm_b_re, ssm_b_im, ssm_c_re, ssm_c_im,
              ssm_d, w_glu, b_glu, w_branch_attn, w_branch_ssm, b_gate, w_out, mix_norm_post,
              ffn_norm_pre, w_up, conv_w, conv_b, w_down, ffn_norm_post):
    B, L, _ = x.shape
    cos, sin = _rope_tables(positions, x.dtype)
    for layer in range(DEPTH):
        hn = _rmsnorm(x, mix_norm_pre[layer])
        proj = hn @ w_in[layer]
        c_q = proj[..., OFF_CQ:OFF_CKV]
        c_kv = proj[..., OFF_CKV:OFF_KR]
        k_r = proj[..., OFF_KR:OFF_U]
        u = proj[..., OFF_U:OFF_GATE]
        gate_logits = proj[..., OFF_GATE:] + b_gate[layer]
        attn = _mla(c_q, c_kv, k_r, cos, sin, q_norm[layer], w_uq[layer],
                    kv_norm[layer], w_uk[layer], w_uv[layer])
        ssm = _s5(u, ssm_lambda_re[layer], ssm_lambda_im[layer], ssm_log_dt[layer],
                  ssm_b_re[layer], ssm_b_im[layer], ssm_c_re[layer], ssm_c_im[layer],
                  ssm_d[layer], w_glu[layer], b_glu[layer])
        gates = jax.nn.sigmoid(gate_logits.astype(jnp.float32)).astype(x.dtype)
        gates = gates.reshape(B, L, N_BRANCHES, D_MODEL)
        merged = (gates[:, :, 0] * (attn @ w_branch_attn[layer])
                  + gates[:, :, 1] * (ssm @ w_branch_ssm[layer]))
        x = x + _rmsnorm(merged @ w_out[layer], mix_norm_post[layer])
        hn = _rmsnorm(x, ffn_norm_pre[layer])
        ff = _conv_ffn(hn, w_up[layer], conv_w[layer], conv_b[layer], w_down[layer])
        x = x + _rmsnorm(ff, ffn_norm_post[layer])
    return x


import jax as _jax
import jax.numpy as _jnp

TWIN_FORMAT = 'train_step'
FWD_PARAMS = ['x', 'positions', 'mix_norm_pre', 'w_in', 'q_norm', 'w_uq', 'kv_norm', 'w_uk', 'w_uv', 'ssm_lambda_re', 'ssm_lambda_im', 'ssm_log_dt', 'ssm_b_re', 'ssm_b_im', 'ssm_c_re', 'ssm_c_im', 'ssm_d', 'w_glu', 'b_glu', 'w_branch_attn', 'w_branch_ssm', 'b_gate', 'w_out', 'mix_norm_post', 'ffn_norm_pre', 'w_up', 'conv_w', 'conv_b', 'w_down', 'ffn_norm_post']
TWIN_WEIGHTS = ['mix_norm_pre', 'w_in', 'q_norm', 'w_uq', 'kv_norm', 'w_uk', 'w_uv', 'ssm_lambda_re', 'ssm_lambda_im', 'ssm_log_dt', 'ssm_b_re', 'ssm_b_im', 'ssm_c_re', 'ssm_c_im', 'ssm_d', 'w_glu', 'b_glu', 'w_branch_attn', 'w_branch_ssm', 'b_gate', 'w_out', 'mix_norm_post', 'ffn_norm_pre', 'w_up', 'conv_w', 'conv_b', 'w_down', 'ffn_norm_post']
TWIN_DIFF_INPUT = 'x'
TWIN_INPUTS = ['x', 'positions', 'mix_norm_pre', 'w_in', 'q_norm', 'w_uq', 'kv_norm', 'w_uk', 'w_uv', 'ssm_lambda_re', 'ssm_lambda_im', 'ssm_log_dt', 'ssm_b_re', 'ssm_b_im', 'ssm_c_re', 'ssm_c_im', 'ssm_d', 'w_glu', 'b_glu', 'w_branch_attn', 'w_branch_ssm', 'b_gate', 'w_out', 'mix_norm_post', 'ffn_norm_pre', 'w_up', 'conv_w', 'conv_b', 'w_down', 'ffn_norm_post', 'loss_target', 'm_mix_norm_pre', 'm_w_in', 'm_q_norm', 'm_w_uq', 'm_kv_norm', 'm_w_uk', 'm_w_uv', 'm_ssm_lambda_re', 'm_ssm_lambda_im', 'm_ssm_log_dt', 'm_ssm_b_re', 'm_ssm_b_im', 'm_ssm_c_re', 'm_ssm_c_im', 'm_ssm_d', 'm_w_glu', 'm_b_glu', 'm_w_branch_attn', 'm_w_branch_ssm', 'm_b_gate', 'm_w_out', 'm_mix_norm_post', 'm_ffn_norm_pre', 'm_w_up', 'm_conv_w', 'm_conv_b', 'm_w_down', 'm_ffn_norm_post', 'v_mix_norm_pre', 'v_w_in', 'v_q_norm', 'v_w_uq', 'v_kv_norm', 'v_w_uk', 'v_w_uv', 'v_ssm_lambda_re', 'v_ssm_lambda_im', 'v_ssm_log_dt', 'v_ssm_b_re', 'v_ssm_b_im', 'v_ssm_c_re', 'v_ssm_c_im', 'v_ssm_d', 'v_w_glu', 'v_b_glu', 'v_w_branch_attn', 'v_w_branch_ssm', 'v_b_gate', 'v_w_out', 'v_mix_norm_post', 'v_ffn_norm_pre', 'v_w_up', 'v_conv_w', 'v_conv_b', 'v_w_down', 'v_ffn_norm_post']
TWIN_OUTPUTS = ['loss', 'grad_x', 'grad_mix_norm_pre', 'grad_w_in', 'grad_q_norm', 'grad_w_uq', 'grad_kv_norm', 'grad_w_uk', 'grad_w_uv', 'grad_ssm_lambda_re', 'grad_ssm_lambda_im', 'grad_ssm_log_dt', 'grad_ssm_b_re', 'grad_ssm_b_im', 'grad_ssm_c_re', 'grad_ssm_c_im', 'grad_ssm_d', 'grad_w_glu', 'grad_b_glu', 'grad_w_branch_attn', 'grad_w_branch_ssm', 'grad_b_gate', 'grad_w_out', 'grad_mix_norm_post', 'grad_ffn_norm_pre', 'grad_w_up', 'grad_conv_w', 'grad_conv_b', 'grad_w_down', 'grad_ffn_norm_post', 'delta_mix_norm_pre', 'delta_w_in', 'delta_q_norm', 'delta_w_uq', 'delta_kv_norm', 'delta_w_uk', 'delta_w_uv', 'delta_ssm_lambda_re', 'delta_ssm_lambda_im', 'delta_ssm_log_dt', 'delta_ssm_b_re', 'delta_ssm_b_im', 'delta_ssm_c_re', 'delta_ssm_c_im', 'delta_ssm_d', 'delta_w_glu', 'delta_b_glu', 'delta_w_branch_attn', 'delta_w_branch_ssm', 'delta_b_gate', 'delta_w_out', 'delta_mix_norm_post', 'delta_ffn_norm_pre', 'delta_w_up', 'delta_conv_w', 'delta_conv_b', 'delta_w_down', 'delta_ffn_norm_post', 'new_m_mix_norm_pre', 'new_m_w_in', 'new_m_q_norm', 'new_m_w_uq', 'new_m_kv_norm', 'new_m_w_uk', 'new_m_w_uv', 'new_m_ssm_lambda_re', 'new_m_ssm_lambda_im', 'new_m_ssm_log_dt', 'new_m_ssm_b_re', 'new_m_ssm_b_im', 'new_m_ssm_c_re', 'new_m_ssm_c_im', 'new_m_ssm_d', 'new_m_w_glu', 'new_m_b_glu', 'new_m_w_branch_attn', 'new_m_w_branch_ssm', 'new_m_b_gate', 'new_m_w_out', 'new_m_mix_norm_post', 'new_m_ffn_norm_pre', 'new_m_w_up', 'new_m_conv_w', 'new_m_conv_b', 'new_m_w_down', 'new_m_ffn_norm_post', 'new_v_mix_norm_pre', 'new_v_w_in', 'new_v_q_norm', 'new_v_w_uq', 'new_v_kv_norm', 'new_v_w_uk', 'new_v_w_uv', 'new_v_ssm_lambda_re', 'new_v_ssm_lambda_im', 'new_v_ssm_log_dt', 'new_v_ssm_b_re', 'new_v_ssm_b_im', 'new_v_ssm_c_re', 'new_v_ssm_c_im', 'new_v_ssm_d', 'new_v_w_glu', 'new_v_b_glu', 'new_v_w_branch_attn', 'new_v_w_branch_ssm', 'new_v_b_gate', 'new_v_w_out', 'new_v_mix_norm_post', 'new_v_ffn_norm_pre', 'new_v_w_up', 'new_v_conv_w', 'new_v_conv_b', 'new_v_w_down', 'new_v_ffn_norm_post']
TWIN_LEAF_KINDS = {'loss': 'loss', 'grad_x': 'grad_x', 'grad_mix_norm_pre': 'grad_w', 'grad_w_in': 'grad_w', 'grad_q_norm': 'grad_w', 'grad_w_uq': 'grad_w', 'grad_kv_norm': 'grad_w', 'grad_w_uk': 'grad_w', 'grad_w_uv': 'grad_w', 'grad_ssm_lambda_re': 'grad_w', 'grad_ssm_lambda_im': 'grad_w', 'grad_ssm_log_dt': 'grad_w', 'grad_ssm_b_re': 'grad_w', 'grad_ssm_b_im': 'grad_w', 'grad_ssm_c_re': 'grad_w', 'grad_ssm_c_im': 'grad_w', 'grad_ssm_d': 'grad_w', 'grad_w_glu': 'grad_w', 'grad_b_glu': 'grad_w', 'grad_w_branch_attn': 'grad_w', 'grad_w_branch_ssm': 'grad_w', 'grad_b_gate': 'grad_w', 'grad_w_out': 'grad_w', 'grad_mix_norm_post': 'grad_w', 'grad_ffn_norm_pre': 'grad_w', 'grad_w_up': 'grad_w', 'grad_conv_w': 'grad_w', 'grad_conv_b': 'grad_w', 'grad_w_down': 'grad_w', 'grad_ffn_norm_post': 'grad_w', 'delta_mix_norm_pre': 'delta_w', 'delta_w_in': 'delta_w', 'delta_q_norm': 'delta_w', 'delta_w_uq': 'delta_w', 'delta_kv_norm': 'delta_w', 'delta_w_uk': 'delta_w', 'delta_w_uv': 'delta_w', 'delta_ssm_lambda_re': 'delta_w', 'delta_ssm_lambda_im': 'delta_w', 'delta_ssm_log_dt': 'delta_w', 'delta_ssm_b_re': 'delta_w', 'delta_ssm_b_im': 'delta_w', 'delta_ssm_c_re': 'delta_w', 'delta_ssm_c_im': 'delta_w', 'delta_ssm_d': 'delta_w', 'delta_w_glu': 'delta_w', 'delta_b_glu': 'delta_w', 'delta_w_branch_attn': 'delta_w', 'delta_w_branch_ssm': 'delta_w', 'delta_b_gate': 'delta_w', 'delta_w_out': 'delta_w', 'delta_mix_norm_post': 'delta_w', 'delta_ffn_norm_pre': 'delta_w', 'delta_w_up': 'delta_w', 'delta_conv_w': 'delta_w', 'delta_conv_b': 'delta_w', 'delta_w_down': 'delta_w', 'delta_ffn_norm_post': 'delta_w', 'new_m_mix_norm_pre': 'new_m', 'new_m_w_in': 'new_m', 'new_m_q_norm': 'new_m', 'new_m_w_uq': 'new_m', 'new_m_kv_norm': 'new_m', 'new_m_w_uk': 'new_m', 'new_m_w_uv': 'new_m', 'new_m_ssm_lambda_re': 'new_m', 'new_m_ssm_lambda_im': 'new_m', 'new_m_ssm_log_dt': 'new_m', 'new_m_ssm_b_re': 'new_m', 'new_m_ssm_b_im': 'new_m', 'new_m_ssm_c_re': 'new_m', 'new_m_ssm_c_im': 'new_m', 'new_m_ssm_d': 'new_m', 'new_m_w_glu': 'new_m', 'new_m_b_glu': 'new_m', 'new_m_w_branch_attn': 'new_m', 'new_m_w_branch_ssm': 'new_m', 'new_m_b_gate': 'new_m', 'new_m_w_out': 'new_m', 'new_m_mix_norm_post': 'new_m', 'new_m_ffn_norm_pre': 'new_m', 'new_m_w_up': 'new_m', 'new_m_conv_w': 'new_m', 'new_m_conv_b': 'new_m', 'new_m_w_down': 'new_m', 'new_m_ffn_norm_post': 'new_m', 'new_v_mix_norm_pre': 'new_v', 'new_v_w_in': 'new_v', 'new_v_q_norm': 'new_v', 'new_v_w_uq': 'new_v', 'new_v_kv_norm': 'new_v', 'new_v_w_uk': 'new_v', 'new_v_w_uv': 'new_v', 'new_v_ssm_lambda_re': 'new_v', 'new_v_ssm_lambda_im': 'new_v', 'new_v_ssm_log_dt': 'new_v', 'new_v_ssm_b_re': 'new_v', 'new_v_ssm_b_im': 'new_v', 'new_v_ssm_c_re': 'new_v', 'new_v_ssm_c_im': 'new_v', 'new_v_ssm_d': 'new_v', 'new_v_w_glu': 'new_v', 'new_v_b_glu': 'new_v', 'new_v_w_branch_attn': 'new_v', 'new_v_w_branch_ssm': 'new_v', 'new_v_b_gate': 'new_v', 'new_v_w_out': 'new_v', 'new_v_mix_norm_post': 'new_v', 'new_v_ffn_norm_pre': 'new_v', 'new_v_w_up': 'new_v', 'new_v_conv_w': 'new_v', 'new_v_conv_b': 'new_v', 'new_v_w_down': 'new_v', 'new_v_ffn_norm_post': 'new_v'}


def _forward(args):
    return _fwd_reference(*[args[k] for k in FWD_PARAMS])


def _output_shape():
    out = _jax.eval_shape(lambda: _forward(_fwd_setup_inputs(0)))
    return out.shape, out.dtype

N_MICROBATCH = 1
ADAM_LR = 0.001
ADAM_B1 = 0.9
ADAM_B2 = 0.999
ADAM_EPS = 1e-08
ADAM_WD = 0.01
ADAM_STEP = 10
PER_EXAMPLE_BATCH_AXIS = {'x': 0, 'positions': 0, 'loss_target': 0}
SHARED_INPUTS = []
_WEIGHT_DTYPES = {'mix_norm_pre': _jnp.float32, 'w_in': _jnp.float32, 'q_norm': _jnp.float32, 'w_uq': _jnp.float32, 'kv_norm': _jnp.float32, 'w_uk': _jnp.float32, 'w_uv': _jnp.float32, 'ssm_lambda_re': _jnp.float32, 'ssm_lambda_im': _jnp.float32, 'ssm_log_dt': _jnp.float32, 'ssm_b_re': _jnp.float32, 'ssm_b_im': _jnp.float32, 'ssm_c_re': _jnp.float32, 'ssm_c_im': _jnp.float32, 'ssm_d': _jnp.float32, 'w_glu': _jnp.float32, 'b_glu': _jnp.float32, 'w_branch_attn': _jnp.float32, 'w_branch_ssm': _jnp.float32, 'b_gate': _jnp.float32, 'w_out': _jnp.float32, 'mix_norm_post': _jnp.float32, 'ffn_norm_pre': _jnp.float32, 'w_up': _jnp.float32, 'conv_w': _jnp.float32, 'conv_b': _jnp.float32, 'w_down': _jnp.float32, 'ffn_norm_post': _jnp.float32}
MOMENT_SCALE = {'mix_norm_pre': 6.518005e-01, 'w_in': 3.630278e-01, 'q_norm': 3.362899e-01, 'w_uq': 2.253212e-01, 'kv_norm': 6.404072e-01, 'w_uk': 2.307212e-01, 'w_uv': 3.568989e-01, 'ssm_lambda_re': 4.945748e-02, 'ssm_lambda_im': 3.502747e-02, 'ssm_log_dt': 4.944086e+01, 'ssm_b_re': 2.489363e-02, 'ssm_b_im': 2.364110e-02, 'ssm_c_re': 5.002212e-02, 'ssm_c_im': 5.170638e-02, 'ssm_d': 3.836940e+00, 'w_glu': 4.486723e-01, 'b_glu': 1.553602e+00, 'w_branch_attn': 2.534563e-01, 'w_branch_ssm': 2.708406e+00, 'b_gate': 7.379723e-01, 'w_out': 2.379665e+00, 'mix_norm_post': 3.290214e+01, 'ffn_norm_pre': 1.886753e+00, 'w_up': 7.156300e-01, 'conv_w': 9.695747e-01, 'conv_b': 3.055728e+00, 'w_down': 1.690135e+00, 'ffn_norm_post': 3.202713e+01}


def _to_microbatches(a, axis):
    t = _jnp.moveaxis(a, axis, 0)
    t = t.reshape((N_MICROBATCH, t.shape[0] // N_MICROBATCH) + t.shape[1:])
    return _jnp.moveaxis(t, 1, axis + 1)


def setup_inputs(seed: int = 0) -> dict:
    inp = _fwd_setup_inputs(seed)
    key = _jax.random.fold_in(_jax.random.key(seed), 7919)
    shape, _ = _output_shape()
    out = dict(inp)
    out["loss_target"] = _jax.random.normal(_jax.random.fold_in(key, 0), shape, _jnp.float32)
    for i, name in enumerate(TWIN_WEIGHTS):
        w = inp[name].astype(_jnp.float32)
        if MOMENT_SCALE is None:
            s = _jnp.sqrt(_jnp.mean(_jnp.square(w)) + 1e-30)
        else:
            s = MOMENT_SCALE[name]
        km, kv = _jax.random.split(_jax.random.fold_in(key, i + 1))
        out[name] = w
        out["m_" + name] = s * _jax.random.normal(km, w.shape, _jnp.float32)
        out["v_" + name] = (s * s) * _jax.random.uniform(kv, w.shape, _jnp.float32, 0.5, 1.5)
    if N_MICROBATCH > 1:
        for name, axis in PER_EXAMPLE_BATCH_AXIS.items():
            out[name] = _to_microbatches(out[name], axis)
    return {'x': out['x'], 'positions': out['positions'], 'mix_norm_pre': out['mix_norm_pre'], 'w_in': out['w_in'], 'q_norm': out['q_norm'], 'w_uq': out['w_uq'], 'kv_norm': out['kv_norm'], 'w_uk': out['w_uk'], 'w_uv': out['w_uv'], 'ssm_lambda_re': out['ssm_lambda_re'], 'ssm_lambda_im': out['ssm_lambda_im'], 'ssm_log_dt': out['ssm_log_dt'], 'ssm_b_re': out['ssm_b_re'], 'ssm_b_im': out['ssm_b_im'], 'ssm_c_re': out['ssm_c_re'], 'ssm_c_im': out['ssm_c_im'], 'ssm_d': out['ssm_d'], 'w_glu': out['w_glu'], 'b_glu': out['b_glu'], 'w_branch_attn': out['w_branch_attn'], 'w_branch_ssm': out['w_branch_ssm'], 'b_gate': out['b_gate'], 'w_out': out['w_out'], 'mix_norm_post': out['mix_norm_post'], 'ffn_norm_pre': out['ffn_norm_pre'], 'w_up': out['w_up'], 'conv_w': out['conv_w'], 'conv_b': out['conv_b'], 'w_down': out['w_down'], 'ffn_norm_post': out['ffn_norm_post'], 'loss_target': out['loss_target'], 'm_mix_norm_pre': out['m_mix_norm_pre'], 'm_w_in': out['m_w_in'], 'm_q_norm': out['m_q_norm'], 'm_w_uq': out['m_w_uq'], 'm_kv_norm': out['m_kv_norm'], 'm_w_uk': out['m_w_uk'], 'm_w_uv': out['m_w_uv'], 'm_ssm_lambda_re': out['m_ssm_lambda_re'], 'm_ssm_lambda_im': out['m_ssm_lambda_im'], 'm_ssm_log_dt': out['m_ssm_log_dt'], 'm_ssm_b_re': out['m_ssm_b_re'], 'm_ssm_b_im': out['m_ssm_b_im'], 'm_ssm_c_re': out['m_ssm_c_re'], 'm_ssm_c_im': out['m_ssm_c_im'], 'm_ssm_d': out['m_ssm_d'], 'm_w_glu': out['m_w_glu'], 'm_b_glu': out['m_b_glu'], 'm_w_branch_attn': out['m_w_branch_attn'], 'm_w_branch_ssm': out['m_w_branch_ssm'], 'm_b_gate': out['m_b_gate'], 'm_w_out': out['m_w_out'], 'm_mix_norm_post': out['m_mix_norm_post'], 'm_ffn_norm_pre': out['m_ffn_norm_pre'], 'm_w_up': out['m_w_up'], 'm_conv_w': out['m_conv_w'], 'm_conv_b': out['m_conv_b'], 'm_w_down': out['m_w_down'], 'm_ffn_norm_post': out['m_ffn_norm_post'], 'v_mix_norm_pre': out['v_mix_norm_pre'], 'v_w_in': out['v_w_in'], 'v_q_norm': out['v_q_norm'], 'v_w_uq': out['v_w_uq'], 'v_kv_norm': out['v_kv_norm'], 'v_w_uk': out['v_w_uk'], 'v_w_uv': out['v_w_uv'], 'v_ssm_lambda_re': out['v_ssm_lambda_re'], 'v_ssm_lambda_im': out['v_ssm_lambda_im'], 'v_ssm_log_dt': out['v_ssm_log_dt'], 'v_ssm_b_re': out['v_ssm_b_re'], 'v_ssm_b_im': out['v_ssm_b_im'], 'v_ssm_c_re': out['v_ssm_c_re'], 'v_ssm_c_im': out['v_ssm_c_im'], 'v_ssm_d': out['v_ssm_d'], 'v_w_glu': out['v_w_glu'], 'v_b_glu': out['v_b_glu'], 'v_w_branch_attn': out['v_w_branch_attn'], 'v_w_branch_ssm': out['v_w_branch_ssm'], 'v_b_gate': out['v_b_gate'], 'v_w_out': out['v_w_out'], 'v_mix_norm_post': out['v_mix_norm_post'], 'v_ffn_norm_pre': out['v_ffn_norm_pre'], 'v_w_up': out['v_w_up'], 'v_conv_w': out['v_conv_w'], 'v_conv_b': out['v_conv_b'], 'v_w_down': out['v_w_down'], 'v_ffn_norm_post': out['v_ffn_norm_post']}


def _loss(weights, diff, rest, loss_target):
    with _jax.named_scope("forward"):
        args = {**rest, TWIN_DIFF_INPUT: diff, **{k: w.astype(_WEIGHT_DTYPES[k]) for k, w in weights.items()}}
        y = _forward(args)
    with _jax.named_scope("loss_head"):
        err = _jnp.square(y.astype(_jnp.float32) - loss_target)
        return 0.5 * _jnp.sum(_jnp.mean(err, axis=-1)) if err.ndim else 0.5 * err


def _adamw(w, g, m, v):
    m = ADAM_B1 * m + (1.0 - ADAM_B1) * g
    v = ADAM_B2 * v + (1.0 - ADAM_B2) * _jnp.square(g)
    m_hat = m / (1.0 - ADAM_B1 ** ADAM_STEP)
    v_hat = v / (1.0 - ADAM_B2 ** ADAM_STEP)
    delta = -ADAM_LR * (m_hat / (_jnp.sqrt(v_hat) + ADAM_EPS) + ADAM_WD * w)
    return delta, m, v


def reference(x, positions, mix_norm_pre, w_in, q_norm, w_uq, kv_norm, w_uk, w_uv, ssm_lambda_re, ssm_lambda_im, ssm_log_dt, ssm_b_re, ssm_b_im, ssm_c_re, ssm_c_im, ssm_d, w_glu, b_glu, w_branch_attn, w_branch_ssm, b_gate, w_out, mix_norm_post, ffn_norm_pre, w_up, conv_w, conv_b, w_down, ffn_norm_post, loss_target, m_mix_norm_pre, m_w_in, m_q_norm, m_w_uq, m_kv_norm, m_w_uk, m_w_uv, m_ssm_lambda_re, m_ssm_lambda_im, m_ssm_log_dt, m_ssm_b_re, m_ssm_b_im, m_ssm_c_re, m_ssm_c_im, m_ssm_d, m_w_glu, m_b_glu, m_w_branch_attn, m_w_branch_ssm, m_b_gate, m_w_out, m_mix_norm_post, m_ffn_norm_pre, m_w_up, m_conv_w, m_conv_b, m_w_down, m_ffn_norm_post, v_mix_norm_pre, v_w_in, v_q_norm, v_w_uq, v_kv_norm, v_w_uk, v_w_uv, v_ssm_lambda_re, v_ssm_lambda_im, v_ssm_log_dt, v_ssm_b_re, v_ssm_b_im, v_ssm_c_re, v_ssm_c_im, v_ssm_d, v_w_glu, v_b_glu, v_w_branch_attn, v_w_branch_ssm, v_b_gate, v_w_out, v_mix_norm_post, v_ffn_norm_pre, v_w_up, v_conv_w, v_conv_b, v_w_down, v_ffn_norm_post):
    given = dict(x=x, positions=positions, mix_norm_pre=mix_norm_pre, w_in=w_in, q_norm=q_norm, w_uq=w_uq, kv_norm=kv_norm, w_uk=w_uk, w_uv=w_uv, ssm_lambda_re=ssm_lambda_re, ssm_lambda_im=ssm_lambda_im, ssm_log_dt=ssm_log_dt, ssm_b_re=ssm_b_re, ssm_b_im=ssm_b_im, ssm_c_re=ssm_c_re, ssm_c_im=ssm_c_im, ssm_d=ssm_d, w_glu=w_glu, b_glu=b_glu, w_branch_attn=w_branch_attn, w_branch_ssm=w_branch_ssm, b_gate=b_gate, w_out=w_out, mix_norm_post=mix_norm_post, ffn_norm_pre=ffn_norm_pre, w_up=w_up, conv_w=conv_w, conv_b=conv_b, w_down=w_down, ffn_norm_post=ffn_norm_post, loss_target=loss_target, m_mix_norm_pre=m_mix_norm_pre, m_w_in=m_w_in, m_q_norm=m_q_norm, m_w_uq=m_w_uq, m_kv_norm=m_kv_norm, m_w_uk=m_w_uk, m_w_uv=m_w_uv, m_ssm_lambda_re=m_ssm_lambda_re, m_ssm_lambda_im=m_ssm_lambda_im, m_ssm_log_dt=m_ssm_log_dt, m_ssm_b_re=m_ssm_b_re, m_ssm_b_im=m_ssm_b_im, m_ssm_c_re=m_ssm_c_re, m_ssm_c_im=m_ssm_c_im, m_ssm_d=m_ssm_d, m_w_glu=m_w_glu, m_b_glu=m_b_glu, m_w_branch_attn=m_w_branch_attn, m_w_branch_ssm=m_w_branch_ssm, m_b_gate=m_b_gate, m_w_out=m_w_out, m_mix_norm_post=m_mix_norm_post, m_ffn_norm_pre=m_ffn_norm_pre, m_w_up=m_w_up, m_conv_w=m_conv_w, m_conv_b=m_conv_b, m_w_down=m_w_down, m_ffn_norm_post=m_ffn_norm_post, v_mix_norm_pre=v_mix_norm_pre, v_w_in=v_w_in, v_q_norm=v_q_norm, v_w_uq=v_w_uq, v_kv_norm=v_kv_norm, v_w_uk=v_w_uk, v_w_uv=v_w_uv, v_ssm_lambda_re=v_ssm_lambda_re, v_ssm_lambda_im=v_ssm_lambda_im, v_ssm_log_dt=v_ssm_log_dt, v_ssm_b_re=v_ssm_b_re, v_ssm_b_im=v_ssm_b_im, v_ssm_c_re=v_ssm_c_re, v_ssm_c_im=v_ssm_c_im, v_ssm_d=v_ssm_d, v_w_glu=v_w_glu, v_b_glu=v_b_glu, v_w_branch_attn=v_w_branch_attn, v_w_branch_ssm=v_w_branch_ssm, v_b_gate=v_b_gate, v_w_out=v_w_out, v_mix_norm_post=v_mix_norm_post, v_ffn_norm_pre=v_ffn_norm_pre, v_w_up=v_w_up, v_conv_w=v_conv_w, v_conv_b=v_conv_b, v_w_down=v_w_down, v_ffn_norm_post=v_ffn_norm_post)
    weights = {n: given[n] for n in TWIN_WEIGHTS}
    shared = {n: given[n] for n in SHARED_INPUTS}
    per_example = {n: given[n] for n in ['x', 'positions']}
    grad_fn = _jax.value_and_grad(_loss, argnums=(0, 1))

    def one_microbatch(ex, loss_target):
        ex = dict(ex)
        diff = ex.pop(TWIN_DIFF_INPUT)
        return grad_fn(weights, diff, {**shared, **ex}, loss_target)

    if N_MICROBATCH == 1:
        loss, (grad_w, grad_x) = one_microbatch(per_example, given["loss_target"])
    else:
        def body(carry, xs):
            loss_sum, grad_sum = carry
            l_k, (gw_k, gx_k) = one_microbatch(xs[0], xs[1])
            with _jax.named_scope("update"):
                return (loss_sum + l_k, _jax.tree.map(_jnp.add, grad_sum, gw_k)), gx_k

        init = (_jnp.zeros((), _jnp.float32), _jax.tree.map(_jnp.zeros_like, weights))
        (loss, grad_w), grad_x = _jax.lax.scan(body, init, (per_example, given["loss_target"]))
    with _jax.named_scope("update"):
        delta_w, new_m, new_v = {}, {}, {}
        for n in TWIN_WEIGHTS:
            delta_w[n], new_m[n], new_v[n] = _adamw(weights[n], grad_w[n], given["m_" + n], given["v_" + n])
    return (loss, grad_x, *[grad_w[n] for n in TWIN_WEIGHTS], *[delta_w[n] for n in TWIN_WEIGHTS],
            *[new_m[n] for n in TWIN_WEIGHTS], *[new_v[n] for n in TWIN_WEIGHTS])
```

```python
import math

import jax
import jax.numpy as jnp
from jax import lax
from jax.experimental import pallas as pl
from jax.experimental.pallas import tpu as pltpu

F32 = jnp.float32
BF16 = jnp.bfloat16
MESH_ID = pl.DeviceIdType.MESH

N_DEV = 8
LANES = 128
D_MODEL = 1024
N_HEADS = 8
QK_NOPE = 64
QK_ROPE = 32
QK_HEAD = QK_NOPE + QK_ROPE
V_HEAD = 64
Q_RANK = 384
KV_RANK = 256
ROPE_THETA = 10000.0
SSM_WIDTH = 512
SSM_GROUP = 16
SSM_GROUPS = 32
SSM_STATE = 64
SSM_NSTATE = SSM_GROUPS * SSM_STATE
SSM_CHUNKS = 4
D_FF = 2816
EPS = 1e-6
ADAM_LR, ADAM_B1, ADAM_B2, ADAM_EPS, ADAM_WD, ADAM_STEP = 0.001, 0.9, 0.999, 1e-08, 0.01, 10

P_CQ, P_CKV, P_KR, P_U, P_GATE = 0, 384, 640, 768, 1280
P_IN = P_GATE + 2 * D_MODEL
HEAD_PAD = N_HEADS * LANES

PACK_ROWS = 1024
VMEM_BIG = 52 * 1024 * 1024

_GELU_C0 = math.sqrt(2.0 / math.pi)
_GELU_C1 = 0.044715
NEG = -1e30


def _fit(n, pref, mult=LANES):
    if n <= pref:
        return n
    t = (pref // mult) * mult
    while t > 0 and n % t:
        t -= mult
    assert t > 0, (n, pref, mult)
    return t


def _gelu(x):
    return 0.5 * x * (1.0 + jnp.tanh(_GELU_C0 * (x + _GELU_C1 * x * x * x)))


def _gelu_grad(x):
    x2 = x * x
    t = jnp.tanh(_GELU_C0 * x * (1.0 + _GELU_C1 * x2))
    return 0.5 * (1.0 + t) + 0.5 * x * (1.0 - t * t) * _GELU_C0 * (1.0 + 3.0 * _GELU_C1 * x2)


def _sigmoid(x):
    return 1.0 / (1.0 + jnp.exp(-x))


def _dot(a, b, dims):
    return lax.dot_general(a, b, (dims, ((), ())), preferred_element_type=F32)


NN = ((1,), (0,))
NT = ((1,), (1,))
TN = ((0,), (0,))


def _params(*sem, vmem=None):
    return pltpu.CompilerParams(dimension_semantics=tuple(sem), vmem_limit_bytes=vmem)


def _mm(a, b, name, ta=False, tb=False, out_dtype=F32, tm=512, tn=512, tk=1024):
    if ta:
        K, M = a.shape
    else:
        M, K = a.shape
    if tb:
        N, K2 = b.shape
    else:
        K2, N = b.shape
    assert K == K2, (a.shape, b.shape, ta, tb)
    tm, tn, tk = _fit(M, tm), _fit(N, tn), _fit(K, tk)
    nk = K // tk
    dims = ((0,) if ta else (1,), (1,) if tb else (0,))

    def body(a_ref, b_ref, o_ref, *scratch):
        part = _dot(a_ref[...].astype(BF16), b_ref[...].astype(BF16), dims)
        if nk == 1:
            o_ref[...] = part.astype(out_dtype)
        else:
            acc_ref, = scratch
            k = pl.program_id(2)

            @pl.when(k == 0)
            def _():
                acc_ref[...] = part

            @pl.when(k > 0)
            def _():
                acc_ref[...] += part

            @pl.when(k == nk - 1)
            def _():
                o_ref[...] = acc_ref[...].astype(out_dtype)

    a_spec = pl.BlockSpec((tk, tm), lambda i, j, k: (k, i)) if ta else pl.BlockSpec((tm, tk), lambda i, j, k: (i, k))
    b_spec = pl.BlockSpec((tn, tk), lambda i, j, k: (j, k)) if tb else pl.BlockSpec((tk, tn), lambda i, j, k: (k, j))
    return pl.pallas_call(
        body, name=name, grid=(M // tm, N // tn, nk),
        in_specs=[a_spec, b_spec],
        out_specs=pl.BlockSpec((tm, tn), lambda i, j, k: (i, j)),
        out_shape=jax.ShapeDtypeStruct((M, N), out_dtype),
        scratch_shapes=[] if nk == 1 else [pltpu.VMEM((tm, tn), F32)],
        compiler_params=_params("parallel", "parallel", "arbitrary"),
    )(a, b)


def _row(tl, n, col=0):
    return pl.BlockSpec((tl, n), lambda i: (i, col))


def _full(shape):
    return pl.BlockSpec(shape, lambda i: (0,) * len(shape))


def _rms(x, g):
    r = lax.rsqrt(jnp.mean(x * x, axis=-1, keepdims=True) + EPS)
    return x * r * g


def _rms_bwd(x, g, dy):
    n = x.shape[-1]
    r = lax.rsqrt(jnp.mean(x * x, axis=-1, keepdims=True) + EPS)
    gy = dy * g
    dx = r * gy - x * (r * r * r * (1.0 / n)) * jnp.sum(x * gy, axis=-1, keepdims=True)
    return dx, jnp.sum(dy * x * r, axis=0, keepdims=True)


def _acc(ref, first, val):
    @pl.when(first)
    def _():
        ref[...] = val

    @pl.when(jnp.logical_not(first))
    def _():
        ref[...] += val


def _rms_fwd_call(x, g, name):
    L, n = x.shape
    tl = _fit(L, 512)

    def body(x_ref, g_ref, o_ref):
        o_ref[...] = _rms(x_ref[...], g_ref[...]).astype(BF16)

    return pl.pallas_call(
        body, name=name, grid=(L // tl,), in_specs=[_row(tl, n), _full((1, n))], out_specs=_row(tl, n),
        out_shape=jax.ShapeDtypeStruct((L, n), BF16), compiler_params=_params("parallel"))(x, g)


def _mla_norms_call(proj, q_norm, kv_norm):
    L = proj.shape[0]
    tl = _fit(L, 512)

    def body(p_ref, gq_ref, gk_ref, qn_ref, kn_ref):
        p = p_ref[...]
        qn_ref[...] = _rms(p[:, P_CQ:P_CKV], gq_ref[...]).astype(BF16)
        kn_ref[...] = _rms(p[:, P_CKV:P_KR], gk_ref[...]).astype(BF16)

    return pl.pallas_call(
        body, name="mla_norms", grid=(L // tl,),
        in_specs=[_row(tl, P_KR), _full((1, Q_RANK)), _full((1, KV_RANK))],
        out_specs=[_row(tl, Q_RANK), _row(tl, KV_RANK)],
        out_shape=[jax.ShapeDtypeStruct((L, Q_RANK), BF16), jax.ShapeDtypeStruct((L, KV_RANK), BF16)],
        compiler_params=_params("parallel"))(proj, q_norm, kv_norm)


def _rope_lanes(shape):
    lane = lax.broadcasted_iota(jnp.int32, shape, 1)
    return lane, jnp.logical_and(lane >= QK_NOPE, lane < QK_HEAD)


def _rope_apply(x, cosf, sinf, lane):
    rot = jnp.where(lane < QK_NOPE + QK_ROPE // 2, -pltpu.roll(x, LANES - QK_ROPE // 2, 1), pltpu.roll(x, QK_ROPE // 2, 1))
    return x * cosf + rot * sinf


def _rope_apply_t(dy, cosf, sinf, lane, is_rope):
    g = dy * sinf
    rot_t = jnp.where(lane < QK_NOPE + QK_ROPE // 2, pltpu.roll(g, LANES - QK_ROPE // 2, 1), -pltpu.roll(g, QK_ROPE // 2, 1))
    return dy * cosf + jnp.where(is_rope, rot_t, 0.0)


def _mla_prep_call(q_pad, kv_pad, proj, pos_col, inv_freq):
    L = q_pad.shape[0]
    tl = _fit(L, 512)

    def body(q_ref, kv_ref, kr_ref, pos_ref, f_ref, qo_ref, kvo_ref, cos_ref, sin_ref):
        lane, is_rope = _rope_lanes((tl, LANES))
        ang = pos_ref[...] * f_ref[...]
        cosf = jnp.where(is_rope, jnp.cos(ang), jnp.where(lane < QK_NOPE, 1.0, 0.0))
        sinf = jnp.where(is_rope, jnp.sin(ang), 0.0)
        cos_ref[...] = cosf
        sin_ref[...] = sinf
        kr = _rope_apply(kr_ref[...], cosf, sinf, lane)
        for h in range(N_HEADS):
            qo_ref[:, h * LANES:(h + 1) * LANES] = _rope_apply(q_ref[:, h * LANES:(h + 1) * LANES], cosf, sinf, lane).astype(BF16)
            kvo_ref[:, 2 * h * LANES:(2 * h + 1) * LANES] = (kv_ref[:, 2 * h * LANES:(2 * h + 1) * LANES] + kr).astype(BF16)
            kvo_ref[:, (2 * h + 1) * LANES:(2 * h + 2) * LANES] = kv_ref[:, (2 * h + 1) * LANES:(2 * h + 2) * LANES].astype(BF16)

    return pl.pallas_call(
        body, name="mla_prep", grid=(L // tl,),
        in_specs=[_row(tl, HEAD_PAD), _row(tl, 2 * HEAD_PAD), _row(tl, LANES, P_KR // LANES), _row(tl, 1), _full((1, LANES))],
        out_specs=[_row(tl, HEAD_PAD), _row(tl, 2 * HEAD_PAD), _row(tl, LANES), _row(tl, LANES)],
        out_shape=[jax.ShapeDtypeStruct((L, HEAD_PAD), BF16), jax.ShapeDtypeStruct((L, 2 * HEAD_PAD), BF16),
                   jax.ShapeDtypeStruct((L, LANES), F32), jax.ShapeDtypeStruct((L, LANES), F32)],
        compiler_params=_params("parallel"))(q_pad, kv_pad, proj, pos_col, inv_freq)


def _mla_prep_bwd_call(dq, dkv, cosf, sinf):
    L = dq.shape[0]
    tl = _fit(L, 512)

    def body(dq_ref, dkv_ref, cos_ref, sin_ref, dqo_ref, dkvo_ref, dkr_ref):
        lane, is_rope = _rope_lanes((tl, LANES))
        cosf, sinf = cos_ref[...], sin_ref[...]
        dk_sum = jnp.zeros((tl, LANES), F32)
        for h in range(N_HEADS):
            dqo_ref[:, h * LANES:(h + 1) * LANES] = _rope_apply_t(dq_ref[:, h * LANES:(h + 1) * LANES], cosf, sinf, lane, is_rope).astype(BF16)
            dk_sum = dk_sum + dkv_ref[:, 2 * h * LANES:(2 * h + 1) * LANES]
        dkvo_ref[...] = dkv_ref[...].astype(BF16)
        dkr_ref[...] = _rope_apply_t(dk_sum, cosf, sinf, lane, is_rope).astype(BF16)

    return pl.pallas_call(
        body, name="mla_prep_bwd", grid=(L // tl,),
        in_specs=[_row(tl, HEAD_PAD), _row(tl, 2 * HEAD_PAD), _row(tl, LANES), _row(tl, LANES)],
        out_specs=[_row(tl, HEAD_PAD), _row(tl, 2 * HEAD_PAD), _row(tl, LANES)],
        out_shape=[jax.ShapeDtypeStruct((L, HEAD_PAD), BF16), jax.ShapeDtypeStruct((L, 2 * HEAD_PAD), BF16),
                   jax.ShapeDtypeStruct((L, LANES), BF16)],
        compiler_params=_params("parallel"))(dq, dkv, cosf, sinf)


def _mla_norms_bwd_call(proj, dqn, dkn, q_norm, kv_norm):
    L = proj.shape[0]
    tl = _fit(L, 512)

    def body(p_ref, dqn_ref, dkn_ref, gq_ref, gk_ref, d_ref, dgq_ref, dgk_ref):
        first = pl.program_id(0) == 0
        p = p_ref[...]
        dq, dgq = _rms_bwd(p[:, P_CQ:P_CKV], gq_ref[...], dqn_ref[...])
        dk, dgk = _rms_bwd(p[:, P_CKV:P_KR], gk_ref[...], dkn_ref[...])
        d_ref[:, P_CQ:P_CKV] = dq.astype(BF16)
        d_ref[:, P_CKV:P_KR] = dk.astype(BF16)
        _acc(dgq_ref, first, dgq)
        _acc(dgk_ref, first, dgk)

    return pl.pallas_call(
        body, name="mla_norms_bwd", grid=(L // tl,),
        in_specs=[_row(tl, P_KR), _row(tl, Q_RANK), _row(tl, KV_RANK), _full((1, Q_RANK)), _full((1, KV_RANK))],
        out_specs=[_row(tl, P_KR), _full((1, Q_RANK)), _full((1, KV_RANK))],
        out_shape=[jax.ShapeDtypeStruct((L, P_KR), BF16), jax.ShapeDtypeStruct((1, Q_RANK), F32),
                   jax.ShapeDtypeStruct((1, KV_RANK), F32)],
        compiler_params=_params("arbitrary"))(proj, dqn, dkn, q_norm, kv_norm)


GATE_TILE = 256


def _merge_call(proj, b_gate, pa, ps):
    L = proj.shape[0]
    tl = _fit(L, 512)
    nc = D_MODEL // GATE_TILE
    g0, g1 = P_GATE // GATE_TILE, (P_GATE + D_MODEL) // GATE_TILE

    def body(l0_ref, l1_ref, b0_ref, b1_ref, pa_ref, ps_ref, o_ref):
        s0 = _sigmoid(l0_ref[...] + b0_ref[...])
        s1 = _sigmoid(l1_ref[...] + b1_ref[...])
        o_ref[...] = (s0 * pa_ref[...] + s1 * ps_ref[...]).astype(BF16)

    blk = lambda off: pl.BlockSpec((tl, GATE_TILE), lambda i, j: (i, off + j))
    bias = lambda off: pl.BlockSpec((1, GATE_TILE), lambda i, j: (0, off + j))
    return pl.pallas_call(
        body, name="merge", grid=(L // tl, nc),
        in_specs=[blk(g0), blk(g1), bias(0), bias(nc), blk(0), blk(0)],
        out_specs=blk(0), out_shape=jax.ShapeDtypeStruct((L, D_MODEL), BF16),
        compiler_params=_params("parallel", "parallel"))(proj, proj, b_gate, b_gate, pa, ps)


def _merge_bwd_call(dm, proj, b_gate, pa, ps):
    L = proj.shape[0]
    tl = _fit(L, 512)
    nc = D_MODEL // GATE_TILE
    g0, g1 = P_GATE // GATE_TILE, (P_GATE + D_MODEL) // GATE_TILE

    def body(dm_ref, l0_ref, l1_ref, b0_ref, b1_ref, pa_ref, ps_ref, dpa_ref, dps_ref, dl0_ref, dl1_ref, db0_ref, db1_ref):
        first = pl.program_id(1) == 0
        dm_ = dm_ref[...]
        s0 = _sigmoid(l0_ref[...] + b0_ref[...])
        s1 = _sigmoid(l1_ref[...] + b1_ref[...])
        dpa_ref[...] = (dm_ * s0).astype(BF16)
        dps_ref[...] = (dm_ * s1).astype(BF16)
        dl0 = dm_ * pa_ref[...] * s0 * (1.0 - s0)
        dl1 = dm_ * ps_ref[...] * s1 * (1.0 - s1)
        dl0_ref[...] = dl0.astype(BF16)
        dl1_ref[...] = dl1.astype(BF16)
        _acc(db0_ref, first, jnp.sum(dl0, axis=0, keepdims=True))
        _acc(db1_ref, first, jnp.sum(dl1, axis=0, keepdims=True))

    blk = lambda off: pl.BlockSpec((tl, GATE_TILE), lambda j, i: (i, off + j))
    bias = lambda off: pl.BlockSpec((1, GATE_TILE), lambda j, i: (0, off + j))
    act = jax.ShapeDtypeStruct((L, D_MODEL), BF16)
    vec = jax.ShapeDtypeStruct((1, D_MODEL), F32)
    return pl.pallas_call(
        body, name="merge_bwd", grid=(nc, L // tl),
        in_specs=[blk(0), blk(g0), blk(g1), bias(0), bias(nc), blk(0), blk(0)],
        out_specs=[blk(0), blk(0), blk(0), blk(0), bias(0), bias(0)],
        out_shape=[act, act, act, act, vec, vec],
        compiler_params=_params("parallel", "arbitrary"))(dm, proj, proj, b_gate, b_gate, pa, ps)


def _post_mix_call(o, x, g_post, g_fpre):
    L, n = x.shape
    tl = _fit(L, 512)

    def body(o_ref, x_ref, gp_ref, gf_ref, x2_ref, hn_ref):
        x2 = x_ref[...] + _rms(o_ref[...], gp_ref[...])
        x2_ref[...] = x2
        hn_ref[...] = _rms(x2, gf_ref[...]).astype(BF16)

    return pl.pallas_call(
        body, name="post_mix", grid=(L // tl,),
        in_specs=[_row(tl, n), _row(tl, n), _full((1, n)), _full((1, n))],
        out_specs=[_row(tl, n), _row(tl, n)],
        out_shape=[jax.ShapeDtypeStruct((L, n), F32), jax.ShapeDtypeStruct((L, n), BF16)],
        compiler_params=_params("parallel"))(o, x, g_post, g_fpre)


def _ffn_out_call(ff, x2, target, g_fpost):
    L, n = x2.shape
    tl = _fit(L, 512)

    def body(ff_ref, x2_ref, t_ref, g_ref, loss_ref, dy_ref, dff_ref, dg_ref):
        first = pl.program_id(0) == 0
        ff_ = ff_ref[...]
        err = x2_ref[...] + _rms(ff_, g_ref[...]) - t_ref[...]
        part = 0.5 * jnp.sum(jnp.sum(err * err, axis=-1, keepdims=True) * (1.0 / n), axis=0, keepdims=True)
        dy = err * (1.0 / n)
        dy_ref[...] = dy
        dff, dg = _rms_bwd(ff_, g_ref[...], dy)
        dff_ref[...] = dff.astype(BF16)
        _acc(loss_ref, first, jnp.broadcast_to(part, (1, LANES)))
        _acc(dg_ref, first, dg)

    return pl.pallas_call(
        body, name="ffn_out", grid=(L // tl,),
        in_specs=[_row(tl, n), _row(tl, n), _row(tl, n), _full((1, n))],
        out_specs=[_full((1, LANES)), _row(tl, n), _row(tl, n), _full((1, n))],
        out_shape=[jax.ShapeDtypeStruct((1, LANES), F32), jax.ShapeDtypeStruct((L, n), F32),
                   jax.ShapeDtypeStruct((L, n), BF16), jax.ShapeDtypeStruct((1, n), F32)],
        compiler_params=_params("arbitrary"))(ff, x2, target, g_fpost)


def _post_bwd_call(x2, dhn2, dy, o, g_fpre, g_post):
    L, n = x2.shape
    tl = _fit(L, 512)

    def body(x2_ref, dh_ref, dy_ref, o_ref, gf_ref, gp_ref, dx2_ref, do_ref, dgf_ref, dgp_ref):
        first = pl.program_id(0) == 0
        d1, dgf = _rms_bwd(x2_ref[...], gf_ref[...], dh_ref[...])
        dx2 = dy_ref[...] + d1
        dx2_ref[...] = dx2
        do, dgp = _rms_bwd(o_ref[...], gp_ref[...], dx2)
        do_ref[...] = do.astype(BF16)
        _acc(dgf_ref, first, dgf)
        _acc(dgp_ref, first, dgp)

    return pl.pallas_call(
        body, name="post_bwd", grid=(L // tl,),
        in_specs=[_row(tl, n), _row(tl, n), _row(tl, n), _row(tl, n), _full((1, n)), _full((1, n))],
        out_specs=[_row(tl, n), _row(tl, n), _full((1, n)), _full((1, n))],
        out_shape=[jax.ShapeDtypeStruct((L, n), F32), jax.ShapeDtypeStruct((L, n), BF16),
                   jax.ShapeDtypeStruct((1, n), F32), jax.ShapeDtypeStruct((1, n), F32)],
        compiler_params=_params("arbitrary"))(x2, dhn2, dy, o, g_fpre, g_post)


def _pre_bwd_call(x, dhn1, dx2, g_pre):
    L, n = x.shape
    tl = _fit(L, 512)

    def body(x_ref, dh_ref, dx2_ref, g_ref, dx_ref, dg_ref):
        first = pl.program_id(0) == 0
        d1, dg = _rms_bwd(x_ref[...], g_ref[...], dh_ref[...])
        dx_ref[...] = dx2_ref[...] + d1
        _acc(dg_ref, first, dg)

    return pl.pallas_call(
        body, name="pre_bwd", grid=(L // tl,),
        in_specs=[_row(tl, n), _row(tl, n), _row(tl, n), _full((1, n))],
        out_specs=[_row(tl, n), _full((1, n))],
        out_shape=[jax.ShapeDtypeStruct((L, n), F32), jax.ShapeDtypeStruct((1, n), F32)],
        compiler_params=_params("arbitrary"))(x, dhn1, dx2, g_pre)


CONV_TILE = 256
HALO = 16


def _shift_down(cur, halo_tail, by):
    rolled = pltpu.roll(cur, by, 0)
    r8 = lax.broadcasted_iota(jnp.int32, halo_tail.shape, 0)
    head = jnp.where(r8 < by, pltpu.roll(halo_tail, by, 0), rolled[0:8])
    return jnp.concatenate([head, rolled[8:]], axis=0)


def _shift_up(cur, halo_head, by):
    n = cur.shape[0]
    rolled = pltpu.roll(cur, n - by, 0)
    r8 = lax.broadcasted_iota(jnp.int32, halo_head.shape, 0)
    tail = jnp.where(r8 >= 8 - by, pltpu.roll(halo_head, 8 - by, 0), rolled[n - 8:])
    return jnp.concatenate([rolled[:n - 8], tail], axis=0)


def _conv_fwd_vals(cur, halo, w, b, not_first):
    tail = halo[HALO - 8:] * not_first
    s1 = _shift_down(cur, tail, 1)
    s2 = _shift_down(cur, tail, 2)
    return b + w[2:3] * cur + w[1:2] * s1 + w[0:1] * s2, s1, s2


def _conv_specs(L, tl, nc, rows_inner):
    nh = tl // HALO
    if rows_inner:
        ij = lambda f: (lambda j, i: f(i, j))
    else:
        ij = lambda f: f
    cur = lambda off: pl.BlockSpec((tl, CONV_TILE), ij(lambda i, j: (i, off + j)))
    prev = lambda off: pl.BlockSpec((HALO, CONV_TILE), ij(lambda i, j: (jnp.maximum(i * nh - 1, 0), off + j)))
    nxt = lambda off: pl.BlockSpec((HALO, CONV_TILE), ij(lambda i, j: (jnp.minimum((i + 1) * nh, L // HALO - 1), off + j)))
    par = lambda rows, off: pl.BlockSpec((rows, CONV_TILE), ij(lambda i, j: (0, off + j)))
    return cur, prev, nxt, par


def _conv_act_call(h, conv_w, conv_b):
    L = h.shape[0]
    tl = _fit(L, 512)
    nc = D_FF // CONV_TILE
    cur, prev, _, par = _conv_specs(L, tl, nc, False)

    def body(hg_ref, hv_ref, pg_ref, pv_ref, wg_ref, wv_ref, bg_ref, bv_ref, a_ref):
        not_first = (pl.program_id(0) > 0).astype(F32)
        gate, _, _ = _conv_fwd_vals(hg_ref[...], pg_ref[...], wg_ref[...], bg_ref[...], not_first)
        val, _, _ = _conv_fwd_vals(hv_ref[...], pv_ref[...], wv_ref[...], bv_ref[...], not_first)
        a_ref[...] = (_gelu(gate) * val).astype(BF16)

    return pl.pallas_call(
        body, name="conv_act", grid=(L // tl, nc),
        in_specs=[cur(0), cur(nc), prev(0), prev(nc), par(3, 0), par(3, nc), par(1, 0), par(1, nc)],
        out_specs=cur(0), out_shape=jax.ShapeDtypeStruct((L, D_FF), BF16),
        compiler_params=_params("parallel", "parallel"))(h, h, h, h, conv_w, conv_w, conv_b, conv_b)


def _conv_act_bwd_call(da, h, conv_w, conv_b):
    L = h.shape[0]
    tl = _fit(L, 512)
    nc = D_FF // CONV_TILE
    cur, prev, _, par = _conv_specs(L, tl, nc, True)

    def body(da_ref, hg_ref, hv_ref, pg_ref, pv_ref, wg_ref, wv_ref, bg_ref, bv_ref,
             dg_ref, dv_ref, dwg_ref, dwv_ref, dbg_ref, dbv_ref):
        first = pl.program_id(1) == 0
        not_first = (pl.program_id(1) > 0).astype(F32)
        hg, hv = hg_ref[...], hv_ref[...]
        gate, g1, g2 = _conv_fwd_vals(hg, pg_ref[...], wg_ref[...], bg_ref[...], not_first)
        val, v1, v2 = _conv_fwd_vals(hv, pv_ref[...], wv_ref[...], bv_ref[...], not_first)
        da_ = da_ref[...]
        dgate = da_ * val * _gelu_grad(gate)
        dval = da_ * _gelu(gate)
        dg_ref[...] = dgate.astype(BF16)
        dv_ref[...] = dval.astype(BF16)
        col = lambda t: jnp.sum(t, axis=0, keepdims=True)
        _acc(dwg_ref, first, jnp.concatenate([col(dgate * g2), col(dgate * g1), col(dgate * hg)], axis=0))
        _acc(dwv_ref, first, jnp.concatenate([col(dval * v2), col(dval * v1), col(dval * hv)], axis=0))
        _acc(dbg_ref, first, col(dgate))
        _acc(dbv_ref, first, col(dval))

    act = jax.ShapeDtypeStruct((L, D_FF), BF16)
    w3 = jax.ShapeDtypeStruct((3, D_FF), F32)
    w1 = jax.ShapeDtypeStruct((1, D_FF), F32)
    return pl.pallas_call(
        body, name="conv_act_bwd", grid=(nc, L // tl),
        in_specs=[cur(0), cur(0), cur(nc), prev(0), prev(nc), par(3, 0), par(3, nc), par(1, 0), par(1, nc)],
        out_specs=[cur(0), cur(0), par(3, 0), par(3, 0), par(1, 0), par(1, 0)],
        out_shape=[act, act, w3, w3, w1, w1],
        compiler_params=_params("parallel", "arbitrary"))(da, h, h, h, h, conv_w, conv_w, conv_b, conv_b)


def _conv_t_call(dgate, dval, conv_w):
    L = dgate.shape[0]
    tl = _fit(L, 512)
    nc = D_FF // CONV_TILE
    cur, _, nxt, par = _conv_specs(L, tl, nc, False)

    def run(d, off, name):
        def body(d_ref, n_ref, w_ref, o_ref):
            not_last = (pl.program_id(0) < L // tl - 1).astype(F32)
            c = d_ref[...].astype(F32)
            head = n_ref[...].astype(F32)[0:8] * not_last
            w = w_ref[...]
            o_ref[...] = (w[2:3] * c + w[1:2] * _shift_up(c, head, 1) + w[0:1] * _shift_up(c, head, 2)).astype(BF16)

        return pl.pallas_call(
            body, name=name, grid=(L // tl, nc),
            in_specs=[cur(0), nxt(0), par(3, off)],
            out_specs=cur(0), out_shape=jax.ShapeDtypeStruct((L, D_FF), BF16),
            compiler_params=_params("parallel", "parallel"))(d, d, conv_w)

    return run(dgate, 0, "conv_t_gate"), run(dval, nc, "conv_t_val")


def _glu_call(y1, w_glu, b_glu):
    L, n = y1.shape
    tl = _fit(L, 512)

    def body(y_ref, w_ref, b_ref, o_ref):
        y2 = _gelu(y_ref[...])
        z = _dot(y2.astype(BF16), w_ref[...], NN) + b_ref[...]
        o_ref[...] = (y2 * _sigmoid(z)).astype(BF16)

    return pl.pallas_call(
        body, name="glu", grid=(L // tl,), in_specs=[_row(tl, n), _full((n, n)), _full((1, n))],
        out_specs=_row(tl, n), out_shape=jax.ShapeDtypeStruct((L, n), BF16),
        compiler_params=_params("parallel"))(y1, w_glu, b_glu)


def _glu_bwd_call(dout, y1, w_glu, b_glu):
    L, n = y1.shape
    tl = _fit(L, 512)

    def body(do_ref, y_ref, w_ref, b_ref, dy_ref, dw_ref, db_ref):
        first = pl.program_id(0) == 0
        y1_ = y_ref[...]
        y2 = _gelu(y1_)
        y2b = y2.astype(BF16)
        w = w_ref[...]
        sg = _sigmoid(_dot(y2b, w, NN) + b_ref[...])
        dout_ = do_ref[...].astype(F32)
        dz = dout_ * y2 * sg * (1.0 - sg)
        dzb = dz.astype(BF16)
        dy2 = dout_ * sg + _dot(dzb, w, NT)
        dy_ref[...] = dy2 * _gelu_grad(y1_)
        _acc(dw_ref, first, _dot(y2b, dzb, TN))
        _acc(db_ref, first, jnp.sum(dz, axis=0, keepdims=True))

    return pl.pallas_call(
        body, name="glu_bwd", grid=(L // tl,),
        in_specs=[_row(tl, n), _row(tl, n), _full((n, n)), _full((1, n))],
        out_specs=[_row(tl, n), _full((n, n)), _full((1, n))],
        out_shape=[jax.ShapeDtypeStruct((L, n), F32), jax.ShapeDtypeStruct((n, n), F32), jax.ShapeDtypeStruct((1, n), F32)],
        compiler_params=_params("arbitrary"))(dout, y1, w_glu, b_glu)


ATTN_TILE = 512
ATTN_SCALE = 1.0 / math.sqrt(QK_HEAD)


def _attn_fwd_call(q, kv):
    L = q.shape[0]
    t = _fit(L, ATTN_TILE)
    nq = L // t

    def body(q_ref, k_ref, v_ref, o_ref, lse_ref, m_s, l_s, acc_s):
        i = pl.program_id(1)
        m_s[...] = jnp.full((t, 1), NEG, F32)
        l_s[...] = jnp.zeros((t, 1), F32)
        acc_s[...] = jnp.zeros((t, LANES), F32)
        qv = q_ref[...]
        row = i * t + lax.broadcasted_iota(jnp.int32, (t, t), 0)
        col0 = lax.broadcasted_iota(jnp.int32, (t, t), 1)

        def step(kb, carry):
            off = pl.multiple_of(kb * t, t)
            s = _dot(qv, k_ref[pl.ds(off, t), :], NT) * ATTN_SCALE
            s = jnp.where(col0 + kb * t <= row, s, NEG)
            m_prev = m_s[...]
            m_new = jnp.maximum(m_prev, jnp.max(s, axis=1, keepdims=True))
            alpha = jnp.exp(m_prev - m_new)
            p = jnp.exp(s - m_new)
            l_s[...] = alpha * l_s[...] + jnp.sum(p, axis=1, keepdims=True)
            acc_s[...] = alpha * acc_s[...] + _dot(p.astype(BF16), v_ref[pl.ds(off, t), :], NN)
            m_s[...] = m_new
            return carry

        lax.fori_loop(0, i + 1, step, 0)
        o_ref[...] = (acc_s[...] / l_s[...]).astype(BF16)
        lse_ref[0] = m_s[...] + jnp.log(l_s[...])

    return pl.pallas_call(
        body, name="attn_fwd", grid=(N_HEADS, nq),
        in_specs=[pl.BlockSpec((t, LANES), lambda h, i: (i, h)),
                  pl.BlockSpec((L, LANES), lambda h, i: (0, 2 * h)),
                  pl.BlockSpec((L, LANES), lambda h, i: (0, 2 * h + 1))],
        out_specs=[pl.BlockSpec((t, LANES), lambda h, i: (i, h)),
                   pl.BlockSpec((1, t, 1), lambda h, i: (h, i, 0))],
        out_shape=[jax.ShapeDtypeStruct((L, HEAD_PAD), BF16), jax.ShapeDtypeStruct((N_HEADS, L, 1), F32)],
        scratch_shapes=[pltpu.VMEM((t, 1), F32), pltpu.VMEM((t, 1), F32), pltpu.VMEM((t, LANES), F32)],
        compiler_params=_params("parallel", "parallel"))(q, kv, kv)


def _attn_bwd_call(q, kv, o, do, lse):
    L = q.shape[0]
    t = _fit(L, ATTN_TILE)
    nq = L // t

    def body(q_ref, do_ref, o_ref, lse_ref, k_ref, v_ref, dq_ref, dkv_ref, dk_s, dv_s):
        j = pl.program_id(1)

        @pl.when(j == 0)
        def _():
            dq_ref[...] = jnp.zeros((L, LANES), F32)

        dk_s[...] = jnp.zeros((t, LANES), F32)
        dv_s[...] = jnp.zeros((t, LANES), F32)
        kblk, vblk = k_ref[...], v_ref[...]
        col = j * t + lax.broadcasted_iota(jnp.int32, (t, t), 1)
        row0 = lax.broadcasted_iota(jnp.int32, (t, t), 0)

        def step(i, carry):
            off = pl.multiple_of(i * t, t)
            qi = q_ref[pl.ds(off, t), :]
            doi = do_ref[pl.ds(off, t), :]
            delta = jnp.sum(doi.astype(F32) * o_ref[pl.ds(off, t), :].astype(F32), axis=1, keepdims=True)
            s = _dot(qi, kblk, NT) * ATTN_SCALE
            s = jnp.where(col <= row0 + i * t, s, NEG)
            p = jnp.exp(s - lse_ref[0, pl.ds(off, t), :])
            dv_s[...] += _dot(p.astype(BF16), doi, TN)
            dp = _dot(doi, vblk, NT)
            ds = (p * (dp - delta) * ATTN_SCALE).astype(BF16)
            dk_s[...] += _dot(ds, qi, TN)
            dq_ref[pl.ds(off, t), :] += _dot(ds, kblk, NN)
            return carry

        lax.fori_loop(j, nq, step, 0)
        dkv_ref[:, 0:LANES] = dk_s[...]
        dkv_ref[:, LANES:2 * LANES] = dv_s[...]

    whole = lambda: pl.BlockSpec((L, LANES), lambda h, j: (0, h))
    return pl.pallas_call(
        body, name="attn_bwd", grid=(N_HEADS, nq),
        in_specs=[whole(), whole(), whole(), pl.BlockSpec((1, L, 1), lambda h, j: (h, 0, 0)),
                  pl.BlockSpec((t, LANES), lambda h, j: (j, 2 * h)),
                  pl.BlockSpec((t, LANES), lambda h, j: (j, 2 * h + 1))],
        out_specs=[whole(), pl.BlockSpec((t, 2 * LANES), lambda h, j: (j, h))],
        out_shape=[jax.ShapeDtypeStruct((L, HEAD_PAD), F32), jax.ShapeDtypeStruct((L, 2 * HEAD_PAD), F32)],
        scratch_shapes=[pltpu.VMEM((t, LANES), F32), pltpu.VMEM((t, LANES), F32)],
        compiler_params=_params("parallel", "arbitrary"))(q, do, o, lse, kv, kv)


def _disc(lr, li, ldt, br, bi):
    dt = jnp.exp(ldt)
    mag = jnp.exp(lr * dt)
    ang = li * dt
    a_re, a_im = mag * jnp.cos(ang), mag * jnp.sin(ang)
    den = lr * lr + li * li
    n_re, n_im = a_re - 1.0, a_im
    z_re = (n_re * lr + n_im * li) / den
    z_im = (n_im * lr - n_re * li) / den
    return a_re, a_im, z_re * br - z_im * bi, z_re * bi + z_im * br


def _disc_call(lr, li, ldt, br, bi):
    def body(lr_ref, li_ref, ldt_ref, br_ref, bi_ref, ar_ref, ai_ref, bbr_ref, bbi_ref):
        ar_ref[...], ai_ref[...], bbr_ref[...], bbi_ref[...] = _disc(
            lr_ref[...], li_ref[...], ldt_ref[...], br_ref[...], bi_ref[...])

    c1 = jax.ShapeDtypeStruct((SSM_NSTATE, 1), F32)
    c16 = jax.ShapeDtypeStruct((SSM_NSTATE, SSM_GROUP), F32)
    return pl.pallas_call(body, name="ssm_disc", out_shape=[c1, c1, c16, c16])(lr, li, ldt, br, bi)


def _disc_bwd_call(lr, li, ldt, br, bi, dar, dai, dbbr, dbbi):
    def body(lr_ref, li_ref, ldt_ref, br_ref, bi_ref, dar_ref, dai_ref, dbbr_ref, dbbi_ref,
             dlr_ref, dli_ref, dldt_ref, dbr_ref, dbi_ref):
        _, vjp = jax.vjp(_disc, lr_ref[...], li_ref[...], ldt_ref[...], br_ref[...], bi_ref[...])
        dlr_ref[...], dli_ref[...], dldt_ref[...], dbr_ref[...], dbi_ref[...] = vjp(
            (dar_ref[...], dai_ref[...], dbbr_ref[...], dbbi_ref[...]))

    c1 = jax.ShapeDtypeStruct((SSM_NSTATE, 1), F32)
    c16 = jax.ShapeDtypeStruct((SSM_NSTATE, SSM_GROUP), F32)
    return pl.pallas_call(body, name="ssm_disc_bwd", out_shape=[c1, c1, c1, c16, c16])(
        lr, li, ldt, br, bi, dar, dai, dbbr, dbbi)


SSM_ROWS = 512
SSM_CW = SSM_NSTATE // SSM_CHUNKS
SSM_CU = SSM_WIDTH // SSM_CHUNKS


def _cmul(ar, ai, br, bi):
    return ar * br - ai * bi, ar * bi + ai * br


def _power(ar1, ai1, n):
    def step(_, c):
        return _cmul(c[0], c[1], ar1, ai1)

    return lax.fori_loop(0, n, step, (jnp.ones_like(ar1), jnp.zeros_like(ar1)))


def _tile(k):
    return pl.ds(pl.multiple_of(k * 8, 8), 8)


def _ssm_fwd_call(u, a_re, a_im, bb_re, bb_im, cm_re, cm_im, d_skip):
    L = u.shape[0]
    seg = L // 8
    rb = _fit(L, SSM_ROWS)

    def body(u_ref, ar_ref, ai_ref, bbr_ref, bbi_ref, cmr_ref, cmi_ref, d_ref, y_ref, sre_hbm, sim_hbm,
             s_re, s_im, sems):
        q = pl.program_id(0)

        def bu_step(r, c):
            rows = pl.ds(pl.multiple_of(r * rb, rb), rb)
            ub = u_ref[rows, :].astype(BF16)
            s_re[rows, :] = _dot(ub, bbr_ref[0], NN)
            s_im[rows, :] = _dot(ub, bbi_ref[0], NN)
            return c

        lax.fori_loop(0, L // rb, bu_step, 0)
        ar1, ai1 = ar_ref[...], ai_ref[...]
        ar = jnp.broadcast_to(ar1, (8, SSM_CW))
        ai = jnp.broadcast_to(ai1, (8, SSM_CW))

        def local(k, c):
            nr, ni = _cmul(ar, ai, c[0], c[1])
            nr = nr + s_re[_tile(k), :]
            ni = ni + s_im[_tile(k), :]
            s_re[_tile(k), :] = nr
            s_im[_tile(k), :] = ni
            return nr, ni

        zero8 = jnp.zeros((8, SSM_CW), F32)
        lax.fori_loop(0, seg, local, (zero8, zero8))
        pr, pi = _power(ar1, ai1, seg)
        end_r = s_re[pl.ds((seg - 1) * 8, 8), :]
        end_i = s_im[pl.ds((seg - 1) * 8, 8), :]
        er = jnp.zeros((1, SSM_CW), F32)
        ei = jnp.zeros((1, SSM_CW), F32)
        rows_r, rows_i = [er], [ei]
        for j in range(7):
            tr, ti = _cmul(pr, pi, er, ei)
            er, ei = end_r[j:j + 1] + tr, end_i[j:j + 1] + ti
            rows_r.append(er)
            rows_i.append(ei)
        e_r = jnp.concatenate(rows_r, axis=0)
        e_i = jnp.concatenate(rows_i, axis=0)

        def fix(k, c):
            wr, wi = _cmul(c[0], c[1], ar, ai)
            fr, fi = _cmul(wr, wi, e_r, e_i)
            s_re[_tile(k), :] += fr
            s_im[_tile(k), :] += fi
            return wr, wi

        lax.fori_loop(0, seg, fix, (jnp.ones((8, SSM_CW), F32), zero8))
        out_r = pltpu.make_async_copy(s_re, sre_hbm.at[q], sems.at[0])
        out_i = pltpu.make_async_copy(s_im, sim_hbm.at[q], sems.at[1])
        out_r.start()
        out_i.start()

        def y_step(r, c):
            rows = pl.ds(pl.multiple_of(r * rb, rb), rb)
            y = _dot(s_re[rows, :].astype(BF16), cmr_ref[0], NN) - _dot(s_im[rows, :].astype(BF16), cmi_ref[0], NN)
            y_ref[rows, :] = y + d_ref[...] * u_ref[rows, :]
            return c

        lax.fori_loop(0, L // rb, y_step, 0)
        out_r.wait()
        out_i.wait()

    chunk = lambda rows, cols: pl.BlockSpec((rows, cols), lambda q: (0, q))
    mat = lambda r, c: pl.BlockSpec((1, r, c), lambda q: (q, 0, 0))
    anyspec = pl.BlockSpec(memory_space=pl.ANY)
    states = jax.ShapeDtypeStruct((SSM_CHUNKS, L, SSM_CW), F32)
    return pl.pallas_call(
        body, name="ssm_fwd", grid=(SSM_CHUNKS,),
        in_specs=[chunk(L, SSM_CU), chunk(1, SSM_CW), chunk(1, SSM_CW), mat(SSM_CU, SSM_CW), mat(SSM_CU, SSM_CW),
                  mat(SSM_CW, SSM_CU), mat(SSM_CW, SSM_CU), chunk(1, SSM_CU)],
        out_specs=[chunk(L, SSM_CU), anyspec, anyspec],
        out_shape=[jax.ShapeDtypeStruct((L, SSM_WIDTH), F32), states, states],
        scratch_shapes=[pltpu.VMEM((L, SSM_CW), F32), pltpu.VMEM((L, SSM_CW), F32), pltpu.SemaphoreType.DMA((2,))],
        compiler_params=_params("arbitrary", vmem=VMEM_BIG))(u, a_re, a_im, bb_re, bb_im, cm_re, cm_im, d_skip)


def _ssm_bwd_call(dy, u, s_re_all, s_im_all, a_re, a_im, bb_re, bb_im, cm_re, cm_im, d_skip):
    L = u.shape[0]
    seg = L // 8
    rb = _fit(L, SSM_ROWS)

    def body(dy_ref, u_ref, sre_hbm, sim_hbm, ar_ref, ai_ref, bbr_ref, bbi_ref, cmr_ref, cmi_ref, d_ref,
             du_ref, dbbr_ref, dbbi_ref, dcmr_ref, dcmi_ref, dar_ref, dai_ref, dd_ref,
             g_re, g_im, s_re, s_im, sems):
        q = pl.program_id(0)
        in_r = pltpu.make_async_copy(sre_hbm.at[q], s_re, sems.at[0])
        in_i = pltpu.make_async_copy(sim_hbm.at[q], s_im, sems.at[1])
        in_r.start()
        in_i.start()

        def ds_step(r, c):
            rows = pl.ds(pl.multiple_of(r * rb, rb), rb)
            dyb = dy_ref[rows, :].astype(BF16)
            g_re[rows, :] = _dot(dyb, cmr_ref[0], NT)
            g_im[rows, :] = -_dot(dyb, cmi_ref[0], NT)
            return c

        lax.fori_loop(0, L // rb, ds_step, 0)
        ar1, ai1 = ar_ref[...], ai_ref[...]
        ar = jnp.broadcast_to(ar1, (8, SSM_CW))
        nai = jnp.broadcast_to(-ai1, (8, SSM_CW))

        def local(kk, c):
            k = seg - 1 - kk
            nr, ni = _cmul(ar, nai, c[0], c[1])
            nr = nr + g_re[_tile(k), :]
            ni = ni + g_im[_tile(k), :]
            g_re[_tile(k), :] = nr
            g_im[_tile(k), :] = ni
            return nr, ni

        zero8 = jnp.zeros((8, SSM_CW), F32)
        lax.fori_loop(0, seg, local, (zero8, zero8))
        pr, pi = _power(ar1, -ai1, seg)
        head_r = g_re[pl.ds(0, 8), :]
        head_i = g_im[pl.ds(0, 8), :]
        fr = jnp.zeros((1, SSM_CW), F32)
        fi = jnp.zeros((1, SSM_CW), F32)
        rows_r, rows_i = [fr], [fi]
        for j in range(6, -1, -1):
            tr, ti = _cmul(pr, pi, fr, fi)
            fr, fi = head_r[j + 1:j + 2] + tr, head_i[j + 1:j + 2] + ti
            rows_r.insert(0, fr)
            rows_i.insert(0, fi)
        f_r = jnp.concatenate(rows_r, axis=0)
        f_i = jnp.concatenate(rows_i, axis=0)
        in_r.wait()
        in_i.wait()

        def fixed(k, wr, wi):
            xr, xi = _cmul(wr, wi, f_r, f_i)
            gr = g_re[_tile(k), :] + xr
            gi = g_im[_tile(k), :] + xi
            g_re[_tile(k), :] = gr
            g_im[_tile(k), :] = gi
            return gr, gi

        def fix(kk, c):
            k = seg - 1 - kk
            wr, wi = _cmul(c[0], c[1], ar, nai)
            gr, gi = fixed(k, wr, wi)
            pr_, pi_ = s_re[_tile(k - 1), :], s_im[_tile(k - 1), :]
            return wr, wi, c[2] + gr * pr_ + gi * pi_, c[3] + gi * pr_ - gr * pi_

        wr, wi, acc_r, acc_i = lax.fori_loop(0, seg - 1, fix, (jnp.ones((8, SSM_CW), F32), zero8, zero8, zero8))
        wr, wi = _cmul(wr, wi, ar, nai)
        gr, gi = fixed(0, wr, wi)
        row8 = lax.broadcasted_iota(jnp.int32, (8, SSM_CW), 0)
        pr_ = jnp.where(row8 > 0, pltpu.roll(s_re[pl.ds((seg - 1) * 8, 8), :], 1, 0), 0.0)
        pi_ = jnp.where(row8 > 0, pltpu.roll(s_im[pl.ds((seg - 1) * 8, 8), :], 1, 0), 0.0)
        acc_r = acc_r + gr * pr_ + gi * pi_
        acc_i = acc_i + gi * pr_ - gr * pi_
        dar_ref[...] = jnp.sum(acc_r, axis=0, keepdims=True)
        dai_ref[...] = jnp.sum(acc_i, axis=0, keepdims=True)

        dbbr_ref[...] = jnp.zeros((1, SSM_CU, SSM_CW), F32)
        dbbi_ref[...] = jnp.zeros((1, SSM_CU, SSM_CW), F32)
        dcmr_ref[...] = jnp.zeros((1, SSM_CW, SSM_CU), F32)
        dcmi_ref[...] = jnp.zeros((1, SSM_CW, SSM_CU), F32)
        dd_ref[...] = jnp.zeros((1, SSM_CU), F32)

        def grad_step(r, c):
            rows = pl.ds(pl.multiple_of(r * rb, rb), rb)
            ub, dyv = u_ref[rows, :], dy_ref[rows, :]
            ubb, dyb = ub.astype(BF16), dyv.astype(BF16)
            grb, gib = g_re[rows, :].astype(BF16), g_im[rows, :].astype(BF16)
            dbbr_ref[0] += _dot(ubb, grb, TN)
            dbbi_ref[0] += _dot(ubb, gib, TN)
            dcmr_ref[0] += _dot(s_re[rows, :].astype(BF16), dyb, TN)
            dcmi_ref[0] -= _dot(s_im[rows, :].astype(BF16), dyb, TN)
            du_ref[rows, :] = _dot(grb, bbr_ref[0], NT) + _dot(gib, bbi_ref[0], NT) + d_ref[...] * dyv
            dd_ref[...] += jnp.sum(dyv * ub, axis=0, keepdims=True)
            return c

        lax.fori_loop(0, L // rb, grad_step, 0)

    chunk = lambda rows, cols: pl.BlockSpec((rows, cols), lambda q: (0, q))
    mat = lambda r, c: pl.BlockSpec((1, r, c), lambda q: (q, 0, 0))
    anyspec = pl.BlockSpec(memory_space=pl.ANY)
    big = lambda: pltpu.VMEM((L, SSM_CW), F32)
    return pl.pallas_call(
        body, name="ssm_bwd", grid=(SSM_CHUNKS,),
        in_specs=[chunk(L, SSM_CU), chunk(L, SSM_CU), anyspec, anyspec, chunk(1, SSM_CW), chunk(1, SSM_CW),
                  mat(SSM_CU, SSM_CW), mat(SSM_CU, SSM_CW), mat(SSM_CW, SSM_CU), mat(SSM_CW, SSM_CU), chunk(1, SSM_CU)],
        out_specs=[chunk(L, SSM_CU), mat(SSM_CU, SSM_CW), mat(SSM_CU, SSM_CW), mat(SSM_CW, SSM_CU), mat(SSM_CW, SSM_CU),
                   chunk(1, SSM_CW), chunk(1, SSM_CW), chunk(1, SSM_CU)],
        out_shape=[jax.ShapeDtypeStruct((L, SSM_WIDTH), F32),
                   jax.ShapeDtypeStruct((SSM_CHUNKS, SSM_CU, SSM_CW), F32), jax.ShapeDtypeStruct((SSM_CHUNKS, SSM_CU, SSM_CW), F32),
                   jax.ShapeDtypeStruct((SSM_CHUNKS, SSM_CW, SSM_CU), F32), jax.ShapeDtypeStruct((SSM_CHUNKS, SSM_CW, SSM_CU), F32),
                   jax.ShapeDtypeStruct((1, SSM_NSTATE), F32), jax.ShapeDtypeStruct((1, SSM_NSTATE), F32),
                   jax.ShapeDtypeStruct((1, SSM_WIDTH), F32)],
        scratch_shapes=[big(), big(), big(), big(), pltpu.SemaphoreType.DMA((2,))],
        compiler_params=_params("arbitrary", vmem=VMEM_BIG))(
            dy, u, s_re_all, s_im_all, a_re, a_im, bb_re, bb_im, cm_re, cm_im, d_skip)


def _place():
    return lax.axis_index("x"), lax.axis_index("y"), lax.axis_index("c")


def _all_gather_call(block, name):
    R = block.shape[0]

    def body(x_ref, out_ref, send_sems, recv_sems, local_sem):
        x, y, c = _place()
        me, sibling = (x, y, c), (x, y, 1 - c)
        chips = [(1 - x, y), (x, 1 - y), (1 - x, 1 - y)]

        def slot(px, py, pc):
            return out_ref.at[4 * px + 2 * py + pc]

        def copy(k, blk, to, src=None):
            return pltpu.make_async_remote_copy(
                src_ref=slot(*blk) if src is None else src, dst_ref=slot(*blk),
                send_sem=send_sems.at[k], recv_sem=recv_sems.at[k], device_id=to, device_id_type=MESH_ID)

        mine = pltpu.make_async_copy(x_ref, slot(*me), local_sem)
        mine.start()
        first = [copy(0, me, sibling, src=x_ref)]
        first += [copy(1 + j, me, (*chip, c), src=x_ref) for j, chip in enumerate(chips)]
        for cp in first:
            cp.start()
        passed = [copy(4 + j, (*chip, c), sibling) for j, chip in enumerate(chips)]
        for j, chip in enumerate(chips):
            copy(1 + j, (*chip, c), me).wait_recv()
            passed[j].start()
        copy(0, sibling, me).wait_recv()
        for j, chip in enumerate(chips):
            copy(4 + j, (*chip, 1 - c), me).wait_recv()
        for cp in first + passed:
            cp.wait_send()
        mine.wait()

    anyspec = pl.BlockSpec(memory_space=pl.ANY)
    return pl.pallas_call(
        body, name=name, in_specs=[anyspec], out_specs=anyspec,
        out_shape=jax.ShapeDtypeStruct((N_DEV,) + block.shape, block.dtype),
        scratch_shapes=[pltpu.SemaphoreType.DMA((7,)), pltpu.SemaphoreType.DMA((7,)), pltpu.SemaphoreType.DMA],
    )(block)


def _exchange_call(parts, name):
    def body(p_ref, out_ref, send_sems, recv_sems, local_sem):
        x, y, c = _place()
        me = 4 * x + 2 * y + c

        def flip(k):
            px = 1 - x if k & 4 else x
            py = 1 - y if k & 2 else y
            pc = 1 - c if k & 1 else c
            return (px, py, pc), 4 * px + 2 * py + pc

        def copy(k):
            peer, peer_slot = flip(k)
            return pltpu.make_async_remote_copy(
                src_ref=p_ref.at[peer_slot], dst_ref=out_ref.at[me],
                send_sem=send_sems.at[k - 1], recv_sem=recv_sems.at[k - 1], device_id=peer, device_id_type=MESH_ID)

        def arrival(k):
            peer, peer_slot = flip(k)
            return pltpu.make_async_remote_copy(
                src_ref=p_ref.at[peer_slot], dst_ref=out_ref.at[peer_slot],
                send_sem=send_sems.at[k - 1], recv_sem=recv_sems.at[k - 1], device_id=peer, device_id_type=MESH_ID)

        mine = pltpu.make_async_copy(p_ref.at[me], out_ref.at[me], local_sem)
        mine.start()
        sends = [copy(k) for k in range(1, N_DEV)]
        for cp in sends:
            cp.start()
        for k in range(1, N_DEV):
            arrival(k).wait_recv()
        for cp in sends:
            cp.wait_send()
        mine.wait()

    anyspec = pl.BlockSpec(memory_space=pl.ANY)
    return pl.pallas_call(
        body, name=name, in_specs=[anyspec], out_specs=anyspec,
        out_shape=jax.ShapeDtypeStruct(parts.shape, parts.dtype),
        scratch_shapes=[pltpu.SemaphoreType.DMA((7,)), pltpu.SemaphoreType.DMA((7,)), pltpu.SemaphoreType.DMA],
    )(parts)


def _adam_call(slices, w, m, v, name):
    R = w.shape[0]
    tr = _fit(R, PACK_ROWS, 16)
    c1 = 1.0 / (1.0 - ADAM_B1 ** ADAM_STEP)
    c2 = 1.0 / (1.0 - ADAM_B2 ** ADAM_STEP)

    def body(s_ref, w_ref, m_ref, v_ref, g_ref, d_ref, mo_ref, vo_ref):
        g = s_ref[0].astype(F32)
        for k in range(1, N_DEV):
            g = g + s_ref[k].astype(F32)
        m_new = ADAM_B1 * m_ref[...] + (1.0 - ADAM_B1) * g
        v_new = ADAM_B2 * v_ref[...] + (1.0 - ADAM_B2) * (g * g)
        g_ref[...] = g
        mo_ref[...] = m_new
        vo_ref[...] = v_new
        d_ref[...] = -ADAM_LR * ((m_new * c1) / (jnp.sqrt(v_new * c2) + ADAM_EPS) + ADAM_WD * w_ref[...])

    flat = pl.BlockSpec((tr, LANES), lambda i: (i, 0))
    out = jax.ShapeDtypeStruct((R, LANES), F32)
    return pl.pallas_call(
        body, name=name, grid=(R // tr,),
        in_specs=[pl.BlockSpec((N_DEV, tr, LANES), lambda i: (0, i, 0)), flat, flat, flat],
        out_specs=[flat, flat, flat, flat], out_shape=[out, out, out, out],
        compiler_params=_params("parallel"))(slices, w, m, v)


BIG = (("w_in", 1024, 404, 1), ("w_uq", 384, 96, 1), ("w_uk", 256, 64, 1), ("w_uv", 256, 64, 1),
       ("w_glu", 64, 512, 0), ("w_branch_attn", 512, 128, 1), ("w_branch_ssm", 512, 128, 1),
       ("w_out", 128, 1024, 0), ("w_up", 1024, 704, 1), ("w_down", 352, 1024, 0), ("conv_w", 3, 704, 1))
SMALL = (("mix_norm_pre", (1024,)), ("q_norm", (384,)), ("kv_norm", (256,)), ("ssm_lambda_re", (32, 64)),
         ("ssm_lambda_im", (32, 64)), ("ssm_log_dt", (32,)), ("ssm_b_re", (32, 64, 16)), ("ssm_b_im", (32, 64, 16)),
         ("ssm_c_re", (32, 16, 64)), ("ssm_c_im", (32, 16, 64)), ("ssm_d", (32, 16)), ("b_glu", (512,)),
         ("b_gate", (2048,)), ("mix_norm_post", (1024,)), ("ffn_norm_pre", (1024,)), ("conv_b", (5632,)),
         ("ffn_norm_post", (1024,)))


def _pad_rows(flat, lead):
    n = flat.shape[-1]
    total = -(-n // (PACK_ROWS * LANES)) * PACK_ROWS * LANES
    flat = jnp.pad(flat, [(0, 0)] * lead + [(0, total - n)])
    return flat.reshape(flat.shape[:lead] + (total // LANES, LANES))


def _pack(pieces, lead=0):
    flats = []
    for p in pieces:
        f = p.reshape(p.shape[:lead] + (-1,))
        n = f.shape[-1]
        flats.append(jnp.pad(f, [(0, 0)] * lead + [(0, -n % LANES)]))
    return _pad_rows(jnp.concatenate(flats, axis=-1), lead)


def _unpack(buf, shapes, lead=0):
    flat = buf.reshape(buf.shape[:lead] + (-1,))
    out, off = [], 0
    for shp in shapes:
        n = math.prod(shp)
        out.append(lax.slice_in_dim(flat, off, off + n, axis=lead).reshape(buf.shape[:lead] + tuple(shp)))
        off += n + (-n % LANES)
    return out


def _to_slices(full, rows, cols, axis):
    if axis == 1:
        return full.reshape(rows, N_DEV, cols).transpose(1, 0, 2)
    return full.reshape(N_DEV, rows, cols)


def _from_slices(parts, rows, cols, axis):
    if axis == 1:
        return parts.transpose(1, 0, 2).reshape(rows, N_DEV * cols)
    return parts.reshape(N_DEV * rows, cols)


def _head_pad_cols(w, width):
    k = w.shape[0]
    return jnp.pad(w.reshape(k, N_HEADS, width), ((0, 0), (0, 0), (0, LANES - width))).reshape(k, HEAD_PAD)


def _head_unpad_cols(w, width):
    k = w.shape[0]
    return w.reshape(k, N_HEADS, LANES)[:, :, :width].reshape(k, N_HEADS * width)


def _time_perm(a, L):
    return a.reshape(8, L // 8, a.shape[-1]).transpose(1, 0, 2).reshape(L, a.shape[-1])


def _time_unperm(a, L):
    return a.reshape(L // 8, 8, a.shape[-1]).transpose(1, 0, 2).reshape(L, a.shape[-1])


def _block_diag(w, rows_first):
    eye = jnp.eye(8, dtype=w.dtype)
    g = w.reshape(SSM_CHUNKS, 8, w.shape[1], w.shape[2])
    return jnp.einsum("qgrc,gk->qgrkc", g, eye).reshape(SSM_CHUNKS, 8 * w.shape[1], 8 * w.shape[2])


def _block_diag_t(m, r, c):
    eye = jnp.eye(8, dtype=m.dtype)
    return jnp.einsum("qgrkc,gk->qgrc", m.reshape(SSM_CHUNKS, 8, r, 8, c), eye).reshape(SSM_GROUPS, r, c)


def kernel(x, positions, mix_norm_pre, w_in, q_norm, w_uq, kv_norm, w_uk, w_uv, ssm_lambda_re, ssm_lambda_im, ssm_log_dt, ssm_b_re, ssm_b_im, ssm_c_re, ssm_c_im, ssm_d, w_glu, b_glu, w_branch_attn, w_branch_ssm, b_gate, w_out, mix_norm_post, ffn_norm_pre, w_up, conv_w, conv_b, w_down, ffn_norm_post, loss_target, m_mix_norm_pre, m_w_in, m_q_norm, m_w_uq, m_kv_norm, m_w_uk, m_w_uv, m_ssm_lambda_re, m_ssm_lambda_im, m_ssm_log_dt, m_ssm_b_re, m_ssm_b_im, m_ssm_c_re, m_ssm_c_im, m_ssm_d, m_w_glu, m_b_glu, m_w_branch_attn, m_w_branch_ssm, m_b_gate, m_w_out, m_mix_norm_post, m_ffn_norm_pre, m_w_up, m_conv_w, m_conv_b, m_w_down, m_ffn_norm_post, v_mix_norm_pre, v_w_in, v_q_norm, v_w_uq, v_kv_norm, v_w_uk, v_w_uv, v_ssm_lambda_re, v_ssm_lambda_im, v_ssm_log_dt, v_ssm_b_re, v_ssm_b_im, v_ssm_c_re, v_ssm_c_im, v_ssm_d, v_w_glu, v_b_glu, v_w_branch_attn, v_w_branch_ssm, v_b_gate, v_w_out, v_mix_norm_post, v_ffn_norm_pre, v_w_up, v_conv_w, v_conv_b, v_w_down, v_ffn_norm_post):
    given = dict(locals())
    L = x.shape[1]
    xs = x[0]
    target = loss_target[0]

    shard_bits = []
    for name, rows, cols, _ in BIG:
        w = given[name][0]
        shard_bits.append(lax.bitcast_convert_type(w, BF16) if name == "conv_w" else w.astype(BF16))
    gathered = _all_gather_call(_pack(shard_bits), "gather_weights")
    shapes = [(rows, cols, 2) if name == "conv_w" else (rows, cols) for name, rows, cols, _ in BIG]
    W = {}
    for (name, rows, cols, axis), parts in zip(BIG, _unpack(gathered, shapes, lead=1)):
        if name == "conv_w":
            parts = lax.bitcast_convert_type(parts, F32)
        W[name] = _from_slices(parts, rows, cols, axis)

    wi = W["w_in"]
    kr_cols = jnp.pad(wi[:, 640:672], ((0, 0), (QK_NOPE, LANES - QK_HEAD)))
    w_in_p = jnp.concatenate([wi[:, :640], kr_cols, wi[:, 672:]], axis=1)
    w_uq_p = _head_pad_cols(W["w_uq"], QK_HEAD)
    w_kv_p = jnp.stack([_head_pad_cols(W["w_uk"], QK_NOPE).reshape(KV_RANK, N_HEADS, LANES),
                        _head_pad_cols(W["w_uv"], V_HEAD).reshape(KV_RANK, N_HEADS, LANES)], axis=2
                       ).reshape(KV_RANK, 2 * HEAD_PAD)
    w_ba_p = jnp.pad(W["w_branch_attn"].reshape(N_HEADS, V_HEAD, D_MODEL), ((0, 0), (0, LANES - V_HEAD), (0, 0))
                     ).reshape(HEAD_PAD, D_MODEL)

    hn1 = _rms_fwd_call(xs, mix_norm_pre, "rms_pre")
    proj = _mm(hn1, w_in_p, "mm_in", tn=256)
    qn, ckvn = _mla_norms_call(proj, q_norm, kv_norm)
    q_pad = _mm(qn, w_uq_p, "mm_uq")
    kv_pad = _mm(ckvn, w_kv_p, "mm_ukv")
    half = jnp.arange(QK_ROPE // 2, dtype=F32)
    inv_freq = ROPE_THETA ** (-2.0 * half / QK_ROPE)
    inv_freq = jnp.pad(jnp.concatenate([inv_freq, inv_freq]), (QK_NOPE, LANES - QK_HEAD)).reshape(1, LANES)
    pos_col = positions.astype(F32).reshape(L, 1)
    q_r, kv_r, cosf, sinf = _mla_prep_call(q_pad, kv_pad, proj, pos_col, inv_freq)
    attn, lse = _attn_fwd_call(q_r, kv_r)

    col = lambda a: a.reshape(SSM_NSTATE, -1)
    lr_c, li_c = col(ssm_lambda_re[0]), col(ssm_lambda_im[0])
    ldt_c = col(jnp.broadcast_to(ssm_log_dt[0][:, None], (SSM_GROUPS, SSM_STATE)))
    br_c, bi_c = col(ssm_b_re[0]), col(ssm_b_im[0])
    a_re_c, a_im_c, bb_re_c, bb_im_c = _disc_call(lr_c, li_c, ldt_c, br_c, bi_c)
    a_re, a_im = a_re_c.reshape(1, SSM_NSTATE), a_im_c.reshape(1, SSM_NSTATE)
    to_bb = lambda b: _block_diag(b.reshape(SSM_GROUPS, SSM_STATE, SSM_GROUP).transpose(0, 2, 1), True).astype(BF16)
    bb_re, bb_im = to_bb(bb_re_c), to_bb(bb_im_c)
    to_cm = lambda c_: _block_diag(c_[0].transpose(0, 2, 1), True).astype(BF16)
    cm_re, cm_im = to_cm(ssm_c_re), to_cm(ssm_c_im)
    d_skip = ssm_d.reshape(1, SSM_WIDTH)
    u_p = _time_perm(proj[:, P_U:P_GATE], L)
    y1, s_re, s_im = _ssm_fwd_call(u_p, a_re, a_im, bb_re, bb_im, cm_re, cm_im, d_skip)
    w_glu_b = W["w_glu"]
    ssm_p = _glu_call(y1, w_glu_b, b_glu)
    ssm = _time_unperm(ssm_p, L)

    pa = _mm(attn, w_ba_p, "mm_ba")
    ps = _mm(ssm, W["w_branch_ssm"], "mm_bs")
    merged = _merge_call(proj, b_gate, pa, ps)
    o = _mm(merged, W["w_out"], "mm_out")
    x2, hn2 = _post_mix_call(o, xs, mix_norm_post, ffn_norm_pre)
    h = _mm(hn2, W["w_up"], "mm_up")
    cw = W["conv_w"]
    act = _conv_act_call(h, cw, conv_b)
    ff = _mm(act, W["w_down"], "mm_down", tk=1408)
    loss_row, dy, dff, g_ffn_norm_post = _ffn_out_call(ff, x2, target, ffn_norm_post)
    loss = lax.psum(loss_row[0, 0], ("x", "y", "c"))

    da = _mm(dff, W["w_down"], "mm_down_dx", tb=True, tn=256)
    g_w_down = _mm(act, dff, "mm_down_dw", ta=True, tm=256)
    dgate, dval, dcw_g, dcw_v, dcb_g, dcb_v = _conv_act_bwd_call(da, h, cw, conv_b)
    g_conv_w = jnp.concatenate([dcw_g, dcw_v], axis=1)
    g_conv_b = jnp.concatenate([dcb_g, dcb_v], axis=1)
    dh_g, dh_v = _conv_t_call(dgate, dval, cw)
    dh = jnp.concatenate([dh_g, dh_v], axis=1)
    dhn2 = _mm(dh, W["w_up"], "mm_up_dx", tb=True)
    g_w_up = _mm(hn2, dh, "mm_up_dw", ta=True)
    dx2, do, g_ffn_norm_pre, g_mix_norm_post = _post_bwd_call(x2, dhn2, dy, o, ffn_norm_pre, mix_norm_post)
    dmerged = _mm(do, W["w_out"], "mm_out_dx", tb=True)
    g_w_out = _mm(merged, do, "mm_out_dw", ta=True)
    dpa, dps, dl0, dl1, db0, db1 = _merge_bwd_call(dmerged, proj, b_gate, pa, ps)
    g_b_gate = jnp.concatenate([db0, db1], axis=1)
    dattn = _mm(dpa, w_ba_p, "mm_ba_dx", tb=True, out_dtype=BF16)
    g_w_ba = _mm(attn, dpa, "mm_ba_dw", ta=True).reshape(N_HEADS, LANES, D_MODEL)[:, :V_HEAD].reshape(N_HEADS * V_HEAD, D_MODEL)
    dssm = _mm(dps, W["w_branch_ssm"], "mm_bs_dx", tb=True)
    g_w_bs = _mm(ssm, dps, "mm_bs_dw", ta=True)

    dy1, g_w_glu, g_b_glu = _glu_bwd_call(_time_perm(dssm, L), y1, w_glu_b, b_glu)
    du_p, dbb_re, dbb_im, dcm_re, dcm_im, da_re, da_im, g_ssm_d = _ssm_bwd_call(
        dy1, u_p, s_re, s_im, a_re, a_im, bb_re, bb_im, cm_re, cm_im, d_skip)
    du = _time_unperm(du_p, L)
    from_bb = lambda m: col(_block_diag_t(m, SSM_GROUP, SSM_STATE).transpose(0, 2, 1))
    dlr, dli, dldt, dbr, dbi = _disc_bwd_call(
        lr_c, li_c, ldt_c, br_c, bi_c, da_re.reshape(SSM_NSTATE, 1), da_im.reshape(SSM_NSTATE, 1), from_bb(dbb_re), from_bb(dbb_im))
    g_c_re = _block_diag_t(dcm_re, SSM_STATE, SSM_GROUP).transpose(0, 2, 1)
    g_c_im = _block_diag_t(dcm_im, SSM_STATE, SSM_GROUP).transpose(0, 2, 1)

    dq, dkv = _attn_bwd_call(q_r, kv_r, attn, dattn, lse)
    dq_p, dkv_p, dkr_p = _mla_prep_bwd_call(dq, dkv, cosf, sinf)
    dqn = _mm(dq_p, w_uq_p, "mm_uq_dx", tb=True)
    g_w_uq = _head_unpad_cols(_mm(qn, dq_p, "mm_uq_dw", ta=True), QK_HEAD)
    dckvn = _mm(dkv_p, w_kv_p, "mm_ukv_dx", tb=True)
    g_w_kv = _mm(ckvn, dkv_p, "mm_ukv_dw", ta=True).reshape(KV_RANK, N_HEADS, 2, LANES)
    g_w_uk = g_w_kv[:, :, 0, :QK_NOPE].reshape(KV_RANK, N_HEADS * QK_NOPE)
    g_w_uv = g_w_kv[:, :, 1, :V_HEAD].reshape(KV_RANK, N_HEADS * V_HEAD)
    dcqkv, g_q_norm, g_kv_norm = _mla_norms_bwd_call(proj, dqn, dckvn, q_norm, kv_norm)
    dproj = jnp.concatenate([dcqkv, dkr_p, du.astype(BF16), dl0, dl1], axis=1)
    dhn1 = _mm(dproj, w_in_p, "mm_in_dx", tb=True, tk=1664)
    g_w_in_p = _mm(hn1, dproj, "mm_in_dw", ta=True, tn=256)
    g_w_in = jnp.concatenate([g_w_in_p[:, :640], g_w_in_p[:, 640 + QK_NOPE:640 + QK_HEAD], g_w_in_p[:, 768:]], axis=1)
    grad_x, g_mix_norm_pre = _pre_bwd_call(xs, dhn1, dx2, mix_norm_pre)

    full_grads = {"w_in": g_w_in, "w_uq": g_w_uq, "w_uk": g_w_uk, "w_uv": g_w_uv, "w_glu": g_w_glu,
                  "w_branch_attn": g_w_ba, "w_branch_ssm": g_w_bs, "w_out": g_w_out, "w_up": g_w_up,
                  "w_down": g_w_down, "conv_w": g_conv_w}
    slices = _pack([_to_slices(full_grads[name], rows, cols, axis).astype(BF16) for name, rows, cols, axis in BIG], lead=1)
    received = _exchange_call(slices, "exchange_grads")
    pack_big = lambda prefix: _pack([given[prefix + name][0] for name, _, _, _ in BIG])
    big_out = _adam_call(received, pack_big(""), pack_big("m_"), pack_big("v_"), "adam_big")
    big_shapes = [(rows, cols) for _, rows, cols, _ in BIG]
    big_res = [dict(zip([n for n, _, _, _ in BIG], _unpack(buf, big_shapes))) for buf in big_out]

    small_grads = {"mix_norm_pre": g_mix_norm_pre, "q_norm": g_q_norm, "kv_norm": g_kv_norm,
                   "ssm_lambda_re": dlr, "ssm_lambda_im": dli,
                   "ssm_log_dt": jnp.sum(dldt.reshape(SSM_GROUPS, SSM_STATE), axis=1),
                   "ssm_b_re": dbr, "ssm_b_im": dbi, "ssm_c_re": g_c_re, "ssm_c_im": g_c_im, "ssm_d": g_ssm_d,
                   "b_glu": g_b_glu, "b_gate": g_b_gate, "mix_norm_post": g_mix_norm_post,
                   "ffn_norm_pre": g_ffn_norm_pre, "conv_b": g_conv_b, "ffn_norm_post": g_ffn_norm_post}
    partial = _pack([small_grads[name].reshape(shp) for name, shp in SMALL])
    all_partials = _all_gather_call(partial, "gather_small_grads")
    pack_small = lambda prefix: _pack([given[prefix + name][0] for name, _ in SMALL])
    small_out = _adam_call(all_partials, pack_small(""), pack_small("m_"), pack_small("v_"), "adam_small")
    small_res = [dict(zip([n for n, _ in SMALL], _unpack(buf, [shp for _, shp in SMALL]))) for buf in small_out]

    order = ["mix_norm_pre", "w_in", "q_norm", "w_uq", "kv_norm", "w_uk", "w_uv", "ssm_lambda_re", "ssm_lambda_im",
             "ssm_log_dt", "ssm_b_re", "ssm_b_im", "ssm_c_re", "ssm_c_im", "ssm_d", "w_glu", "b_glu", "w_branch_attn",
             "w_branch_ssm", "b_gate", "w_out", "mix_norm_post", "ffn_norm_pre", "w_up", "conv_w", "conv_b", "w_down",
             "ffn_norm_post"]
    outs = [loss, grad_x[None]]
    for kind in range(4):
        for name in order:
            src = big_res[kind] if name in big_res[kind] else small_res[kind]
            outs.append(src[name][None])
    return tuple(outs)
```

```python
import math

import jax
import jax.numpy as jnp
from jax import lax
from jax.experimental import pallas as pl
from jax.experimental.pallas import tpu as pltpu

F32 = jnp.float32
BF16 = jnp.bfloat16
MESH_ID = pl.DeviceIdType.MESH

N_DEV = 8
LANES = 128
D_MODEL = 1024
N_HEADS = 8
QK_NOPE = 64
QK_ROPE = 32
QK_HEAD = QK_NOPE + QK_ROPE
V_HEAD = 64
Q_RANK = 384
KV_RANK = 256
ROPE_THETA = 10000.0
SSM_WIDTH = 512
SSM_GROUP = 16
SSM_GROUPS = 32
SSM_STATE = 64
SSM_NSTATE = SSM_GROUPS * SSM_STATE
SSM_CHUNKS = 4
D_FF = 2816
EPS = 1e-6
ADAM_LR, ADAM_B1, ADAM_B2, ADAM_EPS, ADAM_WD, ADAM_STEP = 0.001, 0.9, 0.999, 1e-08, 0.01, 10

P_CQ, P_CKV, P_KR, P_U, P_GATE = 0, 384, 640, 768, 1280
P_IN = P_GATE + 2 * D_MODEL
HEAD_PAD = N_HEADS * LANES

PACK_ROWS = 1024
VMEM_BIG = 52 * 1024 * 1024

_GELU_C0 = math.sqrt(2.0 / math.pi)
_GELU_C1 = 0.044715
NEG = -1e30


def _fit(n, pref, mult=LANES):
    if n <= pref:
        return n
    t = (pref // mult) * mult
    while t > 0 and n % t:
        t -= mult
    assert t > 0, (n, pref, mult)
    return t


def _gelu(x):
    return 0.5 * x * (1.0 + jnp.tanh(_GELU_C0 * (x + _GELU_C1 * x * x * x)))


def _gelu_grad(x):
    x2 = x * x
    t = jnp.tanh(_GELU_C0 * x * (1.0 + _GELU_C1 * x2))
    return 0.5 * (1.0 + t) + 0.5 * x * (1.0 - t * t) * _GELU_C0 * (1.0 + 3.0 * _GELU_C1 * x2)


def _sigmoid(x):
    return 1.0 / (1.0 + jnp.exp(-x))


def _dot(a, b, dims):
    return lax.dot_general(a, b, (dims, ((), ())), preferred_element_type=F32)


NN = ((1,), (0,))
NT = ((1,), (1,))
TN = ((0,), (0,))


def _params(*sem, vmem=None):
    return pltpu.CompilerParams(dimension_semantics=tuple(sem), vmem_limit_bytes=vmem)


def _mm(a, b, name, ta=False, tb=False, out_dtype=F32, tm=1024, tn=512, tk=1024):
    if ta:
        K, M = a.shape
    else:
        M, K = a.shape
    if tb:
        N, K2 = b.shape
    else:
        K2, N = b.shape
    assert K == K2, (a.shape, b.shape, ta, tb)
    tm, tn, tk = _fit(M, tm), _fit(N, tn), _fit(K, tk)
    nk = K // tk
    dims = ((0,) if ta else (1,), (1,) if tb else (0,))

    def body(a_ref, b_ref, o_ref, *scratch):
        part = _dot(a_ref[...].astype(BF16), b_ref[...].astype(BF16), dims)
        if nk == 1:
            o_ref[...] = part.astype(out_dtype)
        else:
            acc_ref, = scratch
            k = pl.program_id(2)

            @pl.when(k == 0)
            def _():
                acc_ref[...] = part

            @pl.when(k > 0)
            def _():
                acc_ref[...] += part

            @pl.when(k == nk - 1)
            def _():
                o_ref[...] = acc_ref[...].astype(out_dtype)

    a_spec = pl.BlockSpec((tk, tm), lambda i, j, k: (k, i)) if ta else pl.BlockSpec((tm, tk), lambda i, j, k: (i, k))
    b_spec = pl.BlockSpec((tn, tk), lambda i, j, k: (j, k)) if tb else pl.BlockSpec((tk, tn), lambda i, j, k: (k, j))
    return pl.pallas_call(
        body, name=name, grid=(M // tm, N // tn, nk),
        in_specs=[a_spec, b_spec],
        out_specs=pl.BlockSpec((tm, tn), lambda i, j, k: (i, j)),
        out_shape=jax.ShapeDtypeStruct((M, N), out_dtype),
        scratch_shapes=[] if nk == 1 else [pltpu.VMEM((tm, tn), F32)],
        compiler_params=_params("parallel", "parallel", "arbitrary"),
    )(a, b)


def _row(tl, n, col=0):
    return pl.BlockSpec((tl, n), lambda i: (i, col))


def _full(shape):
    return pl.BlockSpec(shape, lambda i: (0,) * len(shape))


def _rms(x, g):
    r = lax.rsqrt(jnp.mean(x * x, axis=-1, keepdims=True) + EPS)
    return x * r * g


def _rms_bwd(x, g, dy):
    n = x.shape[-1]
    r = lax.rsqrt(jnp.mean(x * x, axis=-1, keepdims=True) + EPS)
    gy = dy * g
    dx = r * gy - x * (r * r * r * (1.0 / n)) * jnp.sum(x * gy, axis=-1, keepdims=True)
    return dx, jnp.sum(dy * x * r, axis=0, keepdims=True)


def _acc(ref, first, val):
    @pl.when(first)
    def _():
        ref[...] = val

    @pl.when(jnp.logical_not(first))
    def _():
        ref[...] += val


def _rms_fwd_call(x, g, name):
    L, n = x.shape
    tl = _fit(L, 512)

    def body(x_ref, g_ref, o_ref):
        o_ref[...] = _rms(x_ref[...], g_ref[...]).astype(BF16)

    return pl.pallas_call(
        body, name=name, grid=(L // tl,), in_specs=[_row(tl, n), _full((1, n))], out_specs=_row(tl, n),
        out_shape=jax.ShapeDtypeStruct((L, n), BF16), compiler_params=_params("parallel"))(x, g)


def _mla_norms_call(proj, q_norm, kv_norm):
    L = proj.shape[0]
    tl = _fit(L, 512)

    def body(p_ref, gq_ref, gk_ref, qn_ref, kn_ref):
        p = p_ref[...]
        qn_ref[...] = _rms(p[:, P_CQ:P_CKV], gq_ref[...]).astype(BF16)
        kn_ref[...] = _rms(p[:, P_CKV:P_KR], gk_ref[...]).astype(BF16)

    return pl.pallas_call(
        body, name="mla_norms", grid=(L // tl,),
        in_specs=[_row(tl, P_KR), _full((1, Q_RANK)), _full((1, KV_RANK))],
        out_specs=[_row(tl, Q_RANK), _row(tl, KV_RANK)],
        out_shape=[jax.ShapeDtypeStruct((L, Q_RANK), BF16), jax.ShapeDtypeStruct((L, KV_RANK), BF16)],
        compiler_params=_params("parallel"))(proj, q_norm, kv_norm)


def _rope_lanes(shape):
    lane = lax.broadcasted_iota(jnp.int32, shape, 1)
    return lane, jnp.logical_and(lane >= QK_NOPE, lane < QK_HEAD)


def _rope_apply(x, cosf, sinf, lane):
    rot = jnp.where(lane < QK_NOPE + QK_ROPE // 2, -pltpu.roll(x, LANES - QK_ROPE // 2, 1), pltpu.roll(x, QK_ROPE // 2, 1))
    return x * cosf + rot * sinf


def _rope_apply_t(dy, cosf, sinf, lane, is_rope):
    g = dy * sinf
    rot_t = jnp.where(lane < QK_NOPE + QK_ROPE // 2, pltpu.roll(g, LANES - QK_ROPE // 2, 1), -pltpu.roll(g, QK_ROPE // 2, 1))
    return dy * cosf + jnp.where(is_rope, rot_t, 0.0)


def _mla_prep_call(q_pad, kv_pad, proj, pos_col, inv_freq):
    L = q_pad.shape[0]
    tl = _fit(L, 512)

    def body(q_ref, kv_ref, kr_ref, pos_ref, f_ref, qo_ref, kvo_ref, cos_ref, sin_ref):
        lane, is_rope = _rope_lanes((tl, LANES))
        ang = pos_ref[...] * f_ref[...]
        cosf = jnp.where(is_rope, jnp.cos(ang), jnp.where(lane < QK_NOPE, 1.0, 0.0))
        sinf = jnp.where(is_rope, jnp.sin(ang), 0.0)
        cos_ref[...] = cosf
        sin_ref[...] = sinf
        kr = _rope_apply(kr_ref[...], cosf, sinf, lane)
        for h in range(N_HEADS):
            qo_ref[:, h * LANES:(h + 1) * LANES] = _rope_apply(q_ref[:, h * LANES:(h + 1) * LANES], cosf, sinf, lane).astype(BF16)
            kvo_ref[:, 2 * h * LANES:(2 * h + 1) * LANES] = (kv_ref[:, 2 * h * LANES:(2 * h + 1) * LANES] + kr).astype(BF16)
            kvo_ref[:, (2 * h + 1) * LANES:(2 * h + 2) * LANES] = kv_ref[:, (2 * h + 1) * LANES:(2 * h + 2) * LANES].astype(BF16)

    return pl.pallas_call(
        body, name="mla_prep", grid=(L // tl,),
        in_specs=[_row(tl, HEAD_PAD), _row(tl, 2 * HEAD_PAD), _row(tl, LANES, P_KR // LANES), _row(tl, 1), _full((1, LANES))],
        out_specs=[_row(tl, HEAD_PAD), _row(tl, 2 * HEAD_PAD), _row(tl, LANES), _row(tl, LANES)],
        out_shape=[jax.ShapeDtypeStruct((L, HEAD_PAD), BF16), jax.ShapeDtypeStruct((L, 2 * HEAD_PAD), BF16),
                   jax.ShapeDtypeStruct((L, LANES), F32), jax.ShapeDtypeStruct((L, LANES), F32)],
        compiler_params=_params("parallel"))(q_pad, kv_pad, proj, pos_col, inv_freq)


def _mla_prep_bwd_call(dq, dkv, cosf, sinf):
    L = dq.shape[0]
    tl = _fit(L, 512)

    def body(dq_ref, dkv_ref, cos_ref, sin_ref, dqo_ref, dkvo_ref, dkr_ref):
        lane, is_rope = _rope_lanes((tl, LANES))
        cosf, sinf = cos_ref[...], sin_ref[...]
        dk_sum = jnp.zeros((tl, LANES), F32)
        for h in range(N_HEADS):
            dqo_ref[:, h * LANES:(h + 1) * LANES] = _rope_apply_t(dq_ref[:, h * LANES:(h + 1) * LANES], cosf, sinf, lane, is_rope).astype(BF16)
            dk_sum = dk_sum + dkv_ref[:, 2 * h * LANES:(2 * h + 1) * LANES]
        dkvo_ref[...] = dkv_ref[...].astype(BF16)
        dkr_ref[...] = _rope_apply_t(dk_sum, cosf, sinf, lane, is_rope).astype(BF16)

    return pl.pallas_call(
        body, name="mla_prep_bwd", grid=(L // tl,),
        in_specs=[_row(tl, HEAD_PAD), _row(tl, 2 * HEAD_PAD), _row(tl, LANES), _row(tl, LANES)],
        out_specs=[_row(tl, HEAD_PAD), _row(tl, 2 * HEAD_PAD), _row(tl, LANES)],
        out_shape=[jax.ShapeDtypeStruct((L, HEAD_PAD), BF16), jax.ShapeDtypeStruct((L, 2 * HEAD_PAD), BF16),
                   jax.ShapeDtypeStruct((L, LANES), BF16)],
        compiler_params=_params("parallel"))(dq, dkv, cosf, sinf)


def _mla_norms_bwd_call(proj, dqn, dkn, q_norm, kv_norm):
    L = proj.shape[0]
    tl = _fit(L, 512)

    def body(p_ref, dqn_ref, dkn_ref, gq_ref, gk_ref, d_ref, dgq_ref, dgk_ref):
        first = pl.program_id(0) == 0
        p = p_ref[...]
        dq, dgq = _rms_bwd(p[:, P_CQ:P_CKV], gq_ref[...], dqn_ref[...])
        dk, dgk = _rms_bwd(p[:, P_CKV:P_KR], gk_ref[...], dkn_ref[...])
        d_ref[:, P_CQ:P_CKV] = dq.astype(BF16)
        d_ref[:, P_CKV:P_KR] = dk.astype(BF16)
        _acc(dgq_ref, first, dgq)
        _acc(dgk_ref, first, dgk)

    return pl.pallas_call(
        body, name="mla_norms_bwd", grid=(L // tl,),
        in_specs=[_row(tl, P_KR), _row(tl, Q_RANK), _row(tl, KV_RANK), _full((1, Q_RANK)), _full((1, KV_RANK))],
        out_specs=[_row(tl, P_KR), _full((1, Q_RANK)), _full((1, KV_RANK))],
        out_shape=[jax.ShapeDtypeStruct((L, P_KR), BF16), jax.ShapeDtypeStruct((1, Q_RANK), F32),
                   jax.ShapeDtypeStruct((1, KV_RANK), F32)],
        compiler_params=_params("arbitrary"))(proj, dqn, dkn, q_norm, kv_norm)


GATE_TILE = 256


def _merge_call(proj, b_gate, pa, ps):
    L = proj.shape[0]
    tl = _fit(L, 512)
    nc = D_MODEL // GATE_TILE
    g0, g1 = P_GATE // GATE_TILE, (P_GATE + D_MODEL) // GATE_TILE

    def body(l0_ref, l1_ref, b0_ref, b1_ref, pa_ref, ps_ref, o_ref):
        s0 = _sigmoid(l0_ref[...] + b0_ref[...])
        s1 = _sigmoid(l1_ref[...] + b1_ref[...])
        o_ref[...] = (s0 * pa_ref[...] + s1 * ps_ref[...]).astype(BF16)

    blk = lambda off: pl.BlockSpec((tl, GATE_TILE), lambda i, j: (i, off + j))
    bias = lambda off: pl.BlockSpec((1, GATE_TILE), lambda i, j: (0, off + j))
    return pl.pallas_call(
        body, name="merge", grid=(L // tl, nc),
        in_specs=[blk(g0), blk(g1), bias(0), bias(nc), blk(0), blk(0)],
        out_specs=blk(0), out_shape=jax.ShapeDtypeStruct((L, D_MODEL), BF16),
        compiler_params=_params("parallel", "parallel"))(proj, proj, b_gate, b_gate, pa, ps)


def _merge_bwd_call(dm, proj, b_gate, pa, ps):
    L = proj.shape[0]
    tl = _fit(L, 512)
    nc = D_MODEL // GATE_TILE
    g0, g1 = P_GATE // GATE_TILE, (P_GATE + D_MODEL) // GATE_TILE

    def body(dm_ref, l0_ref, l1_ref, b0_ref, b1_ref, pa_ref, ps_ref, dpa_ref, dps_ref, dl0_ref, dl1_ref, db0_ref, db1_ref):
        first = pl.program_id(1) == 0
        dm_ = dm_ref[...]
        s0 = _sigmoid(l0_ref[...] + b0_ref[...])
        s1 = _sigmoid(l1_ref[...] + b1_ref[...])
        dpa_ref[...] = (dm_ * s0).astype(BF16)
        dps_ref[...] = (dm_ * s1).astype(BF16)
        dl0 = dm_ * pa_ref[...] * s0 * (1.0 - s0)
        dl1 = dm_ * ps_ref[...] * s1 * (1.0 - s1)
        dl0_ref[...] = dl0.astype(BF16)
        dl1_ref[...] = dl1.astype(BF16)
        _acc(db0_ref, first, jnp.sum(dl0, axis=0, keepdims=True))
        _acc(db1_ref, first, jnp.sum(dl1, axis=0, keepdims=True))

    blk = lambda off: pl.BlockSpec((tl, GATE_TILE), lambda j, i: (i, off + j))
    bias = lambda off: pl.BlockSpec((1, GATE_TILE), lambda j, i: (0, off + j))
    act = jax.ShapeDtypeStruct((L, D_MODEL), BF16)
    vec = jax.ShapeDtypeStruct((1, D_MODEL), F32)
    return pl.pallas_call(
        body, name="merge_bwd", grid=(nc, L // tl),
        in_specs=[blk(0), blk(g0), blk(g1), bias(0), bias(nc), blk(0), blk(0)],
        out_specs=[blk(0), blk(0), blk(0), blk(0), bias(0), bias(0)],
        out_shape=[act, act, act, act, vec, vec],
        compiler_params=_params("parallel", "arbitrary"))(dm, proj, proj, b_gate, b_gate, pa, ps)


def _post_mix_call(o, x, g_post, g_fpre):
    L, n = x.shape
    tl = _fit(L, 512)

    def body(o_ref, x_ref, gp_ref, gf_ref, x2_ref, hn_ref):
        x2 = x_ref[...] + _rms(o_ref[...], gp_ref[...])
        x2_ref[...] = x2
        hn_ref[...] = _rms(x2, gf_ref[...]).astype(BF16)

    return pl.pallas_call(
        body, name="post_mix", grid=(L // tl,),
        in_specs=[_row(tl, n), _row(tl, n), _full((1, n)), _full((1, n))],
        out_specs=[_row(tl, n), _row(tl, n)],
        out_shape=[jax.ShapeDtypeStruct((L, n), F32), jax.ShapeDtypeStruct((L, n), BF16)],
        compiler_params=_params("parallel"))(o, x, g_post, g_fpre)


def _ffn_out_call(ff, x2, target, g_fpost):
    L, n = x2.shape
    tl = _fit(L, 512)

    def body(ff_ref, x2_ref, t_ref, g_ref, loss_ref, dy_ref, dff_ref, dg_ref):
        first = pl.program_id(0) == 0
        ff_ = ff_ref[...]
        err = x2_ref[...] + _rms(ff_, g_ref[...]) - t_ref[...]
        part = 0.5 * jnp.sum(jnp.sum(err * err, axis=-1, keepdims=True) * (1.0 / n), axis=0, keepdims=True)
        dy = err * (1.0 / n)
        dy_ref[...] = dy
        dff, dg = _rms_bwd(ff_, g_ref[...], dy)
        dff_ref[...] = dff.astype(BF16)
        _acc(loss_ref, first, jnp.broadcast_to(part, (1, LANES)))
        _acc(dg_ref, first, dg)

    return pl.pallas_call(
        body, name="ffn_out", grid=(L // tl,),
        in_specs=[_row(tl, n), _row(tl, n), _row(tl, n), _full((1, n))],
        out_specs=[_full((1, LANES)), _row(tl, n), _row(tl, n), _full((1, n))],
        out_shape=[jax.ShapeDtypeStruct((1, LANES), F32), jax.ShapeDtypeStruct((L, n), F32),
                   jax.ShapeDtypeStruct((L, n), BF16), jax.ShapeDtypeStruct((1, n), F32)],
        compiler_params=_params("arbitrary"))(ff, x2, target, g_fpost)


def _post_bwd_call(x2, dhn2, dy, o, g_fpre, g_post):
    L, n = x2.shape
    tl = _fit(L, 512)

    def body(x2_ref, dh_ref, dy_ref, o_ref, gf_ref, gp_ref, dx2_ref, do_ref, dgf_ref, dgp_ref):
        first = pl.program_id(0) == 0
        d1, dgf = _rms_bwd(x2_ref[...], gf_ref[...], dh_ref[...])
        dx2 = dy_ref[...] + d1
        dx2_ref[...] = dx2
        do, dgp = _rms_bwd(o_ref[...], gp_ref[...], dx2)
        do_ref[...] = do.astype(BF16)
        _acc(dgf_ref, first, dgf)
        _acc(dgp_ref, first, dgp)

    return pl.pallas_call(
        body, name="post_bwd", grid=(L // tl,),
        in_specs=[_row(tl, n), _row(tl, n), _row(tl, n), _row(tl, n), _full((1, n)), _full((1, n))],
        out_specs=[_row(tl, n), _row(tl, n), _full((1, n)), _full((1, n))],
        out_shape=[jax.ShapeDtypeStruct((L, n), F32), jax.ShapeDtypeStruct((L, n), BF16),
                   jax.ShapeDtypeStruct((1, n), F32), jax.ShapeDtypeStruct((1, n), F32)],
        compiler_params=_params("arbitrary"))(x2, dhn2, dy, o, g_fpre, g_post)


def _pre_bwd_call(x, dhn1, dx2, g_pre):
    L, n = x.shape
    tl = _fit(L, 512)

    def body(x_ref, dh_ref, dx2_ref, g_ref, dx_ref, dg_ref):
        first = pl.program_id(0) == 0
        d1, dg = _rms_bwd(x_ref[...], g_ref[...], dh_ref[...])
        dx_ref[...] = dx2_ref[...] + d1
        _acc(dg_ref, first, dg)

    return pl.pallas_call(
        body, name="pre_bwd", grid=(L // tl,),
        in_specs=[_row(tl, n), _row(tl, n), _row(tl, n), _full((1, n))],
        out_specs=[_row(tl, n), _full((1, n))],
        out_shape=[jax.ShapeDtypeStruct((L, n), F32), jax.ShapeDtypeStruct((1, n), F32)],
        compiler_params=_params("arbitrary"))(x, dhn1, dx2, g_pre)


CONV_TILE = 256
HALO = 16


def _shift_down(cur, halo_tail, by):
    rolled = pltpu.roll(cur, by, 0)
    r8 = lax.broadcasted_iota(jnp.int32, halo_tail.shape, 0)
    head = jnp.where(r8 < by, pltpu.roll(halo_tail, by, 0), rolled[0:8])
    return jnp.concatenate([head, rolled[8:]], axis=0)


def _shift_up(cur, halo_head, by):
    n = cur.shape[0]
    rolled = pltpu.roll(cur, n - by, 0)
    r8 = lax.broadcasted_iota(jnp.int32, halo_head.shape, 0)
    tail = jnp.where(r8 >= 8 - by, pltpu.roll(halo_head, 8 - by, 0), rolled[n - 8:])
    return jnp.concatenate([rolled[:n - 8], tail], axis=0)


def _conv_fwd_vals(cur, halo, w, b, not_first):
    tail = halo[HALO - 8:] * not_first
    s1 = _shift_down(cur, tail, 1)
    s2 = _shift_down(cur, tail, 2)
    return b + w[2:3] * cur + w[1:2] * s1 + w[0:1] * s2, s1, s2


def _conv_specs(L, tl, nc, rows_inner):
    nh = tl // HALO
    if rows_inner:
        ij = lambda f: (lambda j, i: f(i, j))
    else:
        ij = lambda f: f
    cur = lambda off: pl.BlockSpec((tl, CONV_TILE), ij(lambda i, j: (i, off + j)))
    prev = lambda off: pl.BlockSpec((HALO, CONV_TILE), ij(lambda i, j: (jnp.maximum(i * nh - 1, 0), off + j)))
    nxt = lambda off: pl.BlockSpec((HALO, CONV_TILE), ij(lambda i, j: (jnp.minimum((i + 1) * nh, L // HALO - 1), off + j)))
    par = lambda rows, off: pl.BlockSpec((rows, CONV_TILE), ij(lambda i, j: (0, off + j)))
    return cur, prev, nxt, par


def _conv_act_call(h, conv_w, conv_b):
    L = h.shape[0]
    tl = _fit(L, 512)
    nc = D_FF // CONV_TILE
    cur, prev, _, par = _conv_specs(L, tl, nc, False)

    def body(hg_ref, hv_ref, pg_ref, pv_ref, wg_ref, wv_ref, bg_ref, bv_ref, a_ref):
        not_first = (pl.program_id(0) > 0).astype(F32)
        gate, _, _ = _conv_fwd_vals(hg_ref[...], pg_ref[...], wg_ref[...], bg_ref[...], not_first)
        val, _, _ = _conv_fwd_vals(hv_ref[...], pv_ref[...], wv_ref[...], bv_ref[...], not_first)
        a_ref[...] = (_gelu(gate) * val).astype(BF16)

    return pl.pallas_call(
        body, name="conv_act", grid=(L // tl, nc),
        in_specs=[cur(0), cur(nc), prev(0), prev(nc), par(3, 0), par(3, nc), par(1, 0), par(1, nc)],
        out_specs=cur(0), out_shape=jax.ShapeDtypeStruct((L, D_FF), BF16),
        compiler_params=_params("parallel", "parallel"))(h, h, h, h, conv_w, conv_w, conv_b, conv_b)


def _conv_act_bwd_call(da, h, conv_w, conv_b):
    L = h.shape[0]
    tl = _fit(L, 512)
    nc = D_FF // CONV_TILE
    cur, prev, _, par = _conv_specs(L, tl, nc, True)

    def body(da_ref, hg_ref, hv_ref, pg_ref, pv_ref, wg_ref, wv_ref, bg_ref, bv_ref,
             dg_ref, dv_ref, dwg_ref, dwv_ref, dbg_ref, dbv_ref):
        first = pl.program_id(1) == 0
        not_first = (pl.program_id(1) > 0).astype(F32)
        hg, hv = hg_ref[...], hv_ref[...]
        gate, g1, g2 = _conv_fwd_vals(hg, pg_ref[...], wg_ref[...], bg_ref[...], not_first)
        val, v1, v2 = _conv_fwd_vals(hv, pv_ref[...], wv_ref[...], bv_ref[...], not_first)
        da_ = da_ref[...]
        dgate = da_ * val * _gelu_grad(gate)
        dval = da_ * _gelu(gate)
        dg_ref[...] = dgate.astype(BF16)
        dv_ref[...] = dval.astype(BF16)
        col = lambda t: jnp.sum(t, axis=0, keepdims=True)
        _acc(dwg_ref, first, jnp.concatenate([col(dgate * g2), col(dgate * g1), col(dgate * hg)], axis=0))
        _acc(dwv_ref, first, jnp.concatenate([col(dval * v2), col(dval * v1), col(dval * hv)], axis=0))
        _acc(dbg_ref, first, col(dgate))
        _acc(dbv_ref, first, col(dval))

    act = jax.ShapeDtypeStruct((L, D_FF), BF16)
    w3 = jax.ShapeDtypeStruct((3, D_FF), F32)
    w1 = jax.ShapeDtypeStruct((1, D_FF), F32)
    return pl.pallas_call(
        body, name="conv_act_bwd", grid=(nc, L // tl),
        in_specs=[cur(0), cur(0), cur(nc), prev(0), prev(nc), par(3, 0), par(3, nc), par(1, 0), par(1, nc)],
        out_specs=[cur(0), cur(0), par(3, 0), par(3, 0), par(1, 0), par(1, 0)],
        out_shape=[act, act, w3, w3, w1, w1],
        compiler_params=_params("parallel", "arbitrary"))(da, h, h, h, h, conv_w, conv_w, conv_b, conv_b)


def _conv_t_call(dgate, dval, conv_w):
    L = dgate.shape[0]
    tl = _fit(L, 512)
    nc = D_FF // CONV_TILE
    cur, _, nxt, par = _conv_specs(L, tl, nc, False)

    def run(d, off, name):
        def body(d_ref, n_ref, w_ref, o_ref):
            not_last = (pl.program_id(0) < L // tl - 1).astype(F32)
            c = d_ref[...].astype(F32)
            head = n_ref[...].astype(F32)[0:8] * not_last
            w = w_ref[...]
            o_ref[...] = (w[2:3] * c + w[1:2] * _shift_up(c, head, 1) + w[0:1] * _shift_up(c, head, 2)).astype(BF16)

        return pl.pallas_call(
            body, name=name, grid=(L // tl, nc),
            in_specs=[cur(0), nxt(0), par(3, off)],
            out_specs=cur(0), out_shape=jax.ShapeDtypeStruct((L, D_FF), BF16),
            compiler_params=_params("parallel", "parallel"))(d, d, conv_w)

    return run(dgate, 0, "conv_t_gate"), run(dval, nc, "conv_t_val")


def _glu_call(y1, w_glu, b_glu):
    L, n = y1.shape
    tl = _fit(L, 512)

    def body(y_ref, w_ref, b_ref, o_ref):
        y2 = _gelu(y_ref[...])
        z = _dot(y2.astype(BF16), w_ref[...], NN) + b_ref[...]
        o_ref[...] = (y2 * _sigmoid(z)).astype(BF16)

    return pl.pallas_call(
        body, name="glu", grid=(L // tl,), in_specs=[_row(tl, n), _full((n, n)), _full((1, n))],
        out_specs=_row(tl, n), out_shape=jax.ShapeDtypeStruct((L, n), BF16),
        compiler_params=_params("parallel"))(y1, w_glu, b_glu)


def _glu_bwd_call(dout, y1, w_glu, b_glu):
    L, n = y1.shape
    tl = _fit(L, 512)

    def body(do_ref, y_ref, w_ref, b_ref, dy_ref, dw_ref, db_ref):
        first = pl.program_id(0) == 0
        y1_ = y_ref[...]
        y2 = _gelu(y1_)
        y2b = y2.astype(BF16)
        w = w_ref[...]
        sg = _sigmoid(_dot(y2b, w, NN) + b_ref[...])
        dout_ = do_ref[...].astype(F32)
        dz = dout_ * y2 * sg * (1.0 - sg)
        dzb = dz.astype(BF16)
        dy2 = dout_ * sg + _dot(dzb, w, NT)
        dy_ref[...] = dy2 * _gelu_grad(y1_)
        _acc(dw_ref, first, _dot(y2b, dzb, TN))
        _acc(db_ref, first, jnp.sum(dz, axis=0, keepdims=True))

    return pl.pallas_call(
        body, name="glu_bwd", grid=(L // tl,),
        in_specs=[_row(tl, n), _row(tl, n), _full((n, n)), _full((1, n))],
        out_specs=[_row(tl, n), _full((n, n)), _full((1, n))],
        out_shape=[jax.ShapeDtypeStruct((L, n), F32), jax.ShapeDtypeStruct((n, n), F32), jax.ShapeDtypeStruct((1, n), F32)],
        compiler_params=_params("arbitrary"))(dout, y1, w_glu, b_glu)


ATTN_TILE = 512
ATTN_SCALE = 1.0 / math.sqrt(QK_HEAD)


def _attn_fwd_call(q, kv):
    L = q.shape[0]
    t = _fit(L, ATTN_TILE)
    nq = L // t

    def body(q_ref, k_ref, v_ref, o_ref, lse_ref, m_s, l_s, acc_s):
        i = pl.program_id(1)
        m_s[...] = jnp.full((t, 1), NEG, F32)
        l_s[...] = jnp.zeros((t, 1), F32)
        acc_s[...] = jnp.zeros((t, LANES), F32)
        qv = q_ref[...]
        row = i * t + lax.broadcasted_iota(jnp.int32, (t, t), 0)
        col0 = lax.broadcasted_iota(jnp.int32, (t, t), 1)

        def step(kb, carry):
            off = pl.multiple_of(kb * t, t)
            s = _dot(qv, k_ref[pl.ds(off, t), :], NT) * ATTN_SCALE
            s = jnp.where(col0 + kb * t <= row, s, NEG)
            m_prev = m_s[...]
            m_new = jnp.maximum(m_prev, jnp.max(s, axis=1, keepdims=True))
            alpha = jnp.exp(m_prev - m_new)
            p = jnp.exp(s - m_new)
            l_s[...] = alpha * l_s[...] + jnp.sum(p, axis=1, keepdims=True)
            acc_s[...] = alpha * acc_s[...] + _dot(p.astype(BF16), v_ref[pl.ds(off, t), :], NN)
            m_s[...] = m_new
            return carry

        lax.fori_loop(0, i + 1, step, 0)
        o_ref[...] = (acc_s[...] / l_s[...]).astype(BF16)
        lse_ref[0] = m_s[...] + jnp.log(l_s[...])

    return pl.pallas_call(
        body, name="attn_fwd", grid=(N_HEADS, nq),
        in_specs=[pl.BlockSpec((t, LANES), lambda h, i: (i, h)),
                  pl.BlockSpec((L, LANES), lambda h, i: (0, 2 * h)),
                  pl.BlockSpec((L, LANES), lambda h, i: (0, 2 * h + 1))],
        out_specs=[pl.BlockSpec((t, LANES), lambda h, i: (i, h)),
                   pl.BlockSpec((1, t, 1), lambda h, i: (h, i, 0))],
        out_shape=[jax.ShapeDtypeStruct((L, HEAD_PAD), BF16), jax.ShapeDtypeStruct((N_HEADS, L, 1), F32)],
        scratch_shapes=[pltpu.VMEM((t, 1), F32), pltpu.VMEM((t, 1), F32), pltpu.VMEM((t, LANES), F32)],
        compiler_params=_params("parallel", "parallel"))(q, kv, kv)


def _attn_bwd_call(q, kv, o, do, lse):
    L = q.shape[0]
    t = _fit(L, ATTN_TILE)
    nq = L // t

    def body(q_ref, do_ref, o_ref, lse_ref, k_ref, v_ref, dq_ref, dkv_ref, dk_s, dv_s):
        j = pl.program_id(1)

        @pl.when(j == 0)
        def _():
            dq_ref[...] = jnp.zeros((L, LANES), F32)

        dk_s[...] = jnp.zeros((t, LANES), F32)
        dv_s[...] = jnp.zeros((t, LANES), F32)
        kblk, vblk = k_ref[...], v_ref[...]
        col = j * t + lax.broadcasted_iota(jnp.int32, (t, t), 1)
        row0 = lax.broadcasted_iota(jnp.int32, (t, t), 0)

        def step(i, carry):
            off = pl.multiple_of(i * t, t)
            qi = q_ref[pl.ds(off, t), :]
            doi = do_ref[pl.ds(off, t), :]
            delta = jnp.sum(doi.astype(F32) * o_ref[pl.ds(off, t), :].astype(F32), axis=1, keepdims=True)
            s = _dot(qi, kblk, NT) * ATTN_SCALE
            s = jnp.where(col <= row0 + i * t, s, NEG)
            p = jnp.exp(s - lse_ref[0, pl.ds(off, t), :])
            dv_s[...] += _dot(p.astype(BF16), doi, TN)
            dp = _dot(doi, vblk, NT)
            ds = (p * (dp - delta) * ATTN_SCALE).astype(BF16)
            dk_s[...] += _dot(ds, qi, TN)
            dq_ref[pl.ds(off, t), :] += _dot(ds, kblk, NN)
            return carry

        lax.fori_loop(j, nq, step, 0)
        dkv_ref[:, 0:LANES] = dk_s[...]
        dkv_ref[:, LANES:2 * LANES] = dv_s[...]

    whole = lambda: pl.BlockSpec((L, LANES), lambda h, j: (0, h))
    return pl.pallas_call(
        body, name="attn_bwd", grid=(N_HEADS, nq),
        in_specs=[whole(), whole(), whole(), pl.BlockSpec((1, L, 1), lambda h, j: (h, 0, 0)),
                  pl.BlockSpec((t, LANES), lambda h, j: (j, 2 * h)),
                  pl.BlockSpec((t, LANES), lambda h, j: (j, 2 * h + 1))],
        out_specs=[whole(), pl.BlockSpec((t, 2 * LANES), lambda h, j: (j, h))],
        out_shape=[jax.ShapeDtypeStruct((L, HEAD_PAD), F32), jax.ShapeDtypeStruct((L, 2 * HEAD_PAD), F32)],
        scratch_shapes=[pltpu.VMEM((t, LANES), F32), pltpu.VMEM((t, LANES), F32)],
        compiler_params=_params("parallel", "arbitrary"))(q, do, o, lse, kv, kv)


def _disc(lr, li, ldt, br, bi):
    dt = jnp.exp(ldt)
    mag = jnp.exp(lr * dt)
    ang = li * dt
    a_re, a_im = mag * jnp.cos(ang), mag * jnp.sin(ang)
    den = lr * lr + li * li
    n_re, n_im = a_re - 1.0, a_im
    z_re = (n_re * lr + n_im * li) / den
    z_im = (n_im * lr - n_re * li) / den
    return a_re, a_im, z_re * br - z_im * bi, z_re * bi + z_im * br


def _disc_call(lr, li, ldt, br, bi):
    def body(lr_ref, li_ref, ldt_ref, br_ref, bi_ref, ar_ref, ai_ref, bbr_ref, bbi_ref):
        ar_ref[...], ai_ref[...], bbr_ref[...], bbi_ref[...] = _disc(
            lr_ref[...], li_ref[...], ldt_ref[...], br_ref[...], bi_ref[...])

    c1 = jax.ShapeDtypeStruct((SSM_NSTATE, 1), F32)
    c16 = jax.ShapeDtypeStruct((SSM_NSTATE, SSM_GROUP), F32)
    return pl.pallas_call(body, name="ssm_disc", out_shape=[c1, c1, c16, c16])(lr, li, ldt, br, bi)


def _disc_bwd_call(lr, li, ldt, br, bi, dar, dai, dbbr, dbbi):
    def body(lr_ref, li_ref, ldt_ref, br_ref, bi_ref, dar_ref, dai_ref, dbbr_ref, dbbi_ref,
             dlr_ref, dli_ref, dldt_ref, dbr_ref, dbi_ref):
        _, vjp = jax.vjp(_disc, lr_ref[...], li_ref[...], ldt_ref[...], br_ref[...], bi_ref[...])
        dlr_ref[...], dli_ref[...], dldt_ref[...], dbr_ref[...], dbi_ref[...] = vjp(
            (dar_ref[...], dai_ref[...], dbbr_ref[...], dbbi_ref[...]))

    c1 = jax.ShapeDtypeStruct((SSM_NSTATE, 1), F32)
    c16 = jax.ShapeDtypeStruct((SSM_NSTATE, SSM_GROUP), F32)
    return pl.pallas_call(body, name="ssm_disc_bwd", out_shape=[c1, c1, c1, c16, c16])(
        lr, li, ldt, br, bi, dar, dai, dbbr, dbbi)


SSM_ROWS = 512
SSM_CW = SSM_NSTATE // SSM_CHUNKS
SSM_CU = SSM_WIDTH // SSM_CHUNKS


def _cmul(ar, ai, br, bi):
    return ar * br - ai * bi, ar * bi + ai * br


def _power(ar1, ai1, n):
    def step(_, c):
        return _cmul(c[0], c[1], ar1, ai1)

    return lax.fori_loop(0, n, step, (jnp.ones_like(ar1), jnp.zeros_like(ar1)))


def _tile(k):
    return pl.ds(pl.multiple_of(k * 8, 8), 8)


def _ssm_fwd_call(u, a_re, a_im, bb_re, bb_im, cm_re, cm_im, d_skip):
    L = u.shape[0]
    seg = L // 8
    rb = _fit(L, SSM_ROWS)

    def body(u_ref, ar_ref, ai_ref, bbr_ref, bbi_ref, cmr_ref, cmi_ref, d_ref, y_ref, sre_hbm, sim_hbm,
             s_re, s_im, sems):
        q = pl.program_id(0)

        def bu_step(r, c):
            rows = pl.ds(pl.multiple_of(r * rb, rb), rb)
            ub = u_ref[rows, :].astype(BF16)
            s_re[rows, :] = _dot(ub, bbr_ref[0], NN)
            s_im[rows, :] = _dot(ub, bbi_ref[0], NN)
            return c

        lax.fori_loop(0, L // rb, bu_step, 0)
        ar1, ai1 = ar_ref[...], ai_ref[...]
        ar = jnp.broadcast_to(ar1, (8, SSM_CW))
        ai = jnp.broadcast_to(ai1, (8, SSM_CW))

        def local(k, c):
            nr, ni = _cmul(ar, ai, c[0], c[1])
            nr = nr + s_re[_tile(k), :]
            ni = ni + s_im[_tile(k), :]
            s_re[_tile(k), :] = nr
            s_im[_tile(k), :] = ni
            return nr, ni

        zero8 = jnp.zeros((8, SSM_CW), F32)
        lax.fori_loop(0, seg, local, (zero8, zero8))
        pr, pi = _power(ar1, ai1, seg)
        end_r = s_re[pl.ds((seg - 1) * 8, 8), :]
        end_i = s_im[pl.ds((seg - 1) * 8, 8), :]
        er = jnp.zeros((1, SSM_CW), F32)
        ei = jnp.zeros((1, SSM_CW), F32)
        rows_r, rows_i = [er], [ei]
        for j in range(7):
            tr, ti = _cmul(pr, pi, er, ei)
            er, ei = end_r[j:j + 1] + tr, end_i[j:j + 1] + ti
            rows_r.append(er)
            rows_i.append(ei)
        e_r = jnp.concatenate(rows_r, axis=0)
        e_i = jnp.concatenate(rows_i, axis=0)

        def fix(k, c):
            wr, wi = _cmul(c[0], c[1], ar, ai)
            fr, fi = _cmul(wr, wi, e_r, e_i)
            s_re[_tile(k), :] += fr
            s_im[_tile(k), :] += fi
            return wr, wi

        lax.fori_loop(0, seg, fix, (jnp.ones((8, SSM_CW), F32), zero8))
        out_r = pltpu.make_async_copy(s_re, sre_hbm.at[q], sems.at[0])
        out_i = pltpu.make_async_copy(s_im, sim_hbm.at[q], sems.at[1])
        out_r.start()
        out_i.start()

        def y_step(r, c):
            rows = pl.ds(pl.multiple_of(r * rb, rb), rb)
            y = _dot(s_re[rows, :].astype(BF16), cmr_ref[0], NN) - _dot(s_im[rows, :].astype(BF16), cmi_ref[0], NN)
            y_ref[rows, :] = y + d_ref[...] * u_ref[rows, :]
            return c

        lax.fori_loop(0, L // rb, y_step, 0)
        out_r.wait()
        out_i.wait()

    chunk = lambda rows, cols: pl.BlockSpec((rows, cols), lambda q: (0, q))
    mat = lambda r, c: pl.BlockSpec((1, r, c), lambda q: (q, 0, 0))
    anyspec = pl.BlockSpec(memory_space=pl.ANY)
    states = jax.ShapeDtypeStruct((SSM_CHUNKS, L, SSM_CW), F32)
    return pl.pallas_call(
        body, name="ssm_fwd", grid=(SSM_CHUNKS,),
        in_specs=[chunk(L, SSM_CU), chunk(1, SSM_CW), chunk(1, SSM_CW), mat(SSM_CU, SSM_CW), mat(SSM_CU, SSM_CW),
                  mat(SSM_CW, SSM_CU), mat(SSM_CW, SSM_CU), chunk(1, SSM_CU)],
        out_specs=[chunk(L, SSM_CU), anyspec, anyspec],
        out_shape=[jax.ShapeDtypeStruct((L, SSM_WIDTH), F32), states, states],
        scratch_shapes=[pltpu.VMEM((L, SSM_CW), F32), pltpu.VMEM((L, SSM_CW), F32), pltpu.SemaphoreType.DMA((2,))],
        compiler_params=_params("arbitrary", vmem=VMEM_BIG))(u, a_re, a_im, bb_re, bb_im, cm_re, cm_im, d_skip)


def _ssm_bwd_call(dy, u, s_re_all, s_im_all, a_re, a_im, bb_re, bb_im, cm_re, cm_im, d_skip):
    L = u.shape[0]
    seg = L // 8
    rb = _fit(L, SSM_ROWS)

    def body(dy_ref, u_ref, sre_hbm, sim_hbm, ar_ref, ai_ref, bbr_ref, bbi_ref, cmr_ref, cmi_ref, d_ref,
             du_ref, dbbr_ref, dbbi_ref, dcmr_ref, dcmi_ref, dar_ref, dai_ref, dd_ref,
             g_re, g_im, s_re, s_im, sems):
        q = pl.program_id(0)
        in_r = pltpu.make_async_copy(sre_hbm.at[q], s_re, sems.at[0])
        in_i = pltpu.make_async_copy(sim_hbm.at[q], s_im, sems.at[1])
        in_r.start()
        in_i.start()

        def ds_step(r, c):
            rows = pl.ds(pl.multiple_of(r * rb, rb), rb)
            dyb = dy_ref[rows, :].astype(BF16)
            g_re[rows, :] = _dot(dyb, cmr_ref[0], NT)
            g_im[rows, :] = -_dot(dyb, cmi_ref[0], NT)
            return c

        lax.fori_loop(0, L // rb, ds_step, 0)
        ar1, ai1 = ar_ref[...], ai_ref[...]
        ar = jnp.broadcast_to(ar1, (8, SSM_CW))
        nai = jnp.broadcast_to(-ai1, (8, SSM_CW))

        def local(kk, c):
            k = seg - 1 - kk
            nr, ni = _cmul(ar, nai, c[0], c[1])
            nr = nr + g_re[_tile(k), :]
            ni = ni + g_im[_tile(k), :]
            g_re[_tile(k), :] = nr
            g_im[_tile(k), :] = ni
            return nr, ni

        zero8 = jnp.zeros((8, SSM_CW), F32)
        lax.fori_loop(0, seg, local, (zero8, zero8))
        pr, pi = _power(ar1, -ai1, seg)
        head_r = g_re[pl.ds(0, 8), :]
        head_i = g_im[pl.ds(0, 8), :]
        fr = jnp.zeros((1, SSM_CW), F32)
        fi = jnp.zeros((1, SSM_CW), F32)
        rows_r, rows_i = [fr], [fi]
        for j in range(6, -1, -1):
            tr, ti = _cmul(pr, pi, fr, fi)
            fr, fi = head_r[j + 1:j + 2] + tr, head_i[j + 1:j + 2] + ti
            rows_r.insert(0, fr)
            rows_i.insert(0, fi)
        f_r = jnp.concatenate(rows_r, axis=0)
        f_i = jnp.concatenate(rows_i, axis=0)
        in_r.wait()
        in_i.wait()

        def fixed(k, wr, wi):
            xr, xi = _cmul(wr, wi, f_r, f_i)
            gr = g_re[_tile(k), :] + xr
            gi = g_im[_tile(k), :] + xi
            g_re[_tile(k), :] = gr
            g_im[_tile(k), :] = gi
            return gr, gi

        def fix(kk, c):
            k = seg - 1 - kk
            wr, wi = _cmul(c[0], c[1], ar, nai)
            gr, gi = fixed(k, wr, wi)
            pr_, pi_ = s_re[_tile(k - 1), :], s_im[_tile(k - 1), :]
            return wr, wi, c[2] + gr * pr_ + gi * pi_, c[3] + gi * pr_ - gr * pi_

        wr, wi, acc_r, acc_i = lax.fori_loop(0, seg - 1, fix, (jnp.ones((8, SSM_CW), F32), zero8, zero8, zero8))
        wr, wi = _cmul(wr, wi, ar, nai)
        gr, gi = fixed(0, wr, wi)
        row8 = lax.broadcasted_iota(jnp.int32, (8, SSM_CW), 0)
        pr_ = jnp.where(row8 > 0, pltpu.roll(s_re[pl.ds((seg - 1) * 8, 8), :], 1, 0), 0.0)
        pi_ = jnp.where(row8 > 0, pltpu.roll(s_im[pl.ds((seg - 1) * 8, 8), :], 1, 0), 0.0)
        acc_r = acc_r + gr * pr_ + gi * pi_
        acc_i = acc_i + gi * pr_ - gr * pi_
        dar_ref[...] = jnp.sum(acc_r, axis=0, keepdims=True)
        dai_ref[...] = jnp.sum(acc_i, axis=0, keepdims=True)

        dbbr_ref[...] = jnp.zeros((1, SSM_CU, SSM_CW), F32)
        dbbi_ref[...] = jnp.zeros((1, SSM_CU, SSM_CW), F32)
        dcmr_ref[...] = jnp.zeros((1, SSM_CW, SSM_CU), F32)
        dcmi_ref[...] = jnp.zeros((1, SSM_CW, SSM_CU), F32)
        dd_ref[...] = jnp.zeros((1, SSM_CU), F32)

        def grad_step(r, c):
            rows = pl.ds(pl.multiple_of(r * rb, rb), rb)
            ub, dyv = u_ref[rows, :], dy_ref[rows, :]
            ubb, dyb = ub.astype(BF16), dyv.astype(BF16)
            grb, gib = g_re[rows, :].astype(BF16), g_im[rows, :].astype(BF16)
            dbbr_ref[0] += _dot(ubb, grb, TN)
            dbbi_ref[0] += _dot(ubb, gib, TN)
            dcmr_ref[0] += _dot(s_re[rows, :].astype(BF16), dyb, TN)
            dcmi_ref[0] -= _dot(s_im[rows, :].astype(BF16), dyb, TN)
            du_ref[rows, :] = _dot(grb, bbr_ref[0], NT) + _dot(gib, bbi_ref[0], NT) + d_ref[...] * dyv
            dd_ref[...] += jnp.sum(dyv * ub, axis=0, keepdims=True)
            return c

        lax.fori_loop(0, L // rb, grad_step, 0)

    chunk = lambda rows, cols: pl.BlockSpec((rows, cols), lambda q: (0, q))
    mat = lambda r, c: pl.BlockSpec((1, r, c), lambda q: (q, 0, 0))
    anyspec = pl.BlockSpec(memory_space=pl.ANY)
    big = lambda: pltpu.VMEM((L, SSM_CW), F32)
    return pl.pallas_call(
        body, name="ssm_bwd", grid=(SSM_CHUNKS,),
        in_specs=[chunk(L, SSM_CU), chunk(L, SSM_CU), anyspec, anyspec, chunk(1, SSM_CW), chunk(1, SSM_CW),
                  mat(SSM_CU, SSM_CW), mat(SSM_CU, SSM_CW), mat(SSM_CW, SSM_CU), mat(SSM_CW, SSM_CU), chunk(1, SSM_CU)],
        out_specs=[chunk(L, SSM_CU), mat(SSM_CU, SSM_CW), mat(SSM_CU, SSM_CW), mat(SSM_CW, SSM_CU), mat(SSM_CW, SSM_CU),
                   chunk(1, SSM_CW), chunk(1, SSM_CW), chunk(1, SSM_CU)],
        out_shape=[jax.ShapeDtypeStruct((L, SSM_WIDTH), F32),
                   jax.ShapeDtypeStruct((SSM_CHUNKS, SSM_CU, SSM_CW), F32), jax.ShapeDtypeStruct((SSM_CHUNKS, SSM_CU, SSM_CW), F32),
                   jax.ShapeDtypeStruct((SSM_CHUNKS, SSM_CW, SSM_CU), F32), jax.ShapeDtypeStruct((SSM_CHUNKS, SSM_CW, SSM_CU), F32),
                   jax.ShapeDtypeStruct((1, SSM_NSTATE), F32), jax.ShapeDtypeStruct((1, SSM_NSTATE), F32),
                   jax.ShapeDtypeStruct((1, SSM_WIDTH), F32)],
        scratch_shapes=[big(), big(), big(), big(), pltpu.SemaphoreType.DMA((2,))],
        compiler_params=_params("arbitrary", vmem=VMEM_BIG))(
            dy, u, s_re_all, s_im_all, a_re, a_im, bb_re, bb_im, cm_re, cm_im, d_skip)


def _place():
    return lax.axis_index("x"), lax.axis_index("y"), lax.axis_index("c")


def _all_gather_call(block, name):
    R = block.shape[0]

    def body(x_ref, out_ref, send_sems, recv_sems, local_sem):
        x, y, c = _place()
        me, sibling = (x, y, c), (x, y, 1 - c)
        chips = [(1 - x, y), (x, 1 - y), (1 - x, 1 - y)]

        def slot(px, py, pc):
            return out_ref.at[4 * px + 2 * py + pc]

        def copy(k, blk, to, src=None):
            return pltpu.make_async_remote_copy(
                src_ref=slot(*blk) if src is None else src, dst_ref=slot(*blk),
                send_sem=send_sems.at[k], recv_sem=recv_sems.at[k], device_id=to, device_id_type=MESH_ID)

        mine = pltpu.make_async_copy(x_ref, slot(*me), local_sem)
        mine.start()
        first = [copy(0, me, sibling, src=x_ref)]
        first += [copy(1 + j, me, (*chip, c), src=x_ref) for j, chip in enumerate(chips)]
        for cp in first:
            cp.start()
        passed = [copy(4 + j, (*chip, c), sibling) for j, chip in enumerate(chips)]
        for j, chip in enumerate(chips):
            copy(1 + j, (*chip, c), me).wait_recv()
            passed[j].start()
        copy(0, sibling, me).wait_recv()
        for j, chip in enumerate(chips):
            copy(4 + j, (*chip, 1 - c), me).wait_recv()
        for cp in first + passed:
            cp.wait_send()
        mine.wait()

    anyspec = pl.BlockSpec(memory_space=pl.ANY)
    return pl.pallas_call(
        body, name=name, in_specs=[anyspec], out_specs=anyspec,
        out_shape=jax.ShapeDtypeStruct((N_DEV,) + block.shape, block.dtype),
        scratch_shapes=[pltpu.SemaphoreType.DMA((7,)), pltpu.SemaphoreType.DMA((7,)), pltpu.SemaphoreType.DMA],
    )(block)


def _exchange_call(parts, name):
    def body(p_ref, out_ref, send_sems, recv_sems, local_sem):
        x, y, c = _place()
        me = 4 * x + 2 * y + c

        def flip(k):
            px = 1 - x if k & 4 else x
            py = 1 - y if k & 2 else y
            pc = 1 - c if k & 1 else c
            return (px, py, pc), 4 * px + 2 * py + pc

        def copy(k):
            peer, peer_slot = flip(k)
            return pltpu.make_async_remote_copy(
                src_ref=p_ref.at[peer_slot], dst_ref=out_ref.at[me],
                send_sem=send_sems.at[k - 1], recv_sem=recv_sems.at[k - 1], device_id=peer, device_id_type=MESH_ID)

        def arrival(k):
            peer, peer_slot = flip(k)
            return pltpu.make_async_remote_copy(
                src_ref=p_ref.at[peer_slot], dst_ref=out_ref.at[peer_slot],
                send_sem=send_sems.at[k - 1], recv_sem=recv_sems.at[k - 1], device_id=peer, device_id_type=MESH_ID)

        mine = pltpu.make_async_copy(p_ref.at[me], out_ref.at[me], local_sem)
        mine.start()
        sends = [copy(k) for k in range(1, N_DEV)]
        for cp in sends:
            cp.start()
        for k in range(1, N_DEV):
            arrival(k).wait_recv()
        for cp in sends:
            cp.wait_send()
        mine.wait()

    anyspec = pl.BlockSpec(memory_space=pl.ANY)
    return pl.pallas_call(
        body, name=name, in_specs=[anyspec], out_specs=anyspec,
        out_shape=jax.ShapeDtypeStruct(parts.shape, parts.dtype),
        scratch_shapes=[pltpu.SemaphoreType.DMA((7,)), pltpu.SemaphoreType.DMA((7,)), pltpu.SemaphoreType.DMA],
    )(parts)


def _adam_call(slices, w, m, v, name):
    R = w.shape[0]
    tr = _fit(R, PACK_ROWS, 16)
    c1 = 1.0 / (1.0 - ADAM_B1 ** ADAM_STEP)
    c2 = 1.0 / (1.0 - ADAM_B2 ** ADAM_STEP)

    def body(s_ref, w_ref, m_ref, v_ref, g_ref, d_ref, mo_ref, vo_ref):
        g = s_ref[0].astype(F32)
        for k in range(1, N_DEV):
            g = g + s_ref[k].astype(F32)
        m_new = ADAM_B1 * m_ref[...] + (1.0 - ADAM_B1) * g
        v_new = ADAM_B2 * v_ref[...] + (1.0 - ADAM_B2) * (g * g)
        g_ref[...] = g
        mo_ref[...] = m_new
        vo_ref[...] = v_new
        d_ref[...] = -ADAM_LR * ((m_new * c1) / (jnp.sqrt(v_new * c2) + ADAM_EPS) + ADAM_WD * w_ref[...])

    flat = pl.BlockSpec((tr, LANES), lambda i: (i, 0))
    out = jax.ShapeDtypeStruct((R, LANES), F32)
    return pl.pallas_call(
        body, name=name, grid=(R // tr,),
        in_specs=[pl.BlockSpec((N_DEV, tr, LANES), lambda i: (0, i, 0)), flat, flat, flat],
        out_specs=[flat, flat, flat, flat], out_shape=[out, out, out, out],
        compiler_params=_params("parallel"))(slices, w, m, v)


BIG = (("w_in", 1024, 404, 1), ("w_uq", 384, 96, 1), ("w_uk", 256, 64, 1), ("w_uv", 256, 64, 1),
       ("w_glu", 64, 512, 0), ("w_branch_attn", 512, 128, 1), ("w_branch_ssm", 512, 128, 1),
       ("w_out", 128, 1024, 0), ("w_up", 1024, 704, 1), ("w_down", 352, 1024, 0), ("conv_w", 3, 704, 1))
SMALL = (("mix_norm_pre", (1024,)), ("q_norm", (384,)), ("kv_norm", (256,)), ("ssm_lambda_re", (32, 64)),
         ("ssm_lambda_im", (32, 64)), ("ssm_log_dt", (32,)), ("ssm_b_re", (32, 64, 16)), ("ssm_b_im", (32, 64, 16)),
         ("ssm_c_re", (32, 16, 64)), ("ssm_c_im", (32, 16, 64)), ("ssm_d", (32, 16)), ("b_glu", (512,)),
         ("b_gate", (2048,)), ("mix_norm_post", (1024,)), ("ffn_norm_pre", (1024,)), ("conv_b", (5632,)),
         ("ffn_norm_post", (1024,)))


def _pack(pieces, lead=0):
    blocks, rows = [], 0
    for p in pieces:
        lead_shape = p.shape[:lead]
        n = math.prod(p.shape[lead:])
        if n % LANES:
            p = jnp.pad(p.reshape(lead_shape + (n,)), [(0, 0)] * lead + [(0, -n % LANES)])
        blocks.append(p.reshape(lead_shape + (-1, LANES)))
        rows += blocks[-1].shape[lead]
    if rows % PACK_ROWS:
        blocks.append(jnp.zeros(blocks[0].shape[:lead] + (-rows % PACK_ROWS, LANES), blocks[0].dtype))
    return jnp.concatenate(blocks, axis=lead)


def _unpack(buf, shapes, lead=0):
    lead_shape = buf.shape[:lead]
    out, off = [], 0
    for shp in shapes:
        n = math.prod(shp)
        rows = -(-n // LANES)
        piece = lax.slice_in_dim(buf, off, off + rows, axis=lead)
        if n % LANES:
            piece = lax.slice_in_dim(piece.reshape(lead_shape + (rows * LANES,)), 0, n, axis=lead)
        out.append(piece.reshape(lead_shape + tuple(shp)))
        off += rows
    return out


def _to_slices(full, rows, cols, axis):
    if axis == 1:
        return full.reshape(rows, N_DEV, cols).transpose(1, 0, 2)
    return full.reshape(N_DEV, rows, cols)


def _from_slices(parts, rows, cols, axis):
    if axis == 1:
        return parts.transpose(1, 0, 2).reshape(rows, N_DEV * cols)
    return parts.reshape(N_DEV * rows, cols)


def _head_pad_cols(w, width):
    k = w.shape[0]
    return jnp.pad(w.reshape(k, N_HEADS, width), ((0, 0), (0, 0), (0, LANES - width))).reshape(k, HEAD_PAD)


def _head_unpad_cols(w, width):
    k = w.shape[0]
    return w.reshape(k, N_HEADS, LANES)[:, :, :width].reshape(k, N_HEADS * width)


def _time_perm(a, L):
    return a.reshape(8, L // 8, a.shape[-1]).transpose(1, 0, 2).reshape(L, a.shape[-1])


def _time_unperm(a, L):
    return a.reshape(L // 8, 8, a.shape[-1]).transpose(1, 0, 2).reshape(L, a.shape[-1])


def _block_diag(w, rows_first):
    eye = jnp.eye(8, dtype=w.dtype)
    g = w.reshape(SSM_CHUNKS, 8, w.shape[1], w.shape[2])
    return jnp.einsum("qgrc,gk->qgrkc", g, eye).reshape(SSM_CHUNKS, 8 * w.shape[1], 8 * w.shape[2])


def _block_diag_t(m, r, c):
    eye = jnp.eye(8, dtype=m.dtype)
    return jnp.einsum("qgrkc,gk->qgrc", m.reshape(SSM_CHUNKS, 8, r, 8, c), eye).reshape(SSM_GROUPS, r, c)


def kernel(x, positions, mix_norm_pre, w_in, q_norm, w_uq, kv_norm, w_uk, w_uv, ssm_lambda_re, ssm_lambda_im, ssm_log_dt, ssm_b_re, ssm_b_im, ssm_c_re, ssm_c_im, ssm_d, w_glu, b_glu, w_branch_attn, w_branch_ssm, b_gate, w_out, mix_norm_post, ffn_norm_pre, w_up, conv_w, conv_b, w_down, ffn_norm_post, loss_target, m_mix_norm_pre, m_w_in, m_q_norm, m_w_uq, m_kv_norm, m_w_uk, m_w_uv, m_ssm_lambda_re, m_ssm_lambda_im, m_ssm_log_dt, m_ssm_b_re, m_ssm_b_im, m_ssm_c_re, m_ssm_c_im, m_ssm_d, m_w_glu, m_b_glu, m_w_branch_attn, m_w_branch_ssm, m_b_gate, m_w_out, m_mix_norm_post, m_ffn_norm_pre, m_w_up, m_conv_w, m_conv_b, m_w_down, m_ffn_norm_post, v_mix_norm_pre, v_w_in, v_q_norm, v_w_uq, v_kv_norm, v_w_uk, v_w_uv, v_ssm_lambda_re, v_ssm_lambda_im, v_ssm_log_dt, v_ssm_b_re, v_ssm_b_im, v_ssm_c_re, v_ssm_c_im, v_ssm_d, v_w_glu, v_b_glu, v_w_branch_attn, v_w_branch_ssm, v_b_gate, v_w_out, v_mix_norm_post, v_ffn_norm_pre, v_w_up, v_conv_w, v_conv_b, v_w_down, v_ffn_norm_post):
    given = dict(locals())
    L = x.shape[1]
    xs = x[0]
    target = loss_target[0]

    shard_bits = []
    for name, rows, cols, _ in BIG:
        w = given[name][0]
        shard_bits.append(lax.bitcast_convert_type(w, BF16) if name == "conv_w" else w.astype(BF16))
    gathered = _all_gather_call(_pack(shard_bits), "gather_weights")
    shapes = [(rows, cols, 2) if name == "conv_w" else (rows, cols) for name, rows, cols, _ in BIG]
    W = {}
    for (name, rows, cols, axis), parts in zip(BIG, _unpack(gathered, shapes, lead=1)):
        if name == "conv_w":
            parts = lax.bitcast_convert_type(parts, F32)
        W[name] = _from_slices(parts, rows, cols, axis)

    wi = W["w_in"]
    kr_cols = jnp.pad(wi[:, 640:672], ((0, 0), (QK_NOPE, LANES - QK_HEAD)))
    w_in_p = jnp.concatenate([wi[:, :640], kr_cols, wi[:, 672:]], axis=1)
    w_uq_p = _head_pad_cols(W["w_uq"], QK_HEAD)
    w_kv_p = jnp.stack([_head_pad_cols(W["w_uk"], QK_NOPE).reshape(KV_RANK, N_HEADS, LANES),
                        _head_pad_cols(W["w_uv"], V_HEAD).reshape(KV_RANK, N_HEADS, LANES)], axis=2
                       ).reshape(KV_RANK, 2 * HEAD_PAD)
    w_ba_p = jnp.pad(W["w_branch_attn"].reshape(N_HEADS, V_HEAD, D_MODEL), ((0, 0), (0, LANES - V_HEAD), (0, 0))
                     ).reshape(HEAD_PAD, D_MODEL)

    hn1 = _rms_fwd_call(xs, mix_norm_pre, "rms_pre")
    proj = _mm(hn1, w_in_p, "mm_in", tn=256)
    qn, ckvn = _mla_norms_call(proj, q_norm, kv_norm)
    q_pad = _mm(qn, w_uq_p, "mm_uq")
    kv_pad = _mm(ckvn, w_kv_p, "mm_ukv")
    half = jnp.arange(QK_ROPE // 2, dtype=F32)
    inv_freq = ROPE_THETA ** (-2.0 * half / QK_ROPE)
    inv_freq = jnp.pad(jnp.concatenate([inv_freq, inv_freq]), (QK_NOPE, LANES - QK_HEAD)).reshape(1, LANES)
    pos_col = positions.astype(F32).reshape(L, 1)
    q_r, kv_r, cosf, sinf = _mla_prep_call(q_pad, kv_pad, proj, pos_col, inv_freq)
    attn, lse = _attn_fwd_call(q_r, kv_r)

    col = lambda a: a.reshape(SSM_NSTATE, -1)
    lr_c, li_c = col(ssm_lambda_re[0]), col(ssm_lambda_im[0])
    ldt_c = col(jnp.broadcast_to(ssm_log_dt[0][:, None], (SSM_GROUPS, SSM_STATE)))
    br_c, bi_c = col(ssm_b_re[0]), col(ssm_b_im[0])
    a_re_c, a_im_c, bb_re_c, bb_im_c = _disc_call(lr_c, li_c, ldt_c, br_c, bi_c)
    a_re, a_im = a_re_c.reshape(1, SSM_NSTATE), a_im_c.reshape(1, SSM_NSTATE)
    to_bb = lambda b: _block_diag(b.reshape(SSM_GROUPS, SSM_STATE, SSM_GROUP).transpose(0, 2, 1), True).astype(BF16)
    bb_re, bb_im = to_bb(bb_re_c), to_bb(bb_im_c)
    to_cm = lambda c_: _block_diag(c_[0].transpose(0, 2, 1), True).astype(BF16)
    cm_re, cm_im = to_cm(ssm_c_re), to_cm(ssm_c_im)
    d_skip = ssm_d.reshape(1, SSM_WIDTH)
    u_p = _time_perm(proj[:, P_U:P_GATE], L)
    y1, s_re, s_im = _ssm_fwd_call(u_p, a_re, a_im, bb_re, bb_im, cm_re, cm_im, d_skip)
    w_glu_b = W["w_glu"]
    ssm_p = _glu_call(y1, w_glu_b, b_glu)
    ssm = _time_unperm(ssm_p, L)

    pa = _mm(attn, w_ba_p, "mm_ba")
    ps = _mm(ssm, W["w_branch_ssm"], "mm_bs")
    merged = _merge_call(proj, b_gate, pa, ps)
    o = _mm(merged, W["w_out"], "mm_out")
    x2, hn2 = _post_mix_call(o, xs, mix_norm_post, ffn_norm_pre)
    h = _mm(hn2, W["w_up"], "mm_up")
    cw = W["conv_w"]
    act = _conv_act_call(h, cw, conv_b)
    ff = _mm(act, W["w_down"], "mm_down", tk=1408)
    loss_row, dy, dff, g_ffn_norm_post = _ffn_out_call(ff, x2, target, ffn_norm_post)
    loss = lax.psum(loss_row[0, 0], ("x", "y", "c"))

    da = _mm(dff, W["w_down"], "mm_down_dx", tb=True, tn=256)
    g_w_down = _mm(act, dff, "mm_down_dw", ta=True, out_dtype=BF16, tm=256, tn=1024)
    dgate, dval, dcw_g, dcw_v, dcb_g, dcb_v = _conv_act_bwd_call(da, h, cw, conv_b)
    g_conv_w = jnp.concatenate([dcw_g, dcw_v], axis=1)
    g_conv_b = jnp.concatenate([dcb_g, dcb_v], axis=1)
    dh_g, dh_v = _conv_t_call(dgate, dval, cw)
    dh = jnp.concatenate([dh_g, dh_v], axis=1)
    dhn2 = _mm(dh, W["w_up"], "mm_up_dx", tb=True, tk=1408)
    g_w_up = _mm(hn2, dh, "mm_up_dw", ta=True, out_dtype=BF16, tm=512)
    dx2, do, g_ffn_norm_pre, g_mix_norm_post = _post_bwd_call(x2, dhn2, dy, o, ffn_norm_pre, mix_norm_post)
    dmerged = _mm(do, W["w_out"], "mm_out_dx", tb=True)
    g_w_out = _mm(merged, do, "mm_out_dw", ta=True, out_dtype=BF16, tm=512, tn=1024)
    dpa, dps, dl0, dl1, db0, db1 = _merge_bwd_call(dmerged, proj, b_gate, pa, ps)
    g_b_gate = jnp.concatenate([db0, db1], axis=1)
    dattn = _mm(dpa, w_ba_p, "mm_ba_dx", tb=True, out_dtype=BF16)
    g_w_ba = _mm(attn, dpa, "mm_ba_dw", ta=True, out_dtype=BF16, tm=512, tn=1024).reshape(N_HEADS, LANES, D_MODEL)[:, :V_HEAD].reshape(N_HEADS * V_HEAD, D_MODEL)
    dssm = _mm(dps, W["w_branch_ssm"], "mm_bs_dx", tb=True)
    g_w_bs = _mm(ssm, dps, "mm_bs_dw", ta=True, out_dtype=BF16, tm=512, tn=1024)

    dy1, g_w_glu, g_b_glu = _glu_bwd_call(_time_perm(dssm, L), y1, w_glu_b, b_glu)
    du_p, dbb_re, dbb_im, dcm_re, dcm_im, da_re, da_im, g_ssm_d = _ssm_bwd_call(
        dy1, u_p, s_re, s_im, a_re, a_im, bb_re, bb_im, cm_re, cm_im, d_skip)
    du = _time_unperm(du_p, L)
    from_bb = lambda m: col(_block_diag_t(m, SSM_GROUP, SSM_STATE).transpose(0, 2, 1))
    dlr, dli, dldt, dbr, dbi = _disc_bwd_call(
        lr_c, li_c, ldt_c, br_c, bi_c, da_re.reshape(SSM_NSTATE, 1), da_im.reshape(SSM_NSTATE, 1), from_bb(dbb_re), from_bb(dbb_im))
    g_c_re = _block_diag_t(dcm_re, SSM_STATE, SSM_GROUP).transpose(0, 2, 1)
    g_c_im = _block_diag_t(dcm_im, SSM_STATE, SSM_GROUP).transpose(0, 2, 1)

    dq, dkv = _attn_bwd_call(q_r, kv_r, attn, dattn, lse)
    dq_p, dkv_p, dkr_p = _mla_prep_bwd_call(dq, dkv, cosf, sinf)
    dqn = _mm(dq_p, w_uq_p, "mm_uq_dx", tb=True)
    g_w_uq = _head_unpad_cols(_mm(qn, dq_p, "mm_uq_dw", ta=True, out_dtype=BF16, tn=1024), QK_HEAD)
    dckvn = _mm(dkv_p, w_kv_p, "mm_ukv_dx", tb=True)
    g_w_kv = _mm(ckvn, dkv_p, "mm_ukv_dw", ta=True, out_dtype=BF16, tn=1024).reshape(KV_RANK, N_HEADS, 2, LANES)
    g_w_uk = g_w_kv[:, :, 0, :QK_NOPE].reshape(KV_RANK, N_HEADS * QK_NOPE)
    g_w_uv = g_w_kv[:, :, 1, :V_HEAD].reshape(KV_RANK, N_HEADS * V_HEAD)
    dcqkv, g_q_norm, g_kv_norm = _mla_norms_bwd_call(proj, dqn, dckvn, q_norm, kv_norm)
    dproj = jnp.concatenate([dcqkv, dkr_p, du.astype(BF16), dl0, dl1], axis=1)
    dhn1 = _mm(dproj, w_in_p, "mm_in_dx", tb=True, tk=1664)
    g_w_in_p = _mm(hn1, dproj, "mm_in_dw", ta=True, out_dtype=BF16, tm=512, tn=256)
    g_w_in = jnp.concatenate([g_w_in_p[:, :640], g_w_in_p[:, 640 + QK_NOPE:640 + QK_HEAD], g_w_in_p[:, 768:]], axis=1)
    grad_x, g_mix_norm_pre = _pre_bwd_call(xs, dhn1, dx2, mix_norm_pre)

    full_grads = {"w_in": g_w_in, "w_uq": g_w_uq, "w_uk": g_w_uk, "w_uv": g_w_uv, "w_glu": g_w_glu,
                  "w_branch_attn": g_w_ba, "w_branch_ssm": g_w_bs, "w_out": g_w_out, "w_up": g_w_up,
                  "w_down": g_w_down, "conv_w": g_conv_w}
    slices = _pack([_to_slices(full_grads[name], rows, cols, axis).astype(BF16) for name, rows, cols, axis in BIG], lead=1)
    received = _exchange_call(slices, "exchange_grads")
    pack_big = lambda prefix: _pack([given[prefix + name][0] for name, _, _, _ in BIG])
    big_out = _adam_call(received, pack_big(""), pack_big("m_"), pack_big("v_"), "adam_big")
    big_shapes = [(rows, cols) for _, rows, cols, _ in BIG]
    big_res = [dict(zip([n for n, _, _, _ in BIG], _unpack(buf, big_shapes))) for buf in big_out]

    small_grads = {"mix_norm_pre": g_mix_norm_pre, "q_norm": g_q_norm, "kv_norm": g_kv_norm,
                   "ssm_lambda_re": dlr, "ssm_lambda_im": dli,
                   "ssm_log_dt": jnp.sum(dldt.reshape(SSM_GROUPS, SSM_STATE), axis=1),
                   "ssm_b_re": dbr, "ssm_b_im": dbi, "ssm_c_re": g_c_re, "ssm_c_im": g_c_im, "ssm_d": g_ssm_d,
                   "b_glu": g_b_glu, "b_gate": g_b_gate, "mix_norm_post": g_mix_norm_post,
                   "ffn_norm_pre": g_ffn_norm_pre, "conv_b": g_conv_b, "ffn_norm_post": g_ffn_norm_post}
    partial = _pack([small_grads[name].reshape(shp) for name, shp in SMALL])
    all_partials = _all_gather_call(partial, "gather_small_grads")
    pack_small = lambda prefix: _pack([given[prefix + name][0] for name, _ in SMALL])
    small_out = _adam_call(all_partials, pack_small(""), pack_small("m_"), pack_small("v_"), "adam_small")
    small_res = [dict(zip([n for n, _ in SMALL], _unpack(buf, [shp for _, shp in SMALL]))) for buf in small_out]

    order = ["mix_norm_pre", "w_in", "q_norm", "w_uq", "kv_norm", "w_uk", "w_uv", "ssm_lambda_re", "ssm_lambda_im",
             "ssm_log_dt", "ssm_b_re", "ssm_b_im", "ssm_c_re", "ssm_c_im", "ssm_d", "w_glu", "b_glu", "w_branch_attn",
             "w_branch_ssm", "b_gate", "w_out", "mix_norm_post", "ffn_norm_pre", "w_up", "conv_w", "conv_b", "w_down",
             "ffn_norm_post"]
    outs = [loss, grad_x[None]]
    for kind in range(4):
        for name in order:
            src = big_res[kind] if name in big_res[kind] else small_res[kind]
            outs.append(src[name][None])
    return tuple(outs)
```

```python
import math

import jax
import jax.numpy as jnp
from jax import lax
from jax.experimental import pallas as pl
from jax.experimental.pallas import tpu as pltpu

F32 = jnp.float32
BF16 = jnp.bfloat16
MESH_ID = pl.DeviceIdType.MESH

N_DEV = 8
LANES = 128
D_MODEL = 1024
N_HEADS = 8
QK_NOPE = 64
QK_ROPE = 32
QK_HEAD = QK_NOPE + QK_ROPE
V_HEAD = 64
Q_RANK = 384
KV_RANK = 256
ROPE_THETA = 10000.0
SSM_WIDTH = 512
SSM_GROUP = 16
SSM_GROUPS = 32
SSM_STATE = 64
SSM_NSTATE = SSM_GROUPS * SSM_STATE
SSM_CHUNKS = 4
D_FF = 2816
EPS = 1e-6
ADAM_LR, ADAM_B1, ADAM_B2, ADAM_EPS, ADAM_WD, ADAM_STEP = 0.001, 0.9, 0.999, 1e-08, 0.01, 10

P_CQ, P_CKV, P_KR, P_U, P_GATE = 0, 384, 640, 768, 1280
P_IN = P_GATE + 2 * D_MODEL
HEAD_PAD = N_HEADS * LANES

PACK_ROWS = 1024
VMEM_BIG = 52 * 1024 * 1024

_GELU_C0 = math.sqrt(2.0 / math.pi)
_GELU_C1 = 0.044715
NEG = -1e30


def _fit(n, pref, mult=LANES):
    if n <= pref:
        return n
    t = (pref // mult) * mult
    while t > 0 and n % t:
        t -= mult
    assert t > 0, (n, pref, mult)
    return t


def _gelu(x):
    return 0.5 * x * (1.0 + jnp.tanh(_GELU_C0 * (x + _GELU_C1 * x * x * x)))


def _gelu_grad(x):
    x2 = x * x
    t = jnp.tanh(_GELU_C0 * x * (1.0 + _GELU_C1 * x2))
    return 0.5 * (1.0 + t) + 0.5 * x * (1.0 - t * t) * _GELU_C0 * (1.0 + 3.0 * _GELU_C1 * x2)


def _sigmoid(x):
    return 1.0 / (1.0 + jnp.exp(-x))


def _dot(a, b, dims):
    return lax.dot_general(a, b, (dims, ((), ())), preferred_element_type=F32)


NN = ((1,), (0,))
NT = ((1,), (1,))
TN = ((0,), (0,))


def _params(*sem, vmem=None):
    return pltpu.CompilerParams(dimension_semantics=tuple(sem), vmem_limit_bytes=vmem)


def _mm(a, b, name, ta=False, tb=False, out_dtype=F32, tm=1024, tn=512, tk=1024):
    if ta:
        K, M = a.shape
    else:
        M, K = a.shape
    if tb:
        N, K2 = b.shape
    else:
        K2, N = b.shape
    assert K == K2, (a.shape, b.shape, ta, tb)
    tm, tn, tk = _fit(M, tm), _fit(N, tn), _fit(K, tk)
    nk = K // tk
    dims = ((0,) if ta else (1,), (1,) if tb else (0,))

    def body(a_ref, b_ref, o_ref, *scratch):
        part = _dot(a_ref[...].astype(BF16), b_ref[...].astype(BF16), dims)
        if nk == 1:
            o_ref[...] = part.astype(out_dtype)
        else:
            acc_ref, = scratch
            k = pl.program_id(2)

            @pl.when(k == 0)
            def _():
                acc_ref[...] = part

            @pl.when(k > 0)
            def _():
                acc_ref[...] += part

            @pl.when(k == nk - 1)
            def _():
                o_ref[...] = acc_ref[...].astype(out_dtype)

    a_spec = pl.BlockSpec((tk, tm), lambda i, j, k: (k, i)) if ta else pl.BlockSpec((tm, tk), lambda i, j, k: (i, k))
    b_spec = pl.BlockSpec((tn, tk), lambda i, j, k: (j, k)) if tb else pl.BlockSpec((tk, tn), lambda i, j, k: (k, j))
    return pl.pallas_call(
        body, name=name, grid=(M // tm, N // tn, nk),
        in_specs=[a_spec, b_spec],
        out_specs=pl.BlockSpec((tm, tn), lambda i, j, k: (i, j)),
        out_shape=jax.ShapeDtypeStruct((M, N), out_dtype),
        scratch_shapes=[] if nk == 1 else [pltpu.VMEM((tm, tn), F32)],
        compiler_params=_params("parallel", "parallel", "arbitrary"),
    )(a, b)


def _row(tl, n, col=0):
    return pl.BlockSpec((tl, n), lambda i: (i, col))


def _full(shape):
    return pl.BlockSpec(shape, lambda i: (0,) * len(shape))


def _rms(x, g):
    r = lax.rsqrt(jnp.mean(x * x, axis=-1, keepdims=True) + EPS)
    return x * r * g


def _rms_bwd(x, g, dy):
    n = x.shape[-1]
    r = lax.rsqrt(jnp.mean(x * x, axis=-1, keepdims=True) + EPS)
    gy = dy * g
    dx = r * gy - x * (r * r * r * (1.0 / n)) * jnp.sum(x * gy, axis=-1, keepdims=True)
    return dx, jnp.sum(dy * x * r, axis=0, keepdims=True)


def _acc(ref, first, val):
    @pl.when(first)
    def _():
        ref[...] = val

    @pl.when(jnp.logical_not(first))
    def _():
        ref[...] += val


def _rms_fwd_call(x, g, name):
    L, n = x.shape
    tl = _fit(L, 512)

    def body(x_ref, g_ref, o_ref):
        o_ref[...] = _rms(x_ref[...], g_ref[...]).astype(BF16)

    return pl.pallas_call(
        body, name=name, grid=(L // tl,), in_specs=[_row(tl, n), _full((1, n))], out_specs=_row(tl, n),
        out_shape=jax.ShapeDtypeStruct((L, n), BF16), compiler_params=_params("parallel"))(x, g)


def _mla_norms_call(proj, q_norm, kv_norm):
    L = proj.shape[0]
    tl = _fit(L, 512)

    def body(p_ref, gq_ref, gk_ref, qn_ref, kn_ref):
        p = p_ref[...]
        qn_ref[...] = _rms(p[:, P_CQ:P_CKV], gq_ref[...]).astype(BF16)
        kn_ref[...] = _rms(p[:, P_CKV:P_KR], gk_ref[...]).astype(BF16)

    return pl.pallas_call(
        body, name="mla_norms", grid=(L // tl,),
        in_specs=[_row(tl, P_KR), _full((1, Q_RANK)), _full((1, KV_RANK))],
        out_specs=[_row(tl, Q_RANK), _row(tl, KV_RANK)],
        out_shape=[jax.ShapeDtypeStruct((L, Q_RANK), BF16), jax.ShapeDtypeStruct((L, KV_RANK), BF16)],
        compiler_params=_params("parallel"))(proj, q_norm, kv_norm)


def _rope_lanes(shape):
    lane = lax.broadcasted_iota(jnp.int32, shape, 1)
    return lane, jnp.logical_and(lane >= QK_NOPE, lane < QK_HEAD)


def _rope_apply(x, cosf, sinf, lane):
    rot = jnp.where(lane < QK_NOPE + QK_ROPE // 2, -pltpu.roll(x, LANES - QK_ROPE // 2, 1), pltpu.roll(x, QK_ROPE // 2, 1))
    return x * cosf + rot * sinf


def _rope_apply_t(dy, cosf, sinf, lane, is_rope):
    g = dy * sinf
    rot_t = jnp.where(lane < QK_NOPE + QK_ROPE // 2, pltpu.roll(g, LANES - QK_ROPE // 2, 1), -pltpu.roll(g, QK_ROPE // 2, 1))
    return dy * cosf + jnp.where(is_rope, rot_t, 0.0)


def _mla_prep_call(q_pad, kv_pad, proj, pos_col, inv_freq):
    L = q_pad.shape[0]
    tl = _fit(L, 512)

    def body(q_ref, kv_ref, kr_ref, pos_ref, f_ref, qo_ref, kvo_ref, cos_ref, sin_ref):
        lane, is_rope = _rope_lanes((tl, LANES))
        ang = pos_ref[...] * f_ref[...]
        cosf = jnp.where(is_rope, jnp.cos(ang), jnp.where(lane < QK_NOPE, 1.0, 0.0))
        sinf = jnp.where(is_rope, jnp.sin(ang), 0.0)
        cos_ref[...] = cosf
        sin_ref[...] = sinf
        kr = _rope_apply(kr_ref[...], cosf, sinf, lane)
        for h in range(N_HEADS):
            qh = _rope_apply(q_ref[:, h * LANES:(h + 1) * LANES], cosf, sinf, lane)
            qo_ref[:, h * LANES:(h + 1) * LANES] = (qh * Q_PRESCALE).astype(BF16)
            kvo_ref[:, 2 * h * LANES:(2 * h + 1) * LANES] = (kv_ref[:, 2 * h * LANES:(2 * h + 1) * LANES] + kr).astype(BF16)
            vh = jnp.where(lane == V_HEAD, 1.0, kv_ref[:, (2 * h + 1) * LANES:(2 * h + 2) * LANES])
            kvo_ref[:, (2 * h + 1) * LANES:(2 * h + 2) * LANES] = vh.astype(BF16)

    return pl.pallas_call(
        body, name="mla_prep", grid=(L // tl,),
        in_specs=[_row(tl, HEAD_PAD), _row(tl, 2 * HEAD_PAD), _row(tl, LANES, P_KR // LANES), _row(tl, 1), _full((1, LANES))],
        out_specs=[_row(tl, HEAD_PAD), _row(tl, 2 * HEAD_PAD), _row(tl, LANES), _row(tl, LANES)],
        out_shape=[jax.ShapeDtypeStruct((L, HEAD_PAD), BF16), jax.ShapeDtypeStruct((L, 2 * HEAD_PAD), BF16),
                   jax.ShapeDtypeStruct((L, LANES), F32), jax.ShapeDtypeStruct((L, LANES), F32)],
        compiler_params=_params("parallel"))(q_pad, kv_pad, proj, pos_col, inv_freq)


def _mla_prep_bwd_call(dq, dkv, cosf, sinf):
    L = dq.shape[0]
    tl = _fit(L, 512)

    def body(dq_ref, dkv_ref, cos_ref, sin_ref, dqo_ref, dkvo_ref, dkr_ref):
        lane, is_rope = _rope_lanes((tl, LANES))
        cosf, sinf = cos_ref[...], sin_ref[...]
        dk_sum = jnp.zeros((tl, LANES), F32)
        for h in range(N_HEADS):
            dqo_ref[:, h * LANES:(h + 1) * LANES] = _rope_apply_t(dq_ref[:, h * LANES:(h + 1) * LANES], cosf, sinf, lane, is_rope).astype(BF16)
            dk_sum = dk_sum + dkv_ref[:, 2 * h * LANES:(2 * h + 1) * LANES]
        dkvo_ref[...] = dkv_ref[...].astype(BF16)
        dkr_ref[...] = _rope_apply_t(dk_sum, cosf, sinf, lane, is_rope).astype(BF16)

    return pl.pallas_call(
        body, name="mla_prep_bwd", grid=(L // tl,),
        in_specs=[_row(tl, HEAD_PAD), _row(tl, 2 * HEAD_PAD), _row(tl, LANES), _row(tl, LANES)],
        out_specs=[_row(tl, HEAD_PAD), _row(tl, 2 * HEAD_PAD), _row(tl, LANES)],
        out_shape=[jax.ShapeDtypeStruct((L, HEAD_PAD), BF16), jax.ShapeDtypeStruct((L, 2 * HEAD_PAD), BF16),
                   jax.ShapeDtypeStruct((L, LANES), BF16)],
        compiler_params=_params("parallel"))(dq, dkv, cosf, sinf)


def _mla_norms_bwd_call(proj, dqn, dkn, q_norm, kv_norm):
    L = proj.shape[0]
    tl = _fit(L, 512)

    def body(p_ref, dqn_ref, dkn_ref, gq_ref, gk_ref, d_ref, dgq_ref, dgk_ref):
        first = pl.program_id(0) == 0
        p = p_ref[...]
        dq, dgq = _rms_bwd(p[:, P_CQ:P_CKV], gq_ref[...], dqn_ref[...])
        dk, dgk = _rms_bwd(p[:, P_CKV:P_KR], gk_ref[...], dkn_ref[...])
        d_ref[:, P_CQ:P_CKV] = dq.astype(BF16)
        d_ref[:, P_CKV:P_KR] = dk.astype(BF16)
        _acc(dgq_ref, first, dgq)
        _acc(dgk_ref, first, dgk)

    return pl.pallas_call(
        body, name="mla_norms_bwd", grid=(L // tl,),
        in_specs=[_row(tl, P_KR), _row(tl, Q_RANK), _row(tl, KV_RANK), _full((1, Q_RANK)), _full((1, KV_RANK))],
        out_specs=[_row(tl, P_KR), _full((1, Q_RANK)), _full((1, KV_RANK))],
        out_shape=[jax.ShapeDtypeStruct((L, P_KR), BF16), jax.ShapeDtypeStruct((1, Q_RANK), F32),
                   jax.ShapeDtypeStruct((1, KV_RANK), F32)],
        compiler_params=_params("arbitrary"))(proj, dqn, dkn, q_norm, kv_norm)


GATE_TILE = 256


def _merge_call(proj, b_gate, pa, ps):
    L = proj.shape[0]
    tl = _fit(L, 512)
    nc = D_MODEL // GATE_TILE
    g0, g1 = P_GATE // GATE_TILE, (P_GATE + D_MODEL) // GATE_TILE

    def body(l0_ref, l1_ref, b0_ref, b1_ref, pa_ref, ps_ref, o_ref):
        s0 = _sigmoid(l0_ref[...] + b0_ref[...])
        s1 = _sigmoid(l1_ref[...] + b1_ref[...])
        o_ref[...] = (s0 * pa_ref[...] + s1 * ps_ref[...]).astype(BF16)

    blk = lambda off: pl.BlockSpec((tl, GATE_TILE), lambda i, j: (i, off + j))
    bias = lambda off: pl.BlockSpec((1, GATE_TILE), lambda i, j: (0, off + j))
    return pl.pallas_call(
        body, name="merge", grid=(L // tl, nc),
        in_specs=[blk(g0), blk(g1), bias(0), bias(nc), blk(0), blk(0)],
        out_specs=blk(0), out_shape=jax.ShapeDtypeStruct((L, D_MODEL), BF16),
        compiler_params=_params("parallel", "parallel"))(proj, proj, b_gate, b_gate, pa, ps)


def _merge_bwd_call(dm, proj, b_gate, pa, ps):
    L = proj.shape[0]
    tl = _fit(L, 512)
    nc = D_MODEL // GATE_TILE
    g0, g1 = P_GATE // GATE_TILE, (P_GATE + D_MODEL) // GATE_TILE

    def body(dm_ref, l0_ref, l1_ref, b0_ref, b1_ref, pa_ref, ps_ref, dpa_ref, dps_ref, dl0_ref, dl1_ref, db0_ref, db1_ref):
        first = pl.program_id(1) == 0
        dm_ = dm_ref[...]
        s0 = _sigmoid(l0_ref[...] + b0_ref[...])
        s1 = _sigmoid(l1_ref[...] + b1_ref[...])
        dpa_ref[...] = (dm_ * s0).astype(BF16)
        dps_ref[...] = (dm_ * s1).astype(BF16)
        dl0 = dm_ * pa_ref[...] * s0 * (1.0 - s0)
        dl1 = dm_ * ps_ref[...] * s1 * (1.0 - s1)
        dl0_ref[...] = dl0.astype(BF16)
        dl1_ref[...] = dl1.astype(BF16)
        _acc(db0_ref, first, jnp.sum(dl0, axis=0, keepdims=True))
        _acc(db1_ref, first, jnp.sum(dl1, axis=0, keepdims=True))

    blk = lambda off: pl.BlockSpec((tl, GATE_TILE), lambda j, i: (i, off + j))
    bias = lambda off: pl.BlockSpec((1, GATE_TILE), lambda j, i: (0, off + j))
    act = jax.ShapeDtypeStruct((L, D_MODEL), BF16)
    vec = jax.ShapeDtypeStruct((1, D_MODEL), F32)
    return pl.pallas_call(
        body, name="merge_bwd", grid=(nc, L // tl),
        in_specs=[blk(0), blk(g0), blk(g1), bias(0), bias(nc), blk(0), blk(0)],
        out_specs=[blk(0), blk(0), blk(0), blk(0), bias(0), bias(0)],
        out_shape=[act, act, act, act, vec, vec],
        compiler_params=_params("parallel", "arbitrary"))(dm, proj, proj, b_gate, b_gate, pa, ps)


def _post_mix_call(o, x, g_post, g_fpre):
    L, n = x.shape
    tl = _fit(L, 512)

    def body(o_ref, x_ref, gp_ref, gf_ref, x2_ref, hn_ref):
        x2 = x_ref[...] + _rms(o_ref[...], gp_ref[...])
        x2_ref[...] = x2
        hn_ref[...] = _rms(x2, gf_ref[...]).astype(BF16)

    return pl.pallas_call(
        body, name="post_mix", grid=(L // tl,),
        in_specs=[_row(tl, n), _row(tl, n), _full((1, n)), _full((1, n))],
        out_specs=[_row(tl, n), _row(tl, n)],
        out_shape=[jax.ShapeDtypeStruct((L, n), F32), jax.ShapeDtypeStruct((L, n), BF16)],
        compiler_params=_params("parallel"))(o, x, g_post, g_fpre)


def _ffn_out_call(ff, x2, target, g_fpost):
    L, n = x2.shape
    tl = _fit(L, 512)

    def body(ff_ref, x2_ref, t_ref, g_ref, loss_ref, dy_ref, dff_ref, dg_ref):
        first = pl.program_id(0) == 0
        ff_ = ff_ref[...]
        err = x2_ref[...] + _rms(ff_, g_ref[...]) - t_ref[...]
        part = 0.5 * jnp.sum(jnp.sum(err * err, axis=-1, keepdims=True) * (1.0 / n), axis=0, keepdims=True)
        dy = err * (1.0 / n)
        dy_ref[...] = dy
        dff, dg = _rms_bwd(ff_, g_ref[...], dy)
        dff_ref[...] = dff.astype(BF16)
        _acc(loss_ref, first, jnp.broadcast_to(part, (1, LANES)))
        _acc(dg_ref, first, dg)

    return pl.pallas_call(
        body, name="ffn_out", grid=(L // tl,),
        in_specs=[_row(tl, n), _row(tl, n), _row(tl, n), _full((1, n))],
        out_specs=[_full((1, LANES)), _row(tl, n), _row(tl, n), _full((1, n))],
        out_shape=[jax.ShapeDtypeStruct((1, LANES), F32), jax.ShapeDtypeStruct((L, n), F32),
                   jax.ShapeDtypeStruct((L, n), BF16), jax.ShapeDtypeStruct((1, n), F32)],
        compiler_params=_params("arbitrary"))(ff, x2, target, g_fpost)


def _post_bwd_call(x2, dhn2, dy, o, g_fpre, g_post):
    L, n = x2.shape
    tl = _fit(L, 512)

    def body(x2_ref, dh_ref, dy_ref, o_ref, gf_ref, gp_ref, dx2_ref, do_ref, dgf_ref, dgp_ref):
        first = pl.program_id(0) == 0
        d1, dgf = _rms_bwd(x2_ref[...], gf_ref[...], dh_ref[...])
        dx2 = dy_ref[...] + d1
        dx2_ref[...] = dx2
        do, dgp = _rms_bwd(o_ref[...], gp_ref[...], dx2)
        do_ref[...] = do.astype(BF16)
        _acc(dgf_ref, first, dgf)
        _acc(dgp_ref, first, dgp)

    return pl.pallas_call(
        body, name="post_bwd", grid=(L // tl,),
        in_specs=[_row(tl, n), _row(tl, n), _row(tl, n), _row(tl, n), _full((1, n)), _full((1, n))],
        out_specs=[_row(tl, n), _row(tl, n), _full((1, n)), _full((1, n))],
        out_shape=[jax.ShapeDtypeStruct((L, n), F32), jax.ShapeDtypeStruct((L, n), BF16),
                   jax.ShapeDtypeStruct((1, n), F32), jax.ShapeDtypeStruct((1, n), F32)],
        compiler_params=_params("arbitrary"))(x2, dhn2, dy, o, g_fpre, g_post)


def _pre_bwd_call(x, dhn1, dx2, g_pre):
    L, n = x.shape
    tl = _fit(L, 512)

    def body(x_ref, dh_ref, dx2_ref, g_ref, dx_ref, dg_ref):
        first = pl.program_id(0) == 0
        d1, dg = _rms_bwd(x_ref[...], g_ref[...], dh_ref[...])
        dx_ref[...] = dx2_ref[...] + d1
        _acc(dg_ref, first, dg)

    return pl.pallas_call(
        body, name="pre_bwd", grid=(L // tl,),
        in_specs=[_row(tl, n), _row(tl, n), _row(tl, n), _full((1, n))],
        out_specs=[_row(tl, n), _full((1, n))],
        out_shape=[jax.ShapeDtypeStruct((L, n), F32), jax.ShapeDtypeStruct((1, n), F32)],
        compiler_params=_params("arbitrary"))(x, dhn1, dx2, g_pre)


CONV_TILE = 256
HALO = 16


def _shift_down(cur, halo_tail, by):
    rolled = pltpu.roll(cur, by, 0)
    r8 = lax.broadcasted_iota(jnp.int32, halo_tail.shape, 0)
    head = jnp.where(r8 < by, pltpu.roll(halo_tail, by, 0), rolled[0:8])
    return jnp.concatenate([head, rolled[8:]], axis=0)


def _shift_up(cur, halo_head, by):
    n = cur.shape[0]
    rolled = pltpu.roll(cur, n - by, 0)
    r8 = lax.broadcasted_iota(jnp.int32, halo_head.shape, 0)
    tail = jnp.where(r8 >= 8 - by, pltpu.roll(halo_head, 8 - by, 0), rolled[n - 8:])
    return jnp.concatenate([rolled[:n - 8], tail], axis=0)


def _conv_fwd_vals(cur, halo, w, b, not_first):
    tail = halo[HALO - 8:] * not_first
    s1 = _shift_down(cur, tail, 1)
    s2 = _shift_down(cur, tail, 2)
    return b + w[2:3] * cur + w[1:2] * s1 + w[0:1] * s2, s1, s2


def _conv_specs(L, tl, nc, rows_inner):
    nh = tl // HALO
    if rows_inner:
        ij = lambda f: (lambda j, i: f(i, j))
    else:
        ij = lambda f: f
    cur = lambda off: pl.BlockSpec((tl, CONV_TILE), ij(lambda i, j: (i, off + j)))
    prev = lambda off: pl.BlockSpec((HALO, CONV_TILE), ij(lambda i, j: (jnp.maximum(i * nh - 1, 0), off + j)))
    nxt = lambda off: pl.BlockSpec((HALO, CONV_TILE), ij(lambda i, j: (jnp.minimum((i + 1) * nh, L // HALO - 1), off + j)))
    par = lambda rows, off: pl.BlockSpec((rows, CONV_TILE), ij(lambda i, j: (0, off + j)))
    return cur, prev, nxt, par


def _conv_act_call(h, conv_w, conv_b):
    L = h.shape[0]
    tl = _fit(L, 512)
    nc = D_FF // CONV_TILE
    cur, prev, _, par = _conv_specs(L, tl, nc, False)

    def body(hg_ref, hv_ref, pg_ref, pv_ref, wg_ref, wv_ref, bg_ref, bv_ref, a_ref):
        not_first = (pl.program_id(0) > 0).astype(F32)
        gate, _, _ = _conv_fwd_vals(hg_ref[...], pg_ref[...], wg_ref[...], bg_ref[...], not_first)
        val, _, _ = _conv_fwd_vals(hv_ref[...], pv_ref[...], wv_ref[...], bv_ref[...], not_first)
        a_ref[...] = (_gelu(gate) * val).astype(BF16)

    return pl.pallas_call(
        body, name="conv_act", grid=(L // tl, nc),
        in_specs=[cur(0), cur(nc), prev(0), prev(nc), par(3, 0), par(3, nc), par(1, 0), par(1, nc)],
        out_specs=cur(0), out_shape=jax.ShapeDtypeStruct((L, D_FF), BF16),
        compiler_params=_params("parallel", "parallel"))(h, h, h, h, conv_w, conv_w, conv_b, conv_b)


def _conv_act_bwd_call(da, h, conv_w, conv_b):
    L = h.shape[0]
    tl = _fit(L, 512)
    nc = D_FF // CONV_TILE
    cur, prev, _, par = _conv_specs(L, tl, nc, True)

    def body(da_ref, hg_ref, hv_ref, pg_ref, pv_ref, wg_ref, wv_ref, bg_ref, bv_ref,
             dg_ref, dv_ref, dwg_ref, dwv_ref, dbg_ref, dbv_ref):
        first = pl.program_id(1) == 0
        not_first = (pl.program_id(1) > 0).astype(F32)
        hg, hv = hg_ref[...], hv_ref[...]
        gate, g1, g2 = _conv_fwd_vals(hg, pg_ref[...], wg_ref[...], bg_ref[...], not_first)
        val, v1, v2 = _conv_fwd_vals(hv, pv_ref[...], wv_ref[...], bv_ref[...], not_first)
        da_ = da_ref[...]
        dgate = da_ * val * _gelu_grad(gate)
        dval = da_ * _gelu(gate)
        dg_ref[...] = dgate.astype(BF16)
        dv_ref[...] = dval.astype(BF16)
        col = lambda t: jnp.sum(t, axis=0, keepdims=True)
        _acc(dwg_ref, first, jnp.concatenate([col(dgate * g2), col(dgate * g1), col(dgate * hg)], axis=0))
        _acc(dwv_ref, first, jnp.concatenate([col(dval * v2), col(dval * v1), col(dval * hv)], axis=0))
        _acc(dbg_ref, first, col(dgate))
        _acc(dbv_ref, first, col(dval))

    act = jax.ShapeDtypeStruct((L, D_FF), BF16)
    w3 = jax.ShapeDtypeStruct((3, D_FF), F32)
    w1 = jax.ShapeDtypeStruct((1, D_FF), F32)
    return pl.pallas_call(
        body, name="conv_act_bwd", grid=(nc, L // tl),
        in_specs=[cur(0), cur(0), cur(nc), prev(0), prev(nc), par(3, 0), par(3, nc), par(1, 0), par(1, nc)],
        out_specs=[cur(0), cur(0), par(3, 0), par(3, 0), par(1, 0), par(1, 0)],
        out_shape=[act, act, w3, w3, w1, w1],
        compiler_params=_params("parallel", "arbitrary"))(da, h, h, h, h, conv_w, conv_w, conv_b, conv_b)


def _conv_t_call(dgate, dval, conv_w):
    L = dgate.shape[0]
    tl = _fit(L, 512)
    nc = D_FF // CONV_TILE
    cur, _, nxt, par = _conv_specs(L, tl, nc, False)

    def run(d, off, name):
        def body(d_ref, n_ref, w_ref, o_ref):
            not_last = (pl.program_id(0) < L // tl - 1).astype(F32)
            c = d_ref[...].astype(F32)
            head = n_ref[...].astype(F32)[0:8] * not_last
            w = w_ref[...]
            o_ref[...] = (w[2:3] * c + w[1:2] * _shift_up(c, head, 1) + w[0:1] * _shift_up(c, head, 2)).astype(BF16)

        return pl.pallas_call(
            body, name=name, grid=(L // tl, nc),
            in_specs=[cur(0), nxt(0), par(3, off)],
            out_specs=cur(0), out_shape=jax.ShapeDtypeStruct((L, D_FF), BF16),
            compiler_params=_params("parallel", "parallel"))(d, d, conv_w)

    return run(dgate, 0, "conv_t_gate"), run(dval, nc, "conv_t_val")


def _glu_call(y1, w_glu, b_glu):
    L, n = y1.shape
    tl = _fit(L, 512)

    def body(y_ref, w_ref, b_ref, o_ref):
        y2 = _gelu(y_ref[...])
        z = _dot(y2.astype(BF16), w_ref[...], NN) + b_ref[...]
        o_ref[...] = (y2 * _sigmoid(z)).astype(BF16)

    return pl.pallas_call(
        body, name="glu", grid=(L // tl,), in_specs=[_row(tl, n), _full((n, n)), _full((1, n))],
        out_specs=_row(tl, n), out_shape=jax.ShapeDtypeStruct((L, n), BF16),
        compiler_params=_params("parallel"))(y1, w_glu, b_glu)


def _glu_bwd_call(dout, y1, w_glu, b_glu):
    L, n = y1.shape
    tl = _fit(L, 512)

    def body(do_ref, y_ref, w_ref, b_ref, dy_ref, dw_ref, db_ref):
        first = pl.program_id(0) == 0
        y1_ = y_ref[...]
        y2 = _gelu(y1_)
        y2b = y2.astype(BF16)
        w = w_ref[...]
        sg = _sigmoid(_dot(y2b, w, NN) + b_ref[...])
        dout_ = do_ref[...].astype(F32)
        dz = dout_ * y2 * sg * (1.0 - sg)
        dzb = dz.astype(BF16)
        dy2 = dout_ * sg + _dot(dzb, w, NT)
        dy_ref[...] = dy2 * _gelu_grad(y1_)
        _acc(dw_ref, first, _dot(y2b, dzb, TN))
        _acc(db_ref, first, jnp.sum(dz, axis=0, keepdims=True))

    return pl.pallas_call(
        body, name="glu_bwd", grid=(L // tl,),
        in_specs=[_row(tl, n), _row(tl, n), _full((n, n)), _full((1, n))],
        out_specs=[_row(tl, n), _full((n, n)), _full((1, n))],
        out_shape=[jax.ShapeDtypeStruct((L, n), F32), jax.ShapeDtypeStruct((n, n), F32), jax.ShapeDtypeStruct((1, n), F32)],
        compiler_params=_params("arbitrary"))(dout, y1, w_glu, b_glu)


ATTN_TILE = 512
ATTN_SCALE = 1.0 / math.sqrt(QK_HEAD)


LOG2E = 1.0 / math.log(2.0)
Q_PRESCALE = ATTN_SCALE * LOG2E
COMM_SEMS = [pltpu.SemaphoreType.DMA((7,)), pltpu.SemaphoreType.DMA((7,)), pltpu.SemaphoreType.DMA]
ANY_SPEC = pl.BlockSpec(memory_space=pl.ANY)


def _attn_fwd_call(q, kv, block):
    L = q.shape[0]
    t = _fit(L, ATTN_TILE)
    nq = L // t

    def body(q_ref, k_ref, v_ref, blk_ref, o_ref, lse_ref, gat_ref, m_s, acc_s, send_sems, recv_sems, local_sem):
        h, i = pl.program_id(0), pl.program_id(1)
        start, forward, finish = _gather_phases(blk_ref, gat_ref, send_sems, recv_sems, local_sem)
        pl.when(jnp.logical_and(h == 0, i == 0))(start)
        m_s[...] = jnp.full((t, 1), NEG, F32)
        acc_s[...] = jnp.zeros((t, LANES), F32)
        qv = q_ref[...]
        below = lax.broadcasted_iota(jnp.int32, (t, t), 1) <= lax.broadcasted_iota(jnp.int32, (t, t), 0)

        def block_step(kb, on_diagonal):
            off = pl.multiple_of(kb * t, t)
            s = _dot(qv, k_ref[pl.ds(off, t), :], NT)
            if on_diagonal:
                s = jnp.where(below, s, NEG)
            m_prev = m_s[...]
            m_new = jnp.maximum(m_prev, jnp.max(s, axis=1, keepdims=True))
            p = jnp.exp2(s - m_new)
            acc_s[...] = jnp.exp2(m_prev - m_new) * acc_s[...] + _dot(p.astype(BF16), v_ref[pl.ds(off, t), :], NN)
            m_s[...] = m_new

        def step(kb, carry):
            block_step(kb, False)
            return carry

        lax.fori_loop(0, i, step, 0)
        block_step(i, True)
        acc = acc_s[...]
        lane = lax.broadcasted_iota(jnp.int32, (t, LANES), 1)
        l = jnp.sum(jnp.where(lane == V_HEAD, acc, 0.0), axis=1, keepdims=True)
        o_ref[...] = (acc / l).astype(BF16)
        lse_ref[0] = m_s[...] + jnp.log(l) * LOG2E
        pl.when(jnp.logical_and(h == (3 * N_HEADS) // 4, i == 0))(forward)
        pl.when(jnp.logical_and(h == N_HEADS - 1, i == nq - 1))(finish)

    return pl.pallas_call(
        body, name="attn_fwd", grid=(N_HEADS, nq),
        in_specs=[pl.BlockSpec((t, LANES), lambda h, i: (i, h)),
                  pl.BlockSpec((L, LANES), lambda h, i: (0, 2 * h)),
                  pl.BlockSpec((L, LANES), lambda h, i: (0, 2 * h + 1)), ANY_SPEC],
        out_specs=[pl.BlockSpec((t, LANES), lambda h, i: (i, h)),
                   pl.BlockSpec((1, t, 1), lambda h, i: (h, i, 0)), ANY_SPEC],
        out_shape=[jax.ShapeDtypeStruct((L, HEAD_PAD), BF16), jax.ShapeDtypeStruct((N_HEADS, L, 1), F32),
                   jax.ShapeDtypeStruct((N_DEV,) + block.shape, block.dtype)],
        scratch_shapes=[pltpu.VMEM((t, 1), F32), pltpu.VMEM((t, LANES), F32)] + COMM_SEMS,
        compiler_params=_params("arbitrary", "arbitrary"))(q, kv, kv, block)


def _attn_bwd_call(q, kv, o, do, lse, parts):
    L = q.shape[0]
    t = _fit(L, ATTN_TILE)
    nq = L // t

    def body(q_ref, do_ref, o_ref, lse_ref, k_ref, v_ref, parts_ref, dq_ref, dkv_ref, got_ref,
             dk_s, dv_s, send_sems, recv_sems, local_sem):
        h, j = pl.program_id(0), pl.program_id(1)
        start, finish = _exchange_phases(parts_ref, got_ref, send_sems, recv_sems, local_sem)
        pl.when(jnp.logical_and(h == 0, j == 0))(start)

        @pl.when(j == 0)
        def _():
            dq_ref[...] = jnp.zeros((L, LANES), F32)

        dk_s[...] = jnp.zeros((t, LANES), F32)
        dv_s[...] = jnp.zeros((t, LANES), F32)
        kblk, vblk = k_ref[...], v_ref[...]
        below = lax.broadcasted_iota(jnp.int32, (t, t), 1) <= lax.broadcasted_iota(jnp.int32, (t, t), 0)

        def block_step(i, on_diagonal):
            rows = pl.ds(pl.multiple_of(i * t, t), t)
            qi = q_ref[rows, :]
            doi = do_ref[rows, :]
            delta = jnp.sum(doi.astype(F32) * o_ref[rows, :].astype(F32), axis=1, keepdims=True)
            s = _dot(qi, kblk, NT)
            if on_diagonal:
                s = jnp.where(below, s, NEG)
            p = jnp.exp2(s - lse_ref[0, rows, :])
            dv_s[...] += _dot(p.astype(BF16), doi, TN)
            ds = (p * (_dot(doi, vblk, NT) - delta)).astype(BF16)
            dk_s[...] += _dot(ds, qi, TN)
            dq_ref[rows, :] += _dot(ds, kblk, NN) * ATTN_SCALE

        def step(i, carry):
            block_step(i, False)
            return carry

        block_step(j, True)
        lax.fori_loop(j + 1, nq, step, 0)
        dkv_ref[:, 0:LANES] = dk_s[...] * (1.0 / LOG2E)
        dkv_ref[:, LANES:2 * LANES] = dv_s[...]
        pl.when(jnp.logical_and(h == N_HEADS - 1, j == nq - 1))(finish)

    whole = lambda: pl.BlockSpec((L, LANES), lambda h, j: (0, h))
    return pl.pallas_call(
        body, name="attn_bwd", grid=(N_HEADS, nq),
        in_specs=[whole(), whole(), whole(), pl.BlockSpec((1, L, 1), lambda h, j: (h, 0, 0)),
                  pl.BlockSpec((t, LANES), lambda h, j: (j, 2 * h)),
                  pl.BlockSpec((t, LANES), lambda h, j: (j, 2 * h + 1)), ANY_SPEC],
        out_specs=[whole(), pl.BlockSpec((t, 2 * LANES), lambda h, j: (j, h)), ANY_SPEC],
        out_shape=[jax.ShapeDtypeStruct((L, HEAD_PAD), F32), jax.ShapeDtypeStruct((L, 2 * HEAD_PAD), F32),
                   jax.ShapeDtypeStruct(parts.shape, parts.dtype)],
        scratch_shapes=[pltpu.VMEM((t, LANES), F32), pltpu.VMEM((t, LANES), F32)] + COMM_SEMS,
        compiler_params=_params("arbitrary", "arbitrary"))(q, do, o, lse, kv, kv, parts)


def _disc(lr, li, ldt, br, bi):
    dt = jnp.exp(ldt)
    mag = jnp.exp(lr * dt)
    ang = li * dt
    a_re, a_im = mag * jnp.cos(ang), mag * jnp.sin(ang)
    den = lr * lr + li * li
    n_re, n_im = a_re - 1.0, a_im
    z_re = (n_re * lr + n_im * li) / den
    z_im = (n_im * lr - n_re * li) / den
    return a_re, a_im, z_re * br - z_im * bi, z_re * bi + z_im * br


def _disc_call(lr, li, ldt, br, bi):
    def body(lr_ref, li_ref, ldt_ref, br_ref, bi_ref, ar_ref, ai_ref, bbr_ref, bbi_ref):
        ar_ref[...], ai_ref[...], bbr_ref[...], bbi_ref[...] = _disc(
            lr_ref[...], li_ref[...], ldt_ref[...], br_ref[...], bi_ref[...])

    c1 = jax.ShapeDtypeStruct((SSM_NSTATE, 1), F32)
    c16 = jax.ShapeDtypeStruct((SSM_NSTATE, SSM_GROUP), F32)
    return pl.pallas_call(body, name="ssm_disc", out_shape=[c1, c1, c16, c16])(lr, li, ldt, br, bi)


def _disc_bwd_call(lr, li, ldt, br, bi, dar, dai, dbbr, dbbi):
    def body(lr_ref, li_ref, ldt_ref, br_ref, bi_ref, dar_ref, dai_ref, dbbr_ref, dbbi_ref,
             dlr_ref, dli_ref, dldt_ref, dbr_ref, dbi_ref):
        _, vjp = jax.vjp(_disc, lr_ref[...], li_ref[...], ldt_ref[...], br_ref[...], bi_ref[...])
        dlr_ref[...], dli_ref[...], dldt_ref[...], dbr_ref[...], dbi_ref[...] = vjp(
            (dar_ref[...], dai_ref[...], dbbr_ref[...], dbbi_ref[...]))

    c1 = jax.ShapeDtypeStruct((SSM_NSTATE, 1), F32)
    c16 = jax.ShapeDtypeStruct((SSM_NSTATE, SSM_GROUP), F32)
    return pl.pallas_call(body, name="ssm_disc_bwd", out_shape=[c1, c1, c1, c16, c16])(
        lr, li, ldt, br, bi, dar, dai, dbbr, dbbi)


SSM_ROWS = 512
SSM_CW = SSM_NSTATE // SSM_CHUNKS
SSM_CU = SSM_WIDTH // SSM_CHUNKS


def _cmul(ar, ai, br, bi):
    return ar * br - ai * bi, ar * bi + ai * br


def _power(ar1, ai1, n):
    def step(_, c):
        return _cmul(c[0], c[1], ar1, ai1)

    return lax.fori_loop(0, n, step, (jnp.ones_like(ar1), jnp.zeros_like(ar1)))


def _tile(k):
    return pl.ds(pl.multiple_of(k * 8, 8), 8)


def _ssm_fwd_call(u, a_re, a_im, bb_re, bb_im, cm_re, cm_im, d_skip):
    L = u.shape[0]
    seg = L // 8
    rb = _fit(L, SSM_ROWS)

    def body(u_ref, ar_ref, ai_ref, bbr_ref, bbi_ref, cmr_ref, cmi_ref, d_ref, y_ref, sre_hbm, sim_hbm,
             s_re, s_im, sems):
        q = pl.program_id(0)

        def bu_step(r, c):
            rows = pl.ds(pl.multiple_of(r * rb, rb), rb)
            ub = u_ref[rows, :].astype(BF16)
            s_re[rows, :] = _dot(ub, bbr_ref[0], NN)
            s_im[rows, :] = _dot(ub, bbi_ref[0], NN)
            return c

        lax.fori_loop(0, L // rb, bu_step, 0)
        ar1, ai1 = ar_ref[...], ai_ref[...]
        ar = jnp.broadcast_to(ar1, (8, SSM_CW))
        ai = jnp.broadcast_to(ai1, (8, SSM_CW))

        def local(k, c):
            nr, ni = _cmul(ar, ai, c[0], c[1])
            nr = nr + s_re[_tile(k), :]
            ni = ni + s_im[_tile(k), :]
            s_re[_tile(k), :] = nr
            s_im[_tile(k), :] = ni
            return nr, ni

        zero8 = jnp.zeros((8, SSM_CW), F32)
        lax.fori_loop(0, seg, local, (zero8, zero8))
        pr, pi = _power(ar1, ai1, seg)
        end_r = s_re[pl.ds((seg - 1) * 8, 8), :]
        end_i = s_im[pl.ds((seg - 1) * 8, 8), :]
        er = jnp.zeros((1, SSM_CW), F32)
        ei = jnp.zeros((1, SSM_CW), F32)
        rows_r, rows_i = [er], [ei]
        for j in range(7):
            tr, ti = _cmul(pr, pi, er, ei)
            er, ei = end_r[j:j + 1] + tr, end_i[j:j + 1] + ti
            rows_r.append(er)
            rows_i.append(ei)
        e_r = jnp.concatenate(rows_r, axis=0)
        e_i = jnp.concatenate(rows_i, axis=0)

        def fix(k, c):
            wr, wi = _cmul(c[0], c[1], ar, ai)
            fr, fi = _cmul(wr, wi, e_r, e_i)
            s_re[_tile(k), :] += fr
            s_im[_tile(k), :] += fi
            return wr, wi

        lax.fori_loop(0, seg, fix, (jnp.ones((8, SSM_CW), F32), zero8))
        out_r = pltpu.make_async_copy(s_re, sre_hbm.at[q], sems.at[0])
        out_i = pltpu.make_async_copy(s_im, sim_hbm.at[q], sems.at[1])
        out_r.start()
        out_i.start()

        def y_step(r, c):
            rows = pl.ds(pl.multiple_of(r * rb, rb), rb)
            y = _dot(s_re[rows, :].astype(BF16), cmr_ref[0], NN) - _dot(s_im[rows, :].astype(BF16), cmi_ref[0], NN)
            y_ref[rows, :] = y + d_ref[...] * u_ref[rows, :]
            return c

        lax.fori_loop(0, L // rb, y_step, 0)
        out_r.wait()
        out_i.wait()

    chunk = lambda rows, cols: pl.BlockSpec((rows, cols), lambda q: (0, q))
    mat = lambda r, c: pl.BlockSpec((1, r, c), lambda q: (q, 0, 0))
    anyspec = pl.BlockSpec(memory_space=pl.ANY)
    states = jax.ShapeDtypeStruct((SSM_CHUNKS, L, SSM_CW), F32)
    return pl.pallas_call(
        body, name="ssm_fwd", grid=(SSM_CHUNKS,),
        in_specs=[chunk(L, SSM_CU), chunk(1, SSM_CW), chunk(1, SSM_CW), mat(SSM_CU, SSM_CW), mat(SSM_CU, SSM_CW),
                  mat(SSM_CW, SSM_CU), mat(SSM_CW, SSM_CU), chunk(1, SSM_CU)],
        out_specs=[chunk(L, SSM_CU), anyspec, anyspec],
        out_shape=[jax.ShapeDtypeStruct((L, SSM_WIDTH), F32), states, states],
        scratch_shapes=[pltpu.VMEM((L, SSM_CW), F32), pltpu.VMEM((L, SSM_CW), F32), pltpu.SemaphoreType.DMA((2,))],
        compiler_params=_params("arbitrary", vmem=VMEM_BIG))(u, a_re, a_im, bb_re, bb_im, cm_re, cm_im, d_skip)


def _ssm_bwd_call(dy, u, s_re_all, s_im_all, a_re, a_im, bb_re, bb_im, cm_re, cm_im, d_skip):
    L = u.shape[0]
    seg = L // 8
    rb = _fit(L, SSM_ROWS)

    def body(dy_ref, u_ref, sre_hbm, sim_hbm, ar_ref, ai_ref, bbr_ref, bbi_ref, cmr_ref, cmi_ref, d_ref,
             du_ref, dbbr_ref, dbbi_ref, dcmr_ref, dcmi_ref, dar_ref, dai_ref, dd_ref,
             g_re, g_im, s_re, s_im, sems):
        q = pl.program_id(0)
        in_r = pltpu.make_async_copy(sre_hbm.at[q], s_re, sems.at[0])
        in_i = pltpu.make_async_copy(sim_hbm.at[q], s_im, sems.at[1])
        in_r.start()
        in_i.start()

        def ds_step(r, c):
            rows = pl.ds(pl.multiple_of(r * rb, rb), rb)
            dyb = dy_ref[rows, :].astype(BF16)
            g_re[rows, :] = _dot(dyb, cmr_ref[0], NT)
            g_im[rows, :] = -_dot(dyb, cmi_ref[0], NT)
            return c

        lax.fori_loop(0, L // rb, ds_step, 0)
        ar1, ai1 = ar_ref[...], ai_ref[...]
        ar = jnp.broadcast_to(ar1, (8, SSM_CW))
        nai = jnp.broadcast_to(-ai1, (8, SSM_CW))

        def local(kk, c):
            k = seg - 1 - kk
            nr, ni = _cmul(ar, nai, c[0], c[1])
            nr = nr + g_re[_tile(k), :]
            ni = ni + g_im[_tile(k), :]
            g_re[_tile(k), :] = nr
            g_im[_tile(k), :] = ni
            return nr, ni

        zero8 = jnp.zeros((8, SSM_CW), F32)
        lax.fori_loop(0, seg, local, (zero8, zero8))
        pr, pi = _power(ar1, -ai1, seg)
        head_r = g_re[pl.ds(0, 8), :]
        head_i = g_im[pl.ds(0, 8), :]
        fr = jnp.zeros((1, SSM_CW), F32)
        fi = jnp.zeros((1, SSM_CW), F32)
        rows_r, rows_i = [fr], [fi]
        for j in range(6, -1, -1):
            tr, ti = _cmul(pr, pi, fr, fi)
            fr, fi = head_r[j + 1:j + 2] + tr, head_i[j + 1:j + 2] + ti
            rows_r.insert(0, fr)
            rows_i.insert(0, fi)
        f_r = jnp.concatenate(rows_r, axis=0)
        f_i = jnp.concatenate(rows_i, axis=0)
        in_r.wait()
        in_i.wait()

        def fixed(k, wr, wi):
            xr, xi = _cmul(wr, wi, f_r, f_i)
            gr = g_re[_tile(k), :] + xr
            gi = g_im[_tile(k), :] + xi
            g_re[_tile(k), :] = gr
            g_im[_tile(k), :] = gi
            return gr, gi

        def fix(kk, c):
            k = seg - 1 - kk
            wr, wi = _cmul(c[0], c[1], ar, nai)
            gr, gi = fixed(k, wr, wi)
            pr_, pi_ = s_re[_tile(k - 1), :], s_im[_tile(k - 1), :]
            return wr, wi, c[2] + gr * pr_ + gi * pi_, c[3] + gi * pr_ - gr * pi_

        wr, wi, acc_r, acc_i = lax.fori_loop(0, seg - 1, fix, (jnp.ones((8, SSM_CW), F32), zero8, zero8, zero8))
        wr, wi = _cmul(wr, wi, ar, nai)
        gr, gi = fixed(0, wr, wi)
        row8 = lax.broadcasted_iota(jnp.int32, (8, SSM_CW), 0)
        pr_ = jnp.where(row8 > 0, pltpu.roll(s_re[pl.ds((seg - 1) * 8, 8), :], 1, 0), 0.0)
        pi_ = jnp.where(row8 > 0, pltpu.roll(s_im[pl.ds((seg - 1) * 8, 8), :], 1, 0), 0.0)
        acc_r = acc_r + gr * pr_ + gi * pi_
        acc_i = acc_i + gi * pr_ - gr * pi_
        dar_ref[...] = jnp.sum(acc_r, axis=0, keepdims=True)
        dai_ref[...] = jnp.sum(acc_i, axis=0, keepdims=True)

        dbbr_ref[...] = jnp.zeros((1, SSM_CU, SSM_CW), F32)
        dbbi_ref[...] = jnp.zeros((1, SSM_CU, SSM_CW), F32)
        dcmr_ref[...] = jnp.zeros((1, SSM_CW, SSM_CU), F32)
        dcmi_ref[...] = jnp.zeros((1, SSM_CW, SSM_CU), F32)
        dd_ref[...] = jnp.zeros((1, SSM_CU), F32)

        def grad_step(r, c):
            rows = pl.ds(pl.multiple_of(r * rb, rb), rb)
            ub, dyv = u_ref[rows, :], dy_ref[rows, :]
            ubb, dyb = ub.astype(BF16), dyv.astype(BF16)
            grb, gib = g_re[rows, :].astype(BF16), g_im[rows, :].astype(BF16)
            dbbr_ref[0] += _dot(ubb, grb, TN)
            dbbi_ref[0] += _dot(ubb, gib, TN)
            dcmr_ref[0] += _dot(s_re[rows, :].astype(BF16), dyb, TN)
            dcmi_ref[0] -= _dot(s_im[rows, :].astype(BF16), dyb, TN)
            du_ref[rows, :] = _dot(grb, bbr_ref[0], NT) + _dot(gib, bbi_ref[0], NT) + d_ref[...] * dyv
            dd_ref[...] += jnp.sum(dyv * ub, axis=0, keepdims=True)
            return c

        lax.fori_loop(0, L // rb, grad_step, 0)

    chunk = lambda rows, cols: pl.BlockSpec((rows, cols), lambda q: (0, q))
    mat = lambda r, c: pl.BlockSpec((1, r, c), lambda q: (q, 0, 0))
    anyspec = pl.BlockSpec(memory_space=pl.ANY)
    big = lambda: pltpu.VMEM((L, SSM_CW), F32)
    return pl.pallas_call(
        body, name="ssm_bwd", grid=(SSM_CHUNKS,),
        in_specs=[chunk(L, SSM_CU), chunk(L, SSM_CU), anyspec, anyspec, chunk(1, SSM_CW), chunk(1, SSM_CW),
                  mat(SSM_CU, SSM_CW), mat(SSM_CU, SSM_CW), mat(SSM_CW, SSM_CU), mat(SSM_CW, SSM_CU), chunk(1, SSM_CU)],
        out_specs=[chunk(L, SSM_CU), mat(SSM_CU, SSM_CW), mat(SSM_CU, SSM_CW), mat(SSM_CW, SSM_CU), mat(SSM_CW, SSM_CU),
                   chunk(1, SSM_CW), chunk(1, SSM_CW), chunk(1, SSM_CU)],
        out_shape=[jax.ShapeDtypeStruct((L, SSM_WIDTH), F32),
                   jax.ShapeDtypeStruct((SSM_CHUNKS, SSM_CU, SSM_CW), F32), jax.ShapeDtypeStruct((SSM_CHUNKS, SSM_CU, SSM_CW), F32),
                   jax.ShapeDtypeStruct((SSM_CHUNKS, SSM_CW, SSM_CU), F32), jax.ShapeDtypeStruct((SSM_CHUNKS, SSM_CW, SSM_CU), F32),
                   jax.ShapeDtypeStruct((1, SSM_NSTATE), F32), jax.ShapeDtypeStruct((1, SSM_NSTATE), F32),
                   jax.ShapeDtypeStruct((1, SSM_WIDTH), F32)],
        scratch_shapes=[big(), big(), big(), big(), pltpu.SemaphoreType.DMA((2,))],
        compiler_params=_params("arbitrary", vmem=VMEM_BIG))(
            dy, u, s_re_all, s_im_all, a_re, a_im, bb_re, bb_im, cm_re, cm_im, d_skip)


def _place():
    return lax.axis_index("x"), lax.axis_index("y"), lax.axis_index("c")


def _all_gather_call(block, name):
    def body(x_ref, out_ref, send_sems, recv_sems, local_sem):
        start, forward, finish = _gather_phases(x_ref, out_ref, send_sems, recv_sems, local_sem)
        start()
        forward()
        finish()

    return pl.pallas_call(
        body, name=name, in_specs=[ANY_SPEC], out_specs=ANY_SPEC,
        out_shape=jax.ShapeDtypeStruct((N_DEV,) + block.shape, block.dtype), scratch_shapes=COMM_SEMS)(block)


def _gather_phases(x_ref, out_ref, send_sems, recv_sems, local_sem):
    x, y, c = _place()
    me, sibling = (x, y, c), (x, y, 1 - c)
    chips = [(1 - x, y), (x, 1 - y), (1 - x, 1 - y)]

    def slot(px, py, pc):
        return out_ref.at[4 * px + 2 * py + pc]

    def copy(k, blk, to, src=None):
        return pltpu.make_async_remote_copy(
            src_ref=slot(*blk) if src is None else src, dst_ref=slot(*blk),
            send_sem=send_sems.at[k], recv_sem=recv_sems.at[k], device_id=to, device_id_type=MESH_ID)

    mine = pltpu.make_async_copy(x_ref, slot(*me), local_sem)
    first = [copy(0, me, sibling, src=x_ref)]
    first += [copy(1 + j, me, (*chip, c), src=x_ref) for j, chip in enumerate(chips)]
    passed = [copy(4 + j, (*chip, c), sibling) for j, chip in enumerate(chips)]

    def start():
        mine.start()
        for cp in first:
            cp.start()

    def forward():
        for j, chip in enumerate(chips):
            copy(1 + j, (*chip, c), me).wait_recv()
            passed[j].start()

    def finish():
        copy(0, sibling, me).wait_recv()
        for j, chip in enumerate(chips):
            copy(4 + j, (*chip, 1 - c), me).wait_recv()
        for cp in first + passed:
            cp.wait_send()
        mine.wait()

    return start, forward, finish


def _exchange_call(parts, name):
    def body(p_ref, out_ref, send_sems, recv_sems, local_sem):
        start, finish = _exchange_phases(p_ref, out_ref, send_sems, recv_sems, local_sem)
        start()
        finish()

    return pl.pallas_call(
        body, name=name, in_specs=[ANY_SPEC], out_specs=ANY_SPEC,
        out_shape=jax.ShapeDtypeStruct(parts.shape, parts.dtype), scratch_shapes=COMM_SEMS)(parts)


def _exchange_phases(p_ref, out_ref, send_sems, recv_sems, local_sem):
    x, y, c = _place()
    me = 4 * x + 2 * y + c

    def flip(k):
        px = 1 - x if k & 4 else x
        py = 1 - y if k & 2 else y
        pc = 1 - c if k & 1 else c
        return (px, py, pc), 4 * px + 2 * py + pc

    def copy(k, landing):
        peer, peer_slot = flip(k)
        return pltpu.make_async_remote_copy(
            src_ref=p_ref.at[peer_slot], dst_ref=out_ref.at[peer_slot if landing else me],
            send_sem=send_sems.at[k - 1], recv_sem=recv_sems.at[k - 1], device_id=peer, device_id_type=MESH_ID)

    mine = pltpu.make_async_copy(p_ref.at[me], out_ref.at[me], local_sem)
    sends = [copy(k, False) for k in range(1, N_DEV)]

    def start():
        mine.start()
        for cp in sends:
            cp.start()

    def finish():
        for k in range(1, N_DEV):
            copy(k, True).wait_recv()
        for cp in sends:
            cp.wait_send()
        mine.wait()

    return start, finish


def _adam_call(slices, w, m, v, name):
    R = w.shape[0]
    tr = _fit(R, PACK_ROWS, 16)
    c1 = 1.0 / (1.0 - ADAM_B1 ** ADAM_STEP)
    c2 = 1.0 / (1.0 - ADAM_B2 ** ADAM_STEP)

    def body(s_ref, w_ref, m_ref, v_ref, g_ref, d_ref, mo_ref, vo_ref):
        g = s_ref[0].astype(F32)
        for k in range(1, N_DEV):
            g = g + s_ref[k].astype(F32)
        m_new = ADAM_B1 * m_ref[...] + (1.0 - ADAM_B1) * g
        v_new = ADAM_B2 * v_ref[...] + (1.0 - ADAM_B2) * (g * g)
        g_ref[...] = g
        mo_ref[...] = m_new
        vo_ref[...] = v_new
        d_ref[...] = -ADAM_LR * ((m_new * c1) / (jnp.sqrt(v_new * c2) + ADAM_EPS) + ADAM_WD * w_ref[...])

    flat = pl.BlockSpec((tr, LANES), lambda i: (i, 0))
    out = jax.ShapeDtypeStruct((R, LANES), F32)
    return pl.pallas_call(
        body, name=name, grid=(R // tr,),
        in_specs=[pl.BlockSpec((N_DEV, tr, LANES), lambda i: (0, i, 0)), flat, flat, flat],
        out_specs=[flat, flat, flat, flat], out_shape=[out, out, out, out],
        compiler_params=_params("parallel"))(slices, w, m, v)


BIG = (("w_in", 1024, 404, 1), ("w_uq", 384, 96, 1), ("w_uk", 256, 64, 1), ("w_uv", 256, 64, 1),
       ("w_glu", 64, 512, 0), ("w_branch_attn", 512, 128, 1), ("w_branch_ssm", 512, 128, 1),
       ("w_out", 128, 1024, 0), ("w_up", 1024, 704, 1), ("w_down", 352, 1024, 0), ("conv_w", 3, 704, 1))
BIG_MIX, BIG_FFN = BIG[:8], BIG[8:]
SMALL = (("mix_norm_pre", (1024,)), ("q_norm", (384,)), ("kv_norm", (256,)), ("ssm_lambda_re", (32, 64)),
         ("ssm_lambda_im", (32, 64)), ("ssm_log_dt", (32,)), ("ssm_b_re", (32, 64, 16)), ("ssm_b_im", (32, 64, 16)),
         ("ssm_c_re", (32, 16, 64)), ("ssm_c_im", (32, 16, 64)), ("ssm_d", (32, 16)), ("b_glu", (512,)),
         ("b_gate", (2048,)), ("mix_norm_post", (1024,)), ("ffn_norm_pre", (1024,)), ("conv_b", (5632,)),
         ("ffn_norm_post", (1024,)))


def _pack(pieces, lead=0):
    blocks, rows = [], 0
    for p in pieces:
        lead_shape = p.shape[:lead]
        n = math.prod(p.shape[lead:])
        if n % LANES:
            p = jnp.pad(p.reshape(lead_shape + (n,)), [(0, 0)] * lead + [(0, -n % LANES)])
        blocks.append(p.reshape(lead_shape + (-1, LANES)))
        rows += blocks[-1].shape[lead]
    if rows % PACK_ROWS:
        blocks.append(jnp.zeros(blocks[0].shape[:lead] + (-rows % PACK_ROWS, LANES), blocks[0].dtype))
    return jnp.concatenate(blocks, axis=lead)


def _unpack(buf, shapes, lead=0):
    lead_shape = buf.shape[:lead]
    out, off = [], 0
    for shp in shapes:
        n = math.prod(shp)
        rows = -(-n // LANES)
        piece = lax.slice_in_dim(buf, off, off + rows, axis=lead)
        if n % LANES:
            piece = lax.slice_in_dim(piece.reshape(lead_shape + (rows * LANES,)), 0, n, axis=lead)
        out.append(piece.reshape(lead_shape + tuple(shp)))
        off += rows
    return out


def _to_slices(full, rows, cols, axis):
    if axis == 1:
        return full.reshape(rows, N_DEV, cols).transpose(1, 0, 2)
    return full.reshape(N_DEV, rows, cols)


def _from_slices(parts, rows, cols, axis):
    if axis == 1:
        return parts.transpose(1, 0, 2).reshape(rows, N_DEV * cols)
    return parts.reshape(N_DEV * rows, cols)


def _head_pad_cols(w, width):
    k = w.shape[0]
    return jnp.pad(w.reshape(k, N_HEADS, width), ((0, 0), (0, 0), (0, LANES - width))).reshape(k, HEAD_PAD)


def _head_unpad_cols(w, width):
    k = w.shape[0]
    return w.reshape(k, N_HEADS, LANES)[:, :, :width].reshape(k, N_HEADS * width)


def _time_perm(a, L):
    return a.reshape(8, L // 8, a.shape[-1]).transpose(1, 0, 2).reshape(L, a.shape[-1])


def _time_unperm(a, L):
    return a.reshape(L // 8, 8, a.shape[-1]).transpose(1, 0, 2).reshape(L, a.shape[-1])


def _block_diag(w, rows_first):
    eye = jnp.eye(8, dtype=w.dtype)
    g = w.reshape(SSM_CHUNKS, 8, w.shape[1], w.shape[2])
    return jnp.einsum("qgrc,gk->qgrkc", g, eye).reshape(SSM_CHUNKS, 8 * w.shape[1], 8 * w.shape[2])


def _block_diag_t(m, r, c):
    eye = jnp.eye(8, dtype=m.dtype)
    return jnp.einsum("qgrkc,gk->qgrc", m.reshape(SSM_CHUNKS, 8, r, 8, c), eye).reshape(SSM_GROUPS, r, c)


def kernel(x, positions, mix_norm_pre, w_in, q_norm, w_uq, kv_norm, w_uk, w_uv, ssm_lambda_re, ssm_lambda_im, ssm_log_dt, ssm_b_re, ssm_b_im, ssm_c_re, ssm_c_im, ssm_d, w_glu, b_glu, w_branch_attn, w_branch_ssm, b_gate, w_out, mix_norm_post, ffn_norm_pre, w_up, conv_w, conv_b, w_down, ffn_norm_post, loss_target, m_mix_norm_pre, m_w_in, m_q_norm, m_w_uq, m_kv_norm, m_w_uk, m_w_uv, m_ssm_lambda_re, m_ssm_lambda_im, m_ssm_log_dt, m_ssm_b_re, m_ssm_b_im, m_ssm_c_re, m_ssm_c_im, m_ssm_d, m_w_glu, m_b_glu, m_w_branch_attn, m_w_branch_ssm, m_b_gate, m_w_out, m_mix_norm_post, m_ffn_norm_pre, m_w_up, m_conv_w, m_conv_b, m_w_down, m_ffn_norm_post, v_mix_norm_pre, v_w_in, v_q_norm, v_w_uq, v_kv_norm, v_w_uk, v_w_uv, v_ssm_lambda_re, v_ssm_lambda_im, v_ssm_log_dt, v_ssm_b_re, v_ssm_b_im, v_ssm_c_re, v_ssm_c_im, v_ssm_d, v_w_glu, v_b_glu, v_w_branch_attn, v_w_branch_ssm, v_b_gate, v_w_out, v_mix_norm_post, v_ffn_norm_pre, v_w_up, v_conv_w, v_conv_b, v_w_down, v_ffn_norm_post):
    given = dict(locals())
    L = x.shape[1]
    xs = x[0]
    target = loss_target[0]

    def shard_bits(group):
        bits = []
        for name, rows, cols, _ in group:
            w = given[name][0]
            bits.append(lax.bitcast_convert_type(w, BF16) if name == "conv_w" else w.astype(BF16))
        return _pack(bits)

    W = {}

    def unpack_weights(gathered, group):
        shapes = [(rows, cols, 2) if name == "conv_w" else (rows, cols) for name, rows, cols, _ in group]
        for (name, rows, cols, axis), parts in zip(group, _unpack(gathered, shapes, lead=1)):
            if name == "conv_w":
                parts = lax.bitcast_convert_type(parts, F32)
            W[name] = _from_slices(parts, rows, cols, axis)

    unpack_weights(_all_gather_call(shard_bits(BIG_MIX), "gather_weights"), BIG_MIX)

    wi = W["w_in"]
    kr_cols = jnp.pad(wi[:, 640:672], ((0, 0), (QK_NOPE, LANES - QK_HEAD)))
    w_in_p = jnp.concatenate([wi[:, :640], kr_cols, wi[:, 672:]], axis=1)
    w_uq_p = _head_pad_cols(W["w_uq"], QK_HEAD)
    w_kv_p = jnp.stack([_head_pad_cols(W["w_uk"], QK_NOPE).reshape(KV_RANK, N_HEADS, LANES),
                        _head_pad_cols(W["w_uv"], V_HEAD).reshape(KV_RANK, N_HEADS, LANES)], axis=2
                       ).reshape(KV_RANK, 2 * HEAD_PAD)
    w_ba_p = jnp.pad(W["w_branch_attn"].reshape(N_HEADS, V_HEAD, D_MODEL), ((0, 0), (0, LANES - V_HEAD), (0, 0))
                     ).reshape(HEAD_PAD, D_MODEL)

    hn1 = _rms_fwd_call(xs, mix_norm_pre, "rms_pre")
    proj = _mm(hn1, w_in_p, "mm_in", tn=256)
    qn, ckvn = _mla_norms_call(proj, q_norm, kv_norm)
    q_pad = _mm(qn, w_uq_p, "mm_uq")
    kv_pad = _mm(ckvn, w_kv_p, "mm_ukv")
    half = jnp.arange(QK_ROPE // 2, dtype=F32)
    inv_freq = ROPE_THETA ** (-2.0 * half / QK_ROPE)
    inv_freq = jnp.pad(jnp.concatenate([inv_freq, inv_freq]), (QK_NOPE, LANES - QK_HEAD)).reshape(1, LANES)
    pos_col = positions.astype(F32).reshape(L, 1)
    q_r, kv_r, cosf, sinf = _mla_prep_call(q_pad, kv_pad, proj, pos_col, inv_freq)
    attn, lse, gathered_ffn = _attn_fwd_call(q_r, kv_r, shard_bits(BIG_FFN))
    unpack_weights(gathered_ffn, BIG_FFN)

    col = lambda a: a.reshape(SSM_NSTATE, -1)
    lr_c, li_c = col(ssm_lambda_re[0]), col(ssm_lambda_im[0])
    ldt_c = col(jnp.broadcast_to(ssm_log_dt[0][:, None], (SSM_GROUPS, SSM_STATE)))
    br_c, bi_c = col(ssm_b_re[0]), col(ssm_b_im[0])
    a_re_c, a_im_c, bb_re_c, bb_im_c = _disc_call(lr_c, li_c, ldt_c, br_c, bi_c)
    a_re, a_im = a_re_c.reshape(1, SSM_NSTATE), a_im_c.reshape(1, SSM_NSTATE)
    to_bb = lambda b: _block_diag(b.reshape(SSM_GROUPS, SSM_STATE, SSM_GROUP).transpose(0, 2, 1), True).astype(BF16)
    bb_re, bb_im = to_bb(bb_re_c), to_bb(bb_im_c)
    to_cm = lambda c_: _block_diag(c_[0].transpose(0, 2, 1), True).astype(BF16)
    cm_re, cm_im = to_cm(ssm_c_re), to_cm(ssm_c_im)
    d_skip = ssm_d.reshape(1, SSM_WIDTH)
    u_p = _time_perm(proj[:, P_U:P_GATE], L)
    y1, s_re, s_im = _ssm_fwd_call(u_p, a_re, a_im, bb_re, bb_im, cm_re, cm_im, d_skip)
    w_glu_b = W["w_glu"]
    ssm_p = _glu_call(y1, w_glu_b, b_glu)
    ssm = _time_unperm(ssm_p, L)

    pa = _mm(attn, w_ba_p, "mm_ba")
    ps = _mm(ssm, W["w_branch_ssm"], "mm_bs")
    merged = _merge_call(proj, b_gate, pa, ps)
    o = _mm(merged, W["w_out"], "mm_out")
    x2, hn2 = _post_mix_call(o, xs, mix_norm_post, ffn_norm_pre)
    h = _mm(hn2, W["w_up"], "mm_up")
    cw = W["conv_w"]
    act = _conv_act_call(h, cw, conv_b)
    ff = _mm(act, W["w_down"], "mm_down", tk=1408)
    loss_row, dy, dff, g_ffn_norm_post = _ffn_out_call(ff, x2, target, ffn_norm_post)
    loss = lax.psum(loss_row[0, 0], ("x", "y", "c"))

    da = _mm(dff, W["w_down"], "mm_down_dx", tb=True, tn=256)
    g_w_down = _mm(act, dff, "mm_down_dw", ta=True, out_dtype=BF16, tm=256, tn=1024)
    dgate, dval, dcw_g, dcw_v, dcb_g, dcb_v = _conv_act_bwd_call(da, h, cw, conv_b)
    g_conv_w = jnp.concatenate([dcw_g, dcw_v], axis=1)
    g_conv_b = jnp.concatenate([dcb_g, dcb_v], axis=1)
    dh_g, dh_v = _conv_t_call(dgate, dval, cw)
    dh = jnp.concatenate([dh_g, dh_v], axis=1)
    dhn2 = _mm(dh, W["w_up"], "mm_up_dx", tb=True, tk=1408)
    g_w_up = _mm(hn2, dh, "mm_up_dw", ta=True, out_dtype=BF16, tm=512)
    dx2, do, g_ffn_norm_pre, g_mix_norm_post = _post_bwd_call(x2, dhn2, dy, o, ffn_norm_pre, mix_norm_post)
    dmerged = _mm(do, W["w_out"], "mm_out_dx", tb=True)
    g_w_out = _mm(merged, do, "mm_out_dw", ta=True, out_dtype=BF16, tm=512, tn=1024)
    dpa, dps, dl0, dl1, db0, db1 = _merge_bwd_call(dmerged, proj, b_gate, pa, ps)
    g_b_gate = jnp.concatenate([db0, db1], axis=1)
    dattn = _mm(dpa, w_ba_p, "mm_ba_dx", tb=True, out_dtype=BF16)
    g_w_ba = _mm(attn, dpa, "mm_ba_dw", ta=True, out_dtype=BF16, tm=512, tn=1024).reshape(N_HEADS, LANES, D_MODEL)[:, :V_HEAD].reshape(N_HEADS * V_HEAD, D_MODEL)
    dssm = _mm(dps, W["w_branch_ssm"], "mm_bs_dx", tb=True)
    g_w_bs = _mm(ssm, dps, "mm_bs_dw", ta=True, out_dtype=BF16, tm=512, tn=1024)

    dy1, g_w_glu, g_b_glu = _glu_bwd_call(_time_perm(dssm, L), y1, w_glu_b, b_glu)
    du_p, dbb_re, dbb_im, dcm_re, dcm_im, da_re, da_im, g_ssm_d = _ssm_bwd_call(
        dy1, u_p, s_re, s_im, a_re, a_im, bb_re, bb_im, cm_re, cm_im, d_skip)
    du = _time_unperm(du_p, L)
    from_bb = lambda m: col(_block_diag_t(m, SSM_GROUP, SSM_STATE).transpose(0, 2, 1))
    dlr, dli, dldt, dbr, dbi = _disc_bwd_call(
        lr_c, li_c, ldt_c, br_c, bi_c, da_re.reshape(SSM_NSTATE, 1), da_im.reshape(SSM_NSTATE, 1), from_bb(dbb_re), from_bb(dbb_im))
    g_c_re = _block_diag_t(dcm_re, SSM_STATE, SSM_GROUP).transpose(0, 2, 1)
    g_c_im = _block_diag_t(dcm_im, SSM_STATE, SSM_GROUP).transpose(0, 2, 1)

    def grad_slices(group, grads):
        return _pack([_to_slices(grads[name], rows, cols, axis).astype(BF16) for name, rows, cols, axis in group], lead=1)

    ffn_grads = {"w_up": g_w_up, "w_down": g_w_down, "conv_w": g_conv_w}
    dq, dkv, received_ffn = _attn_bwd_call(q_r, kv_r, attn, dattn, lse, grad_slices(BIG_FFN, ffn_grads))
    dq_p, dkv_p, dkr_p = _mla_prep_bwd_call(dq, dkv, cosf, sinf)
    dqn = _mm(dq_p, w_uq_p, "mm_uq_dx", tb=True)
    g_w_uq = _head_unpad_cols(_mm(qn, dq_p, "mm_uq_dw", ta=True, out_dtype=BF16, tn=1024), QK_HEAD)
    dckvn = _mm(dkv_p, w_kv_p, "mm_ukv_dx", tb=True)
    g_w_kv = _mm(ckvn, dkv_p, "mm_ukv_dw", ta=True, out_dtype=BF16, tn=1024).reshape(KV_RANK, N_HEADS, 2, LANES)
    g_w_uk = g_w_kv[:, :, 0, :QK_NOPE].reshape(KV_RANK, N_HEADS * QK_NOPE)
    g_w_uv = g_w_kv[:, :, 1, :V_HEAD].reshape(KV_RANK, N_HEADS * V_HEAD)
    dcqkv, g_q_norm, g_kv_norm = _mla_norms_bwd_call(proj, dqn, dckvn, q_norm, kv_norm)
    dproj = jnp.concatenate([dcqkv, dkr_p, du.astype(BF16), dl0, dl1], axis=1)
    dhn1 = _mm(dproj, w_in_p, "mm_in_dx", tb=True, tk=1664)
    g_w_in_p = _mm(hn1, dproj, "mm_in_dw", ta=True, out_dtype=BF16, tm=512, tn=256)
    g_w_in = jnp.concatenate([g_w_in_p[:, :640], g_w_in_p[:, 640 + QK_NOPE:640 + QK_HEAD], g_w_in_p[:, 768:]], axis=1)
    grad_x, g_mix_norm_pre = _pre_bwd_call(xs, dhn1, dx2, mix_norm_pre)

    mix_grads = {"w_in": g_w_in, "w_uq": g_w_uq, "w_uk": g_w_uk, "w_uv": g_w_uv, "w_glu": g_w_glu,
                 "w_branch_attn": g_w_ba, "w_branch_ssm": g_w_bs, "w_out": g_w_out}
    received_mix = _exchange_call(grad_slices(BIG_MIX, mix_grads), "exchange_grads")
    big_res = [{}, {}, {}, {}]
    for group, received, name in ((BIG_FFN, received_ffn, "adam_ffn"), (BIG_MIX, received_mix, "adam_mix")):
        pack_group = lambda prefix: _pack([given[prefix + n][0] for n, _, _, _ in group])
        bufs = _adam_call(received, pack_group(""), pack_group("m_"), pack_group("v_"), name)
        for res, buf in zip(big_res, bufs):
            res.update(zip([n for n, _, _, _ in group], _unpack(buf, [(rows, cols) for _, rows, cols, _ in group])))

    small_grads = {"mix_norm_pre": g_mix_norm_pre, "q_norm": g_q_norm, "kv_norm": g_kv_norm,
                   "ssm_lambda_re": dlr, "ssm_lambda_im": dli,
                   "ssm_log_dt": jnp.sum(dldt.reshape(SSM_GROUPS, SSM_STATE), axis=1),
                   "ssm_b_re": dbr, "ssm_b_im": dbi, "ssm_c_re": g_c_re, "ssm_c_im": g_c_im, "ssm_d": g_ssm_d,
                   "b_glu": g_b_glu, "b_gate": g_b_gate, "mix_norm_post": g_mix_norm_post,
                   "ffn_norm_pre": g_ffn_norm_pre, "conv_b": g_conv_b, "ffn_norm_post": g_ffn_norm_post}
    partial = _pack([small_grads[name].reshape(shp) for name, shp in SMALL])
    all_partials = _all_gather_call(partial, "gather_small_grads")
    pack_small = lambda prefix: _pack([given[prefix + name][0] for name, _ in SMALL])
    small_out = _adam_call(all_partials, pack_small(""), pack_small("m_"), pack_small("v_"), "adam_small")
    small_res = [dict(zip([n for n, _ in SMALL], _unpack(buf, [shp for _, shp in SMALL]))) for buf in small_out]

    order = ["mix_norm_pre", "w_in", "q_norm", "w_uq", "kv_norm", "w_uk", "w_uv", "ssm_lambda_re", "ssm_lambda_im",
             "ssm_log_dt", "ssm_b_re", "ssm_b_im", "ssm_c_re", "ssm_c_im", "ssm_d", "w_glu", "b_glu", "w_branch_attn",
             "w_branch_ssm", "b_gate", "w_out", "mix_norm_post", "ffn_norm_pre", "w_up", "conv_w", "conv_b", "w_down",
             "ffn_norm_post"]
    outs = [loss, grad_x[None]]
    for kind in range(4):
        for name in order:
            src = big_res[kind] if name in big_res[kind] else small_res[kind]
            outs.append(src[name][None])
    return tuple(outs)
```

```python
import math

import jax
import jax.numpy as jnp
from jax import lax
from jax.experimental import pallas as pl
from jax.experimental.pallas import tpu as pltpu

F32 = jnp.float32
BF16 = jnp.bfloat16
MESH_ID = pl.DeviceIdType.MESH

N_DEV = 8
LANES = 128
D_MODEL = 1024
N_HEADS = 8
QK_NOPE = 64
QK_ROPE = 32
QK_HEAD = QK_NOPE + QK_ROPE
V_HEAD = 64
Q_RANK = 384
KV_RANK = 256
ROPE_THETA = 10000.0
SSM_WIDTH = 512
SSM_GROUP = 16
SSM_GROUPS = 32
SSM_STATE = 64
SSM_NSTATE = SSM_GROUPS * SSM_STATE
SSM_CHUNKS = 4
D_FF = 2816
EPS = 1e-6
ADAM_LR, ADAM_B1, ADAM_B2, ADAM_EPS, ADAM_WD, ADAM_STEP = 0.001, 0.9, 0.999, 1e-08, 0.01, 10

P_CQ, P_CKV, P_KR, P_U, P_GATE = 0, 384, 640, 768, 1280
P_IN = P_GATE + 2 * D_MODEL
HEAD_PAD = N_HEADS * LANES

PACK_ROWS = 1024
VMEM_BIG = 52 * 1024 * 1024

_GELU_C0 = math.sqrt(2.0 / math.pi)
_GELU_C1 = 0.044715
NEG = -1e30


def _fit(n, pref, mult=LANES):
    if n <= pref:
        return n
    t = (pref // mult) * mult
    while t > 0 and n % t:
        t -= mult
    assert t > 0, (n, pref, mult)
    return t


def _gelu(x):
    return 0.5 * x * (1.0 + jnp.tanh(_GELU_C0 * (x + _GELU_C1 * x * x * x)))


def _gelu_grad(x):
    x2 = x * x
    t = jnp.tanh(_GELU_C0 * x * (1.0 + _GELU_C1 * x2))
    return 0.5 * (1.0 + t) + 0.5 * x * (1.0 - t * t) * _GELU_C0 * (1.0 + 3.0 * _GELU_C1 * x2)


def _sigmoid(x):
    return 1.0 / (1.0 + jnp.exp(-x))


def _dot(a, b, dims):
    return lax.dot_general(a, b, (dims, ((), ())), preferred_element_type=F32)


NN = ((1,), (0,))
NT = ((1,), (1,))
TN = ((0,), (0,))


def _params(*sem, vmem=None):
    return pltpu.CompilerParams(dimension_semantics=tuple(sem), vmem_limit_bytes=vmem)


def _mm(a, b, name, ta=False, tb=False, out_dtype=F32, tm=1024, tn=512, tk=1024):
    if ta:
        K, M = a.shape
    else:
        M, K = a.shape
    if tb:
        N, K2 = b.shape
    else:
        K2, N = b.shape
    assert K == K2, (a.shape, b.shape, ta, tb)
    tm, tn, tk = _fit(M, tm), _fit(N, tn), _fit(K, tk)
    nk = K // tk
    dims = ((0,) if ta else (1,), (1,) if tb else (0,))

    def body(a_ref, b_ref, o_ref, *scratch):
        part = _dot(a_ref[...].astype(BF16), b_ref[...].astype(BF16), dims)
        if nk == 1:
            o_ref[...] = part.astype(out_dtype)
        else:
            acc_ref, = scratch
            k = pl.program_id(2)

            @pl.when(k == 0)
            def _():
                acc_ref[...] = part

            @pl.when(k > 0)
            def _():
                acc_ref[...] += part

            @pl.when(k == nk - 1)
            def _():
                o_ref[...] = acc_ref[...].astype(out_dtype)

    a_spec = pl.BlockSpec((tk, tm), lambda i, j, k: (k, i)) if ta else pl.BlockSpec((tm, tk), lambda i, j, k: (i, k))
    b_spec = pl.BlockSpec((tn, tk), lambda i, j, k: (j, k)) if tb else pl.BlockSpec((tk, tn), lambda i, j, k: (k, j))
    return pl.pallas_call(
        body, name=name, grid=(M // tm, N // tn, nk),
        in_specs=[a_spec, b_spec],
        out_specs=pl.BlockSpec((tm, tn), lambda i, j, k: (i, j)),
        out_shape=jax.ShapeDtypeStruct((M, N), out_dtype),
        scratch_shapes=[] if nk == 1 else [pltpu.VMEM((tm, tn), F32)],
        compiler_params=_params("parallel", "parallel", "arbitrary"),
    )(a, b)


def _row(tl, n, col=0):
    return pl.BlockSpec((tl, n), lambda i: (i, col))


def _full(shape):
    return pl.BlockSpec(shape, lambda i: (0,) * len(shape))


def _rms(x, g):
    r = lax.rsqrt(jnp.mean(x * x, axis=-1, keepdims=True) + EPS)
    return x * r * g


def _rms_bwd(x, g, dy):
    n = x.shape[-1]
    r = lax.rsqrt(jnp.mean(x * x, axis=-1, keepdims=True) + EPS)
    gy = dy * g
    dx = r * gy - x * (r * r * r * (1.0 / n)) * jnp.sum(x * gy, axis=-1, keepdims=True)
    return dx, jnp.sum(dy * x * r, axis=0, keepdims=True)


def _acc(ref, first, val):
    @pl.when(first)
    def _():
        ref[...] = val

    @pl.when(jnp.logical_not(first))
    def _():
        ref[...] += val


def _rms_fwd_call(x, g, name):
    L, n = x.shape
    tl = _fit(L, 512)

    def body(x_ref, g_ref, o_ref):
        o_ref[...] = _rms(x_ref[...], g_ref[...]).astype(BF16)

    return pl.pallas_call(
        body, name=name, grid=(L // tl,), in_specs=[_row(tl, n), _full((1, n))], out_specs=_row(tl, n),
        out_shape=jax.ShapeDtypeStruct((L, n), BF16), compiler_params=_params("parallel"))(x, g)


def _mla_norms_call(proj, q_norm, kv_norm):
    L = proj.shape[0]
    tl = _fit(L, 512)

    def body(p_ref, gq_ref, gk_ref, qn_ref, kn_ref):
        p = p_ref[...]
        qn_ref[...] = _rms(p[:, P_CQ:P_CKV], gq_ref[...]).astype(BF16)
        kn_ref[...] = _rms(p[:, P_CKV:P_KR], gk_ref[...]).astype(BF16)

    return pl.pallas_call(
        body, name="mla_norms", grid=(L // tl,),
        in_specs=[_row(tl, P_KR), _full((1, Q_RANK)), _full((1, KV_RANK))],
        out_specs=[_row(tl, Q_RANK), _row(tl, KV_RANK)],
        out_shape=[jax.ShapeDtypeStruct((L, Q_RANK), BF16), jax.ShapeDtypeStruct((L, KV_RANK), BF16)],
        compiler_params=_params("parallel"))(proj, q_norm, kv_norm)


def _rope_lanes(shape):
    lane = lax.broadcasted_iota(jnp.int32, shape, 1)
    return lane, jnp.logical_and(lane >= QK_NOPE, lane < QK_HEAD)


def _rope_apply(x, cosf, sinf, lane):
    rot = jnp.where(lane < QK_NOPE + QK_ROPE // 2, -pltpu.roll(x, LANES - QK_ROPE // 2, 1), pltpu.roll(x, QK_ROPE // 2, 1))
    return x * cosf + rot * sinf


def _rope_apply_t(dy, cosf, sinf, lane, is_rope):
    g = dy * sinf
    rot_t = jnp.where(lane < QK_NOPE + QK_ROPE // 2, pltpu.roll(g, LANES - QK_ROPE // 2, 1), -pltpu.roll(g, QK_ROPE // 2, 1))
    return dy * cosf + jnp.where(is_rope, rot_t, 0.0)


def _mla_prep_call(q_pad, kv_pad, proj, pos_col, inv_freq):
    L = q_pad.shape[0]
    tl = _fit(L, 512)

    def body(q_ref, kv_ref, kr_ref, pos_ref, f_ref, qo_ref, kvo_ref, cos_ref, sin_ref):
        lane, is_rope = _rope_lanes((tl, LANES))
        ang = pos_ref[...] * f_ref[...]
        cosf = jnp.where(is_rope, jnp.cos(ang), jnp.where(lane < QK_NOPE, 1.0, 0.0))
        sinf = jnp.where(is_rope, jnp.sin(ang), 0.0)
        cos_ref[...] = cosf
        sin_ref[...] = sinf
        kr = _rope_apply(kr_ref[...], cosf, sinf, lane)
        for h in range(N_HEADS):
            qh = _rope_apply(q_ref[:, h * LANES:(h + 1) * LANES], cosf, sinf, lane)
            qo_ref[:, h * LANES:(h + 1) * LANES] = (qh * Q_PRESCALE).astype(BF16)
            kvo_ref[:, 2 * h * LANES:(2 * h + 1) * LANES] = (kv_ref[:, 2 * h * LANES:(2 * h + 1) * LANES] + kr).astype(BF16)
            vh = jnp.where(lane == V_HEAD, 1.0, kv_ref[:, (2 * h + 1) * LANES:(2 * h + 2) * LANES])
            kvo_ref[:, (2 * h + 1) * LANES:(2 * h + 2) * LANES] = vh.astype(BF16)

    return pl.pallas_call(
        body, name="mla_prep", grid=(L // tl,),
        in_specs=[_row(tl, HEAD_PAD), _row(tl, 2 * HEAD_PAD), _row(tl, LANES, P_KR // LANES), _row(tl, 1), _full((1, LANES))],
        out_specs=[_row(tl, HEAD_PAD), _row(tl, 2 * HEAD_PAD), _row(tl, LANES), _row(tl, LANES)],
        out_shape=[jax.ShapeDtypeStruct((L, HEAD_PAD), BF16), jax.ShapeDtypeStruct((L, 2 * HEAD_PAD), BF16),
                   jax.ShapeDtypeStruct((L, LANES), F32), jax.ShapeDtypeStruct((L, LANES), F32)],
        compiler_params=_params("parallel"))(q_pad, kv_pad, proj, pos_col, inv_freq)


def _mla_prep_bwd_call(dq, dkv, cosf, sinf):
    L = dq.shape[0]
    tl = _fit(L, 512)

    def body(dq_ref, dkv_ref, cos_ref, sin_ref, dqo_ref, dkvo_ref, dkr_ref):
        lane, is_rope = _rope_lanes((tl, LANES))
        cosf, sinf = cos_ref[...], sin_ref[...]
        dk_sum = jnp.zeros((tl, LANES), F32)
        for h in range(N_HEADS):
            dqo_ref[:, h * LANES:(h + 1) * LANES] = _rope_apply_t(dq_ref[:, h * LANES:(h + 1) * LANES], cosf, sinf, lane, is_rope).astype(BF16)
            dk_sum = dk_sum + dkv_ref[:, 2 * h * LANES:(2 * h + 1) * LANES]
        dkvo_ref[...] = dkv_ref[...].astype(BF16)
        dkr_ref[...] = _rope_apply_t(dk_sum, cosf, sinf, lane, is_rope).astype(BF16)

    return pl.pallas_call(
        body, name="mla_prep_bwd", grid=(L // tl,),
        in_specs=[_row(tl, HEAD_PAD), _row(tl, 2 * HEAD_PAD), _row(tl, LANES), _row(tl, LANES)],
        out_specs=[_row(tl, HEAD_PAD), _row(tl, 2 * HEAD_PAD), _row(tl, LANES)],
        out_shape=[jax.ShapeDtypeStruct((L, HEAD_PAD), BF16), jax.ShapeDtypeStruct((L, 2 * HEAD_PAD), BF16),
                   jax.ShapeDtypeStruct((L, LANES), BF16)],
        compiler_params=_params("parallel"))(dq, dkv, cosf, sinf)


def _mla_norms_bwd_call(proj, dqn, dkn, q_norm, kv_norm):
    L = proj.shape[0]
    tl = _fit(L, 512)

    def body(p_ref, dqn_ref, dkn_ref, gq_ref, gk_ref, d_ref, dgq_ref, dgk_ref):
        first = pl.program_id(0) == 0
        p = p_ref[...]
        dq, dgq = _rms_bwd(p[:, P_CQ:P_CKV], gq_ref[...], dqn_ref[...])
        dk, dgk = _rms_bwd(p[:, P_CKV:P_KR], gk_ref[...], dkn_ref[...])
        d_ref[:, P_CQ:P_CKV] = dq.astype(BF16)
        d_ref[:, P_CKV:P_KR] = dk.astype(BF16)
        _acc(dgq_ref, first, dgq)
        _acc(dgk_ref, first, dgk)

    return pl.pallas_call(
        body, name="mla_norms_bwd", grid=(L // tl,),
        in_specs=[_row(tl, P_KR), _row(tl, Q_RANK), _row(tl, KV_RANK), _full((1, Q_RANK)), _full((1, KV_RANK))],
        out_specs=[_row(tl, P_KR), _full((1, Q_RANK)), _full((1, KV_RANK))],
        out_shape=[jax.ShapeDtypeStruct((L, P_KR), BF16), jax.ShapeDtypeStruct((1, Q_RANK), F32),
                   jax.ShapeDtypeStruct((1, KV_RANK), F32)],
        compiler_params=_params("arbitrary"))(proj, dqn, dkn, q_norm, kv_norm)


GATE_TILE = 256


def _merge_call(proj, b_gate, pa, ps):
    L = proj.shape[0]
    tl = _fit(L, 512)
    nc = D_MODEL // GATE_TILE
    g0, g1 = P_GATE // GATE_TILE, (P_GATE + D_MODEL) // GATE_TILE

    def body(l0_ref, l1_ref, b0_ref, b1_ref, pa_ref, ps_ref, o_ref):
        s0 = _sigmoid(l0_ref[...] + b0_ref[...])
        s1 = _sigmoid(l1_ref[...] + b1_ref[...])
        o_ref[...] = (s0 * pa_ref[...] + s1 * ps_ref[...]).astype(BF16)

    blk = lambda off: pl.BlockSpec((tl, GATE_TILE), lambda i, j: (i, off + j))
    bias = lambda off: pl.BlockSpec((1, GATE_TILE), lambda i, j: (0, off + j))
    return pl.pallas_call(
        body, name="merge", grid=(L // tl, nc),
        in_specs=[blk(g0), blk(g1), bias(0), bias(nc), blk(0), blk(0)],
        out_specs=blk(0), out_shape=jax.ShapeDtypeStruct((L, D_MODEL), BF16),
        compiler_params=_params("parallel", "parallel"))(proj, proj, b_gate, b_gate, pa, ps)


def _merge_bwd_call(dm, proj, b_gate, pa, ps):
    L = proj.shape[0]
    tl = _fit(L, 512)
    nc = D_MODEL // GATE_TILE
    g0, g1 = P_GATE // GATE_TILE, (P_GATE + D_MODEL) // GATE_TILE

    def body(dm_ref, l0_ref, l1_ref, b0_ref, b1_ref, pa_ref, ps_ref, dpa_ref, dps_ref, dl0_ref, dl1_ref, db0_ref, db1_ref):
        first = pl.program_id(1) == 0
        dm_ = dm_ref[...]
        s0 = _sigmoid(l0_ref[...] + b0_ref[...])
        s1 = _sigmoid(l1_ref[...] + b1_ref[...])
        dpa_ref[...] = (dm_ * s0).astype(BF16)
        dps_ref[...] = (dm_ * s1).astype(BF16)
        dl0 = dm_ * pa_ref[...] * s0 * (1.0 - s0)
        dl1 = dm_ * ps_ref[...] * s1 * (1.0 - s1)
        dl0_ref[...] = dl0.astype(BF16)
        dl1_ref[...] = dl1.astype(BF16)
        _acc(db0_ref, first, jnp.sum(dl0, axis=0, keepdims=True))
        _acc(db1_ref, first, jnp.sum(dl1, axis=0, keepdims=True))

    blk = lambda off: pl.BlockSpec((tl, GATE_TILE), lambda j, i: (i, off + j))
    bias = lambda off: pl.BlockSpec((1, GATE_TILE), lambda j, i: (0, off + j))
    act = jax.ShapeDtypeStruct((L, D_MODEL), BF16)
    vec = jax.ShapeDtypeStruct((1, D_MODEL), F32)
    return pl.pallas_call(
        body, name="merge_bwd", grid=(nc, L // tl),
        in_specs=[blk(0), blk(g0), blk(g1), bias(0), bias(nc), blk(0), blk(0)],
        out_specs=[blk(0), blk(0), blk(0), blk(0), bias(0), bias(0)],
        out_shape=[act, act, act, act, vec, vec],
        compiler_params=_params("parallel", "arbitrary"))(dm, proj, proj, b_gate, b_gate, pa, ps)


def _post_mix_call(o, x, g_post, g_fpre):
    L, n = x.shape
    tl = _fit(L, 512)

    def body(o_ref, x_ref, gp_ref, gf_ref, x2_ref, hn_ref):
        x2 = x_ref[...] + _rms(o_ref[...], gp_ref[...])
        x2_ref[...] = x2
        hn_ref[...] = _rms(x2, gf_ref[...]).astype(BF16)

    return pl.pallas_call(
        body, name="post_mix", grid=(L // tl,),
        in_specs=[_row(tl, n), _row(tl, n), _full((1, n)), _full((1, n))],
        out_specs=[_row(tl, n), _row(tl, n)],
        out_shape=[jax.ShapeDtypeStruct((L, n), F32), jax.ShapeDtypeStruct((L, n), BF16)],
        compiler_params=_params("parallel"))(o, x, g_post, g_fpre)


def _ffn_out_call(ff, x2, target, g_fpost):
    L, n = x2.shape
    tl = _fit(L, 512)

    def body(ff_ref, x2_ref, t_ref, g_ref, loss_ref, dy_ref, dff_ref, dg_ref):
        first = pl.program_id(0) == 0
        ff_ = ff_ref[...]
        err = x2_ref[...] + _rms(ff_, g_ref[...]) - t_ref[...]
        part = 0.5 * jnp.sum(jnp.sum(err * err, axis=-1, keepdims=True) * (1.0 / n), axis=0, keepdims=True)
        dy = err * (1.0 / n)
        dy_ref[...] = dy
        dff, dg = _rms_bwd(ff_, g_ref[...], dy)
        dff_ref[...] = dff.astype(BF16)
        _acc(loss_ref, first, jnp.broadcast_to(part, (1, LANES)))
        _acc(dg_ref, first, dg)

    return pl.pallas_call(
        body, name="ffn_out", grid=(L // tl,),
        in_specs=[_row(tl, n), _row(tl, n), _row(tl, n), _full((1, n))],
        out_specs=[_full((1, LANES)), _row(tl, n), _row(tl, n), _full((1, n))],
        out_shape=[jax.ShapeDtypeStruct((1, LANES), F32), jax.ShapeDtypeStruct((L, n), F32),
                   jax.ShapeDtypeStruct((L, n), BF16), jax.ShapeDtypeStruct((1, n), F32)],
        compiler_params=_params("arbitrary"))(ff, x2, target, g_fpost)


def _post_bwd_call(x2, dhn2, dy, o, g_fpre, g_post):
    L, n = x2.shape
    tl = _fit(L, 512)

    def body(x2_ref, dh_ref, dy_ref, o_ref, gf_ref, gp_ref, dx2_ref, do_ref, dgf_ref, dgp_ref):
        first = pl.program_id(0) == 0
        d1, dgf = _rms_bwd(x2_ref[...], gf_ref[...], dh_ref[...])
        dx2 = dy_ref[...] + d1
        dx2_ref[...] = dx2
        do, dgp = _rms_bwd(o_ref[...], gp_ref[...], dx2)
        do_ref[...] = do.astype(BF16)
        _acc(dgf_ref, first, dgf)
        _acc(dgp_ref, first, dgp)

    return pl.pallas_call(
        body, name="post_bwd", grid=(L // tl,),
        in_specs=[_row(tl, n), _row(tl, n), _row(tl, n), _row(tl, n), _full((1, n)), _full((1, n))],
        out_specs=[_row(tl, n), _row(tl, n), _full((1, n)), _full((1, n))],
        out_shape=[jax.ShapeDtypeStruct((L, n), F32), jax.ShapeDtypeStruct((L, n), BF16),
                   jax.ShapeDtypeStruct((1, n), F32), jax.ShapeDtypeStruct((1, n), F32)],
        compiler_params=_params("arbitrary"))(x2, dhn2, dy, o, g_fpre, g_post)


def _pre_bwd_call(x, dhn1, dx2, g_pre):
    L, n = x.shape
    tl = _fit(L, 512)

    def body(x_ref, dh_ref, dx2_ref, g_ref, dx_ref, dg_ref):
        first = pl.program_id(0) == 0
        d1, dg = _rms_bwd(x_ref[...], g_ref[...], dh_ref[...])
        dx_ref[...] = dx2_ref[...] + d1
        _acc(dg_ref, first, dg)

    return pl.pallas_call(
        body, name="pre_bwd", grid=(L // tl,),
        in_specs=[_row(tl, n), _row(tl, n), _row(tl, n), _full((1, n))],
        out_specs=[_row(tl, n), _full((1, n))],
        out_shape=[jax.ShapeDtypeStruct((L, n), F32), jax.ShapeDtypeStruct((1, n), F32)],
        compiler_params=_params("arbitrary"))(x, dhn1, dx2, g_pre)


CONV_TILE = 256
HALO = 16


def _shift_down(cur, halo_tail, by):
    rolled = pltpu.roll(cur, by, 0)
    r8 = lax.broadcasted_iota(jnp.int32, halo_tail.shape, 0)
    head = jnp.where(r8 < by, pltpu.roll(halo_tail, by, 0), rolled[0:8])
    return jnp.concatenate([head, rolled[8:]], axis=0)


def _shift_up(cur, halo_head, by):
    n = cur.shape[0]
    rolled = pltpu.roll(cur, n - by, 0)
    r8 = lax.broadcasted_iota(jnp.int32, halo_head.shape, 0)
    tail = jnp.where(r8 >= 8 - by, pltpu.roll(halo_head, 8 - by, 0), rolled[n - 8:])
    return jnp.concatenate([rolled[:n - 8], tail], axis=0)


def _conv_fwd_vals(cur, halo, w, b, not_first):
    tail = halo[HALO - 8:] * not_first
    s1 = _shift_down(cur, tail, 1)
    s2 = _shift_down(cur, tail, 2)
    return b + w[2:3] * cur + w[1:2] * s1 + w[0:1] * s2, s1, s2


def _conv_specs(L, tl, nc, rows_inner):
    nh = tl // HALO
    if rows_inner:
        ij = lambda f: (lambda j, i: f(i, j))
    else:
        ij = lambda f: f
    cur = lambda off: pl.BlockSpec((tl, CONV_TILE), ij(lambda i, j: (i, off + j)))
    prev = lambda off: pl.BlockSpec((HALO, CONV_TILE), ij(lambda i, j: (jnp.maximum(i * nh - 1, 0), off + j)))
    nxt = lambda off: pl.BlockSpec((HALO, CONV_TILE), ij(lambda i, j: (jnp.minimum((i + 1) * nh, L // HALO - 1), off + j)))
    par = lambda rows, off: pl.BlockSpec((rows, CONV_TILE), ij(lambda i, j: (0, off + j)))
    return cur, prev, nxt, par


def _conv_act_call(h, conv_w, conv_b):
    L = h.shape[0]
    tl = _fit(L, 512)
    nc = D_FF // CONV_TILE
    cur, prev, _, par = _conv_specs(L, tl, nc, False)

    def body(hg_ref, hv_ref, pg_ref, pv_ref, wg_ref, wv_ref, bg_ref, bv_ref, a_ref):
        not_first = (pl.program_id(0) > 0).astype(F32)
        gate, _, _ = _conv_fwd_vals(hg_ref[...], pg_ref[...], wg_ref[...], bg_ref[...], not_first)
        val, _, _ = _conv_fwd_vals(hv_ref[...], pv_ref[...], wv_ref[...], bv_ref[...], not_first)
        a_ref[...] = (_gelu(gate) * val).astype(BF16)

    return pl.pallas_call(
        body, name="conv_act", grid=(L // tl, nc),
        in_specs=[cur(0), cur(nc), prev(0), prev(nc), par(3, 0), par(3, nc), par(1, 0), par(1, nc)],
        out_specs=cur(0), out_shape=jax.ShapeDtypeStruct((L, D_FF), BF16),
        compiler_params=_params("parallel", "parallel"))(h, h, h, h, conv_w, conv_w, conv_b, conv_b)


def _conv_act_bwd_call(da, h, conv_w, conv_b):
    L = h.shape[0]
    tl = _fit(L, 512)
    nc = D_FF // CONV_TILE
    cur, prev, _, par = _conv_specs(L, tl, nc, True)

    def body(da_ref, hg_ref, hv_ref, pg_ref, pv_ref, wg_ref, wv_ref, bg_ref, bv_ref,
             dg_ref, dv_ref, dwg_ref, dwv_ref, dbg_ref, dbv_ref):
        first = pl.program_id(1) == 0
        not_first = (pl.program_id(1) > 0).astype(F32)
        hg, hv = hg_ref[...], hv_ref[...]
        gate, g1, g2 = _conv_fwd_vals(hg, pg_ref[...], wg_ref[...], bg_ref[...], not_first)
        val, v1, v2 = _conv_fwd_vals(hv, pv_ref[...], wv_ref[...], bv_ref[...], not_first)
        da_ = da_ref[...]
        dgate = da_ * val * _gelu_grad(gate)
        dval = da_ * _gelu(gate)
        dg_ref[...] = dgate.astype(BF16)
        dv_ref[...] = dval.astype(BF16)
        col = lambda t: jnp.sum(t, axis=0, keepdims=True)
        _acc(dwg_ref, first, jnp.concatenate([col(dgate * g2), col(dgate * g1), col(dgate * hg)], axis=0))
        _acc(dwv_ref, first, jnp.concatenate([col(dval * v2), col(dval * v1), col(dval * hv)], axis=0))
        _acc(dbg_ref, first, col(dgate))
        _acc(dbv_ref, first, col(dval))

    act = jax.ShapeDtypeStruct((L, D_FF), BF16)
    w3 = jax.ShapeDtypeStruct((3, D_FF), F32)
    w1 = jax.ShapeDtypeStruct((1, D_FF), F32)
    return pl.pallas_call(
        body, name="conv_act_bwd", grid=(nc, L // tl),
        in_specs=[cur(0), cur(0), cur(nc), prev(0), prev(nc), par(3, 0), par(3, nc), par(1, 0), par(1, nc)],
        out_specs=[cur(0), cur(0), par(3, 0), par(3, 0), par(1, 0), par(1, 0)],
        out_shape=[act, act, w3, w3, w1, w1],
        compiler_params=_params("parallel", "arbitrary"))(da, h, h, h, h, conv_w, conv_w, conv_b, conv_b)


def _conv_t_call(dgate, dval, conv_w):
    L = dgate.shape[0]
    tl = _fit(L, 512)
    nc = D_FF // CONV_TILE
    cur, _, nxt, par = _conv_specs(L, tl, nc, False)

    def run(d, off, name):
        def body(d_ref, n_ref, w_ref, o_ref):
            not_last = (pl.program_id(0) < L // tl - 1).astype(F32)
            c = d_ref[...].astype(F32)
            head = n_ref[...].astype(F32)[0:8] * not_last
            w = w_ref[...]
            o_ref[...] = (w[2:3] * c + w[1:2] * _shift_up(c, head, 1) + w[0:1] * _shift_up(c, head, 2)).astype(BF16)

        return pl.pallas_call(
            body, name=name, grid=(L // tl, nc),
            in_specs=[cur(0), nxt(0), par(3, off)],
            out_specs=cur(0), out_shape=jax.ShapeDtypeStruct((L, D_FF), BF16),
            compiler_params=_params("parallel", "parallel"))(d, d, conv_w)

    return run(dgate, 0, "conv_t_gate"), run(dval, nc, "conv_t_val")


def _glu_call(y1, w_glu, b_glu):
    L, n = y1.shape
    tl = _fit(L, 512)

    def body(y_ref, w_ref, b_ref, o_ref):
        y2 = _gelu(y_ref[...])
        z = _dot(y2.astype(BF16), w_ref[...], NN) + b_ref[...]
        o_ref[...] = (y2 * _sigmoid(z)).astype(BF16)

    return pl.pallas_call(
        body, name="glu", grid=(L // tl,), in_specs=[_row(tl, n), _full((n, n)), _full((1, n))],
        out_specs=_row(tl, n), out_shape=jax.ShapeDtypeStruct((L, n), BF16),
        compiler_params=_params("parallel"))(y1, w_glu, b_glu)


def _glu_bwd_call(dout, y1, w_glu, b_glu):
    L, n = y1.shape
    tl = _fit(L, 512)

    def body(do_ref, y_ref, w_ref, b_ref, dy_ref, dw_ref, db_ref):
        first = pl.program_id(0) == 0
        y1_ = y_ref[...]
        y2 = _gelu(y1_)
        y2b = y2.astype(BF16)
        w = w_ref[...]
        sg = _sigmoid(_dot(y2b, w, NN) + b_ref[...])
        dout_ = do_ref[...].astype(F32)
        dz = dout_ * y2 * sg * (1.0 - sg)
        dzb = dz.astype(BF16)
        dy2 = dout_ * sg + _dot(dzb, w, NT)
        dy_ref[...] = dy2 * _gelu_grad(y1_)
        _acc(dw_ref, first, _dot(y2b, dzb, TN))
        _acc(db_ref, first, jnp.sum(dz, axis=0, keepdims=True))

    return pl.pallas_call(
        body, name="glu_bwd", grid=(L // tl,),
        in_specs=[_row(tl, n), _row(tl, n), _full((n, n)), _full((1, n))],
        out_specs=[_row(tl, n), _full((n, n)), _full((1, n))],
        out_shape=[jax.ShapeDtypeStruct((L, n), F32), jax.ShapeDtypeStruct((n, n), F32), jax.ShapeDtypeStruct((1, n), F32)],
        compiler_params=_params("arbitrary"))(dout, y1, w_glu, b_glu)


ATTN_TILE = 512
ATTN_SCALE = 1.0 / math.sqrt(QK_HEAD)


LOG2E = 1.0 / math.log(2.0)
Q_PRESCALE = ATTN_SCALE * LOG2E
ANY_SPEC = pl.BlockSpec(memory_space=pl.ANY)


def _attn_fwd_call(q, kv, blocks):
    L = q.shape[0]
    t = _fit(L, ATTN_TILE)
    nq = L // t
    n = len(blocks)

    def body(q_ref, k_ref, v_ref, *refs):
        blk_refs, (o_ref, lse_ref), gat_refs = refs[:n], refs[n:n + 2], refs[n + 2:2 * n + 2]
        m_s, acc_s, send_sems, recv_sems, local_sems = refs[2 * n + 2:]
        h, i = pl.program_id(0), pl.program_id(1)
        start, forward, finish = _gather_phases(blk_refs, gat_refs, send_sems, recv_sems, local_sems)
        pl.when(jnp.logical_and(h == 0, i == 0))(start)
        m_s[...] = jnp.full((t, 1), NEG, F32)
        acc_s[...] = jnp.zeros((t, LANES), F32)
        qv = q_ref[...]
        below = lax.broadcasted_iota(jnp.int32, (t, t), 1) <= lax.broadcasted_iota(jnp.int32, (t, t), 0)

        def block_step(kb, on_diagonal):
            off = pl.multiple_of(kb * t, t)
            s = _dot(qv, k_ref[pl.ds(off, t), :], NT)
            if on_diagonal:
                s = jnp.where(below, s, NEG)
            m_prev = m_s[...]
            m_new = jnp.maximum(m_prev, jnp.max(s, axis=1, keepdims=True))
            p = jnp.exp2(s - m_new)
            acc_s[...] = jnp.exp2(m_prev - m_new) * acc_s[...] + _dot(p.astype(BF16), v_ref[pl.ds(off, t), :], NN)
            m_s[...] = m_new

        def step(kb, carry):
            block_step(kb, False)
            return carry

        lax.fori_loop(0, i, step, 0)
        block_step(i, True)
        acc = acc_s[...]
        lane = lax.broadcasted_iota(jnp.int32, (t, LANES), 1)
        l = jnp.sum(jnp.where(lane == V_HEAD, acc, 0.0), axis=1, keepdims=True)
        o_ref[...] = (acc / l).astype(BF16)
        lse_ref[0] = m_s[...] + jnp.log(l) * LOG2E
        pl.when(jnp.logical_and(h == (3 * N_HEADS) // 4, i == 0))(forward)
        pl.when(jnp.logical_and(h == N_HEADS - 1, i == nq - 1))(finish)

    return pl.pallas_call(
        body, name="attn_fwd", grid=(N_HEADS, nq),
        in_specs=[pl.BlockSpec((t, LANES), lambda h, i: (i, h)),
                  pl.BlockSpec((L, LANES), lambda h, i: (0, 2 * h)),
                  pl.BlockSpec((L, LANES), lambda h, i: (0, 2 * h + 1))] + [ANY_SPEC] * n,
        out_specs=[pl.BlockSpec((t, LANES), lambda h, i: (i, h)),
                   pl.BlockSpec((1, t, 1), lambda h, i: (h, i, 0))] + [ANY_SPEC] * n,
        out_shape=[jax.ShapeDtypeStruct((L, HEAD_PAD), BF16), jax.ShapeDtypeStruct((N_HEADS, L, 1), F32)]
        + [jax.ShapeDtypeStruct((N_DEV,) + b.shape, b.dtype) for b in blocks],
        scratch_shapes=[pltpu.VMEM((t, 1), F32), pltpu.VMEM((t, LANES), F32)] + _comm_sems(n),
        compiler_params=_params("arbitrary", "arbitrary"))(q, kv, kv, *blocks)


def _attn_bwd_call(q, kv, o, do, lse, parts):
    L = q.shape[0]
    t = _fit(L, ATTN_TILE)
    nq = L // t
    n = len(parts)

    def body(q_ref, do_ref, o_ref, lse_ref, k_ref, v_ref, *refs):
        parts_refs, (dq_ref, dkv_ref), got_refs = refs[:n], refs[n:n + 2], refs[n + 2:2 * n + 2]
        dk_s, dv_s, send_sems, recv_sems, local_sems = refs[2 * n + 2:]
        h, j = pl.program_id(0), pl.program_id(1)
        start, finish = _exchange_phases(parts_refs, got_refs, send_sems, recv_sems, local_sems)
        pl.when(jnp.logical_and(h == 0, j == 0))(start)

        @pl.when(j == 0)
        def _():
            dq_ref[...] = jnp.zeros((L, LANES), F32)

        dk_s[...] = jnp.zeros((t, LANES), F32)
        dv_s[...] = jnp.zeros((t, LANES), F32)
        kblk, vblk = k_ref[...], v_ref[...]
        below = lax.broadcasted_iota(jnp.int32, (t, t), 1) <= lax.broadcasted_iota(jnp.int32, (t, t), 0)

        def block_step(i, on_diagonal):
            rows = pl.ds(pl.multiple_of(i * t, t), t)
            qi = q_ref[rows, :]
            doi = do_ref[rows, :]
            delta = jnp.sum(doi.astype(F32) * o_ref[rows, :].astype(F32), axis=1, keepdims=True)
            s = _dot(qi, kblk, NT)
            if on_diagonal:
                s = jnp.where(below, s, NEG)
            p = jnp.exp2(s - lse_ref[0, rows, :])
            dv_s[...] += _dot(p.astype(BF16), doi, TN)
            ds = (p * (_dot(doi, vblk, NT) - delta)).astype(BF16)
            dk_s[...] += _dot(ds, qi, TN)
            dq_ref[rows, :] += _dot(ds, kblk, NN) * ATTN_SCALE

        def step(i, carry):
            block_step(i, False)
            return carry

        block_step(j, True)
        lax.fori_loop(j + 1, nq, step, 0)
        dkv_ref[:, 0:LANES] = dk_s[...] * (1.0 / LOG2E)
        dkv_ref[:, LANES:2 * LANES] = dv_s[...]
        pl.when(jnp.logical_and(h == N_HEADS - 1, j == nq - 1))(finish)

    whole = lambda: pl.BlockSpec((L, LANES), lambda h, j: (0, h))
    return pl.pallas_call(
        body, name="attn_bwd", grid=(N_HEADS, nq),
        in_specs=[whole(), whole(), whole(), pl.BlockSpec((1, L, 1), lambda h, j: (h, 0, 0)),
                  pl.BlockSpec((t, LANES), lambda h, j: (j, 2 * h)),
                  pl.BlockSpec((t, LANES), lambda h, j: (j, 2 * h + 1))] + [ANY_SPEC] * n,
        out_specs=[whole(), pl.BlockSpec((t, 2 * LANES), lambda h, j: (j, h))] + [ANY_SPEC] * n,
        out_shape=[jax.ShapeDtypeStruct((L, HEAD_PAD), F32), jax.ShapeDtypeStruct((L, 2 * HEAD_PAD), F32)]
        + [jax.ShapeDtypeStruct(p.shape, p.dtype) for p in parts],
        scratch_shapes=[pltpu.VMEM((t, LANES), F32), pltpu.VMEM((t, LANES), F32)] + _comm_sems(n),
        compiler_params=_params("arbitrary", "arbitrary"))(q, do, o, lse, kv, kv, *parts)


def _disc(lr, li, ldt, br, bi):
    dt = jnp.exp(ldt)
    mag = jnp.exp(lr * dt)
    ang = li * dt
    a_re, a_im = mag * jnp.cos(ang), mag * jnp.sin(ang)
    den = lr * lr + li * li
    n_re, n_im = a_re - 1.0, a_im
    z_re = (n_re * lr + n_im * li) / den
    z_im = (n_im * lr - n_re * li) / den
    return a_re, a_im, z_re * br - z_im * bi, z_re * bi + z_im * br


def _disc_call(lr, li, ldt, br, bi):
    def body(lr_ref, li_ref, ldt_ref, br_ref, bi_ref, ar_ref, ai_ref, bbr_ref, bbi_ref):
        ar_ref[...], ai_ref[...], bbr_ref[...], bbi_ref[...] = _disc(
            lr_ref[...], li_ref[...], ldt_ref[...], br_ref[...], bi_ref[...])

    c1 = jax.ShapeDtypeStruct((SSM_NSTATE, 1), F32)
    c16 = jax.ShapeDtypeStruct((SSM_NSTATE, SSM_GROUP), F32)
    return pl.pallas_call(body, name="ssm_disc", out_shape=[c1, c1, c16, c16])(lr, li, ldt, br, bi)


def _disc_bwd_call(lr, li, ldt, br, bi, dar, dai, dbbr, dbbi):
    def body(lr_ref, li_ref, ldt_ref, br_ref, bi_ref, dar_ref, dai_ref, dbbr_ref, dbbi_ref,
             dlr_ref, dli_ref, dldt_ref, dbr_ref, dbi_ref):
        _, vjp = jax.vjp(_disc, lr_ref[...], li_ref[...], ldt_ref[...], br_ref[...], bi_ref[...])
        dlr_ref[...], dli_ref[...], dldt_ref[...], dbr_ref[...], dbi_ref[...] = vjp(
            (dar_ref[...], dai_ref[...], dbbr_ref[...], dbbi_ref[...]))

    c1 = jax.ShapeDtypeStruct((SSM_NSTATE, 1), F32)
    c16 = jax.ShapeDtypeStruct((SSM_NSTATE, SSM_GROUP), F32)
    return pl.pallas_call(body, name="ssm_disc_bwd", out_shape=[c1, c1, c1, c16, c16])(
        lr, li, ldt, br, bi, dar, dai, dbbr, dbbi)


SSM_ROWS = 512
SSM_CW = SSM_NSTATE // SSM_CHUNKS
SSM_CU = SSM_WIDTH // SSM_CHUNKS


def _cmul(ar, ai, br, bi):
    return ar * br - ai * bi, ar * bi + ai * br


def _power(ar1, ai1, n):
    def step(_, c):
        return _cmul(c[0], c[1], ar1, ai1)

    return lax.fori_loop(0, n, step, (jnp.ones_like(ar1), jnp.zeros_like(ar1)))


def _tile(k):
    return pl.ds(pl.multiple_of(k * 8, 8), 8)


def _ssm_fwd_call(u, a_re, a_im, bb_re, bb_im, cm_re, cm_im, d_skip):
    L = u.shape[0]
    seg = L // 8
    rb = _fit(L, SSM_ROWS)

    def body(u_ref, ar_ref, ai_ref, bbr_ref, bbi_ref, cmr_ref, cmi_ref, d_ref, y_ref, sre_hbm, sim_hbm,
             s_re, s_im, sems):
        q = pl.program_id(0)

        def bu_step(r, c):
            rows = pl.ds(pl.multiple_of(r * rb, rb), rb)
            ub = u_ref[rows, :].astype(BF16)
            s_re[rows, :] = _dot(ub, bbr_ref[0], NN)
            s_im[rows, :] = _dot(ub, bbi_ref[0], NN)
            return c

        lax.fori_loop(0, L // rb, bu_step, 0)
        ar1, ai1 = ar_ref[...], ai_ref[...]
        ar = jnp.broadcast_to(ar1, (8, SSM_CW))
        ai = jnp.broadcast_to(ai1, (8, SSM_CW))

        def local(k, c):
            nr, ni = _cmul(ar, ai, c[0], c[1])
            nr = nr + s_re[_tile(k), :]
            ni = ni + s_im[_tile(k), :]
            s_re[_tile(k), :] = nr
            s_im[_tile(k), :] = ni
            return nr, ni

        zero8 = jnp.zeros((8, SSM_CW), F32)
        lax.fori_loop(0, seg, local, (zero8, zero8))
        pr, pi = _power(ar1, ai1, seg)
        end_r = s_re[pl.ds((seg - 1) * 8, 8), :]
        end_i = s_im[pl.ds((seg - 1) * 8, 8), :]
        er = jnp.zeros((1, SSM_CW), F32)
        ei = jnp.zeros((1, SSM_CW), F32)
        rows_r, rows_i = [er], [ei]
        for j in range(7):
            tr, ti = _cmul(pr, pi, er, ei)
            er, ei = end_r[j:j + 1] + tr, end_i[j:j + 1] + ti
            rows_r.append(er)
            rows_i.append(ei)
        e_r = jnp.concatenate(rows_r, axis=0)
        e_i = jnp.concatenate(rows_i, axis=0)

        def fix(k, c):
            wr, wi = _cmul(c[0], c[1], ar, ai)
            fr, fi = _cmul(wr, wi, e_r, e_i)
            s_re[_tile(k), :] += fr
            s_im[_tile(k), :] += fi
            return wr, wi

        lax.fori_loop(0, seg, fix, (jnp.ones((8, SSM_CW), F32), zero8))
        out_r = pltpu.make_async_copy(s_re, sre_hbm.at[q], sems.at[0])
        out_i = pltpu.make_async_copy(s_im, sim_hbm.at[q], sems.at[1])
        out_r.start()
        out_i.start()

        def y_step(r, c):
            rows = pl.ds(pl.multiple_of(r * rb, rb), rb)
            y = _dot(s_re[rows, :].astype(BF16), cmr_ref[0], NN) - _dot(s_im[rows, :].astype(BF16), cmi_ref[0], NN)
            y_ref[rows, :] = y + d_ref[...] * u_ref[rows, :]
            return c

        lax.fori_loop(0, L // rb, y_step, 0)
        out_r.wait()
        out_i.wait()

    chunk = lambda rows, cols: pl.BlockSpec((rows, cols), lambda q: (0, q))
    mat = lambda r, c: pl.BlockSpec((1, r, c), lambda q: (q, 0, 0))
    anyspec = pl.BlockSpec(memory_space=pl.ANY)
    states = jax.ShapeDtypeStruct((SSM_CHUNKS, L, SSM_CW), F32)
    return pl.pallas_call(
        body, name="ssm_fwd", grid=(SSM_CHUNKS,),
        in_specs=[chunk(L, SSM_CU), chunk(1, SSM_CW), chunk(1, SSM_CW), mat(SSM_CU, SSM_CW), mat(SSM_CU, SSM_CW),
                  mat(SSM_CW, SSM_CU), mat(SSM_CW, SSM_CU), chunk(1, SSM_CU)],
        out_specs=[chunk(L, SSM_CU), anyspec, anyspec],
        out_shape=[jax.ShapeDtypeStruct((L, SSM_WIDTH), F32), states, states],
        scratch_shapes=[pltpu.VMEM((L, SSM_CW), F32), pltpu.VMEM((L, SSM_CW), F32), pltpu.SemaphoreType.DMA((2,))],
        compiler_params=_params("arbitrary", vmem=VMEM_BIG))(u, a_re, a_im, bb_re, bb_im, cm_re, cm_im, d_skip)


def _ssm_bwd_call(dy, u, s_re_all, s_im_all, a_re, a_im, bb_re, bb_im, cm_re, cm_im, d_skip):
    L = u.shape[0]
    seg = L // 8
    rb = _fit(L, SSM_ROWS)

    def body(dy_ref, u_ref, sre_hbm, sim_hbm, ar_ref, ai_ref, bbr_ref, bbi_ref, cmr_ref, cmi_ref, d_ref,
             du_ref, dbbr_ref, dbbi_ref, dcmr_ref, dcmi_ref, dar_ref, dai_ref, dd_ref,
             g_re, g_im, s_re, s_im, sems):
        q = pl.program_id(0)
        in_r = pltpu.make_async_copy(sre_hbm.at[q], s_re, sems.at[0])
        in_i = pltpu.make_async_copy(sim_hbm.at[q], s_im, sems.at[1])
        in_r.start()
        in_i.start()

        def ds_step(r, c):
            rows = pl.ds(pl.multiple_of(r * rb, rb), rb)
            dyb = dy_ref[rows, :].astype(BF16)
            g_re[rows, :] = _dot(dyb, cmr_ref[0], NT)
            g_im[rows, :] = -_dot(dyb, cmi_ref[0], NT)
            return c

        lax.fori_loop(0, L // rb, ds_step, 0)
        ar1, ai1 = ar_ref[...], ai_ref[...]
        ar = jnp.broadcast_to(ar1, (8, SSM_CW))
        nai = jnp.broadcast_to(-ai1, (8, SSM_CW))

        def local(kk, c):
            k = seg - 1 - kk
            nr, ni = _cmul(ar, nai, c[0], c[1])
            nr = nr + g_re[_tile(k), :]
            ni = ni + g_im[_tile(k), :]
            g_re[_tile(k), :] = nr
            g_im[_tile(k), :] = ni
            return nr, ni

        zero8 = jnp.zeros((8, SSM_CW), F32)
        lax.fori_loop(0, seg, local, (zero8, zero8))
        pr, pi = _power(ar1, -ai1, seg)
        head_r = g_re[pl.ds(0, 8), :]
        head_i = g_im[pl.ds(0, 8), :]
        fr = jnp.zeros((1, SSM_CW), F32)
        fi = jnp.zeros((1, SSM_CW), F32)
        rows_r, rows_i = [fr], [fi]
        for j in range(6, -1, -1):
            tr, ti = _cmul(pr, pi, fr, fi)
            fr, fi = head_r[j + 1:j + 2] + tr, head_i[j + 1:j + 2] + ti
            rows_r.insert(0, fr)
            rows_i.insert(0, fi)
        f_r = jnp.concatenate(rows_r, axis=0)
        f_i = jnp.concatenate(rows_i, axis=0)
        in_r.wait()
        in_i.wait()

        def fixed(k, wr, wi):
            xr, xi = _cmul(wr, wi, f_r, f_i)
            gr = g_re[_tile(k), :] + xr
            gi = g_im[_tile(k), :] + xi
            g_re[_tile(k), :] = gr
            g_im[_tile(k), :] = gi
            return gr, gi

        def fix(kk, c):
            k = seg - 1 - kk
            wr, wi = _cmul(c[0], c[1], ar, nai)
            gr, gi = fixed(k, wr, wi)
            pr_, pi_ = s_re[_tile(k - 1), :], s_im[_tile(k - 1), :]
            return wr, wi, c[2] + gr * pr_ + gi * pi_, c[3] + gi * pr_ - gr * pi_

        wr, wi, acc_r, acc_i = lax.fori_loop(0, seg - 1, fix, (jnp.ones((8, SSM_CW), F32), zero8, zero8, zero8))
        wr, wi = _cmul(wr, wi, ar, nai)
        gr, gi = fixed(0, wr, wi)
        row8 = lax.broadcasted_iota(jnp.int32, (8, SSM_CW), 0)
        pr_ = jnp.where(row8 > 0, pltpu.roll(s_re[pl.ds((seg - 1) * 8, 8), :], 1, 0), 0.0)
        pi_ = jnp.where(row8 > 0, pltpu.roll(s_im[pl.ds((seg - 1) * 8, 8), :], 1, 0), 0.0)
        acc_r = acc_r + gr * pr_ + gi * pi_
        acc_i = acc_i + gi * pr_ - gr * pi_
        dar_ref[...] = jnp.sum(acc_r, axis=0, keepdims=True)
        dai_ref[...] = jnp.sum(acc_i, axis=0, keepdims=True)

        dbbr_ref[...] = jnp.zeros((1, SSM_CU, SSM_CW), F32)
        dbbi_ref[...] = jnp.zeros((1, SSM_CU, SSM_CW), F32)
        dcmr_ref[...] = jnp.zeros((1, SSM_CW, SSM_CU), F32)
        dcmi_ref[...] = jnp.zeros((1, SSM_CW, SSM_CU), F32)
        dd_ref[...] = jnp.zeros((1, SSM_CU), F32)

        def grad_step(r, c):
            rows = pl.ds(pl.multiple_of(r * rb, rb), rb)
            ub, dyv = u_ref[rows, :], dy_ref[rows, :]
            ubb, dyb = ub.astype(BF16), dyv.astype(BF16)
            grb, gib = g_re[rows, :].astype(BF16), g_im[rows, :].astype(BF16)
            dbbr_ref[0] += _dot(ubb, grb, TN)
            dbbi_ref[0] += _dot(ubb, gib, TN)
            dcmr_ref[0] += _dot(s_re[rows, :].astype(BF16), dyb, TN)
            dcmi_ref[0] -= _dot(s_im[rows, :].astype(BF16), dyb, TN)
            du_ref[rows, :] = _dot(grb, bbr_ref[0], NT) + _dot(gib, bbi_ref[0], NT) + d_ref[...] * dyv
            dd_ref[...] += jnp.sum(dyv * ub, axis=0, keepdims=True)
            return c

        lax.fori_loop(0, L // rb, grad_step, 0)

    chunk = lambda rows, cols: pl.BlockSpec((rows, cols), lambda q: (0, q))
    mat = lambda r, c: pl.BlockSpec((1, r, c), lambda q: (q, 0, 0))
    anyspec = pl.BlockSpec(memory_space=pl.ANY)
    big = lambda: pltpu.VMEM((L, SSM_CW), F32)
    return pl.pallas_call(
        body, name="ssm_bwd", grid=(SSM_CHUNKS,),
        in_specs=[chunk(L, SSM_CU), chunk(L, SSM_CU), anyspec, anyspec, chunk(1, SSM_CW), chunk(1, SSM_CW),
                  mat(SSM_CU, SSM_CW), mat(SSM_CU, SSM_CW), mat(SSM_CW, SSM_CU), mat(SSM_CW, SSM_CU), chunk(1, SSM_CU)],
        out_specs=[chunk(L, SSM_CU), mat(SSM_CU, SSM_CW), mat(SSM_CU, SSM_CW), mat(SSM_CW, SSM_CU), mat(SSM_CW, SSM_CU),
                   chunk(1, SSM_CW), chunk(1, SSM_CW), chunk(1, SSM_CU)],
        out_shape=[jax.ShapeDtypeStruct((L, SSM_WIDTH), F32),
                   jax.ShapeDtypeStruct((SSM_CHUNKS, SSM_CU, SSM_CW), F32), jax.ShapeDtypeStruct((SSM_CHUNKS, SSM_CU, SSM_CW), F32),
                   jax.ShapeDtypeStruct((SSM_CHUNKS, SSM_CW, SSM_CU), F32), jax.ShapeDtypeStruct((SSM_CHUNKS, SSM_CW, SSM_CU), F32),
                   jax.ShapeDtypeStruct((1, SSM_NSTATE), F32), jax.ShapeDtypeStruct((1, SSM_NSTATE), F32),
                   jax.ShapeDtypeStruct((1, SSM_WIDTH), F32)],
        scratch_shapes=[big(), big(), big(), big(), pltpu.SemaphoreType.DMA((2,))],
        compiler_params=_params("arbitrary", vmem=VMEM_BIG))(
            dy, u, s_re_all, s_im_all, a_re, a_im, bb_re, bb_im, cm_re, cm_im, d_skip)


def _place():
    return lax.axis_index("x"), lax.axis_index("y"), lax.axis_index("c")


def _all_gather_call(blocks, name):
    n = len(blocks)

    def body(*refs):
        start, forward, finish = _gather_phases(refs[:n], refs[n:2 * n], *refs[2 * n:])
        start()
        forward()
        finish()

    return pl.pallas_call(
        body, name=name, in_specs=[ANY_SPEC] * n, out_specs=[ANY_SPEC] * n,
        out_shape=[jax.ShapeDtypeStruct((N_DEV,) + b.shape, b.dtype) for b in blocks],
        scratch_shapes=_comm_sems(n))(*blocks)


def _comm_sems(n):
    return [pltpu.SemaphoreType.DMA((7 * n,)), pltpu.SemaphoreType.DMA((7 * n,)), pltpu.SemaphoreType.DMA((n,))]


def _gather_phases(x_refs, out_refs, send_sems, recv_sems, local_sems):
    x, y, c = _place()
    me, sibling = (x, y, c), (x, y, 1 - c)
    chips = [(1 - x, y), (x, 1 - y), (1 - x, 1 - y)]
    n = len(x_refs)

    def copy(k, a, blk, to, from_input=False):
        slot = out_refs[a].at[4 * blk[0] + 2 * blk[1] + blk[2]]
        return pltpu.make_async_remote_copy(
            src_ref=x_refs[a] if from_input else slot, dst_ref=slot,
            send_sem=send_sems.at[k * n + a], recv_sem=recv_sems.at[k * n + a], device_id=to, device_id_type=MESH_ID)

    mine = [pltpu.make_async_copy(x_refs[a], out_refs[a].at[4 * x + 2 * y + c], local_sems.at[a]) for a in range(n)]
    first, passed = [], []
    for a in range(n):
        first.append(copy(0, a, me, sibling, True))
        first += [copy(1 + j, a, me, (*chip, c), True) for j, chip in enumerate(chips)]
        passed += [copy(4 + j, a, (*chip, c), sibling) for j, chip in enumerate(chips)]

    def start():
        for cp in mine + first:
            cp.start()

    def forward():
        for j, chip in enumerate(chips):
            for a in range(n):
                copy(1 + j, a, (*chip, c), me).wait_recv()
                passed[3 * a + j].start()

    def finish():
        for a in range(n):
            copy(0, a, sibling, me).wait_recv()
            for j, chip in enumerate(chips):
                copy(4 + j, a, (*chip, 1 - c), me).wait_recv()
        for cp in first + passed:
            cp.wait_send()
        for cp in mine:
            cp.wait()

    return start, forward, finish


def _exchange_call(parts, name):
    n = len(parts)

    def body(*refs):
        start, finish = _exchange_phases(refs[:n], refs[n:2 * n], *refs[2 * n:])
        start()
        finish()

    return pl.pallas_call(
        body, name=name, in_specs=[ANY_SPEC] * n, out_specs=[ANY_SPEC] * n,
        out_shape=[jax.ShapeDtypeStruct(p.shape, p.dtype) for p in parts], scratch_shapes=_comm_sems(n))(*parts)


def _exchange_phases(p_refs, out_refs, send_sems, recv_sems, local_sems):
    x, y, c = _place()
    me = 4 * x + 2 * y + c
    n = len(p_refs)

    def flip(k):
        px = 1 - x if k & 4 else x
        py = 1 - y if k & 2 else y
        pc = 1 - c if k & 1 else c
        return (px, py, pc), 4 * px + 2 * py + pc

    def copy(k, a, landing):
        peer, peer_slot = flip(k)
        return pltpu.make_async_remote_copy(
            src_ref=p_refs[a].at[peer_slot], dst_ref=out_refs[a].at[peer_slot if landing else me],
            send_sem=send_sems.at[(k - 1) * n + a], recv_sem=recv_sems.at[(k - 1) * n + a],
            device_id=peer, device_id_type=MESH_ID)

    mine = [pltpu.make_async_copy(p_refs[a].at[me], out_refs[a].at[me], local_sems.at[a]) for a in range(n)]
    sends = [copy(k, a, False) for k in range(1, N_DEV) for a in range(n)]

    def start():
        for cp in mine + sends:
            cp.start()

    def finish():
        for k in range(1, N_DEV):
            for a in range(n):
                copy(k, a, True).wait_recv()
        for cp in sends:
            cp.wait_send()
        for cp in mine:
            cp.wait()

    return start, finish


def _adam_math(g, w, m, v):
    c1 = 1.0 / (1.0 - ADAM_B1 ** ADAM_STEP)
    c2 = 1.0 / (1.0 - ADAM_B2 ** ADAM_STEP)
    m_new = ADAM_B1 * m + (1.0 - ADAM_B1) * g
    v_new = ADAM_B2 * v + (1.0 - ADAM_B2) * (g * g)
    delta = -ADAM_LR * ((m_new * c1) / (jnp.sqrt(v_new * c2) + ADAM_EPS) + ADAM_WD * w)
    return g, delta, m_new, v_new


def _sum_slices(s_ref):
    g = s_ref[0].astype(F32)
    for k in range(1, N_DEV):
        g = g + s_ref[k].astype(F32)
    return g


def _adam_call(slices, w, m, v, name):
    rest = w.shape[2:]
    t1 = _fit(w.shape[1], 256, 16) if len(rest) == 1 else _fit(w.shape[1], 8, 8)
    zeros = (0,) * len(rest)

    def body(s_ref, w_ref, m_ref, v_ref, g_ref, d_ref, mo_ref, vo_ref):
        g_ref[...], d_ref[...], mo_ref[...], vo_ref[...] = _adam_math(_sum_slices(s_ref), w_ref[...], m_ref[...], v_ref[...])

    own = pl.BlockSpec((1, t1) + rest, lambda i: (0, i) + zeros)
    out = jax.ShapeDtypeStruct(w.shape, F32)
    return pl.pallas_call(
        body, name=name, grid=(w.shape[1] // t1,),
        in_specs=[pl.BlockSpec((N_DEV, 1, t1) + rest, lambda i: (0, 0, i) + zeros), own, own, own],
        out_specs=[own, own, own, own], out_shape=[out, out, out, out],
        compiler_params=_params("parallel"))(slices, w, m, v)


def _adam_small_call(slices, ws, ms, vs):
    n = len(ws)

    def body(*refs):
        outs = refs[4 * n:]
        for a in range(n):
            res = _adam_math(_sum_slices(refs[a]), refs[n + a][...], refs[2 * n + a][...], refs[3 * n + a][...])
            for r in range(4):
                outs[4 * a + r][...] = res[r]

    flat = pl.pallas_call(
        body, name="adam_small",
        out_shape=[jax.ShapeDtypeStruct(w.shape, F32) for w in ws for _ in range(4)])(*slices, *ws, *ms, *vs)
    return [flat[4 * a:4 * a + 4] for a in range(n)]


BIG = (("w_in", 1024, 404, 1), ("w_uq", 384, 96, 1), ("w_uk", 256, 64, 1), ("w_uv", 256, 64, 1),
       ("w_glu", 64, 512, 0), ("w_branch_attn", 512, 128, 1), ("w_branch_ssm", 512, 128, 1),
       ("w_out", 128, 1024, 0), ("w_up", 1024, 704, 1), ("w_down", 352, 1024, 0), ("conv_w", 3, 704, 1))
BIG_MIX, BIG_FFN = BIG[:8], BIG[8:]
SMALL = (("mix_norm_pre", (1024,)), ("q_norm", (384,)), ("kv_norm", (256,)), ("ssm_lambda_re", (32, 64)),
         ("ssm_lambda_im", (32, 64)), ("ssm_log_dt", (32,)), ("ssm_b_re", (32, 64, 16)), ("ssm_b_im", (32, 64, 16)),
         ("ssm_c_re", (32, 16, 64)), ("ssm_c_im", (32, 16, 64)), ("ssm_d", (32, 16)), ("b_glu", (512,)),
         ("b_gate", (2048,)), ("mix_norm_post", (1024,)), ("ffn_norm_pre", (1024,)), ("conv_b", (5632,)),
         ("ffn_norm_post", (1024,)))


def _to_slices(full, rows, cols, axis):
    if axis == 1:
        return full.reshape(rows, N_DEV, cols).transpose(1, 0, 2)
    return full.reshape(N_DEV, rows, cols)


def _from_slices(parts, rows, cols, axis):
    if axis == 1:
        return parts.transpose(1, 0, 2).reshape(rows, N_DEV * cols)
    return parts.reshape(N_DEV * rows, cols)


def _head_pad_cols(w, width):
    k = w.shape[0]
    return jnp.pad(w.reshape(k, N_HEADS, width), ((0, 0), (0, 0), (0, LANES - width))).reshape(k, HEAD_PAD)


def _head_unpad_cols(w, width):
    k = w.shape[0]
    return w.reshape(k, N_HEADS, LANES)[:, :, :width].reshape(k, N_HEADS * width)


def _time_perm(a, L):
    return a.reshape(8, L // 8, a.shape[-1]).transpose(1, 0, 2).reshape(L, a.shape[-1])


def _time_unperm(a, L):
    return a.reshape(L // 8, 8, a.shape[-1]).transpose(1, 0, 2).reshape(L, a.shape[-1])


def _block_diag(w, rows_first):
    eye = jnp.eye(8, dtype=w.dtype)
    g = w.reshape(SSM_CHUNKS, 8, w.shape[1], w.shape[2])
    return jnp.einsum("qgrc,gk->qgrkc", g, eye).reshape(SSM_CHUNKS, 8 * w.shape[1], 8 * w.shape[2])


def _block_diag_t(m, r, c):
    eye = jnp.eye(8, dtype=m.dtype)
    return jnp.einsum("qgrkc,gk->qgrc", m.reshape(SSM_CHUNKS, 8, r, 8, c), eye).reshape(SSM_GROUPS, r, c)


def kernel(x, positions, mix_norm_pre, w_in, q_norm, w_uq, kv_norm, w_uk, w_uv, ssm_lambda_re, ssm_lambda_im, ssm_log_dt, ssm_b_re, ssm_b_im, ssm_c_re, ssm_c_im, ssm_d, w_glu, b_glu, w_branch_attn, w_branch_ssm, b_gate, w_out, mix_norm_post, ffn_norm_pre, w_up, conv_w, conv_b, w_down, ffn_norm_post, loss_target, m_mix_norm_pre, m_w_in, m_q_norm, m_w_uq, m_kv_norm, m_w_uk, m_w_uv, m_ssm_lambda_re, m_ssm_lambda_im, m_ssm_log_dt, m_ssm_b_re, m_ssm_b_im, m_ssm_c_re, m_ssm_c_im, m_ssm_d, m_w_glu, m_b_glu, m_w_branch_attn, m_w_branch_ssm, m_b_gate, m_w_out, m_mix_norm_post, m_ffn_norm_pre, m_w_up, m_conv_w, m_conv_b, m_w_down, m_ffn_norm_post, v_mix_norm_pre, v_w_in, v_q_norm, v_w_uq, v_kv_norm, v_w_uk, v_w_uv, v_ssm_lambda_re, v_ssm_lambda_im, v_ssm_log_dt, v_ssm_b_re, v_ssm_b_im, v_ssm_c_re, v_ssm_c_im, v_ssm_d, v_w_glu, v_b_glu, v_w_branch_attn, v_w_branch_ssm, v_b_gate, v_w_out, v_mix_norm_post, v_ffn_norm_pre, v_w_up, v_conv_w, v_conv_b, v_w_down, v_ffn_norm_post):
    given = dict(locals())
    L = x.shape[1]
    xs = x[0]
    target = loss_target[0]

    def shard_bits(group):
        return [given[name][0] if name == "conv_w" else given[name][0].astype(BF16) for name, _, _, _ in group]

    W = {}

    def unpack_weights(gathered, group):
        for (name, rows, cols, axis), parts in zip(group, gathered):
            W[name] = _from_slices(parts, rows, cols, axis)

    unpack_weights(_all_gather_call(shard_bits(BIG_MIX), "gather_weights"), BIG_MIX)

    wi = W["w_in"]
    kr_cols = jnp.pad(wi[:, 640:672], ((0, 0), (QK_NOPE, LANES - QK_HEAD)))
    w_in_p = jnp.concatenate([wi[:, :640], kr_cols, wi[:, 672:]], axis=1)
    w_uq_p = _head_pad_cols(W["w_uq"], QK_HEAD)
    w_kv_p = jnp.stack([_head_pad_cols(W["w_uk"], QK_NOPE).reshape(KV_RANK, N_HEADS, LANES),
                        _head_pad_cols(W["w_uv"], V_HEAD).reshape(KV_RANK, N_HEADS, LANES)], axis=2
                       ).reshape(KV_RANK, 2 * HEAD_PAD)
    w_ba_p = jnp.pad(W["w_branch_attn"].reshape(N_HEADS, V_HEAD, D_MODEL), ((0, 0), (0, LANES - V_HEAD), (0, 0))
                     ).reshape(HEAD_PAD, D_MODEL)

    hn1 = _rms_fwd_call(xs, mix_norm_pre, "rms_pre")
    proj = _mm(hn1, w_in_p, "mm_in", tn=256)
    qn, ckvn = _mla_norms_call(proj, q_norm, kv_norm)
    q_pad = _mm(qn, w_uq_p, "mm_uq")
    kv_pad = _mm(ckvn, w_kv_p, "mm_ukv")
    half = jnp.arange(QK_ROPE // 2, dtype=F32)
    inv_freq = ROPE_THETA ** (-2.0 * half / QK_ROPE)
    inv_freq = jnp.pad(jnp.concatenate([inv_freq, inv_freq]), (QK_NOPE, LANES - QK_HEAD)).reshape(1, LANES)
    pos_col = positions.astype(F32).reshape(L, 1)
    q_r, kv_r, cosf, sinf = _mla_prep_call(q_pad, kv_pad, proj, pos_col, inv_freq)
    attn, lse, *gathered_ffn = _attn_fwd_call(q_r, kv_r, shard_bits(BIG_FFN))
    unpack_weights(gathered_ffn, BIG_FFN)

    col = lambda a: a.reshape(SSM_NSTATE, -1)
    lr_c, li_c = col(ssm_lambda_re[0]), col(ssm_lambda_im[0])
    ldt_c = col(jnp.broadcast_to(ssm_log_dt[0][:, None], (SSM_GROUPS, SSM_STATE)))
    br_c, bi_c = col(ssm_b_re[0]), col(ssm_b_im[0])
    a_re_c, a_im_c, bb_re_c, bb_im_c = _disc_call(lr_c, li_c, ldt_c, br_c, bi_c)
    a_re, a_im = a_re_c.reshape(1, SSM_NSTATE), a_im_c.reshape(1, SSM_NSTATE)
    to_bb = lambda b: _block_diag(b.reshape(SSM_GROUPS, SSM_STATE, SSM_GROUP).transpose(0, 2, 1), True).astype(BF16)
    bb_re, bb_im = to_bb(bb_re_c), to_bb(bb_im_c)
    to_cm = lambda c_: _block_diag(c_[0].transpose(0, 2, 1), True).astype(BF16)
    cm_re, cm_im = to_cm(ssm_c_re), to_cm(ssm_c_im)
    d_skip = ssm_d.reshape(1, SSM_WIDTH)
    u_p = _time_perm(proj[:, P_U:P_GATE], L)
    y1, s_re, s_im = _ssm_fwd_call(u_p, a_re, a_im, bb_re, bb_im, cm_re, cm_im, d_skip)
    w_glu_b = W["w_glu"]
    ssm_p = _glu_call(y1, w_glu_b, b_glu)
    ssm = _time_unperm(ssm_p, L)

    pa = _mm(attn, w_ba_p, "mm_ba")
    ps = _mm(ssm, W["w_branch_ssm"], "mm_bs")
    merged = _merge_call(proj, b_gate, pa, ps)
    o = _mm(merged, W["w_out"], "mm_out")
    x2, hn2 = _post_mix_call(o, xs, mix_norm_post, ffn_norm_pre)
    h = _mm(hn2, W["w_up"], "mm_up")
    cw = W["conv_w"]
    act = _conv_act_call(h, cw, conv_b)
    ff = _mm(act, W["w_down"], "mm_down", tk=1408)
    loss_row, dy, dff, g_ffn_norm_post = _ffn_out_call(ff, x2, target, ffn_norm_post)
    loss = lax.psum(loss_row[0, 0], ("x", "y", "c"))

    da = _mm(dff, W["w_down"], "mm_down_dx", tb=True, tn=256)
    g_w_down = _mm(act, dff, "mm_down_dw", ta=True, out_dtype=BF16, tm=256, tn=1024)
    dgate, dval, dcw_g, dcw_v, dcb_g, dcb_v = _conv_act_bwd_call(da, h, cw, conv_b)
    g_conv_w = jnp.concatenate([dcw_g, dcw_v], axis=1)
    g_conv_b = jnp.concatenate([dcb_g, dcb_v], axis=1)
    dh_g, dh_v = _conv_t_call(dgate, dval, cw)
    dh = jnp.concatenate([dh_g, dh_v], axis=1)
    dhn2 = _mm(dh, W["w_up"], "mm_up_dx", tb=True, tk=1408)
    g_w_up = _mm(hn2, dh, "mm_up_dw", ta=True, out_dtype=BF16, tm=512)
    dx2, do, g_ffn_norm_pre, g_mix_norm_post = _post_bwd_call(x2, dhn2, dy, o, ffn_norm_pre, mix_norm_post)
    dmerged = _mm(do, W["w_out"], "mm_out_dx", tb=True)
    g_w_out = _mm(merged, do, "mm_out_dw", ta=True, out_dtype=BF16, tm=512, tn=1024)
    dpa, dps, dl0, dl1, db0, db1 = _merge_bwd_call(dmerged, proj, b_gate, pa, ps)
    g_b_gate = jnp.concatenate([db0, db1], axis=1)
    dattn = _mm(dpa, w_ba_p, "mm_ba_dx", tb=True, out_dtype=BF16)
    g_w_ba = _mm(attn, dpa, "mm_ba_dw", ta=True, out_dtype=BF16, tm=512, tn=1024).reshape(N_HEADS, LANES, D_MODEL)[:, :V_HEAD].reshape(N_HEADS * V_HEAD, D_MODEL)
    dssm = _mm(dps, W["w_branch_ssm"], "mm_bs_dx", tb=True)
    g_w_bs = _mm(ssm, dps, "mm_bs_dw", ta=True, out_dtype=BF16, tm=512, tn=1024)

    dy1, g_w_glu, g_b_glu = _glu_bwd_call(_time_perm(dssm, L), y1, w_glu_b, b_glu)
    du_p, dbb_re, dbb_im, dcm_re, dcm_im, da_re, da_im, g_ssm_d = _ssm_bwd_call(
        dy1, u_p, s_re, s_im, a_re, a_im, bb_re, bb_im, cm_re, cm_im, d_skip)
    du = _time_unperm(du_p, L)
    from_bb = lambda m: col(_block_diag_t(m, SSM_GROUP, SSM_STATE).transpose(0, 2, 1))
    dlr, dli, dldt, dbr, dbi = _disc_bwd_call(
        lr_c, li_c, ldt_c, br_c, bi_c, da_re.reshape(SSM_NSTATE, 1), da_im.reshape(SSM_NSTATE, 1), from_bb(dbb_re), from_bb(dbb_im))
    g_c_re = _block_diag_t(dcm_re, SSM_STATE, SSM_GROUP).transpose(0, 2, 1)
    g_c_im = _block_diag_t(dcm_im, SSM_STATE, SSM_GROUP).transpose(0, 2, 1)

    def grad_slices(group, grads):
        return [_to_slices(grads[name], rows, cols, axis) for name, rows, cols, axis in group]

    ffn_grads = {"w_up": g_w_up, "w_down": g_w_down, "conv_w": g_conv_w}
    dq, dkv, *received_ffn = _attn_bwd_call(q_r, kv_r, attn, dattn, lse, grad_slices(BIG_FFN, ffn_grads))
    dq_p, dkv_p, dkr_p = _mla_prep_bwd_call(dq, dkv, cosf, sinf)
    dqn = _mm(dq_p, w_uq_p, "mm_uq_dx", tb=True)
    g_w_uq = _head_unpad_cols(_mm(qn, dq_p, "mm_uq_dw", ta=True, out_dtype=BF16, tn=1024), QK_HEAD)
    dckvn = _mm(dkv_p, w_kv_p, "mm_ukv_dx", tb=True)
    g_w_kv = _mm(ckvn, dkv_p, "mm_ukv_dw", ta=True, out_dtype=BF16, tn=1024).reshape(KV_RANK, N_HEADS, 2, LANES)
    g_w_uk = g_w_kv[:, :, 0, :QK_NOPE].reshape(KV_RANK, N_HEADS * QK_NOPE)
    g_w_uv = g_w_kv[:, :, 1, :V_HEAD].reshape(KV_RANK, N_HEADS * V_HEAD)
    dcqkv, g_q_norm, g_kv_norm = _mla_norms_bwd_call(proj, dqn, dckvn, q_norm, kv_norm)
    dproj = jnp.concatenate([dcqkv, dkr_p, du.astype(BF16), dl0, dl1], axis=1)
    dhn1 = _mm(dproj, w_in_p, "mm_in_dx", tb=True, tk=1664)
    g_w_in_p = _mm(hn1, dproj, "mm_in_dw", ta=True, out_dtype=BF16, tm=512, tn=256)
    g_w_in = jnp.concatenate([g_w_in_p[:, :640], g_w_in_p[:, 640 + QK_NOPE:640 + QK_HEAD], g_w_in_p[:, 768:]], axis=1)
    grad_x, g_mix_norm_pre = _pre_bwd_call(xs, dhn1, dx2, mix_norm_pre)

    mix_grads = {"w_in": g_w_in, "w_uq": g_w_uq, "w_uk": g_w_uk, "w_uv": g_w_uv, "w_glu": g_w_glu.astype(BF16),
                 "w_branch_attn": g_w_ba, "w_branch_ssm": g_w_bs, "w_out": g_w_out}
    received_mix = _exchange_call(grad_slices(BIG_MIX, mix_grads), "exchange_grads")
    results = {}
    for group, received in ((BIG_FFN, received_ffn), (BIG_MIX, received_mix)):
        for (name, _, _, _), rec in zip(group, received):
            results[name] = _adam_call(rec[:, None], given[name], given["m_" + name], given["v_" + name], "adam_" + name)

    small_grads = {"mix_norm_pre": g_mix_norm_pre, "q_norm": g_q_norm, "kv_norm": g_kv_norm,
                   "ssm_lambda_re": dlr, "ssm_lambda_im": dli,
                   "ssm_log_dt": jnp.sum(dldt.reshape(SSM_GROUPS, SSM_STATE), axis=1),
                   "ssm_b_re": dbr, "ssm_b_im": dbi, "ssm_c_re": g_c_re, "ssm_c_im": g_c_im, "ssm_d": g_ssm_d,
                   "b_glu": g_b_glu, "b_gate": g_b_gate, "mix_norm_post": g_mix_norm_post,
                   "ffn_norm_pre": g_ffn_norm_pre, "conv_b": g_conv_b, "ffn_norm_post": g_ffn_norm_post}
    partials = [small_grads[name].reshape((1,) + shp) for name, shp in SMALL]
    all_partials = dict(zip([n for n, _ in SMALL], _all_gather_call(partials, "gather_small_grads")))
    rows_like = [n for n, shp in SMALL if len(shp) < 3]
    for name, res in zip(rows_like, _adam_small_call(
            [all_partials[n] for n in rows_like], [given[n] for n in rows_like],
            [given["m_" + n] for n in rows_like], [given["v_" + n] for n in rows_like])):
        results[name] = res
    for name, shp in SMALL:
        if len(shp) == 3:
            results[name] = _adam_call(all_partials[name], given[name], given["m_" + name], given["v_" + name], "adam_" + name)

    order = ["mix_norm_pre", "w_in", "q_norm", "w_uq", "kv_norm", "w_uk", "w_uv", "ssm_lambda_re", "ssm_lambda_im",
             "ssm_log_dt", "ssm_b_re", "ssm_b_im", "ssm_c_re", "ssm_c_im", "ssm_d", "w_glu", "b_glu", "w_branch_attn",
             "w_branch_ssm", "b_gate", "w_out", "mix_norm_post", "ffn_norm_pre", "w_up", "conv_w", "conv_b", "w_down",
             "ffn_norm_post"]
    outs = [loss, grad_x[None]]
    for kind in range(4):
        outs += [results[name][kind] for name in order]
    return tuple(outs)
```

```python
import math

import jax
import jax.numpy as jnp
from jax import lax
from jax.experimental import pallas as pl
from jax.experimental.pallas import tpu as pltpu

F32 = jnp.float32
BF16 = jnp.bfloat16
MESH_ID = pl.DeviceIdType.MESH

N_DEV = 8
LANES = 128
D_MODEL = 1024
N_HEADS = 8
QK_NOPE = 64
QK_ROPE = 32
QK_HEAD = QK_NOPE + QK_ROPE
V_HEAD = 64
Q_RANK = 384
KV_RANK = 256
ROPE_THETA = 10000.0
SSM_WIDTH = 512
SSM_GROUP = 16
SSM_GROUPS = 32
SSM_STATE = 64
SSM_NSTATE = SSM_GROUPS * SSM_STATE
SSM_CHUNKS = 4
D_FF = 2816
EPS = 1e-6
ADAM_LR, ADAM_B1, ADAM_B2, ADAM_EPS, ADAM_WD, ADAM_STEP = 0.001, 0.9, 0.999, 1e-08, 0.01, 10

P_CQ, P_CKV, P_KR, P_U, P_GATE = 0, 384, 640, 768, 1280
P_IN = P_GATE + 2 * D_MODEL
HEAD_PAD = N_HEADS * LANES

PACK_ROWS = 1024
VMEM_BIG = 52 * 1024 * 1024

_GELU_C0 = math.sqrt(2.0 / math.pi)
_GELU_C1 = 0.044715
NEG = -1e30


def _fit(n, pref, mult=LANES):
    if n <= pref:
        return n
    t = (pref // mult) * mult
    while t > 0 and n % t:
        t -= mult
    assert t > 0, (n, pref, mult)
    return t


def _gelu(x):
    return 0.5 * x * (1.0 + jnp.tanh(_GELU_C0 * (x + _GELU_C1 * x * x * x)))


def _gelu_grad(x):
    x2 = x * x
    t = jnp.tanh(_GELU_C0 * x * (1.0 + _GELU_C1 * x2))
    return 0.5 * (1.0 + t) + 0.5 * x * (1.0 - t * t) * _GELU_C0 * (1.0 + 3.0 * _GELU_C1 * x2)


def _sigmoid(x):
    return 1.0 / (1.0 + jnp.exp(-x))


def _dot(a, b, dims):
    return lax.dot_general(a, b, (dims, ((), ())), preferred_element_type=F32)


NN = ((1,), (0,))
NT = ((1,), (1,))
TN = ((0,), (0,))


def _params(*sem, vmem=None):
    return pltpu.CompilerParams(dimension_semantics=tuple(sem), vmem_limit_bytes=vmem)


def _mm(a, b, name, ta=False, tb=False, out_dtype=F32, tm=1024, tn=512, tk=1024, exchange=()):
    if ta:
        K, M = a.shape
    else:
        M, K = a.shape
    if tb:
        N, K2 = b.shape
    else:
        K2, N = b.shape
    assert K == K2, (a.shape, b.shape, ta, tb)
    tm, tn, tk = _fit(M, tm), _fit(N, tn), _fit(K, tk)
    nk = K // tk
    grid = (M // tm, N // tn, nk)
    dims = ((0,) if ta else (1,), (1,) if tb else (0,))
    n = len(exchange)

    def body(a_ref, b_ref, *refs):
        o_ref, scratch = refs[n], refs[2 * n + 1:]
        step = (pl.program_id(0) * grid[1] + pl.program_id(1)) * grid[2] + pl.program_id(2)
        if n:
            start, finish = _exchange_phases(refs[:n], refs[n + 1:2 * n + 1], *scratch[-3:])
            pl.when(step == 0)(start)
        part = _dot(a_ref[...].astype(BF16), b_ref[...].astype(BF16), dims)
        if nk == 1:
            o_ref[...] = part.astype(out_dtype)
        else:
            acc_ref = scratch[0]
            k = pl.program_id(2)

            @pl.when(k == 0)
            def _():
                acc_ref[...] = part

            @pl.when(k > 0)
            def _():
                acc_ref[...] += part

            @pl.when(k == nk - 1)
            def _():
                o_ref[...] = acc_ref[...].astype(out_dtype)
        if n:
            pl.when(step == grid[0] * grid[1] * grid[2] - 1)(finish)

    a_spec = pl.BlockSpec((tk, tm), lambda i, j, k: (k, i)) if ta else pl.BlockSpec((tm, tk), lambda i, j, k: (i, k))
    b_spec = pl.BlockSpec((tn, tk), lambda i, j, k: (j, k)) if tb else pl.BlockSpec((tk, tn), lambda i, j, k: (k, j))
    out = pl.pallas_call(
        body, name=name, grid=grid,
        in_specs=[a_spec, b_spec] + [ANY_SPEC] * n,
        out_specs=[pl.BlockSpec((tm, tn), lambda i, j, k: (i, j))] + [ANY_SPEC] * n,
        out_shape=[jax.ShapeDtypeStruct((M, N), out_dtype)] + [jax.ShapeDtypeStruct(p.shape, p.dtype) for p in exchange],
        scratch_shapes=([] if nk == 1 else [pltpu.VMEM((tm, tn), F32)]) + (_comm_sems(n) if n else []),
        compiler_params=_params(*(("arbitrary",) * 3 if n else ("parallel", "parallel", "arbitrary"))))(a, b, *exchange)
    return out if n else out[0]


def _row(tl, n, col=0):
    return pl.BlockSpec((tl, n), lambda i: (i, col))


def _full(shape):
    return pl.BlockSpec(shape, lambda i: (0,) * len(shape))


def _rms(x, g):
    r = lax.rsqrt(jnp.mean(x * x, axis=-1, keepdims=True) + EPS)
    return x * r * g


def _rms_bwd(x, g, dy):
    n = x.shape[-1]
    r = lax.rsqrt(jnp.mean(x * x, axis=-1, keepdims=True) + EPS)
    gy = dy * g
    dx = r * gy - x * (r * r * r * (1.0 / n)) * jnp.sum(x * gy, axis=-1, keepdims=True)
    return dx, jnp.sum(dy * x * r, axis=0, keepdims=True)


def _acc(ref, first, val):
    @pl.when(first)
    def _():
        ref[...] = val

    @pl.when(jnp.logical_not(first))
    def _():
        ref[...] += val


def _rms_fwd_call(x, g, name):
    L, n = x.shape
    tl = _fit(L, 512)

    def body(x_ref, g_ref, o_ref):
        o_ref[...] = _rms(x_ref[...], g_ref[...]).astype(BF16)

    return pl.pallas_call(
        body, name=name, grid=(L // tl,), in_specs=[_row(tl, n), _full((1, n))], out_specs=_row(tl, n),
        out_shape=jax.ShapeDtypeStruct((L, n), BF16), compiler_params=_params("parallel"))(x, g)


def _mla_norms_call(proj, q_norm, kv_norm):
    L = proj.shape[0]
    tl = _fit(L, 512)

    def body(p_ref, gq_ref, gk_ref, qn_ref, kn_ref):
        p = p_ref[...]
        qn_ref[...] = _rms(p[:, P_CQ:P_CKV], gq_ref[...]).astype(BF16)
        kn_ref[...] = _rms(p[:, P_CKV:P_KR], gk_ref[...]).astype(BF16)

    return pl.pallas_call(
        body, name="mla_norms", grid=(L // tl,),
        in_specs=[_row(tl, P_KR), _full((1, Q_RANK)), _full((1, KV_RANK))],
        out_specs=[_row(tl, Q_RANK), _row(tl, KV_RANK)],
        out_shape=[jax.ShapeDtypeStruct((L, Q_RANK), BF16), jax.ShapeDtypeStruct((L, KV_RANK), BF16)],
        compiler_params=_params("parallel"))(proj, q_norm, kv_norm)


def _rope_lanes(shape):
    lane = lax.broadcasted_iota(jnp.int32, shape, 1)
    return lane, jnp.logical_and(lane >= QK_NOPE, lane < QK_HEAD)


def _rope_apply(x, cosf, sinf, lane):
    rot = jnp.where(lane < QK_NOPE + QK_ROPE // 2, -pltpu.roll(x, LANES - QK_ROPE // 2, 1), pltpu.roll(x, QK_ROPE // 2, 1))
    return x * cosf + rot * sinf


def _rope_apply_t(dy, cosf, sinf, lane, is_rope):
    g = dy * sinf
    rot_t = jnp.where(lane < QK_NOPE + QK_ROPE // 2, pltpu.roll(g, LANES - QK_ROPE // 2, 1), -pltpu.roll(g, QK_ROPE // 2, 1))
    return dy * cosf + jnp.where(is_rope, rot_t, 0.0)


def _mla_prep_call(q_pad, kv_pad, proj, pos_col, inv_freq):
    L = q_pad.shape[0]
    tl = _fit(L, 512)

    def body(q_ref, kv_ref, kr_ref, pos_ref, f_ref, qo_ref, kvo_ref, cos_ref, sin_ref):
        lane, is_rope = _rope_lanes((tl, LANES))
        ang = pos_ref[...] * f_ref[...]
        cosf = jnp.where(is_rope, jnp.cos(ang), jnp.where(lane < QK_NOPE, 1.0, 0.0))
        sinf = jnp.where(is_rope, jnp.sin(ang), 0.0)
        cos_ref[...] = cosf
        sin_ref[...] = sinf
        kr = _rope_apply(kr_ref[...], cosf, sinf, lane)
        for h in range(N_HEADS):
            qh = _rope_apply(q_ref[:, h * LANES:(h + 1) * LANES], cosf, sinf, lane)
            qo_ref[:, h * LANES:(h + 1) * LANES] = (qh * Q_PRESCALE).astype(BF16)
            kvo_ref[:, 2 * h * LANES:(2 * h + 1) * LANES] = (kv_ref[:, 2 * h * LANES:(2 * h + 1) * LANES] + kr).astype(BF16)
            vh = jnp.where(lane == V_HEAD, 1.0, kv_ref[:, (2 * h + 1) * LANES:(2 * h + 2) * LANES])
            kvo_ref[:, (2 * h + 1) * LANES:(2 * h + 2) * LANES] = vh.astype(BF16)

    return pl.pallas_call(
        body, name="mla_prep", grid=(L // tl,),
        in_specs=[_row(tl, HEAD_PAD), _row(tl, 2 * HEAD_PAD), _row(tl, LANES, P_KR // LANES), _row(tl, 1), _full((1, LANES))],
        out_specs=[_row(tl, HEAD_PAD), _row(tl, 2 * HEAD_PAD), _row(tl, LANES), _row(tl, LANES)],
        out_shape=[jax.ShapeDtypeStruct((L, HEAD_PAD), BF16), jax.ShapeDtypeStruct((L, 2 * HEAD_PAD), BF16),
                   jax.ShapeDtypeStruct((L, LANES), F32), jax.ShapeDtypeStruct((L, LANES), F32)],
        compiler_params=_params("parallel"))(q_pad, kv_pad, proj, pos_col, inv_freq)


def _mla_prep_bwd_call(dq, dkv, cosf, sinf):
    L = dq.shape[0]
    tl = _fit(L, 512)

    def body(dq_ref, dkv_ref, cos_ref, sin_ref, dqo_ref, dkvo_ref, dkr_ref):
        lane, is_rope = _rope_lanes((tl, LANES))
        cosf, sinf = cos_ref[...], sin_ref[...]
        dk_sum = jnp.zeros((tl, LANES), F32)
        for h in range(N_HEADS):
            dqo_ref[:, h * LANES:(h + 1) * LANES] = _rope_apply_t(dq_ref[:, h * LANES:(h + 1) * LANES], cosf, sinf, lane, is_rope).astype(BF16)
            dk_sum = dk_sum + dkv_ref[:, 2 * h * LANES:(2 * h + 1) * LANES]
        dkvo_ref[...] = dkv_ref[...].astype(BF16)
        dkr_ref[...] = _rope_apply_t(dk_sum, cosf, sinf, lane, is_rope).astype(BF16)

    return pl.pallas_call(
        body, name="mla_prep_bwd", grid=(L // tl,),
        in_specs=[_row(tl, HEAD_PAD), _row(tl, 2 * HEAD_PAD), _row(tl, LANES), _row(tl, LANES)],
        out_specs=[_row(tl, HEAD_PAD), _row(tl, 2 * HEAD_PAD), _row(tl, LANES)],
        out_shape=[jax.ShapeDtypeStruct((L, HEAD_PAD), BF16), jax.ShapeDtypeStruct((L, 2 * HEAD_PAD), BF16),
                   jax.ShapeDtypeStruct((L, LANES), BF16)],
        compiler_params=_params("parallel"))(dq, dkv, cosf, sinf)


def _mla_norms_bwd_call(proj, dqn, dkn, q_norm, kv_norm):
    L = proj.shape[0]
    tl = _fit(L, 512)

    def body(p_ref, dqn_ref, dkn_ref, gq_ref, gk_ref, d_ref, dgq_ref, dgk_ref):
        first = pl.program_id(0) == 0
        p = p_ref[...]
        dq, dgq = _rms_bwd(p[:, P_CQ:P_CKV], gq_ref[...], dqn_ref[...])
        dk, dgk = _rms_bwd(p[:, P_CKV:P_KR], gk_ref[...], dkn_ref[...])
        d_ref[:, P_CQ:P_CKV] = dq.astype(BF16)
        d_ref[:, P_CKV:P_KR] = dk.astype(BF16)
        _acc(dgq_ref, first, dgq)
        _acc(dgk_ref, first, dgk)

    return pl.pallas_call(
        body, name="mla_norms_bwd", grid=(L // tl,),
        in_specs=[_row(tl, P_KR), _row(tl, Q_RANK), _row(tl, KV_RANK), _full((1, Q_RANK)), _full((1, KV_RANK))],
        out_specs=[_row(tl, P_KR), _full((1, Q_RANK)), _full((1, KV_RANK))],
        out_shape=[jax.ShapeDtypeStruct((L, P_KR), BF16), jax.ShapeDtypeStruct((1, Q_RANK), F32),
                   jax.ShapeDtypeStruct((1, KV_RANK), F32)],
        compiler_params=_params("arbitrary"))(proj, dqn, dkn, q_norm, kv_norm)


GATE_TILE = 256


def _merge_call(proj, b_gate, pa, ps):
    L = proj.shape[0]
    tl = _fit(L, 512)
    nc = D_MODEL // GATE_TILE
    g0, g1 = P_GATE // GATE_TILE, (P_GATE + D_MODEL) // GATE_TILE

    def body(l0_ref, l1_ref, b0_ref, b1_ref, pa_ref, ps_ref, o_ref):
        s0 = _sigmoid(l0_ref[...] + b0_ref[...])
        s1 = _sigmoid(l1_ref[...] + b1_ref[...])
        o_ref[...] = (s0 * pa_ref[...] + s1 * ps_ref[...]).astype(BF16)

    blk = lambda off: pl.BlockSpec((tl, GATE_TILE), lambda i, j: (i, off + j))
    bias = lambda off: pl.BlockSpec((1, GATE_TILE), lambda i, j: (0, off + j))
    return pl.pallas_call(
        body, name="merge", grid=(L // tl, nc),
        in_specs=[blk(g0), blk(g1), bias(0), bias(nc), blk(0), blk(0)],
        out_specs=blk(0), out_shape=jax.ShapeDtypeStruct((L, D_MODEL), BF16),
        compiler_params=_params("parallel", "parallel"))(proj, proj, b_gate, b_gate, pa, ps)


def _merge_bwd_call(dm, proj, b_gate, pa, ps):
    L = proj.shape[0]
    tl = _fit(L, 512)
    nc = D_MODEL // GATE_TILE
    g0, g1 = P_GATE // GATE_TILE, (P_GATE + D_MODEL) // GATE_TILE

    def body(dm_ref, l0_ref, l1_ref, b0_ref, b1_ref, pa_ref, ps_ref, dpa_ref, dps_ref, dl0_ref, dl1_ref, db0_ref, db1_ref):
        first = pl.program_id(1) == 0
        dm_ = dm_ref[...]
        s0 = _sigmoid(l0_ref[...] + b0_ref[...])
        s1 = _sigmoid(l1_ref[...] + b1_ref[...])
        dpa_ref[...] = (dm_ * s0).astype(BF16)
        dps_ref[...] = (dm_ * s1).astype(BF16)
        dl0 = dm_ * pa_ref[...] * s0 * (1.0 - s0)
        dl1 = dm_ * ps_ref[...] * s1 * (1.0 - s1)
        dl0_ref[...] = dl0.astype(BF16)
        dl1_ref[...] = dl1.astype(BF16)
        _acc(db0_ref, first, jnp.sum(dl0, axis=0, keepdims=True))
        _acc(db1_ref, first, jnp.sum(dl1, axis=0, keepdims=True))

    blk = lambda off: pl.BlockSpec((tl, GATE_TILE), lambda j, i: (i, off + j))
    bias = lambda off: pl.BlockSpec((1, GATE_TILE), lambda j, i: (0, off + j))
    act = jax.ShapeDtypeStruct((L, D_MODEL), BF16)
    vec = jax.ShapeDtypeStruct((1, D_MODEL), F32)
    return pl.pallas_call(
        body, name="merge_bwd", grid=(nc, L // tl),
        in_specs=[blk(0), blk(g0), blk(g1), bias(0), bias(nc), blk(0), blk(0)],
        out_specs=[blk(0), blk(0), blk(0), blk(0), bias(0), bias(0)],
        out_shape=[act, act, act, act, vec, vec],
        compiler_params=_params("parallel", "arbitrary"))(dm, proj, proj, b_gate, b_gate, pa, ps)


def _post_mix_call(o, x, g_post, g_fpre):
    L, n = x.shape
    tl = _fit(L, 512)

    def body(o_ref, x_ref, gp_ref, gf_ref, x2_ref, hn_ref):
        x2 = x_ref[...] + _rms(o_ref[...], gp_ref[...])
        x2_ref[...] = x2
        hn_ref[...] = _rms(x2, gf_ref[...]).astype(BF16)

    return pl.pallas_call(
        body, name="post_mix", grid=(L // tl,),
        in_specs=[_row(tl, n), _row(tl, n), _full((1, n)), _full((1, n))],
        out_specs=[_row(tl, n), _row(tl, n)],
        out_shape=[jax.ShapeDtypeStruct((L, n), F32), jax.ShapeDtypeStruct((L, n), BF16)],
        compiler_params=_params("parallel"))(o, x, g_post, g_fpre)


def _ffn_out_call(ff, x2, target, g_fpost):
    L, n = x2.shape
    tl = _fit(L, 512)

    def body(ff_ref, x2_ref, t_ref, g_ref, loss_ref, dy_ref, dff_ref, dg_ref):
        first = pl.program_id(0) == 0
        ff_ = ff_ref[...]
        err = x2_ref[...] + _rms(ff_, g_ref[...]) - t_ref[...]
        part = 0.5 * jnp.sum(jnp.sum(err * err, axis=-1, keepdims=True) * (1.0 / n), axis=0, keepdims=True)
        dy = err * (1.0 / n)
        dy_ref[...] = dy
        dff, dg = _rms_bwd(ff_, g_ref[...], dy)
        dff_ref[...] = dff.astype(BF16)
        _acc(loss_ref, first, jnp.broadcast_to(part, (1, LANES)))
        _acc(dg_ref, first, dg)

    return pl.pallas_call(
        body, name="ffn_out", grid=(L // tl,),
        in_specs=[_row(tl, n), _row(tl, n), _row(tl, n), _full((1, n))],
        out_specs=[_full((1, LANES)), _row(tl, n), _row(tl, n), _full((1, n))],
        out_shape=[jax.ShapeDtypeStruct((1, LANES), F32), jax.ShapeDtypeStruct((L, n), F32),
                   jax.ShapeDtypeStruct((L, n), BF16), jax.ShapeDtypeStruct((1, n), F32)],
        compiler_params=_params("arbitrary"))(ff, x2, target, g_fpost)


def _post_bwd_call(x2, dhn2, dy, o, g_fpre, g_post):
    L, n = x2.shape
    tl = _fit(L, 512)

    def body(x2_ref, dh_ref, dy_ref, o_ref, gf_ref, gp_ref, dx2_ref, do_ref, dgf_ref, dgp_ref):
        first = pl.program_id(0) == 0
        d1, dgf = _rms_bwd(x2_ref[...], gf_ref[...], dh_ref[...])
        dx2 = dy_ref[...] + d1
        dx2_ref[...] = dx2
        do, dgp = _rms_bwd(o_ref[...], gp_ref[...], dx2)
        do_ref[...] = do.astype(BF16)
        _acc(dgf_ref, first, dgf)
        _acc(dgp_ref, first, dgp)

    return pl.pallas_call(
        body, name="post_bwd", grid=(L // tl,),
        in_specs=[_row(tl, n), _row(tl, n), _row(tl, n), _row(tl, n), _full((1, n)), _full((1, n))],
        out_specs=[_row(tl, n), _row(tl, n), _full((1, n)), _full((1, n))],
        out_shape=[jax.ShapeDtypeStruct((L, n), F32), jax.ShapeDtypeStruct((L, n), BF16),
                   jax.ShapeDtypeStruct((1, n), F32), jax.ShapeDtypeStruct((1, n), F32)],
        compiler_params=_params("arbitrary"))(x2, dhn2, dy, o, g_fpre, g_post)


def _pre_bwd_call(x, dhn1, dx2, g_pre):
    L, n = x.shape
    tl = _fit(L, 512)

    def body(x_ref, dh_ref, dx2_ref, g_ref, dx_ref, dg_ref):
        first = pl.program_id(0) == 0
        d1, dg = _rms_bwd(x_ref[...], g_ref[...], dh_ref[...])
        dx_ref[...] = dx2_ref[...] + d1
        _acc(dg_ref, first, dg)

    return pl.pallas_call(
        body, name="pre_bwd", grid=(L // tl,),
        in_specs=[_row(tl, n), _row(tl, n), _row(tl, n), _full((1, n))],
        out_specs=[_row(tl, n), _full((1, n))],
        out_shape=[jax.ShapeDtypeStruct((L, n), F32), jax.ShapeDtypeStruct((1, n), F32)],
        compiler_params=_params("arbitrary"))(x, dhn1, dx2, g_pre)


CONV_TILE = 256
HALO = 16


def _shift_down(cur, halo_tail, by):
    rolled = pltpu.roll(cur, by, 0)
    r8 = lax.broadcasted_iota(jnp.int32, halo_tail.shape, 0)
    head = jnp.where(r8 < by, pltpu.roll(halo_tail, by, 0), rolled[0:8])
    return jnp.concatenate([head, rolled[8:]], axis=0)


def _shift_up(cur, halo_head, by):
    n = cur.shape[0]
    rolled = pltpu.roll(cur, n - by, 0)
    r8 = lax.broadcasted_iota(jnp.int32, halo_head.shape, 0)
    tail = jnp.where(r8 >= 8 - by, pltpu.roll(halo_head, 8 - by, 0), rolled[n - 8:])
    return jnp.concatenate([rolled[:n - 8], tail], axis=0)


def _conv_fwd_vals(cur, halo, w, b, not_first):
    tail = halo[HALO - 8:] * not_first
    s1 = _shift_down(cur, tail, 1)
    s2 = _shift_down(cur, tail, 2)
    return b + w[2:3] * cur + w[1:2] * s1 + w[0:1] * s2, s1, s2


def _conv_specs(L, tl, nc, rows_inner):
    nh = tl // HALO
    if rows_inner:
        ij = lambda f: (lambda j, i: f(i, j))
    else:
        ij = lambda f: f
    cur = lambda off: pl.BlockSpec((tl, CONV_TILE), ij(lambda i, j: (i, off + j)))
    prev = lambda off: pl.BlockSpec((HALO, CONV_TILE), ij(lambda i, j: (jnp.maximum(i * nh - 1, 0), off + j)))
    nxt = lambda off: pl.BlockSpec((HALO, CONV_TILE), ij(lambda i, j: (jnp.minimum((i + 1) * nh, L // HALO - 1), off + j)))
    par = lambda rows, off: pl.BlockSpec((rows, CONV_TILE), ij(lambda i, j: (0, off + j)))
    return cur, prev, nxt, par


def _conv_act_call(h, conv_w, conv_b):
    L = h.shape[0]
    tl = _fit(L, 512)
    nc = D_FF // CONV_TILE
    cur, prev, _, par = _conv_specs(L, tl, nc, False)

    def body(hg_ref, hv_ref, pg_ref, pv_ref, wg_ref, wv_ref, bg_ref, bv_ref, a_ref):
        not_first = (pl.program_id(0) > 0).astype(F32)
        gate, _, _ = _conv_fwd_vals(hg_ref[...], pg_ref[...], wg_ref[...], bg_ref[...], not_first)
        val, _, _ = _conv_fwd_vals(hv_ref[...], pv_ref[...], wv_ref[...], bv_ref[...], not_first)
        a_ref[...] = (_gelu(gate) * val).astype(BF16)

    return pl.pallas_call(
        body, name="conv_act", grid=(L // tl, nc),
        in_specs=[cur(0), cur(nc), prev(0), prev(nc), par(3, 0), par(3, nc), par(1, 0), par(1, nc)],
        out_specs=cur(0), out_shape=jax.ShapeDtypeStruct((L, D_FF), BF16),
        compiler_params=_params("parallel", "parallel"))(h, h, h, h, conv_w, conv_w, conv_b, conv_b)


def _conv_act_bwd_call(da, h, conv_w, conv_b):
    L = h.shape[0]
    tl = _fit(L, 512)
    nc = D_FF // CONV_TILE
    cur, prev, _, par = _conv_specs(L, tl, nc, True)

    def body(da_ref, hg_ref, hv_ref, pg_ref, pv_ref, wg_ref, wv_ref, bg_ref, bv_ref,
             dg_ref, dv_ref, dwg_ref, dwv_ref, dbg_ref, dbv_ref):
        first = pl.program_id(1) == 0
        not_first = (pl.program_id(1) > 0).astype(F32)
        hg, hv = hg_ref[...], hv_ref[...]
        gate, g1, g2 = _conv_fwd_vals(hg, pg_ref[...], wg_ref[...], bg_ref[...], not_first)
        val, v1, v2 = _conv_fwd_vals(hv, pv_ref[...], wv_ref[...], bv_ref[...], not_first)
        da_ = da_ref[...]
        dgate = da_ * val * _gelu_grad(gate)
        dval = da_ * _gelu(gate)
        dg_ref[...] = dgate.astype(BF16)
        dv_ref[...] = dval.astype(BF16)
        col = lambda t: jnp.sum(t, axis=0, keepdims=True)
        _acc(dwg_ref, first, jnp.concatenate([col(dgate * g2), col(dgate * g1), col(dgate * hg)], axis=0))
        _acc(dwv_ref, first, jnp.concatenate([col(dval * v2), col(dval * v1), col(dval * hv)], axis=0))
        _acc(dbg_ref, first, col(dgate))
        _acc(dbv_ref, first, col(dval))

    act = jax.ShapeDtypeStruct((L, D_FF), BF16)
    w3 = jax.ShapeDtypeStruct((3, D_FF), F32)
    w1 = jax.ShapeDtypeStruct((1, D_FF), F32)
    return pl.pallas_call(
        body, name="conv_act_bwd", grid=(nc, L // tl),
        in_specs=[cur(0), cur(0), cur(nc), prev(0), prev(nc), par(3, 0), par(3, nc), par(1, 0), par(1, nc)],
        out_specs=[cur(0), cur(0), par(3, 0), par(3, 0), par(1, 0), par(1, 0)],
        out_shape=[act, act, w3, w3, w1, w1],
        compiler_params=_params("parallel", "arbitrary"))(da, h, h, h, h, conv_w, conv_w, conv_b, conv_b)


def _conv_t_call(dgate, dval, conv_w):
    L = dgate.shape[0]
    tl = _fit(L, 512)
    nc = D_FF // CONV_TILE
    cur, _, nxt, par = _conv_specs(L, tl, nc, False)

    def run(d, off, name):
        def body(d_ref, n_ref, w_ref, o_ref):
            not_last = (pl.program_id(0) < L // tl - 1).astype(F32)
            c = d_ref[...].astype(F32)
            head = n_ref[...].astype(F32)[0:8] * not_last
            w = w_ref[...]
            o_ref[...] = (w[2:3] * c + w[1:2] * _shift_up(c, head, 1) + w[0:1] * _shift_up(c, head, 2)).astype(BF16)

        return pl.pallas_call(
            body, name=name, grid=(L // tl, nc),
            in_specs=[cur(0), nxt(0), par(3, off)],
            out_specs=cur(0), out_shape=jax.ShapeDtypeStruct((L, D_FF), BF16),
            compiler_params=_params("parallel", "parallel"))(d, d, conv_w)

    return run(dgate, 0, "conv_t_gate"), run(dval, nc, "conv_t_val")


def _glu_call(y1, w_glu, b_glu):
    L, n = y1.shape
    tl = _fit(L, 512)

    def body(y_ref, w_ref, b_ref, o_ref):
        y2 = _gelu(y_ref[...])
        z = _dot(y2.astype(BF16), w_ref[...], NN) + b_ref[...]
        o_ref[...] = (y2 * _sigmoid(z)).astype(BF16)

    return pl.pallas_call(
        body, name="glu", grid=(L // tl,), in_specs=[_row(tl, n), _full((n, n)), _full((1, n))],
        out_specs=_row(tl, n), out_shape=jax.ShapeDtypeStruct((L, n), BF16),
        compiler_params=_params("parallel"))(y1, w_glu, b_glu)


def _glu_bwd_call(dout, y1, w_glu, b_glu):
    L, n = y1.shape
    tl = _fit(L, 512)

    def body(do_ref, y_ref, w_ref, b_ref, dy_ref, dw_ref, db_ref):
        first = pl.program_id(0) == 0
        y1_ = y_ref[...]
        y2 = _gelu(y1_)
        y2b = y2.astype(BF16)
        w = w_ref[...]
        sg = _sigmoid(_dot(y2b, w, NN) + b_ref[...])
        dout_ = do_ref[...].astype(F32)
        dz = dout_ * y2 * sg * (1.0 - sg)
        dzb = dz.astype(BF16)
        dy2 = dout_ * sg + _dot(dzb, w, NT)
        dy_ref[...] = dy2 * _gelu_grad(y1_)
        _acc(dw_ref, first, _dot(y2b, dzb, TN))
        _acc(db_ref, first, jnp.sum(dz, axis=0, keepdims=True))

    return pl.pallas_call(
        body, name="glu_bwd", grid=(L // tl,),
        in_specs=[_row(tl, n), _row(tl, n), _full((n, n)), _full((1, n))],
        out_specs=[_row(tl, n), _full((n, n)), _full((1, n))],
        out_shape=[jax.ShapeDtypeStruct((L, n), F32), jax.ShapeDtypeStruct((n, n), F32), jax.ShapeDtypeStruct((1, n), F32)],
        compiler_params=_params("arbitrary"))(dout, y1, w_glu, b_glu)


ATTN_TILE = 512
ATTN_SCALE = 1.0 / math.sqrt(QK_HEAD)


ATTN_HEADS = 2
ATTN_GROUPS = N_HEADS // ATTN_HEADS
LOG2E = 1.0 / math.log(2.0)
Q_PRESCALE = ATTN_SCALE * LOG2E
ANY_SPEC = pl.BlockSpec(memory_space=pl.ANY)


def _attn_fwd_call(q, kv, blocks):
    L = q.shape[0]
    t = _fit(L, ATTN_TILE)
    nq = L // t
    n = len(blocks)

    def body(q_ref, kv_ref, *refs):
        blk_refs, (o_ref, lse_ref), gat_refs = refs[:n], refs[n:n + 2], refs[n + 2:2 * n + 2]
        m_s, acc_s, send_sems, recv_sems, local_sems = refs[2 * n + 2:]
        g, i = pl.program_id(0), pl.program_id(1)
        start, forward, finish = _gather_phases(blk_refs, gat_refs, send_sems, recv_sems, local_sems)
        pl.when(jnp.logical_and(g == 0, i == 0))(start)
        m_s[...] = jnp.full((ATTN_HEADS, t, 1), NEG, F32)
        acc_s[...] = jnp.zeros((ATTN_HEADS, t, LANES), F32)
        below = lax.broadcasted_iota(jnp.int32, (t, t), 1) <= lax.broadcasted_iota(jnp.int32, (t, t), 0)

        def block_step(kb, on_diagonal):
            rows = pl.ds(pl.multiple_of(kb * t, t), t)
            for a in range(ATTN_HEADS):
                s = _dot(q_ref[:, a * LANES:(a + 1) * LANES], kv_ref[rows, 2 * a * LANES:(2 * a + 1) * LANES], NT)
                if on_diagonal:
                    s = jnp.where(below, s, NEG)
                m_prev = m_s[a]
                m_new = jnp.maximum(m_prev, jnp.max(s, axis=1, keepdims=True))
                p = jnp.exp2(s - m_new)
                pv = _dot(p.astype(BF16), kv_ref[rows, (2 * a + 1) * LANES:(2 * a + 2) * LANES], NN)
                acc_s[a] = jnp.exp2(m_prev - m_new) * acc_s[a] + pv
                m_s[a] = m_new

        def step(kb, carry):
            block_step(kb, False)
            return carry

        lax.fori_loop(0, i, step, 0)
        block_step(i, True)
        lane = lax.broadcasted_iota(jnp.int32, (t, LANES), 1)
        for a in range(ATTN_HEADS):
            acc = acc_s[a]
            l = jnp.sum(jnp.where(lane == V_HEAD, acc, 0.0), axis=1, keepdims=True)
            o_ref[:, a * LANES:(a + 1) * LANES] = (acc / l).astype(BF16)
            lse_ref[a] = m_s[a] + jnp.log(l) * LOG2E
        pl.when(jnp.logical_and(g == (3 * ATTN_GROUPS) // 4, i == 0))(forward)
        pl.when(jnp.logical_and(g == ATTN_GROUPS - 1, i == nq - 1))(finish)

    gw = ATTN_HEADS * LANES
    return pl.pallas_call(
        body, name="attn_fwd", grid=(ATTN_GROUPS, nq),
        in_specs=[pl.BlockSpec((t, gw), lambda g, i: (i, g)),
                  pl.BlockSpec((L, 2 * gw), lambda g, i: (0, g))] + [ANY_SPEC] * n,
        out_specs=[pl.BlockSpec((t, gw), lambda g, i: (i, g)),
                   pl.BlockSpec((ATTN_HEADS, t, 1), lambda g, i: (g, i, 0))] + [ANY_SPEC] * n,
        out_shape=[jax.ShapeDtypeStruct((L, HEAD_PAD), BF16), jax.ShapeDtypeStruct((N_HEADS, L, 1), F32)]
        + [jax.ShapeDtypeStruct((N_DEV,) + b.shape, b.dtype) for b in blocks],
        scratch_shapes=[pltpu.VMEM((ATTN_HEADS, t, 1), F32), pltpu.VMEM((ATTN_HEADS, t, LANES), F32)] + _comm_sems(n),
        compiler_params=_params("arbitrary", "arbitrary"))(q, kv, *blocks)


def _attn_bwd_call(q, kv, o, do, lse, parts):
    L = q.shape[0]
    t = _fit(L, ATTN_TILE)
    nq = L // t
    n = len(parts)

    def body(q_ref, do_ref, o_ref, lse_ref, kv_ref, *refs):
        parts_refs, (dq_ref, dkv_ref), got_refs = refs[:n], refs[n:n + 2], refs[n + 2:2 * n + 2]
        dk_s, dv_s, send_sems, recv_sems, local_sems = refs[2 * n + 2:]
        g, j = pl.program_id(0), pl.program_id(1)
        start, finish = _exchange_phases(parts_refs, got_refs, send_sems, recv_sems, local_sems)
        pl.when(jnp.logical_and(g == 0, j == 0))(start)

        @pl.when(j == 0)
        def _():
            dq_ref[...] = jnp.zeros((L, ATTN_HEADS * LANES), F32)

        dk_s[...] = jnp.zeros((ATTN_HEADS, t, LANES), F32)
        dv_s[...] = jnp.zeros((ATTN_HEADS, t, LANES), F32)
        below = lax.broadcasted_iota(jnp.int32, (t, t), 1) <= lax.broadcasted_iota(jnp.int32, (t, t), 0)

        def block_step(i, on_diagonal):
            rows = pl.ds(pl.multiple_of(i * t, t), t)
            for a in range(ATTN_HEADS):
                lanes = slice(a * LANES, (a + 1) * LANES)
                qi = q_ref[rows, lanes]
                doi = do_ref[rows, lanes]
                kblk = kv_ref[:, 2 * a * LANES:(2 * a + 1) * LANES]
                delta = jnp.sum(doi.astype(F32) * o_ref[rows, lanes].astype(F32), axis=1, keepdims=True)
                s = _dot(qi, kblk, NT)
                if on_diagonal:
                    s = jnp.where(below, s, NEG)
                p = jnp.exp2(s - lse_ref[a, rows, :])
                dv_s[a] += _dot(p.astype(BF16), doi, TN)
                ds = (p * (_dot(doi, kv_ref[:, (2 * a + 1) * LANES:(2 * a + 2) * LANES], NT) - delta)).astype(BF16)
                dk_s[a] += _dot(ds, qi, TN)
                dq_ref[rows, lanes] += _dot(ds, kblk, NN) * ATTN_SCALE

        def step(i, carry):
            block_step(i, False)
            return carry

        block_step(j, True)
        lax.fori_loop(j + 1, nq, step, 0)
        for a in range(ATTN_HEADS):
            dkv_ref[:, 2 * a * LANES:(2 * a + 1) * LANES] = dk_s[a] * (1.0 / LOG2E)
            dkv_ref[:, (2 * a + 1) * LANES:(2 * a + 2) * LANES] = dv_s[a]
        pl.when(jnp.logical_and(g == ATTN_GROUPS - 1, j == nq - 1))(finish)

    gw = ATTN_HEADS * LANES
    whole = lambda: pl.BlockSpec((L, gw), lambda g, j: (0, g))
    acc = pltpu.VMEM((ATTN_HEADS, t, LANES), F32)
    return pl.pallas_call(
        body, name="attn_bwd", grid=(ATTN_GROUPS, nq),
        in_specs=[whole(), whole(), whole(), pl.BlockSpec((ATTN_HEADS, L, 1), lambda g, j: (g, 0, 0)),
                  pl.BlockSpec((t, 2 * gw), lambda g, j: (j, g))] + [ANY_SPEC] * n,
        out_specs=[whole(), pl.BlockSpec((t, 2 * gw), lambda g, j: (j, g))] + [ANY_SPEC] * n,
        out_shape=[jax.ShapeDtypeStruct((L, HEAD_PAD), F32), jax.ShapeDtypeStruct((L, 2 * HEAD_PAD), F32)]
        + [jax.ShapeDtypeStruct(p.shape, p.dtype) for p in parts],
        scratch_shapes=[acc, acc] + _comm_sems(n),
        compiler_params=_params("arbitrary", "arbitrary"))(q, do, o, lse, kv, *parts)


def _disc(lr, li, ldt, br, bi):
    dt = jnp.exp(ldt)
    mag = jnp.exp(lr * dt)
    ang = li * dt
    a_re, a_im = mag * jnp.cos(ang), mag * jnp.sin(ang)
    den = lr * lr + li * li
    n_re, n_im = a_re - 1.0, a_im
    z_re = (n_re * lr + n_im * li) / den
    z_im = (n_im * lr - n_re * li) / den
    return a_re, a_im, z_re * br - z_im * bi, z_re * bi + z_im * br


def _disc_call(lr, li, ldt, br, bi):
    def body(lr_ref, li_ref, ldt_ref, br_ref, bi_ref, ar_ref, ai_ref, bbr_ref, bbi_ref):
        ar_ref[...], ai_ref[...], bbr_ref[...], bbi_ref[...] = _disc(
            lr_ref[...], li_ref[...], ldt_ref[...], br_ref[...], bi_ref[...])

    c1 = jax.ShapeDtypeStruct((SSM_NSTATE, 1), F32)
    c16 = jax.ShapeDtypeStruct((SSM_NSTATE, SSM_GROUP), F32)
    return pl.pallas_call(body, name="ssm_disc", out_shape=[c1, c1, c16, c16])(lr, li, ldt, br, bi)


def _disc_bwd_call(lr, li, ldt, br, bi, dar, dai, dbbr, dbbi):
    def body(lr_ref, li_ref, ldt_ref, br_ref, bi_ref, dar_ref, dai_ref, dbbr_ref, dbbi_ref,
             dlr_ref, dli_ref, dldt_ref, dbr_ref, dbi_ref):
        _, vjp = jax.vjp(_disc, lr_ref[...], li_ref[...], ldt_ref[...], br_ref[...], bi_ref[...])
        dlr_ref[...], dli_ref[...], dldt_ref[...], dbr_ref[...], dbi_ref[...] = vjp(
            (dar_ref[...], dai_ref[...], dbbr_ref[...], dbbi_ref[...]))

    c1 = jax.ShapeDtypeStruct((SSM_NSTATE, 1), F32)
    c16 = jax.ShapeDtypeStruct((SSM_NSTATE, SSM_GROUP), F32)
    return pl.pallas_call(body, name="ssm_disc_bwd", out_shape=[c1, c1, c1, c16, c16])(
        lr, li, ldt, br, bi, dar, dai, dbbr, dbbi)


SSM_ROWS = 512
SSM_CW = SSM_NSTATE // SSM_CHUNKS
SSM_CU = SSM_WIDTH // SSM_CHUNKS


def _cmul(ar, ai, br, bi):
    return ar * br - ai * bi, ar * bi + ai * br


def _power(ar1, ai1, n):
    def step(_, c):
        return _cmul(c[0], c[1], ar1, ai1)

    return lax.fori_loop(0, n, step, (jnp.ones_like(ar1), jnp.zeros_like(ar1)))


def _tile(k):
    return pl.ds(pl.multiple_of(k * 8, 8), 8)


def _ssm_fwd_call(u, a_re, a_im, bb_re, bb_im, cm_re, cm_im, d_skip):
    L = u.shape[0]
    seg = L // 8
    rb = _fit(L, SSM_ROWS)

    def body(u_ref, ar_ref, ai_ref, bbr_ref, bbi_ref, cmr_ref, cmi_ref, d_ref, y_ref, sre_hbm, sim_hbm,
             s_re, s_im, sems):
        q = pl.program_id(0)

        def bu_step(r, c):
            rows = pl.ds(pl.multiple_of(r * rb, rb), rb)
            ub = u_ref[rows, :].astype(BF16)
            s_re[rows, :] = _dot(ub, bbr_ref[0], NN)
            s_im[rows, :] = _dot(ub, bbi_ref[0], NN)
            return c

        lax.fori_loop(0, L // rb, bu_step, 0)
        ar1, ai1 = ar_ref[...], ai_ref[...]
        ar = jnp.broadcast_to(ar1, (8, SSM_CW))
        ai = jnp.broadcast_to(ai1, (8, SSM_CW))

        def local(k, c):
            nr, ni = _cmul(ar, ai, c[0], c[1])
            nr = nr + s_re[_tile(k), :]
            ni = ni + s_im[_tile(k), :]
            s_re[_tile(k), :] = nr
            s_im[_tile(k), :] = ni
            return nr, ni

        zero8 = jnp.zeros((8, SSM_CW), F32)
        lax.fori_loop(0, seg, local, (zero8, zero8))
        pr, pi = _power(ar1, ai1, seg)
        end_r = s_re[pl.ds((seg - 1) * 8, 8), :]
        end_i = s_im[pl.ds((seg - 1) * 8, 8), :]
        er = jnp.zeros((1, SSM_CW), F32)
        ei = jnp.zeros((1, SSM_CW), F32)
        rows_r, rows_i = [er], [ei]
        for j in range(7):
            tr, ti = _cmul(pr, pi, er, ei)
            er, ei = end_r[j:j + 1] + tr, end_i[j:j + 1] + ti
            rows_r.append(er)
            rows_i.append(ei)
        e_r = jnp.concatenate(rows_r, axis=0)
        e_i = jnp.concatenate(rows_i, axis=0)

        def fix(k, c):
            wr, wi = _cmul(c[0], c[1], ar, ai)
            fr, fi = _cmul(wr, wi, e_r, e_i)
            s_re[_tile(k), :] += fr
            s_im[_tile(k), :] += fi
            return wr, wi

        lax.fori_loop(0, seg, fix, (jnp.ones((8, SSM_CW), F32), zero8))
        out_r = pltpu.make_async_copy(s_re, sre_hbm.at[q], sems.at[0])
        out_i = pltpu.make_async_copy(s_im, sim_hbm.at[q], sems.at[1])
        out_r.start()
        out_i.start()

        def y_step(r, c):
            rows = pl.ds(pl.multiple_of(r * rb, rb), rb)
            y = _dot(s_re[rows, :].astype(BF16), cmr_ref[0], NN) - _dot(s_im[rows, :].astype(BF16), cmi_ref[0], NN)
            y_ref[rows, :] = y + d_ref[...] * u_ref[rows, :]
            return c

        lax.fori_loop(0, L // rb, y_step, 0)
        out_r.wait()
        out_i.wait()

    chunk = lambda rows, cols: pl.BlockSpec((rows, cols), lambda q: (0, q))
    mat = lambda r, c: pl.BlockSpec((1, r, c), lambda q: (q, 0, 0))
    anyspec = pl.BlockSpec(memory_space=pl.ANY)
    states = jax.ShapeDtypeStruct((SSM_CHUNKS, L, SSM_CW), F32)
    return pl.pallas_call(
        body, name="ssm_fwd", grid=(SSM_CHUNKS,),
        in_specs=[chunk(L, SSM_CU), chunk(1, SSM_CW), chunk(1, SSM_CW), mat(SSM_CU, SSM_CW), mat(SSM_CU, SSM_CW),
                  mat(SSM_CW, SSM_CU), mat(SSM_CW, SSM_CU), chunk(1, SSM_CU)],
        out_specs=[chunk(L, SSM_CU), anyspec, anyspec],
        out_shape=[jax.ShapeDtypeStruct((L, SSM_WIDTH), F32), states, states],
        scratch_shapes=[pltpu.VMEM((L, SSM_CW), F32), pltpu.VMEM((L, SSM_CW), F32), pltpu.SemaphoreType.DMA((2,))],
        compiler_params=_params("arbitrary", vmem=VMEM_BIG))(u, a_re, a_im, bb_re, bb_im, cm_re, cm_im, d_skip)


def _ssm_bwd_call(dy, u, s_re_all, s_im_all, a_re, a_im, bb_re, bb_im, cm_re, cm_im, d_skip):
    L = u.shape[0]
    seg = L // 8
    rb = _fit(L, SSM_ROWS)

    def body(dy_ref, u_ref, sre_hbm, sim_hbm, ar_ref, ai_ref, bbr_ref, bbi_ref, cmr_ref, cmi_ref, d_ref,
             du_ref, dbbr_ref, dbbi_ref, dcmr_ref, dcmi_ref, dar_ref, dai_ref, dd_ref,
             g_re, g_im, s_re, s_im, sems):
        q = pl.program_id(0)
        in_r = pltpu.make_async_copy(sre_hbm.at[q], s_re, sems.at[0])
        in_i = pltpu.make_async_copy(sim_hbm.at[q], s_im, sems.at[1])
        in_r.start()
        in_i.start()

        def ds_step(r, c):
            rows = pl.ds(pl.multiple_of(r * rb, rb), rb)
            dyb = dy_ref[rows, :].astype(BF16)
            g_re[rows, :] = _dot(dyb, cmr_ref[0], NT)
            g_im[rows, :] = -_dot(dyb, cmi_ref[0], NT)
            return c

        lax.fori_loop(0, L // rb, ds_step, 0)
        ar1, ai1 = ar_ref[...], ai_ref[...]
        ar = jnp.broadcast_to(ar1, (8, SSM_CW))
        nai = jnp.broadcast_to(-ai1, (8, SSM_CW))

        def local(kk, c):
            k = seg - 1 - kk
            nr, ni = _cmul(ar, nai, c[0], c[1])
            nr = nr + g_re[_tile(k), :]
            ni = ni + g_im[_tile(k), :]
            g_re[_tile(k), :] = nr
            g_im[_tile(k), :] = ni
            return nr, ni

        zero8 = jnp.zeros((8, SSM_CW), F32)
        lax.fori_loop(0, seg, local, (zero8, zero8))
        pr, pi = _power(ar1, -ai1, seg)
        head_r = g_re[pl.ds(0, 8), :]
        head_i = g_im[pl.ds(0, 8), :]
        fr = jnp.zeros((1, SSM_CW), F32)
        fi = jnp.zeros((1, SSM_CW), F32)
        rows_r, rows_i = [fr], [fi]
        for j in range(6, -1, -1):
            tr, ti = _cmul(pr, pi, fr, fi)
            fr, fi = head_r[j + 1:j + 2] + tr, head_i[j + 1:j + 2] + ti
            rows_r.insert(0, fr)
            rows_i.insert(0, fi)
        f_r = jnp.concatenate(rows_r, axis=0)
        f_i = jnp.concatenate(rows_i, axis=0)
        in_r.wait()
        in_i.wait()

        def fixed(k, wr, wi):
            xr, xi = _cmul(wr, wi, f_r, f_i)
            gr = g_re[_tile(k), :] + xr
            gi = g_im[_tile(k), :] + xi
            g_re[_tile(k), :] = gr
            g_im[_tile(k), :] = gi
            return gr, gi

        def fix(kk, c):
            k = seg - 1 - kk
            wr, wi = _cmul(c[0], c[1], ar, nai)
            gr, gi = fixed(k, wr, wi)
            pr_, pi_ = s_re[_tile(k - 1), :], s_im[_tile(k - 1), :]
            return wr, wi, c[2] + gr * pr_ + gi * pi_, c[3] + gi * pr_ - gr * pi_

        wr, wi, acc_r, acc_i = lax.fori_loop(0, seg - 1, fix, (jnp.ones((8, SSM_CW), F32), zero8, zero8, zero8))
        wr, wi = _cmul(wr, wi, ar, nai)
        gr, gi = fixed(0, wr, wi)
        row8 = lax.broadcasted_iota(jnp.int32, (8, SSM_CW), 0)
        pr_ = jnp.where(row8 > 0, pltpu.roll(s_re[pl.ds((seg - 1) * 8, 8), :], 1, 0), 0.0)
        pi_ = jnp.where(row8 > 0, pltpu.roll(s_im[pl.ds((seg - 1) * 8, 8), :], 1, 0), 0.0)
        acc_r = acc_r + gr * pr_ + gi * pi_
        acc_i = acc_i + gi * pr_ - gr * pi_
        dar_ref[...] = jnp.sum(acc_r, axis=0, keepdims=True)
        dai_ref[...] = jnp.sum(acc_i, axis=0, keepdims=True)

        dbbr_ref[...] = jnp.zeros((1, SSM_CU, SSM_CW), F32)
        dbbi_ref[...] = jnp.zeros((1, SSM_CU, SSM_CW), F32)
        dcmr_ref[...] = jnp.zeros((1, SSM_CW, SSM_CU), F32)
        dcmi_ref[...] = jnp.zeros((1, SSM_CW, SSM_CU), F32)
        dd_ref[...] = jnp.zeros((1, SSM_CU), F32)

        def grad_step(r, c):
            rows = pl.ds(pl.multiple_of(r * rb, rb), rb)
            ub, dyv = u_ref[rows, :], dy_ref[rows, :]
            ubb, dyb = ub.astype(BF16), dyv.astype(BF16)
            grb, gib = g_re[rows, :].astype(BF16), g_im[rows, :].astype(BF16)
            dbbr_ref[0] += _dot(ubb, grb, TN)
            dbbi_ref[0] += _dot(ubb, gib, TN)
            dcmr_ref[0] += _dot(s_re[rows, :].astype(BF16), dyb, TN)
            dcmi_ref[0] -= _dot(s_im[rows, :].astype(BF16), dyb, TN)
            du_ref[rows, :] = _dot(grb, bbr_ref[0], NT) + _dot(gib, bbi_ref[0], NT) + d_ref[...] * dyv
            dd_ref[...] += jnp.sum(dyv * ub, axis=0, keepdims=True)
            return c

        lax.fori_loop(0, L // rb, grad_step, 0)

    chunk = lambda rows, cols: pl.BlockSpec((rows, cols), lambda q: (0, q))
    mat = lambda r, c: pl.BlockSpec((1, r, c), lambda q: (q, 0, 0))
    anyspec = pl.BlockSpec(memory_space=pl.ANY)
    big = lambda: pltpu.VMEM((L, SSM_CW), F32)
    return pl.pallas_call(
        body, name="ssm_bwd", grid=(SSM_CHUNKS,),
        in_specs=[chunk(L, SSM_CU), chunk(L, SSM_CU), anyspec, anyspec, chunk(1, SSM_CW), chunk(1, SSM_CW),
                  mat(SSM_CU, SSM_CW), mat(SSM_CU, SSM_CW), mat(SSM_CW, SSM_CU), mat(SSM_CW, SSM_CU), chunk(1, SSM_CU)],
        out_specs=[chunk(L, SSM_CU), mat(SSM_CU, SSM_CW), mat(SSM_CU, SSM_CW), mat(SSM_CW, SSM_CU), mat(SSM_CW, SSM_CU),
                   chunk(1, SSM_CW), chunk(1, SSM_CW), chunk(1, SSM_CU)],
        out_shape=[jax.ShapeDtypeStruct((L, SSM_WIDTH), F32),
                   jax.ShapeDtypeStruct((SSM_CHUNKS, SSM_CU, SSM_CW), F32), jax.ShapeDtypeStruct((SSM_CHUNKS, SSM_CU, SSM_CW), F32),
                   jax.ShapeDtypeStruct((SSM_CHUNKS, SSM_CW, SSM_CU), F32), jax.ShapeDtypeStruct((SSM_CHUNKS, SSM_CW, SSM_CU), F32),
                   jax.ShapeDtypeStruct((1, SSM_NSTATE), F32), jax.ShapeDtypeStruct((1, SSM_NSTATE), F32),
                   jax.ShapeDtypeStruct((1, SSM_WIDTH), F32)],
        scratch_shapes=[big(), big(), big(), big(), pltpu.SemaphoreType.DMA((2,))],
        compiler_params=_params("arbitrary", vmem=VMEM_BIG))(
            dy, u, s_re_all, s_im_all, a_re, a_im, bb_re, bb_im, cm_re, cm_im, d_skip)


def _place():
    return lax.axis_index("x"), lax.axis_index("y"), lax.axis_index("c")


def _all_gather_call(blocks, name, direct=False):
    n = len(blocks)

    def body(*refs):
        if direct:
            start, finish = _exchange_phases(refs[:n], refs[n:2 * n], *refs[2 * n:], same_source=True)
            start()
        else:
            start, forward, finish = _gather_phases(refs[:n], refs[n:2 * n], *refs[2 * n:])
            start()
            forward()
        finish()

    return pl.pallas_call(
        body, name=name, in_specs=[ANY_SPEC] * n, out_specs=[ANY_SPEC] * n,
        out_shape=[jax.ShapeDtypeStruct((N_DEV,) + b.shape, b.dtype) for b in blocks],
        scratch_shapes=_comm_sems(n))(*blocks)


def _comm_sems(n):
    return [pltpu.SemaphoreType.DMA((7 * n,)), pltpu.SemaphoreType.DMA((7 * n,)), pltpu.SemaphoreType.DMA((n,))]


def _gather_phases(x_refs, out_refs, send_sems, recv_sems, local_sems):
    x, y, c = _place()
    me, sibling = (x, y, c), (x, y, 1 - c)
    chips = [(1 - x, y), (x, 1 - y), (1 - x, 1 - y)]
    n = len(x_refs)

    def copy(k, a, blk, to, from_input=False):
        slot = out_refs[a].at[4 * blk[0] + 2 * blk[1] + blk[2]]
        return pltpu.make_async_remote_copy(
            src_ref=x_refs[a] if from_input else slot, dst_ref=slot,
            send_sem=send_sems.at[k * n + a], recv_sem=recv_sems.at[k * n + a], device_id=to, device_id_type=MESH_ID)

    mine = [pltpu.make_async_copy(x_refs[a], out_refs[a].at[4 * x + 2 * y + c], local_sems.at[a]) for a in range(n)]
    first, passed = [], []
    for a in range(n):
        first.append(copy(0, a, me, sibling, True))
        first += [copy(1 + j, a, me, (*chip, c), True) for j, chip in enumerate(chips)]
        passed += [copy(4 + j, a, (*chip, c), sibling) for j, chip in enumerate(chips)]

    def start():
        for cp in mine + first:
            cp.start()

    def forward():
        for j, chip in enumerate(chips):
            for a in range(n):
                copy(1 + j, a, (*chip, c), me).wait_recv()
                passed[3 * a + j].start()

    def finish():
        for a in range(n):
            copy(0, a, sibling, me).wait_recv()
            for j, chip in enumerate(chips):
                copy(4 + j, a, (*chip, 1 - c), me).wait_recv()
        for cp in first + passed:
            cp.wait_send()
        for cp in mine:
            cp.wait()

    return start, forward, finish


def _exchange_phases(p_refs, out_refs, send_sems, recv_sems, local_sems, same_source=False):
    x, y, c = _place()
    me = 4 * x + 2 * y + c
    n = len(p_refs)

    def flip(k):
        px = 1 - x if k & 4 else x
        py = 1 - y if k & 2 else y
        pc = 1 - c if k & 1 else c
        return (px, py, pc), 4 * px + 2 * py + pc

    def source(a, slot):
        return p_refs[a] if same_source else p_refs[a].at[slot]

    def copy(k, a, landing):
        peer, peer_slot = flip(k)
        return pltpu.make_async_remote_copy(
            src_ref=source(a, peer_slot), dst_ref=out_refs[a].at[peer_slot if landing else me],
            send_sem=send_sems.at[(k - 1) * n + a], recv_sem=recv_sems.at[(k - 1) * n + a],
            device_id=peer, device_id_type=MESH_ID)

    mine = [pltpu.make_async_copy(source(a, me), out_refs[a].at[me], local_sems.at[a]) for a in range(n)]
    sends = [copy(k, a, False) for k in range(1, N_DEV) for a in range(n)]

    def start():
        for cp in mine + sends:
            cp.start()

    def finish():
        for k in range(1, N_DEV):
            for a in range(n):
                copy(k, a, True).wait_recv()
        for cp in sends:
            cp.wait_send()
        for cp in mine:
            cp.wait()

    return start, finish


def _adam_math(g, w, m, v):
    c1 = 1.0 / (1.0 - ADAM_B1 ** ADAM_STEP)
    c2 = 1.0 / (1.0 - ADAM_B2 ** ADAM_STEP)
    m_new = ADAM_B1 * m + (1.0 - ADAM_B1) * g
    v_new = ADAM_B2 * v + (1.0 - ADAM_B2) * (g * g)
    delta = -ADAM_LR * ((m_new * c1) / (jnp.sqrt(v_new * c2) + ADAM_EPS) + ADAM_WD * w)
    return g, delta, m_new, v_new


def _sum_slices(s_ref):
    g = s_ref[0].astype(F32)
    for k in range(1, N_DEV):
        g = g + s_ref[k].astype(F32)
    return g


def _adam_call(slices, w, m, v, name):
    rest = w.shape[2:]
    t1 = _fit(w.shape[1], 256, 16) if len(rest) == 1 else _fit(w.shape[1], 8, 8)
    zeros = (0,) * len(rest)

    def body(s_ref, w_ref, m_ref, v_ref, g_ref, d_ref, mo_ref, vo_ref):
        g_ref[...], d_ref[...], mo_ref[...], vo_ref[...] = _adam_math(_sum_slices(s_ref), w_ref[...], m_ref[...], v_ref[...])

    own = pl.BlockSpec((1, t1) + rest, lambda i: (0, i) + zeros)
    out = jax.ShapeDtypeStruct(w.shape, F32)
    return pl.pallas_call(
        body, name=name, grid=(w.shape[1] // t1,),
        in_specs=[pl.BlockSpec((N_DEV, 1, t1) + rest, lambda i: (0, 0, i) + zeros), own, own, own],
        out_specs=[own, own, own, own], out_shape=[out, out, out, out],
        compiler_params=_params("parallel"))(slices, w, m, v)


def _adam_small_call(slices, ws, ms, vs):
    n = len(ws)

    def body(*refs):
        outs = refs[4 * n:]
        for a in range(n):
            res = _adam_math(_sum_slices(refs[a]), refs[n + a][...], refs[2 * n + a][...], refs[3 * n + a][...])
            for r in range(4):
                outs[4 * a + r][...] = res[r]

    flat = pl.pallas_call(
        body, name="adam_small",
        out_shape=[jax.ShapeDtypeStruct(w.shape, F32) for w in ws for _ in range(4)])(*slices, *ws, *ms, *vs)
    return [flat[4 * a:4 * a + 4] for a in range(n)]


BIG = (("w_in", 1024, 404, 1), ("w_uq", 384, 96, 1), ("w_uk", 256, 64, 1), ("w_uv", 256, 64, 1),
       ("w_glu", 64, 512, 0), ("w_branch_attn", 512, 128, 1), ("w_branch_ssm", 512, 128, 1),
       ("w_out", 128, 1024, 0), ("w_up", 1024, 704, 1), ("w_down", 352, 1024, 0), ("conv_w", 3, 704, 1))
BIG_MIX, BIG_FFN = BIG[:8], BIG[8:]
GRADS_EARLY, GRADS_LATE = BIG[8:] + BIG[4:8], BIG[:4]
SMALL = (("mix_norm_pre", (1024,)), ("q_norm", (384,)), ("kv_norm", (256,)), ("ssm_lambda_re", (32, 64)),
         ("ssm_lambda_im", (32, 64)), ("ssm_log_dt", (32,)), ("ssm_b_re", (32, 64, 16)), ("ssm_b_im", (32, 64, 16)),
         ("ssm_c_re", (32, 16, 64)), ("ssm_c_im", (32, 16, 64)), ("ssm_d", (32, 16)), ("b_glu", (512,)),
         ("b_gate", (2048,)), ("mix_norm_post", (1024,)), ("ffn_norm_pre", (1024,)), ("conv_b", (5632,)),
         ("ffn_norm_post", (1024,)))


def _to_slices(full, rows, cols, axis):
    if axis == 1:
        return full.reshape(rows, N_DEV, cols).transpose(1, 0, 2)
    return full.reshape(N_DEV, rows, cols)


def _from_slices(parts, rows, cols, axis):
    if axis == 1:
        return parts.transpose(1, 0, 2).reshape(rows, N_DEV * cols)
    return parts.reshape(N_DEV * rows, cols)


def _head_pad_cols(w, width):
    k = w.shape[0]
    return jnp.pad(w.reshape(k, N_HEADS, width), ((0, 0), (0, 0), (0, LANES - width))).reshape(k, HEAD_PAD)


def _head_unpad_cols(w, width):
    k = w.shape[0]
    return w.reshape(k, N_HEADS, LANES)[:, :, :width].reshape(k, N_HEADS * width)


def _time_perm(a, L):
    return a.reshape(8, L // 8, a.shape[-1]).transpose(1, 0, 2).reshape(L, a.shape[-1])


def _time_unperm(a, L):
    return a.reshape(L // 8, 8, a.shape[-1]).transpose(1, 0, 2).reshape(L, a.shape[-1])


def _block_diag(w, rows_first):
    eye = jnp.eye(8, dtype=w.dtype)
    g = w.reshape(SSM_CHUNKS, 8, w.shape[1], w.shape[2])
    return jnp.einsum("qgrc,gk->qgrkc", g, eye).reshape(SSM_CHUNKS, 8 * w.shape[1], 8 * w.shape[2])


def _block_diag_t(m, r, c):
    eye = jnp.eye(8, dtype=m.dtype)
    return jnp.einsum("qgrkc,gk->qgrc", m.reshape(SSM_CHUNKS, 8, r, 8, c), eye).reshape(SSM_GROUPS, r, c)


def kernel(x, positions, mix_norm_pre, w_in, q_norm, w_uq, kv_norm, w_uk, w_uv, ssm_lambda_re, ssm_lambda_im, ssm_log_dt, ssm_b_re, ssm_b_im, ssm_c_re, ssm_c_im, ssm_d, w_glu, b_glu, w_branch_attn, w_branch_ssm, b_gate, w_out, mix_norm_post, ffn_norm_pre, w_up, conv_w, conv_b, w_down, ffn_norm_post, loss_target, m_mix_norm_pre, m_w_in, m_q_norm, m_w_uq, m_kv_norm, m_w_uk, m_w_uv, m_ssm_lambda_re, m_ssm_lambda_im, m_ssm_log_dt, m_ssm_b_re, m_ssm_b_im, m_ssm_c_re, m_ssm_c_im, m_ssm_d, m_w_glu, m_b_glu, m_w_branch_attn, m_w_branch_ssm, m_b_gate, m_w_out, m_mix_norm_post, m_ffn_norm_pre, m_w_up, m_conv_w, m_conv_b, m_w_down, m_ffn_norm_post, v_mix_norm_pre, v_w_in, v_q_norm, v_w_uq, v_kv_norm, v_w_uk, v_w_uv, v_ssm_lambda_re, v_ssm_lambda_im, v_ssm_log_dt, v_ssm_b_re, v_ssm_b_im, v_ssm_c_re, v_ssm_c_im, v_ssm_d, v_w_glu, v_b_glu, v_w_branch_attn, v_w_branch_ssm, v_b_gate, v_w_out, v_mix_norm_post, v_ffn_norm_pre, v_w_up, v_conv_w, v_conv_b, v_w_down, v_ffn_norm_post):
    given = dict(locals())
    L = x.shape[1]
    xs = x[0]
    target = loss_target[0]

    def shard_bits(group):
        return [given[name][0] if name == "conv_w" else given[name][0].astype(BF16) for name, _, _, _ in group]

    W = {}

    def unpack_weights(gathered, group):
        for (name, rows, cols, axis), parts in zip(group, gathered):
            W[name] = _from_slices(parts, rows, cols, axis)

    unpack_weights(_all_gather_call(shard_bits(BIG_MIX), "gather_weights"), BIG_MIX)

    wi = W["w_in"]
    kr_cols = jnp.pad(wi[:, 640:672], ((0, 0), (QK_NOPE, LANES - QK_HEAD)))
    w_in_p = jnp.concatenate([wi[:, :640], kr_cols, wi[:, 672:]], axis=1)
    w_uq_p = _head_pad_cols(W["w_uq"], QK_HEAD)
    w_kv_p = jnp.stack([_head_pad_cols(W["w_uk"], QK_NOPE).reshape(KV_RANK, N_HEADS, LANES),
                        _head_pad_cols(W["w_uv"], V_HEAD).reshape(KV_RANK, N_HEADS, LANES)], axis=2
                       ).reshape(KV_RANK, 2 * HEAD_PAD)
    w_ba_p = jnp.pad(W["w_branch_attn"].reshape(N_HEADS, V_HEAD, D_MODEL), ((0, 0), (0, LANES - V_HEAD), (0, 0))
                     ).reshape(HEAD_PAD, D_MODEL)

    hn1 = _rms_fwd_call(xs, mix_norm_pre, "rms_pre")
    proj = _mm(hn1, w_in_p, "mm_in", tn=256)
    qn, ckvn = _mla_norms_call(proj, q_norm, kv_norm)
    q_pad = _mm(qn, w_uq_p, "mm_uq")
    kv_pad = _mm(ckvn, w_kv_p, "mm_ukv")
    half = jnp.arange(QK_ROPE // 2, dtype=F32)
    inv_freq = ROPE_THETA ** (-2.0 * half / QK_ROPE)
    inv_freq = jnp.pad(jnp.concatenate([inv_freq, inv_freq]), (QK_NOPE, LANES - QK_HEAD)).reshape(1, LANES)
    pos_col = positions.astype(F32).reshape(L, 1)
    q_r, kv_r, cosf, sinf = _mla_prep_call(q_pad, kv_pad, proj, pos_col, inv_freq)
    attn, lse, *gathered_ffn = _attn_fwd_call(q_r, kv_r, shard_bits(BIG_FFN))
    unpack_weights(gathered_ffn, BIG_FFN)

    col = lambda a: a.reshape(SSM_NSTATE, -1)
    lr_c, li_c = col(ssm_lambda_re[0]), col(ssm_lambda_im[0])
    ldt_c = col(jnp.broadcast_to(ssm_log_dt[0][:, None], (SSM_GROUPS, SSM_STATE)))
    br_c, bi_c = col(ssm_b_re[0]), col(ssm_b_im[0])
    a_re_c, a_im_c, bb_re_c, bb_im_c = _disc_call(lr_c, li_c, ldt_c, br_c, bi_c)
    a_re, a_im = a_re_c.reshape(1, SSM_NSTATE), a_im_c.reshape(1, SSM_NSTATE)
    to_bb = lambda b: _block_diag(b.reshape(SSM_GROUPS, SSM_STATE, SSM_GROUP).transpose(0, 2, 1), True).astype(BF16)
    bb_re, bb_im = to_bb(bb_re_c), to_bb(bb_im_c)
    to_cm = lambda c_: _block_diag(c_[0].transpose(0, 2, 1), True).astype(BF16)
    cm_re, cm_im = to_cm(ssm_c_re), to_cm(ssm_c_im)
    d_skip = ssm_d.reshape(1, SSM_WIDTH)
    u_p = _time_perm(proj[:, P_U:P_GATE], L)
    y1, s_re, s_im = _ssm_fwd_call(u_p, a_re, a_im, bb_re, bb_im, cm_re, cm_im, d_skip)
    w_glu_b = W["w_glu"]
    ssm_p = _glu_call(y1, w_glu_b, b_glu)
    ssm = _time_unperm(ssm_p, L)

    pa = _mm(attn, w_ba_p, "mm_ba")
    ps = _mm(ssm, W["w_branch_ssm"], "mm_bs")
    merged = _merge_call(proj, b_gate, pa, ps)
    o = _mm(merged, W["w_out"], "mm_out")
    x2, hn2 = _post_mix_call(o, xs, mix_norm_post, ffn_norm_pre)
    h = _mm(hn2, W["w_up"], "mm_up")
    cw = W["conv_w"]
    act = _conv_act_call(h, cw, conv_b)
    ff = _mm(act, W["w_down"], "mm_down", tk=1408)
    loss_row, dy, dff, g_ffn_norm_post = _ffn_out_call(ff, x2, target, ffn_norm_post)
    loss = lax.psum(loss_row[0, 0], ("x", "y", "c"))

    da = _mm(dff, W["w_down"], "mm_down_dx", tb=True, tn=256)
    g_w_down = _mm(act, dff, "mm_down_dw", ta=True, out_dtype=BF16, tm=256, tn=1024)
    dgate, dval, dcw_g, dcw_v, dcb_g, dcb_v = _conv_act_bwd_call(da, h, cw, conv_b)
    g_conv_w = jnp.concatenate([dcw_g, dcw_v], axis=1)
    g_conv_b = jnp.concatenate([dcb_g, dcb_v], axis=1)
    dh_g, dh_v = _conv_t_call(dgate, dval, cw)
    dh = jnp.concatenate([dh_g, dh_v], axis=1)
    dhn2 = _mm(dh, W["w_up"], "mm_up_dx", tb=True, tk=1408)
    g_w_up = _mm(hn2, dh, "mm_up_dw", ta=True, out_dtype=BF16, tm=512)
    dx2, do, g_ffn_norm_pre, g_mix_norm_post = _post_bwd_call(x2, dhn2, dy, o, ffn_norm_pre, mix_norm_post)
    dmerged = _mm(do, W["w_out"], "mm_out_dx", tb=True)
    g_w_out = _mm(merged, do, "mm_out_dw", ta=True, out_dtype=BF16, tm=512, tn=1024)
    dpa, dps, dl0, dl1, db0, db1 = _merge_bwd_call(dmerged, proj, b_gate, pa, ps)
    g_b_gate = jnp.concatenate([db0, db1], axis=1)
    dattn = _mm(dpa, w_ba_p, "mm_ba_dx", tb=True, out_dtype=BF16)
    g_w_ba = _mm(attn, dpa, "mm_ba_dw", ta=True, out_dtype=BF16, tm=512, tn=1024).reshape(N_HEADS, LANES, D_MODEL)[:, :V_HEAD].reshape(N_HEADS * V_HEAD, D_MODEL)
    dssm = _mm(dps, W["w_branch_ssm"], "mm_bs_dx", tb=True)
    g_w_bs = _mm(ssm, dps, "mm_bs_dw", ta=True, out_dtype=BF16, tm=512, tn=1024)

    dy1, g_w_glu, g_b_glu = _glu_bwd_call(_time_perm(dssm, L), y1, w_glu_b, b_glu)
    du_p, dbb_re, dbb_im, dcm_re, dcm_im, da_re, da_im, g_ssm_d = _ssm_bwd_call(
        dy1, u_p, s_re, s_im, a_re, a_im, bb_re, bb_im, cm_re, cm_im, d_skip)
    du = _time_unperm(du_p, L)
    from_bb = lambda m: col(_block_diag_t(m, SSM_GROUP, SSM_STATE).transpose(0, 2, 1))
    dlr, dli, dldt, dbr, dbi = _disc_bwd_call(
        lr_c, li_c, ldt_c, br_c, bi_c, da_re.reshape(SSM_NSTATE, 1), da_im.reshape(SSM_NSTATE, 1), from_bb(dbb_re), from_bb(dbb_im))
    g_c_re = _block_diag_t(dcm_re, SSM_STATE, SSM_GROUP).transpose(0, 2, 1)
    g_c_im = _block_diag_t(dcm_im, SSM_STATE, SSM_GROUP).transpose(0, 2, 1)

    def grad_slices(group, grads):
        return [_to_slices(grads[name], rows, cols, axis) for name, rows, cols, axis in group]

    early_grads = {"w_up": g_w_up, "w_down": g_w_down, "conv_w": g_conv_w, "w_glu": g_w_glu.astype(BF16),
                   "w_branch_attn": g_w_ba, "w_branch_ssm": g_w_bs, "w_out": g_w_out}
    dq, dkv, *received_early = _attn_bwd_call(q_r, kv_r, attn, dattn, lse, grad_slices(GRADS_EARLY, early_grads))
    dq_p, dkv_p, dkr_p = _mla_prep_bwd_call(dq, dkv, cosf, sinf)
    dqn = _mm(dq_p, w_uq_p, "mm_uq_dx", tb=True)
    g_w_uq = _head_unpad_cols(_mm(qn, dq_p, "mm_uq_dw", ta=True, out_dtype=BF16, tn=1024), QK_HEAD)
    dckvn = _mm(dkv_p, w_kv_p, "mm_ukv_dx", tb=True)
    g_w_kv = _mm(ckvn, dkv_p, "mm_ukv_dw", ta=True, out_dtype=BF16, tn=1024).reshape(KV_RANK, N_HEADS, 2, LANES)
    g_w_uk = g_w_kv[:, :, 0, :QK_NOPE].reshape(KV_RANK, N_HEADS * QK_NOPE)
    g_w_uv = g_w_kv[:, :, 1, :V_HEAD].reshape(KV_RANK, N_HEADS * V_HEAD)
    dcqkv, g_q_norm, g_kv_norm = _mla_norms_bwd_call(proj, dqn, dckvn, q_norm, kv_norm)
    dproj = jnp.concatenate([dcqkv, dkr_p, du.astype(BF16), dl0, dl1], axis=1)
    g_w_in_p = _mm(hn1, dproj, "mm_in_dw", ta=True, out_dtype=BF16, tm=512, tn=256)
    g_w_in = jnp.concatenate([g_w_in_p[:, :640], g_w_in_p[:, 640 + QK_NOPE:640 + QK_HEAD], g_w_in_p[:, 768:]], axis=1)
    late_grads = {"w_in": g_w_in, "w_uq": g_w_uq, "w_uk": g_w_uk, "w_uv": g_w_uv}
    dhn1, *received_late = _mm(dproj, w_in_p, "mm_in_dx", tb=True, tk=1664, exchange=grad_slices(GRADS_LATE, late_grads))
    grad_x, g_mix_norm_pre = _pre_bwd_call(xs, dhn1, dx2, mix_norm_pre)

    results = {}
    for group, received in ((GRADS_EARLY, received_early), (GRADS_LATE, received_late)):
        for (name, _, _, _), rec in zip(group, received):
            results[name] = _adam_call(rec[:, None], given[name], given["m_" + name], given["v_" + name], "adam_" + name)

    small_grads = {"mix_norm_pre": g_mix_norm_pre, "q_norm": g_q_norm, "kv_norm": g_kv_norm,
                   "ssm_lambda_re": dlr, "ssm_lambda_im": dli,
                   "ssm_log_dt": jnp.sum(dldt.reshape(SSM_GROUPS, SSM_STATE), axis=1),
                   "ssm_b_re": dbr, "ssm_b_im": dbi, "ssm_c_re": g_c_re, "ssm_c_im": g_c_im, "ssm_d": g_ssm_d,
                   "b_glu": g_b_glu, "b_gate": g_b_gate, "mix_norm_post": g_mix_norm_post,
                   "ffn_norm_pre": g_ffn_norm_pre, "conv_b": g_conv_b, "ffn_norm_post": g_ffn_norm_post}
    partials = [small_grads[name].reshape((1,) + shp) for name, shp in SMALL]
    all_partials = dict(zip([n for n, _ in SMALL], _all_gather_call(partials, "gather_small_grads", direct=True)))
    rows_like = [n for n, shp in SMALL if len(shp) < 3]
    for name, res in zip(rows_like, _adam_small_call(
            [all_partials[n] for n in rows_like], [given[n] for n in rows_like],
            [given["m_" + n] for n in rows_like], [given["v_" + n] for n in rows_like])):
        results[name] = res
    for name, shp in SMALL:
        if len(shp) == 3:
            results[name] = _adam_call(all_partials[name], given[name], given["m_" + name], given["v_" + name], "adam_" + name)

    order = ["mix_norm_pre", "w_in", "q_norm", "w_uq", "kv_norm", "w_uk", "w_uv", "ssm_lambda_re", "ssm_lambda_im",
             "ssm_log_dt", "ssm_b_re", "ssm_b_im", "ssm_c_re", "ssm_c_im", "ssm_d", "w_glu", "b_glu", "w_branch_attn",
             "w_branch_ssm", "b_gate", "w_out", "mix_norm_post", "ffn_norm_pre", "w_up", "conv_w", "conv_b", "w_down",
             "ffn_norm_post"]
    outs = [loss, grad_x[None]]
    for kind in range(4):
        outs += [results[name][kind] for name in order]
    return tuple(outs)
```

```python
import math

import jax
import jax.numpy as jnp
from jax import lax
from jax.experimental import pallas as pl
from jax.experimental.pallas import tpu as pltpu

F32 = jnp.float32
BF16 = jnp.bfloat16
MESH_ID = pl.DeviceIdType.MESH

N_DEV = 8
LANES = 128
D_MODEL = 1024
N_HEADS = 8
QK_NOPE = 64
QK_ROPE = 32
QK_HEAD = QK_NOPE + QK_ROPE
V_HEAD = 64
Q_RANK = 384
KV_RANK = 256
ROPE_THETA = 10000.0
SSM_WIDTH = 512
SSM_GROUP = 16
SSM_GROUPS = 32
SSM_STATE = 64
SSM_NSTATE = SSM_GROUPS * SSM_STATE
SSM_CHUNKS = 4
D_FF = 2816
EPS = 1e-6
ADAM_LR, ADAM_B1, ADAM_B2, ADAM_EPS, ADAM_WD, ADAM_STEP = 0.001, 0.9, 0.999, 1e-08, 0.01, 10

P_CQ, P_CKV, P_KR, P_U, P_GATE = 0, 384, 640, 768, 1280
P_IN = P_GATE + 2 * D_MODEL
HEAD_PAD = N_HEADS * LANES

PACK_ROWS = 1024
VMEM_BIG = 52 * 1024 * 1024

_GELU_C0 = math.sqrt(2.0 / math.pi)
_GELU_C1 = 0.044715
NEG = -1e30


def _fit(n, pref, mult=LANES):
    if n <= pref:
        return n
    t = (pref // mult) * mult
    while t > 0 and n % t:
        t -= mult
    assert t > 0, (n, pref, mult)
    return t


def _gelu(x):
    return 0.5 * x * (1.0 + jnp.tanh(_GELU_C0 * (x + _GELU_C1 * x * x * x)))


def _gelu_grad(x):
    x2 = x * x
    t = jnp.tanh(_GELU_C0 * x * (1.0 + _GELU_C1 * x2))
    return 0.5 * (1.0 + t) + 0.5 * x * (1.0 - t * t) * _GELU_C0 * (1.0 + 3.0 * _GELU_C1 * x2)


def _sigmoid(x):
    return 1.0 / (1.0 + jnp.exp(-x))


def _dot(a, b, dims):
    return lax.dot_general(a, b, (dims, ((), ())), preferred_element_type=F32)


NN = ((1,), (0,))
NT = ((1,), (1,))
TN = ((0,), (0,))


def _params(*sem, vmem=None):
    return pltpu.CompilerParams(dimension_semantics=tuple(sem), vmem_limit_bytes=vmem)


def _mm(a, b, name, ta=False, tb=False, out_dtype=F32, tm=1024, tn=512, tk=1024, exchange=()):
    if ta:
        K, M = a.shape
    else:
        M, K = a.shape
    if tb:
        N, K2 = b.shape
    else:
        K2, N = b.shape
    assert K == K2, (a.shape, b.shape, ta, tb)
    tm, tn, tk = _fit(M, tm), _fit(N, tn), _fit(K, tk)
    nk = K // tk
    grid = (M // tm, N // tn, nk)
    dims = ((0,) if ta else (1,), (1,) if tb else (0,))
    n = len(exchange)

    def body(a_ref, b_ref, *refs):
        o_ref, scratch = refs[n], refs[2 * n + 1:]
        step = (pl.program_id(0) * grid[1] + pl.program_id(1)) * grid[2] + pl.program_id(2)
        if n:
            start, finish = _exchange_phases(refs[:n], refs[n + 1:2 * n + 1], *scratch[-3:])
            pl.when(step == 0)(start)
        part = _dot(a_ref[...].astype(BF16), b_ref[...].astype(BF16), dims)
        if nk == 1:
            o_ref[...] = part.astype(out_dtype)
        else:
            acc_ref = scratch[0]
            k = pl.program_id(2)

            @pl.when(k == 0)
            def _():
                acc_ref[...] = part

            @pl.when(k > 0)
            def _():
                acc_ref[...] += part

            @pl.when(k == nk - 1)
            def _():
                o_ref[...] = acc_ref[...].astype(out_dtype)
        if n:
            pl.when(step == grid[0] * grid[1] * grid[2] - 1)(finish)

    a_spec = pl.BlockSpec((tk, tm), lambda i, j, k: (k, i)) if ta else pl.BlockSpec((tm, tk), lambda i, j, k: (i, k))
    b_spec = pl.BlockSpec((tn, tk), lambda i, j, k: (j, k)) if tb else pl.BlockSpec((tk, tn), lambda i, j, k: (k, j))
    out = pl.pallas_call(
        body, name=name, grid=grid,
        in_specs=[a_spec, b_spec] + [ANY_SPEC] * n,
        out_specs=[pl.BlockSpec((tm, tn), lambda i, j, k: (i, j))] + [ANY_SPEC] * n,
        out_shape=[jax.ShapeDtypeStruct((M, N), out_dtype)] + [jax.ShapeDtypeStruct(p.shape, p.dtype) for p in exchange],
        scratch_shapes=([] if nk == 1 else [pltpu.VMEM((tm, tn), F32)]) + (_comm_sems(n) if n else []),
        compiler_params=_params(*(("arbitrary",) * 3 if n else ("parallel", "parallel", "arbitrary"))))(a, b, *exchange)
    return out if n else out[0]


def _row(tl, n, col=0):
    return pl.BlockSpec((tl, n), lambda i: (i, col))


def _full(shape):
    return pl.BlockSpec(shape, lambda i: (0,) * len(shape))


def _rms(x, g):
    r = lax.rsqrt(jnp.mean(x * x, axis=-1, keepdims=True) + EPS)
    return x * r * g


def _rms_bwd(x, g, dy):
    n = x.shape[-1]
    r = lax.rsqrt(jnp.mean(x * x, axis=-1, keepdims=True) + EPS)
    gy = dy * g
    dx = r * gy - x * (r * r * r * (1.0 / n)) * jnp.sum(x * gy, axis=-1, keepdims=True)
    return dx, jnp.sum(dy * x * r, axis=0, keepdims=True)


def _acc(ref, first, val):
    @pl.when(first)
    def _():
        ref[...] = val

    @pl.when(jnp.logical_not(first))
    def _():
        ref[...] += val


def _rms_fwd_call(x, g, name):
    L, n = x.shape
    tl = _fit(L, 512)

    def body(x_ref, g_ref, o_ref):
        o_ref[...] = _rms(x_ref[...], g_ref[...]).astype(BF16)

    return pl.pallas_call(
        body, name=name, grid=(L // tl,), in_specs=[_row(tl, n), _full((1, n))], out_specs=_row(tl, n),
        out_shape=jax.ShapeDtypeStruct((L, n), BF16), compiler_params=_params("parallel"))(x, g)


def _mla_norms_call(proj, q_norm, kv_norm):
    L = proj.shape[0]
    tl = _fit(L, 512)

    def body(p_ref, gq_ref, gk_ref, qn_ref, kn_ref):
        p = p_ref[...]
        qn_ref[...] = _rms(p[:, P_CQ:P_CKV], gq_ref[...]).astype(BF16)
        kn_ref[...] = _rms(p[:, P_CKV:P_KR], gk_ref[...]).astype(BF16)

    return pl.pallas_call(
        body, name="mla_norms", grid=(L // tl,),
        in_specs=[_row(tl, P_KR), _full((1, Q_RANK)), _full((1, KV_RANK))],
        out_specs=[_row(tl, Q_RANK), _row(tl, KV_RANK)],
        out_shape=[jax.ShapeDtypeStruct((L, Q_RANK), BF16), jax.ShapeDtypeStruct((L, KV_RANK), BF16)],
        compiler_params=_params("parallel"))(proj, q_norm, kv_norm)


def _rope_lanes(shape):
    lane = lax.broadcasted_iota(jnp.int32, shape, 1)
    return lane, jnp.logical_and(lane >= QK_NOPE, lane < QK_HEAD)


def _rope_apply(x, cosf, sinf, lane):
    rot = jnp.where(lane < QK_NOPE + QK_ROPE // 2, -pltpu.roll(x, LANES - QK_ROPE // 2, 1), pltpu.roll(x, QK_ROPE // 2, 1))
    return x * cosf + rot * sinf


def _rope_apply_t(dy, cosf, sinf, lane, is_rope):
    g = dy * sinf
    rot_t = jnp.where(lane < QK_NOPE + QK_ROPE // 2, pltpu.roll(g, LANES - QK_ROPE // 2, 1), -pltpu.roll(g, QK_ROPE // 2, 1))
    return dy * cosf + jnp.where(is_rope, rot_t, 0.0)


def _mla_prep_call(q_pad, kv_pad, proj, pos_col, inv_freq):
    L = q_pad.shape[0]
    tl = _fit(L, 512)

    def body(q_ref, kv_ref, kr_ref, pos_ref, f_ref, qo_ref, kvo_ref, cos_ref, sin_ref):
        lane, is_rope = _rope_lanes((tl, LANES))
        ang = pos_ref[...] * f_ref[...]
        cosf = jnp.where(is_rope, jnp.cos(ang), jnp.where(lane < QK_NOPE, 1.0, 0.0))
        sinf = jnp.where(is_rope, jnp.sin(ang), 0.0)
        cos_ref[...] = cosf
        sin_ref[...] = sinf
        kr = _rope_apply(kr_ref[...], cosf, sinf, lane)
        for h in range(N_HEADS):
            qh = _rope_apply(q_ref[:, h * LANES:(h + 1) * LANES], cosf, sinf, lane)
            qo_ref[:, h * LANES:(h + 1) * LANES] = (qh * Q_PRESCALE).astype(BF16)
            kvo_ref[:, 2 * h * LANES:(2 * h + 1) * LANES] = (kv_ref[:, 2 * h * LANES:(2 * h + 1) * LANES] + kr).astype(BF16)
            vh = jnp.where(lane == V_HEAD, 1.0, kv_ref[:, (2 * h + 1) * LANES:(2 * h + 2) * LANES])
            kvo_ref[:, (2 * h + 1) * LANES:(2 * h + 2) * LANES] = vh.astype(BF16)

    return pl.pallas_call(
        body, name="mla_prep", grid=(L // tl,),
        in_specs=[_row(tl, HEAD_PAD), _row(tl, 2 * HEAD_PAD), _row(tl, LANES, P_KR // LANES), _row(tl, 1), _full((1, LANES))],
        out_specs=[_row(tl, HEAD_PAD), _row(tl, 2 * HEAD_PAD), _row(tl, LANES), _row(tl, LANES)],
        out_shape=[jax.ShapeDtypeStruct((L, HEAD_PAD), BF16), jax.ShapeDtypeStruct((L, 2 * HEAD_PAD), BF16),
                   jax.ShapeDtypeStruct((L, LANES), F32), jax.ShapeDtypeStruct((L, LANES), F32)],
        compiler_params=_params("parallel"))(q_pad, kv_pad, proj, pos_col, inv_freq)


def _mla_prep_bwd_call(dq, dkv, cosf, sinf):
    L = dq.shape[0]
    tl = _fit(L, 512)

    def body(dq_ref, dkv_ref, cos_ref, sin_ref, dqo_ref, dkvo_ref, dkr_ref):
        lane, is_rope = _rope_lanes((tl, LANES))
        cosf, sinf = cos_ref[...], sin_ref[...]
        dk_sum = jnp.zeros((tl, LANES), F32)
        for h in range(N_HEADS):
            dqo_ref[:, h * LANES:(h + 1) * LANES] = _rope_apply_t(dq_ref[:, h * LANES:(h + 1) * LANES], cosf, sinf, lane, is_rope).astype(BF16)
            dk_sum = dk_sum + dkv_ref[:, 2 * h * LANES:(2 * h + 1) * LANES]
        dkvo_ref[...] = dkv_ref[...].astype(BF16)
        dkr_ref[...] = _rope_apply_t(dk_sum, cosf, sinf, lane, is_rope).astype(BF16)

    return pl.pallas_call(
        body, name="mla_prep_bwd", grid=(L // tl,),
        in_specs=[_row(tl, HEAD_PAD), _row(tl, 2 * HEAD_PAD), _row(tl, LANES), _row(tl, LANES)],
        out_specs=[_row(tl, HEAD_PAD), _row(tl, 2 * HEAD_PAD), _row(tl, LANES)],
        out_shape=[jax.ShapeDtypeStruct((L, HEAD_PAD), BF16), jax.ShapeDtypeStruct((L, 2 * HEAD_PAD), BF16),
                   jax.ShapeDtypeStruct((L, LANES), BF16)],
        compiler_params=_params("parallel"))(dq, dkv, cosf, sinf)


def _mla_norms_bwd_call(proj, dqn, dkn, q_norm, kv_norm):
    L = proj.shape[0]
    tl = _fit(L, 512)

    def body(p_ref, dqn_ref, dkn_ref, gq_ref, gk_ref, d_ref, dgq_ref, dgk_ref):
        first = pl.program_id(0) == 0
        p = p_ref[...]
        dq, dgq = _rms_bwd(p[:, P_CQ:P_CKV], gq_ref[...], dqn_ref[...])
        dk, dgk = _rms_bwd(p[:, P_CKV:P_KR], gk_ref[...], dkn_ref[...])
        d_ref[:, P_CQ:P_CKV] = dq.astype(BF16)
        d_ref[:, P_CKV:P_KR] = dk.astype(BF16)
        _acc(dgq_ref, first, dgq)
        _acc(dgk_ref, first, dgk)

    return pl.pallas_call(
        body, name="mla_norms_bwd", grid=(L // tl,),
        in_specs=[_row(tl, P_KR), _row(tl, Q_RANK), _row(tl, KV_RANK), _full((1, Q_RANK)), _full((1, KV_RANK))],
        out_specs=[_row(tl, P_KR), _full((1, Q_RANK)), _full((1, KV_RANK))],
        out_shape=[jax.ShapeDtypeStruct((L, P_KR), BF16), jax.ShapeDtypeStruct((1, Q_RANK), F32),
                   jax.ShapeDtypeStruct((1, KV_RANK), F32)],
        compiler_params=_params("arbitrary"))(proj, dqn, dkn, q_norm, kv_norm)


GATE_TILE = 256


def _merge_call(proj, b_gate, pa, ps):
    L = proj.shape[0]
    tl = _fit(L, 512)
    nc = D_MODEL // GATE_TILE
    g0, g1 = P_GATE // GATE_TILE, (P_GATE + D_MODEL) // GATE_TILE

    def body(l0_ref, l1_ref, b0_ref, b1_ref, pa_ref, ps_ref, o_ref):
        s0 = _sigmoid(l0_ref[...] + b0_ref[...])
        s1 = _sigmoid(l1_ref[...] + b1_ref[...])
        o_ref[...] = (s0 * pa_ref[...] + s1 * ps_ref[...]).astype(BF16)

    blk = lambda off: pl.BlockSpec((tl, GATE_TILE), lambda i, j: (i, off + j))
    bias = lambda off: pl.BlockSpec((1, GATE_TILE), lambda i, j: (0, off + j))
    return pl.pallas_call(
        body, name="merge", grid=(L // tl, nc),
        in_specs=[blk(g0), blk(g1), bias(0), bias(nc), blk(0), blk(0)],
        out_specs=blk(0), out_shape=jax.ShapeDtypeStruct((L, D_MODEL), BF16),
        compiler_params=_params("parallel", "parallel"))(proj, proj, b_gate, b_gate, pa, ps)


def _merge_bwd_call(dm, proj, b_gate, pa, ps):
    L = proj.shape[0]
    tl = _fit(L, 512)
    nc = D_MODEL // GATE_TILE
    g0, g1 = P_GATE // GATE_TILE, (P_GATE + D_MODEL) // GATE_TILE

    def body(dm_ref, l0_ref, l1_ref, b0_ref, b1_ref, pa_ref, ps_ref, dpa_ref, dps_ref, dl0_ref, dl1_ref, db0_ref, db1_ref):
        first = pl.program_id(1) == 0
        dm_ = dm_ref[...]
        s0 = _sigmoid(l0_ref[...] + b0_ref[...])
        s1 = _sigmoid(l1_ref[...] + b1_ref[...])
        dpa_ref[...] = (dm_ * s0).astype(BF16)
        dps_ref[...] = (dm_ * s1).astype(BF16)
        dl0 = dm_ * pa_ref[...] * s0 * (1.0 - s0)
        dl1 = dm_ * ps_ref[...] * s1 * (1.0 - s1)
        dl0_ref[...] = dl0.astype(BF16)
        dl1_ref[...] = dl1.astype(BF16)
        _acc(db0_ref, first, jnp.sum(dl0, axis=0, keepdims=True))
        _acc(db1_ref, first, jnp.sum(dl1, axis=0, keepdims=True))

    blk = lambda off: pl.BlockSpec((tl, GATE_TILE), lambda j, i: (i, off + j))
    bias = lambda off: pl.BlockSpec((1, GATE_TILE), lambda j, i: (0, off + j))
    act = jax.ShapeDtypeStruct((L, D_MODEL), BF16)
    vec = jax.ShapeDtypeStruct((1, D_MODEL), F32)
    return pl.pallas_call(
        body, name="merge_bwd", grid=(nc, L // tl),
        in_specs=[blk(0), blk(g0), blk(g1), bias(0), bias(nc), blk(0), blk(0)],
        out_specs=[blk(0), blk(0), blk(0), blk(0), bias(0), bias(0)],
        out_shape=[act, act, act, act, vec, vec],
        compiler_params=_params("parallel", "arbitrary"))(dm, proj, proj, b_gate, b_gate, pa, ps)


def _post_mix_call(o, x, g_post, g_fpre):
    L, n = x.shape
    tl = _fit(L, 512)

    def body(o_ref, x_ref, gp_ref, gf_ref, x2_ref, hn_ref):
        x2 = x_ref[...] + _rms(o_ref[...], gp_ref[...])
        x2_ref[...] = x2
        hn_ref[...] = _rms(x2, gf_ref[...]).astype(BF16)

    return pl.pallas_call(
        body, name="post_mix", grid=(L // tl,),
        in_specs=[_row(tl, n), _row(tl, n), _full((1, n)), _full((1, n))],
        out_specs=[_row(tl, n), _row(tl, n)],
        out_shape=[jax.ShapeDtypeStruct((L, n), F32), jax.ShapeDtypeStruct((L, n), BF16)],
        compiler_params=_params("parallel"))(o, x, g_post, g_fpre)


def _ffn_out_call(ff, x2, target, g_fpost):
    L, n = x2.shape
    tl = _fit(L, 512)

    def body(ff_ref, x2_ref, t_ref, g_ref, loss_ref, dy_ref, dff_ref, dg_ref):
        first = pl.program_id(0) == 0
        ff_ = ff_ref[...]
        err = x2_ref[...] + _rms(ff_, g_ref[...]) - t_ref[...]
        part = 0.5 * jnp.sum(jnp.sum(err * err, axis=-1, keepdims=True) * (1.0 / n), axis=0, keepdims=True)
        dy = err * (1.0 / n)
        dy_ref[...] = dy
        dff, dg = _rms_bwd(ff_, g_ref[...], dy)
        dff_ref[...] = dff.astype(BF16)
        _acc(loss_ref, first, jnp.broadcast_to(part, (1, LANES)))
        _acc(dg_ref, first, dg)

    return pl.pallas_call(
        body, name="ffn_out", grid=(L // tl,),
        in_specs=[_row(tl, n), _row(tl, n), _row(tl, n), _full((1, n))],
        out_specs=[_full((1, LANES)), _row(tl, n), _row(tl, n), _full((1, n))],
        out_shape=[jax.ShapeDtypeStruct((1, LANES), F32), jax.ShapeDtypeStruct((L, n), F32),
                   jax.ShapeDtypeStruct((L, n), BF16), jax.ShapeDtypeStruct((1, n), F32)],
        compiler_params=_params("arbitrary"))(ff, x2, target, g_fpost)


def _post_bwd_call(x2, dhn2, dy, o, g_fpre, g_post):
    L, n = x2.shape
    tl = _fit(L, 512)

    def body(x2_ref, dh_ref, dy_ref, o_ref, gf_ref, gp_ref, dx2_ref, do_ref, dgf_ref, dgp_ref):
        first = pl.program_id(0) == 0
        d1, dgf = _rms_bwd(x2_ref[...], gf_ref[...], dh_ref[...])
        dx2 = dy_ref[...] + d1
        dx2_ref[...] = dx2
        do, dgp = _rms_bwd(o_ref[...], gp_ref[...], dx2)
        do_ref[...] = do.astype(BF16)
        _acc(dgf_ref, first, dgf)
        _acc(dgp_ref, first, dgp)

    return pl.pallas_call(
        body, name="post_bwd", grid=(L // tl,),
        in_specs=[_row(tl, n), _row(tl, n), _row(tl, n), _row(tl, n), _full((1, n)), _full((1, n))],
        out_specs=[_row(tl, n), _row(tl, n), _full((1, n)), _full((1, n))],
        out_shape=[jax.ShapeDtypeStruct((L, n), F32), jax.ShapeDtypeStruct((L, n), BF16),
                   jax.ShapeDtypeStruct((1, n), F32), jax.ShapeDtypeStruct((1, n), F32)],
        compiler_params=_params("arbitrary"))(x2, dhn2, dy, o, g_fpre, g_post)


def _pre_bwd_call(x, dhn1, dx2, g_pre):
    L, n = x.shape
    tl = _fit(L, 512)

    def body(x_ref, dh_ref, dx2_ref, g_ref, dx_ref, dg_ref):
        first = pl.program_id(0) == 0
        d1, dg = _rms_bwd(x_ref[...], g_ref[...], dh_ref[...])
        dx_ref[...] = dx2_ref[...] + d1
        _acc(dg_ref, first, dg)

    return pl.pallas_call(
        body, name="pre_bwd", grid=(L // tl,),
        in_specs=[_row(tl, n), _row(tl, n), _row(tl, n), _full((1, n))],
        out_specs=[_row(tl, n), _full((1, n))],
        out_shape=[jax.ShapeDtypeStruct((L, n), F32), jax.ShapeDtypeStruct((1, n), F32)],
        compiler_params=_params("arbitrary"))(x, dhn1, dx2, g_pre)


CONV_TILE = 256
HALO = 16


def _shift_down(cur, halo_tail, by):
    rolled = pltpu.roll(cur, by, 0)
    r8 = lax.broadcasted_iota(jnp.int32, halo_tail.shape, 0)
    head = jnp.where(r8 < by, pltpu.roll(halo_tail, by, 0), rolled[0:8])
    return jnp.concatenate([head, rolled[8:]], axis=0)


def _shift_up(cur, halo_head, by):
    n = cur.shape[0]
    rolled = pltpu.roll(cur, n - by, 0)
    r8 = lax.broadcasted_iota(jnp.int32, halo_head.shape, 0)
    tail = jnp.where(r8 >= 8 - by, pltpu.roll(halo_head, 8 - by, 0), rolled[n - 8:])
    return jnp.concatenate([rolled[:n - 8], tail], axis=0)


def _conv_fwd_vals(cur, halo, w, b, not_first):
    tail = halo[HALO - 8:] * not_first
    s1 = _shift_down(cur, tail, 1)
    s2 = _shift_down(cur, tail, 2)
    return b + w[2:3] * cur + w[1:2] * s1 + w[0:1] * s2, s1, s2


def _conv_specs(L, tl, nc, rows_inner):
    nh = tl // HALO
    if rows_inner:
        ij = lambda f: (lambda j, i: f(i, j))
    else:
        ij = lambda f: f
    cur = lambda off: pl.BlockSpec((tl, CONV_TILE), ij(lambda i, j: (i, off + j)))
    prev = lambda off: pl.BlockSpec((HALO, CONV_TILE), ij(lambda i, j: (jnp.maximum(i * nh - 1, 0), off + j)))
    nxt = lambda off: pl.BlockSpec((HALO, CONV_TILE), ij(lambda i, j: (jnp.minimum((i + 1) * nh, L // HALO - 1), off + j)))
    par = lambda rows, off: pl.BlockSpec((rows, CONV_TILE), ij(lambda i, j: (0, off + j)))
    return cur, prev, nxt, par


def _conv_act_call(h, conv_w, conv_b):
    L = h.shape[0]
    tl = _fit(L, 512)
    nc = D_FF // CONV_TILE
    cur, prev, _, par = _conv_specs(L, tl, nc, False)

    def body(hg_ref, hv_ref, pg_ref, pv_ref, wg_ref, wv_ref, bg_ref, bv_ref, a_ref):
        not_first = (pl.program_id(0) > 0).astype(F32)
        gate, _, _ = _conv_fwd_vals(hg_ref[...], pg_ref[...], wg_ref[...], bg_ref[...], not_first)
        val, _, _ = _conv_fwd_vals(hv_ref[...], pv_ref[...], wv_ref[...], bv_ref[...], not_first)
        a_ref[...] = (_gelu(gate) * val).astype(BF16)

    return pl.pallas_call(
        body, name="conv_act", grid=(L // tl, nc),
        in_specs=[cur(0), cur(nc), prev(0), prev(nc), par(3, 0), par(3, nc), par(1, 0), par(1, nc)],
        out_specs=cur(0), out_shape=jax.ShapeDtypeStruct((L, D_FF), BF16),
        compiler_params=_params("parallel", "parallel"))(h, h, h, h, conv_w, conv_w, conv_b, conv_b)


def _conv_act_bwd_call(da, h, conv_w, conv_b):
    L = h.shape[0]
    tl = _fit(L, 512)
    nc = D_FF // CONV_TILE
    cur, prev, _, par = _conv_specs(L, tl, nc, True)

    def body(da_ref, hg_ref, hv_ref, pg_ref, pv_ref, wg_ref, wv_ref, bg_ref, bv_ref,
             dg_ref, dv_ref, dwg_ref, dwv_ref, dbg_ref, dbv_ref):
        first = pl.program_id(1) == 0
        not_first = (pl.program_id(1) > 0).astype(F32)
        hg, hv = hg_ref[...], hv_ref[...]
        gate, g1, g2 = _conv_fwd_vals(hg, pg_ref[...], wg_ref[...], bg_ref[...], not_first)
        val, v1, v2 = _conv_fwd_vals(hv, pv_ref[...], wv_ref[...], bv_ref[...], not_first)
        da_ = da_ref[...]
        dgate = da_ * val * _gelu_grad(gate)
        dval = da_ * _gelu(gate)
        dg_ref[...] = dgate.astype(BF16)
        dv_ref[...] = dval.astype(BF16)
        col = lambda t: jnp.sum(t, axis=0, keepdims=True)
        _acc(dwg_ref, first, jnp.concatenate([col(dgate * g2), col(dgate * g1), col(dgate * hg)], axis=0))
        _acc(dwv_ref, first, jnp.concatenate([col(dval * v2), col(dval * v1), col(dval * hv)], axis=0))
        _acc(dbg_ref, first, col(dgate))
        _acc(dbv_ref, first, col(dval))

    act = jax.ShapeDtypeStruct((L, D_FF), BF16)
    w3 = jax.ShapeDtypeStruct((3, D_FF), F32)
    w1 = jax.ShapeDtypeStruct((1, D_FF), F32)
    return pl.pallas_call(
        body, name="conv_act_bwd", grid=(nc, L // tl),
        in_specs=[cur(0), cur(0), cur(nc), prev(0), prev(nc), par(3, 0), par(3, nc), par(1, 0), par(1, nc)],
        out_specs=[cur(0), cur(0), par(3, 0), par(3, 0), par(1, 0), par(1, 0)],
        out_shape=[act, act, w3, w3, w1, w1],
        compiler_params=_params("parallel", "arbitrary"))(da, h, h, h, h, conv_w, conv_w, conv_b, conv_b)


def _conv_t_call(dgate, dval, conv_w):
    L = dgate.shape[0]
    tl = _fit(L, 512)
    nc = D_FF // CONV_TILE
    cur, _, nxt, par = _conv_specs(L, tl, nc, False)

    def run(d, off, name):
        def body(d_ref, n_ref, w_ref, o_ref):
            not_last = (pl.program_id(0) < L // tl - 1).astype(F32)
            c = d_ref[...].astype(F32)
            head = n_ref[...].astype(F32)[0:8] * not_last
            w = w_ref[...]
            o_ref[...] = (w[2:3] * c + w[1:2] * _shift_up(c, head, 1) + w[0:1] * _shift_up(c, head, 2)).astype(BF16)

        return pl.pallas_call(
            body, name=name, grid=(L // tl, nc),
            in_specs=[cur(0), nxt(0), par(3, off)],
            out_specs=cur(0), out_shape=jax.ShapeDtypeStruct((L, D_FF), BF16),
            compiler_params=_params("parallel", "parallel"))(d, d, conv_w)

    return run(dgate, 0, "conv_t_gate"), run(dval, nc, "conv_t_val")


def _glu_call(y1, w_glu, b_glu):
    L, n = y1.shape
    tl = _fit(L, 512)

    def body(y_ref, w_ref, b_ref, o_ref):
        y2 = _gelu(y_ref[...])
        z = _dot(y2.astype(BF16), w_ref[...], NN) + b_ref[...]
        o_ref[...] = (y2 * _sigmoid(z)).astype(BF16)

    return pl.pallas_call(
        body, name="glu", grid=(L // tl,), in_specs=[_row(tl, n), _full((n, n)), _full((1, n))],
        out_specs=_row(tl, n), out_shape=jax.ShapeDtypeStruct((L, n), BF16),
        compiler_params=_params("parallel"))(y1, w_glu, b_glu)


def _glu_bwd_call(dout, y1, w_glu, b_glu):
    L, n = y1.shape
    tl = _fit(L, 512)

    def body(do_ref, y_ref, w_ref, b_ref, dy_ref, dw_ref, db_ref):
        first = pl.program_id(0) == 0
        y1_ = y_ref[...]
        y2 = _gelu(y1_)
        y2b = y2.astype(BF16)
        w = w_ref[...]
        sg = _sigmoid(_dot(y2b, w, NN) + b_ref[...])
        dout_ = do_ref[...].astype(F32)
        dz = dout_ * y2 * sg * (1.0 - sg)
        dzb = dz.astype(BF16)
        dy2 = dout_ * sg + _dot(dzb, w, NT)
        dy_ref[...] = dy2 * _gelu_grad(y1_)
        _acc(dw_ref, first, _dot(y2b, dzb, TN))
        _acc(db_ref, first, jnp.sum(dz, axis=0, keepdims=True))

    return pl.pallas_call(
        body, name="glu_bwd", grid=(L // tl,),
        in_specs=[_row(tl, n), _row(tl, n), _full((n, n)), _full((1, n))],
        out_specs=[_row(tl, n), _full((n, n)), _full((1, n))],
        out_shape=[jax.ShapeDtypeStruct((L, n), F32), jax.ShapeDtypeStruct((n, n), F32), jax.ShapeDtypeStruct((1, n), F32)],
        compiler_params=_params("arbitrary"))(dout, y1, w_glu, b_glu)


ATTN_TILE = 512
ATTN_SCALE = 1.0 / math.sqrt(QK_HEAD)


ATTN_HEADS = 2
ATTN_GROUPS = N_HEADS // ATTN_HEADS
LOG2E = 1.0 / math.log(2.0)
Q_PRESCALE = ATTN_SCALE * LOG2E
ANY_SPEC = pl.BlockSpec(memory_space=pl.ANY)


def _attn_fwd_call(q, kv, blocks):
    L = q.shape[0]
    t = _fit(L, ATTN_TILE)
    nq = L // t
    n = len(blocks)

    def body(q_ref, kv_ref, *refs):
        blk_refs, (o_ref, lse_ref), gat_refs = refs[:n], refs[n:n + 2], refs[n + 2:2 * n + 2]
        m_s, acc_s, send_sems, recv_sems, local_sems = refs[2 * n + 2:]
        g, i = pl.program_id(0), pl.program_id(1)
        start, forward, finish = _gather_phases(blk_refs, gat_refs, send_sems, recv_sems, local_sems)
        pl.when(jnp.logical_and(g == 0, i == 0))(start)
        m_s[...] = jnp.full((ATTN_HEADS, t, 1), NEG, F32)
        acc_s[...] = jnp.zeros((ATTN_HEADS, t, LANES), F32)
        below = lax.broadcasted_iota(jnp.int32, (t, t), 1) <= lax.broadcasted_iota(jnp.int32, (t, t), 0)

        def block_step(kb, on_diagonal):
            rows = pl.ds(pl.multiple_of(kb * t, t), t)
            for a in range(ATTN_HEADS):
                s = _dot(q_ref[:, a * LANES:(a + 1) * LANES], kv_ref[rows, 2 * a * LANES:(2 * a + 1) * LANES], NT)
                if on_diagonal:
                    s = jnp.where(below, s, NEG)
                m_prev = m_s[a]
                m_new = jnp.maximum(m_prev, jnp.max(s, axis=1, keepdims=True))
                p = jnp.exp2(s - m_new)
                pv = _dot(p.astype(BF16), kv_ref[rows, (2 * a + 1) * LANES:(2 * a + 2) * LANES], NN)
                acc_s[a] = jnp.exp2(m_prev - m_new) * acc_s[a] + pv
                m_s[a] = m_new

        def step(kb, carry):
            block_step(kb, False)
            return carry

        lax.fori_loop(0, i, step, 0)
        block_step(i, True)
        lane = lax.broadcasted_iota(jnp.int32, (t, LANES), 1)
        for a in range(ATTN_HEADS):
            acc = acc_s[a]
            l = jnp.sum(jnp.where(lane == V_HEAD, acc, 0.0), axis=1, keepdims=True)
            o_ref[:, a * LANES:(a + 1) * LANES] = (acc / l).astype(BF16)
            lse_ref[a] = m_s[a] + jnp.log(l) * LOG2E
        pl.when(jnp.logical_and(g == (3 * ATTN_GROUPS) // 4, i == 0))(forward)
        pl.when(jnp.logical_and(g == ATTN_GROUPS - 1, i == nq - 1))(finish)

    gw = ATTN_HEADS * LANES
    return pl.pallas_call(
        body, name="attn_fwd", grid=(ATTN_GROUPS, nq),
        in_specs=[pl.BlockSpec((t, gw), lambda g, i: (i, g)),
                  pl.BlockSpec((L, 2 * gw), lambda g, i: (0, g))] + [ANY_SPEC] * n,
        out_specs=[pl.BlockSpec((t, gw), lambda g, i: (i, g)),
                   pl.BlockSpec((ATTN_HEADS, t, 1), lambda g, i: (g, i, 0))] + [ANY_SPEC] * n,
        out_shape=[jax.ShapeDtypeStruct((L, HEAD_PAD), BF16), jax.ShapeDtypeStruct((N_HEADS, L, 1), F32)]
        + [jax.ShapeDtypeStruct((N_DEV,) + b.shape, b.dtype) for b in blocks],
        scratch_shapes=[pltpu.VMEM((ATTN_HEADS, t, 1), F32), pltpu.VMEM((ATTN_HEADS, t, LANES), F32)] + _comm_sems(n),
        compiler_params=_params("arbitrary", "arbitrary"))(q, kv, *blocks)


def _attn_bwd_call(q, kv, o, do, lse, parts, blocks):
    L = q.shape[0]
    t = _fit(L, ATTN_TILE)
    nq = L // t
    n1, n = len(parts), len(parts) + len(blocks)

    def body(q_ref, do_ref, o_ref, lse_ref, kv_ref, *refs):
        in_refs, (dq_ref, dkv_ref), out_refs = refs[:n], refs[n:n + 2], refs[n + 2:2 * n + 2]
        dk_s, dv_s = refs[2 * n + 2:2 * n + 4]
        g, j = pl.program_id(0), pl.program_id(1)
        start, finish = _exchange_phases(in_refs[:n1], out_refs[:n1], *refs[2 * n + 4:2 * n + 7])
        start_blocks, finish_blocks = _exchange_phases(in_refs[n1:], out_refs[n1:], *refs[2 * n + 7:], same_source=True)

        @pl.when(jnp.logical_and(g == 0, j == 0))
        def _():
            start()
            start_blocks()

        @pl.when(j == 0)
        def _():
            dq_ref[...] = jnp.zeros((L, ATTN_HEADS * LANES), F32)

        dk_s[...] = jnp.zeros((ATTN_HEADS, t, LANES), F32)
        dv_s[...] = jnp.zeros((ATTN_HEADS, t, LANES), F32)
        below = lax.broadcasted_iota(jnp.int32, (t, t), 1) <= lax.broadcasted_iota(jnp.int32, (t, t), 0)

        def block_step(i, on_diagonal):
            rows = pl.ds(pl.multiple_of(i * t, t), t)
            for a in range(ATTN_HEADS):
                lanes = slice(a * LANES, (a + 1) * LANES)
                qi = q_ref[rows, lanes]
                doi = do_ref[rows, lanes]
                kblk = kv_ref[:, 2 * a * LANES:(2 * a + 1) * LANES]
                delta = jnp.sum(doi.astype(F32) * o_ref[rows, lanes].astype(F32), axis=1, keepdims=True)
                s = _dot(qi, kblk, NT)
                if on_diagonal:
                    s = jnp.where(below, s, NEG)
                p = jnp.exp2(s - lse_ref[a, rows, :])
                dv_s[a] += _dot(p.astype(BF16), doi, TN)
                ds = (p * (_dot(doi, kv_ref[:, (2 * a + 1) * LANES:(2 * a + 2) * LANES], NT) - delta)).astype(BF16)
                dk_s[a] += _dot(ds, qi, TN)
                dq_ref[rows, lanes] += _dot(ds, kblk, NN) * ATTN_SCALE

        def step(i, carry):
            block_step(i, False)
            return carry

        block_step(j, True)
        lax.fori_loop(j + 1, nq, step, 0)
        for a in range(ATTN_HEADS):
            dkv_ref[:, 2 * a * LANES:(2 * a + 1) * LANES] = dk_s[a] * (1.0 / LOG2E)
            dkv_ref[:, (2 * a + 1) * LANES:(2 * a + 2) * LANES] = dv_s[a]
        @pl.when(jnp.logical_and(g == ATTN_GROUPS - 1, j == nq - 1))
        def _():
            finish()
            finish_blocks()

    gw = ATTN_HEADS * LANES
    whole = lambda: pl.BlockSpec((L, gw), lambda g, j: (0, g))
    acc = pltpu.VMEM((ATTN_HEADS, t, LANES), F32)
    return pl.pallas_call(
        body, name="attn_bwd", grid=(ATTN_GROUPS, nq),
        in_specs=[whole(), whole(), whole(), pl.BlockSpec((ATTN_HEADS, L, 1), lambda g, j: (g, 0, 0)),
                  pl.BlockSpec((t, 2 * gw), lambda g, j: (j, g))] + [ANY_SPEC] * n,
        out_specs=[whole(), pl.BlockSpec((t, 2 * gw), lambda g, j: (j, g))] + [ANY_SPEC] * n,
        out_shape=[jax.ShapeDtypeStruct((L, HEAD_PAD), F32), jax.ShapeDtypeStruct((L, 2 * HEAD_PAD), F32)]
        + [jax.ShapeDtypeStruct(p.shape, p.dtype) for p in parts]
        + [jax.ShapeDtypeStruct((N_DEV,) + b.shape, b.dtype) for b in blocks],
        scratch_shapes=[acc, acc] + _comm_sems(n1) + _comm_sems(n - n1),
        compiler_params=_params("arbitrary", "arbitrary"))(q, do, o, lse, kv, *parts, *blocks)


def _disc(lr, li, ldt, br, bi):
    dt = jnp.exp(ldt)
    mag = jnp.exp(lr * dt)
    ang = li * dt
    a_re, a_im = mag * jnp.cos(ang), mag * jnp.sin(ang)
    den = lr * lr + li * li
    n_re, n_im = a_re - 1.0, a_im
    z_re = (n_re * lr + n_im * li) / den
    z_im = (n_im * lr - n_re * li) / den
    return a_re, a_im, z_re * br - z_im * bi, z_re * bi + z_im * br


def _disc_call(lr, li, ldt, br, bi):
    def body(lr_ref, li_ref, ldt_ref, br_ref, bi_ref, ar_ref, ai_ref, bbr_ref, bbi_ref):
        ar_ref[...], ai_ref[...], bbr_ref[...], bbi_ref[...] = _disc(
            lr_ref[...], li_ref[...], ldt_ref[...], br_ref[...], bi_ref[...])

    c1 = jax.ShapeDtypeStruct((SSM_NSTATE, 1), F32)
    c16 = jax.ShapeDtypeStruct((SSM_NSTATE, SSM_GROUP), F32)
    return pl.pallas_call(body, name="ssm_disc", out_shape=[c1, c1, c16, c16])(lr, li, ldt, br, bi)


def _disc_bwd_call(lr, li, ldt, br, bi, dar, dai, dbbr, dbbi):
    def body(lr_ref, li_ref, ldt_ref, br_ref, bi_ref, dar_ref, dai_ref, dbbr_ref, dbbi_ref,
             dlr_ref, dli_ref, dldt_ref, dbr_ref, dbi_ref):
        _, vjp = jax.vjp(_disc, lr_ref[...], li_ref[...], ldt_ref[...], br_ref[...], bi_ref[...])
        dlr_ref[...], dli_ref[...], dldt_ref[...], dbr_ref[...], dbi_ref[...] = vjp(
            (dar_ref[...], dai_ref[...], dbbr_ref[...], dbbi_ref[...]))

    c1 = jax.ShapeDtypeStruct((SSM_NSTATE, 1), F32)
    c16 = jax.ShapeDtypeStruct((SSM_NSTATE, SSM_GROUP), F32)
    return pl.pallas_call(body, name="ssm_disc_bwd", out_shape=[c1, c1, c1, c16, c16])(
        lr, li, ldt, br, bi, dar, dai, dbbr, dbbi)


SSM_ROWS = 512
SSM_CW = SSM_NSTATE // SSM_CHUNKS
SSM_CU = SSM_WIDTH // SSM_CHUNKS


def _cmul(ar, ai, br, bi):
    return ar * br - ai * bi, ar * bi + ai * br


def _power(ar1, ai1, n):
    def step(_, c):
        return _cmul(c[0], c[1], ar1, ai1)

    return lax.fori_loop(0, n, step, (jnp.ones_like(ar1), jnp.zeros_like(ar1)))


def _tile(k):
    return pl.ds(pl.multiple_of(k * 8, 8), 8)


def _ssm_fwd_call(u, a_re, a_im, bb_re, bb_im, cm_re, cm_im, d_skip):
    L = u.shape[0]
    seg = L // 8
    rb = _fit(L, SSM_ROWS)

    def body(u_ref, ar_ref, ai_ref, bbr_ref, bbi_ref, cmr_ref, cmi_ref, d_ref, y_ref, sre_hbm, sim_hbm,
             s_re, s_im, sems):
        q = pl.program_id(0)

        def bu_step(r, c):
            rows = pl.ds(pl.multiple_of(r * rb, rb), rb)
            ub = u_ref[rows, :].astype(BF16)
            s_re[rows, :] = _dot(ub, bbr_ref[0], NN)
            s_im[rows, :] = _dot(ub, bbi_ref[0], NN)
            return c

        lax.fori_loop(0, L // rb, bu_step, 0)
        ar1, ai1 = ar_ref[...], ai_ref[...]
        ar = jnp.broadcast_to(ar1, (8, SSM_CW))
        ai = jnp.broadcast_to(ai1, (8, SSM_CW))

        def local(k, c):
            nr, ni = _cmul(ar, ai, c[0], c[1])
            nr = nr + s_re[_tile(k), :]
            ni = ni + s_im[_tile(k), :]
            s_re[_tile(k), :] = nr
            s_im[_tile(k), :] = ni
            return nr, ni

        zero8 = jnp.zeros((8, SSM_CW), F32)
        lax.fori_loop(0, seg, local, (zero8, zero8))
        pr, pi = _power(ar1, ai1, seg)
        end_r = s_re[pl.ds((seg - 1) * 8, 8), :]
        end_i = s_im[pl.ds((seg - 1) * 8, 8), :]
        er = jnp.zeros((1, SSM_CW), F32)
        ei = jnp.zeros((1, SSM_CW), F32)
        rows_r, rows_i = [er], [ei]
        for j in range(7):
            tr, ti = _cmul(pr, pi, er, ei)
            er, ei = end_r[j:j + 1] + tr, end_i[j:j + 1] + ti
            rows_r.append(er)
            rows_i.append(ei)
        e_r = jnp.concatenate(rows_r, axis=0)
        e_i = jnp.concatenate(rows_i, axis=0)

        def fix(k, c):
            wr, wi = _cmul(c[0], c[1], ar, ai)
            fr, fi = _cmul(wr, wi, e_r, e_i)
            s_re[_tile(k), :] += fr
            s_im[_tile(k), :] += fi
            return wr, wi

        lax.fori_loop(0, seg, fix, (jnp.ones((8, SSM_CW), F32), zero8))
        out_r = pltpu.make_async_copy(s_re, sre_hbm.at[q], sems.at[0])
        out_i = pltpu.make_async_copy(s_im, sim_hbm.at[q], sems.at[1])
        out_r.start()
        out_i.start()

        def y_step(r, c):
            rows = pl.ds(pl.multiple_of(r * rb, rb), rb)
            y = _dot(s_re[rows, :].astype(BF16), cmr_ref[0], NN) - _dot(s_im[rows, :].astype(BF16), cmi_ref[0], NN)
            y_ref[rows, :] = y + d_ref[...] * u_ref[rows, :]
            return c

        lax.fori_loop(0, L // rb, y_step, 0)
        out_r.wait()
        out_i.wait()

    chunk = lambda rows, cols: pl.BlockSpec((rows, cols), lambda q: (0, q))
    mat = lambda r, c: pl.BlockSpec((1, r, c), lambda q: (q, 0, 0))
    anyspec = pl.BlockSpec(memory_space=pl.ANY)
    states = jax.ShapeDtypeStruct((SSM_CHUNKS, L, SSM_CW), F32)
    return pl.pallas_call(
        body, name="ssm_fwd", grid=(SSM_CHUNKS,),
        in_specs=[chunk(L, SSM_CU), chunk(1, SSM_CW), chunk(1, SSM_CW), mat(SSM_CU, SSM_CW), mat(SSM_CU, SSM_CW),
                  mat(SSM_CW, SSM_CU), mat(SSM_CW, SSM_CU), chunk(1, SSM_CU)],
        out_specs=[chunk(L, SSM_CU), anyspec, anyspec],
        out_shape=[jax.ShapeDtypeStruct((L, SSM_WIDTH), F32), states, states],
        scratch_shapes=[pltpu.VMEM((L, SSM_CW), F32), pltpu.VMEM((L, SSM_CW), F32), pltpu.SemaphoreType.DMA((2,))],
        compiler_params=_params("arbitrary", vmem=VMEM_BIG))(u, a_re, a_im, bb_re, bb_im, cm_re, cm_im, d_skip)


def _ssm_bwd_call(dy, u, s_re_all, s_im_all, a_re, a_im, bb_re, bb_im, cm_re, cm_im, d_skip):
    L = u.shape[0]
    seg = L // 8
    rb = _fit(L, SSM_ROWS)

    def body(dy_ref, u_ref, sre_hbm, sim_hbm, ar_ref, ai_ref, bbr_ref, bbi_ref, cmr_ref, cmi_ref, d_ref,
             du_ref, dbbr_ref, dbbi_ref, dcmr_ref, dcmi_ref, dar_ref, dai_ref, dd_ref,
             g_re, g_im, s_re, s_im, sems):
        q = pl.program_id(0)
        in_r = pltpu.make_async_copy(sre_hbm.at[q], s_re, sems.at[0])
        in_i = pltpu.make_async_copy(sim_hbm.at[q], s_im, sems.at[1])
        in_r.start()
        in_i.start()

        def ds_step(r, c):
            rows = pl.ds(pl.multiple_of(r * rb, rb), rb)
            dyb = dy_ref[rows, :].astype(BF16)
            g_re[rows, :] = _dot(dyb, cmr_ref[0], NT)
            g_im[rows, :] = -_dot(dyb, cmi_ref[0], NT)
            return c

        lax.fori_loop(0, L // rb, ds_step, 0)
        ar1, ai1 = ar_ref[...], ai_ref[...]
        ar = jnp.broadcast_to(ar1, (8, SSM_CW))
        nai = jnp.broadcast_to(-ai1, (8, SSM_CW))

        def local(kk, c):
            k = seg - 1 - kk
            nr, ni = _cmul(ar, nai, c[0], c[1])
            nr = nr + g_re[_tile(k), :]
            ni = ni + g_im[_tile(k), :]
            g_re[_tile(k), :] = nr
            g_im[_tile(k), :] = ni
            return nr, ni

        zero8 = jnp.zeros((8, SSM_CW), F32)
        lax.fori_loop(0, seg, local, (zero8, zero8))
        pr, pi = _power(ar1, -ai1, seg)
        head_r = g_re[pl.ds(0, 8), :]
        head_i = g_im[pl.ds(0, 8), :]
        fr = jnp.zeros((1, SSM_CW), F32)
        fi = jnp.zeros((1, SSM_CW), F32)
        rows_r, rows_i = [fr], [fi]
        for j in range(6, -1, -1):
            tr, ti = _cmul(pr, pi, fr, fi)
            fr, fi = head_r[j + 1:j + 2] + tr, head_i[j + 1:j + 2] + ti
            rows_r.insert(0, fr)
            rows_i.insert(0, fi)
        f_r = jnp.concatenate(rows_r, axis=0)
        f_i = jnp.concatenate(rows_i, axis=0)
        in_r.wait()
        in_i.wait()

        def fixed(k, wr, wi):
            xr, xi = _cmul(wr, wi, f_r, f_i)
            gr = g_re[_tile(k), :] + xr
            gi = g_im[_tile(k), :] + xi
            g_re[_tile(k), :] = gr
            g_im[_tile(k), :] = gi
            return gr, gi

        def fix(kk, c):
            k = seg - 1 - kk
            wr, wi = _cmul(c[0], c[1], ar, nai)
            gr, gi = fixed(k, wr, wi)
            pr_, pi_ = s_re[_tile(k - 1), :], s_im[_tile(k - 1), :]
            return wr, wi, c[2] + gr * pr_ + gi * pi_, c[3] + gi * pr_ - gr * pi_

        wr, wi, acc_r, acc_i = lax.fori_loop(0, seg - 1, fix, (jnp.ones((8, SSM_CW), F32), zero8, zero8, zero8))
        wr, wi = _cmul(wr, wi, ar, nai)
        gr, gi = fixed(0, wr, wi)
        row8 = lax.broadcasted_iota(jnp.int32, (8, SSM_CW), 0)
        pr_ = jnp.where(row8 > 0, pltpu.roll(s_re[pl.ds((seg - 1) * 8, 8), :], 1, 0), 0.0)
        pi_ = jnp.where(row8 > 0, pltpu.roll(s_im[pl.ds((seg - 1) * 8, 8), :], 1, 0), 0.0)
        acc_r = acc_r + gr * pr_ + gi * pi_
        acc_i = acc_i + gi * pr_ - gr * pi_
        dar_ref[...] = jnp.sum(acc_r, axis=0, keepdims=True)
        dai_ref[...] = jnp.sum(acc_i, axis=0, keepdims=True)

        dbbr_ref[...] = jnp.zeros((1, SSM_CU, SSM_CW), F32)
        dbbi_ref[...] = jnp.zeros((1, SSM_CU, SSM_CW), F32)
        dcmr_ref[...] = jnp.zeros((1, SSM_CW, SSM_CU), F32)
        dcmi_ref[...] = jnp.zeros((1, SSM_CW, SSM_CU), F32)
        dd_ref[...] = jnp.zeros((1, SSM_CU), F32)

        def grad_step(r, c):
            rows = pl.ds(pl.multiple_of(r * rb, rb), rb)
            ub, dyv = u_ref[rows, :], dy_ref[rows, :]
            ubb, dyb = ub.astype(BF16), dyv.astype(BF16)
            grb, gib = g_re[rows, :].astype(BF16), g_im[rows, :].astype(BF16)
            dbbr_ref[0] += _dot(ubb, grb, TN)
            dbbi_ref[0] += _dot(ubb, gib, TN)
            dcmr_ref[0] += _dot(s_re[rows, :].astype(BF16), dyb, TN)
            dcmi_ref[0] -= _dot(s_im[rows, :].astype(BF16), dyb, TN)
            du_ref[rows, :] = _dot(grb, bbr_ref[0], NT) + _dot(gib, bbi_ref[0], NT) + d_ref[...] * dyv
            dd_ref[...] += jnp.sum(dyv * ub, axis=0, keepdims=True)
            return c

        lax.fori_loop(0, L // rb, grad_step, 0)

    chunk = lambda rows, cols: pl.BlockSpec((rows, cols), lambda q: (0, q))
    mat = lambda r, c: pl.BlockSpec((1, r, c), lambda q: (q, 0, 0))
    anyspec = pl.BlockSpec(memory_space=pl.ANY)
    big = lambda: pltpu.VMEM((L, SSM_CW), F32)
    return pl.pallas_call(
        body, name="ssm_bwd", grid=(SSM_CHUNKS,),
        in_specs=[chunk(L, SSM_CU), chunk(L, SSM_CU), anyspec, anyspec, chunk(1, SSM_CW), chunk(1, SSM_CW),
                  mat(SSM_CU, SSM_CW), mat(SSM_CU, SSM_CW), mat(SSM_CW, SSM_CU), mat(SSM_CW, SSM_CU), chunk(1, SSM_CU)],
        out_specs=[chunk(L, SSM_CU), mat(SSM_CU, SSM_CW), mat(SSM_CU, SSM_CW), mat(SSM_CW, SSM_CU), mat(SSM_CW, SSM_CU),
                   chunk(1, SSM_CW), chunk(1, SSM_CW), chunk(1, SSM_CU)],
        out_shape=[jax.ShapeDtypeStruct((L, SSM_WIDTH), F32),
                   jax.ShapeDtypeStruct((SSM_CHUNKS, SSM_CU, SSM_CW), F32), jax.ShapeDtypeStruct((SSM_CHUNKS, SSM_CU, SSM_CW), F32),
                   jax.ShapeDtypeStruct((SSM_CHUNKS, SSM_CW, SSM_CU), F32), jax.ShapeDtypeStruct((SSM_CHUNKS, SSM_CW, SSM_CU), F32),
                   jax.ShapeDtypeStruct((1, SSM_NSTATE), F32), jax.ShapeDtypeStruct((1, SSM_NSTATE), F32),
                   jax.ShapeDtypeStruct((1, SSM_WIDTH), F32)],
        scratch_shapes=[big(), big(), big(), big(), pltpu.SemaphoreType.DMA((2,))],
        compiler_params=_params("arbitrary", vmem=VMEM_BIG))(
            dy, u, s_re_all, s_im_all, a_re, a_im, bb_re, bb_im, cm_re, cm_im, d_skip)


def _place():
    return lax.axis_index("x"), lax.axis_index("y"), lax.axis_index("c")


def _all_gather_call(blocks, name, direct=False):
    n = len(blocks)

    def body(*refs):
        if direct:
            start, finish = _exchange_phases(refs[:n], refs[n:2 * n], *refs[2 * n:], same_source=True)
            start()
        else:
            start, forward, finish = _gather_phases(refs[:n], refs[n:2 * n], *refs[2 * n:])
            start()
            forward()
        finish()

    return pl.pallas_call(
        body, name=name, in_specs=[ANY_SPEC] * n, out_specs=[ANY_SPEC] * n,
        out_shape=[jax.ShapeDtypeStruct((N_DEV,) + b.shape, b.dtype) for b in blocks],
        scratch_shapes=_comm_sems(n))(*blocks)


def _comm_sems(n):
    return [pltpu.SemaphoreType.DMA((7 * n,)), pltpu.SemaphoreType.DMA((7 * n,)), pltpu.SemaphoreType.DMA((n,))]


def _gather_phases(x_refs, out_refs, send_sems, recv_sems, local_sems):
    x, y, c = _place()
    me, sibling = (x, y, c), (x, y, 1 - c)
    chips = [(1 - x, y), (x, 1 - y), (1 - x, 1 - y)]
    n = len(x_refs)

    def copy(k, a, blk, to, from_input=False):
        slot = out_refs[a].at[4 * blk[0] + 2 * blk[1] + blk[2]]
        return pltpu.make_async_remote_copy(
            src_ref=x_refs[a] if from_input else slot, dst_ref=slot,
            send_sem=send_sems.at[k * n + a], recv_sem=recv_sems.at[k * n + a], device_id=to, device_id_type=MESH_ID)

    mine = [pltpu.make_async_copy(x_refs[a], out_refs[a].at[4 * x + 2 * y + c], local_sems.at[a]) for a in range(n)]
    first, passed = [], []
    for a in range(n):
        first.append(copy(0, a, me, sibling, True))
        first += [copy(1 + j, a, me, (*chip, c), True) for j, chip in enumerate(chips)]
        passed += [copy(4 + j, a, (*chip, c), sibling) for j, chip in enumerate(chips)]

    def start():
        for cp in mine + first:
            cp.start()

    def forward():
        for j, chip in enumerate(chips):
            for a in range(n):
                copy(1 + j, a, (*chip, c), me).wait_recv()
                passed[3 * a + j].start()

    def finish():
        for a in range(n):
            copy(0, a, sibling, me).wait_recv()
            for j, chip in enumerate(chips):
                copy(4 + j, a, (*chip, 1 - c), me).wait_recv()
        for cp in first + passed:
            cp.wait_send()
        for cp in mine:
            cp.wait()

    return start, forward, finish


def _exchange_phases(p_refs, out_refs, send_sems, recv_sems, local_sems, same_source=False):
    x, y, c = _place()
    me = 4 * x + 2 * y + c
    n = len(p_refs)

    def flip(k):
        px = 1 - x if k & 4 else x
        py = 1 - y if k & 2 else y
        pc = 1 - c if k & 1 else c
        return (px, py, pc), 4 * px + 2 * py + pc

    def source(a, slot):
        return p_refs[a] if same_source else p_refs[a].at[slot]

    def copy(k, a, landing):
        peer, peer_slot = flip(k)
        return pltpu.make_async_remote_copy(
            src_ref=source(a, peer_slot), dst_ref=out_refs[a].at[peer_slot if landing else me],
            send_sem=send_sems.at[(k - 1) * n + a], recv_sem=recv_sems.at[(k - 1) * n + a],
            device_id=peer, device_id_type=MESH_ID)

    mine = [pltpu.make_async_copy(source(a, me), out_refs[a].at[me], local_sems.at[a]) for a in range(n)]
    sends = [copy(k, a, False) for k in range(1, N_DEV) for a in range(n)]

    def start():
        for cp in mine + sends:
            cp.start()

    def finish():
        for k in range(1, N_DEV):
            for a in range(n):
                copy(k, a, True).wait_recv()
        for cp in sends:
            cp.wait_send()
        for cp in mine:
            cp.wait()

    return start, finish


def _adam_math(g, w, m, v):
    c1 = 1.0 / (1.0 - ADAM_B1 ** ADAM_STEP)
    c2 = 1.0 / (1.0 - ADAM_B2 ** ADAM_STEP)
    m_new = ADAM_B1 * m + (1.0 - ADAM_B1) * g
    v_new = ADAM_B2 * v + (1.0 - ADAM_B2) * (g * g)
    delta = -ADAM_LR * ((m_new * c1) / (jnp.sqrt(v_new * c2) + ADAM_EPS) + ADAM_WD * w)
    return g, delta, m_new, v_new


def _sum_slices(s_ref):
    g = s_ref[0].astype(F32)
    for k in range(1, N_DEV):
        g = g + s_ref[k].astype(F32)
    return g


def _adam_call(slices, w, m, v, name):
    rest = w.shape[2:]
    t1 = _fit(w.shape[1], 256, 16) if len(rest) == 1 else _fit(w.shape[1], 8, 8)
    zeros = (0,) * len(rest)

    def body(s_ref, w_ref, m_ref, v_ref, g_ref, d_ref, mo_ref, vo_ref):
        g_ref[...], d_ref[...], mo_ref[...], vo_ref[...] = _adam_math(_sum_slices(s_ref), w_ref[...], m_ref[...], v_ref[...])

    own = pl.BlockSpec((1, t1) + rest, lambda i: (0, i) + zeros)
    out = jax.ShapeDtypeStruct(w.shape, F32)
    return pl.pallas_call(
        body, name=name, grid=(w.shape[1] // t1,),
        in_specs=[pl.BlockSpec((N_DEV, 1, t1) + rest, lambda i: (0, 0, i) + zeros), own, own, own],
        out_specs=[own, own, own, own], out_shape=[out, out, out, out],
        compiler_params=_params("parallel"))(slices, w, m, v)


def _adam_small_call(rows_all, row_params, slices, params):
    nr, n = len(row_params), len(row_params) + len(params)

    def body(rows_ref, *refs):
        slice_refs, wmv, outs = refs[:n - nr], refs[n - nr:n - nr + 3 * n], refs[n - nr + 3 * n:]
        for a in range(n):
            w_ref, m_ref, v_ref = wmv[3 * a:3 * a + 3]
            if a < nr:
                width = w_ref.shape[1]
                g = rows_ref[0, pl.ds(a, 1), pl.ds(0, width)]
                for k in range(1, N_DEV):
                    g = g + rows_ref[k, pl.ds(a, 1), pl.ds(0, width)]
            else:
                g = _sum_slices(slice_refs[a - nr])
            res = _adam_math(g, w_ref[...], m_ref[...], v_ref[...])
            for r in range(4):
                outs[4 * a + r][...] = res[r]

    every = list(row_params) + list(params)
    flat = pl.pallas_call(
        body, name="adam_small",
        out_shape=[jax.ShapeDtypeStruct(w.shape, F32) for w, _, _ in every for _ in range(4)],
    )(rows_all, *slices, *[t for wmv in every for t in wmv])
    return [flat[4 * a:4 * a + 4] for a in range(n)]


BIG = (("w_in", 1024, 404, 1), ("w_uq", 384, 96, 1), ("w_uk", 256, 64, 1), ("w_uv", 256, 64, 1),
       ("w_glu", 64, 512, 0), ("w_branch_attn", 512, 128, 1), ("w_branch_ssm", 512, 128, 1),
       ("w_out", 128, 1024, 0), ("w_up", 1024, 704, 1), ("w_down", 352, 1024, 0), ("conv_w", 3, 704, 1))
BIG_MIX, BIG_FFN = BIG[:8], BIG[8:]
GRADS_EARLY, GRADS_LATE = BIG[8:] + BIG[4:8], BIG[:4]
SMALL = (("mix_norm_pre", (1024,)), ("q_norm", (384,)), ("kv_norm", (256,)), ("ssm_lambda_re", (32, 64)),
         ("ssm_lambda_im", (32, 64)), ("ssm_log_dt", (32,)), ("ssm_b_re", (32, 64, 16)), ("ssm_b_im", (32, 64, 16)),
         ("ssm_c_re", (32, 16, 64)), ("ssm_c_im", (32, 16, 64)), ("ssm_d", (32, 16)), ("b_glu", (512,)),
         ("b_gate", (2048,)), ("mix_norm_post", (1024,)), ("ffn_norm_pre", (1024,)), ("conv_b", (5632,)),
         ("ffn_norm_post", (1024,)))


def _to_slices(full, rows, cols, axis):
    if axis == 1:
        return full.reshape(rows, N_DEV, cols).transpose(1, 0, 2)
    return full.reshape(N_DEV, rows, cols)


def _from_slices(parts, rows, cols, axis):
    if axis == 1:
        return parts.transpose(1, 0, 2).reshape(rows, N_DEV * cols)
    return parts.reshape(N_DEV * rows, cols)


def _head_pad_cols(w, width):
    k = w.shape[0]
    return jnp.pad(w.reshape(k, N_HEADS, width), ((0, 0), (0, 0), (0, LANES - width))).reshape(k, HEAD_PAD)


def _head_unpad_cols(w, width):
    k = w.shape[0]
    return w.reshape(k, N_HEADS, LANES)[:, :, :width].reshape(k, N_HEADS * width)


def _time_perm(a, L):
    return a.reshape(8, L // 8, a.shape[-1]).transpose(1, 0, 2).reshape(L, a.shape[-1])


def _time_unperm(a, L):
    return a.reshape(L // 8, 8, a.shape[-1]).transpose(1, 0, 2).reshape(L, a.shape[-1])


def _block_diag(w, rows_first):
    eye = jnp.eye(8, dtype=w.dtype)
    g = w.reshape(SSM_CHUNKS, 8, w.shape[1], w.shape[2])
    return jnp.einsum("qgrc,gk->qgrkc", g, eye).reshape(SSM_CHUNKS, 8 * w.shape[1], 8 * w.shape[2])


def _block_diag_t(m, r, c):
    eye = jnp.eye(8, dtype=m.dtype)
    return jnp.einsum("qgrkc,gk->qgrc", m.reshape(SSM_CHUNKS, 8, r, 8, c), eye).reshape(SSM_GROUPS, r, c)


def kernel(x, positions, mix_norm_pre, w_in, q_norm, w_uq, kv_norm, w_uk, w_uv, ssm_lambda_re, ssm_lambda_im, ssm_log_dt, ssm_b_re, ssm_b_im, ssm_c_re, ssm_c_im, ssm_d, w_glu, b_glu, w_branch_attn, w_branch_ssm, b_gate, w_out, mix_norm_post, ffn_norm_pre, w_up, conv_w, conv_b, w_down, ffn_norm_post, loss_target, m_mix_norm_pre, m_w_in, m_q_norm, m_w_uq, m_kv_norm, m_w_uk, m_w_uv, m_ssm_lambda_re, m_ssm_lambda_im, m_ssm_log_dt, m_ssm_b_re, m_ssm_b_im, m_ssm_c_re, m_ssm_c_im, m_ssm_d, m_w_glu, m_b_glu, m_w_branch_attn, m_w_branch_ssm, m_b_gate, m_w_out, m_mix_norm_post, m_ffn_norm_pre, m_w_up, m_conv_w, m_conv_b, m_w_down, m_ffn_norm_post, v_mix_norm_pre, v_w_in, v_q_norm, v_w_uq, v_kv_norm, v_w_uk, v_w_uv, v_ssm_lambda_re, v_ssm_lambda_im, v_ssm_log_dt, v_ssm_b_re, v_ssm_b_im, v_ssm_c_re, v_ssm_c_im, v_ssm_d, v_w_glu, v_b_glu, v_w_branch_attn, v_w_branch_ssm, v_b_gate, v_w_out, v_mix_norm_post, v_ffn_norm_pre, v_w_up, v_conv_w, v_conv_b, v_w_down, v_ffn_norm_post):
    given = dict(locals())
    L = x.shape[1]
    xs = x[0]
    target = loss_target[0]

    def shard_bits(group):
        return [given[name][0] if name == "conv_w" else given[name][0].astype(BF16) for name, _, _, _ in group]

    W = {}

    def unpack_weights(gathered, group):
        for (name, rows, cols, axis), parts in zip(group, gathered):
            W[name] = _from_slices(parts, rows, cols, axis)

    unpack_weights(_all_gather_call(shard_bits(BIG_MIX), "gather_weights"), BIG_MIX)

    wi = W["w_in"]
    kr_cols = jnp.pad(wi[:, 640:672], ((0, 0), (QK_NOPE, LANES - QK_HEAD)))
    w_in_p = jnp.concatenate([wi[:, :640], kr_cols, wi[:, 672:]], axis=1)
    w_uq_p = _head_pad_cols(W["w_uq"], QK_HEAD)
    w_kv_p = jnp.stack([_head_pad_cols(W["w_uk"], QK_NOPE).reshape(KV_RANK, N_HEADS, LANES),
                        _head_pad_cols(W["w_uv"], V_HEAD).reshape(KV_RANK, N_HEADS, LANES)], axis=2
                       ).reshape(KV_RANK, 2 * HEAD_PAD)
    w_ba_p = jnp.pad(W["w_branch_attn"].reshape(N_HEADS, V_HEAD, D_MODEL), ((0, 0), (0, LANES - V_HEAD), (0, 0))
                     ).reshape(HEAD_PAD, D_MODEL)

    hn1 = _rms_fwd_call(xs, mix_norm_pre, "rms_pre")
    proj = _mm(hn1, w_in_p, "mm_in", tn=256)
    qn, ckvn = _mla_norms_call(proj, q_norm, kv_norm)
    q_pad = _mm(qn, w_uq_p, "mm_uq")
    kv_pad = _mm(ckvn, w_kv_p, "mm_ukv")
    half = jnp.arange(QK_ROPE // 2, dtype=F32)
    inv_freq = ROPE_THETA ** (-2.0 * half / QK_ROPE)
    inv_freq = jnp.pad(jnp.concatenate([inv_freq, inv_freq]), (QK_NOPE, LANES - QK_HEAD)).reshape(1, LANES)
    pos_col = positions.astype(F32).reshape(L, 1)
    q_r, kv_r, cosf, sinf = _mla_prep_call(q_pad, kv_pad, proj, pos_col, inv_freq)
    attn, lse, *gathered_ffn = _attn_fwd_call(q_r, kv_r, shard_bits(BIG_FFN))
    unpack_weights(gathered_ffn, BIG_FFN)

    col = lambda a: a.reshape(SSM_NSTATE, -1)
    lr_c, li_c = col(ssm_lambda_re[0]), col(ssm_lambda_im[0])
    ldt_c = col(jnp.broadcast_to(ssm_log_dt[0][:, None], (SSM_GROUPS, SSM_STATE)))
    br_c, bi_c = col(ssm_b_re[0]), col(ssm_b_im[0])
    a_re_c, a_im_c, bb_re_c, bb_im_c = _disc_call(lr_c, li_c, ldt_c, br_c, bi_c)
    a_re, a_im = a_re_c.reshape(1, SSM_NSTATE), a_im_c.reshape(1, SSM_NSTATE)
    to_bb = lambda b: _block_diag(b.reshape(SSM_GROUPS, SSM_STATE, SSM_GROUP).transpose(0, 2, 1), True).astype(BF16)
    bb_re, bb_im = to_bb(bb_re_c), to_bb(bb_im_c)
    to_cm = lambda c_: _block_diag(c_[0].transpose(0, 2, 1), True).astype(BF16)
    cm_re, cm_im = to_cm(ssm_c_re), to_cm(ssm_c_im)
    d_skip = ssm_d.reshape(1, SSM_WIDTH)
    u_p = _time_perm(proj[:, P_U:P_GATE], L)
    y1, s_re, s_im = _ssm_fwd_call(u_p, a_re, a_im, bb_re, bb_im, cm_re, cm_im, d_skip)
    w_glu_b = W["w_glu"]
    ssm_p = _glu_call(y1, w_glu_b, b_glu)
    ssm = _time_unperm(ssm_p, L)

    pa = _mm(attn, w_ba_p, "mm_ba")
    ps = _mm(ssm, W["w_branch_ssm"], "mm_bs")
    merged = _merge_call(proj, b_gate, pa, ps)
    o = _mm(merged, W["w_out"], "mm_out")
    x2, hn2 = _post_mix_call(o, xs, mix_norm_post, ffn_norm_pre)
    h = _mm(hn2, W["w_up"], "mm_up")
    cw = W["conv_w"]
    act = _conv_act_call(h, cw, conv_b)
    ff = _mm(act, W["w_down"], "mm_down", tk=1408)
    loss_row, dy, dff, g_ffn_norm_post = _ffn_out_call(ff, x2, target, ffn_norm_post)
    loss = lax.psum(loss_row[0, 0], ("x", "y", "c"))

    da = _mm(dff, W["w_down"], "mm_down_dx", tb=True, tn=256)
    g_w_down = _mm(act, dff, "mm_down_dw", ta=True, out_dtype=BF16, tm=256, tn=1024)
    dgate, dval, dcw_g, dcw_v, dcb_g, dcb_v = _conv_act_bwd_call(da, h, cw, conv_b)
    g_conv_w = jnp.concatenate([dcw_g, dcw_v], axis=1)
    g_conv_b = jnp.concatenate([dcb_g, dcb_v], axis=1)
    dh_g, dh_v = _conv_t_call(dgate, dval, cw)
    dh = jnp.concatenate([dh_g, dh_v], axis=1)
    dhn2 = _mm(dh, W["w_up"], "mm_up_dx", tb=True, tk=1408)
    g_w_up = _mm(hn2, dh, "mm_up_dw", ta=True, out_dtype=BF16, tm=512)
    dx2, do, g_ffn_norm_pre, g_mix_norm_post = _post_bwd_call(x2, dhn2, dy, o, ffn_norm_pre, mix_norm_post)
    dmerged = _mm(do, W["w_out"], "mm_out_dx", tb=True)
    g_w_out = _mm(merged, do, "mm_out_dw", ta=True, out_dtype=BF16, tm=512, tn=1024)
    dpa, dps, dl0, dl1, db0, db1 = _merge_bwd_call(dmerged, proj, b_gate, pa, ps)
    g_b_gate = jnp.concatenate([db0, db1], axis=1)
    dattn = _mm(dpa, w_ba_p, "mm_ba_dx", tb=True, out_dtype=BF16)
    g_w_ba = _mm(attn, dpa, "mm_ba_dw", ta=True, out_dtype=BF16, tm=512, tn=1024).reshape(N_HEADS, LANES, D_MODEL)[:, :V_HEAD].reshape(N_HEADS * V_HEAD, D_MODEL)
    dssm = _mm(dps, W["w_branch_ssm"], "mm_bs_dx", tb=True)
    g_w_bs = _mm(ssm, dps, "mm_bs_dw", ta=True, out_dtype=BF16, tm=512, tn=1024)

    dy1, g_w_glu, g_b_glu = _glu_bwd_call(_time_perm(dssm, L), y1, w_glu_b, b_glu)
    du_p, dbb_re, dbb_im, dcm_re, dcm_im, da_re, da_im, g_ssm_d = _ssm_bwd_call(
        dy1, u_p, s_re, s_im, a_re, a_im, bb_re, bb_im, cm_re, cm_im, d_skip)
    du = _time_unperm(du_p, L)
    from_bb = lambda m: col(_block_diag_t(m, SSM_GROUP, SSM_STATE).transpose(0, 2, 1))
    dlr, dli, dldt, dbr, dbi = _disc_bwd_call(
        lr_c, li_c, ldt_c, br_c, bi_c, da_re.reshape(SSM_NSTATE, 1), da_im.reshape(SSM_NSTATE, 1), from_bb(dbb_re), from_bb(dbb_im))
    g_c_re = _block_diag_t(dcm_re, SSM_STATE, SSM_GROUP).transpose(0, 2, 1)
    g_c_im = _block_diag_t(dcm_im, SSM_STATE, SSM_GROUP).transpose(0, 2, 1)

    def grad_slices(group, grads):
        return [_to_slices(grads[name], rows, cols, axis) for name, rows, cols, axis in group]

    early_grads = {"w_up": g_w_up, "w_down": g_w_down, "conv_w": g_conv_w, "w_glu": g_w_glu.astype(BF16),
                   "w_branch_attn": g_w_ba, "w_branch_ssm": g_w_bs, "w_out": g_w_out}
    ssm_partials = {"ssm_lambda_re": dlr, "ssm_lambda_im": dli, "ssm_b_re": dbr, "ssm_b_im": dbi,
                    "ssm_c_re": g_c_re, "ssm_c_im": g_c_im, "ssm_d": g_ssm_d}
    ssm_shapes = [(name, shp) for name, shp in SMALL if name in ssm_partials]
    dq, dkv, *landed = _attn_bwd_call(
        q_r, kv_r, attn, dattn, lse, grad_slices(GRADS_EARLY, early_grads),
        [ssm_partials[name].reshape((1,) + shp) for name, shp in ssm_shapes])
    received_early, ssm_all = landed[:len(GRADS_EARLY)], dict(zip([n for n, _ in ssm_shapes], landed[len(GRADS_EARLY):]))
    dq_p, dkv_p, dkr_p = _mla_prep_bwd_call(dq, dkv, cosf, sinf)
    dqn = _mm(dq_p, w_uq_p, "mm_uq_dx", tb=True)
    g_w_uq = _head_unpad_cols(_mm(qn, dq_p, "mm_uq_dw", ta=True, out_dtype=BF16, tn=1024), QK_HEAD)
    dckvn = _mm(dkv_p, w_kv_p, "mm_ukv_dx", tb=True)
    g_w_kv = _mm(ckvn, dkv_p, "mm_ukv_dw", ta=True, out_dtype=BF16, tn=1024).reshape(KV_RANK, N_HEADS, 2, LANES)
    g_w_uk = g_w_kv[:, :, 0, :QK_NOPE].reshape(KV_RANK, N_HEADS * QK_NOPE)
    g_w_uv = g_w_kv[:, :, 1, :V_HEAD].reshape(KV_RANK, N_HEADS * V_HEAD)
    dcqkv, g_q_norm, g_kv_norm = _mla_norms_bwd_call(proj, dqn, dckvn, q_norm, kv_norm)
    dproj = jnp.concatenate([dcqkv, dkr_p, du.astype(BF16), dl0, dl1], axis=1)
    g_w_in_p = _mm(hn1, dproj, "mm_in_dw", ta=True, out_dtype=BF16, tm=512, tn=256)
    g_w_in = jnp.concatenate([g_w_in_p[:, :640], g_w_in_p[:, 640 + QK_NOPE:640 + QK_HEAD], g_w_in_p[:, 768:]], axis=1)
    late_grads = {"w_in": g_w_in, "w_uq": g_w_uq, "w_uk": g_w_uk, "w_uv": g_w_uv}
    dhn1, *received_late = _mm(dproj, w_in_p, "mm_in_dx", tb=True, tk=1664, exchange=grad_slices(GRADS_LATE, late_grads))
    grad_x, g_mix_norm_pre = _pre_bwd_call(xs, dhn1, dx2, mix_norm_pre)

    results = {}
    for group, received in ((GRADS_EARLY, received_early), (GRADS_LATE, received_late)):
        for (name, _, _, _), rec in zip(group, received):
            results[name] = _adam_call(rec[:, None], given[name], given["m_" + name], given["v_" + name], "adam_" + name)

    vec_grads = {"mix_norm_pre": g_mix_norm_pre, "q_norm": g_q_norm, "kv_norm": g_kv_norm,
                 "ssm_log_dt": jnp.sum(dldt.reshape(SSM_GROUPS, SSM_STATE), axis=1),
                 "b_glu": g_b_glu, "b_gate": g_b_gate, "mix_norm_post": g_mix_norm_post,
                 "ffn_norm_pre": g_ffn_norm_pre, "conv_b": g_conv_b, "ffn_norm_post": g_ffn_norm_post}
    vec_names = [name for name, _ in SMALL if name in vec_grads]
    width = max(shp[0] for name, shp in SMALL if name in vec_grads)
    rows = [jnp.pad(vec_grads[name].reshape(1, -1), ((0, 0), (0, width - vec_grads[name].size))) for name in vec_names]
    rows.append(jnp.zeros((-len(rows) % 8, width), F32))
    rows_all, = _all_gather_call([jnp.concatenate(rows, axis=0)], "gather_small_grads", direct=True)
    wmv = lambda name: (given[name], given["m_" + name], given["v_" + name])
    few = ["ssm_lambda_re", "ssm_lambda_im", "ssm_d"]
    for name, res in zip(vec_names + few, _adam_small_call(
            rows_all, [wmv(n) for n in vec_names], [ssm_all[n] for n in few], [wmv(n) for n in few])):
        results[name] = res
    for name in ("ssm_b_re", "ssm_b_im", "ssm_c_re", "ssm_c_im"):
        results[name] = _adam_call(ssm_all[name], *wmv(name), "adam_" + name)

    order = ["mix_norm_pre", "w_in", "q_norm", "w_uq", "kv_norm", "w_uk", "w_uv", "ssm_lambda_re", "ssm_lambda_im",
             "ssm_log_dt", "ssm_b_re", "ssm_b_im", "ssm_c_re", "ssm_c_im", "ssm_d", "w_glu", "b_glu", "w_branch_attn",
             "w_branch_ssm", "b_gate", "w_out", "mix_norm_post", "ffn_norm_pre", "w_up", "conv_w", "conv_b", "w_down",
             "ffn_norm_post"]
    outs = [loss, grad_x[None]]
    for kind in range(4):
        outs += [results[name][kind] for name in order]
    return tuple(outs)
```

```python
import math

import jax
import jax.numpy as jnp
from jax import lax
from jax.experimental import pallas as pl
from jax.experimental.pallas import tpu as pltpu

F32 = jnp.float32
BF16 = jnp.bfloat16
MESH_ID = pl.DeviceIdType.MESH

N_DEV = 8
LANES = 128
D_MODEL = 1024
N_HEADS = 8
QK_NOPE = 64
QK_ROPE = 32
QK_HEAD = QK_NOPE + QK_ROPE
V_HEAD = 64
Q_RANK = 384
KV_RANK = 256
ROPE_THETA = 10000.0
SSM_WIDTH = 512
SSM_GROUP = 16
SSM_GROUPS = 32
SSM_STATE = 64
SSM_NSTATE = SSM_GROUPS * SSM_STATE
SSM_CHUNKS = 4
D_FF = 2816
EPS = 1e-6
ADAM_LR, ADAM_B1, ADAM_B2, ADAM_EPS, ADAM_WD, ADAM_STEP = 0.001, 0.9, 0.999, 1e-08, 0.01, 10

P_CQ, P_CKV, P_KR, P_U, P_GATE = 0, 384, 640, 768, 1280
P_IN = P_GATE + 2 * D_MODEL
HEAD_PAD = N_HEADS * LANES

PACK_ROWS = 1024
VMEM_BIG = 52 * 1024 * 1024

_GELU_C0 = math.sqrt(2.0 / math.pi)
_GELU_C1 = 0.044715
NEG = -1e30


def _fit(n, pref, mult=LANES):
    if n <= pref:
        return n
    t = (pref // mult) * mult
    while t > 0 and n % t:
        t -= mult
    assert t > 0, (n, pref, mult)
    return t


def _gelu(x):
    return 0.5 * x * (1.0 + jnp.tanh(_GELU_C0 * (x + _GELU_C1 * x * x * x)))


def _gelu_grad(x):
    x2 = x * x
    t = jnp.tanh(_GELU_C0 * x * (1.0 + _GELU_C1 * x2))
    return 0.5 * (1.0 + t) + 0.5 * x * (1.0 - t * t) * _GELU_C0 * (1.0 + 3.0 * _GELU_C1 * x2)


def _sigmoid(x):
    return 1.0 / (1.0 + jnp.exp(-x))


def _dot(a, b, dims):
    return lax.dot_general(a, b, (dims, ((), ())), preferred_element_type=F32)


NN = ((1,), (0,))
NT = ((1,), (1,))
TN = ((0,), (0,))


def _params(*sem, vmem=None):
    return pltpu.CompilerParams(dimension_semantics=tuple(sem), vmem_limit_bytes=vmem)


def _mm(a, b, name, tb=False, out_dtype=F32, tm=1024, tn=512, tk=1024, exchange=()):
    M, K = a.shape
    if tb:
        N, K2 = b.shape
    else:
        K2, N = b.shape
    assert K == K2, (a.shape, b.shape, tb)
    tm, tn, tk = _fit(M, tm), _fit(N, tn), _fit(K, tk)
    nk = K // tk
    grid = (M // tm, N // tn, nk)
    dims = NT if tb else NN
    n = len(exchange)

    def body(a_ref, b_ref, *refs):
        o_ref, scratch = refs[n], refs[2 * n + 1:]
        step = (pl.program_id(0) * grid[1] + pl.program_id(1)) * grid[2] + pl.program_id(2)
        if n:
            start, finish = _exchange_phases(refs[:n], refs[n + 1:2 * n + 1], *scratch[-3:])
            pl.when(step == 0)(start)
        part = _dot(a_ref[...].astype(BF16), b_ref[...].astype(BF16), dims)
        if nk == 1:
            o_ref[...] = part.astype(out_dtype)
        else:
            acc_ref = scratch[0]
            k = pl.program_id(2)

            @pl.when(k == 0)
            def _():
                acc_ref[...] = part

            @pl.when(k > 0)
            def _():
                acc_ref[...] += part

            @pl.when(k == nk - 1)
            def _():
                o_ref[...] = acc_ref[...].astype(out_dtype)
        if n:
            pl.when(step == grid[0] * grid[1] * grid[2] - 1)(finish)

    a_spec = pl.BlockSpec((tm, tk), lambda i, j, k: (i, k))
    b_spec = pl.BlockSpec((tn, tk), lambda i, j, k: (j, k)) if tb else pl.BlockSpec((tk, tn), lambda i, j, k: (k, j))
    out = pl.pallas_call(
        body, name=name, grid=grid,
        in_specs=[a_spec, b_spec] + [ANY_SPEC] * n,
        out_specs=[pl.BlockSpec((tm, tn), lambda i, j, k: (i, j))] + [ANY_SPEC] * n,
        out_shape=[jax.ShapeDtypeStruct((M, N), out_dtype)] + [jax.ShapeDtypeStruct(p.shape, p.dtype) for p in exchange],
        scratch_shapes=([] if nk == 1 else [pltpu.VMEM((tm, tn), F32)]) + (_comm_sems(n) if n else []),
        compiler_params=_params(*(("arbitrary",) * 3 if n else ("parallel", "parallel", "arbitrary")), vmem=VMEM_BIG),
    )(a, b, *exchange)
    return out if n else out[0]


TN_CHUNK = 512


def _mm_tn(a, b, name, tm=512, tk=512):
    K, M = a.shape
    K2, N = b.shape
    assert K == K2, (a.shape, b.shape)
    tm, tk, cn = _fit(M, tm), _fit(K, tk), _fit(N, TN_CHUNK)
    nk = K // tk

    def body(a_ref, b_ref, o_ref, acc_ref):
        k = pl.program_id(1)

        @pl.when(k == 0)
        def _():
            acc_ref[...] = jnp.zeros((tm, N), F32)

        at = a_ref[...].astype(BF16).T
        for c in range(N // cn):
            cols = slice(c * cn, (c + 1) * cn)
            acc_ref[:, cols] += _dot(at, b_ref[:, cols].astype(BF16), NN)

        @pl.when(k == nk - 1)
        def _():
            o_ref[...] = acc_ref[...].astype(BF16)

    return pl.pallas_call(
        body, name=name, grid=(M // tm, nk),
        in_specs=[pl.BlockSpec((tk, tm), lambda i, k: (k, i)), pl.BlockSpec((tk, N), lambda i, k: (k, 0))],
        out_specs=pl.BlockSpec((tm, N), lambda i, k: (i, 0)),
        out_shape=jax.ShapeDtypeStruct((M, N), BF16),
        scratch_shapes=[pltpu.VMEM((tm, N), F32)],
        compiler_params=_params("parallel", "arbitrary", vmem=VMEM_BIG))(a, b)


def _row(tl, n, col=0):
    return pl.BlockSpec((tl, n), lambda i: (i, col))


def _full(shape):
    return pl.BlockSpec(shape, lambda i: (0,) * len(shape))


def _rms(x, g):
    r = lax.rsqrt(jnp.mean(x * x, axis=-1, keepdims=True) + EPS)
    return x * r * g


def _rms_bwd(x, g, dy):
    n = x.shape[-1]
    r = lax.rsqrt(jnp.mean(x * x, axis=-1, keepdims=True) + EPS)
    gy = dy * g
    dx = r * gy - x * (r * r * r * (1.0 / n)) * jnp.sum(x * gy, axis=-1, keepdims=True)
    return dx, jnp.sum(dy * x * r, axis=0, keepdims=True)


def _acc(ref, first, val):
    @pl.when(first)
    def _():
        ref[...] = val

    @pl.when(jnp.logical_not(first))
    def _():
        ref[...] += val


def _rms_fwd_call(x, g, name):
    L, n = x.shape
    tl = _fit(L, 512)

    def body(x_ref, g_ref, o_ref):
        o_ref[...] = _rms(x_ref[...], g_ref[...]).astype(BF16)

    return pl.pallas_call(
        body, name=name, grid=(L // tl,), in_specs=[_row(tl, n), _full((1, n))], out_specs=_row(tl, n),
        out_shape=jax.ShapeDtypeStruct((L, n), BF16), compiler_params=_params("parallel"))(x, g)


def _mla_norms_call(proj, q_norm, kv_norm):
    L = proj.shape[0]
    tl = _fit(L, 512)

    def body(p_ref, gq_ref, gk_ref, qn_ref, kn_ref):
        p = p_ref[...]
        qn_ref[...] = _rms(p[:, P_CQ:P_CKV], gq_ref[...]).astype(BF16)
        kn_ref[...] = _rms(p[:, P_CKV:P_KR], gk_ref[...]).astype(BF16)

    return pl.pallas_call(
        body, name="mla_norms", grid=(L // tl,),
        in_specs=[_row(tl, P_KR), _full((1, Q_RANK)), _full((1, KV_RANK))],
        out_specs=[_row(tl, Q_RANK), _row(tl, KV_RANK)],
        out_shape=[jax.ShapeDtypeStruct((L, Q_RANK), BF16), jax.ShapeDtypeStruct((L, KV_RANK), BF16)],
        compiler_params=_params("parallel"))(proj, q_norm, kv_norm)


def _rope_lanes(shape):
    lane = lax.broadcasted_iota(jnp.int32, shape, 1)
    return lane, jnp.logical_and(lane >= QK_NOPE, lane < QK_HEAD)


def _rope_apply(x, cosf, sinf, lane):
    rot = jnp.where(lane < QK_NOPE + QK_ROPE // 2, -pltpu.roll(x, LANES - QK_ROPE // 2, 1), pltpu.roll(x, QK_ROPE // 2, 1))
    return x * cosf + rot * sinf


def _rope_apply_t(dy, cosf, sinf, lane, is_rope):
    g = dy * sinf
    rot_t = jnp.where(lane < QK_NOPE + QK_ROPE // 2, pltpu.roll(g, LANES - QK_ROPE // 2, 1), -pltpu.roll(g, QK_ROPE // 2, 1))
    return dy * cosf + jnp.where(is_rope, rot_t, 0.0)


def _mla_prep_call(q_pad, kv_pad, proj, pos_col, inv_freq):
    L = q_pad.shape[0]
    tl = _fit(L, 512)

    def body(q_ref, kv_ref, kr_ref, pos_ref, f_ref, qo_ref, kvo_ref, cos_ref, sin_ref):
        lane, is_rope = _rope_lanes((tl, LANES))
        ang = pos_ref[...] * f_ref[...]
        cosf = jnp.where(is_rope, jnp.cos(ang), jnp.where(lane < QK_NOPE, 1.0, 0.0))
        sinf = jnp.where(is_rope, jnp.sin(ang), 0.0)
        cos_ref[...] = cosf
        sin_ref[...] = sinf
        kr = _rope_apply(kr_ref[...], cosf, sinf, lane)
        for h in range(N_HEADS):
            qh = _rope_apply(q_ref[:, h * LANES:(h + 1) * LANES], cosf, sinf, lane)
            qo_ref[:, h * LANES:(h + 1) * LANES] = (qh * Q_PRESCALE).astype(BF16)
            kvo_ref[:, 2 * h * LANES:(2 * h + 1) * LANES] = (kv_ref[:, 2 * h * LANES:(2 * h + 1) * LANES] + kr).astype(BF16)
            vh = jnp.where(lane == V_HEAD, 1.0, kv_ref[:, (2 * h + 1) * LANES:(2 * h + 2) * LANES])
            kvo_ref[:, (2 * h + 1) * LANES:(2 * h + 2) * LANES] = vh.astype(BF16)

    return pl.pallas_call(
        body, name="mla_prep", grid=(L // tl,),
        in_specs=[_row(tl, HEAD_PAD), _row(tl, 2 * HEAD_PAD), _row(tl, LANES, P_KR // LANES), _row(tl, 1), _full((1, LANES))],
        out_specs=[_row(tl, HEAD_PAD), _row(tl, 2 * HEAD_PAD), _row(tl, LANES), _row(tl, LANES)],
        out_shape=[jax.ShapeDtypeStruct((L, HEAD_PAD), BF16), jax.ShapeDtypeStruct((L, 2 * HEAD_PAD), BF16),
                   jax.ShapeDtypeStruct((L, LANES), F32), jax.ShapeDtypeStruct((L, LANES), F32)],
        compiler_params=_params("parallel"))(q_pad, kv_pad, proj, pos_col, inv_freq)


def _mla_prep_bwd_call(dq, dkv, cosf, sinf):
    L = dq.shape[0]
    tl = _fit(L, 512)

    def body(dq_ref, dkv_ref, cos_ref, sin_ref, dqo_ref, dkvo_ref, dkr_ref):
        lane, is_rope = _rope_lanes((tl, LANES))
        cosf, sinf = cos_ref[...], sin_ref[...]
        dk_sum = jnp.zeros((tl, LANES), F32)
        for h in range(N_HEADS):
            dqo_ref[:, h * LANES:(h + 1) * LANES] = _rope_apply_t(dq_ref[:, h * LANES:(h + 1) * LANES], cosf, sinf, lane, is_rope).astype(BF16)
            dk_sum = dk_sum + dkv_ref[:, 2 * h * LANES:(2 * h + 1) * LANES]
        dkvo_ref[...] = dkv_ref[...].astype(BF16)
        dkr_ref[...] = _rope_apply_t(dk_sum, cosf, sinf, lane, is_rope).astype(BF16)

    return pl.pallas_call(
        body, name="mla_prep_bwd", grid=(L // tl,),
        in_specs=[_row(tl, HEAD_PAD), _row(tl, 2 * HEAD_PAD), _row(tl, LANES), _row(tl, LANES)],
        out_specs=[_row(tl, HEAD_PAD), _row(tl, 2 * HEAD_PAD), _row(tl, LANES)],
        out_shape=[jax.ShapeDtypeStruct((L, HEAD_PAD), BF16), jax.ShapeDtypeStruct((L, 2 * HEAD_PAD), BF16),
                   jax.ShapeDtypeStruct((L, LANES), BF16)],
        compiler_params=_params("parallel"))(dq, dkv, cosf, sinf)


def _mla_norms_bwd_call(proj, dqn, dkn, q_norm, kv_norm):
    L = proj.shape[0]
    tl = _fit(L, 512)

    def body(p_ref, dqn_ref, dkn_ref, gq_ref, gk_ref, d_ref, dgq_ref, dgk_ref):
        first = pl.program_id(0) == 0
        p = p_ref[...]
        dq, dgq = _rms_bwd(p[:, P_CQ:P_CKV], gq_ref[...], dqn_ref[...])
        dk, dgk = _rms_bwd(p[:, P_CKV:P_KR], gk_ref[...], dkn_ref[...])
        d_ref[:, P_CQ:P_CKV] = dq.astype(BF16)
        d_ref[:, P_CKV:P_KR] = dk.astype(BF16)
        _acc(dgq_ref, first, dgq)
        _acc(dgk_ref, first, dgk)

    return pl.pallas_call(
        body, name="mla_norms_bwd", grid=(L // tl,),
        in_specs=[_row(tl, P_KR), _row(tl, Q_RANK), _row(tl, KV_RANK), _full((1, Q_RANK)), _full((1, KV_RANK))],
        out_specs=[_row(tl, P_KR), _full((1, Q_RANK)), _full((1, KV_RANK))],
        out_shape=[jax.ShapeDtypeStruct((L, P_KR), BF16), jax.ShapeDtypeStruct((1, Q_RANK), F32),
                   jax.ShapeDtypeStruct((1, KV_RANK), F32)],
        compiler_params=_params("arbitrary"))(proj, dqn, dkn, q_norm, kv_norm)


GATE_TILE = 256


def _merge_call(proj, b_gate, pa, ps):
    L = proj.shape[0]
    tl = _fit(L, 512)
    nc = D_MODEL // GATE_TILE
    g0, g1 = P_GATE // GATE_TILE, (P_GATE + D_MODEL) // GATE_TILE

    def body(l0_ref, l1_ref, b0_ref, b1_ref, pa_ref, ps_ref, o_ref):
        s0 = _sigmoid(l0_ref[...] + b0_ref[...])
        s1 = _sigmoid(l1_ref[...] + b1_ref[...])
        o_ref[...] = (s0 * pa_ref[...] + s1 * ps_ref[...]).astype(BF16)

    blk = lambda off: pl.BlockSpec((tl, GATE_TILE), lambda i, j: (i, off + j))
    bias = lambda off: pl.BlockSpec((1, GATE_TILE), lambda i, j: (0, off + j))
    return pl.pallas_call(
        body, name="merge", grid=(L // tl, nc),
        in_specs=[blk(g0), blk(g1), bias(0), bias(nc), blk(0), blk(0)],
        out_specs=blk(0), out_shape=jax.ShapeDtypeStruct((L, D_MODEL), BF16),
        compiler_params=_params("parallel", "parallel"))(proj, proj, b_gate, b_gate, pa, ps)


def _merge_bwd_call(dm, proj, b_gate, pa, ps):
    L = proj.shape[0]
    tl = _fit(L, 512)
    nc = D_MODEL // GATE_TILE
    g0, g1 = P_GATE // GATE_TILE, (P_GATE + D_MODEL) // GATE_TILE

    def body(dm_ref, l0_ref, l1_ref, b0_ref, b1_ref, pa_ref, ps_ref, dpa_ref, dps_ref, dl0_ref, dl1_ref, db0_ref, db1_ref):
        first = pl.program_id(1) == 0
        dm_ = dm_ref[...]
        s0 = _sigmoid(l0_ref[...] + b0_ref[...])
        s1 = _sigmoid(l1_ref[...] + b1_ref[...])
        dpa_ref[...] = (dm_ * s0).astype(BF16)
        dps_ref[...] = (dm_ * s1).astype(BF16)
        dl0 = dm_ * pa_ref[...] * s0 * (1.0 - s0)
        dl1 = dm_ * ps_ref[...] * s1 * (1.0 - s1)
        dl0_ref[...] = dl0.astype(BF16)
        dl1_ref[...] = dl1.astype(BF16)
        _acc(db0_ref, first, jnp.sum(dl0, axis=0, keepdims=True))
        _acc(db1_ref, first, jnp.sum(dl1, axis=0, keepdims=True))

    blk = lambda off: pl.BlockSpec((tl, GATE_TILE), lambda j, i: (i, off + j))
    bias = lambda off: pl.BlockSpec((1, GATE_TILE), lambda j, i: (0, off + j))
    act = jax.ShapeDtypeStruct((L, D_MODEL), BF16)
    vec = jax.ShapeDtypeStruct((1, D_MODEL), F32)
    return pl.pallas_call(
        body, name="merge_bwd", grid=(nc, L // tl),
        in_specs=[blk(0), blk(g0), blk(g1), bias(0), bias(nc), blk(0), blk(0)],
        out_specs=[blk(0), blk(0), blk(0), blk(0), bias(0), bias(0)],
        out_shape=[act, act, act, act, vec, vec],
        compiler_params=_params("parallel", "arbitrary"))(dm, proj, proj, b_gate, b_gate, pa, ps)


def _post_mix_call(o, x, g_post, g_fpre):
    L, n = x.shape
    tl = _fit(L, 512)

    def body(o_ref, x_ref, gp_ref, gf_ref, x2_ref, hn_ref):
        x2 = x_ref[...] + _rms(o_ref[...], gp_ref[...])
        x2_ref[...] = x2
        hn_ref[...] = _rms(x2, gf_ref[...]).astype(BF16)

    return pl.pallas_call(
        body, name="post_mix", grid=(L // tl,),
        in_specs=[_row(tl, n), _row(tl, n), _full((1, n)), _full((1, n))],
        out_specs=[_row(tl, n), _row(tl, n)],
        out_shape=[jax.ShapeDtypeStruct((L, n), F32), jax.ShapeDtypeStruct((L, n), BF16)],
        compiler_params=_params("parallel"))(o, x, g_post, g_fpre)


def _ffn_out_call(ff, x2, target, g_fpost):
    L, n = x2.shape
    tl = _fit(L, 512)

    def body(ff_ref, x2_ref, t_ref, g_ref, loss_ref, dy_ref, dff_ref, dg_ref):
        first = pl.program_id(0) == 0
        ff_ = ff_ref[...]
        err = x2_ref[...] + _rms(ff_, g_ref[...]) - t_ref[...]
        part = 0.5 * jnp.sum(jnp.sum(err * err, axis=-1, keepdims=True) * (1.0 / n), axis=0, keepdims=True)
        dy = err * (1.0 / n)
        dy_ref[...] = dy
        dff, dg = _rms_bwd(ff_, g_ref[...], dy)
        dff_ref[...] = dff.astype(BF16)
        _acc(loss_ref, first, jnp.broadcast_to(part, (1, LANES)))
        _acc(dg_ref, first, dg)

    return pl.pallas_call(
        body, name="ffn_out", grid=(L // tl,),
        in_specs=[_row(tl, n), _row(tl, n), _row(tl, n), _full((1, n))],
        out_specs=[_full((1, LANES)), _row(tl, n), _row(tl, n), _full((1, n))],
        out_shape=[jax.ShapeDtypeStruct((1, LANES), F32), jax.ShapeDtypeStruct((L, n), F32),
                   jax.ShapeDtypeStruct((L, n), BF16), jax.ShapeDtypeStruct((1, n), F32)],
        compiler_params=_params("arbitrary"))(ff, x2, target, g_fpost)


def _post_bwd_call(x2, dhn2, dy, o, g_fpre, g_post):
    L, n = x2.shape
    tl = _fit(L, 512)

    def body(x2_ref, dh_ref, dy_ref, o_ref, gf_ref, gp_ref, dx2_ref, do_ref, dgf_ref, dgp_ref):
        first = pl.program_id(0) == 0
        d1, dgf = _rms_bwd(x2_ref[...], gf_ref[...], dh_ref[...])
        dx2 = dy_ref[...] + d1
        dx2_ref[...] = dx2
        do, dgp = _rms_bwd(o_ref[...], gp_ref[...], dx2)
        do_ref[...] = do.astype(BF16)
        _acc(dgf_ref, first, dgf)
        _acc(dgp_ref, first, dgp)

    return pl.pallas_call(
        body, name="post_bwd", grid=(L // tl,),
        in_specs=[_row(tl, n), _row(tl, n), _row(tl, n), _row(tl, n), _full((1, n)), _full((1, n))],
        out_specs=[_row(tl, n), _row(tl, n), _full((1, n)), _full((1, n))],
        out_shape=[jax.ShapeDtypeStruct((L, n), F32), jax.ShapeDtypeStruct((L, n), BF16),
                   jax.ShapeDtypeStruct((1, n), F32), jax.ShapeDtypeStruct((1, n), F32)],
        compiler_params=_params("arbitrary"))(x2, dhn2, dy, o, g_fpre, g_post)


def _pre_bwd_call(x, dhn1, dx2, g_pre):
    L, n = x.shape
    tl = _fit(L, 512)

    def body(x_ref, dh_ref, dx2_ref, g_ref, dx_ref, dg_ref):
        first = pl.program_id(0) == 0
        d1, dg = _rms_bwd(x_ref[...], g_ref[...], dh_ref[...])
        dx_ref[...] = dx2_ref[...] + d1
        _acc(dg_ref, first, dg)

    return pl.pallas_call(
        body, name="pre_bwd", grid=(L // tl,),
        in_specs=[_row(tl, n), _row(tl, n), _row(tl, n), _full((1, n))],
        out_specs=[_row(tl, n), _full((1, n))],
        out_shape=[jax.ShapeDtypeStruct((L, n), F32), jax.ShapeDtypeStruct((1, n), F32)],
        compiler_params=_params("arbitrary"))(x, dhn1, dx2, g_pre)


CONV_TILE = 256
HALO = 16


def _conv3(w, b, x0, x1, x2):
    return b + w[2:3] * x0 + w[1:2] * x1 + w[0:1] * x2


def _down(x, by):
    return pltpu.roll(x, by, 0)


def _edge_down(edge, before, by):
    r = lax.broadcasted_iota(jnp.int32, edge.shape, 0)
    return jnp.where(r < by, pltpu.roll(before, by, 0), pltpu.roll(edge, by, 0))


def _edge_up(edge, after, by):
    r = lax.broadcasted_iota(jnp.int32, edge.shape, 0)
    return jnp.where(r >= HALO - by, pltpu.roll(after, HALO - by, 0), pltpu.roll(edge, HALO - by, 0))


def _gated(w_g, b_g, w_v, b_v, hg, hv, g1, g2, v1, v2):
    return _conv3(w_g, b_g, hg, g1, g2), _conv3(w_v, b_v, hv, v1, v2)


def _conv_specs(L, tl, nc, rows_inner):
    nh = tl // HALO
    if rows_inner:
        ij = lambda f: (lambda j, i: f(i, j))
    else:
        ij = lambda f: f
    cur = lambda off: pl.BlockSpec((tl, CONV_TILE), ij(lambda i, j: (i, off + j)))
    prev = lambda off: pl.BlockSpec((HALO, CONV_TILE), ij(lambda i, j: (jnp.maximum(i * nh - 1, 0), off + j)))
    nxt = lambda off: pl.BlockSpec((HALO, CONV_TILE), ij(lambda i, j: (jnp.minimum((i + 1) * nh, L // HALO - 1), off + j)))
    par = lambda rows, off: pl.BlockSpec((rows, CONV_TILE), ij(lambda i, j: (0, off + j)))
    return cur, prev, nxt, par


def _conv_act_call(h, conv_w, conv_b):
    L = h.shape[0]
    tl = _fit(L, 512)
    nc = D_FF // CONV_TILE
    cur, prev, _, par = _conv_specs(L, tl, nc, False)

    def body(hg_ref, hv_ref, pg_ref, pv_ref, wg_ref, wv_ref, bg_ref, bv_ref, a_ref):
        not_first = (pl.program_id(0) > 0).astype(F32)
        par = (wg_ref[...], bg_ref[...], wv_ref[...], bv_ref[...])
        hg, hv = hg_ref[...], hv_ref[...]
        gate, val = _gated(*par, hg, hv, _down(hg, 1), _down(hg, 2), _down(hv, 1), _down(hv, 2))
        a_ref[...] = (_gelu(gate) * val).astype(BF16)
        eg, ev, bg, bv = hg[:HALO], hv[:HALO], pg_ref[...] * not_first, pv_ref[...] * not_first
        gate, val = _gated(*par, eg, ev, _edge_down(eg, bg, 1), _edge_down(eg, bg, 2),
                           _edge_down(ev, bv, 1), _edge_down(ev, bv, 2))
        a_ref[:HALO, :] = (_gelu(gate) * val).astype(BF16)

    return pl.pallas_call(
        body, name="conv_act", grid=(L // tl, nc),
        in_specs=[cur(0), cur(nc), prev(0), prev(nc), par(3, 0), par(3, nc), par(1, 0), par(1, nc)],
        out_specs=cur(0), out_shape=jax.ShapeDtypeStruct((L, D_FF), BF16),
        compiler_params=_params("parallel", "parallel"))(h, h, h, h, conv_w, conv_w, conv_b, conv_b)


def _conv_act_bwd_call(da, h, conv_w, conv_b):
    L = h.shape[0]
    tl = _fit(L, 512)
    nc = D_FF // CONV_TILE
    cur, prev, _, par = _conv_specs(L, tl, nc, True)

    def body(da_ref, hg_ref, hv_ref, pg_ref, pv_ref, wg_ref, wv_ref, bg_ref, bv_ref,
             dg_ref, dv_ref, dwg_ref, dwv_ref, dbg_ref, dbv_ref):
        first = pl.program_id(1) == 0
        not_first = (pl.program_id(1) > 0).astype(F32)
        par = (wg_ref[...], bg_ref[...], wv_ref[...], bv_ref[...])
        col = lambda t: jnp.sum(t, axis=0, keepdims=True)

        def grads(da_, hg, hv, g1, g2, v1, v2):
            gate, val = _gated(*par, hg, hv, g1, g2, v1, v2)
            dgate = da_ * val * _gelu_grad(gate)
            dval = da_ * _gelu(gate)
            sums = (jnp.concatenate([col(dgate * g2), col(dgate * g1), col(dgate * hg)], axis=0),
                    jnp.concatenate([col(dval * v2), col(dval * v1), col(dval * hv)], axis=0), col(dgate), col(dval))
            return dgate, dval, sums

        da_, hg, hv = da_ref[...], hg_ref[...], hv_ref[...]
        shifted = (_down(hg, 1), _down(hg, 2), _down(hv, 1), _down(hv, 2))
        dgate, dval, whole = grads(da_, hg, hv, *shifted)
        dg_ref[...] = dgate.astype(BF16)
        dv_ref[...] = dval.astype(BF16)
        edge = lambda t: t[:HALO]
        _, _, wrapped = grads(edge(da_), edge(hg), edge(hv), *[edge(s) for s in shifted])
        eg, ev, bg, bv = edge(hg), edge(hv), pg_ref[...] * not_first, pv_ref[...] * not_first
        dgate, dval, fixed = grads(edge(da_), eg, ev, _edge_down(eg, bg, 1), _edge_down(eg, bg, 2),
                                   _edge_down(ev, bv, 1), _edge_down(ev, bv, 2))
        dg_ref[:HALO, :] = dgate.astype(BF16)
        dv_ref[:HALO, :] = dval.astype(BF16)
        for ref, a, b, c in zip((dwg_ref, dwv_ref, dbg_ref, dbv_ref), whole, wrapped, fixed):
            _acc(ref, first, a - b + c)

    act = jax.ShapeDtypeStruct((L, D_FF), BF16)
    w3 = jax.ShapeDtypeStruct((3, D_FF), F32)
    w1 = jax.ShapeDtypeStruct((1, D_FF), F32)
    return pl.pallas_call(
        body, name="conv_act_bwd", grid=(nc, L // tl),
        in_specs=[cur(0), cur(0), cur(nc), prev(0), prev(nc), par(3, 0), par(3, nc), par(1, 0), par(1, nc)],
        out_specs=[cur(0), cur(0), par(3, 0), par(3, 0), par(1, 0), par(1, 0)],
        out_shape=[act, act, w3, w3, w1, w1],
        compiler_params=_params("parallel", "arbitrary"))(da, h, h, h, h, conv_w, conv_w, conv_b, conv_b)


def _conv_t_call(dgate, dval, conv_w):
    L = dgate.shape[0]
    tl = _fit(L, 512)
    nc = D_FF // CONV_TILE
    cur, _, nxt, par = _conv_specs(L, tl, nc, False)

    def run(d, off, name):
        def body(d_ref, n_ref, w_ref, o_ref):
            not_last = (pl.program_id(0) < L // tl - 1).astype(F32)
            c = d_ref[...].astype(F32)
            w = w_ref[...]
            o_ref[...] = _conv3(w, 0.0, c, pltpu.roll(c, tl - 1, 0), pltpu.roll(c, tl - 2, 0)).astype(BF16)
            edge, after = c[tl - HALO:], n_ref[...].astype(F32) * not_last
            o_ref[tl - HALO:, :] = _conv3(w, 0.0, edge, _edge_up(edge, after, 1), _edge_up(edge, after, 2)).astype(BF16)

        return pl.pallas_call(
            body, name=name, grid=(L // tl, nc),
            in_specs=[cur(0), nxt(0), par(3, off)],
            out_specs=cur(0), out_shape=jax.ShapeDtypeStruct((L, D_FF), BF16),
            compiler_params=_params("parallel", "parallel"))(d, d, conv_w)

    return run(dgate, 0, "conv_t_gate"), run(dval, nc, "conv_t_val")


def _glu_call(y1, w_glu, b_glu):
    L, n = y1.shape
    tl = _fit(L, 512)

    def body(y_ref, w_ref, b_ref, o_ref):
        y2 = _gelu(y_ref[...])
        z = _dot(y2.astype(BF16), w_ref[...], NN) + b_ref[...]
        o_ref[...] = (y2 * _sigmoid(z)).astype(BF16)

    return pl.pallas_call(
        body, name="glu", grid=(L // tl,), in_specs=[_row(tl, n), _full((n, n)), _full((1, n))],
        out_specs=_row(tl, n), out_shape=jax.ShapeDtypeStruct((L, n), BF16),
        compiler_params=_params("parallel"))(y1, w_glu, b_glu)


def _glu_bwd_call(dout, y1, w_glu, b_glu):
    L, n = y1.shape
    tl = _fit(L, 512)

    def body(do_ref, y_ref, w_ref, b_ref, dy_ref, dw_ref, db_ref):
        first = pl.program_id(0) == 0
        y1_ = y_ref[...]
        y2 = _gelu(y1_)
        y2b = y2.astype(BF16)
        w = w_ref[...]
        sg = _sigmoid(_dot(y2b, w, NN) + b_ref[...])
        dout_ = do_ref[...].astype(F32)
        dz = dout_ * y2 * sg * (1.0 - sg)
        dzb = dz.astype(BF16)
        dy2 = dout_ * sg + _dot(dzb, w, NT)
        dy_ref[...] = dy2 * _gelu_grad(y1_)
        _acc(dw_ref, first, _dot(y2b, dzb, TN))
        _acc(db_ref, first, jnp.sum(dz, axis=0, keepdims=True))

    return pl.pallas_call(
        body, name="glu_bwd", grid=(L // tl,),
        in_specs=[_row(tl, n), _row(tl, n), _full((n, n)), _full((1, n))],
        out_specs=[_row(tl, n), _full((n, n)), _full((1, n))],
        out_shape=[jax.ShapeDtypeStruct((L, n), F32), jax.ShapeDtypeStruct((n, n), F32), jax.ShapeDtypeStruct((1, n), F32)],
        compiler_params=_params("arbitrary"))(dout, y1, w_glu, b_glu)


ATTN_TILE = 512
ATTN_SCALE = 1.0 / math.sqrt(QK_HEAD)


ATTN_HEADS = 2
ATTN_GROUPS = N_HEADS // ATTN_HEADS
LOG2E = 1.0 / math.log(2.0)
Q_PRESCALE = ATTN_SCALE * LOG2E
ANY_SPEC = pl.BlockSpec(memory_space=pl.ANY)


def _attn_fwd_call(q, kv, blocks):
    L = q.shape[0]
    t = _fit(L, ATTN_TILE)
    nq = L // t
    n = len(blocks)

    def body(q_ref, kv_ref, *refs):
        blk_refs, (o_ref, lse_ref), gat_refs = refs[:n], refs[n:n + 2], refs[n + 2:2 * n + 2]
        m_s, acc_s, send_sems, recv_sems, local_sems = refs[2 * n + 2:]
        g, i = pl.program_id(0), pl.program_id(1)
        start, forward, finish = _gather_phases(blk_refs, gat_refs, send_sems, recv_sems, local_sems)
        pl.when(jnp.logical_and(g == 0, i == 0))(start)
        m_s[...] = jnp.full((ATTN_HEADS, t, 1), NEG, F32)
        acc_s[...] = jnp.zeros((ATTN_HEADS, t, LANES), F32)
        below = lax.broadcasted_iota(jnp.int32, (t, t), 1) <= lax.broadcasted_iota(jnp.int32, (t, t), 0)

        def block_step(kb, on_diagonal):
            rows = pl.ds(pl.multiple_of(kb * t, t), t)
            for a in range(ATTN_HEADS):
                s = _dot(q_ref[:, a * LANES:(a + 1) * LANES], kv_ref[rows, 2 * a * LANES:(2 * a + 1) * LANES], NT)
                if on_diagonal:
                    s = jnp.where(below, s, NEG)
                m_prev = m_s[a]
                m_new = jnp.maximum(m_prev, jnp.max(s, axis=1, keepdims=True))
                p = jnp.exp2(s - m_new)
                pv = _dot(p.astype(BF16), kv_ref[rows, (2 * a + 1) * LANES:(2 * a + 2) * LANES], NN)
                acc_s[a] = jnp.exp2(m_prev - m_new) * acc_s[a] + pv
                m_s[a] = m_new

        def step(kb, carry):
            block_step(kb, False)
            return carry

        lax.fori_loop(0, i, step, 0)
        block_step(i, True)
        lane = lax.broadcasted_iota(jnp.int32, (t, LANES), 1)
        for a in range(ATTN_HEADS):
            acc = acc_s[a]
            l = jnp.sum(jnp.where(lane == V_HEAD, acc, 0.0), axis=1, keepdims=True)
            o_ref[:, a * LANES:(a + 1) * LANES] = (acc / l).astype(BF16)
            lse_ref[a] = m_s[a] + jnp.log(l) * LOG2E
        pl.when(jnp.logical_and(g == (3 * ATTN_GROUPS) // 4, i == 0))(forward)
        pl.when(jnp.logical_and(g == ATTN_GROUPS - 1, i == nq - 1))(finish)

    gw = ATTN_HEADS * LANES
    return pl.pallas_call(
        body, name="attn_fwd", grid=(ATTN_GROUPS, nq),
        in_specs=[pl.BlockSpec((t, gw), lambda g, i: (i, g)),
                  pl.BlockSpec((L, 2 * gw), lambda g, i: (0, g))] + [ANY_SPEC] * n,
        out_specs=[pl.BlockSpec((t, gw), lambda g, i: (i, g)),
                   pl.BlockSpec((ATTN_HEADS, t, 1), lambda g, i: (g, i, 0))] + [ANY_SPEC] * n,
        out_shape=[jax.ShapeDtypeStruct((L, HEAD_PAD), BF16), jax.ShapeDtypeStruct((N_HEADS, L, 1), F32)]
        + [jax.ShapeDtypeStruct((N_DEV,) + b.shape, b.dtype) for b in blocks],
        scratch_shapes=[pltpu.VMEM((ATTN_HEADS, t, 1), F32), pltpu.VMEM((ATTN_HEADS, t, LANES), F32)] + _comm_sems(n),
        compiler_params=_params("arbitrary", "arbitrary"))(q, kv, *blocks)


def _attn_bwd_call(q, kv, o, do, lse, parts, blocks):
    L = q.shape[0]
    t = _fit(L, ATTN_TILE)
    nq = L // t
    n1, n = len(parts), len(parts) + len(blocks)

    def body(q_ref, do_ref, o_ref, lse_ref, kv_ref, *refs):
        in_refs, (dq_ref, dkv_ref), out_refs = refs[:n], refs[n:n + 2], refs[n + 2:2 * n + 2]
        dk_s, dv_s = refs[2 * n + 2:2 * n + 4]
        g, j = pl.program_id(0), pl.program_id(1)
        start, finish = _exchange_phases(in_refs[:n1], out_refs[:n1], *refs[2 * n + 4:2 * n + 7])
        start_blocks, finish_blocks = _exchange_phases(in_refs[n1:], out_refs[n1:], *refs[2 * n + 7:], same_source=True)

        @pl.when(jnp.logical_and(g == 0, j == 0))
        def _():
            start()
            start_blocks()

        @pl.when(j == 0)
        def _():
            dq_ref[...] = jnp.zeros((L, ATTN_HEADS * LANES), F32)

        dk_s[...] = jnp.zeros((ATTN_HEADS, t, LANES), F32)
        dv_s[...] = jnp.zeros((ATTN_HEADS, t, LANES), F32)
        below = lax.broadcasted_iota(jnp.int32, (t, t), 1) <= lax.broadcasted_iota(jnp.int32, (t, t), 0)

        def block_step(i, on_diagonal):
            rows = pl.ds(pl.multiple_of(i * t, t), t)
            for a in range(ATTN_HEADS):
                lanes = slice(a * LANES, (a + 1) * LANES)
                qi = q_ref[rows, lanes]
                doi = do_ref[rows, lanes]
                kblk = kv_ref[:, 2 * a * LANES:(2 * a + 1) * LANES]
                delta = jnp.sum(doi.astype(F32) * o_ref[rows, lanes].astype(F32), axis=1, keepdims=True)
                s = _dot(qi, kblk, NT)
                if on_diagonal:
                    s = jnp.where(below, s, NEG)
                p = jnp.exp2(s - lse_ref[a, rows, :])
                dv_s[a] += _dot(p.astype(BF16), doi, TN)
                ds = (p * (_dot(doi, kv_ref[:, (2 * a + 1) * LANES:(2 * a + 2) * LANES], NT) - delta)).astype(BF16)
                dk_s[a] += _dot(ds, qi, TN)
                dq_ref[rows, lanes] += _dot(ds, kblk, NN) * ATTN_SCALE

        def step(i, carry):
            block_step(i, False)
            return carry

        block_step(j, True)
        lax.fori_loop(j + 1, nq, step, 0)
        for a in range(ATTN_HEADS):
            dkv_ref[:, 2 * a * LANES:(2 * a + 1) * LANES] = dk_s[a] * (1.0 / LOG2E)
            dkv_ref[:, (2 * a + 1) * LANES:(2 * a + 2) * LANES] = dv_s[a]
        @pl.when(jnp.logical_and(g == ATTN_GROUPS - 1, j == nq - 1))
        def _():
            finish()
            finish_blocks()

    gw = ATTN_HEADS * LANES
    whole = lambda: pl.BlockSpec((L, gw), lambda g, j: (0, g))
    acc = pltpu.VMEM((ATTN_HEADS, t, LANES), F32)
    return pl.pallas_call(
        body, name="attn_bwd", grid=(ATTN_GROUPS, nq),
        in_specs=[whole(), whole(), whole(), pl.BlockSpec((ATTN_HEADS, L, 1), lambda g, j: (g, 0, 0)),
                  pl.BlockSpec((t, 2 * gw), lambda g, j: (j, g))] + [ANY_SPEC] * n,
        out_specs=[whole(), pl.BlockSpec((t, 2 * gw), lambda g, j: (j, g))] + [ANY_SPEC] * n,
        out_shape=[jax.ShapeDtypeStruct((L, HEAD_PAD), F32), jax.ShapeDtypeStruct((L, 2 * HEAD_PAD), F32)]
        + [jax.ShapeDtypeStruct(p.shape, p.dtype) for p in parts]
        + [jax.ShapeDtypeStruct((N_DEV,) + b.shape, b.dtype) for b in blocks],
        scratch_shapes=[acc, acc] + _comm_sems(n1) + _comm_sems(n - n1),
        compiler_params=_params("arbitrary", "arbitrary"))(q, do, o, lse, kv, *parts, *blocks)


def _disc(lr, li, ldt, br, bi):
    dt = jnp.exp(ldt)
    mag = jnp.exp(lr * dt)
    ang = li * dt
    a_re, a_im = mag * jnp.cos(ang), mag * jnp.sin(ang)
    den = lr * lr + li * li
    n_re, n_im = a_re - 1.0, a_im
    z_re = (n_re * lr + n_im * li) / den
    z_im = (n_im * lr - n_re * li) / den
    return a_re, a_im, z_re * br - z_im * bi, z_re * bi + z_im * br


def _disc_call(lr, li, ldt, br, bi):
    def body(lr_ref, li_ref, ldt_ref, br_ref, bi_ref, ar_ref, ai_ref, bbr_ref, bbi_ref):
        ar_ref[...], ai_ref[...], bbr_ref[...], bbi_ref[...] = _disc(
            lr_ref[...], li_ref[...], ldt_ref[...], br_ref[...], bi_ref[...])

    c1 = jax.ShapeDtypeStruct((SSM_NSTATE, 1), F32)
    c16 = jax.ShapeDtypeStruct((SSM_NSTATE, SSM_GROUP), F32)
    return pl.pallas_call(body, name="ssm_disc", out_shape=[c1, c1, c16, c16])(lr, li, ldt, br, bi)


def _disc_bwd_call(lr, li, ldt, br, bi, dar, dai, dbbr, dbbi):
    def body(lr_ref, li_ref, ldt_ref, br_ref, bi_ref, dar_ref, dai_ref, dbbr_ref, dbbi_ref,
             dlr_ref, dli_ref, dldt_ref, dbr_ref, dbi_ref):
        _, vjp = jax.vjp(_disc, lr_ref[...], li_ref[...], ldt_ref[...], br_ref[...], bi_ref[...])
        dlr_ref[...], dli_ref[...], dldt_ref[...], dbr_ref[...], dbi_ref[...] = vjp(
            (dar_ref[...], dai_ref[...], dbbr_ref[...], dbbi_ref[...]))

    c1 = jax.ShapeDtypeStruct((SSM_NSTATE, 1), F32)
    c16 = jax.ShapeDtypeStruct((SSM_NSTATE, SSM_GROUP), F32)
    return pl.pallas_call(body, name="ssm_disc_bwd", out_shape=[c1, c1, c1, c16, c16])(
        lr, li, ldt, br, bi, dar, dai, dbbr, dbbi)


SSM_ROWS = 512
SSM_CW = SSM_NSTATE // SSM_CHUNKS
SSM_CU = SSM_WIDTH // SSM_CHUNKS


def _cmul(ar, ai, br, bi):
    return ar * br - ai * bi, ar * bi + ai * br


def _power(ar1, ai1, n):
    def step(_, c):
        return _cmul(c[0], c[1], ar1, ai1)

    return lax.fori_loop(0, n, step, (jnp.ones_like(ar1), jnp.zeros_like(ar1)))


def _tile(k):
    return pl.ds(pl.multiple_of(k * 8, 8), 8)


def _ssm_fwd_call(u, a_re, a_im, bb_re, bb_im, cm_re, cm_im, d_skip):
    L = u.shape[0]
    seg = L // 8
    rb = _fit(L, SSM_ROWS)

    def body(u_ref, ar_ref, ai_ref, bbr_ref, bbi_ref, cmr_ref, cmi_ref, d_ref, y_ref, sre_hbm, sim_hbm,
             s_re, s_im, sems):
        q = pl.program_id(0)

        def bu_step(r, c):
            rows = pl.ds(pl.multiple_of(r * rb, rb), rb)
            ub = u_ref[rows, :].astype(BF16)
            s_re[rows, :] = _dot(ub, bbr_ref[0], NN)
            s_im[rows, :] = _dot(ub, bbi_ref[0], NN)
            return c

        lax.fori_loop(0, L // rb, bu_step, 0)
        ar1, ai1 = ar_ref[...], ai_ref[...]
        ar = jnp.broadcast_to(ar1, (8, SSM_CW))
        ai = jnp.broadcast_to(ai1, (8, SSM_CW))

        def local(k, c):
            nr, ni = _cmul(ar, ai, c[0], c[1])
            nr = nr + s_re[_tile(k), :]
            ni = ni + s_im[_tile(k), :]
            s_re[_tile(k), :] = nr
            s_im[_tile(k), :] = ni
            return nr, ni

        zero8 = jnp.zeros((8, SSM_CW), F32)
        lax.fori_loop(0, seg, local, (zero8, zero8))
        pr, pi = _power(ar1, ai1, seg)
        end_r = s_re[pl.ds((seg - 1) * 8, 8), :]
        end_i = s_im[pl.ds((seg - 1) * 8, 8), :]
        er = jnp.zeros((1, SSM_CW), F32)
        ei = jnp.zeros((1, SSM_CW), F32)
        rows_r, rows_i = [er], [ei]
        for j in range(7):
            tr, ti = _cmul(pr, pi, er, ei)
            er, ei = end_r[j:j + 1] + tr, end_i[j:j + 1] + ti
            rows_r.append(er)
            rows_i.append(ei)
        e_r = jnp.concatenate(rows_r, axis=0)
        e_i = jnp.concatenate(rows_i, axis=0)

        def fix(k, c):
            wr, wi = _cmul(c[0], c[1], ar, ai)
            fr, fi = _cmul(wr, wi, e_r, e_i)
            s_re[_tile(k), :] += fr
            s_im[_tile(k), :] += fi
            return wr, wi

        lax.fori_loop(0, seg, fix, (jnp.ones((8, SSM_CW), F32), zero8))
        out_r = pltpu.make_async_copy(s_re, sre_hbm.at[q], sems.at[0])
        out_i = pltpu.make_async_copy(s_im, sim_hbm.at[q], sems.at[1])
        out_r.start()
        out_i.start()

        def y_step(r, c):
            rows = pl.ds(pl.multiple_of(r * rb, rb), rb)
            y = _dot(s_re[rows, :].astype(BF16), cmr_ref[0], NN) - _dot(s_im[rows, :].astype(BF16), cmi_ref[0], NN)
            y_ref[rows, :] = y + d_ref[...] * u_ref[rows, :]
            return c

        lax.fori_loop(0, L // rb, y_step, 0)
        out_r.wait()
        out_i.wait()

    chunk = lambda rows, cols: pl.BlockSpec((rows, cols), lambda q: (0, q))
    mat = lambda r, c: pl.BlockSpec((1, r, c), lambda q: (q, 0, 0))
    anyspec = pl.BlockSpec(memory_space=pl.ANY)
    states = jax.ShapeDtypeStruct((SSM_CHUNKS, L, SSM_CW), F32)
    return pl.pallas_call(
        body, name="ssm_fwd", grid=(SSM_CHUNKS,),
        in_specs=[chunk(L, SSM_CU), chunk(1, SSM_CW), chunk(1, SSM_CW), mat(SSM_CU, SSM_CW), mat(SSM_CU, SSM_CW),
                  mat(SSM_CW, SSM_CU), mat(SSM_CW, SSM_CU), chunk(1, SSM_CU)],
        out_specs=[chunk(L, SSM_CU), anyspec, anyspec],
        out_shape=[jax.ShapeDtypeStruct((L, SSM_WIDTH), F32), states, states],
        scratch_shapes=[pltpu.VMEM((L, SSM_CW), F32), pltpu.VMEM((L, SSM_CW), F32), pltpu.SemaphoreType.DMA((2,))],
        compiler_params=_params("arbitrary", vmem=VMEM_BIG))(u, a_re, a_im, bb_re, bb_im, cm_re, cm_im, d_skip)


def _ssm_bwd_call(dy, u, s_re_all, s_im_all, a_re, a_im, bb_re, bb_im, cm_re, cm_im, d_skip):
    L = u.shape[0]
    seg = L // 8
    rb = _fit(L, SSM_ROWS)

    def body(dy_ref, u_ref, sre_hbm, sim_hbm, ar_ref, ai_ref, bbr_ref, bbi_ref, cmr_ref, cmi_ref, d_ref,
             du_ref, dbbr_ref, dbbi_ref, dcmr_ref, dcmi_ref, dar_ref, dai_ref, dd_ref,
             g_re, g_im, s_re, s_im, sems):
        q = pl.program_id(0)
        in_r = pltpu.make_async_copy(sre_hbm.at[q], s_re, sems.at[0])
        in_i = pltpu.make_async_copy(sim_hbm.at[q], s_im, sems.at[1])
        in_r.start()
        in_i.start()

        def ds_step(r, c):
            rows = pl.ds(pl.multiple_of(r * rb, rb), rb)
            dyb = dy_ref[rows, :].astype(BF16)
            g_re[rows, :] = _dot(dyb, cmr_ref[0], NT)
            g_im[rows, :] = -_dot(dyb, cmi_ref[0], NT)
            return c

        lax.fori_loop(0, L // rb, ds_step, 0)
        ar1, ai1 = ar_ref[...], ai_ref[...]
        ar = jnp.broadcast_to(ar1, (8, SSM_CW))
        nai = jnp.broadcast_to(-ai1, (8, SSM_CW))

        def local(kk, c):
            k = seg - 1 - kk
            nr, ni = _cmul(ar, nai, c[0], c[1])
            nr = nr + g_re[_tile(k), :]
            ni = ni + g_im[_tile(k), :]
            g_re[_tile(k), :] = nr
            g_im[_tile(k), :] = ni
            return nr, ni

        zero8 = jnp.zeros((8, SSM_CW), F32)
        lax.fori_loop(0, seg, local, (zero8, zero8))
        pr, pi = _power(ar1, -ai1, seg)
        head_r = g_re[pl.ds(0, 8), :]
        head_i = g_im[pl.ds(0, 8), :]
        fr = jnp.zeros((1, SSM_CW), F32)
        fi = jnp.zeros((1, SSM_CW), F32)
        rows_r, rows_i = [fr], [fi]
        for j in range(6, -1, -1):
            tr, ti = _cmul(pr, pi, fr, fi)
            fr, fi = head_r[j + 1:j + 2] + tr, head_i[j + 1:j + 2] + ti
            rows_r.insert(0, fr)
            rows_i.insert(0, fi)
        f_r = jnp.concatenate(rows_r, axis=0)
        f_i = jnp.concatenate(rows_i, axis=0)
        in_r.wait()
        in_i.wait()

        def fixed(k, wr, wi):
            xr, xi = _cmul(wr, wi, f_r, f_i)
            gr = g_re[_tile(k), :] + xr
            gi = g_im[_tile(k), :] + xi
            g_re[_tile(k), :] = gr
            g_im[_tile(k), :] = gi
            return gr, gi

        def fix(kk, c):
            k = seg - 1 - kk
            wr, wi = _cmul(c[0], c[1], ar, nai)
            gr, gi = fixed(k, wr, wi)
            pr_, pi_ = s_re[_tile(k - 1), :], s_im[_tile(k - 1), :]
            return wr, wi, c[2] + gr * pr_ + gi * pi_, c[3] + gi * pr_ - gr * pi_

        wr, wi, acc_r, acc_i = lax.fori_loop(0, seg - 1, fix, (jnp.ones((8, SSM_CW), F32), zero8, zero8, zero8))
        wr, wi = _cmul(wr, wi, ar, nai)
        gr, gi = fixed(0, wr, wi)
        row8 = lax.broadcasted_iota(jnp.int32, (8, SSM_CW), 0)
        pr_ = jnp.where(row8 > 0, pltpu.roll(s_re[pl.ds((seg - 1) * 8, 8), :], 1, 0), 0.0)
        pi_ = jnp.where(row8 > 0, pltpu.roll(s_im[pl.ds((seg - 1) * 8, 8), :], 1, 0), 0.0)
        acc_r = acc_r + gr * pr_ + gi * pi_
        acc_i = acc_i + gi * pr_ - gr * pi_
        dar_ref[...] = jnp.sum(acc_r, axis=0, keepdims=True)
        dai_ref[...] = jnp.sum(acc_i, axis=0, keepdims=True)

        dbbr_ref[...] = jnp.zeros((1, SSM_CU, SSM_CW), F32)
        dbbi_ref[...] = jnp.zeros((1, SSM_CU, SSM_CW), F32)
        dcmr_ref[...] = jnp.zeros((1, SSM_CW, SSM_CU), F32)
        dcmi_ref[...] = jnp.zeros((1, SSM_CW, SSM_CU), F32)
        dd_ref[...] = jnp.zeros((1, SSM_CU), F32)

        def grad_step(r, c):
            rows = pl.ds(pl.multiple_of(r * rb, rb), rb)
            ub, dyv = u_ref[rows, :], dy_ref[rows, :]
            ubb, dyb = ub.astype(BF16), dyv.astype(BF16)
            grb, gib = g_re[rows, :].astype(BF16), g_im[rows, :].astype(BF16)
            dbbr_ref[0] += _dot(ubb, grb, TN)
            dbbi_ref[0] += _dot(ubb, gib, TN)
            dcmr_ref[0] += _dot(s_re[rows, :].astype(BF16), dyb, TN)
            dcmi_ref[0] -= _dot(s_im[rows, :].astype(BF16), dyb, TN)
            du_ref[rows, :] = _dot(grb, bbr_ref[0], NT) + _dot(gib, bbi_ref[0], NT) + d_ref[...] * dyv
            dd_ref[...] += jnp.sum(dyv * ub, axis=0, keepdims=True)
            return c

        lax.fori_loop(0, L // rb, grad_step, 0)

    chunk = lambda rows, cols: pl.BlockSpec((rows, cols), lambda q: (0, q))
    mat = lambda r, c: pl.BlockSpec((1, r, c), lambda q: (q, 0, 0))
    anyspec = pl.BlockSpec(memory_space=pl.ANY)
    big = lambda: pltpu.VMEM((L, SSM_CW), F32)
    return pl.pallas_call(
        body, name="ssm_bwd", grid=(SSM_CHUNKS,),
        in_specs=[chunk(L, SSM_CU), chunk(L, SSM_CU), anyspec, anyspec, chunk(1, SSM_CW), chunk(1, SSM_CW),
                  mat(SSM_CU, SSM_CW), mat(SSM_CU, SSM_CW), mat(SSM_CW, SSM_CU), mat(SSM_CW, SSM_CU), chunk(1, SSM_CU)],
        out_specs=[chunk(L, SSM_CU), mat(SSM_CU, SSM_CW), mat(SSM_CU, SSM_CW), mat(SSM_CW, SSM_CU), mat(SSM_CW, SSM_CU),
                   chunk(1, SSM_CW), chunk(1, SSM_CW), chunk(1, SSM_CU)],
        out_shape=[jax.ShapeDtypeStruct((L, SSM_WIDTH), F32),
                   jax.ShapeDtypeStruct((SSM_CHUNKS, SSM_CU, SSM_CW), F32), jax.ShapeDtypeStruct((SSM_CHUNKS, SSM_CU, SSM_CW), F32),
                   jax.ShapeDtypeStruct((SSM_CHUNKS, SSM_CW, SSM_CU), F32), jax.ShapeDtypeStruct((SSM_CHUNKS, SSM_CW, SSM_CU), F32),
                   jax.ShapeDtypeStruct((1, SSM_NSTATE), F32), jax.ShapeDtypeStruct((1, SSM_NSTATE), F32),
                   jax.ShapeDtypeStruct((1, SSM_WIDTH), F32)],
        scratch_shapes=[big(), big(), big(), big(), pltpu.SemaphoreType.DMA((2,))],
        compiler_params=_params("arbitrary", vmem=VMEM_BIG))(
            dy, u, s_re_all, s_im_all, a_re, a_im, bb_re, bb_im, cm_re, cm_im, d_skip)


def _place():
    return lax.axis_index("x"), lax.axis_index("y"), lax.axis_index("c")


def _all_gather_call(blocks, name, direct=False):
    n = len(blocks)

    def body(*refs):
        if direct:
            start, finish = _exchange_phases(refs[:n], refs[n:2 * n], *refs[2 * n:], same_source=True)
            start()
        else:
            start, forward, finish = _gather_phases(refs[:n], refs[n:2 * n], *refs[2 * n:])
            start()
            forward()
        finish()

    return pl.pallas_call(
        body, name=name, in_specs=[ANY_SPEC] * n, out_specs=[ANY_SPEC] * n,
        out_shape=[jax.ShapeDtypeStruct((N_DEV,) + b.shape, b.dtype) for b in blocks],
        scratch_shapes=_comm_sems(n))(*blocks)


def _comm_sems(n):
    return [pltpu.SemaphoreType.DMA((7 * n,)), pltpu.SemaphoreType.DMA((7 * n,)), pltpu.SemaphoreType.DMA((n,))]


def _gather_phases(x_refs, out_refs, send_sems, recv_sems, local_sems):
    x, y, c = _place()
    me, sibling = (x, y, c), (x, y, 1 - c)
    chips = [(1 - x, y), (x, 1 - y), (1 - x, 1 - y)]
    n = len(x_refs)

    def copy(k, a, blk, to, from_input=False):
        slot = out_refs[a].at[4 * blk[0] + 2 * blk[1] + blk[2]]
        return pltpu.make_async_remote_copy(
            src_ref=x_refs[a] if from_input else slot, dst_ref=slot,
            send_sem=send_sems.at[k * n + a], recv_sem=recv_sems.at[k * n + a], device_id=to, device_id_type=MESH_ID)

    mine = [pltpu.make_async_copy(x_refs[a], out_refs[a].at[4 * x + 2 * y + c], local_sems.at[a]) for a in range(n)]
    first, passed = [], []
    for a in range(n):
        first.append(copy(0, a, me, sibling, True))
        first += [copy(1 + j, a, me, (*chip, c), True) for j, chip in enumerate(chips)]
        passed += [copy(4 + j, a, (*chip, c), sibling) for j, chip in enumerate(chips)]

    def start():
        for cp in mine + first:
            cp.start()

    def forward():
        for j, chip in enumerate(chips):
            for a in range(n):
                copy(1 + j, a, (*chip, c), me).wait_recv()
                passed[3 * a + j].start()

    def finish():
        for a in range(n):
            copy(0, a, sibling, me).wait_recv()
            for j, chip in enumerate(chips):
                copy(4 + j, a, (*chip, 1 - c), me).wait_recv()
        for cp in first + passed:
            cp.wait_send()
        for cp in mine:
            cp.wait()

    return start, forward, finish


def _exchange_phases(p_refs, out_refs, send_sems, recv_sems, local_sems, same_source=False):
    x, y, c = _place()
    me = 4 * x + 2 * y + c
    n = len(p_refs)

    def flip(k):
        px = 1 - x if k & 4 else x
        py = 1 - y if k & 2 else y
        pc = 1 - c if k & 1 else c
        return (px, py, pc), 4 * px + 2 * py + pc

    def source(a, slot):
        return p_refs[a] if same_source else p_refs[a].at[slot]

    def copy(k, a, landing):
        peer, peer_slot = flip(k)
        return pltpu.make_async_remote_copy(
            src_ref=source(a, peer_slot), dst_ref=out_refs[a].at[peer_slot if landing else me],
            send_sem=send_sems.at[(k - 1) * n + a], recv_sem=recv_sems.at[(k - 1) * n + a],
            device_id=peer, device_id_type=MESH_ID)

    mine = [pltpu.make_async_copy(source(a, me), out_refs[a].at[me], local_sems.at[a]) for a in range(n)]
    sends = [copy(k, a, False) for k in range(1, N_DEV) for a in range(n)]

    def start():
        for cp in mine + sends:
            cp.start()

    def finish():
        for k in range(1, N_DEV):
            for a in range(n):
                copy(k, a, True).wait_recv()
        for cp in sends:
            cp.wait_send()
        for cp in mine:
            cp.wait()

    return start, finish


def _adam_math(g, w, m, v):
    c1 = 1.0 / (1.0 - ADAM_B1 ** ADAM_STEP)
    c2 = 1.0 / (1.0 - ADAM_B2 ** ADAM_STEP)
    m_new = ADAM_B1 * m + (1.0 - ADAM_B1) * g
    v_new = ADAM_B2 * v + (1.0 - ADAM_B2) * (g * g)
    delta = -ADAM_LR * ((m_new * c1) / (jnp.sqrt(v_new * c2) + ADAM_EPS) + ADAM_WD * w)
    return g, delta, m_new, v_new


def _sum_slices(s_ref):
    g = s_ref[0].astype(F32)
    for k in range(1, N_DEV):
        g = g + s_ref[k].astype(F32)
    return g


def _adam_call(slices, w, m, v, name):
    rest = w.shape[2:]
    t1 = _fit(w.shape[1], 256, 16) if len(rest) == 1 else _fit(w.shape[1], 8, 8)
    zeros = (0,) * len(rest)

    def body(s_ref, w_ref, m_ref, v_ref, g_ref, d_ref, mo_ref, vo_ref):
        g_ref[...], d_ref[...], mo_ref[...], vo_ref[...] = _adam_math(_sum_slices(s_ref), w_ref[...], m_ref[...], v_ref[...])

    own = pl.BlockSpec((1, t1) + rest, lambda i: (0, i) + zeros)
    out = jax.ShapeDtypeStruct(w.shape, F32)
    return pl.pallas_call(
        body, name=name, grid=(w.shape[1] // t1,),
        in_specs=[pl.BlockSpec((N_DEV, 1, t1) + rest, lambda i: (0, 0, i) + zeros), own, own, own],
        out_specs=[own, own, own, own], out_shape=[out, out, out, out],
        compiler_params=_params("parallel"))(slices, w, m, v)


def _adam_small_call(rows_all, row_params, slices, params):
    nr, n = len(row_params), len(row_params) + len(params)

    def body(rows_ref, *refs):
        slice_refs, wmv, outs = refs[:n - nr], refs[n - nr:n - nr + 3 * n], refs[n - nr + 3 * n:]
        for a in range(n):
            w_ref, m_ref, v_ref = wmv[3 * a:3 * a + 3]
            if a < nr:
                width = w_ref.shape[1]
                g = rows_ref[0, pl.ds(a, 1), pl.ds(0, width)]
                for k in range(1, N_DEV):
                    g = g + rows_ref[k, pl.ds(a, 1), pl.ds(0, width)]
            else:
                g = _sum_slices(slice_refs[a - nr])
            res = _adam_math(g, w_ref[...], m_ref[...], v_ref[...])
            for r in range(4):
                outs[4 * a + r][...] = res[r]

    every = list(row_params) + list(params)
    flat = pl.pallas_call(
        body, name="adam_small",
        out_shape=[jax.ShapeDtypeStruct(w.shape, F32) for w, _, _ in every for _ in range(4)],
    )(rows_all, *slices, *[t for wmv in every for t in wmv])
    return [flat[4 * a:4 * a + 4] for a in range(n)]


BIG = (("w_in", 1024, 404, 1), ("w_uq", 384, 96, 1), ("w_uk", 256, 64, 1), ("w_uv", 256, 64, 1),
       ("w_glu", 64, 512, 0), ("w_branch_attn", 512, 128, 1), ("w_branch_ssm", 512, 128, 1),
       ("w_out", 128, 1024, 0), ("w_up", 1024, 704, 1), ("w_down", 352, 1024, 0), ("conv_w", 3, 704, 1))
BIG_MIX, BIG_FFN = BIG[:8], BIG[8:]
GRADS_EARLY, GRADS_LATE = BIG[8:] + BIG[4:8], BIG[:4]
SMALL = (("mix_norm_pre", (1024,)), ("q_norm", (384,)), ("kv_norm", (256,)), ("ssm_lambda_re", (32, 64)),
         ("ssm_lambda_im", (32, 64)), ("ssm_log_dt", (32,)), ("ssm_b_re", (32, 64, 16)), ("ssm_b_im", (32, 64, 16)),
         ("ssm_c_re", (32, 16, 64)), ("ssm_c_im", (32, 16, 64)), ("ssm_d", (32, 16)), ("b_glu", (512,)),
         ("b_gate", (2048,)), ("mix_norm_post", (1024,)), ("ffn_norm_pre", (1024,)), ("conv_b", (5632,)),
         ("ffn_norm_post", (1024,)))


def _to_slices(full, rows, cols, axis):
    if axis == 1:
        return full.reshape(rows, N_DEV, cols).transpose(1, 0, 2)
    return full.reshape(N_DEV, rows, cols)


def _from_slices(parts, rows, cols, axis):
    if axis == 1:
        return parts.transpose(1, 0, 2).reshape(rows, N_DEV * cols)
    return parts.reshape(N_DEV * rows, cols)


def _head_pad_cols(w, width):
    k = w.shape[0]
    return jnp.pad(w.reshape(k, N_HEADS, width), ((0, 0), (0, 0), (0, LANES - width))).reshape(k, HEAD_PAD)


def _head_unpad_cols(w, width):
    k = w.shape[0]
    return w.reshape(k, N_HEADS, LANES)[:, :, :width].reshape(k, N_HEADS * width)


def _time_perm(a, L):
    return a.reshape(8, L // 8, a.shape[-1]).transpose(1, 0, 2).reshape(L, a.shape[-1])


def _time_unperm(a, L):
    return a.reshape(L // 8, 8, a.shape[-1]).transpose(1, 0, 2).reshape(L, a.shape[-1])


def _block_diag(w, rows_first):
    eye = jnp.eye(8, dtype=w.dtype)
    g = w.reshape(SSM_CHUNKS, 8, w.shape[1], w.shape[2])
    return jnp.einsum("qgrc,gk->qgrkc", g, eye).reshape(SSM_CHUNKS, 8 * w.shape[1], 8 * w.shape[2])


def _block_diag_t(m, r, c):
    eye = jnp.eye(8, dtype=m.dtype)
    return jnp.einsum("qgrkc,gk->qgrc", m.reshape(SSM_CHUNKS, 8, r, 8, c), eye).reshape(SSM_GROUPS, r, c)


def kernel(x, positions, mix_norm_pre, w_in, q_norm, w_uq, kv_norm, w_uk, w_uv, ssm_lambda_re, ssm_lambda_im, ssm_log_dt, ssm_b_re, ssm_b_im, ssm_c_re, ssm_c_im, ssm_d, w_glu, b_glu, w_branch_attn, w_branch_ssm, b_gate, w_out, mix_norm_post, ffn_norm_pre, w_up, conv_w, conv_b, w_down, ffn_norm_post, loss_target, m_mix_norm_pre, m_w_in, m_q_norm, m_w_uq, m_kv_norm, m_w_uk, m_w_uv, m_ssm_lambda_re, m_ssm_lambda_im, m_ssm_log_dt, m_ssm_b_re, m_ssm_b_im, m_ssm_c_re, m_ssm_c_im, m_ssm_d, m_w_glu, m_b_glu, m_w_branch_attn, m_w_branch_ssm, m_b_gate, m_w_out, m_mix_norm_post, m_ffn_norm_pre, m_w_up, m_conv_w, m_conv_b, m_w_down, m_ffn_norm_post, v_mix_norm_pre, v_w_in, v_q_norm, v_w_uq, v_kv_norm, v_w_uk, v_w_uv, v_ssm_lambda_re, v_ssm_lambda_im, v_ssm_log_dt, v_ssm_b_re, v_ssm_b_im, v_ssm_c_re, v_ssm_c_im, v_ssm_d, v_w_glu, v_b_glu, v_w_branch_attn, v_w_branch_ssm, v_b_gate, v_w_out, v_mix_norm_post, v_ffn_norm_pre, v_w_up, v_conv_w, v_conv_b, v_w_down, v_ffn_norm_post):
    given = dict(locals())
    L = x.shape[1]
    xs = x[0]
    target = loss_target[0]

    def shard_bits(group):
        return [given[name][0] if name == "conv_w" else given[name][0].astype(BF16) for name, _, _, _ in group]

    W = {}

    def unpack_weights(gathered, group):
        for (name, rows, cols, axis), parts in zip(group, gathered):
            W[name] = _from_slices(parts, rows, cols, axis)

    unpack_weights(_all_gather_call(shard_bits(BIG_MIX), "gather_weights"), BIG_MIX)

    wi = W["w_in"]
    kr_cols = jnp.pad(wi[:, 640:672], ((0, 0), (QK_NOPE, LANES - QK_HEAD)))
    w_in_p = jnp.concatenate([wi[:, :640], kr_cols, wi[:, 672:]], axis=1)
    w_uq_p = _head_pad_cols(W["w_uq"], QK_HEAD)
    w_kv_p = jnp.stack([_head_pad_cols(W["w_uk"], QK_NOPE).reshape(KV_RANK, N_HEADS, LANES),
                        _head_pad_cols(W["w_uv"], V_HEAD).reshape(KV_RANK, N_HEADS, LANES)], axis=2
                       ).reshape(KV_RANK, 2 * HEAD_PAD)
    w_ba_p = jnp.pad(W["w_branch_attn"].reshape(N_HEADS, V_HEAD, D_MODEL), ((0, 0), (0, LANES - V_HEAD), (0, 0))
                     ).reshape(HEAD_PAD, D_MODEL)

    hn1 = _rms_fwd_call(xs, mix_norm_pre, "rms_pre")
    proj = _mm(hn1, w_in_p, "mm_in", tn=1664)
    qn, ckvn = _mla_norms_call(proj, q_norm, kv_norm)
    q_pad = _mm(qn, w_uq_p, "mm_uq")
    kv_pad = _mm(ckvn, w_kv_p, "mm_ukv")
    half = jnp.arange(QK_ROPE // 2, dtype=F32)
    inv_freq = ROPE_THETA ** (-2.0 * half / QK_ROPE)
    inv_freq = jnp.pad(jnp.concatenate([inv_freq, inv_freq]), (QK_NOPE, LANES - QK_HEAD)).reshape(1, LANES)
    pos_col = positions.astype(F32).reshape(L, 1)
    q_r, kv_r, cosf, sinf = _mla_prep_call(q_pad, kv_pad, proj, pos_col, inv_freq)
    attn, lse, *gathered_ffn = _attn_fwd_call(q_r, kv_r, shard_bits(BIG_FFN))
    unpack_weights(gathered_ffn, BIG_FFN)

    col = lambda a: a.reshape(SSM_NSTATE, -1)
    lr_c, li_c = col(ssm_lambda_re[0]), col(ssm_lambda_im[0])
    ldt_c = col(jnp.broadcast_to(ssm_log_dt[0][:, None], (SSM_GROUPS, SSM_STATE)))
    br_c, bi_c = col(ssm_b_re[0]), col(ssm_b_im[0])
    a_re_c, a_im_c, bb_re_c, bb_im_c = _disc_call(lr_c, li_c, ldt_c, br_c, bi_c)
    a_re, a_im = a_re_c.reshape(1, SSM_NSTATE), a_im_c.reshape(1, SSM_NSTATE)
    to_bb = lambda b: _block_diag(b.reshape(SSM_GROUPS, SSM_STATE, SSM_GROUP).transpose(0, 2, 1), True).astype(BF16)
    bb_re, bb_im = to_bb(bb_re_c), to_bb(bb_im_c)
    to_cm = lambda c_: _block_diag(c_[0].transpose(0, 2, 1), True).astype(BF16)
    cm_re, cm_im = to_cm(ssm_c_re), to_cm(ssm_c_im)
    d_skip = ssm_d.reshape(1, SSM_WIDTH)
    u_p = _time_perm(proj[:, P_U:P_GATE], L)
    y1, s_re, s_im = _ssm_fwd_call(u_p, a_re, a_im, bb_re, bb_im, cm_re, cm_im, d_skip)
    w_glu_b = W["w_glu"]
    ssm_p = _glu_call(y1, w_glu_b, b_glu)
    ssm = _time_unperm(ssm_p, L)

    pa = _mm(attn, w_ba_p, "mm_ba")
    ps = _mm(ssm, W["w_branch_ssm"], "mm_bs")
    merged = _merge_call(proj, b_gate, pa, ps)
    o = _mm(merged, W["w_out"], "mm_out")
    x2, hn2 = _post_mix_call(o, xs, mix_norm_post, ffn_norm_pre)
    h = _mm(hn2, W["w_up"], "mm_up", tn=1408)
    cw = W["conv_w"]
    act = _conv_act_call(h, cw, conv_b)
    ff = _mm(act, W["w_down"], "mm_down", tn=1024, tk=1408)
    loss_row, dy, dff, g_ffn_norm_post = _ffn_out_call(ff, x2, target, ffn_norm_post)
    loss = lax.psum(loss_row[0, 0], ("x", "y", "c"))

    da = _mm(dff, W["w_down"], "mm_down_dx", tb=True, tn=1408)
    g_w_down = _mm_tn(act, dff, "mm_down_dw", tm=1408)
    dgate, dval, dcw_g, dcw_v, dcb_g, dcb_v = _conv_act_bwd_call(da, h, cw, conv_b)
    g_conv_w = jnp.concatenate([dcw_g, dcw_v], axis=1)
    g_conv_b = jnp.concatenate([dcb_g, dcb_v], axis=1)
    dh_g, dh_v = _conv_t_call(dgate, dval, cw)
    dh = jnp.concatenate([dh_g, dh_v], axis=1)
    dhn2 = _mm(dh, W["w_up"], "mm_up_dx", tb=True, tn=1024, tk=1408)
    g_w_up = _mm_tn(hn2, dh, "mm_up_dw")
    dx2, do, g_ffn_norm_pre, g_mix_norm_post = _post_bwd_call(x2, dhn2, dy, o, ffn_norm_pre, mix_norm_post)
    dmerged = _mm(do, W["w_out"], "mm_out_dx", tb=True)
    g_w_out = _mm_tn(merged, do, "mm_out_dw")
    dpa, dps, dl0, dl1, db0, db1 = _merge_bwd_call(dmerged, proj, b_gate, pa, ps)
    g_b_gate = jnp.concatenate([db0, db1], axis=1)
    dattn = _mm(dpa, w_ba_p, "mm_ba_dx", tb=True, out_dtype=BF16)
    g_w_ba = _mm_tn(attn, dpa, "mm_ba_dw").reshape(N_HEADS, LANES, D_MODEL)[:, :V_HEAD].reshape(N_HEADS * V_HEAD, D_MODEL)
    dssm = _mm(dps, W["w_branch_ssm"], "mm_bs_dx", tb=True)
    g_w_bs = _mm_tn(ssm, dps, "mm_bs_dw")

    dy1, g_w_glu, g_b_glu = _glu_bwd_call(_time_perm(dssm, L), y1, w_glu_b, b_glu)
    du_p, dbb_re, dbb_im, dcm_re, dcm_im, da_re, da_im, g_ssm_d = _ssm_bwd_call(
        dy1, u_p, s_re, s_im, a_re, a_im, bb_re, bb_im, cm_re, cm_im, d_skip)
    du = _time_unperm(du_p, L)
    from_bb = lambda m: col(_block_diag_t(m, SSM_GROUP, SSM_STATE).transpose(0, 2, 1))
    dlr, dli, dldt, dbr, dbi = _disc_bwd_call(
        lr_c, li_c, ldt_c, br_c, bi_c, da_re.reshape(SSM_NSTATE, 1), da_im.reshape(SSM_NSTATE, 1), from_bb(dbb_re), from_bb(dbb_im))
    g_c_re = _block_diag_t(dcm_re, SSM_STATE, SSM_GROUP).transpose(0, 2, 1)
    g_c_im = _block_diag_t(dcm_im, SSM_STATE, SSM_GROUP).transpose(0, 2, 1)

    def grad_slices(group, grads):
        return [_to_slices(grads[name], rows, cols, axis) for name, rows, cols, axis in group]

    early_grads = {"w_up": g_w_up, "w_down": g_w_down, "conv_w": g_conv_w, "w_glu": g_w_glu.astype(BF16),
                   "w_branch_attn": g_w_ba, "w_branch_ssm": g_w_bs, "w_out": g_w_out}
    ssm_partials = {"ssm_lambda_re": dlr, "ssm_lambda_im": dli, "ssm_b_re": dbr, "ssm_b_im": dbi,
                    "ssm_c_re": g_c_re, "ssm_c_im": g_c_im, "ssm_d": g_ssm_d}
    ssm_shapes = [(name, shp) for name, shp in SMALL if name in ssm_partials]
    dq, dkv, *landed = _attn_bwd_call(
        q_r, kv_r, attn, dattn, lse, grad_slices(GRADS_EARLY, early_grads),
        [ssm_partials[name].reshape(-1, LANES) if len(shp) == 3 else ssm_partials[name].reshape((1,) + shp)
         for name, shp in ssm_shapes])
    received_early = landed[:len(GRADS_EARLY)]
    ssm_all = {name: got.reshape((N_DEV, 1) + shp) for (name, shp), got in zip(ssm_shapes, landed[len(GRADS_EARLY):])}
    dq_p, dkv_p, dkr_p = _mla_prep_bwd_call(dq, dkv, cosf, sinf)
    dqn = _mm(dq_p, w_uq_p, "mm_uq_dx", tb=True)
    g_w_uq = _head_unpad_cols(_mm_tn(qn, dq_p, "mm_uq_dw"), QK_HEAD)
    dckvn = _mm(dkv_p, w_kv_p, "mm_ukv_dx", tb=True)
    g_w_kv = _mm_tn(ckvn, dkv_p, "mm_ukv_dw").reshape(KV_RANK, N_HEADS, 2, LANES)
    g_w_uk = g_w_kv[:, :, 0, :QK_NOPE].reshape(KV_RANK, N_HEADS * QK_NOPE)
    g_w_uv = g_w_kv[:, :, 1, :V_HEAD].reshape(KV_RANK, N_HEADS * V_HEAD)
    dcqkv, g_q_norm, g_kv_norm = _mla_norms_bwd_call(proj, dqn, dckvn, q_norm, kv_norm)
    dproj = jnp.concatenate([dcqkv, dkr_p, du.astype(BF16), dl0, dl1], axis=1)
    g_w_in_p = _mm_tn(hn1, dproj, "mm_in_dw", tk=1024)
    g_w_in = jnp.concatenate([g_w_in_p[:, :640], g_w_in_p[:, 640 + QK_NOPE:640 + QK_HEAD], g_w_in_p[:, 768:]], axis=1)
    late_grads = {"w_in": g_w_in, "w_uq": g_w_uq, "w_uk": g_w_uk, "w_uv": g_w_uv}
    dhn1, *received_late = _mm(dproj, w_in_p, "mm_in_dx", tb=True, tk=1664, exchange=grad_slices(GRADS_LATE, late_grads))
    grad_x, g_mix_norm_pre = _pre_bwd_call(xs, dhn1, dx2, mix_norm_pre)

    results = {}
    for group, received in ((GRADS_EARLY, received_early), (GRADS_LATE, received_late)):
        for (name, _, _, _), rec in zip(group, received):
            results[name] = _adam_call(rec[:, None], given[name], given["m_" + name], given["v_" + name], "adam_" + name)

    vec_grads = {"mix_norm_pre": g_mix_norm_pre, "q_norm": g_q_norm, "kv_norm": g_kv_norm,
                 "ssm_log_dt": jnp.sum(dldt.reshape(SSM_GROUPS, SSM_STATE), axis=1),
                 "b_glu": g_b_glu, "b_gate": g_b_gate, "mix_norm_post": g_mix_norm_post,
                 "ffn_norm_pre": g_ffn_norm_pre, "conv_b": g_conv_b, "ffn_norm_post": g_ffn_norm_post}
    vec_names = [name for name, _ in SMALL if name in vec_grads]
    width = max(shp[0] for name, shp in SMALL if name in vec_grads)
    rows = [jnp.pad(vec_grads[name].reshape(1, -1), ((0, 0), (0, width - vec_grads[name].size))) for name in vec_names]
    rows.append(jnp.zeros((-len(rows) % 8, width), F32))
    rows_all, = _all_gather_call([jnp.concatenate(rows, axis=0)], "gather_small_grads", direct=True)
    wmv = lambda name: (given[name], given["m_" + name], given["v_" + name])
    few = ["ssm_lambda_re", "ssm_lambda_im", "ssm_d"]
    for name, res in zip(vec_names + few, _adam_small_call(
            rows_all, [wmv(n) for n in vec_names], [ssm_all[n] for n in few], [wmv(n) for n in few])):
        results[name] = res
    for name in ("ssm_b_re", "ssm_b_im", "ssm_c_re", "ssm_c_im"):
        results[name] = _adam_call(ssm_all[name], *wmv(name), "adam_" + name)

    order = ["mix_norm_pre", "w_in", "q_norm", "w_uq", "kv_norm", "w_uk", "w_uv", "ssm_lambda_re", "ssm_lambda_im",
             "ssm_log_dt", "ssm_b_re", "ssm_b_im", "ssm_c_re", "ssm_c_im", "ssm_d", "w_glu", "b_glu", "w_branch_attn",
             "w_branch_ssm", "b_gate", "w_out", "mix_norm_post", "ffn_norm_pre", "w_up", "conv_w", "conv_b", "w_down",
             "ffn_norm_post"]
    outs = [loss, grad_x[None]]
    for kind in range(4):
        outs += [results[name][kind] for name in order]
    return tuple(outs)
```

```python
import math

import jax
import jax.numpy as jnp
from jax import lax
from jax.experimental import pallas as pl
from jax.experimental.pallas import tpu as pltpu

F32 = jnp.float32
BF16 = jnp.bfloat16
MESH_ID = pl.DeviceIdType.MESH

N_DEV = 8
LANES = 128
D_MODEL = 1024
N_HEADS = 8
QK_NOPE = 64
QK_ROPE = 32
QK_HEAD = QK_NOPE + QK_ROPE
V_HEAD = 64
Q_RANK = 384
KV_RANK = 256
ROPE_THETA = 10000.0
SSM_WIDTH = 512
SSM_GROUP = 16
SSM_GROUPS = 32
SSM_STATE = 64
SSM_NSTATE = SSM_GROUPS * SSM_STATE
SSM_CHUNKS = 4
D_FF = 2816
EPS = 1e-6
ADAM_LR, ADAM_B1, ADAM_B2, ADAM_EPS, ADAM_WD, ADAM_STEP = 0.001, 0.9, 0.999, 1e-08, 0.01, 10

P_CQ, P_CKV, P_KR, P_U, P_GATE = 0, 384, 640, 768, 1280
P_IN = P_GATE + 2 * D_MODEL
HEAD_PAD = N_HEADS * LANES

PACK_ROWS = 1024
VMEM_BIG = 52 * 1024 * 1024

_GELU_C0 = math.sqrt(2.0 / math.pi)
_GELU_C1 = 0.044715
NEG = -1e30


def _fit(n, pref, mult=LANES):
    if n <= pref:
        return n
    t = (pref // mult) * mult
    while t > 0 and n % t:
        t -= mult
    assert t > 0, (n, pref, mult)
    return t


def _gelu(x):
    return 0.5 * x * (1.0 + jnp.tanh(_GELU_C0 * (x + _GELU_C1 * x * x * x)))


def _gelu_grad(x):
    x2 = x * x
    t = jnp.tanh(_GELU_C0 * x * (1.0 + _GELU_C1 * x2))
    return 0.5 * (1.0 + t) + 0.5 * x * (1.0 - t * t) * _GELU_C0 * (1.0 + 3.0 * _GELU_C1 * x2)


def _sigmoid(x):
    return 1.0 / (1.0 + jnp.exp(-x))


def _dot(a, b, dims):
    return lax.dot_general(a, b, (dims, ((), ())), preferred_element_type=F32)


NN = ((1,), (0,))
NT = ((1,), (1,))
TN = ((0,), (0,))


def _params(*sem, vmem=None):
    return pltpu.CompilerParams(dimension_semantics=tuple(sem), vmem_limit_bytes=vmem)


def _mm(a, b, name, tb=False, out_dtype=F32, tm=1024, tn=512, tk=1024, exchange=(), gather=()):
    M, K = a.shape
    if tb:
        N, K2 = b.shape
    else:
        K2, N = b.shape
    assert K == K2, (a.shape, b.shape, tb)
    tm, tn, tk = _fit(M, tm), _fit(N, tn), _fit(K, tk)
    nk = K // tk
    grid = (M // tm, N // tn, nk)
    steps = grid[0] * grid[1] * grid[2]
    dims = NT if tb else NN
    moved = list(exchange) + list(gather)
    n = len(moved)
    assert not (exchange and gather)

    def body(a_ref, b_ref, *refs):
        o_ref, scratch = refs[n], refs[2 * n + 1:]
        step = (pl.program_id(0) * grid[1] + pl.program_id(1)) * grid[2] + pl.program_id(2)
        if exchange:
            start, finish = _exchange_phases(refs[:n], refs[n + 1:2 * n + 1], *scratch[-3:])
            pl.when(step == 0)(start)
        if gather:
            start, forward, finish = _gather_phases(refs[:n], refs[n + 1:2 * n + 1], *scratch[-3:])
            pl.when(step == 0)(start)
            pl.when(step == steps // 2)(forward)
        part = _dot(a_ref[...].astype(BF16), b_ref[...].astype(BF16), dims)
        if nk == 1:
            o_ref[...] = part.astype(out_dtype)
        else:
            acc_ref = scratch[0]
            k = pl.program_id(2)

            @pl.when(k == 0)
            def _():
                acc_ref[...] = part

            @pl.when(k > 0)
            def _():
                acc_ref[...] += part

            @pl.when(k == nk - 1)
            def _():
                o_ref[...] = acc_ref[...].astype(out_dtype)
        if n:
            pl.when(step == steps - 1)(finish)

    a_spec = pl.BlockSpec((tm, tk), lambda i, j, k: (i, k))
    b_spec = pl.BlockSpec((tn, tk), lambda i, j, k: (j, k)) if tb else pl.BlockSpec((tk, tn), lambda i, j, k: (k, j))
    landed = [jax.ShapeDtypeStruct(p.shape, p.dtype) for p in exchange]
    landed += [jax.ShapeDtypeStruct((N_DEV,) + p.shape, p.dtype) for p in gather]
    out = pl.pallas_call(
        body, name=name, grid=grid,
        in_specs=[a_spec, b_spec] + [ANY_SPEC] * n,
        out_specs=[pl.BlockSpec((tm, tn), lambda i, j, k: (i, j))] + [ANY_SPEC] * n,
        out_shape=[jax.ShapeDtypeStruct((M, N), out_dtype)] + landed,
        scratch_shapes=([] if nk == 1 else [pltpu.VMEM((tm, tn), F32)]) + (_comm_sems(n) if n else []),
        compiler_params=_params(*(("arbitrary",) * 3 if n else ("parallel", "parallel", "arbitrary")), vmem=VMEM_BIG),
    )(a, b, *moved)
    return out if n else out[0]


TN_CHUNK = 512


def _mm_tn(a, b, name, tm=512, tk=512):
    K, M = a.shape
    K2, N = b.shape
    assert K == K2, (a.shape, b.shape)
    tm, tk, cn = _fit(M, tm), _fit(K, tk), _fit(N, TN_CHUNK)
    nk = K // tk

    def body(a_ref, b_ref, o_ref, acc_ref):
        k = pl.program_id(1)

        @pl.when(k == 0)
        def _():
            acc_ref[...] = jnp.zeros((tm, N), F32)

        at = a_ref[...].astype(BF16).T
        for c in range(N // cn):
            cols = slice(c * cn, (c + 1) * cn)
            acc_ref[:, cols] += _dot(at, b_ref[:, cols].astype(BF16), NN)

        @pl.when(k == nk - 1)
        def _():
            o_ref[...] = acc_ref[...].astype(BF16)

    return pl.pallas_call(
        body, name=name, grid=(M // tm, nk),
        in_specs=[pl.BlockSpec((tk, tm), lambda i, k: (k, i)), pl.BlockSpec((tk, N), lambda i, k: (k, 0))],
        out_specs=pl.BlockSpec((tm, N), lambda i, k: (i, 0)),
        out_shape=jax.ShapeDtypeStruct((M, N), BF16),
        scratch_shapes=[pltpu.VMEM((tm, N), F32)],
        compiler_params=_params("parallel", "arbitrary", vmem=VMEM_BIG))(a, b)


def _row(tl, n, col=0):
    return pl.BlockSpec((tl, n), lambda i: (i, col))


def _full(shape):
    return pl.BlockSpec(shape, lambda i: (0,) * len(shape))


def _rms(x, g):
    r = lax.rsqrt(jnp.mean(x * x, axis=-1, keepdims=True) + EPS)
    return x * r * g


def _rms_bwd(x, g, dy):
    n = x.shape[-1]
    r = lax.rsqrt(jnp.mean(x * x, axis=-1, keepdims=True) + EPS)
    gy = dy * g
    dx = r * gy - x * (r * r * r * (1.0 / n)) * jnp.sum(x * gy, axis=-1, keepdims=True)
    return dx, jnp.sum(dy * x * r, axis=0, keepdims=True)


def _acc(ref, first, val):
    @pl.when(first)
    def _():
        ref[...] = val

    @pl.when(jnp.logical_not(first))
    def _():
        ref[...] += val


def _rms_fwd_call(x, g, name):
    L, n = x.shape
    tl = _fit(L, 512)

    def body(x_ref, g_ref, o_ref):
        o_ref[...] = _rms(x_ref[...], g_ref[...]).astype(BF16)

    return pl.pallas_call(
        body, name=name, grid=(L // tl,), in_specs=[_row(tl, n), _full((1, n))], out_specs=_row(tl, n),
        out_shape=jax.ShapeDtypeStruct((L, n), BF16), compiler_params=_params("parallel"))(x, g)


def _mla_norms_call(proj, q_norm, kv_norm):
    L = proj.shape[0]
    tl = _fit(L, 512)

    def body(p_ref, gq_ref, gk_ref, qn_ref, kn_ref):
        p = p_ref[...]
        qn_ref[...] = _rms(p[:, P_CQ:P_CKV], gq_ref[...]).astype(BF16)
        kn_ref[...] = _rms(p[:, P_CKV:P_KR], gk_ref[...]).astype(BF16)

    return pl.pallas_call(
        body, name="mla_norms", grid=(L // tl,),
        in_specs=[_row(tl, P_KR), _full((1, Q_RANK)), _full((1, KV_RANK))],
        out_specs=[_row(tl, Q_RANK), _row(tl, KV_RANK)],
        out_shape=[jax.ShapeDtypeStruct((L, Q_RANK), BF16), jax.ShapeDtypeStruct((L, KV_RANK), BF16)],
        compiler_params=_params("parallel"))(proj, q_norm, kv_norm)


def _rope_lanes(shape):
    lane = lax.broadcasted_iota(jnp.int32, shape, 1)
    return lane, jnp.logical_and(lane >= QK_NOPE, lane < QK_HEAD)


def _rope_apply(x, cosf, sinf, lane):
    rot = jnp.where(lane < QK_NOPE + QK_ROPE // 2, -pltpu.roll(x, LANES - QK_ROPE // 2, 1), pltpu.roll(x, QK_ROPE // 2, 1))
    return x * cosf + rot * sinf


def _rope_apply_t(dy, cosf, sinf, lane, is_rope):
    g = dy * sinf
    rot_t = jnp.where(lane < QK_NOPE + QK_ROPE // 2, pltpu.roll(g, LANES - QK_ROPE // 2, 1), -pltpu.roll(g, QK_ROPE // 2, 1))
    return dy * cosf + jnp.where(is_rope, rot_t, 0.0)


def _mla_prep_call(q_pad, kv_pad, proj, pos_col, inv_freq):
    L = q_pad.shape[0]
    tl = _fit(L, 512)

    def body(q_ref, kv_ref, kr_ref, pos_ref, f_ref, qo_ref, kvo_ref, cos_ref, sin_ref):
        lane, is_rope = _rope_lanes((tl, LANES))
        ang = pos_ref[...] * f_ref[...]
        cosf = jnp.where(is_rope, jnp.cos(ang), jnp.where(lane < QK_NOPE, 1.0, 0.0))
        sinf = jnp.where(is_rope, jnp.sin(ang), 0.0)
        cos_ref[...] = cosf
        sin_ref[...] = sinf
        kr = _rope_apply(kr_ref[...], cosf, sinf, lane)
        for h in range(N_HEADS):
            qh = _rope_apply(q_ref[:, h * LANES:(h + 1) * LANES], cosf, sinf, lane)
            qo_ref[:, h * LANES:(h + 1) * LANES] = (qh * Q_PRESCALE).astype(BF16)
            kvo_ref[:, 2 * h * LANES:(2 * h + 1) * LANES] = (kv_ref[:, 2 * h * LANES:(2 * h + 1) * LANES] + kr).astype(BF16)
            vh = jnp.where(lane == V_HEAD, 1.0, kv_ref[:, (2 * h + 1) * LANES:(2 * h + 2) * LANES])
            kvo_ref[:, (2 * h + 1) * LANES:(2 * h + 2) * LANES] = vh.astype(BF16)

    return pl.pallas_call(
        body, name="mla_prep", grid=(L // tl,),
        in_specs=[_row(tl, HEAD_PAD), _row(tl, 2 * HEAD_PAD), _row(tl, LANES, P_KR // LANES), _row(tl, 1), _full((1, LANES))],
        out_specs=[_row(tl, HEAD_PAD), _row(tl, 2 * HEAD_PAD), _row(tl, LANES), _row(tl, LANES)],
        out_shape=[jax.ShapeDtypeStruct((L, HEAD_PAD), BF16), jax.ShapeDtypeStruct((L, 2 * HEAD_PAD), BF16),
                   jax.ShapeDtypeStruct((L, LANES), F32), jax.ShapeDtypeStruct((L, LANES), F32)],
        compiler_params=_params("parallel"))(q_pad, kv_pad, proj, pos_col, inv_freq)


def _mla_prep_bwd_call(dq, dkv, cosf, sinf):
    L = dq.shape[0]
    tl = _fit(L, 512)

    def body(dq_ref, dkv_ref, cos_ref, sin_ref, dqo_ref, dkvo_ref, dkr_ref):
        lane, is_rope = _rope_lanes((tl, LANES))
        cosf, sinf = cos_ref[...], sin_ref[...]
        dk_sum = jnp.zeros((tl, LANES), F32)
        for h in range(N_HEADS):
            dqo_ref[:, h * LANES:(h + 1) * LANES] = _rope_apply_t(dq_ref[:, h * LANES:(h + 1) * LANES], cosf, sinf, lane, is_rope).astype(BF16)
            dk_sum = dk_sum + dkv_ref[:, 2 * h * LANES:(2 * h + 1) * LANES]
        dkvo_ref[...] = dkv_ref[...].astype(BF16)
        dkr_ref[...] = _rope_apply_t(dk_sum, cosf, sinf, lane, is_rope).astype(BF16)

    return pl.pallas_call(
        body, name="mla_prep_bwd", grid=(L // tl,),
        in_specs=[_row(tl, HEAD_PAD), _row(tl, 2 * HEAD_PAD), _row(tl, LANES), _row(tl, LANES)],
        out_specs=[_row(tl, HEAD_PAD), _row(tl, 2 * HEAD_PAD), _row(tl, LANES)],
        out_shape=[jax.ShapeDtypeStruct((L, HEAD_PAD), BF16), jax.ShapeDtypeStruct((L, 2 * HEAD_PAD), BF16),
                   jax.ShapeDtypeStruct((L, LANES), BF16)],
        compiler_params=_params("parallel"))(dq, dkv, cosf, sinf)


def _mla_norms_bwd_call(proj, dqn, dkn, q_norm, kv_norm):
    L = proj.shape[0]
    tl = _fit(L, 512)

    def body(p_ref, dqn_ref, dkn_ref, gq_ref, gk_ref, d_ref, dgq_ref, dgk_ref):
        first = pl.program_id(0) == 0
        p = p_ref[...]
        dq, dgq = _rms_bwd(p[:, P_CQ:P_CKV], gq_ref[...], dqn_ref[...])
        dk, dgk = _rms_bwd(p[:, P_CKV:P_KR], gk_ref[...], dkn_ref[...])
        d_ref[:, P_CQ:P_CKV] = dq.astype(BF16)
        d_ref[:, P_CKV:P_KR] = dk.astype(BF16)
        _acc(dgq_ref, first, dgq)
        _acc(dgk_ref, first, dgk)

    return pl.pallas_call(
        body, name="mla_norms_bwd", grid=(L // tl,),
        in_specs=[_row(tl, P_KR), _row(tl, Q_RANK), _row(tl, KV_RANK), _full((1, Q_RANK)), _full((1, KV_RANK))],
        out_specs=[_row(tl, P_KR), _full((1, Q_RANK)), _full((1, KV_RANK))],
        out_shape=[jax.ShapeDtypeStruct((L, P_KR), BF16), jax.ShapeDtypeStruct((1, Q_RANK), F32),
                   jax.ShapeDtypeStruct((1, KV_RANK), F32)],
        compiler_params=_params("arbitrary"))(proj, dqn, dkn, q_norm, kv_norm)


GATE_TILE = 256


def _merge_call(proj, b_gate, pa, ps):
    L = proj.shape[0]
    tl = _fit(L, 512)
    nc = D_MODEL // GATE_TILE
    g0, g1 = P_GATE // GATE_TILE, (P_GATE + D_MODEL) // GATE_TILE

    def body(l0_ref, l1_ref, b0_ref, b1_ref, pa_ref, ps_ref, o_ref):
        s0 = _sigmoid(l0_ref[...] + b0_ref[...])
        s1 = _sigmoid(l1_ref[...] + b1_ref[...])
        o_ref[...] = (s0 * pa_ref[...] + s1 * ps_ref[...]).astype(BF16)

    blk = lambda off: pl.BlockSpec((tl, GATE_TILE), lambda i, j: (i, off + j))
    bias = lambda off: pl.BlockSpec((1, GATE_TILE), lambda i, j: (0, off + j))
    return pl.pallas_call(
        body, name="merge", grid=(L // tl, nc),
        in_specs=[blk(g0), blk(g1), bias(0), bias(nc), blk(0), blk(0)],
        out_specs=blk(0), out_shape=jax.ShapeDtypeStruct((L, D_MODEL), BF16),
        compiler_params=_params("parallel", "parallel"))(proj, proj, b_gate, b_gate, pa, ps)


def _merge_bwd_call(dm, proj, b_gate, pa, ps):
    L = proj.shape[0]
    tl = _fit(L, 512)
    nc = D_MODEL // GATE_TILE
    g0, g1 = P_GATE // GATE_TILE, (P_GATE + D_MODEL) // GATE_TILE

    def body(dm_ref, l0_ref, l1_ref, b0_ref, b1_ref, pa_ref, ps_ref, dpa_ref, dps_ref, dl0_ref, dl1_ref, db0_ref, db1_ref):
        first = pl.program_id(1) == 0
        dm_ = dm_ref[...]
        s0 = _sigmoid(l0_ref[...] + b0_ref[...])
        s1 = _sigmoid(l1_ref[...] + b1_ref[...])
        dpa_ref[...] = (dm_ * s0).astype(BF16)
        dps_ref[...] = (dm_ * s1).astype(BF16)
        dl0 = dm_ * pa_ref[...] * s0 * (1.0 - s0)
        dl1 = dm_ * ps_ref[...] * s1 * (1.0 - s1)
        dl0_ref[...] = dl0.astype(BF16)
        dl1_ref[...] = dl1.astype(BF16)
        _acc(db0_ref, first, jnp.sum(dl0, axis=0, keepdims=True))
        _acc(db1_ref, first, jnp.sum(dl1, axis=0, keepdims=True))

    blk = lambda off: pl.BlockSpec((tl, GATE_TILE), lambda j, i: (i, off + j))
    bias = lambda off: pl.BlockSpec((1, GATE_TILE), lambda j, i: (0, off + j))
    act = jax.ShapeDtypeStruct((L, D_MODEL), BF16)
    vec = jax.ShapeDtypeStruct((1, D_MODEL), F32)
    return pl.pallas_call(
        body, name="merge_bwd", grid=(nc, L // tl),
        in_specs=[blk(0), blk(g0), blk(g1), bias(0), bias(nc), blk(0), blk(0)],
        out_specs=[blk(0), blk(0), blk(0), blk(0), bias(0), bias(0)],
        out_shape=[act, act, act, act, vec, vec],
        compiler_params=_params("parallel", "arbitrary"))(dm, proj, proj, b_gate, b_gate, pa, ps)


def _post_mix_call(o, x, g_post, g_fpre):
    L, n = x.shape
    tl = _fit(L, 512)

    def body(o_ref, x_ref, gp_ref, gf_ref, x2_ref, hn_ref):
        x2 = x_ref[...] + _rms(o_ref[...], gp_ref[...])
        x2_ref[...] = x2
        hn_ref[...] = _rms(x2, gf_ref[...]).astype(BF16)

    return pl.pallas_call(
        body, name="post_mix", grid=(L // tl,),
        in_specs=[_row(tl, n), _row(tl, n), _full((1, n)), _full((1, n))],
        out_specs=[_row(tl, n), _row(tl, n)],
        out_shape=[jax.ShapeDtypeStruct((L, n), F32), jax.ShapeDtypeStruct((L, n), BF16)],
        compiler_params=_params("parallel"))(o, x, g_post, g_fpre)


def _ffn_out_call(ff, x2, target, g_fpost):
    L, n = x2.shape
    tl = _fit(L, 512)

    def body(ff_ref, x2_ref, t_ref, g_ref, loss_ref, dy_ref, dff_ref, dg_ref):
        first = pl.program_id(0) == 0
        ff_ = ff_ref[...]
        err = x2_ref[...] + _rms(ff_, g_ref[...]) - t_ref[...]
        part = 0.5 * jnp.sum(jnp.sum(err * err, axis=-1, keepdims=True) * (1.0 / n), axis=0, keepdims=True)
        dy = err * (1.0 / n)
        dy_ref[...] = dy
        dff, dg = _rms_bwd(ff_, g_ref[...], dy)
        dff_ref[...] = dff.astype(BF16)
        _acc(loss_ref, first, jnp.broadcast_to(part, (1, LANES)))
        _acc(dg_ref, first, dg)

    return pl.pallas_call(
        body, name="ffn_out", grid=(L // tl,),
        in_specs=[_row(tl, n), _row(tl, n), _row(tl, n), _full((1, n))],
        out_specs=[_full((1, LANES)), _row(tl, n), _row(tl, n), _full((1, n))],
        out_shape=[jax.ShapeDtypeStruct((1, LANES), F32), jax.ShapeDtypeStruct((L, n), F32),
                   jax.ShapeDtypeStruct((L, n), BF16), jax.ShapeDtypeStruct((1, n), F32)],
        compiler_params=_params("arbitrary"))(ff, x2, target, g_fpost)


def _post_bwd_call(x2, dhn2, dy, o, g_fpre, g_post):
    L, n = x2.shape
    tl = _fit(L, 512)

    def body(x2_ref, dh_ref, dy_ref, o_ref, gf_ref, gp_ref, dx2_ref, do_ref, dgf_ref, dgp_ref):
        first = pl.program_id(0) == 0
        d1, dgf = _rms_bwd(x2_ref[...], gf_ref[...], dh_ref[...])
        dx2 = dy_ref[...] + d1
        dx2_ref[...] = dx2
        do, dgp = _rms_bwd(o_ref[...], gp_ref[...], dx2)
        do_ref[...] = do.astype(BF16)
        _acc(dgf_ref, first, dgf)
        _acc(dgp_ref, first, dgp)

    return pl.pallas_call(
        body, name="post_bwd", grid=(L // tl,),
        in_specs=[_row(tl, n), _row(tl, n), _row(tl, n), _row(tl, n), _full((1, n)), _full((1, n))],
        out_specs=[_row(tl, n), _row(tl, n), _full((1, n)), _full((1, n))],
        out_shape=[jax.ShapeDtypeStruct((L, n), F32), jax.ShapeDtypeStruct((L, n), BF16),
                   jax.ShapeDtypeStruct((1, n), F32), jax.ShapeDtypeStruct((1, n), F32)],
        compiler_params=_params("arbitrary"))(x2, dhn2, dy, o, g_fpre, g_post)


def _pre_bwd_call(x, dhn1, dx2, g_pre):
    L, n = x.shape
    tl = _fit(L, 512)

    def body(x_ref, dh_ref, dx2_ref, g_ref, dx_ref, dg_ref):
        first = pl.program_id(0) == 0
        d1, dg = _rms_bwd(x_ref[...], g_ref[...], dh_ref[...])
        dx_ref[...] = dx2_ref[...] + d1
        _acc(dg_ref, first, dg)

    return pl.pallas_call(
        body, name="pre_bwd", grid=(L // tl,),
        in_specs=[_row(tl, n), _row(tl, n), _row(tl, n), _full((1, n))],
        out_specs=[_row(tl, n), _full((1, n))],
        out_shape=[jax.ShapeDtypeStruct((L, n), F32), jax.ShapeDtypeStruct((1, n), F32)],
        compiler_params=_params("arbitrary"))(x, dhn1, dx2, g_pre)


CONV_TILE = 256
HALO = 16


def _conv3(w, b, x0, x1, x2):
    return b + w[2:3] * x0 + w[1:2] * x1 + w[0:1] * x2


def _down(x, by):
    return pltpu.roll(x, by, 0)


def _edge_down(edge, before, by):
    r = lax.broadcasted_iota(jnp.int32, edge.shape, 0)
    return jnp.where(r < by, pltpu.roll(before, by, 0), pltpu.roll(edge, by, 0))


def _edge_up(edge, after, by):
    r = lax.broadcasted_iota(jnp.int32, edge.shape, 0)
    return jnp.where(r >= HALO - by, pltpu.roll(after, HALO - by, 0), pltpu.roll(edge, HALO - by, 0))


def _gated(w_g, b_g, w_v, b_v, hg, hv, g1, g2, v1, v2):
    return _conv3(w_g, b_g, hg, g1, g2), _conv3(w_v, b_v, hv, v1, v2)


def _conv_specs(L, tl, nc, rows_inner):
    nh = tl // HALO
    if rows_inner:
        ij = lambda f: (lambda j, i: f(i, j))
    else:
        ij = lambda f: f
    cur = lambda off: pl.BlockSpec((tl, CONV_TILE), ij(lambda i, j: (i, off + j)))
    prev = lambda off: pl.BlockSpec((HALO, CONV_TILE), ij(lambda i, j: (jnp.maximum(i * nh - 1, 0), off + j)))
    nxt = lambda off: pl.BlockSpec((HALO, CONV_TILE), ij(lambda i, j: (jnp.minimum((i + 1) * nh, L // HALO - 1), off + j)))
    par = lambda rows, off: pl.BlockSpec((rows, CONV_TILE), ij(lambda i, j: (0, off + j)))
    return cur, prev, nxt, par


def _conv_act_call(h, conv_w, conv_b):
    L = h.shape[0]
    tl = _fit(L, 512)
    nc = D_FF // CONV_TILE
    cur, prev, _, par = _conv_specs(L, tl, nc, False)

    def body(hg_ref, hv_ref, pg_ref, pv_ref, wg_ref, wv_ref, bg_ref, bv_ref, a_ref):
        not_first = (pl.program_id(0) > 0).astype(F32)
        par = (wg_ref[...], bg_ref[...], wv_ref[...], bv_ref[...])
        hg, hv = hg_ref[...], hv_ref[...]
        gate, val = _gated(*par, hg, hv, _down(hg, 1), _down(hg, 2), _down(hv, 1), _down(hv, 2))
        a_ref[...] = (_gelu(gate) * val).astype(BF16)
        eg, ev, bg, bv = hg[:HALO], hv[:HALO], pg_ref[...] * not_first, pv_ref[...] * not_first
        gate, val = _gated(*par, eg, ev, _edge_down(eg, bg, 1), _edge_down(eg, bg, 2),
                           _edge_down(ev, bv, 1), _edge_down(ev, bv, 2))
        a_ref[:HALO, :] = (_gelu(gate) * val).astype(BF16)

    return pl.pallas_call(
        body, name="conv_act", grid=(L // tl, nc),
        in_specs=[cur(0), cur(nc), prev(0), prev(nc), par(3, 0), par(3, nc), par(1, 0), par(1, nc)],
        out_specs=cur(0), out_shape=jax.ShapeDtypeStruct((L, D_FF), BF16),
        compiler_params=_params("parallel", "parallel"))(h, h, h, h, conv_w, conv_w, conv_b, conv_b)


def _conv_act_bwd_call(da, h, conv_w, conv_b):
    L = h.shape[0]
    tl = _fit(L, 512)
    nc = D_FF // CONV_TILE
    cur, prev, _, par = _conv_specs(L, tl, nc, True)

    def body(da_ref, hg_ref, hv_ref, pg_ref, pv_ref, wg_ref, wv_ref, bg_ref, bv_ref,
             dg_ref, dv_ref, dwg_ref, dwv_ref, dbg_ref, dbv_ref):
        first = pl.program_id(1) == 0
        not_first = (pl.program_id(1) > 0).astype(F32)
        par = (wg_ref[...], bg_ref[...], wv_ref[...], bv_ref[...])
        col = lambda t: jnp.sum(t, axis=0, keepdims=True)

        def grads(da_, hg, hv, g1, g2, v1, v2):
            gate, val = _gated(*par, hg, hv, g1, g2, v1, v2)
            dgate = da_ * val * _gelu_grad(gate)
            dval = da_ * _gelu(gate)
            sums = (jnp.concatenate([col(dgate * g2), col(dgate * g1), col(dgate * hg)], axis=0),
                    jnp.concatenate([col(dval * v2), col(dval * v1), col(dval * hv)], axis=0), col(dgate), col(dval))
            return dgate, dval, sums

        da_, hg, hv = da_ref[...], hg_ref[...], hv_ref[...]
        shifted = (_down(hg, 1), _down(hg, 2), _down(hv, 1), _down(hv, 2))
        dgate, dval, whole = grads(da_, hg, hv, *shifted)
        dg_ref[...] = dgate.astype(BF16)
        dv_ref[...] = dval.astype(BF16)
        edge = lambda t: t[:HALO]
        _, _, wrapped = grads(edge(da_), edge(hg), edge(hv), *[edge(s) for s in shifted])
        eg, ev, bg, bv = edge(hg), edge(hv), pg_ref[...] * not_first, pv_ref[...] * not_first
        dgate, dval, fixed = grads(edge(da_), eg, ev, _edge_down(eg, bg, 1), _edge_down(eg, bg, 2),
                                   _edge_down(ev, bv, 1), _edge_down(ev, bv, 2))
        dg_ref[:HALO, :] = dgate.astype(BF16)
        dv_ref[:HALO, :] = dval.astype(BF16)
        for ref, a, b, c in zip((dwg_ref, dwv_ref, dbg_ref, dbv_ref), whole, wrapped, fixed):
            _acc(ref, first, a - b + c)

    act = jax.ShapeDtypeStruct((L, D_FF), BF16)
    w3 = jax.ShapeDtypeStruct((3, D_FF), F32)
    w1 = jax.ShapeDtypeStruct((1, D_FF), F32)
    return pl.pallas_call(
        body, name="conv_act_bwd", grid=(nc, L // tl),
        in_specs=[cur(0), cur(0), cur(nc), prev(0), prev(nc), par(3, 0), par(3, nc), par(1, 0), par(1, nc)],
        out_specs=[cur(0), cur(0), par(3, 0), par(3, 0), par(1, 0), par(1, 0)],
        out_shape=[act, act, w3, w3, w1, w1],
        compiler_params=_params("parallel", "arbitrary"))(da, h, h, h, h, conv_w, conv_w, conv_b, conv_b)


def _conv_t_call(dgate, dval, conv_w):
    L = dgate.shape[0]
    tl = _fit(L, 512)
    nc = D_FF // CONV_TILE
    nh = tl // HALO

    def body(dg_ref, dv_ref, ng_ref, nv_ref, w_ref, o_ref):
        not_last = (pl.program_id(0) < L // tl - 1).astype(F32)

        def emit(d_ref, n_ref):
            c = d_ref[...].astype(F32)
            w = w_ref[...]
            o_ref[...] = _conv3(w, 0.0, c, pltpu.roll(c, tl - 1, 0), pltpu.roll(c, tl - 2, 0)).astype(BF16)
            edge, after = c[tl - HALO:], n_ref[...].astype(F32) * not_last
            o_ref[tl - HALO:, :] = _conv3(w, 0.0, edge, _edge_up(edge, after, 1), _edge_up(edge, after, 2)).astype(BF16)

        pl.when(pl.program_id(1) < nc)(lambda: emit(dg_ref, ng_ref))
        pl.when(pl.program_id(1) >= nc)(lambda: emit(dv_ref, nv_ref))

    gate_col = lambda j: jnp.minimum(j, nc - 1)
    val_col = lambda j: jnp.maximum(j - nc, 0)
    after_row = lambda i: jnp.minimum((i + 1) * nh, L // HALO - 1)
    tile = lambda col: pl.BlockSpec((tl, CONV_TILE), lambda i, j: (i, col(j)))
    after = lambda col: pl.BlockSpec((HALO, CONV_TILE), lambda i, j: (after_row(i), col(j)))
    return pl.pallas_call(
        body, name="conv_t", grid=(L // tl, 2 * nc),
        in_specs=[tile(gate_col), tile(val_col), after(gate_col), after(val_col), pl.BlockSpec((3, CONV_TILE), lambda i, j: (0, j))],
        out_specs=pl.BlockSpec((tl, CONV_TILE), lambda i, j: (i, j)),
        out_shape=jax.ShapeDtypeStruct((L, 2 * D_FF), BF16),
        compiler_params=_params("parallel", "parallel"))(dgate, dval, dgate, dval, conv_w)


def _glu_call(y1, w_glu, b_glu):
    L, n = y1.shape
    tl = _fit(L, 512)

    def body(y_ref, w_ref, b_ref, o_ref):
        y2 = _gelu(y_ref[...])
        z = _dot(y2.astype(BF16), w_ref[...], NN) + b_ref[...]
        o_ref[...] = (y2 * _sigmoid(z)).astype(BF16)

    return pl.pallas_call(
        body, name="glu", grid=(L // tl,), in_specs=[_row(tl, n), _full((n, n)), _full((1, n))],
        out_specs=_row(tl, n), out_shape=jax.ShapeDtypeStruct((L, n), BF16),
        compiler_params=_params("parallel"))(y1, w_glu, b_glu)


def _glu_bwd_call(dout, y1, w_glu, b_glu):
    L, n = y1.shape
    tl = _fit(L, 512)

    def body(do_ref, y_ref, w_ref, b_ref, dy_ref, dw_ref, db_ref):
        first = pl.program_id(0) == 0
        y1_ = y_ref[...]
        y2 = _gelu(y1_)
        y2b = y2.astype(BF16)
        w = w_ref[...]
        sg = _sigmoid(_dot(y2b, w, NN) + b_ref[...])
        dout_ = do_ref[...].astype(F32)
        dz = dout_ * y2 * sg * (1.0 - sg)
        dzb = dz.astype(BF16)
        dy2 = dout_ * sg + _dot(dzb, w, NT)
        dy_ref[...] = dy2 * _gelu_grad(y1_)
        _acc(dw_ref, first, _dot(y2b, dzb, TN))
        _acc(db_ref, first, jnp.sum(dz, axis=0, keepdims=True))

    return pl.pallas_call(
        body, name="glu_bwd", grid=(L // tl,),
        in_specs=[_row(tl, n), _row(tl, n), _full((n, n)), _full((1, n))],
        out_specs=[_row(tl, n), _full((n, n)), _full((1, n))],
        out_shape=[jax.ShapeDtypeStruct((L, n), F32), jax.ShapeDtypeStruct((n, n), F32), jax.ShapeDtypeStruct((1, n), F32)],
        compiler_params=_params("arbitrary"))(dout, y1, w_glu, b_glu)


ATTN_TILE = 1024
ATTN_SCALE = 1.0 / math.sqrt(QK_HEAD)


ATTN_HEADS = 2
ATTN_GROUPS = N_HEADS // ATTN_HEADS
LOG2E = 1.0 / math.log(2.0)
Q_PRESCALE = ATTN_SCALE * LOG2E
ANY_SPEC = pl.BlockSpec(memory_space=pl.ANY)


def _attn_fwd_call(q, kv, blocks):
    L = q.shape[0]
    t = _fit(L, ATTN_TILE)
    nq = L // t
    n = len(blocks)

    def body(q_ref, kv_ref, *refs):
        blk_refs, (o_ref, lse_ref), gat_refs = refs[:n], refs[n:n + 2], refs[n + 2:2 * n + 2]
        m_s, acc_s, send_sems, recv_sems, local_sems = refs[2 * n + 2:]
        g, i = pl.program_id(0), pl.program_id(1)
        start, forward, finish = _gather_phases(blk_refs, gat_refs, send_sems, recv_sems, local_sems)
        pl.when(jnp.logical_and(g == 0, i == 0))(start)
        m_s[...] = jnp.full((ATTN_HEADS, t, 1), NEG, F32)
        acc_s[...] = jnp.zeros((ATTN_HEADS, t, LANES), F32)
        below = lax.broadcasted_iota(jnp.int32, (t, t), 1) <= lax.broadcasted_iota(jnp.int32, (t, t), 0)

        def block_step(kb, on_diagonal):
            rows = pl.ds(pl.multiple_of(kb * t, t), t)
            for a in range(ATTN_HEADS):
                s = _dot(q_ref[:, a * LANES:(a + 1) * LANES], kv_ref[rows, 2 * a * LANES:(2 * a + 1) * LANES], NT)
                if on_diagonal:
                    s = jnp.where(below, s, NEG)
                m_prev = m_s[a]
                m_new = jnp.maximum(m_prev, jnp.max(s, axis=1, keepdims=True))
                p = jnp.exp2(s - m_new)
                pv = _dot(p.astype(BF16), kv_ref[rows, (2 * a + 1) * LANES:(2 * a + 2) * LANES], NN)
                acc_s[a] = jnp.exp2(m_prev - m_new) * acc_s[a] + pv
                m_s[a] = m_new

        def step(kb, carry):
            block_step(kb, False)
            return carry

        lax.fori_loop(0, i, step, 0)
        block_step(i, True)
        lane = lax.broadcasted_iota(jnp.int32, (t, LANES), 1)
        for a in range(ATTN_HEADS):
            acc = acc_s[a]
            l = jnp.sum(jnp.where(lane == V_HEAD, acc, 0.0), axis=1, keepdims=True)
            o_ref[:, a * LANES:(a + 1) * LANES] = (acc / l).astype(BF16)
            lse_ref[a] = m_s[a] + jnp.log(l) * LOG2E
        pl.when(jnp.logical_and(g == (3 * ATTN_GROUPS) // 4, i == 0))(forward)
        pl.when(jnp.logical_and(g == ATTN_GROUPS - 1, i == nq - 1))(finish)

    gw = ATTN_HEADS * LANES
    return pl.pallas_call(
        body, name="attn_fwd", grid=(ATTN_GROUPS, nq),
        in_specs=[pl.BlockSpec((t, gw), lambda g, i: (i, g)),
                  pl.BlockSpec((L, 2 * gw), lambda g, i: (0, g))] + [ANY_SPEC] * n,
        out_specs=[pl.BlockSpec((t, gw), lambda g, i: (i, g)),
                   pl.BlockSpec((ATTN_HEADS, t, 1), lambda g, i: (g, i, 0))] + [ANY_SPEC] * n,
        out_shape=[jax.ShapeDtypeStruct((L, HEAD_PAD), BF16), jax.ShapeDtypeStruct((N_HEADS, L, 1), F32)]
        + [jax.ShapeDtypeStruct((N_DEV,) + b.shape, b.dtype) for b in blocks],
        scratch_shapes=[pltpu.VMEM((ATTN_HEADS, t, 1), F32), pltpu.VMEM((ATTN_HEADS, t, LANES), F32)] + _comm_sems(n),
        compiler_params=_params("arbitrary", "arbitrary", vmem=VMEM_BIG))(q, kv, *blocks)


def _attn_bwd_call(q, kv, o, do, lse, parts, blocks):
    L = q.shape[0]
    t = _fit(L, ATTN_TILE)
    nq = L // t
    n1, n = len(parts), len(parts) + len(blocks)

    def body(q_ref, do_ref, o_ref, lse_ref, kv_ref, *refs):
        in_refs, (dq_ref, dkv_ref), out_refs = refs[:n], refs[n:n + 2], refs[n + 2:2 * n + 2]
        dk_s, dv_s = refs[2 * n + 2:2 * n + 4]
        g, j = pl.program_id(0), pl.program_id(1)
        start, finish = _exchange_phases(in_refs[:n1], out_refs[:n1], *refs[2 * n + 4:2 * n + 7])
        start_blocks, finish_blocks = _exchange_phases(in_refs[n1:], out_refs[n1:], *refs[2 * n + 7:], same_source=True)

        @pl.when(jnp.logical_and(g == 0, j == 0))
        def _():
            start()
            start_blocks()

        @pl.when(j == 0)
        def _():
            dq_ref[...] = jnp.zeros((L, ATTN_HEADS * LANES), F32)

        dk_s[...] = jnp.zeros((ATTN_HEADS, t, LANES), F32)
        dv_s[...] = jnp.zeros((ATTN_HEADS, t, LANES), F32)
        below = lax.broadcasted_iota(jnp.int32, (t, t), 1) <= lax.broadcasted_iota(jnp.int32, (t, t), 0)

        def block_step(i, on_diagonal):
            rows = pl.ds(pl.multiple_of(i * t, t), t)
            for a in range(ATTN_HEADS):
                lanes = slice(a * LANES, (a + 1) * LANES)
                qi = q_ref[rows, lanes]
                doi = do_ref[rows, lanes]
                kblk = kv_ref[:, 2 * a * LANES:(2 * a + 1) * LANES]
                delta = jnp.sum(doi.astype(F32) * o_ref[rows, lanes].astype(F32), axis=1, keepdims=True)
                s = _dot(qi, kblk, NT)
                if on_diagonal:
                    s = jnp.where(below, s, NEG)
                p = jnp.exp2(s - lse_ref[a, rows, :])
                dv_s[a] += _dot(p.astype(BF16), doi, TN)
                ds = (p * (_dot(doi, kv_ref[:, (2 * a + 1) * LANES:(2 * a + 2) * LANES], NT) - delta)).astype(BF16)
                dk_s[a] += _dot(ds, qi, TN)
                dq_ref[rows, lanes] += _dot(ds, kblk, NN) * ATTN_SCALE

        def step(i, carry):
            block_step(i, False)
            return carry

        block_step(j, True)
        lax.fori_loop(j + 1, nq, step, 0)
        for a in range(ATTN_HEADS):
            dkv_ref[:, 2 * a * LANES:(2 * a + 1) * LANES] = dk_s[a] * (1.0 / LOG2E)
            dkv_ref[:, (2 * a + 1) * LANES:(2 * a + 2) * LANES] = dv_s[a]
        @pl.when(jnp.logical_and(g == ATTN_GROUPS - 1, j == nq - 1))
        def _():
            finish()
            finish_blocks()

    gw = ATTN_HEADS * LANES
    whole = lambda: pl.BlockSpec((L, gw), lambda g, j: (0, g))
    acc = pltpu.VMEM((ATTN_HEADS, t, LANES), F32)
    return pl.pallas_call(
        body, name="attn_bwd", grid=(ATTN_GROUPS, nq),
        in_specs=[whole(), whole(), whole(), pl.BlockSpec((ATTN_HEADS, L, 1), lambda g, j: (g, 0, 0)),
                  pl.BlockSpec((t, 2 * gw), lambda g, j: (j, g))] + [ANY_SPEC] * n,
        out_specs=[whole(), pl.BlockSpec((t, 2 * gw), lambda g, j: (j, g))] + [ANY_SPEC] * n,
        out_shape=[jax.ShapeDtypeStruct((L, HEAD_PAD), F32), jax.ShapeDtypeStruct((L, 2 * HEAD_PAD), F32)]
        + [jax.ShapeDtypeStruct(p.shape, p.dtype) for p in parts]
        + [jax.ShapeDtypeStruct((N_DEV,) + b.shape, b.dtype) for b in blocks],
        scratch_shapes=[acc, acc] + _comm_sems(n1) + _comm_sems(n - n1),
        compiler_params=_params("arbitrary", "arbitrary", vmem=VMEM_BIG))(q, do, o, lse, kv, *parts, *blocks)


def _disc(lr, li, ldt, br, bi):
    dt = jnp.exp(ldt)
    mag = jnp.exp(lr * dt)
    ang = li * dt
    a_re, a_im = mag * jnp.cos(ang), mag * jnp.sin(ang)
    den = lr * lr + li * li
    n_re, n_im = a_re - 1.0, a_im
    z_re = (n_re * lr + n_im * li) / den
    z_im = (n_im * lr - n_re * li) / den
    return a_re, a_im, z_re * br - z_im * bi, z_re * bi + z_im * br


def _disc_call(lr, li, ldt, br, bi):
    def body(lr_ref, li_ref, ldt_ref, br_ref, bi_ref, ar_ref, ai_ref, bbr_ref, bbi_ref):
        ar_ref[...], ai_ref[...], bbr_ref[...], bbi_ref[...] = _disc(
            lr_ref[...], li_ref[...], ldt_ref[...], br_ref[...], bi_ref[...])

    c1 = jax.ShapeDtypeStruct((SSM_NSTATE, 1), F32)
    c16 = jax.ShapeDtypeStruct((SSM_NSTATE, SSM_GROUP), F32)
    return pl.pallas_call(body, name="ssm_disc", out_shape=[c1, c1, c16, c16])(lr, li, ldt, br, bi)


def _disc_bwd_call(lr, li, ldt, br, bi, dar, dai, dbbr, dbbi):
    def body(lr_ref, li_ref, ldt_ref, br_ref, bi_ref, dar_ref, dai_ref, dbbr_ref, dbbi_ref,
             dlr_ref, dli_ref, dldt_ref, dbr_ref, dbi_ref):
        _, vjp = jax.vjp(_disc, lr_ref[...], li_ref[...], ldt_ref[...], br_ref[...], bi_ref[...])
        dlr_ref[...], dli_ref[...], dldt_ref[...], dbr_ref[...], dbi_ref[...] = vjp(
            (dar_ref[...], dai_ref[...], dbbr_ref[...], dbbi_ref[...]))

    c1 = jax.ShapeDtypeStruct((SSM_NSTATE, 1), F32)
    c16 = jax.ShapeDtypeStruct((SSM_NSTATE, SSM_GROUP), F32)
    return pl.pallas_call(body, name="ssm_disc_bwd", out_shape=[c1, c1, c1, c16, c16])(
        lr, li, ldt, br, bi, dar, dai, dbbr, dbbi)


SSM_ROWS = 512
SSM_CW = SSM_NSTATE // SSM_CHUNKS
SSM_CU = SSM_WIDTH // SSM_CHUNKS


def _cmul(ar, ai, br, bi):
    return ar * br - ai * bi, ar * bi + ai * br


def _power(ar1, ai1, n):
    def step(_, c):
        return _cmul(c[0], c[1], ar1, ai1)

    return lax.fori_loop(0, n, step, (jnp.ones_like(ar1), jnp.zeros_like(ar1)))


def _tile(k):
    return pl.ds(pl.multiple_of(k * 8, 8), 8)


def _ssm_fwd_call(u, a_re, a_im, bb_re, bb_im, cm_re, cm_im, d_skip):
    L = u.shape[0]
    seg = L // 8
    rb = _fit(L, SSM_ROWS)

    def body(u_ref, ar_ref, ai_ref, bbr_ref, bbi_ref, cmr_ref, cmi_ref, d_ref, y_ref, sre_hbm, sim_hbm,
             s_re, s_im, sems):
        q = pl.program_id(0)

        def bu_step(r, c):
            rows = pl.ds(pl.multiple_of(r * rb, rb), rb)
            ub = u_ref[rows, :].astype(BF16)
            s_re[rows, :] = _dot(ub, bbr_ref[0], NN)
            s_im[rows, :] = _dot(ub, bbi_ref[0], NN)
            return c

        lax.fori_loop(0, L // rb, bu_step, 0)
        ar1, ai1 = ar_ref[...], ai_ref[...]
        ar = jnp.broadcast_to(ar1, (8, SSM_CW))
        ai = jnp.broadcast_to(ai1, (8, SSM_CW))

        def local(k, c):
            nr, ni = _cmul(ar, ai, c[0], c[1])
            nr = nr + s_re[_tile(k), :]
            ni = ni + s_im[_tile(k), :]
            s_re[_tile(k), :] = nr
            s_im[_tile(k), :] = ni
            return nr, ni

        zero8 = jnp.zeros((8, SSM_CW), F32)
        lax.fori_loop(0, seg, local, (zero8, zero8))
        pr, pi = _power(ar1, ai1, seg)
        end_r = s_re[pl.ds((seg - 1) * 8, 8), :]
        end_i = s_im[pl.ds((seg - 1) * 8, 8), :]
        er = jnp.zeros((1, SSM_CW), F32)
        ei = jnp.zeros((1, SSM_CW), F32)
        rows_r, rows_i = [er], [ei]
        for j in range(7):
            tr, ti = _cmul(pr, pi, er, ei)
            er, ei = end_r[j:j + 1] + tr, end_i[j:j + 1] + ti
            rows_r.append(er)
            rows_i.append(ei)
        e_r = jnp.concatenate(rows_r, axis=0)
        e_i = jnp.concatenate(rows_i, axis=0)

        def fix(k, c):
            wr, wi = _cmul(c[0], c[1], ar, ai)
            fr, fi = _cmul(wr, wi, e_r, e_i)
            s_re[_tile(k), :] += fr
            s_im[_tile(k), :] += fi
            return wr, wi

        lax.fori_loop(0, seg, fix, (jnp.ones((8, SSM_CW), F32), zero8))
        out_r = pltpu.make_async_copy(s_re, sre_hbm.at[q], sems.at[0])
        out_i = pltpu.make_async_copy(s_im, sim_hbm.at[q], sems.at[1])
        out_r.start()
        out_i.start()

        def y_step(r, c):
            rows = pl.ds(pl.multiple_of(r * rb, rb), rb)
            y = _dot(s_re[rows, :].astype(BF16), cmr_ref[0], NN) - _dot(s_im[rows, :].astype(BF16), cmi_ref[0], NN)
            y_ref[rows, :] = y + d_ref[...] * u_ref[rows, :]
            return c

        lax.fori_loop(0, L // rb, y_step, 0)
        out_r.wait()
        out_i.wait()

    chunk = lambda rows, cols: pl.BlockSpec((rows, cols), lambda q: (0, q))
    mat = lambda r, c: pl.BlockSpec((1, r, c), lambda q: (q, 0, 0))
    anyspec = pl.BlockSpec(memory_space=pl.ANY)
    states = jax.ShapeDtypeStruct((SSM_CHUNKS, L, SSM_CW), F32)
    return pl.pallas_call(
        body, name="ssm_fwd", grid=(SSM_CHUNKS,),
        in_specs=[chunk(L, SSM_CU), chunk(1, SSM_CW), chunk(1, SSM_CW), mat(SSM_CU, SSM_CW), mat(SSM_CU, SSM_CW),
                  mat(SSM_CW, SSM_CU), mat(SSM_CW, SSM_CU), chunk(1, SSM_CU)],
        out_specs=[chunk(L, SSM_CU), anyspec, anyspec],
        out_shape=[jax.ShapeDtypeStruct((L, SSM_WIDTH), F32), states, states],
        scratch_shapes=[pltpu.VMEM((L, SSM_CW), F32), pltpu.VMEM((L, SSM_CW), F32), pltpu.SemaphoreType.DMA((2,))],
        compiler_params=_params("arbitrary", vmem=VMEM_BIG))(u, a_re, a_im, bb_re, bb_im, cm_re, cm_im, d_skip)


def _ssm_bwd_call(dy, u, s_re_all, s_im_all, a_re, a_im, bb_re, bb_im, cm_re, cm_im, d_skip):
    L = u.shape[0]
    seg = L // 8
    rb = _fit(L, SSM_ROWS)

    def body(dy_ref, u_ref, sre_hbm, sim_hbm, ar_ref, ai_ref, bbr_ref, bbi_ref, cmr_ref, cmi_ref, d_ref,
             du_ref, dbbr_ref, dbbi_ref, dcmr_ref, dcmi_ref, dar_ref, dai_ref, dd_ref,
             g_re, g_im, s_re, s_im, sems):
        q = pl.program_id(0)
        in_r = pltpu.make_async_copy(sre_hbm.at[q], s_re, sems.at[0])
        in_i = pltpu.make_async_copy(sim_hbm.at[q], s_im, sems.at[1])
        in_r.start()
        in_i.start()

        def ds_step(r, c):
            rows = pl.ds(pl.multiple_of(r * rb, rb), rb)
            dyb = dy_ref[rows, :].astype(BF16)
            g_re[rows, :] = _dot(dyb, cmr_ref[0], NT)
            g_im[rows, :] = -_dot(dyb, cmi_ref[0], NT)
            return c

        lax.fori_loop(0, L // rb, ds_step, 0)
        ar1, ai1 = ar_ref[...], ai_ref[...]
        ar = jnp.broadcast_to(ar1, (8, SSM_CW))
        nai = jnp.broadcast_to(-ai1, (8, SSM_CW))

        def local(kk, c):
            k = seg - 1 - kk
            nr, ni = _cmul(ar, nai, c[0], c[1])
            nr = nr + g_re[_tile(k), :]
            ni = ni + g_im[_tile(k), :]
            g_re[_tile(k), :] = nr
            g_im[_tile(k), :] = ni
            return nr, ni

        zero8 = jnp.zeros((8, SSM_CW), F32)
        lax.fori_loop(0, seg, local, (zero8, zero8))
        pr, pi = _power(ar1, -ai1, seg)
        head_r = g_re[pl.ds(0, 8), :]
        head_i = g_im[pl.ds(0, 8), :]
        fr = jnp.zeros((1, SSM_CW), F32)
        fi = jnp.zeros((1, SSM_CW), F32)
        rows_r, rows_i = [fr], [fi]
        for j in range(6, -1, -1):
            tr, ti = _cmul(pr, pi, fr, fi)
            fr, fi = head_r[j + 1:j + 2] + tr, head_i[j + 1:j + 2] + ti
            rows_r.insert(0, fr)
            rows_i.insert(0, fi)
        f_r = jnp.concatenate(rows_r, axis=0)
        f_i = jnp.concatenate(rows_i, axis=0)
        in_r.wait()
        in_i.wait()

        def fixed(k, wr, wi):
            xr, xi = _cmul(wr, wi, f_r, f_i)
            gr = g_re[_tile(k), :] + xr
            gi = g_im[_tile(k), :] + xi
            g_re[_tile(k), :] = gr
            g_im[_tile(k), :] = gi
            return gr, gi

        def fix(kk, c):
            k = seg - 1 - kk
            wr, wi = _cmul(c[0], c[1], ar, nai)
            gr, gi = fixed(k, wr, wi)
            pr_, pi_ = s_re[_tile(k - 1), :], s_im[_tile(k - 1), :]
            return wr, wi, c[2] + gr * pr_ + gi * pi_, c[3] + gi * pr_ - gr * pi_

        wr, wi, acc_r, acc_i = lax.fori_loop(0, seg - 1, fix, (jnp.ones((8, SSM_CW), F32), zero8, zero8, zero8))
        wr, wi = _cmul(wr, wi, ar, nai)
        gr, gi = fixed(0, wr, wi)
        row8 = lax.broadcasted_iota(jnp.int32, (8, SSM_CW), 0)
        pr_ = jnp.where(row8 > 0, pltpu.roll(s_re[pl.ds((seg - 1) * 8, 8), :], 1, 0), 0.0)
        pi_ = jnp.where(row8 > 0, pltpu.roll(s_im[pl.ds((seg - 1) * 8, 8), :], 1, 0), 0.0)
        acc_r = acc_r + gr * pr_ + gi * pi_
        acc_i = acc_i + gi * pr_ - gr * pi_
        dar_ref[...] = jnp.sum(acc_r, axis=0, keepdims=True)
        dai_ref[...] = jnp.sum(acc_i, axis=0, keepdims=True)

        dbbr_ref[...] = jnp.zeros((1, SSM_CU, SSM_CW), F32)
        dbbi_ref[...] = jnp.zeros((1, SSM_CU, SSM_CW), F32)
        dcmr_ref[...] = jnp.zeros((1, SSM_CW, SSM_CU), F32)
        dcmi_ref[...] = jnp.zeros((1, SSM_CW, SSM_CU), F32)
        dd_ref[...] = jnp.zeros((1, SSM_CU), F32)

        def grad_step(r, c):
            rows = pl.ds(pl.multiple_of(r * rb, rb), rb)
            ub, dyv = u_ref[rows, :], dy_ref[rows, :]
            ubb, dyb = ub.astype(BF16), dyv.astype(BF16)
            grb, gib = g_re[rows, :].astype(BF16), g_im[rows, :].astype(BF16)
            dbbr_ref[0] += _dot(ubb, grb, TN)
            dbbi_ref[0] += _dot(ubb, gib, TN)
            dcmr_ref[0] += _dot(s_re[rows, :].astype(BF16), dyb, TN)
            dcmi_ref[0] -= _dot(s_im[rows, :].astype(BF16), dyb, TN)
            du_ref[rows, :] = _dot(grb, bbr_ref[0], NT) + _dot(gib, bbi_ref[0], NT) + d_ref[...] * dyv
            dd_ref[...] += jnp.sum(dyv * ub, axis=0, keepdims=True)
            return c

        lax.fori_loop(0, L // rb, grad_step, 0)

    chunk = lambda rows, cols: pl.BlockSpec((rows, cols), lambda q: (0, q))
    mat = lambda r, c: pl.BlockSpec((1, r, c), lambda q: (q, 0, 0))
    anyspec = pl.BlockSpec(memory_space=pl.ANY)
    big = lambda: pltpu.VMEM((L, SSM_CW), F32)
    return pl.pallas_call(
        body, name="ssm_bwd", grid=(SSM_CHUNKS,),
        in_specs=[chunk(L, SSM_CU), chunk(L, SSM_CU), anyspec, anyspec, chunk(1, SSM_CW), chunk(1, SSM_CW),
                  mat(SSM_CU, SSM_CW), mat(SSM_CU, SSM_CW), mat(SSM_CW, SSM_CU), mat(SSM_CW, SSM_CU), chunk(1, SSM_CU)],
        out_specs=[chunk(L, SSM_CU), mat(SSM_CU, SSM_CW), mat(SSM_CU, SSM_CW), mat(SSM_CW, SSM_CU), mat(SSM_CW, SSM_CU),
                   chunk(1, SSM_CW), chunk(1, SSM_CW), chunk(1, SSM_CU)],
        out_shape=[jax.ShapeDtypeStruct((L, SSM_WIDTH), F32),
                   jax.ShapeDtypeStruct((SSM_CHUNKS, SSM_CU, SSM_CW), F32), jax.ShapeDtypeStruct((SSM_CHUNKS, SSM_CU, SSM_CW), F32),
                   jax.ShapeDtypeStruct((SSM_CHUNKS, SSM_CW, SSM_CU), F32), jax.ShapeDtypeStruct((SSM_CHUNKS, SSM_CW, SSM_CU), F32),
                   jax.ShapeDtypeStruct((1, SSM_NSTATE), F32), jax.ShapeDtypeStruct((1, SSM_NSTATE), F32),
                   jax.ShapeDtypeStruct((1, SSM_WIDTH), F32)],
        scratch_shapes=[big(), big(), big(), big(), pltpu.SemaphoreType.DMA((2,))],
        compiler_params=_params("arbitrary", vmem=VMEM_BIG))(
            dy, u, s_re_all, s_im_all, a_re, a_im, bb_re, bb_im, cm_re, cm_im, d_skip)


def _place():
    return lax.axis_index("x"), lax.axis_index("y"), lax.axis_index("c")


def _all_gather_call(blocks, name, direct=False):
    n = len(blocks)

    def body(*refs):
        if direct:
            start, finish = _exchange_phases(refs[:n], refs[n:2 * n], *refs[2 * n:], same_source=True)
            start()
        else:
            start, forward, finish = _gather_phases(refs[:n], refs[n:2 * n], *refs[2 * n:])
            start()
            forward()
        finish()

    return pl.pallas_call(
        body, name=name, in_specs=[ANY_SPEC] * n, out_specs=[ANY_SPEC] * n,
        out_shape=[jax.ShapeDtypeStruct((N_DEV,) + b.shape, b.dtype) for b in blocks],
        scratch_shapes=_comm_sems(n))(*blocks)


def _comm_sems(n):
    return [pltpu.SemaphoreType.DMA((7 * n,)), pltpu.SemaphoreType.DMA((7 * n,)), pltpu.SemaphoreType.DMA((n,))]


def _gather_phases(x_refs, out_refs, send_sems, recv_sems, local_sems):
    x, y, c = _place()
    me, sibling = (x, y, c), (x, y, 1 - c)
    chips = [(1 - x, y), (x, 1 - y), (1 - x, 1 - y)]
    n = len(x_refs)

    def copy(k, a, blk, to, from_input=False):
        slot = out_refs[a].at[4 * blk[0] + 2 * blk[1] + blk[2]]
        return pltpu.make_async_remote_copy(
            src_ref=x_refs[a] if from_input else slot, dst_ref=slot,
            send_sem=send_sems.at[k * n + a], recv_sem=recv_sems.at[k * n + a], device_id=to, device_id_type=MESH_ID)

    mine = [pltpu.make_async_copy(x_refs[a], out_refs[a].at[4 * x + 2 * y + c], local_sems.at[a]) for a in range(n)]
    first, passed = [], []
    for a in range(n):
        first.append(copy(0, a, me, sibling, True))
        first += [copy(1 + j, a, me, (*chip, c), True) for j, chip in enumerate(chips)]
        passed += [copy(4 + j, a, (*chip, c), sibling) for j, chip in enumerate(chips)]

    def start():
        for cp in mine + first:
            cp.start()

    def forward():
        for j, chip in enumerate(chips):
            for a in range(n):
                copy(1 + j, a, (*chip, c), me).wait_recv()
                passed[3 * a + j].start()

    def finish():
        for a in range(n):
            copy(0, a, sibling, me).wait_recv()
            for j, chip in enumerate(chips):
                copy(4 + j, a, (*chip, 1 - c), me).wait_recv()
        for cp in first + passed:
            cp.wait_send()
        for cp in mine:
            cp.wait()

    return start, forward, finish


def _exchange_phases(p_refs, out_refs, send_sems, recv_sems, local_sems, same_source=False):
    x, y, c = _place()
    me = 4 * x + 2 * y + c
    n = len(p_refs)

    def flip(k):
        px = 1 - x if k & 4 else x
        py = 1 - y if k & 2 else y
        pc = 1 - c if k & 1 else c
        return (px, py, pc), 4 * px + 2 * py + pc

    def source(a, slot):
        return p_refs[a] if same_source else p_refs[a].at[slot]

    def copy(k, a, landing):
        peer, peer_slot = flip(k)
        return pltpu.make_async_remote_copy(
            src_ref=source(a, peer_slot), dst_ref=out_refs[a].at[peer_slot if landing else me],
            send_sem=send_sems.at[(k - 1) * n + a], recv_sem=recv_sems.at[(k - 1) * n + a],
            device_id=peer, device_id_type=MESH_ID)

    mine = [pltpu.make_async_copy(source(a, me), out_refs[a].at[me], local_sems.at[a]) for a in range(n)]
    sends = [copy(k, a, False) for k in range(1, N_DEV) for a in range(n)]

    def start():
        for cp in mine + sends:
            cp.start()

    def finish():
        for k in range(1, N_DEV):
            for a in range(n):
                copy(k, a, True).wait_recv()
        for cp in sends:
            cp.wait_send()
        for cp in mine:
            cp.wait()

    return start, finish


def _adam_math(g, w, m, v):
    c1 = 1.0 / (1.0 - ADAM_B1 ** ADAM_STEP)
    c2 = 1.0 / (1.0 - ADAM_B2 ** ADAM_STEP)
    m_new = ADAM_B1 * m + (1.0 - ADAM_B1) * g
    v_new = ADAM_B2 * v + (1.0 - ADAM_B2) * (g * g)
    delta = -ADAM_LR * ((m_new * c1) / (jnp.sqrt(v_new * c2) + ADAM_EPS) + ADAM_WD * w)
    return g, delta, m_new, v_new


def _sum_slices(s_ref):
    g = s_ref[0].astype(F32)
    for k in range(1, N_DEV):
        g = g + s_ref[k].astype(F32)
    return g


def _adam_call(slices, w, m, v, name):
    rest = w.shape[2:]
    t1 = _fit(w.shape[1], 256, 16) if len(rest) == 1 else _fit(w.shape[1], 8, 8)
    zeros = (0,) * len(rest)

    def body(s_ref, w_ref, m_ref, v_ref, g_ref, d_ref, mo_ref, vo_ref):
        g_ref[...], d_ref[...], mo_ref[...], vo_ref[...] = _adam_math(_sum_slices(s_ref), w_ref[...], m_ref[...], v_ref[...])

    own = pl.BlockSpec((1, t1) + rest, lambda i: (0, i) + zeros)
    out = jax.ShapeDtypeStruct(w.shape, F32)
    return pl.pallas_call(
        body, name=name, grid=(w.shape[1] // t1,),
        in_specs=[pl.BlockSpec((N_DEV, 1, t1) + rest, lambda i: (0, 0, i) + zeros), own, own, own],
        out_specs=[own, own, own, own], out_shape=[out, out, out, out],
        compiler_params=_params("parallel"))(slices, w, m, v)


def _adam_small_call(rows_all, row_params, slices, params):
    nr, n = len(row_params), len(row_params) + len(params)

    def row_sum(rows_ref, a, width):
        g = rows_ref[0, pl.ds(a, 1), pl.ds(0, width)]
        for k in range(1, N_DEV):
            g = g + rows_ref[k, pl.ds(a, 1), pl.ds(0, width)]
        return g

    def body(rows_ref, *refs):
        slice_refs, wmv, outs = refs[:n - nr], refs[n - nr:n - nr + 3 * n], refs[n - nr + 3 * n:]
        outs[4 * n][...] = row_sum(rows_ref, nr, LANES)
        for a in range(n):
            w_ref, m_ref, v_ref = wmv[3 * a:3 * a + 3]
            if a < nr:
                g = row_sum(rows_ref, a, w_ref.shape[1])
            else:
                g = _sum_slices(slice_refs[a - nr])
            res = _adam_math(g, w_ref[...], m_ref[...], v_ref[...])
            for r in range(4):
                outs[4 * a + r][...] = res[r]

    every = list(row_params) + list(params)
    flat = pl.pallas_call(
        body, name="adam_small",
        out_shape=[jax.ShapeDtypeStruct(w.shape, F32) for w, _, _ in every for _ in range(4)]
        + [jax.ShapeDtypeStruct((1, LANES), F32)],
    )(rows_all, *slices, *[t for wmv in every for t in wmv])
    return [flat[4 * a:4 * a + 4] for a in range(n)], flat[4 * n][0, 0]


BIG = (("w_in", 1024, 404, 1), ("w_uq", 384, 96, 1), ("w_uk", 256, 64, 1), ("w_uv", 256, 64, 1),
       ("w_glu", 64, 512, 0), ("w_branch_attn", 512, 128, 1), ("w_branch_ssm", 512, 128, 1),
       ("w_out", 128, 1024, 0), ("w_up", 1024, 704, 1), ("w_down", 352, 1024, 0), ("conv_w", 3, 704, 1))
BIG_MIX, BIG_FFN = BIG[:8], BIG[8:]
GRADS_EARLY, GRADS_LATE = BIG[8:] + BIG[4:8], BIG[:4]
SMALL = (("mix_norm_pre", (1024,)), ("q_norm", (384,)), ("kv_norm", (256,)), ("ssm_lambda_re", (32, 64)),
         ("ssm_lambda_im", (32, 64)), ("ssm_log_dt", (32,)), ("ssm_b_re", (32, 64, 16)), ("ssm_b_im", (32, 64, 16)),
         ("ssm_c_re", (32, 16, 64)), ("ssm_c_im", (32, 16, 64)), ("ssm_d", (32, 16)), ("b_glu", (512,)),
         ("b_gate", (2048,)), ("mix_norm_post", (1024,)), ("ffn_norm_pre", (1024,)), ("conv_b", (5632,)),
         ("ffn_norm_post", (1024,)))


def _to_slices(full, rows, cols, axis):
    if axis == 1:
        return full.reshape(rows, N_DEV, cols).transpose(1, 0, 2)
    return full.reshape(N_DEV, rows, cols)


def _from_slices(parts, rows, cols, axis):
    if axis == 1:
        return parts.transpose(1, 0, 2).reshape(rows, N_DEV * cols)
    return parts.reshape(N_DEV * rows, cols)


def _head_pad_cols(w, width):
    k = w.shape[0]
    return jnp.pad(w.reshape(k, N_HEADS, width), ((0, 0), (0, 0), (0, LANES - width))).reshape(k, HEAD_PAD)


def _head_unpad_cols(w, width):
    k = w.shape[0]
    return w.reshape(k, N_HEADS, LANES)[:, :, :width].reshape(k, N_HEADS * width)


def _time_perm(a, L):
    return a.reshape(8, L // 8, a.shape[-1]).transpose(1, 0, 2).reshape(L, a.shape[-1])


def _time_unperm(a, L):
    return a.reshape(L // 8, 8, a.shape[-1]).transpose(1, 0, 2).reshape(L, a.shape[-1])


def _block_diag(w, rows_first):
    eye = jnp.eye(8, dtype=w.dtype)
    g = w.reshape(SSM_CHUNKS, 8, w.shape[1], w.shape[2])
    return jnp.einsum("qgrc,gk->qgrkc", g, eye).reshape(SSM_CHUNKS, 8 * w.shape[1], 8 * w.shape[2])


def _block_diag_t(m, r, c):
    eye = jnp.eye(8, dtype=m.dtype)
    return jnp.einsum("qgrkc,gk->qgrc", m.reshape(SSM_CHUNKS, 8, r, 8, c), eye).reshape(SSM_GROUPS, r, c)


def kernel(x, positions, mix_norm_pre, w_in, q_norm, w_uq, kv_norm, w_uk, w_uv, ssm_lambda_re, ssm_lambda_im, ssm_log_dt, ssm_b_re, ssm_b_im, ssm_c_re, ssm_c_im, ssm_d, w_glu, b_glu, w_branch_attn, w_branch_ssm, b_gate, w_out, mix_norm_post, ffn_norm_pre, w_up, conv_w, conv_b, w_down, ffn_norm_post, loss_target, m_mix_norm_pre, m_w_in, m_q_norm, m_w_uq, m_kv_norm, m_w_uk, m_w_uv, m_ssm_lambda_re, m_ssm_lambda_im, m_ssm_log_dt, m_ssm_b_re, m_ssm_b_im, m_ssm_c_re, m_ssm_c_im, m_ssm_d, m_w_glu, m_b_glu, m_w_branch_attn, m_w_branch_ssm, m_b_gate, m_w_out, m_mix_norm_post, m_ffn_norm_pre, m_w_up, m_conv_w, m_conv_b, m_w_down, m_ffn_norm_post, v_mix_norm_pre, v_w_in, v_q_norm, v_w_uq, v_kv_norm, v_w_uk, v_w_uv, v_ssm_lambda_re, v_ssm_lambda_im, v_ssm_log_dt, v_ssm_b_re, v_ssm_b_im, v_ssm_c_re, v_ssm_c_im, v_ssm_d, v_w_glu, v_b_glu, v_w_branch_attn, v_w_branch_ssm, v_b_gate, v_w_out, v_mix_norm_post, v_ffn_norm_pre, v_w_up, v_conv_w, v_conv_b, v_w_down, v_ffn_norm_post):
    given = dict(locals())
    L = x.shape[1]
    xs = x[0]
    target = loss_target[0]

    def shard_bits(group):
        return [given[name][0] if name == "conv_w" else given[name][0].astype(BF16) for name, _, _, _ in group]

    W = {}

    def unpack_weights(gathered, group):
        for (name, rows, cols, axis), parts in zip(group, gathered):
            W[name] = _from_slices(parts, rows, cols, axis)

    unpack_weights(_all_gather_call(shard_bits(BIG_MIX[:1]), "gather_w_in"), BIG_MIX[:1])

    wi = W["w_in"]
    kr_cols = jnp.pad(wi[:, 640:672], ((0, 0), (QK_NOPE, LANES - QK_HEAD)))
    w_in_p = jnp.concatenate([wi[:, :640], kr_cols, wi[:, 672:]], axis=1)

    hn1 = _rms_fwd_call(xs, mix_norm_pre, "rms_pre")
    proj, *gathered_mix = _mm(hn1, w_in_p, "mm_in", tn=1664, gather=shard_bits(BIG_MIX[1:]))
    unpack_weights(gathered_mix, BIG_MIX[1:])
    w_uq_p = _head_pad_cols(W["w_uq"], QK_HEAD)
    w_kv_p = jnp.stack([_head_pad_cols(W["w_uk"], QK_NOPE).reshape(KV_RANK, N_HEADS, LANES),
                        _head_pad_cols(W["w_uv"], V_HEAD).reshape(KV_RANK, N_HEADS, LANES)], axis=2
                       ).reshape(KV_RANK, 2 * HEAD_PAD)
    w_ba_p = jnp.pad(W["w_branch_attn"].reshape(N_HEADS, V_HEAD, D_MODEL), ((0, 0), (0, LANES - V_HEAD), (0, 0))
                     ).reshape(HEAD_PAD, D_MODEL)
    qn, ckvn = _mla_norms_call(proj, q_norm, kv_norm)
    q_pad = _mm(qn, w_uq_p, "mm_uq")
    kv_pad = _mm(ckvn, w_kv_p, "mm_ukv")
    half = jnp.arange(QK_ROPE // 2, dtype=F32)
    inv_freq = ROPE_THETA ** (-2.0 * half / QK_ROPE)
    inv_freq = jnp.pad(jnp.concatenate([inv_freq, inv_freq]), (QK_NOPE, LANES - QK_HEAD)).reshape(1, LANES)
    pos_col = positions.astype(F32).reshape(L, 1)
    q_r, kv_r, cosf, sinf = _mla_prep_call(q_pad, kv_pad, proj, pos_col, inv_freq)
    attn, lse, *gathered_ffn = _attn_fwd_call(q_r, kv_r, shard_bits(BIG_FFN))
    unpack_weights(gathered_ffn, BIG_FFN)

    col = lambda a: a.reshape(SSM_NSTATE, -1)
    lr_c, li_c = col(ssm_lambda_re[0]), col(ssm_lambda_im[0])
    ldt_c = col(jnp.broadcast_to(ssm_log_dt[0][:, None], (SSM_GROUPS, SSM_STATE)))
    br_c, bi_c = col(ssm_b_re[0]), col(ssm_b_im[0])
    a_re_c, a_im_c, bb_re_c, bb_im_c = _disc_call(lr_c, li_c, ldt_c, br_c, bi_c)
    a_re, a_im = a_re_c.reshape(1, SSM_NSTATE), a_im_c.reshape(1, SSM_NSTATE)
    to_bb = lambda b: _block_diag(b.reshape(SSM_GROUPS, SSM_STATE, SSM_GROUP).transpose(0, 2, 1), True).astype(BF16)
    bb_re, bb_im = to_bb(bb_re_c), to_bb(bb_im_c)
    to_cm = lambda c_: _block_diag(c_[0].transpose(0, 2, 1), True).astype(BF16)
    cm_re, cm_im = to_cm(ssm_c_re), to_cm(ssm_c_im)
    d_skip = ssm_d.reshape(1, SSM_WIDTH)
    u_p = _time_perm(proj[:, P_U:P_GATE], L)
    y1, s_re, s_im = _ssm_fwd_call(u_p, a_re, a_im, bb_re, bb_im, cm_re, cm_im, d_skip)
    w_glu_b = W["w_glu"]
    ssm_p = _glu_call(y1, w_glu_b, b_glu)
    ssm = _time_unperm(ssm_p, L)

    pa = _mm(attn, w_ba_p, "mm_ba")
    ps = _mm(ssm, W["w_branch_ssm"], "mm_bs")
    merged = _merge_call(proj, b_gate, pa, ps)
    o = _mm(merged, W["w_out"], "mm_out")
    x2, hn2 = _post_mix_call(o, xs, mix_norm_post, ffn_norm_pre)
    h = _mm(hn2, W["w_up"], "mm_up", tn=1408)
    cw = W["conv_w"]
    act = _conv_act_call(h, cw, conv_b)
    ff = _mm(act, W["w_down"], "mm_down", tn=1024, tk=1408)
    loss_row, dy, dff, g_ffn_norm_post = _ffn_out_call(ff, x2, target, ffn_norm_post)

    da = _mm(dff, W["w_down"], "mm_down_dx", tb=True, tn=1408)
    g_w_down = _mm_tn(act, dff, "mm_down_dw", tm=1408)
    dgate, dval, dcw_g, dcw_v, dcb_g, dcb_v = _conv_act_bwd_call(da, h, cw, conv_b)
    g_conv_w = jnp.concatenate([dcw_g, dcw_v], axis=1)
    g_conv_b = jnp.concatenate([dcb_g, dcb_v], axis=1)
    dh = _conv_t_call(dgate, dval, cw)
    dhn2 = _mm(dh, W["w_up"], "mm_up_dx", tb=True, tn=1024, tk=1408)
    g_w_up = _mm_tn(hn2, dh, "mm_up_dw")
    dx2, do, g_ffn_norm_pre, g_mix_norm_post = _post_bwd_call(x2, dhn2, dy, o, ffn_norm_pre, mix_norm_post)
    dmerged = _mm(do, W["w_out"], "mm_out_dx", tb=True)
    g_w_out = _mm_tn(merged, do, "mm_out_dw")
    dpa, dps, dl0, dl1, db0, db1 = _merge_bwd_call(dmerged, proj, b_gate, pa, ps)
    g_b_gate = jnp.concatenate([db0, db1], axis=1)
    dattn = _mm(dpa, w_ba_p, "mm_ba_dx", tb=True, out_dtype=BF16)
    g_w_ba = _mm_tn(attn, dpa, "mm_ba_dw").reshape(N_HEADS, LANES, D_MODEL)[:, :V_HEAD].reshape(N_HEADS * V_HEAD, D_MODEL)
    dssm = _mm(dps, W["w_branch_ssm"], "mm_bs_dx", tb=True)
    g_w_bs = _mm_tn(ssm, dps, "mm_bs_dw")

    dy1, g_w_glu, g_b_glu = _glu_bwd_call(_time_perm(dssm, L), y1, w_glu_b, b_glu)
    du_p, dbb_re, dbb_im, dcm_re, dcm_im, da_re, da_im, g_ssm_d = _ssm_bwd_call(
        dy1, u_p, s_re, s_im, a_re, a_im, bb_re, bb_im, cm_re, cm_im, d_skip)
    du = _time_unperm(du_p, L)
    from_bb = lambda m: col(_block_diag_t(m, SSM_GROUP, SSM_STATE).transpose(0, 2, 1))
    dlr, dli, dldt, dbr, dbi = _disc_bwd_call(
        lr_c, li_c, ldt_c, br_c, bi_c, da_re.reshape(SSM_NSTATE, 1), da_im.reshape(SSM_NSTATE, 1), from_bb(dbb_re), from_bb(dbb_im))
    g_c_re = _block_diag_t(dcm_re, SSM_STATE, SSM_GROUP).transpose(0, 2, 1)
    g_c_im = _block_diag_t(dcm_im, SSM_STATE, SSM_GROUP).transpose(0, 2, 1)

    def grad_slices(group, grads):
        return [_to_slices(grads[name], rows, cols, axis) for name, rows, cols, axis in group]

    early_grads = {"w_up": g_w_up, "w_down": g_w_down, "conv_w": g_conv_w, "w_glu": g_w_glu.astype(BF16),
                   "w_branch_attn": g_w_ba, "w_branch_ssm": g_w_bs, "w_out": g_w_out}
    ssm_partials = {"ssm_lambda_re": dlr, "ssm_lambda_im": dli, "ssm_b_re": dbr, "ssm_b_im": dbi,
                    "ssm_c_re": g_c_re, "ssm_c_im": g_c_im, "ssm_d": g_ssm_d}
    ssm_shapes = [(name, shp) for name, shp in SMALL if name in ssm_partials]
    dq, dkv, *landed = _attn_bwd_call(
        q_r, kv_r, attn, dattn, lse, grad_slices(GRADS_EARLY, early_grads),
        [ssm_partials[name].reshape(-1, LANES) if len(shp) == 3 else ssm_partials[name].reshape((1,) + shp)
         for name, shp in ssm_shapes])
    received_early = landed[:len(GRADS_EARLY)]
    ssm_all = {name: got.reshape((N_DEV, 1) + shp) for (name, shp), got in zip(ssm_shapes, landed[len(GRADS_EARLY):])}
    dq_p, dkv_p, dkr_p = _mla_prep_bwd_call(dq, dkv, cosf, sinf)
    dqn = _mm(dq_p, w_uq_p, "mm_uq_dx", tb=True)
    g_w_uq = _head_unpad_cols(_mm_tn(qn, dq_p, "mm_uq_dw"), QK_HEAD)
    dckvn = _mm(dkv_p, w_kv_p, "mm_ukv_dx", tb=True)
    g_w_kv = _mm_tn(ckvn, dkv_p, "mm_ukv_dw").reshape(KV_RANK, N_HEADS, 2, LANES)
    g_w_uk = g_w_kv[:, :, 0, :QK_NOPE].reshape(KV_RANK, N_HEADS * QK_NOPE)
    g_w_uv = g_w_kv[:, :, 1, :V_HEAD].reshape(KV_RANK, N_HEADS * V_HEAD)
    dcqkv, g_q_norm, g_kv_norm = _mla_norms_bwd_call(proj, dqn, dckvn, q_norm, kv_norm)
    dproj = jnp.concatenate([dcqkv, dkr_p, du.astype(BF16), dl0, dl1], axis=1)
    g_w_in_p = _mm_tn(hn1, dproj, "mm_in_dw", tk=1024)
    g_w_in = jnp.concatenate([g_w_in_p[:, :640], g_w_in_p[:, 640 + QK_NOPE:640 + QK_HEAD], g_w_in_p[:, 768:]], axis=1)
    late_grads = {"w_in": g_w_in, "w_uq": g_w_uq, "w_uk": g_w_uk, "w_uv": g_w_uv}
    dhn1, *received_late = _mm(dproj, w_in_p, "mm_in_dx", tb=True, tk=1664, exchange=grad_slices(GRADS_LATE, late_grads))
    grad_x, g_mix_norm_pre = _pre_bwd_call(xs, dhn1, dx2, mix_norm_pre)

    results = {}
    for group, received in ((GRADS_EARLY, received_early), (GRADS_LATE, received_late)):
        for (name, _, _, _), rec in zip(group, received):
            results[name] = _adam_call(rec[:, None], given[name], given["m_" + name], given["v_" + name], "adam_" + name)

    vec_grads = {"mix_norm_pre": g_mix_norm_pre, "q_norm": g_q_norm, "kv_norm": g_kv_norm,
                 "ssm_log_dt": jnp.sum(dldt.reshape(SSM_GROUPS, SSM_STATE), axis=1),
                 "b_glu": g_b_glu, "b_gate": g_b_gate, "mix_norm_post": g_mix_norm_post,
                 "ffn_norm_pre": g_ffn_norm_pre, "conv_b": g_conv_b, "ffn_norm_post": g_ffn_norm_post}
    vec_names = [name for name, _ in SMALL if name in vec_grads]
    width = max(shp[0] for name, shp in SMALL if name in vec_grads)
    rows = [jnp.pad(vec_grads[name].reshape(1, -1), ((0, 0), (0, width - vec_grads[name].size))) for name in vec_names]
    rows.append(jnp.pad(loss_row, ((0, 0), (0, width - LANES))))
    rows.append(jnp.zeros((-len(rows) % 8, width), F32))
    rows_all, = _all_gather_call([jnp.concatenate(rows, axis=0)], "gather_small_grads", direct=True)
    wmv = lambda name: (given[name], given["m_" + name], given["v_" + name])
    few = ["ssm_lambda_re", "ssm_lambda_im", "ssm_d"]
    small_results, loss = _adam_small_call(
        rows_all, [wmv(n) for n in vec_names], [ssm_all[n] for n in few], [wmv(n) for n in few])
    results.update(zip(vec_names + few, small_results))
    for name in ("ssm_b_re", "ssm_b_im", "ssm_c_re", "ssm_c_im"):
        results[name] = _adam_call(ssm_all[name], *wmv(name), "adam_" + name)

    order = ["mix_norm_pre", "w_in", "q_norm", "w_uq", "kv_norm", "w_uk", "w_uv", "ssm_lambda_re", "ssm_lambda_im",
             "ssm_log_dt", "ssm_b_re", "ssm_b_im", "ssm_c_re", "ssm_c_im", "ssm_d", "w_glu", "b_glu", "w_branch_attn",
             "w_branch_ssm", "b_gate", "w_out", "mix_norm_post", "ffn_norm_pre", "w_up", "conv_w", "conv_b", "w_down",
             "ffn_norm_post"]
    outs = [loss, grad_x[None]]
    for kind in range(4):
        outs += [results[name][kind] for name in order]
    return tuple(outs)
```

```python
import math

import jax
import jax.numpy as jnp
from jax import lax
from jax.experimental import pallas as pl
from jax.experimental.pallas import tpu as pltpu

F32 = jnp.float32
BF16 = jnp.bfloat16
MESH_ID = pl.DeviceIdType.MESH

N_DEV = 8
LANES = 128
D_MODEL = 1024
N_HEADS = 8
QK_NOPE = 64
QK_ROPE = 32
QK_HEAD = QK_NOPE + QK_ROPE
V_HEAD = 64
Q_RANK = 384
KV_RANK = 256
ROPE_THETA = 10000.0
SSM_WIDTH = 512
SSM_GROUP = 16
SSM_GROUPS = 32
SSM_STATE = 64
SSM_NSTATE = SSM_GROUPS * SSM_STATE
SSM_CHUNKS = 4
D_FF = 2816
EPS = 1e-6
ADAM_LR, ADAM_B1, ADAM_B2, ADAM_EPS, ADAM_WD, ADAM_STEP = 0.001, 0.9, 0.999, 1e-08, 0.01, 10

P_CQ, P_CKV, P_KR, P_U, P_GATE = 0, 384, 640, 768, 1280
P_IN = P_GATE + 2 * D_MODEL
HEAD_PAD = N_HEADS * LANES

PACK_ROWS = 1024
VMEM_BIG = 52 * 1024 * 1024

_GELU_C0 = math.sqrt(2.0 / math.pi)
_GELU_C1 = 0.044715
NEG = -1e30


def _fit(n, pref, mult=LANES):
    if n <= pref:
        return n
    t = (pref // mult) * mult
    while t > 0 and n % t:
        t -= mult
    assert t > 0, (n, pref, mult)
    return t


def _gelu(x):
    return x * (0.5 * (1.0 + jnp.tanh(_GELU_C0 * x * (1.0 + _GELU_C1 * (x * x)))))


def _gelu_and_grad(x):
    x2 = x * x
    t = jnp.tanh(_GELU_C0 * x * (1.0 + _GELU_C1 * x2))
    half = 0.5 * (1.0 + t)
    return x * half, half + 0.5 * x * (1.0 - t * t) * _GELU_C0 * (1.0 + 3.0 * _GELU_C1 * x2)


def _sigmoid(x):
    return 1.0 / (1.0 + jnp.exp(-x))


def _dot(a, b, dims):
    return lax.dot_general(a, b, (dims, ((), ())), preferred_element_type=F32)


NN = ((1,), (0,))
NT = ((1,), (1,))
TN = ((0,), (0,))


def _params(*sem, vmem=None):
    return pltpu.CompilerParams(dimension_semantics=tuple(sem), vmem_limit_bytes=vmem)


def _mm(a, b, name, tb=False, out_dtype=F32, tm=1024, tn=512, tk=1024, exchange=(), gather=()):
    M, K = a.shape
    if tb:
        N, K2 = b.shape
    else:
        K2, N = b.shape
    assert K == K2, (a.shape, b.shape, tb)
    tm, tn, tk = _fit(M, tm), _fit(N, tn), _fit(K, tk)
    nk = K // tk
    grid = (M // tm, N // tn, nk)
    steps = grid[0] * grid[1] * grid[2]
    dims = NT if tb else NN
    moved = list(exchange) + list(gather)
    n = len(moved)
    assert not (exchange and gather)

    def body(a_ref, b_ref, *refs):
        o_ref, scratch = refs[n], refs[2 * n + 1:]
        step = (pl.program_id(0) * grid[1] + pl.program_id(1)) * grid[2] + pl.program_id(2)
        if exchange:
            start, finish = _exchange_phases(refs[:n], refs[n + 1:2 * n + 1], *scratch[-3:])
            pl.when(step == 0)(start)
        if gather:
            start, forward, finish = _gather_phases(refs[:n], refs[n + 1:2 * n + 1], *scratch[-3:])
            pl.when(step == 0)(start)
            pl.when(step == steps // 2)(forward)
        part = _dot(a_ref[...].astype(BF16), b_ref[...].astype(BF16), dims)
        if nk == 1:
            o_ref[...] = part.astype(out_dtype)
        else:
            acc_ref = scratch[0]
            k = pl.program_id(2)

            @pl.when(k == 0)
            def _():
                acc_ref[...] = part

            @pl.when(k > 0)
            def _():
                acc_ref[...] += part

            @pl.when(k == nk - 1)
            def _():
                o_ref[...] = acc_ref[...].astype(out_dtype)
        if n:
            pl.when(step == steps - 1)(finish)

    a_spec = pl.BlockSpec((tm, tk), lambda i, j, k: (i, k))
    b_spec = pl.BlockSpec((tn, tk), lambda i, j, k: (j, k)) if tb else pl.BlockSpec((tk, tn), lambda i, j, k: (k, j))
    landed = [jax.ShapeDtypeStruct(p.shape, p.dtype) for p in exchange]
    landed += [jax.ShapeDtypeStruct((N_DEV,) + p.shape, p.dtype) for p in gather]
    out = pl.pallas_call(
        body, name=name, grid=grid,
        in_specs=[a_spec, b_spec] + [ANY_SPEC] * n,
        out_specs=[pl.BlockSpec((tm, tn), lambda i, j, k: (i, j))] + [ANY_SPEC] * n,
        out_shape=[jax.ShapeDtypeStruct((M, N), out_dtype)] + landed,
        scratch_shapes=([] if nk == 1 else [pltpu.VMEM((tm, tn), F32)]) + (_comm_sems(n) if n else []),
        compiler_params=_params(*(("arbitrary",) * 3 if n else ("parallel", "parallel", "arbitrary")), vmem=VMEM_BIG),
    )(a, b, *moved)
    return out if n else out[0]


TN_CHUNK = 512


def _mm_tn(a, b, name, tm=512, tk=512):
    K, M = a.shape
    K2, N = b.shape
    assert K == K2, (a.shape, b.shape)
    tm, tk, cn = _fit(M, tm), _fit(K, tk), _fit(N, TN_CHUNK)
    nk = K // tk

    def body(a_ref, b_ref, o_ref, acc_ref):
        k = pl.program_id(1)

        @pl.when(k == 0)
        def _():
            acc_ref[...] = jnp.zeros((tm, N), F32)

        at = a_ref[...].astype(BF16).T
        for c in range(N // cn):
            cols = slice(c * cn, (c + 1) * cn)
            acc_ref[:, cols] += _dot(at, b_ref[:, cols].astype(BF16), NN)

        @pl.when(k == nk - 1)
        def _():
            o_ref[...] = acc_ref[...].astype(BF16)

    return pl.pallas_call(
        body, name=name, grid=(M // tm, nk),
        in_specs=[pl.BlockSpec((tk, tm), lambda i, k: (k, i)), pl.BlockSpec((tk, N), lambda i, k: (k, 0))],
        out_specs=pl.BlockSpec((tm, N), lambda i, k: (i, 0)),
        out_shape=jax.ShapeDtypeStruct((M, N), BF16),
        scratch_shapes=[pltpu.VMEM((tm, N), F32)],
        compiler_params=_params("parallel", "arbitrary", vmem=VMEM_BIG))(a, b)


def _row(tl, n, col=0):
    return pl.BlockSpec((tl, n), lambda i: (i, col))


def _full(shape):
    return pl.BlockSpec(shape, lambda i: (0,) * len(shape))


def _rms(x, g):
    r = lax.rsqrt(jnp.mean(x * x, axis=-1, keepdims=True) + EPS)
    return x * r * g


def _rms_bwd(x, g, dy):
    n = x.shape[-1]
    r = lax.rsqrt(jnp.mean(x * x, axis=-1, keepdims=True) + EPS)
    gy = dy * g
    dx = r * gy - x * (r * r * r * (1.0 / n)) * jnp.sum(x * gy, axis=-1, keepdims=True)
    return dx, jnp.sum(dy * x * r, axis=0, keepdims=True)


def _acc(ref, first, val):
    @pl.when(first)
    def _():
        ref[...] = val

    @pl.when(jnp.logical_not(first))
    def _():
        ref[...] += val


def _rms_fwd_call(x, g, name):
    L, n = x.shape
    tl = _fit(L, 512)

    def body(x_ref, g_ref, o_ref):
        o_ref[...] = _rms(x_ref[...], g_ref[...]).astype(BF16)

    return pl.pallas_call(
        body, name=name, grid=(L // tl,), in_specs=[_row(tl, n), _full((1, n))], out_specs=_row(tl, n),
        out_shape=jax.ShapeDtypeStruct((L, n), BF16), compiler_params=_params("parallel"))(x, g)


def _mla_norms_call(proj, q_norm, kv_norm):
    L = proj.shape[0]
    tl = _fit(L, 512)

    def body(p_ref, gq_ref, gk_ref, qn_ref, kn_ref):
        p = p_ref[...]
        qn_ref[...] = _rms(p[:, P_CQ:P_CKV], gq_ref[...]).astype(BF16)
        kn_ref[...] = _rms(p[:, P_CKV:P_KR], gk_ref[...]).astype(BF16)

    return pl.pallas_call(
        body, name="mla_norms", grid=(L // tl,),
        in_specs=[_row(tl, P_KR), _full((1, Q_RANK)), _full((1, KV_RANK))],
        out_specs=[_row(tl, Q_RANK), _row(tl, KV_RANK)],
        out_shape=[jax.ShapeDtypeStruct((L, Q_RANK), BF16), jax.ShapeDtypeStruct((L, KV_RANK), BF16)],
        compiler_params=_params("parallel"))(proj, q_norm, kv_norm)


def _rope_lanes(shape):
    lane = lax.broadcasted_iota(jnp.int32, shape, 1)
    return lane, jnp.logical_and(lane >= QK_NOPE, lane < QK_HEAD)


def _rope_apply(x, cosf, sinf, lane):
    rot = jnp.where(lane < QK_NOPE + QK_ROPE // 2, -pltpu.roll(x, LANES - QK_ROPE // 2, 1), pltpu.roll(x, QK_ROPE // 2, 1))
    return x * cosf + rot * sinf


def _rope_apply_t(dy, cosf, sinf, lane, is_rope):
    g = dy * sinf
    rot_t = jnp.where(lane < QK_NOPE + QK_ROPE // 2, pltpu.roll(g, LANES - QK_ROPE // 2, 1), -pltpu.roll(g, QK_ROPE // 2, 1))
    return dy * cosf + jnp.where(is_rope, rot_t, 0.0)


def _mla_prep_call(q_pad, kv_pad, proj, pos_col, inv_freq):
    L = q_pad.shape[0]
    tl = _fit(L, 512)

    def body(q_ref, kv_ref, kr_ref, pos_ref, f_ref, qo_ref, kvo_ref, cos_ref, sin_ref):
        lane, is_rope = _rope_lanes((tl, LANES))
        ang = pos_ref[...] * f_ref[...]
        cosf = jnp.where(is_rope, jnp.cos(ang), jnp.where(lane < QK_NOPE, 1.0, 0.0))
        sinf = jnp.where(is_rope, jnp.sin(ang), 0.0)
        cos_ref[...] = cosf
        sin_ref[...] = sinf
        kr = _rope_apply(kr_ref[...], cosf, sinf, lane)
        for h in range(N_HEADS):
            qh = _rope_apply(q_ref[:, h * LANES:(h + 1) * LANES], cosf, sinf, lane)
            qo_ref[:, h * LANES:(h + 1) * LANES] = (qh * Q_PRESCALE).astype(BF16)
            kvo_ref[:, 2 * h * LANES:(2 * h + 1) * LANES] = (kv_ref[:, 2 * h * LANES:(2 * h + 1) * LANES] + kr).astype(BF16)
            vh = jnp.where(lane == V_HEAD, 1.0, kv_ref[:, (2 * h + 1) * LANES:(2 * h + 2) * LANES])
            kvo_ref[:, (2 * h + 1) * LANES:(2 * h + 2) * LANES] = vh.astype(BF16)

    return pl.pallas_call(
        body, name="mla_prep", grid=(L // tl,),
        in_specs=[_row(tl, HEAD_PAD), _row(tl, 2 * HEAD_PAD), _row(tl, LANES, P_KR // LANES), _row(tl, 1), _full((1, LANES))],
        out_specs=[_row(tl, HEAD_PAD), _row(tl, 2 * HEAD_PAD), _row(tl, LANES), _row(tl, LANES)],
        out_shape=[jax.ShapeDtypeStruct((L, HEAD_PAD), BF16), jax.ShapeDtypeStruct((L, 2 * HEAD_PAD), BF16),
                   jax.ShapeDtypeStruct((L, LANES), F32), jax.ShapeDtypeStruct((L, LANES), F32)],
        compiler_params=_params("parallel"))(q_pad, kv_pad, proj, pos_col, inv_freq)


def _mla_prep_bwd_call(dq, dkv, cosf, sinf):
    L = dq.shape[0]
    tl = _fit(L, 512)

    def body(dq_ref, dkv_ref, cos_ref, sin_ref, dqo_ref, dkvo_ref, dkr_ref):
        lane, is_rope = _rope_lanes((tl, LANES))
        cosf, sinf = cos_ref[...], sin_ref[...]
        dk_sum = jnp.zeros((tl, LANES), F32)
        for h in range(N_HEADS):
            dqo_ref[:, h * LANES:(h + 1) * LANES] = _rope_apply_t(dq_ref[:, h * LANES:(h + 1) * LANES], cosf, sinf, lane, is_rope).astype(BF16)
            dk_sum = dk_sum + dkv_ref[:, 2 * h * LANES:(2 * h + 1) * LANES]
        dkvo_ref[...] = dkv_ref[...].astype(BF16)
        dkr_ref[...] = _rope_apply_t(dk_sum, cosf, sinf, lane, is_rope).astype(BF16)

    return pl.pallas_call(
        body, name="mla_prep_bwd", grid=(L // tl,),
        in_specs=[_row(tl, HEAD_PAD), _row(tl, 2 * HEAD_PAD), _row(tl, LANES), _row(tl, LANES)],
        out_specs=[_row(tl, HEAD_PAD), _row(tl, 2 * HEAD_PAD), _row(tl, LANES)],
        out_shape=[jax.ShapeDtypeStruct((L, HEAD_PAD), BF16), jax.ShapeDtypeStruct((L, 2 * HEAD_PAD), BF16),
                   jax.ShapeDtypeStruct((L, LANES), BF16)],
        compiler_params=_params("parallel"))(dq, dkv, cosf, sinf)


def _mla_norms_bwd_call(proj, dqn, dkn, q_norm, kv_norm):
    L = proj.shape[0]
    tl = _fit(L, 512)

    def body(p_ref, dqn_ref, dkn_ref, gq_ref, gk_ref, d_ref, dgq_ref, dgk_ref):
        first = pl.program_id(0) == 0
        p = p_ref[...]
        dq, dgq = _rms_bwd(p[:, P_CQ:P_CKV], gq_ref[...], dqn_ref[...])
        dk, dgk = _rms_bwd(p[:, P_CKV:P_KR], gk_ref[...], dkn_ref[...])
        d_ref[:, P_CQ:P_CKV] = dq.astype(BF16)
        d_ref[:, P_CKV:P_KR] = dk.astype(BF16)
        _acc(dgq_ref, first, dgq)
        _acc(dgk_ref, first, dgk)

    return pl.pallas_call(
        body, name="mla_norms_bwd", grid=(L // tl,),
        in_specs=[_row(tl, P_KR), _row(tl, Q_RANK), _row(tl, KV_RANK), _full((1, Q_RANK)), _full((1, KV_RANK))],
        out_specs=[_row(tl, P_KR), _full((1, Q_RANK)), _full((1, KV_RANK))],
        out_shape=[jax.ShapeDtypeStruct((L, P_KR), BF16), jax.ShapeDtypeStruct((1, Q_RANK), F32),
                   jax.ShapeDtypeStruct((1, KV_RANK), F32)],
        compiler_params=_params("arbitrary"))(proj, dqn, dkn, q_norm, kv_norm)


GATE_TILE = 256


def _merge_call(proj, b_gate, pa, ps):
    L = proj.shape[0]
    tl = _fit(L, 512)
    nc = D_MODEL // GATE_TILE
    g0, g1 = P_GATE // GATE_TILE, (P_GATE + D_MODEL) // GATE_TILE

    def body(l0_ref, l1_ref, b0_ref, b1_ref, pa_ref, ps_ref, o_ref):
        s0 = _sigmoid(l0_ref[...] + b0_ref[...])
        s1 = _sigmoid(l1_ref[...] + b1_ref[...])
        o_ref[...] = (s0 * pa_ref[...] + s1 * ps_ref[...]).astype(BF16)

    blk = lambda off: pl.BlockSpec((tl, GATE_TILE), lambda i, j: (i, off + j))
    bias = lambda off: pl.BlockSpec((1, GATE_TILE), lambda i, j: (0, off + j))
    return pl.pallas_call(
        body, name="merge", grid=(L // tl, nc),
        in_specs=[blk(g0), blk(g1), bias(0), bias(nc), blk(0), blk(0)],
        out_specs=blk(0), out_shape=jax.ShapeDtypeStruct((L, D_MODEL), BF16),
        compiler_params=_params("parallel", "parallel"))(proj, proj, b_gate, b_gate, pa, ps)


def _merge_bwd_call(dm, proj, b_gate, pa, ps):
    L = proj.shape[0]
    tl = _fit(L, 512)
    nc = D_MODEL // GATE_TILE
    g0, g1 = P_GATE // GATE_TILE, (P_GATE + D_MODEL) // GATE_TILE

    def body(dm_ref, l0_ref, l1_ref, b0_ref, b1_ref, pa_ref, ps_ref, dpa_ref, dps_ref, dl0_ref, dl1_ref, db0_ref, db1_ref):
        first = pl.program_id(1) == 0
        dm_ = dm_ref[...]
        s0 = _sigmoid(l0_ref[...] + b0_ref[...])
        s1 = _sigmoid(l1_ref[...] + b1_ref[...])
        dpa_ref[...] = (dm_ * s0).astype(BF16)
        dps_ref[...] = (dm_ * s1).astype(BF16)
        dl0 = dm_ * pa_ref[...] * s0 * (1.0 - s0)
        dl1 = dm_ * ps_ref[...] * s1 * (1.0 - s1)
        dl0_ref[...] = dl0.astype(BF16)
        dl1_ref[...] = dl1.astype(BF16)
        _acc(db0_ref, first, jnp.sum(dl0, axis=0, keepdims=True))
        _acc(db1_ref, first, jnp.sum(dl1, axis=0, keepdims=True))

    blk = lambda off: pl.BlockSpec((tl, GATE_TILE), lambda j, i: (i, off + j))
    bias = lambda off: pl.BlockSpec((1, GATE_TILE), lambda j, i: (0, off + j))
    act = jax.ShapeDtypeStruct((L, D_MODEL), BF16)
    vec = jax.ShapeDtypeStruct((1, D_MODEL), F32)
    return pl.pallas_call(
        body, name="merge_bwd", grid=(nc, L // tl),
        in_specs=[blk(0), blk(g0), blk(g1), bias(0), bias(nc), blk(0), blk(0)],
        out_specs=[blk(0), blk(0), blk(0), blk(0), bias(0), bias(0)],
        out_shape=[act, act, act, act, vec, vec],
        compiler_params=_params("parallel", "arbitrary"))(dm, proj, proj, b_gate, b_gate, pa, ps)


def _post_mix_call(o, x, g_post, g_fpre):
    L, n = x.shape
    tl = _fit(L, 512)

    def body(o_ref, x_ref, gp_ref, gf_ref, x2_ref, hn_ref):
        x2 = x_ref[...] + _rms(o_ref[...], gp_ref[...])
        x2_ref[...] = x2
        hn_ref[...] = _rms(x2, gf_ref[...]).astype(BF16)

    return pl.pallas_call(
        body, name="post_mix", grid=(L // tl,),
        in_specs=[_row(tl, n), _row(tl, n), _full((1, n)), _full((1, n))],
        out_specs=[_row(tl, n), _row(tl, n)],
        out_shape=[jax.ShapeDtypeStruct((L, n), F32), jax.ShapeDtypeStruct((L, n), BF16)],
        compiler_params=_params("parallel"))(o, x, g_post, g_fpre)


def _ffn_out_call(ff, x2, target, g_fpost):
    L, n = x2.shape
    tl = _fit(L, 512)

    def body(ff_ref, x2_ref, t_ref, g_ref, loss_ref, dy_ref, dff_ref, dg_ref):
        first = pl.program_id(0) == 0
        ff_ = ff_ref[...]
        err = x2_ref[...] + _rms(ff_, g_ref[...]) - t_ref[...]
        part = 0.5 * jnp.sum(jnp.sum(err * err, axis=-1, keepdims=True) * (1.0 / n), axis=0, keepdims=True)
        dy = err * (1.0 / n)
        dy_ref[...] = dy
        dff, dg = _rms_bwd(ff_, g_ref[...], dy)
        dff_ref[...] = dff.astype(BF16)
        _acc(loss_ref, first, jnp.broadcast_to(part, (1, LANES)))
        _acc(dg_ref, first, dg)

    return pl.pallas_call(
        body, name="ffn_out", grid=(L // tl,),
        in_specs=[_row(tl, n), _row(tl, n), _row(tl, n), _full((1, n))],
        out_specs=[_full((1, LANES)), _row(tl, n), _row(tl, n), _full((1, n))],
        out_shape=[jax.ShapeDtypeStruct((1, LANES), F32), jax.ShapeDtypeStruct((L, n), F32),
                   jax.ShapeDtypeStruct((L, n), BF16), jax.ShapeDtypeStruct((1, n), F32)],
        compiler_params=_params("arbitrary"))(ff, x2, target, g_fpost)


def _post_bwd_call(x2, dhn2, dy, o, g_fpre, g_post):
    L, n = x2.shape
    tl = _fit(L, 512)

    def body(x2_ref, dh_ref, dy_ref, o_ref, gf_ref, gp_ref, dx2_ref, do_ref, dgf_ref, dgp_ref):
        first = pl.program_id(0) == 0
        d1, dgf = _rms_bwd(x2_ref[...], gf_ref[...], dh_ref[...])
        dx2 = dy_ref[...] + d1
        dx2_ref[...] = dx2
        do, dgp = _rms_bwd(o_ref[...], gp_ref[...], dx2)
        do_ref[...] = do.astype(BF16)
        _acc(dgf_ref, first, dgf)
        _acc(dgp_ref, first, dgp)

    return pl.pallas_call(
        body, name="post_bwd", grid=(L // tl,),
        in_specs=[_row(tl, n), _row(tl, n), _row(tl, n), _row(tl, n), _full((1, n)), _full((1, n))],
        out_specs=[_row(tl, n), _row(tl, n), _full((1, n)), _full((1, n))],
        out_shape=[jax.ShapeDtypeStruct((L, n), F32), jax.ShapeDtypeStruct((L, n), BF16),
                   jax.ShapeDtypeStruct((1, n), F32), jax.ShapeDtypeStruct((1, n), F32)],
        compiler_params=_params("arbitrary"))(x2, dhn2, dy, o, g_fpre, g_post)


def _pre_bwd_call(x, dhn1, dx2, g_pre):
    L, n = x.shape
    tl = _fit(L, 512)

    def body(x_ref, dh_ref, dx2_ref, g_ref, dx_ref, dg_ref):
        first = pl.program_id(0) == 0
        d1, dg = _rms_bwd(x_ref[...], g_ref[...], dh_ref[...])
        dx_ref[...] = dx2_ref[...] + d1
        _acc(dg_ref, first, dg)

    return pl.pallas_call(
        body, name="pre_bwd", grid=(L // tl,),
        in_specs=[_row(tl, n), _row(tl, n), _row(tl, n), _full((1, n))],
        out_specs=[_row(tl, n), _full((1, n))],
        out_shape=[jax.ShapeDtypeStruct((L, n), F32), jax.ShapeDtypeStruct((1, n), F32)],
        compiler_params=_params("arbitrary"))(x, dhn1, dx2, g_pre)


CONV_TILE = 256
HALO = 16


def _conv3(w, b, x0, x1, x2):
    return b + w[2:3] * x0 + w[1:2] * x1 + w[0:1] * x2


def _down(x, by):
    return pltpu.roll(x, by, 0)


def _edge_down(edge, before, by):
    r = lax.broadcasted_iota(jnp.int32, edge.shape, 0)
    return jnp.where(r < by, pltpu.roll(before, by, 0), pltpu.roll(edge, by, 0))


def _edge_up(edge, after, by):
    r = lax.broadcasted_iota(jnp.int32, edge.shape, 0)
    return jnp.where(r >= HALO - by, pltpu.roll(after, HALO - by, 0), pltpu.roll(edge, HALO - by, 0))


def _gated(w_g, b_g, w_v, b_v, hg, hv, g1, g2, v1, v2):
    return _conv3(w_g, b_g, hg, g1, g2), _conv3(w_v, b_v, hv, v1, v2)


def _conv_specs(L, tl, nc, rows_inner):
    nh = tl // HALO
    if rows_inner:
        ij = lambda f: (lambda j, i: f(i, j))
    else:
        ij = lambda f: f
    cur = lambda off: pl.BlockSpec((tl, CONV_TILE), ij(lambda i, j: (i, off + j)))
    prev = lambda off: pl.BlockSpec((HALO, CONV_TILE), ij(lambda i, j: (jnp.maximum(i * nh - 1, 0), off + j)))
    nxt = lambda off: pl.BlockSpec((HALO, CONV_TILE), ij(lambda i, j: (jnp.minimum((i + 1) * nh, L // HALO - 1), off + j)))
    par = lambda rows, off: pl.BlockSpec((rows, CONV_TILE), ij(lambda i, j: (0, off + j)))
    return cur, prev, nxt, par


def _conv_act_call(h, conv_w, conv_b):
    L = h.shape[0]
    tl = _fit(L, 512)
    nc = D_FF // CONV_TILE
    cur, prev, _, par = _conv_specs(L, tl, nc, False)

    def body(hg_ref, hv_ref, pg_ref, pv_ref, wg_ref, wv_ref, bg_ref, bv_ref, a_ref):
        not_first = (pl.program_id(0) > 0).astype(F32)
        par = (wg_ref[...], bg_ref[...], wv_ref[...], bv_ref[...])
        hg, hv = hg_ref[...], hv_ref[...]
        gate, val = _gated(*par, hg, hv, _down(hg, 1), _down(hg, 2), _down(hv, 1), _down(hv, 2))
        a_ref[...] = (_gelu(gate) * val).astype(BF16)
        eg, ev, bg, bv = hg[:HALO], hv[:HALO], pg_ref[...] * not_first, pv_ref[...] * not_first
        gate, val = _gated(*par, eg, ev, _edge_down(eg, bg, 1), _edge_down(eg, bg, 2),
                           _edge_down(ev, bv, 1), _edge_down(ev, bv, 2))
        a_ref[:HALO, :] = (_gelu(gate) * val).astype(BF16)

    return pl.pallas_call(
        body, name="conv_act", grid=(L // tl, nc),
        in_specs=[cur(0), cur(nc), prev(0), prev(nc), par(3, 0), par(3, nc), par(1, 0), par(1, nc)],
        out_specs=cur(0), out_shape=jax.ShapeDtypeStruct((L, D_FF), BF16),
        compiler_params=_params("parallel", "parallel"))(h, h, h, h, conv_w, conv_w, conv_b, conv_b)


def _conv_act_bwd_call(da, h, conv_w, conv_b):
    L = h.shape[0]
    tl = _fit(L, 512)
    nc = D_FF // CONV_TILE
    cur, prev, _, par = _conv_specs(L, tl, nc, True)

    def body(da_ref, hg_ref, hv_ref, pg_ref, pv_ref, wg_ref, wv_ref, bg_ref, bv_ref,
             dg_ref, dv_ref, dwg_ref, dwv_ref, dbg_ref, dbv_ref):
        first = pl.program_id(1) == 0
        not_first = (pl.program_id(1) > 0).astype(F32)
        par = (wg_ref[...], bg_ref[...], wv_ref[...], bv_ref[...])
        col = lambda t: jnp.sum(t, axis=0, keepdims=True)

        def grads(da_, hg, hv, g1, g2, v1, v2):
            gate, val = _gated(*par, hg, hv, g1, g2, v1, v2)
            act, slope = _gelu_and_grad(gate)
            dgate = da_ * val * slope
            dval = da_ * act
            sums = (jnp.concatenate([col(dgate * g2), col(dgate * g1), col(dgate * hg)], axis=0),
                    jnp.concatenate([col(dval * v2), col(dval * v1), col(dval * hv)], axis=0), col(dgate), col(dval))
            return dgate, dval, sums

        da_, hg, hv = da_ref[...], hg_ref[...], hv_ref[...]
        shifted = (_down(hg, 1), _down(hg, 2), _down(hv, 1), _down(hv, 2))
        dgate, dval, whole = grads(da_, hg, hv, *shifted)
        dg_ref[...] = dgate.astype(BF16)
        dv_ref[...] = dval.astype(BF16)
        edge = lambda t: t[:HALO]
        _, _, wrapped = grads(edge(da_), edge(hg), edge(hv), *[edge(s) for s in shifted])
        eg, ev, bg, bv = edge(hg), edge(hv), pg_ref[...] * not_first, pv_ref[...] * not_first
        dgate, dval, fixed = grads(edge(da_), eg, ev, _edge_down(eg, bg, 1), _edge_down(eg, bg, 2),
                                   _edge_down(ev, bv, 1), _edge_down(ev, bv, 2))
        dg_ref[:HALO, :] = dgate.astype(BF16)
        dv_ref[:HALO, :] = dval.astype(BF16)
        for ref, a, b, c in zip((dwg_ref, dwv_ref, dbg_ref, dbv_ref), whole, wrapped, fixed):
            _acc(ref, first, a - b + c)

    act = jax.ShapeDtypeStruct((L, D_FF), BF16)
    w3 = jax.ShapeDtypeStruct((3, D_FF), F32)
    w1 = jax.ShapeDtypeStruct((1, D_FF), F32)
    return pl.pallas_call(
        body, name="conv_act_bwd", grid=(nc, L // tl),
        in_specs=[cur(0), cur(0), cur(nc), prev(0), prev(nc), par(3, 0), par(3, nc), par(1, 0), par(1, nc)],
        out_specs=[cur(0), cur(0), par(3, 0), par(3, 0), par(1, 0), par(1, 0)],
        out_shape=[act, act, w3, w3, w1, w1],
        compiler_params=_params("parallel", "arbitrary"))(da, h, h, h, h, conv_w, conv_w, conv_b, conv_b)


def _conv_t_call(dgate, dval, conv_w):
    L = dgate.shape[0]
    tl = _fit(L, 512)
    nc = D_FF // CONV_TILE
    nh = tl // HALO

    def body(dg_ref, dv_ref, ng_ref, nv_ref, w_ref, o_ref):
        not_last = (pl.program_id(0) < L // tl - 1).astype(F32)

        def emit(d_ref, n_ref):
            c = d_ref[...].astype(F32)
            w = w_ref[...]
            o_ref[...] = _conv3(w, 0.0, c, pltpu.roll(c, tl - 1, 0), pltpu.roll(c, tl - 2, 0)).astype(BF16)
            edge, after = c[tl - HALO:], n_ref[...].astype(F32) * not_last
            o_ref[tl - HALO:, :] = _conv3(w, 0.0, edge, _edge_up(edge, after, 1), _edge_up(edge, after, 2)).astype(BF16)

        pl.when(pl.program_id(1) < nc)(lambda: emit(dg_ref, ng_ref))
        pl.when(pl.program_id(1) >= nc)(lambda: emit(dv_ref, nv_ref))

    gate_col = lambda j: jnp.minimum(j, nc - 1)
    val_col = lambda j: jnp.maximum(j - nc, 0)
    after_row = lambda i: jnp.minimum((i + 1) * nh, L // HALO - 1)
    tile = lambda col: pl.BlockSpec((tl, CONV_TILE), lambda i, j: (i, col(j)))
    after = lambda col: pl.BlockSpec((HALO, CONV_TILE), lambda i, j: (after_row(i), col(j)))
    return pl.pallas_call(
        body, name="conv_t", grid=(L // tl, 2 * nc),
        in_specs=[tile(gate_col), tile(val_col), after(gate_col), after(val_col), pl.BlockSpec((3, CONV_TILE), lambda i, j: (0, j))],
        out_specs=pl.BlockSpec((tl, CONV_TILE), lambda i, j: (i, j)),
        out_shape=jax.ShapeDtypeStruct((L, 2 * D_FF), BF16),
        compiler_params=_params("parallel", "parallel"))(dgate, dval, dgate, dval, conv_w)


def _glu_call(y1, w_glu, b_glu):
    L, n = y1.shape
    tl = _fit(L, 512)

    def body(y_ref, w_ref, b_ref, o_ref):
        y2 = _gelu(y_ref[...])
        z = _dot(y2.astype(BF16), w_ref[...], NN) + b_ref[...]
        o_ref[...] = (y2 * _sigmoid(z)).astype(BF16)

    return pl.pallas_call(
        body, name="glu", grid=(L // tl,), in_specs=[_row(tl, n), _full((n, n)), _full((1, n))],
        out_specs=_row(tl, n), out_shape=jax.ShapeDtypeStruct((L, n), BF16),
        compiler_params=_params("parallel"))(y1, w_glu, b_glu)


def _glu_bwd_call(dout, y1, w_glu, b_glu):
    L, n = y1.shape
    tl = _fit(L, 512)

    def body(do_ref, y_ref, w_ref, b_ref, dy_ref, dw_ref, db_ref):
        first = pl.program_id(0) == 0
        y1_ = y_ref[...]
        y2, slope = _gelu_and_grad(y1_)
        y2b = y2.astype(BF16)
        w = w_ref[...]
        sg = _sigmoid(_dot(y2b, w, NN) + b_ref[...])
        dout_ = do_ref[...].astype(F32)
        dz = dout_ * y2 * sg * (1.0 - sg)
        dzb = dz.astype(BF16)
        dy2 = dout_ * sg + _dot(dzb, w, NT)
        dy_ref[...] = dy2 * slope
        _acc(dw_ref, first, _dot(y2b, dzb, TN))
        _acc(db_ref, first, jnp.sum(dz, axis=0, keepdims=True))

    return pl.pallas_call(
        body, name="glu_bwd", grid=(L // tl,),
        in_specs=[_row(tl, n), _row(tl, n), _full((n, n)), _full((1, n))],
        out_specs=[_row(tl, n), _full((n, n)), _full((1, n))],
        out_shape=[jax.ShapeDtypeStruct((L, n), F32), jax.ShapeDtypeStruct((n, n), F32), jax.ShapeDtypeStruct((1, n), F32)],
        compiler_params=_params("arbitrary"))(dout, y1, w_glu, b_glu)


ATTN_TILE = 1024
ATTN_SCALE = 1.0 / math.sqrt(QK_HEAD)


ATTN_HEADS = 2
ATTN_GROUPS = N_HEADS // ATTN_HEADS
LOG2E = 1.0 / math.log(2.0)
Q_PRESCALE = ATTN_SCALE * LOG2E
ANY_SPEC = pl.BlockSpec(memory_space=pl.ANY)


def _attn_fwd_call(q, kv, blocks):
    L = q.shape[0]
    t = _fit(L, ATTN_TILE)
    nq = L // t
    n = len(blocks)

    def body(q_ref, kv_ref, *refs):
        blk_refs, (o_ref, lse_ref), gat_refs = refs[:n], refs[n:n + 2], refs[n + 2:2 * n + 2]
        m_s, acc_s, send_sems, recv_sems, local_sems = refs[2 * n + 2:]
        g, i = pl.program_id(0), pl.program_id(1)
        start, forward, finish = _gather_phases(blk_refs, gat_refs, send_sems, recv_sems, local_sems)
        pl.when(jnp.logical_and(g == 0, i == 0))(start)
        m_s[...] = jnp.full((ATTN_HEADS, t, 1), NEG, F32)
        acc_s[...] = jnp.zeros((ATTN_HEADS, t, LANES), F32)
        below = lax.broadcasted_iota(jnp.int32, (t, t), 1) <= lax.broadcasted_iota(jnp.int32, (t, t), 0)

        def block_step(kb, on_diagonal):
            rows = pl.ds(pl.multiple_of(kb * t, t), t)
            for a in range(ATTN_HEADS):
                s = _dot(q_ref[:, a * LANES:(a + 1) * LANES], kv_ref[rows, 2 * a * LANES:(2 * a + 1) * LANES], NT)
                if on_diagonal:
                    s = jnp.where(below, s, NEG)
                m_prev = m_s[a]
                m_new = jnp.maximum(m_prev, jnp.max(s, axis=1, keepdims=True))
                p = jnp.exp2(s - m_new)
                pv = _dot(p.astype(BF16), kv_ref[rows, (2 * a + 1) * LANES:(2 * a + 2) * LANES], NN)
                acc_s[a] = jnp.exp2(m_prev - m_new) * acc_s[a] + pv
                m_s[a] = m_new

        def step(kb, carry):
            block_step(kb, False)
            return carry

        lax.fori_loop(0, i, step, 0)
        block_step(i, True)
        lane = lax.broadcasted_iota(jnp.int32, (t, LANES), 1)
        for a in range(ATTN_HEADS):
            acc = acc_s[a]
            l = jnp.sum(jnp.where(lane == V_HEAD, acc, 0.0), axis=1, keepdims=True)
            o_ref[:, a * LANES:(a + 1) * LANES] = (acc / l).astype(BF16)
            lse_ref[a] = m_s[a] + jnp.log(l) * LOG2E
        pl.when(jnp.logical_and(g == (3 * ATTN_GROUPS) // 4, i == 0))(forward)
        pl.when(jnp.logical_and(g == ATTN_GROUPS - 1, i == nq - 1))(finish)

    gw = ATTN_HEADS * LANES
    return pl.pallas_call(
        body, name="attn_fwd", grid=(ATTN_GROUPS, nq),
        in_specs=[pl.BlockSpec((t, gw), lambda g, i: (i, g)),
                  pl.BlockSpec((L, 2 * gw), lambda g, i: (0, g))] + [ANY_SPEC] * n,
        out_specs=[pl.BlockSpec((t, gw), lambda g, i: (i, g)),
                   pl.BlockSpec((ATTN_HEADS, t, 1), lambda g, i: (g, i, 0))] + [ANY_SPEC] * n,
        out_shape=[jax.ShapeDtypeStruct((L, HEAD_PAD), BF16), jax.ShapeDtypeStruct((N_HEADS, L, 1), F32)]
        + [jax.ShapeDtypeStruct((N_DEV,) + b.shape, b.dtype) for b in blocks],
        scratch_shapes=[pltpu.VMEM((ATTN_HEADS, t, 1), F32), pltpu.VMEM((ATTN_HEADS, t, LANES), F32)] + _comm_sems(n),
        compiler_params=_params("arbitrary", "arbitrary", vmem=VMEM_BIG))(q, kv, *blocks)


def _attn_bwd_call(q, kv, o, do, lse, parts, blocks):
    L = q.shape[0]
    t = _fit(L, ATTN_TILE)
    nq = L // t
    n1, n = len(parts), len(parts) + len(blocks)

    def body(q_ref, do_ref, o_ref, lse_ref, kv_ref, *refs):
        in_refs, (dq_ref, dkv_ref), out_refs = refs[:n], refs[n:n + 2], refs[n + 2:2 * n + 2]
        dk_s, dv_s = refs[2 * n + 2:2 * n + 4]
        g, j = pl.program_id(0), pl.program_id(1)
        start, finish = _exchange_phases(in_refs[:n1], out_refs[:n1], *refs[2 * n + 4:2 * n + 7])
        start_blocks, finish_blocks = _exchange_phases(in_refs[n1:], out_refs[n1:], *refs[2 * n + 7:], same_source=True)

        @pl.when(jnp.logical_and(g == 0, j == 0))
        def _():
            start()
            start_blocks()

        @pl.when(j == 0)
        def _():
            dq_ref[...] = jnp.zeros((L, ATTN_HEADS * LANES), F32)

        dk_s[...] = jnp.zeros((ATTN_HEADS, t, LANES), F32)
        dv_s[...] = jnp.zeros((ATTN_HEADS, t, LANES), F32)
        below = lax.broadcasted_iota(jnp.int32, (t, t), 1) <= lax.broadcasted_iota(jnp.int32, (t, t), 0)

        def block_step(i, on_diagonal):
            rows = pl.ds(pl.multiple_of(i * t, t), t)
            for a in range(ATTN_HEADS):
                lanes = slice(a * LANES, (a + 1) * LANES)
                qi = q_ref[rows, lanes]
                doi = do_ref[rows, lanes]
                kblk = kv_ref[:, 2 * a * LANES:(2 * a + 1) * LANES]
                delta = jnp.sum(doi.astype(F32) * o_ref[rows, lanes].astype(F32), axis=1, keepdims=True)
                s = _dot(qi, kblk, NT)
                if on_diagonal:
                    s = jnp.where(below, s, NEG)
                p = jnp.exp2(s - lse_ref[a, rows, :])
                dv_s[a] += _dot(p.astype(BF16), doi, TN)
                ds = (p * (_dot(doi, kv_ref[:, (2 * a + 1) * LANES:(2 * a + 2) * LANES], NT) - delta)).astype(BF16)
                dk_s[a] += _dot(ds, qi, TN)
                dq_ref[rows, lanes] += _dot(ds, kblk, NN) * ATTN_SCALE

        def step(i, carry):
            block_step(i, False)
            return carry

        block_step(j, True)
        lax.fori_loop(j + 1, nq, step, 0)
        for a in range(ATTN_HEADS):
            dkv_ref[:, 2 * a * LANES:(2 * a + 1) * LANES] = dk_s[a] * (1.0 / LOG2E)
            dkv_ref[:, (2 * a + 1) * LANES:(2 * a + 2) * LANES] = dv_s[a]
        @pl.when(jnp.logical_and(g == ATTN_GROUPS - 1, j == nq - 1))
        def _():
            finish()
            finish_blocks()

    gw = ATTN_HEADS * LANES
    whole = lambda: pl.BlockSpec((L, gw), lambda g, j: (0, g))
    acc = pltpu.VMEM((ATTN_HEADS, t, LANES), F32)
    return pl.pallas_call(
        body, name="attn_bwd", grid=(ATTN_GROUPS, nq),
        in_specs=[whole(), whole(), whole(), pl.BlockSpec((ATTN_HEADS, L, 1), lambda g, j: (g, 0, 0)),
                  pl.BlockSpec((t, 2 * gw), lambda g, j: (j, g))] + [ANY_SPEC] * n,
        out_specs=[whole(), pl.BlockSpec((t, 2 * gw), lambda g, j: (j, g))] + [ANY_SPEC] * n,
        out_shape=[jax.ShapeDtypeStruct((L, HEAD_PAD), F32), jax.ShapeDtypeStruct((L, 2 * HEAD_PAD), F32)]
        + [jax.ShapeDtypeStruct(p.shape, p.dtype) for p in parts]
        + [jax.ShapeDtypeStruct((N_DEV,) + b.shape, b.dtype) for b in blocks],
        scratch_shapes=[acc, acc] + _comm_sems(n1) + _comm_sems(n - n1),
        compiler_params=_params("arbitrary", "arbitrary", vmem=VMEM_BIG))(q, do, o, lse, kv, *parts, *blocks)


def _disc(lr, li, ldt, br, bi):
    dt = jnp.exp(ldt)
    mag = jnp.exp(lr * dt)
    ang = li * dt
    a_re, a_im = mag * jnp.cos(ang), mag * jnp.sin(ang)
    den = lr * lr + li * li
    n_re, n_im = a_re - 1.0, a_im
    z_re = (n_re * lr + n_im * li) / den
    z_im = (n_im * lr - n_re * li) / den
    return a_re, a_im, z_re * br - z_im * bi, z_re * bi + z_im * br


def _disc_call(lr, li, ldt, br, bi):
    def body(lr_ref, li_ref, ldt_ref, br_ref, bi_ref, ar_ref, ai_ref, bbr_ref, bbi_ref):
        ar_ref[...], ai_ref[...], bbr_ref[...], bbi_ref[...] = _disc(
            lr_ref[...], li_ref[...], ldt_ref[...], br_ref[...], bi_ref[...])

    c1 = jax.ShapeDtypeStruct((SSM_NSTATE, 1), F32)
    c16 = jax.ShapeDtypeStruct((SSM_NSTATE, SSM_GROUP), F32)
    return pl.pallas_call(body, name="ssm_disc", out_shape=[c1, c1, c16, c16])(lr, li, ldt, br, bi)


def _disc_bwd_call(lr, li, ldt, br, bi, dar, dai, dbbr, dbbi):
    def body(lr_ref, li_ref, ldt_ref, br_ref, bi_ref, dar_ref, dai_ref, dbbr_ref, dbbi_ref,
             dlr_ref, dli_ref, dldt_ref, dbr_ref, dbi_ref):
        _, vjp = jax.vjp(_disc, lr_ref[...], li_ref[...], ldt_ref[...], br_ref[...], bi_ref[...])
        dlr_ref[...], dli_ref[...], dldt_ref[...], dbr_ref[...], dbi_ref[...] = vjp(
            (dar_ref[...], dai_ref[...], dbbr_ref[...], dbbi_ref[...]))

    c1 = jax.ShapeDtypeStruct((SSM_NSTATE, 1), F32)
    c16 = jax.ShapeDtypeStruct((SSM_NSTATE, SSM_GROUP), F32)
    return pl.pallas_call(body, name="ssm_disc_bwd", out_shape=[c1, c1, c1, c16, c16])(
        lr, li, ldt, br, bi, dar, dai, dbbr, dbbi)


SSM_ROWS = 512
SSM_CW = SSM_NSTATE // SSM_CHUNKS
SSM_CU = SSM_WIDTH // SSM_CHUNKS


def _cmul(ar, ai, br, bi):
    return ar * br - ai * bi, ar * bi + ai * br


def _power(ar1, ai1, n):
    def step(_, c):
        return _cmul(c[0], c[1], ar1, ai1)

    return lax.fori_loop(0, n, step, (jnp.ones_like(ar1), jnp.zeros_like(ar1)))


def _tile(k):
    return pl.ds(pl.multiple_of(k * 8, 8), 8)


def _ssm_fwd_call(u, a_re, a_im, bb_re, bb_im, cm_re, cm_im, d_skip):
    L = u.shape[0]
    seg = L // 8
    rb = _fit(L, SSM_ROWS)

    def body(u_ref, ar_ref, ai_ref, bbr_ref, bbi_ref, cmr_ref, cmi_ref, d_ref, y_ref, sre_hbm, sim_hbm,
             s_re, s_im, sems):
        q = pl.program_id(0)

        def bu_step(r, c):
            rows = pl.ds(pl.multiple_of(r * rb, rb), rb)
            ub = u_ref[rows, :].astype(BF16)
            s_re[rows, :] = _dot(ub, bbr_ref[0], NN)
            s_im[rows, :] = _dot(ub, bbi_ref[0], NN)
            return c

        lax.fori_loop(0, L // rb, bu_step, 0)
        ar1, ai1 = ar_ref[...], ai_ref[...]
        ar = jnp.broadcast_to(ar1, (8, SSM_CW))
        ai = jnp.broadcast_to(ai1, (8, SSM_CW))

        def local(k, c):
            nr, ni = _cmul(ar, ai, c[0], c[1])
            nr = nr + s_re[_tile(k), :]
            ni = ni + s_im[_tile(k), :]
            s_re[_tile(k), :] = nr
            s_im[_tile(k), :] = ni
            return nr, ni

        zero8 = jnp.zeros((8, SSM_CW), F32)
        lax.fori_loop(0, seg, local, (zero8, zero8))
        pr, pi = _power(ar1, ai1, seg)
        end_r = s_re[pl.ds((seg - 1) * 8, 8), :]
        end_i = s_im[pl.ds((seg - 1) * 8, 8), :]
        er = jnp.zeros((1, SSM_CW), F32)
        ei = jnp.zeros((1, SSM_CW), F32)
        rows_r, rows_i = [er], [ei]
        for j in range(7):
            tr, ti = _cmul(pr, pi, er, ei)
            er, ei = end_r[j:j + 1] + tr, end_i[j:j + 1] + ti
            rows_r.append(er)
            rows_i.append(ei)
        e_r = jnp.concatenate(rows_r, axis=0)
        e_i = jnp.concatenate(rows_i, axis=0)

        def fix(k, c):
            wr, wi = _cmul(c[0], c[1], ar, ai)
            fr, fi = _cmul(wr, wi, e_r, e_i)
            s_re[_tile(k), :] += fr
            s_im[_tile(k), :] += fi
            return wr, wi

        lax.fori_loop(0, seg, fix, (jnp.ones((8, SSM_CW), F32), zero8))
        out_r = pltpu.make_async_copy(s_re, sre_hbm.at[q], sems.at[0])
        out_i = pltpu.make_async_copy(s_im, sim_hbm.at[q], sems.at[1])
        out_r.start()
        out_i.start()

        def y_step(r, c):
            rows = pl.ds(pl.multiple_of(r * rb, rb), rb)
            y = _dot(s_re[rows, :].astype(BF16), cmr_ref[0], NN) - _dot(s_im[rows, :].astype(BF16), cmi_ref[0], NN)
            y_ref[rows, :] = y + d_ref[...] * u_ref[rows, :]
            return c

        lax.fori_loop(0, L // rb, y_step, 0)
        out_r.wait()
        out_i.wait()

    chunk = lambda rows, cols: pl.BlockSpec((rows, cols), lambda q: (0, q))
    mat = lambda r, c: pl.BlockSpec((1, r, c), lambda q: (q, 0, 0))
    anyspec = pl.BlockSpec(memory_space=pl.ANY)
    states = jax.ShapeDtypeStruct((SSM_CHUNKS, L, SSM_CW), F32)
    return pl.pallas_call(
        body, name="ssm_fwd", grid=(SSM_CHUNKS,),
        in_specs=[chunk(L, SSM_CU), chunk(1, SSM_CW), chunk(1, SSM_CW), mat(SSM_CU, SSM_CW), mat(SSM_CU, SSM_CW),
                  mat(SSM_CW, SSM_CU), mat(SSM_CW, SSM_CU), chunk(1, SSM_CU)],
        out_specs=[chunk(L, SSM_CU), anyspec, anyspec],
        out_shape=[jax.ShapeDtypeStruct((L, SSM_WIDTH), F32), states, states],
        scratch_shapes=[pltpu.VMEM((L, SSM_CW), F32), pltpu.VMEM((L, SSM_CW), F32), pltpu.SemaphoreType.DMA((2,))],
        compiler_params=_params("arbitrary", vmem=VMEM_BIG))(u, a_re, a_im, bb_re, bb_im, cm_re, cm_im, d_skip)


def _ssm_bwd_call(dy, u, s_re_all, s_im_all, a_re, a_im, bb_re, bb_im, cm_re, cm_im, d_skip):
    L = u.shape[0]
    seg = L // 8
    rb = _fit(L, SSM_ROWS)

    def body(dy_ref, u_ref, sre_hbm, sim_hbm, ar_ref, ai_ref, bbr_ref, bbi_ref, cmr_ref, cmi_ref, d_ref,
             du_ref, dbbr_ref, dbbi_ref, dcmr_ref, dcmi_ref, dar_ref, dai_ref, dd_ref,
             g_re, g_im, s_re, s_im, sems):
        q = pl.program_id(0)
        in_r = pltpu.make_async_copy(sre_hbm.at[q], s_re, sems.at[0])
        in_i = pltpu.make_async_copy(sim_hbm.at[q], s_im, sems.at[1])
        in_r.start()
        in_i.start()

        def ds_step(r, c):
            rows = pl.ds(pl.multiple_of(r * rb, rb), rb)
            dyb = dy_ref[rows, :].astype(BF16)
            g_re[rows, :] = _dot(dyb, cmr_ref[0], NT)
            g_im[rows, :] = -_dot(dyb, cmi_ref[0], NT)
            return c

        lax.fori_loop(0, L // rb, ds_step, 0)
        ar1, ai1 = ar_ref[...], ai_ref[...]
        ar = jnp.broadcast_to(ar1, (8, SSM_CW))
        nai = jnp.broadcast_to(-ai1, (8, SSM_CW))

        def local(kk, c):
            k = seg - 1 - kk
            nr, ni = _cmul(ar, nai, c[0], c[1])
            nr = nr + g_re[_tile(k), :]
            ni = ni + g_im[_tile(k), :]
            g_re[_tile(k), :] = nr
            g_im[_tile(k), :] = ni
            return nr, ni

        zero8 = jnp.zeros((8, SSM_CW), F32)
        lax.fori_loop(0, seg, local, (zero8, zero8))
        pr, pi = _power(ar1, -ai1, seg)
        head_r = g_re[pl.ds(0, 8), :]
        head_i = g_im[pl.ds(0, 8), :]
        fr = jnp.zeros((1, SSM_CW), F32)
        fi = jnp.zeros((1, SSM_CW), F32)
        rows_r, rows_i = [fr], [fi]
        for j in range(6, -1, -1):
            tr, ti = _cmul(pr, pi, fr, fi)
            fr, fi = head_r[j + 1:j + 2] + tr, head_i[j + 1:j + 2] + ti
            rows_r.insert(0, fr)
            rows_i.insert(0, fi)
        f_r = jnp.concatenate(rows_r, axis=0)
        f_i = jnp.concatenate(rows_i, axis=0)
        in_r.wait()
        in_i.wait()

        def fixed(k, wr, wi):
            xr, xi = _cmul(wr, wi, f_r, f_i)
            gr = g_re[_tile(k), :] + xr
            gi = g_im[_tile(k), :] + xi
            g_re[_tile(k), :] = gr
            g_im[_tile(k), :] = gi
            return gr, gi

        def fix(kk, c):
            k = seg - 1 - kk
            wr, wi = _cmul(c[0], c[1], ar, nai)
            gr, gi = fixed(k, wr, wi)
            pr_, pi_ = s_re[_tile(k - 1), :], s_im[_tile(k - 1), :]
            return wr, wi, c[2] + gr * pr_ + gi * pi_, c[3] + gi * pr_ - gr * pi_

        wr, wi, acc_r, acc_i = lax.fori_loop(0, seg - 1, fix, (jnp.ones((8, SSM_CW), F32), zero8, zero8, zero8))
        wr, wi = _cmul(wr, wi, ar, nai)
        gr, gi = fixed(0, wr, wi)
        row8 = lax.broadcasted_iota(jnp.int32, (8, SSM_CW), 0)
        pr_ = jnp.where(row8 > 0, pltpu.roll(s_re[pl.ds((seg - 1) * 8, 8), :], 1, 0), 0.0)
        pi_ = jnp.where(row8 > 0, pltpu.roll(s_im[pl.ds((seg - 1) * 8, 8), :], 1, 0), 0.0)
        acc_r = acc_r + gr * pr_ + gi * pi_
        acc_i = acc_i + gi * pr_ - gr * pi_
        dar_ref[...] = jnp.sum(acc_r, axis=0, keepdims=True)
        dai_ref[...] = jnp.sum(acc_i, axis=0, keepdims=True)

        dbbr_ref[...] = jnp.zeros((1, SSM_CU, SSM_CW), F32)
        dbbi_ref[...] = jnp.zeros((1, SSM_CU, SSM_CW), F32)
        dcmr_ref[...] = jnp.zeros((1, SSM_CW, SSM_CU), F32)
        dcmi_ref[...] = jnp.zeros((1, SSM_CW, SSM_CU), F32)
        dd_ref[...] = jnp.zeros((1, SSM_CU), F32)

        def grad_step(r, c):
            rows = pl.ds(pl.multiple_of(r * rb, rb), rb)
            ub, dyv = u_ref[rows, :], dy_ref[rows, :]
            ubb, dyb = ub.astype(BF16), dyv.astype(BF16)
            grb, gib = g_re[rows, :].astype(BF16), g_im[rows, :].astype(BF16)
            dbbr_ref[0] += _dot(ubb, grb, TN)
            dbbi_ref[0] += _dot(ubb, gib, TN)
            dcmr_ref[0] += _dot(s_re[rows, :].astype(BF16), dyb, TN)
            dcmi_ref[0] -= _dot(s_im[rows, :].astype(BF16), dyb, TN)
            du_ref[rows, :] = _dot(grb, bbr_ref[0], NT) + _dot(gib, bbi_ref[0], NT) + d_ref[...] * dyv
            dd_ref[...] += jnp.sum(dyv * ub, axis=0, keepdims=True)
            return c

        lax.fori_loop(0, L // rb, grad_step, 0)

    chunk = lambda rows, cols: pl.BlockSpec((rows, cols), lambda q: (0, q))
    mat = lambda r, c: pl.BlockSpec((1, r, c), lambda q: (q, 0, 0))
    anyspec = pl.BlockSpec(memory_space=pl.ANY)
    big = lambda: pltpu.VMEM((L, SSM_CW), F32)
    return pl.pallas_call(
        body, name="ssm_bwd", grid=(SSM_CHUNKS,),
        in_specs=[chunk(L, SSM_CU), chunk(L, SSM_CU), anyspec, anyspec, chunk(1, SSM_CW), chunk(1, SSM_CW),
                  mat(SSM_CU, SSM_CW), mat(SSM_CU, SSM_CW), mat(SSM_CW, SSM_CU), mat(SSM_CW, SSM_CU), chunk(1, SSM_CU)],
        out_specs=[chunk(L, SSM_CU), mat(SSM_CU, SSM_CW), mat(SSM_CU, SSM_CW), mat(SSM_CW, SSM_CU), mat(SSM_CW, SSM_CU),
                   chunk(1, SSM_CW), chunk(1, SSM_CW), chunk(1, SSM_CU)],
        out_shape=[jax.ShapeDtypeStruct((L, SSM_WIDTH), F32),
                   jax.ShapeDtypeStruct((SSM_CHUNKS, SSM_CU, SSM_CW), F32), jax.ShapeDtypeStruct((SSM_CHUNKS, SSM_CU, SSM_CW), F32),
                   jax.ShapeDtypeStruct((SSM_CHUNKS, SSM_CW, SSM_CU), F32), jax.ShapeDtypeStruct((SSM_CHUNKS, SSM_CW, SSM_CU), F32),
                   jax.ShapeDtypeStruct((1, SSM_NSTATE), F32), jax.ShapeDtypeStruct((1, SSM_NSTATE), F32),
                   jax.ShapeDtypeStruct((1, SSM_WIDTH), F32)],
        scratch_shapes=[big(), big(), big(), big(), pltpu.SemaphoreType.DMA((2,))],
        compiler_params=_params("arbitrary", vmem=VMEM_BIG))(
            dy, u, s_re_all, s_im_all, a_re, a_im, bb_re, bb_im, cm_re, cm_im, d_skip)


def _place():
    return lax.axis_index("x"), lax.axis_index("y"), lax.axis_index("c")


def _all_gather_call(blocks, name, direct=False):
    n = len(blocks)

    def body(*refs):
        if direct:
            start, finish = _exchange_phases(refs[:n], refs[n:2 * n], *refs[2 * n:], same_source=True)
            start()
        else:
            start, forward, finish = _gather_phases(refs[:n], refs[n:2 * n], *refs[2 * n:])
            start()
            forward()
        finish()

    return pl.pallas_call(
        body, name=name, in_specs=[ANY_SPEC] * n, out_specs=[ANY_SPEC] * n,
        out_shape=[jax.ShapeDtypeStruct((N_DEV,) + b.shape, b.dtype) for b in blocks],
        scratch_shapes=_comm_sems(n))(*blocks)


def _comm_sems(n):
    return [pltpu.SemaphoreType.DMA((7 * n,)), pltpu.SemaphoreType.DMA((7 * n,)), pltpu.SemaphoreType.DMA((n,))]


def _gather_phases(x_refs, out_refs, send_sems, recv_sems, local_sems):
    x, y, c = _place()
    me, sibling = (x, y, c), (x, y, 1 - c)
    chips = [(1 - x, y), (x, 1 - y), (1 - x, 1 - y)]
    n = len(x_refs)

    def copy(k, a, blk, to, from_input=False):
        slot = out_refs[a].at[4 * blk[0] + 2 * blk[1] + blk[2]]
        return pltpu.make_async_remote_copy(
            src_ref=x_refs[a] if from_input else slot, dst_ref=slot,
            send_sem=send_sems.at[k * n + a], recv_sem=recv_sems.at[k * n + a], device_id=to, device_id_type=MESH_ID)

    mine = [pltpu.make_async_copy(x_refs[a], out_refs[a].at[4 * x + 2 * y + c], local_sems.at[a]) for a in range(n)]
    first, passed = [], []
    for a in range(n):
        first.append(copy(0, a, me, sibling, True))
        first += [copy(1 + j, a, me, (*chip, c), True) for j, chip in enumerate(chips)]
        passed += [copy(4 + j, a, (*chip, c), sibling) for j, chip in enumerate(chips)]

    def start():
        for cp in mine + first:
            cp.start()

    def forward():
        for j, chip in enumerate(chips):
            for a in range(n):
                copy(1 + j, a, (*chip, c), me).wait_recv()
                passed[3 * a + j].start()

    def finish():
        for a in range(n):
            copy(0, a, sibling, me).wait_recv()
            for j, chip in enumerate(chips):
                copy(4 + j, a, (*chip, 1 - c), me).wait_recv()
        for cp in first + passed:
            cp.wait_send()
        for cp in mine:
            cp.wait()

    return start, forward, finish


def _exchange_phases(p_refs, out_refs, send_sems, recv_sems, local_sems, same_source=False):
    x, y, c = _place()
    me = 4 * x + 2 * y + c
    n = len(p_refs)

    def flip(k):
        px = 1 - x if k & 4 else x
        py = 1 - y if k & 2 else y
        pc = 1 - c if k & 1 else c
        return (px, py, pc), 4 * px + 2 * py + pc

    def source(a, slot):
        return p_refs[a] if same_source else p_refs[a].at[slot]

    def copy(k, a, landing):
        peer, peer_slot = flip(k)
        return pltpu.make_async_remote_copy(
            src_ref=source(a, peer_slot), dst_ref=out_refs[a].at[peer_slot if landing else me],
            send_sem=send_sems.at[(k - 1) * n + a], recv_sem=recv_sems.at[(k - 1) * n + a],
            device_id=peer, device_id_type=MESH_ID)

    mine = [pltpu.make_async_copy(source(a, me), out_refs[a].at[me], local_sems.at[a]) for a in range(n)]
    sends = [copy(k, a, False) for k in range(1, N_DEV) for a in range(n)]

    def start():
        for cp in mine + sends:
            cp.start()

    def finish():
        for k in range(1, N_DEV):
            for a in range(n):
                copy(k, a, True).wait_recv()
        for cp in sends:
            cp.wait_send()
        for cp in mine:
            cp.wait()

    return start, finish


def _adam_math(g, w, m, v):
    c1 = 1.0 / (1.0 - ADAM_B1 ** ADAM_STEP)
    c2 = 1.0 / (1.0 - ADAM_B2 ** ADAM_STEP)
    m_new = ADAM_B1 * m + (1.0 - ADAM_B1) * g
    v_new = ADAM_B2 * v + (1.0 - ADAM_B2) * (g * g)
    delta = -ADAM_LR * ((m_new * c1) / (jnp.sqrt(v_new * c2) + ADAM_EPS) + ADAM_WD * w)
    return g, delta, m_new, v_new


def _sum_slices(s_ref):
    g = s_ref[0].astype(F32)
    for k in range(1, N_DEV):
        g = g + s_ref[k].astype(F32)
    return g


def _adam_call(slices, w, m, v, name):
    d1, rest = w.shape[1], w.shape[2:]
    zeros = (0,) * len(rest)
    by_lanes = len(rest) == 1 and d1 > 256 and d1 % 16 != 0
    if by_lanes:
        tile = _fit(rest[0], 256)
        steps = rest[0] // tile
        own = pl.BlockSpec((1, d1, tile), lambda i: (0, 0, i))
        sl = pl.BlockSpec((N_DEV, 1, d1, tile), lambda i: (0, 0, 0, i))
    else:
        tile = _fit(d1, 256, 16) if len(rest) == 1 else _fit(d1, 8, 8)
        steps = d1 // tile
        own = pl.BlockSpec((1, tile) + rest, lambda i: (0, i) + zeros)
        sl = pl.BlockSpec((N_DEV, 1, tile) + rest, lambda i: (0, 0, i) + zeros)

    def body(s_ref, w_ref, m_ref, v_ref, g_ref, d_ref, mo_ref, vo_ref):
        g_ref[...], d_ref[...], mo_ref[...], vo_ref[...] = _adam_math(_sum_slices(s_ref), w_ref[...], m_ref[...], v_ref[...])

    out = jax.ShapeDtypeStruct(w.shape, F32)
    return pl.pallas_call(
        body, name=name, grid=(steps,), in_specs=[sl, own, own, own],
        out_specs=[own, own, own, own], out_shape=[out, out, out, out],
        compiler_params=_params("parallel"))(slices, w, m, v)


def _adam_small_call(rows_all, row_params, slices, params):
    nr, n = len(row_params), len(row_params) + len(params)

    def row_sum(rows_ref, a, width):
        g = rows_ref[0, pl.ds(a, 1), pl.ds(0, width)]
        for k in range(1, N_DEV):
            g = g + rows_ref[k, pl.ds(a, 1), pl.ds(0, width)]
        return g

    def body(rows_ref, *refs):
        slice_refs, wmv, outs = refs[:n - nr], refs[n - nr:n - nr + 3 * n], refs[n - nr + 3 * n:]
        outs[4 * n][...] = row_sum(rows_ref, nr, LANES)
        for a in range(n):
            w_ref, m_ref, v_ref = wmv[3 * a:3 * a + 3]
            if a < nr:
                g = row_sum(rows_ref, a, w_ref.shape[1])
            else:
                g = _sum_slices(slice_refs[a - nr])
            res = _adam_math(g, w_ref[...], m_ref[...], v_ref[...])
            for r in range(4):
                outs[4 * a + r][...] = res[r]

    every = list(row_params) + list(params)
    flat = pl.pallas_call(
        body, name="adam_small",
        out_shape=[jax.ShapeDtypeStruct(w.shape, F32) for w, _, _ in every for _ in range(4)]
        + [jax.ShapeDtypeStruct((1, LANES), F32)],
    )(rows_all, *slices, *[t for wmv in every for t in wmv])
    return [flat[4 * a:4 * a + 4] for a in range(n)], flat[4 * n][0, 0]


BIG = (("w_in", 1024, 404, 1), ("w_uq", 384, 96, 1), ("w_uk", 256, 64, 1), ("w_uv", 256, 64, 1),
       ("w_glu", 64, 512, 0), ("w_branch_attn", 512, 128, 1), ("w_branch_ssm", 512, 128, 1),
       ("w_out", 128, 1024, 0), ("w_up", 1024, 704, 1), ("w_down", 352, 1024, 0), ("conv_w", 3, 704, 1))
BIG_MIX, BIG_FFN = BIG[:8], BIG[8:]
GRADS_EARLY, GRADS_LATE = BIG[8:] + BIG[4:8], BIG[:4]
SMALL = (("mix_norm_pre", (1024,)), ("q_norm", (384,)), ("kv_norm", (256,)), ("ssm_lambda_re", (32, 64)),
         ("ssm_lambda_im", (32, 64)), ("ssm_log_dt", (32,)), ("ssm_b_re", (32, 64, 16)), ("ssm_b_im", (32, 64, 16)),
         ("ssm_c_re", (32, 16, 64)), ("ssm_c_im", (32, 16, 64)), ("ssm_d", (32, 16)), ("b_glu", (512,)),
         ("b_gate", (2048,)), ("mix_norm_post", (1024,)), ("ffn_norm_pre", (1024,)), ("conv_b", (5632,)),
         ("ffn_norm_post", (1024,)))


TRANSPOSED = ("w_in", "w_uq", "w_uk", "w_uv", "w_up")


def _stored(name, arr):
    return jnp.swapaxes(arr, 1, 2) if name in TRANSPOSED else arr


def _to_slices(name, full, rows, cols, axis):
    if name in TRANSPOSED:
        return full.T.reshape(N_DEV, cols, rows)
    if axis == 1:
        return full.reshape(rows, N_DEV, cols).transpose(1, 0, 2)
    return full.reshape(N_DEV, rows, cols)


def _from_slices(name, parts, rows, cols, axis):
    if name in TRANSPOSED:
        return parts.reshape(N_DEV * cols, rows)
    if axis == 1:
        return parts.transpose(1, 0, 2).reshape(rows, N_DEV * cols)
    return parts.reshape(N_DEV * rows, cols)


def _head_pad_cols(w, width):
    k = w.shape[0]
    return jnp.pad(w.reshape(k, N_HEADS, width), ((0, 0), (0, 0), (0, LANES - width))).reshape(k, HEAD_PAD)


def _head_unpad_cols(w, width):
    k = w.shape[0]
    return w.reshape(k, N_HEADS, LANES)[:, :, :width].reshape(k, N_HEADS * width)


def _time_perm(a, L):
    return a.reshape(8, L // 8, a.shape[-1]).transpose(1, 0, 2).reshape(L, a.shape[-1])


def _time_unperm(a, L):
    return a.reshape(L // 8, 8, a.shape[-1]).transpose(1, 0, 2).reshape(L, a.shape[-1])


def _block_diag(w, rows_first):
    eye = jnp.eye(8, dtype=w.dtype)
    g = w.reshape(SSM_CHUNKS, 8, w.shape[1], w.shape[2])
    return jnp.einsum("qgrc,gk->qgrkc", g, eye).reshape(SSM_CHUNKS, 8 * w.shape[1], 8 * w.shape[2])


def _block_diag_t(m, r, c):
    eye = jnp.eye(8, dtype=m.dtype)
    return jnp.einsum("qgrkc,gk->qgrc", m.reshape(SSM_CHUNKS, 8, r, 8, c), eye).reshape(SSM_GROUPS, r, c)


def kernel(x, positions, mix_norm_pre, w_in, q_norm, w_uq, kv_norm, w_uk, w_uv, ssm_lambda_re, ssm_lambda_im, ssm_log_dt, ssm_b_re, ssm_b_im, ssm_c_re, ssm_c_im, ssm_d, w_glu, b_glu, w_branch_attn, w_branch_ssm, b_gate, w_out, mix_norm_post, ffn_norm_pre, w_up, conv_w, conv_b, w_down, ffn_norm_post, loss_target, m_mix_norm_pre, m_w_in, m_q_norm, m_w_uq, m_kv_norm, m_w_uk, m_w_uv, m_ssm_lambda_re, m_ssm_lambda_im, m_ssm_log_dt, m_ssm_b_re, m_ssm_b_im, m_ssm_c_re, m_ssm_c_im, m_ssm_d, m_w_glu, m_b_glu, m_w_branch_attn, m_w_branch_ssm, m_b_gate, m_w_out, m_mix_norm_post, m_ffn_norm_pre, m_w_up, m_conv_w, m_conv_b, m_w_down, m_ffn_norm_post, v_mix_norm_pre, v_w_in, v_q_norm, v_w_uq, v_kv_norm, v_w_uk, v_w_uv, v_ssm_lambda_re, v_ssm_lambda_im, v_ssm_log_dt, v_ssm_b_re, v_ssm_b_im, v_ssm_c_re, v_ssm_c_im, v_ssm_d, v_w_glu, v_b_glu, v_w_branch_attn, v_w_branch_ssm, v_b_gate, v_w_out, v_mix_norm_post, v_ffn_norm_pre, v_w_up, v_conv_w, v_conv_b, v_w_down, v_ffn_norm_post):
    given = dict(locals())
    L = x.shape[1]
    xs = x[0]
    target = loss_target[0]

    def shard_bits(group):
        return [given[name][0] if name == "conv_w" else _stored(name, given[name])[0].astype(BF16) for name, _, _, _ in group]

    W = {}

    def unpack_weights(gathered, group):
        for (name, rows, cols, axis), parts in zip(group, gathered):
            W[name] = _from_slices(name, parts, rows, cols, axis)

    unpack_weights(_all_gather_call(shard_bits(BIG_MIX[:1]), "gather_w_in"), BIG_MIX[:1])

    wit = W["w_in"]
    zero_rows = lambda r: jnp.zeros((r, D_MODEL), BF16)
    w_in_pt = jnp.concatenate([wit[:640], zero_rows(QK_NOPE), wit[640:672], zero_rows(LANES - QK_HEAD), wit[672:]], axis=0)

    hn1 = _rms_fwd_call(xs, mix_norm_pre, "rms_pre")
    proj, *gathered_mix = _mm(hn1, w_in_pt, "mm_in", tb=True, tn=1664, gather=shard_bits(BIG_MIX[1:]))
    unpack_weights(gathered_mix, BIG_MIX[1:])
    head_rows = lambda wt, width: jnp.pad(wt.reshape(N_HEADS, width, wt.shape[1]), ((0, 0), (0, LANES - width), (0, 0)))
    w_uq_pt = head_rows(W["w_uq"], QK_HEAD).reshape(HEAD_PAD, Q_RANK)
    w_kv_pt = jnp.stack([head_rows(W["w_uk"], QK_NOPE), head_rows(W["w_uv"], V_HEAD)], axis=1
                        ).reshape(2 * HEAD_PAD, KV_RANK)
    w_ba_p = jnp.pad(W["w_branch_attn"].reshape(N_HEADS, V_HEAD, D_MODEL), ((0, 0), (0, LANES - V_HEAD), (0, 0))
                     ).reshape(HEAD_PAD, D_MODEL)
    qn, ckvn = _mla_norms_call(proj, q_norm, kv_norm)
    q_pad = _mm(qn, w_uq_pt, "mm_uq", tb=True)
    kv_pad = _mm(ckvn, w_kv_pt, "mm_ukv", tb=True)
    half = jnp.arange(QK_ROPE // 2, dtype=F32)
    inv_freq = ROPE_THETA ** (-2.0 * half / QK_ROPE)
    inv_freq = jnp.pad(jnp.concatenate([inv_freq, inv_freq]), (QK_NOPE, LANES - QK_HEAD)).reshape(1, LANES)
    pos_col = positions.astype(F32).reshape(L, 1)
    q_r, kv_r, cosf, sinf = _mla_prep_call(q_pad, kv_pad, proj, pos_col, inv_freq)
    attn, lse, *gathered_ffn = _attn_fwd_call(q_r, kv_r, shard_bits(BIG_FFN))
    unpack_weights(gathered_ffn, BIG_FFN)

    col = lambda a: a.reshape(SSM_NSTATE, -1)
    lr_c, li_c = col(ssm_lambda_re[0]), col(ssm_lambda_im[0])
    ldt_c = col(jnp.broadcast_to(ssm_log_dt[0][:, None], (SSM_GROUPS, SSM_STATE)))
    br_c, bi_c = col(ssm_b_re[0]), col(ssm_b_im[0])
    a_re_c, a_im_c, bb_re_c, bb_im_c = _disc_call(lr_c, li_c, ldt_c, br_c, bi_c)
    a_re, a_im = a_re_c.reshape(1, SSM_NSTATE), a_im_c.reshape(1, SSM_NSTATE)
    to_bb = lambda b: _block_diag(b.reshape(SSM_GROUPS, SSM_STATE, SSM_GROUP).transpose(0, 2, 1), True).astype(BF16)
    bb_re, bb_im = to_bb(bb_re_c), to_bb(bb_im_c)
    to_cm = lambda c_: _block_diag(c_[0].transpose(0, 2, 1), True).astype(BF16)
    cm_re, cm_im = to_cm(ssm_c_re), to_cm(ssm_c_im)
    d_skip = ssm_d.reshape(1, SSM_WIDTH)
    u_p = _time_perm(proj[:, P_U:P_GATE], L)
    y1, s_re, s_im = _ssm_fwd_call(u_p, a_re, a_im, bb_re, bb_im, cm_re, cm_im, d_skip)
    w_glu_b = W["w_glu"]
    ssm_p = _glu_call(y1, w_glu_b, b_glu)
    ssm = _time_unperm(ssm_p, L)

    pa = _mm(attn, w_ba_p, "mm_ba")
    ps = _mm(ssm, W["w_branch_ssm"], "mm_bs")
    merged = _merge_call(proj, b_gate, pa, ps)
    o = _mm(merged, W["w_out"], "mm_out")
    x2, hn2 = _post_mix_call(o, xs, mix_norm_post, ffn_norm_pre)
    h = _mm(hn2, W["w_up"], "mm_up", tb=True, tn=1408)
    cw = W["conv_w"]
    act = _conv_act_call(h, cw, conv_b)
    ff = _mm(act, W["w_down"], "mm_down", tn=1024, tk=1408)
    loss_row, dy, dff, g_ffn_norm_post = _ffn_out_call(ff, x2, target, ffn_norm_post)

    da = _mm(dff, W["w_down"], "mm_down_dx", tb=True, tn=1408)
    g_w_down = _mm_tn(act, dff, "mm_down_dw", tm=1408)
    dgate, dval, dcw_g, dcw_v, dcb_g, dcb_v = _conv_act_bwd_call(da, h, cw, conv_b)
    g_conv_w = jnp.concatenate([dcw_g, dcw_v], axis=1)
    g_conv_b = jnp.concatenate([dcb_g, dcb_v], axis=1)
    dh = _conv_t_call(dgate, dval, cw)
    dhn2 = _mm(dh, W["w_up"], "mm_up_dx", tn=1024, tk=1408)
    g_w_up = _mm_tn(hn2, dh, "mm_up_dw")
    dx2, do, g_ffn_norm_pre, g_mix_norm_post = _post_bwd_call(x2, dhn2, dy, o, ffn_norm_pre, mix_norm_post)
    dmerged = _mm(do, W["w_out"], "mm_out_dx", tb=True)
    g_w_out = _mm_tn(merged, do, "mm_out_dw")
    dpa, dps, dl0, dl1, db0, db1 = _merge_bwd_call(dmerged, proj, b_gate, pa, ps)
    g_b_gate = jnp.concatenate([db0, db1], axis=1)
    dattn = _mm(dpa, w_ba_p, "mm_ba_dx", tb=True, out_dtype=BF16)
    g_w_ba = _mm_tn(attn, dpa, "mm_ba_dw").reshape(N_HEADS, LANES, D_MODEL)[:, :V_HEAD].reshape(N_HEADS * V_HEAD, D_MODEL)
    dssm = _mm(dps, W["w_branch_ssm"], "mm_bs_dx", tb=True)
    g_w_bs = _mm_tn(ssm, dps, "mm_bs_dw")

    dy1, g_w_glu, g_b_glu = _glu_bwd_call(_time_perm(dssm, L), y1, w_glu_b, b_glu)
    du_p, dbb_re, dbb_im, dcm_re, dcm_im, da_re, da_im, g_ssm_d = _ssm_bwd_call(
        dy1, u_p, s_re, s_im, a_re, a_im, bb_re, bb_im, cm_re, cm_im, d_skip)
    du = _time_unperm(du_p, L)
    from_bb = lambda m: col(_block_diag_t(m, SSM_GROUP, SSM_STATE).transpose(0, 2, 1))
    dlr, dli, dldt, dbr, dbi = _disc_bwd_call(
        lr_c, li_c, ldt_c, br_c, bi_c, da_re.reshape(SSM_NSTATE, 1), da_im.reshape(SSM_NSTATE, 1), from_bb(dbb_re), from_bb(dbb_im))
    g_c_re = _block_diag_t(dcm_re, SSM_STATE, SSM_GROUP).transpose(0, 2, 1)
    g_c_im = _block_diag_t(dcm_im, SSM_STATE, SSM_GROUP).transpose(0, 2, 1)

    def grad_slices(group, grads):
        return [_to_slices(name, grads[name], rows, cols, axis) for name, rows, cols, axis in group]

    early_grads = {"w_up": g_w_up, "w_down": g_w_down, "conv_w": g_conv_w, "w_glu": g_w_glu.astype(BF16),
                   "w_branch_attn": g_w_ba, "w_branch_ssm": g_w_bs, "w_out": g_w_out}
    ssm_partials = {"ssm_lambda_re": dlr, "ssm_lambda_im": dli, "ssm_b_re": dbr, "ssm_b_im": dbi,
                    "ssm_c_re": g_c_re, "ssm_c_im": g_c_im, "ssm_d": g_ssm_d}
    ssm_shapes = [(name, shp) for name, shp in SMALL if name in ssm_partials]
    dq, dkv, *landed = _attn_bwd_call(
        q_r, kv_r, attn, dattn, lse, grad_slices(GRADS_EARLY, early_grads),
        [ssm_partials[name].reshape(-1, LANES) if len(shp) == 3 else ssm_partials[name].reshape((1,) + shp)
         for name, shp in ssm_shapes])
    received_early = landed[:len(GRADS_EARLY)]
    ssm_all = {name: got.reshape((N_DEV, 1) + shp) for (name, shp), got in zip(ssm_shapes, landed[len(GRADS_EARLY):])}
    dq_p, dkv_p, dkr_p = _mla_prep_bwd_call(dq, dkv, cosf, sinf)
    dqn = _mm(dq_p, w_uq_pt, "mm_uq_dx")
    g_w_uq = _head_unpad_cols(_mm_tn(qn, dq_p, "mm_uq_dw"), QK_HEAD)
    dckvn = _mm(dkv_p, w_kv_pt, "mm_ukv_dx")
    g_w_kv = _mm_tn(ckvn, dkv_p, "mm_ukv_dw").reshape(KV_RANK, N_HEADS, 2, LANES)
    g_w_uk = g_w_kv[:, :, 0, :QK_NOPE].reshape(KV_RANK, N_HEADS * QK_NOPE)
    g_w_uv = g_w_kv[:, :, 1, :V_HEAD].reshape(KV_RANK, N_HEADS * V_HEAD)
    dcqkv, g_q_norm, g_kv_norm = _mla_norms_bwd_call(proj, dqn, dckvn, q_norm, kv_norm)
    dproj = jnp.concatenate([dcqkv, dkr_p, du.astype(BF16), dl0, dl1], axis=1)
    g_w_in_p = _mm_tn(hn1, dproj, "mm_in_dw", tk=1024)
    g_w_in = jnp.concatenate([g_w_in_p[:, :640], g_w_in_p[:, 640 + QK_NOPE:640 + QK_HEAD], g_w_in_p[:, 768:]], axis=1)
    late_grads = {"w_in": g_w_in, "w_uq": g_w_uq, "w_uk": g_w_uk, "w_uv": g_w_uv}
    dhn1, *received_late = _mm(dproj, w_in_pt, "mm_in_dx", tk=1664, exchange=grad_slices(GRADS_LATE, late_grads))
    grad_x, g_mix_norm_pre = _pre_bwd_call(xs, dhn1, dx2, mix_norm_pre)

    results = {}
    for group, received in ((GRADS_EARLY, received_early), (GRADS_LATE, received_late)):
        for (name, _, _, _), rec in zip(group, received):
            stored = [_stored(name, given[prefix + name]) for prefix in ("", "m_", "v_")]
            results[name] = [_stored(name, r) for r in _adam_call(rec[:, None], *stored, "adam_" + name)]

    vec_grads = {"mix_norm_pre": g_mix_norm_pre, "q_norm": g_q_norm, "kv_norm": g_kv_norm,
                 "ssm_log_dt": jnp.sum(dldt.reshape(SSM_GROUPS, SSM_STATE), axis=1),
                 "b_glu": g_b_glu, "b_gate": g_b_gate, "mix_norm_post": g_mix_norm_post,
                 "ffn_norm_pre": g_ffn_norm_pre, "conv_b": g_conv_b, "ffn_norm_post": g_ffn_norm_post}
    vec_names = [name for name, _ in SMALL if name in vec_grads]
    width = max(shp[0] for name, shp in SMALL if name in vec_grads)
    rows = [jnp.pad(vec_grads[name].reshape(1, -1), ((0, 0), (0, width - vec_grads[name].size))) for name in vec_names]
    rows.append(jnp.pad(loss_row, ((0, 0), (0, width - LANES))))
    rows.append(jnp.zeros((-len(rows) % 8, width), F32))
    rows_all, = _all_gather_call([jnp.concatenate(rows, axis=0)], "gather_small_grads", direct=True)
    wmv = lambda name: (given[name], given["m_" + name], given["v_" + name])
    few = ["ssm_lambda_re", "ssm_lambda_im", "ssm_d"]
    small_results, loss = _adam_small_call(
        rows_all, [wmv(n) for n in vec_names], [ssm_all[n] for n in few], [wmv(n) for n in few])
    results.update(zip(vec_names + few, small_results))
    for name in ("ssm_b_re", "ssm_b_im", "ssm_c_re", "ssm_c_im"):
        results[name] = _adam_call(ssm_all[name], *wmv(name), "adam_" + name)

    order = ["mix_norm_pre", "w_in", "q_norm", "w_uq", "kv_norm", "w_uk", "w_uv", "ssm_lambda_re", "ssm_lambda_im",
             "ssm_log_dt", "ssm_b_re", "ssm_b_im", "ssm_c_re", "ssm_c_im", "ssm_d", "w_glu", "b_glu", "w_branch_attn",
             "w_branch_ssm", "b_gate", "w_out", "mix_norm_post", "ffn_norm_pre", "w_up", "conv_w", "conv_b", "w_down",
             "ffn_norm_post"]
    outs = [loss, grad_x[None]]
    for kind in range(4):
        outs += [results[name][kind] for name in order]
    return tuple(outs)
```

```python
import math

import jax
import jax.numpy as jnp
from jax import lax
from jax.experimental import pallas as pl
from jax.experimental.pallas import tpu as pltpu

F32 = jnp.float32
BF16 = jnp.bfloat16
MESH_ID = pl.DeviceIdType.MESH

N_DEV = 8
LANES = 128
D_MODEL = 1024
N_HEADS = 8
QK_NOPE = 64
QK_ROPE = 32
QK_HEAD = QK_NOPE + QK_ROPE
V_HEAD = 64
Q_RANK = 384
KV_RANK = 256
ROPE_THETA = 10000.0
SSM_WIDTH = 512
SSM_GROUP = 16
SSM_GROUPS = 32
SSM_STATE = 64
SSM_NSTATE = SSM_GROUPS * SSM_STATE
SSM_CHUNKS = 4
D_FF = 2816
EPS = 1e-6
ADAM_LR, ADAM_B1, ADAM_B2, ADAM_EPS, ADAM_WD, ADAM_STEP = 0.001, 0.9, 0.999, 1e-08, 0.01, 10

P_CQ, P_CKV, P_KR, P_U, P_GATE = 0, 384, 640, 768, 1280
HEAD_PAD = N_HEADS * LANES

VMEM_BIG = 52 * 1024 * 1024

_GELU_C0 = math.sqrt(2.0 / math.pi)
_GELU_C1 = 0.044715
NEG = -1e30


def _fit(n, pref, mult=LANES):
    if n <= pref:
        return n
    t = (pref // mult) * mult
    while t > 0 and n % t:
        t -= mult
    assert t > 0, (n, pref, mult)
    return t


def _gelu(x):
    return x * (0.5 * (1.0 + jnp.tanh(_GELU_C0 * x * (1.0 + _GELU_C1 * (x * x)))))


def _gelu_and_grad(x):
    x2 = x * x
    t = jnp.tanh(_GELU_C0 * x * (1.0 + _GELU_C1 * x2))
    half = 0.5 * (1.0 + t)
    return x * half, half + 0.5 * x * (1.0 - t * t) * _GELU_C0 * (1.0 + 3.0 * _GELU_C1 * x2)


def _sigmoid(x):
    return 1.0 / (1.0 + jnp.exp(-x))


def _dot(a, b, dims):
    return lax.dot_general(a, b, (dims, ((), ())), preferred_element_type=F32)


NN = ((1,), (0,))
NT = ((1,), (1,))
TN = ((0,), (0,))


def _params(*sem, vmem=None):
    return pltpu.CompilerParams(dimension_semantics=tuple(sem), vmem_limit_bytes=vmem)


def _mm(a, b, name, tb=False, out_dtype=F32, tm=1024, tn=1024, tk=1024, exchange=(), gather=()):
    M, K = a.shape
    if tb:
        N, K2 = b.shape
    else:
        K2, N = b.shape
    assert K == K2, (a.shape, b.shape, tb)
    tm, tn, tk = _fit(M, tm), _fit(N, tn), _fit(K, tk)
    nk = K // tk
    grid = (M // tm, N // tn, nk)
    steps = grid[0] * grid[1] * grid[2]
    dims = NT if tb else NN
    moved = list(exchange) + list(gather)
    n = len(moved)
    assert not (exchange and gather)

    def body(a_ref, b_ref, *refs):
        o_ref, scratch = refs[n], refs[2 * n + 1:]
        step = (pl.program_id(0) * grid[1] + pl.program_id(1)) * grid[2] + pl.program_id(2)
        if exchange:
            start, finish = _exchange_phases(refs[:n], refs[n + 1:2 * n + 1], *scratch[-3:])
            pl.when(step == 0)(start)
        if gather:
            start, forward, finish = _gather_phases(refs[:n], refs[n + 1:2 * n + 1], *scratch[-3:])
            pl.when(step == 0)(start)
            pl.when(step == steps // 2)(forward)
        part = _dot(a_ref[...].astype(BF16), b_ref[...].astype(BF16), dims)
        if nk == 1:
            o_ref[...] = part.astype(out_dtype)
        else:
            acc_ref = scratch[0]
            k = pl.program_id(2)

            @pl.when(k == 0)
            def _():
                acc_ref[...] = part

            @pl.when(k > 0)
            def _():
                acc_ref[...] += part

            @pl.when(k == nk - 1)
            def _():
                o_ref[...] = acc_ref[...].astype(out_dtype)
        if n:
            pl.when(step == steps - 1)(finish)

    a_spec = pl.BlockSpec((tm, tk), lambda i, j, k: (i, k))
    b_spec = pl.BlockSpec((tn, tk), lambda i, j, k: (j, k)) if tb else pl.BlockSpec((tk, tn), lambda i, j, k: (k, j))
    landed = [jax.ShapeDtypeStruct(p.shape, p.dtype) for p in exchange]
    landed += [jax.ShapeDtypeStruct((N_DEV,) + p.shape, p.dtype) for p in gather]
    out = pl.pallas_call(
        body, name=name, grid=grid,
        in_specs=[a_spec, b_spec] + [ANY_SPEC] * n,
        out_specs=[pl.BlockSpec((tm, tn), lambda i, j, k: (i, j))] + [ANY_SPEC] * n,
        out_shape=[jax.ShapeDtypeStruct((M, N), out_dtype)] + landed,
        scratch_shapes=([] if nk == 1 else [pltpu.VMEM((tm, tn), F32)]) + (_comm_sems(n) if n else []),
        compiler_params=_params(*(("arbitrary",) * 3 if n else ("parallel", "parallel", "arbitrary")), vmem=VMEM_BIG),
    )(a, b, *moved)
    return out if n else out[0]


TN_CHUNK = 512


def _mm_tn(a, b, name, tm=512, tk=512):
    K, M = a.shape
    K2, N = b.shape
    assert K == K2, (a.shape, b.shape)
    tm, tk, cn = _fit(M, tm), _fit(K, tk), _fit(N, TN_CHUNK)
    nk = K // tk

    def body(a_ref, b_ref, o_ref, acc_ref):
        k = pl.program_id(1)

        @pl.when(k == 0)
        def _():
            acc_ref[...] = jnp.zeros((tm, N), F32)

        at = a_ref[...].astype(BF16).T
        for c in range(N // cn):
            cols = slice(c * cn, (c + 1) * cn)
            acc_ref[:, cols] += _dot(at, b_ref[:, cols].astype(BF16), NN)

        @pl.when(k == nk - 1)
        def _():
            o_ref[...] = acc_ref[...].astype(BF16)

    return pl.pallas_call(
        body, name=name, grid=(M // tm, nk),
        in_specs=[pl.BlockSpec((tk, tm), lambda i, k: (k, i)), pl.BlockSpec((tk, N), lambda i, k: (k, 0))],
        out_specs=pl.BlockSpec((tm, N), lambda i, k: (i, 0)),
        out_shape=jax.ShapeDtypeStruct((M, N), BF16),
        scratch_shapes=[pltpu.VMEM((tm, N), F32)],
        compiler_params=_params("parallel", "arbitrary", vmem=VMEM_BIG))(a, b)


def _row(tl, n, col=0):
    return pl.BlockSpec((tl, n), lambda i: (i, col))


def _full(shape):
    return pl.BlockSpec(shape, lambda i: (0,) * len(shape))


def _rms(x, g):
    r = lax.rsqrt(jnp.mean(x * x, axis=-1, keepdims=True) + EPS)
    return x * r * g


def _rms_bwd(x, g, dy):
    n = x.shape[-1]
    r = lax.rsqrt(jnp.mean(x * x, axis=-1, keepdims=True) + EPS)
    gy = dy * g
    dx = r * gy - x * (r * r * r * (1.0 / n)) * jnp.sum(x * gy, axis=-1, keepdims=True)
    return dx, jnp.sum(dy * x * r, axis=0, keepdims=True)


def _acc(ref, first, val):
    @pl.when(first)
    def _():
        ref[...] = val

    @pl.when(jnp.logical_not(first))
    def _():
        ref[...] += val


def _rms_fwd_call(x, g, name):
    L, n = x.shape
    tl = _fit(L, 512)

    def body(x_ref, g_ref, o_ref):
        o_ref[...] = _rms(x_ref[...], g_ref[...]).astype(BF16)

    return pl.pallas_call(
        body, name=name, grid=(L // tl,), in_specs=[_row(tl, n), _full((1, n))], out_specs=_row(tl, n),
        out_shape=jax.ShapeDtypeStruct((L, n), BF16), compiler_params=_params("parallel"))(x, g)


def _rope_lanes(shape):
    lane = lax.broadcasted_iota(jnp.int32, shape, 1)
    return lane, jnp.logical_and(lane >= QK_NOPE, lane < QK_HEAD)


def _rope_apply(x, cosf, sinf, lane):
    rot = jnp.where(lane < QK_NOPE + QK_ROPE // 2, -pltpu.roll(x, LANES - QK_ROPE // 2, 1), pltpu.roll(x, QK_ROPE // 2, 1))
    return x * cosf + rot * sinf


def _rope_apply_t(dy, cosf, sinf, lane, is_rope):
    g = dy * sinf
    rot_t = jnp.where(lane < QK_NOPE + QK_ROPE // 2, pltpu.roll(g, LANES - QK_ROPE // 2, 1), -pltpu.roll(g, QK_ROPE // 2, 1))
    return dy * cosf + jnp.where(is_rope, rot_t, 0.0)


def _mla_proj_call(proj, q_norm, kv_norm, w_uq_pt, w_kv_pt, pos_col, inv_freq):
    L = proj.shape[0]
    tl = _fit(L, 512)

    def body(p_ref, gq_ref, gk_ref, wq_ref, wkv_ref, pos_ref, f_ref, qn_ref, kn_ref, qo_ref, kvo_ref, cos_ref, sin_ref):
        qn = _rms(p_ref[:, P_CQ:P_CKV], gq_ref[...]).astype(BF16)
        kn = _rms(p_ref[:, P_CKV:P_KR], gk_ref[...]).astype(BF16)
        qn_ref[...] = qn
        kn_ref[...] = kn
        q_pad = _dot(qn, wq_ref[...], NT)
        kv_pad = _dot(kn, wkv_ref[...], NT)
        lane, is_rope = _rope_lanes((tl, LANES))
        ang = pos_ref[...] * f_ref[...]
        cosf = jnp.where(is_rope, jnp.cos(ang), jnp.where(lane < QK_NOPE, 1.0, 0.0))
        sinf = jnp.where(is_rope, jnp.sin(ang), 0.0)
        cos_ref[...] = cosf
        sin_ref[...] = sinf
        kr = _rope_apply(p_ref[:, P_KR:P_U], cosf, sinf, lane)
        for h in range(N_HEADS):
            qh = _rope_apply(q_pad[:, h * LANES:(h + 1) * LANES], cosf, sinf, lane)
            qo_ref[:, h * LANES:(h + 1) * LANES] = (qh * Q_PRESCALE).astype(BF16)
            kvo_ref[:, 2 * h * LANES:(2 * h + 1) * LANES] = (kv_pad[:, 2 * h * LANES:(2 * h + 1) * LANES] + kr).astype(BF16)
            vh = jnp.where(lane == V_HEAD, 1.0, kv_pad[:, (2 * h + 1) * LANES:(2 * h + 2) * LANES])
            kvo_ref[:, (2 * h + 1) * LANES:(2 * h + 2) * LANES] = vh.astype(BF16)

    shape = lambda n, dt: jax.ShapeDtypeStruct((L, n), dt)
    return pl.pallas_call(
        body, name="mla_proj", grid=(L // tl,),
        in_specs=[_row(tl, P_U), _full((1, Q_RANK)), _full((1, KV_RANK)), _full((HEAD_PAD, Q_RANK)),
                  _full((2 * HEAD_PAD, KV_RANK)), _row(tl, 1), _full((1, LANES))],
        out_specs=[_row(tl, Q_RANK), _row(tl, KV_RANK), _row(tl, HEAD_PAD), _row(tl, 2 * HEAD_PAD), _row(tl, LANES), _row(tl, LANES)],
        out_shape=[shape(Q_RANK, BF16), shape(KV_RANK, BF16), shape(HEAD_PAD, BF16), shape(2 * HEAD_PAD, BF16),
                   shape(LANES, F32), shape(LANES, F32)],
        compiler_params=_params("parallel"))(proj, q_norm, kv_norm, w_uq_pt, w_kv_pt, pos_col, inv_freq)


def _mla_proj_bwd_call(dq, dkv, cosf, sinf, proj, q_norm, kv_norm, w_uq_pt, w_kv_pt):
    L = dq.shape[0]
    tl = _fit(L, 512)

    def body(dq_ref, dkv_ref, cos_ref, sin_ref, p_ref, gq_ref, gk_ref, wq_ref, wkv_ref,
             dqo_ref, dkvo_ref, d_ref, dgq_ref, dgk_ref):
        first = pl.program_id(0) == 0
        lane, is_rope = _rope_lanes((tl, LANES))
        cosf, sinf = cos_ref[...], sin_ref[...]
        dk_sum = jnp.zeros((tl, LANES), F32)
        for h in range(N_HEADS):
            dqo_ref[:, h * LANES:(h + 1) * LANES] = _rope_apply_t(dq_ref[:, h * LANES:(h + 1) * LANES], cosf, sinf, lane, is_rope).astype(BF16)
            dk_sum = dk_sum + dkv_ref[:, 2 * h * LANES:(2 * h + 1) * LANES]
        dkvo_ref[...] = dkv_ref[...].astype(BF16)
        dqn = _dot(dqo_ref[...], wq_ref[...], NN)
        dkn = _dot(dkvo_ref[...], wkv_ref[...], NN)
        dcq, dgq = _rms_bwd(p_ref[:, P_CQ:P_CKV], gq_ref[...], dqn)
        dckv, dgk = _rms_bwd(p_ref[:, P_CKV:P_KR], gk_ref[...], dkn)
        d_ref[:, P_CQ:P_CKV] = dcq.astype(BF16)
        d_ref[:, P_CKV:P_KR] = dckv.astype(BF16)
        d_ref[:, P_KR:P_U] = _rope_apply_t(dk_sum, cosf, sinf, lane, is_rope).astype(BF16)
        _acc(dgq_ref, first, dgq)
        _acc(dgk_ref, first, dgk)

    shape = lambda n: jax.ShapeDtypeStruct((L, n), BF16)
    return pl.pallas_call(
        body, name="mla_proj_bwd", grid=(L // tl,),
        in_specs=[_row(tl, HEAD_PAD), _row(tl, 2 * HEAD_PAD), _row(tl, LANES), _row(tl, LANES), _row(tl, P_KR),
                  _full((1, Q_RANK)), _full((1, KV_RANK)), _full((HEAD_PAD, Q_RANK)), _full((2 * HEAD_PAD, KV_RANK))],
        out_specs=[_row(tl, HEAD_PAD), _row(tl, 2 * HEAD_PAD), _row(tl, P_U), _full((1, Q_RANK)), _full((1, KV_RANK))],
        out_shape=[shape(HEAD_PAD), shape(2 * HEAD_PAD), shape(P_U), jax.ShapeDtypeStruct((1, Q_RANK), F32),
                   jax.ShapeDtypeStruct((1, KV_RANK), F32)],
        compiler_params=_params("arbitrary"))(dq, dkv, cosf, sinf, proj, q_norm, kv_norm, w_uq_pt, w_kv_pt)


GATE_TILE = 256
GATE_ROWS = 1024


def _merge_call(proj, b_gate, pa, ps):
    L = proj.shape[0]
    tl = _fit(L, GATE_ROWS)
    nc = D_MODEL // GATE_TILE
    g0, g1 = P_GATE // GATE_TILE, (P_GATE + D_MODEL) // GATE_TILE

    def body(l0_ref, l1_ref, b0_ref, b1_ref, pa_ref, ps_ref, o_ref):
        s0 = _sigmoid(l0_ref[...] + b0_ref[...])
        s1 = _sigmoid(l1_ref[...] + b1_ref[...])
        o_ref[...] = (s0 * pa_ref[...] + s1 * ps_ref[...]).astype(BF16)

    blk = lambda off: pl.BlockSpec((tl, GATE_TILE), lambda i, j: (i, off + j))
    bias = lambda off: pl.BlockSpec((1, GATE_TILE), lambda i, j: (0, off + j))
    return pl.pallas_call(
        body, name="merge", grid=(L // tl, nc),
        in_specs=[blk(g0), blk(g1), bias(0), bias(nc), blk(0), blk(0)],
        out_specs=blk(0), out_shape=jax.ShapeDtypeStruct((L, D_MODEL), BF16),
        compiler_params=_params("parallel", "parallel"))(proj, proj, b_gate, b_gate, pa, ps)


def _merge_bwd_call(dm, proj, b_gate, pa, ps):
    L = proj.shape[0]
    tl = _fit(L, GATE_ROWS)
    nc = D_MODEL // GATE_TILE
    g0, g1 = P_GATE // GATE_TILE, (P_GATE + D_MODEL) // GATE_TILE

    def body(dm_ref, l0_ref, l1_ref, b0_ref, b1_ref, pa_ref, ps_ref, dpa_ref, dps_ref, dl0_ref, dl1_ref, db0_ref, db1_ref):
        first = pl.program_id(1) == 0
        dm_ = dm_ref[...]
        s0 = _sigmoid(l0_ref[...] + b0_ref[...])
        s1 = _sigmoid(l1_ref[...] + b1_ref[...])
        dpa_ref[...] = (dm_ * s0).astype(BF16)
        dps_ref[...] = (dm_ * s1).astype(BF16)
        dl0 = dm_ * pa_ref[...] * s0 * (1.0 - s0)
        dl1 = dm_ * ps_ref[...] * s1 * (1.0 - s1)
        dl0_ref[...] = dl0.astype(BF16)
        dl1_ref[...] = dl1.astype(BF16)
        _acc(db0_ref, first, jnp.sum(dl0, axis=0, keepdims=True))
        _acc(db1_ref, first, jnp.sum(dl1, axis=0, keepdims=True))

    blk = lambda off: pl.BlockSpec((tl, GATE_TILE), lambda j, i: (i, off + j))
    bias = lambda off: pl.BlockSpec((1, GATE_TILE), lambda j, i: (0, off + j))
    act = jax.ShapeDtypeStruct((L, D_MODEL), BF16)
    vec = jax.ShapeDtypeStruct((1, D_MODEL), F32)
    return pl.pallas_call(
        body, name="merge_bwd", grid=(nc, L // tl),
        in_specs=[blk(0), blk(g0), blk(g1), bias(0), bias(nc), blk(0), blk(0)],
        out_specs=[blk(0), blk(0), blk(0), blk(0), bias(0), bias(0)],
        out_shape=[act, act, act, act, vec, vec],
        compiler_params=_params("parallel", "arbitrary"))(dm, proj, proj, b_gate, b_gate, pa, ps)


def _post_mix_call(o, x, g_post, g_fpre):
    L, n = x.shape
    tl = _fit(L, 512)

    def body(o_ref, x_ref, gp_ref, gf_ref, x2_ref, hn_ref):
        x2 = x_ref[...] + _rms(o_ref[...], gp_ref[...])
        x2_ref[...] = x2
        hn_ref[...] = _rms(x2, gf_ref[...]).astype(BF16)

    return pl.pallas_call(
        body, name="post_mix", grid=(L // tl,),
        in_specs=[_row(tl, n), _row(tl, n), _full((1, n)), _full((1, n))],
        out_specs=[_row(tl, n), _row(tl, n)],
        out_shape=[jax.ShapeDtypeStruct((L, n), F32), jax.ShapeDtypeStruct((L, n), BF16)],
        compiler_params=_params("parallel"))(o, x, g_post, g_fpre)


def _ffn_out_call(ff, x2, target, g_fpost):
    L, n = x2.shape
    tl = _fit(L, 512)

    def body(ff_ref, x2_ref, t_ref, g_ref, loss_ref, dy_ref, dff_ref, dg_ref):
        first = pl.program_id(0) == 0
        ff_ = ff_ref[...]
        err = x2_ref[...] + _rms(ff_, g_ref[...]) - t_ref[...]
        part = 0.5 * jnp.sum(jnp.sum(err * err, axis=-1, keepdims=True) * (1.0 / n), axis=0, keepdims=True)
        dy = err * (1.0 / n)
        dy_ref[...] = dy
        dff, dg = _rms_bwd(ff_, g_ref[...], dy)
        dff_ref[...] = dff.astype(BF16)
        _acc(loss_ref, first, jnp.broadcast_to(part, (1, LANES)))
        _acc(dg_ref, first, dg)

    return pl.pallas_call(
        body, name="ffn_out", grid=(L // tl,),
        in_specs=[_row(tl, n), _row(tl, n), _row(tl, n), _full((1, n))],
        out_specs=[_full((1, LANES)), _row(tl, n), _row(tl, n), _full((1, n))],
        out_shape=[jax.ShapeDtypeStruct((1, LANES), F32), jax.ShapeDtypeStruct((L, n), F32),
                   jax.ShapeDtypeStruct((L, n), BF16), jax.ShapeDtypeStruct((1, n), F32)],
        compiler_params=_params("arbitrary"))(ff, x2, target, g_fpost)


def _post_bwd_call(x2, dhn2, dy, o, g_fpre, g_post):
    L, n = x2.shape
    tl = _fit(L, 512)

    def body(x2_ref, dh_ref, dy_ref, o_ref, gf_ref, gp_ref, dx2_ref, do_ref, dgf_ref, dgp_ref):
        first = pl.program_id(0) == 0
        d1, dgf = _rms_bwd(x2_ref[...], gf_ref[...], dh_ref[...])
        dx2 = dy_ref[...] + d1
        dx2_ref[...] = dx2
        do, dgp = _rms_bwd(o_ref[...], gp_ref[...], dx2)
        do_ref[...] = do.astype(BF16)
        _acc(dgf_ref, first, dgf)
        _acc(dgp_ref, first, dgp)

    return pl.pallas_call(
        body, name="post_bwd", grid=(L // tl,),
        in_specs=[_row(tl, n), _row(tl, n), _row(tl, n), _row(tl, n), _full((1, n)), _full((1, n))],
        out_specs=[_row(tl, n), _row(tl, n), _full((1, n)), _full((1, n))],
        out_shape=[jax.ShapeDtypeStruct((L, n), F32), jax.ShapeDtypeStruct((L, n), BF16),
                   jax.ShapeDtypeStruct((1, n), F32), jax.ShapeDtypeStruct((1, n), F32)],
        compiler_params=_params("arbitrary"))(x2, dhn2, dy, o, g_fpre, g_post)


def _pre_bwd_call(x, dhn1, dx2, g_pre):
    L, n = x.shape
    tl = _fit(L, 512)

    def body(x_ref, dh_ref, dx2_ref, g_ref, dx_ref, dg_ref):
        first = pl.program_id(0) == 0
        d1, dg = _rms_bwd(x_ref[...], g_ref[...], dh_ref[...])
        dx_ref[...] = dx2_ref[...] + d1
        _acc(dg_ref, first, dg)

    return pl.pallas_call(
        body, name="pre_bwd", grid=(L // tl,),
        in_specs=[_row(tl, n), _row(tl, n), _row(tl, n), _full((1, n))],
        out_specs=[_row(tl, n), _full((1, n))],
        out_shape=[jax.ShapeDtypeStruct((L, n), F32), jax.ShapeDtypeStruct((1, n), F32)],
        compiler_params=_params("arbitrary"))(x, dhn1, dx2, g_pre)


CONV_TILE = 256
CONV_WIDE = 1408
HALO = 16


def _conv3(w, b, x0, x1, x2):
    return b + w[2:3] * x0 + w[1:2] * x1 + w[0:1] * x2


def _down(x, by):
    return pltpu.roll(x, by, 0)


def _edge_down(edge, before, by):
    r = lax.broadcasted_iota(jnp.int32, edge.shape, 0)
    return jnp.where(r < by, pltpu.roll(before, by, 0), pltpu.roll(edge, by, 0))


def _edge_up(edge, after, by):
    r = lax.broadcasted_iota(jnp.int32, edge.shape, 0)
    return jnp.where(r >= HALO - by, pltpu.roll(after, HALO - by, 0), pltpu.roll(edge, HALO - by, 0))


def _gated(w_g, b_g, w_v, b_v, hg, hv, g1, g2, v1, v2):
    return _conv3(w_g, b_g, hg, g1, g2), _conv3(w_v, b_v, hv, v1, v2)


def _conv_specs(tl, tc, rows_inner):
    nh = tl // HALO
    if rows_inner:
        ij = lambda f: (lambda j, i: f(i, j))
    else:
        ij = lambda f: f
    cur = lambda off: pl.BlockSpec((tl, tc), ij(lambda i, j: (i, off + j)))
    prev = lambda off: pl.BlockSpec((HALO, tc), ij(lambda i, j: (jnp.maximum(i * nh - 1, 0), off + j)))
    par = lambda rows, off: pl.BlockSpec((rows, tc), ij(lambda i, j: (0, off + j)))
    return cur, prev, par


def _conv_act_call(h, conv_w, conv_b):
    L = h.shape[0]
    tl = _fit(L, 256)
    nc = D_FF // CONV_WIDE
    cur, prev, par = _conv_specs(tl, CONV_WIDE, False)

    def body(hg_ref, hv_ref, pg_ref, pv_ref, wg_ref, wv_ref, bg_ref, bv_ref, a_ref):
        not_first = (pl.program_id(0) > 0).astype(F32)
        par = (wg_ref[...], bg_ref[...], wv_ref[...], bv_ref[...])
        hg, hv = hg_ref[...], hv_ref[...]
        gate, val = _gated(*par, hg, hv, _down(hg, 1), _down(hg, 2), _down(hv, 1), _down(hv, 2))
        a_ref[...] = (_gelu(gate) * val).astype(BF16)
        eg, ev, bg, bv = hg[:HALO], hv[:HALO], pg_ref[...] * not_first, pv_ref[...] * not_first
        gate, val = _gated(*par, eg, ev, _edge_down(eg, bg, 1), _edge_down(eg, bg, 2),
                           _edge_down(ev, bv, 1), _edge_down(ev, bv, 2))
        a_ref[:HALO, :] = (_gelu(gate) * val).astype(BF16)

    return pl.pallas_call(
        body, name="conv_act", grid=(L // tl, nc),
        in_specs=[cur(0), cur(nc), prev(0), prev(nc), par(3, 0), par(3, nc), par(1, 0), par(1, nc)],
        out_specs=cur(0), out_shape=jax.ShapeDtypeStruct((L, D_FF), BF16),
        compiler_params=_params("parallel", "parallel"))(h, h, h, h, conv_w, conv_w, conv_b, conv_b)


def _conv_act_bwd_call(da, h, conv_w, conv_b):
    L = h.shape[0]
    tl = _fit(L, 512)
    nc = D_FF // CONV_TILE
    cur, prev, par = _conv_specs(tl, CONV_TILE, True)

    def body(da_ref, hg_ref, hv_ref, pg_ref, pv_ref, wg_ref, wv_ref, bg_ref, bv_ref,
             dg_ref, dv_ref, dwg_ref, dwv_ref, dbg_ref, dbv_ref):
        first = pl.program_id(1) == 0
        not_first = (pl.program_id(1) > 0).astype(F32)
        par = (wg_ref[...], bg_ref[...], wv_ref[...], bv_ref[...])
        col = lambda t: jnp.sum(t, axis=0, keepdims=True)

        def grads(da_, hg, hv, g1, g2, v1, v2):
            gate, val = _gated(*par, hg, hv, g1, g2, v1, v2)
            act, slope = _gelu_and_grad(gate)
            dgate = da_ * val * slope
            dval = da_ * act
            sums = (jnp.concatenate([col(dgate * g2), col(dgate * g1), col(dgate * hg)], axis=0),
                    jnp.concatenate([col(dval * v2), col(dval * v1), col(dval * hv)], axis=0), col(dgate), col(dval))
            return dgate, dval, sums

        da_, hg, hv = da_ref[...], hg_ref[...], hv_ref[...]
        shifted = (_down(hg, 1), _down(hg, 2), _down(hv, 1), _down(hv, 2))
        dgate, dval, whole = grads(da_, hg, hv, *shifted)
        dg_ref[...] = dgate.astype(BF16)
        dv_ref[...] = dval.astype(BF16)
        edge = lambda t: t[:HALO]
        _, _, wrapped = grads(edge(da_), edge(hg), edge(hv), *[edge(s) for s in shifted])
        eg, ev, bg, bv = edge(hg), edge(hv), pg_ref[...] * not_first, pv_ref[...] * not_first
        dgate, dval, fixed = grads(edge(da_), eg, ev, _edge_down(eg, bg, 1), _edge_down(eg, bg, 2),
                                   _edge_down(ev, bv, 1), _edge_down(ev, bv, 2))
        dg_ref[:HALO, :] = dgate.astype(BF16)
        dv_ref[:HALO, :] = dval.astype(BF16)
        for ref, a, b, c in zip((dwg_ref, dwv_ref, dbg_ref, dbv_ref), whole, wrapped, fixed):
            _acc(ref, first, a - b + c)

    act = jax.ShapeDtypeStruct((L, D_FF), BF16)
    w3 = jax.ShapeDtypeStruct((3, D_FF), F32)
    w1 = jax.ShapeDtypeStruct((1, D_FF), F32)
    return pl.pallas_call(
        body, name="conv_act_bwd", grid=(nc, L // tl),
        in_specs=[cur(0), cur(0), cur(nc), prev(0), prev(nc), par(3, 0), par(3, nc), par(1, 0), par(1, nc)],
        out_specs=[cur(0), cur(0), par(3, 0), par(3, 0), par(1, 0), par(1, 0)],
        out_shape=[act, act, w3, w3, w1, w1],
        compiler_params=_params("parallel", "arbitrary"))(da, h, h, h, h, conv_w, conv_w, conv_b, conv_b)


def _conv_t_call(dgate, dval, conv_w):
    L = dgate.shape[0]
    tl = _fit(L, 512)
    nc = D_FF // CONV_WIDE
    nh = tl // HALO

    def body(dg_ref, dv_ref, ng_ref, nv_ref, w_ref, o_ref):
        not_last = (pl.program_id(0) < L // tl - 1).astype(F32)

        def emit(d_ref, n_ref):
            c = d_ref[...].astype(F32)
            w = w_ref[...]
            o_ref[...] = _conv3(w, 0.0, c, pltpu.roll(c, tl - 1, 0), pltpu.roll(c, tl - 2, 0)).astype(BF16)
            edge, after = c[tl - HALO:], n_ref[...].astype(F32) * not_last
            o_ref[tl - HALO:, :] = _conv3(w, 0.0, edge, _edge_up(edge, after, 1), _edge_up(edge, after, 2)).astype(BF16)

        pl.when(pl.program_id(1) < nc)(lambda: emit(dg_ref, ng_ref))
        pl.when(pl.program_id(1) >= nc)(lambda: emit(dv_ref, nv_ref))

    gate_col = lambda j: jnp.minimum(j, nc - 1)
    val_col = lambda j: jnp.maximum(j - nc, 0)
    after_row = lambda i: jnp.minimum((i + 1) * nh, L // HALO - 1)
    tile = lambda col: pl.BlockSpec((tl, CONV_WIDE), lambda i, j: (i, col(j)))
    after = lambda col: pl.BlockSpec((HALO, CONV_WIDE), lambda i, j: (after_row(i), col(j)))
    return pl.pallas_call(
        body, name="conv_t", grid=(L // tl, 2 * nc),
        in_specs=[tile(gate_col), tile(val_col), after(gate_col), after(val_col), pl.BlockSpec((3, CONV_WIDE), lambda i, j: (0, j))],
        out_specs=pl.BlockSpec((tl, CONV_WIDE), lambda i, j: (i, j)),
        out_shape=jax.ShapeDtypeStruct((L, 2 * D_FF), BF16),
        compiler_params=_params("parallel", "parallel"))(dgate, dval, dgate, dval, conv_w)


def _glu_call(y1, w_glu, b_glu):
    L, n = y1.shape
    tl = _fit(L, 512)

    def body(y_ref, w_ref, b_ref, o_ref):
        y2 = _gelu(y_ref[...])
        z = _dot(y2.astype(BF16), w_ref[...], NN) + b_ref[...]
        o_ref[...] = (y2 * _sigmoid(z)).astype(BF16)

    return pl.pallas_call(
        body, name="glu", grid=(L // tl,), in_specs=[_row(tl, n), _full((n, n)), _full((1, n))],
        out_specs=_row(tl, n), out_shape=jax.ShapeDtypeStruct((L, n), BF16),
        compiler_params=_params("parallel"))(y1, w_glu, b_glu)


def _glu_bwd_call(dout, y1, w_glu, b_glu):
    L, n = y1.shape
    tl = _fit(L, 512)

    def body(do_ref, y_ref, w_ref, b_ref, dy_ref, dw_ref, db_ref):
        first = pl.program_id(0) == 0
        y1_ = y_ref[...]
        y2, slope = _gelu_and_grad(y1_)
        y2b = y2.astype(BF16)
        w = w_ref[...]
        sg = _sigmoid(_dot(y2b, w, NN) + b_ref[...])
        dout_ = do_ref[...].astype(F32)
        dz = dout_ * y2 * sg * (1.0 - sg)
        dzb = dz.astype(BF16)
        dy2 = dout_ * sg + _dot(dzb, w, NT)
        dy_ref[...] = dy2 * slope
        _acc(dw_ref, first, _dot(y2b, dzb, TN))
        _acc(db_ref, first, jnp.sum(dz, axis=0, keepdims=True))

    return pl.pallas_call(
        body, name="glu_bwd", grid=(L // tl,),
        in_specs=[_row(tl, n), _row(tl, n), _full((n, n)), _full((1, n))],
        out_specs=[_row(tl, n), _full((n, n)), _full((1, n))],
        out_shape=[jax.ShapeDtypeStruct((L, n), F32), jax.ShapeDtypeStruct((n, n), F32), jax.ShapeDtypeStruct((1, n), F32)],
        compiler_params=_params("arbitrary"))(dout, y1, w_glu, b_glu)


ATTN_TILE = 1024
ATTN_SCALE = 1.0 / math.sqrt(QK_HEAD)


ATTN_HEADS = 2
ATTN_GROUPS = N_HEADS // ATTN_HEADS
LOG2E = 1.0 / math.log(2.0)
Q_PRESCALE = ATTN_SCALE * LOG2E
ANY_SPEC = pl.BlockSpec(memory_space=pl.ANY)


def _attn_fwd_call(q, kv, blocks):
    L = q.shape[0]
    t = _fit(L, ATTN_TILE)
    nq = L // t
    n = len(blocks)

    def body(q_ref, kv_ref, *refs):
        blk_refs, (o_ref, lse_ref), gat_refs = refs[:n], refs[n:n + 2], refs[n + 2:2 * n + 2]
        m_s, acc_s, send_sems, recv_sems, local_sems = refs[2 * n + 2:]
        g, i = pl.program_id(0), pl.program_id(1)
        start, forward, finish = _gather_phases(blk_refs, gat_refs, send_sems, recv_sems, local_sems)
        pl.when(jnp.logical_and(g == 0, i == 0))(start)
        m_s[...] = jnp.full((ATTN_HEADS, t, 1), NEG, F32)
        acc_s[...] = jnp.zeros((ATTN_HEADS, t, LANES), F32)
        below = lax.broadcasted_iota(jnp.int32, (t, t), 1) <= lax.broadcasted_iota(jnp.int32, (t, t), 0)

        def block_step(kb, on_diagonal):
            rows = pl.ds(pl.multiple_of(kb * t, t), t)
            for a in range(ATTN_HEADS):
                s = _dot(q_ref[:, a * LANES:(a + 1) * LANES], kv_ref[rows, 2 * a * LANES:(2 * a + 1) * LANES], NT)
                if on_diagonal:
                    s = jnp.where(below, s, NEG)
                m_prev = m_s[a]
                m_new = jnp.maximum(m_prev, jnp.max(s, axis=1, keepdims=True))
                p = jnp.exp2(s - m_new)
                pv = _dot(p.astype(BF16), kv_ref[rows, (2 * a + 1) * LANES:(2 * a + 2) * LANES], NN)
                acc_s[a] = jnp.exp2(m_prev - m_new) * acc_s[a] + pv
                m_s[a] = m_new

        def step(kb, carry):
            block_step(kb, False)
            return carry

        lax.fori_loop(0, i, step, 0)
        block_step(i, True)
        lane = lax.broadcasted_iota(jnp.int32, (t, LANES), 1)
        for a in range(ATTN_HEADS):
            acc = acc_s[a]
            l = jnp.sum(jnp.where(lane == V_HEAD, acc, 0.0), axis=1, keepdims=True)
            o_ref[:, a * LANES:(a + 1) * LANES] = (acc / l).astype(BF16)
            lse_ref[a] = m_s[a] + jnp.log(l) * LOG2E
        pl.when(jnp.logical_and(g == (3 * ATTN_GROUPS) // 4, i == 0))(forward)
        pl.when(jnp.logical_and(g == ATTN_GROUPS - 1, i == nq - 1))(finish)

    gw = ATTN_HEADS * LANES
    return pl.pallas_call(
        body, name="attn_fwd", grid=(ATTN_GROUPS, nq),
        in_specs=[pl.BlockSpec((t, gw), lambda g, i: (i, g)),
                  pl.BlockSpec((L, 2 * gw), lambda g, i: (0, g))] + [ANY_SPEC] * n,
        out_specs=[pl.BlockSpec((t, gw), lambda g, i: (i, g)),
                   pl.BlockSpec((ATTN_HEADS, t, 1), lambda g, i: (g, i, 0))] + [ANY_SPEC] * n,
        out_shape=[jax.ShapeDtypeStruct((L, HEAD_PAD), BF16), jax.ShapeDtypeStruct((N_HEADS, L, 1), F32)]
        + [jax.ShapeDtypeStruct((N_DEV,) + b.shape, b.dtype) for b in blocks],
        scratch_shapes=[pltpu.VMEM((ATTN_HEADS, t, 1), F32), pltpu.VMEM((ATTN_HEADS, t, LANES), F32)] + _comm_sems(n),
        compiler_params=_params("arbitrary", "arbitrary", vmem=VMEM_BIG))(q, kv, *blocks)


def _attn_bwd_call(q, kv, o, do, lse, parts, blocks):
    L = q.shape[0]
    t = _fit(L, ATTN_TILE)
    nq = L // t
    n1, n = len(parts), len(parts) + len(blocks)

    def body(q_ref, do_ref, o_ref, lse_ref, kv_ref, *refs):
        in_refs, (dq_ref, dkv_ref), out_refs = refs[:n], refs[n:n + 2], refs[n + 2:2 * n + 2]
        dk_s, dv_s = refs[2 * n + 2:2 * n + 4]
        g, j = pl.program_id(0), pl.program_id(1)
        start, finish = _exchange_phases(in_refs[:n1], out_refs[:n1], *refs[2 * n + 4:2 * n + 7])
        start_blocks, finish_blocks = _exchange_phases(in_refs[n1:], out_refs[n1:], *refs[2 * n + 7:], same_source=True)

        @pl.when(jnp.logical_and(g == 0, j == 0))
        def _():
            start()
            start_blocks()

        @pl.when(j == 0)
        def _():
            dq_ref[...] = jnp.zeros((L, ATTN_HEADS * LANES), F32)

        dk_s[...] = jnp.zeros((ATTN_HEADS, t, LANES), F32)
        dv_s[...] = jnp.zeros((ATTN_HEADS, t, LANES), F32)
        below = lax.broadcasted_iota(jnp.int32, (t, t), 1) <= lax.broadcasted_iota(jnp.int32, (t, t), 0)

        def block_step(i, on_diagonal):
            rows = pl.ds(pl.multiple_of(i * t, t), t)
            for a in range(ATTN_HEADS):
                lanes = slice(a * LANES, (a + 1) * LANES)
                qi = q_ref[rows, lanes]
                doi = do_ref[rows, lanes]
                kblk = kv_ref[:, 2 * a * LANES:(2 * a + 1) * LANES]
                delta = jnp.sum(doi.astype(F32) * o_ref[rows, lanes].astype(F32), axis=1, keepdims=True)
                s = _dot(qi, kblk, NT)
                if on_diagonal:
                    s = jnp.where(below, s, NEG)
                p = jnp.exp2(s - lse_ref[a, rows, :])
                dv_s[a] += _dot(p.astype(BF16), doi, TN)
                ds = (p * (_dot(doi, kv_ref[:, (2 * a + 1) * LANES:(2 * a + 2) * LANES], NT) - delta)).astype(BF16)
                dk_s[a] += _dot(ds, qi, TN)
                dq_ref[rows, lanes] += _dot(ds, kblk, NN) * ATTN_SCALE

        def step(i, carry):
            block_step(i, False)
            return carry

        block_step(j, True)
        lax.fori_loop(j + 1, nq, step, 0)
        for a in range(ATTN_HEADS):
            dkv_ref[:, 2 * a * LANES:(2 * a + 1) * LANES] = dk_s[a] * (1.0 / LOG2E)
            dkv_ref[:, (2 * a + 1) * LANES:(2 * a + 2) * LANES] = dv_s[a]
        @pl.when(jnp.logical_and(g == ATTN_GROUPS - 1, j == nq - 1))
        def _():
            finish()
            finish_blocks()

    gw = ATTN_HEADS * LANES
    whole = lambda: pl.BlockSpec((L, gw), lambda g, j: (0, g))
    acc = pltpu.VMEM((ATTN_HEADS, t, LANES), F32)
    return pl.pallas_call(
        body, name="attn_bwd", grid=(ATTN_GROUPS, nq),
        in_specs=[whole(), whole(), whole(), pl.BlockSpec((ATTN_HEADS, L, 1), lambda g, j: (g, 0, 0)),
                  pl.BlockSpec((t, 2 * gw), lambda g, j: (j, g))] + [ANY_SPEC] * n,
        out_specs=[whole(), pl.BlockSpec((t, 2 * gw), lambda g, j: (j, g))] + [ANY_SPEC] * n,
        out_shape=[jax.ShapeDtypeStruct((L, HEAD_PAD), F32), jax.ShapeDtypeStruct((L, 2 * HEAD_PAD), F32)]
        + [jax.ShapeDtypeStruct(p.shape, p.dtype) for p in parts]
        + [jax.ShapeDtypeStruct((N_DEV,) + b.shape, b.dtype) for b in blocks],
        scratch_shapes=[acc, acc] + _comm_sems(n1) + _comm_sems(n - n1),
        compiler_params=_params("arbitrary", "arbitrary", vmem=VMEM_BIG))(q, do, o, lse, kv, *parts, *blocks)


def _disc(lr, li, ldt, br, bi):
    dt = jnp.exp(ldt)
    mag = jnp.exp(lr * dt)
    ang = li * dt
    a_re, a_im = mag * jnp.cos(ang), mag * jnp.sin(ang)
    den = lr * lr + li * li
    n_re, n_im = a_re - 1.0, a_im
    z_re = (n_re * lr + n_im * li) / den
    z_im = (n_im * lr - n_re * li) / den
    return a_re, a_im, z_re * br - z_im * bi, z_re * bi + z_im * br


def _disc_call(lr, li, ldt, br, bi):
    def body(lr_ref, li_ref, ldt_ref, br_ref, bi_ref, ar_ref, ai_ref, bbr_ref, bbi_ref):
        ar_ref[...], ai_ref[...], bbr_ref[...], bbi_ref[...] = _disc(
            lr_ref[...], li_ref[...], ldt_ref[...], br_ref[...], bi_ref[...])

    c1 = jax.ShapeDtypeStruct((SSM_NSTATE, 1), F32)
    c16 = jax.ShapeDtypeStruct((SSM_NSTATE, SSM_GROUP), F32)
    return pl.pallas_call(body, name="ssm_disc", out_shape=[c1, c1, c16, c16])(lr, li, ldt, br, bi)


def _disc_bwd_call(lr, li, ldt, br, bi, dar, dai, dbbr, dbbi):
    def body(lr_ref, li_ref, ldt_ref, br_ref, bi_ref, dar_ref, dai_ref, dbbr_ref, dbbi_ref,
             dlr_ref, dli_ref, dldt_ref, dbr_ref, dbi_ref):
        _, vjp = jax.vjp(_disc, lr_ref[...], li_ref[...], ldt_ref[...], br_ref[...], bi_ref[...])
        dlr_ref[...], dli_ref[...], dldt_ref[...], dbr_ref[...], dbi_ref[...] = vjp(
            (dar_ref[...], dai_ref[...], dbbr_ref[...], dbbi_ref[...]))

    c1 = jax.ShapeDtypeStruct((SSM_NSTATE, 1), F32)
    c16 = jax.ShapeDtypeStruct((SSM_NSTATE, SSM_GROUP), F32)
    return pl.pallas_call(body, name="ssm_disc_bwd", out_shape=[c1, c1, c1, c16, c16])(
        lr, li, ldt, br, bi, dar, dai, dbbr, dbbi)


SSM_ROWS = 512
SSM_CW = SSM_NSTATE // SSM_CHUNKS
SSM_CU = SSM_WIDTH // SSM_CHUNKS


def _cmul(ar, ai, br, bi):
    return ar * br - ai * bi, ar * bi + ai * br


def _power(ar1, ai1, n):
    def step(_, c):
        return _cmul(c[0], c[1], ar1, ai1)

    return lax.fori_loop(0, n, step, (jnp.ones_like(ar1), jnp.zeros_like(ar1)))


def _tile(k):
    return pl.ds(pl.multiple_of(k * 8, 8), 8)


def _ssm_fwd_call(u, a_re, a_im, bb_re, bb_im, cm_re, cm_im, d_skip):
    L = u.shape[0]
    seg = L // 8
    rb = _fit(L, SSM_ROWS)

    def body(u_ref, ar_ref, ai_ref, bbr_ref, bbi_ref, cmr_ref, cmi_ref, d_ref, y_ref, sre_hbm, sim_hbm,
             s_re, s_im, sems):
        q = pl.program_id(0)

        def bu_step(r, c):
            rows = pl.ds(pl.multiple_of(r * rb, rb), rb)
            ub = u_ref[rows, :].astype(BF16)
            s_re[rows, :] = _dot(ub, bbr_ref[0], NN)
            s_im[rows, :] = _dot(ub, bbi_ref[0], NN)
            return c

        lax.fori_loop(0, L // rb, bu_step, 0)
        ar1, ai1 = ar_ref[...], ai_ref[...]
        ar = jnp.broadcast_to(ar1, (8, SSM_CW))
        ai = jnp.broadcast_to(ai1, (8, SSM_CW))

        def local(k, c):
            nr, ni = _cmul(ar, ai, c[0], c[1])
            nr = nr + s_re[_tile(k), :]
            ni = ni + s_im[_tile(k), :]
            s_re[_tile(k), :] = nr
            s_im[_tile(k), :] = ni
            return nr, ni

        zero8 = jnp.zeros((8, SSM_CW), F32)
        lax.fori_loop(0, seg, local, (zero8, zero8))
        pr, pi = _power(ar1, ai1, seg)
        end_r = s_re[pl.ds((seg - 1) * 8, 8), :]
        end_i = s_im[pl.ds((seg - 1) * 8, 8), :]
        er = jnp.zeros((1, SSM_CW), F32)
        ei = jnp.zeros((1, SSM_CW), F32)
        rows_r, rows_i = [er], [ei]
        for j in range(7):
            tr, ti = _cmul(pr, pi, er, ei)
            er, ei = end_r[j:j + 1] + tr, end_i[j:j + 1] + ti
            rows_r.append(er)
            rows_i.append(ei)
        e_r = jnp.concatenate(rows_r, axis=0)
        e_i = jnp.concatenate(rows_i, axis=0)

        def fix(k, c):
            wr, wi = _cmul(c[0], c[1], ar, ai)
            fr, fi = _cmul(wr, wi, e_r, e_i)
            s_re[_tile(k), :] += fr
            s_im[_tile(k), :] += fi
            return wr, wi

        lax.fori_loop(0, seg, fix, (jnp.ones((8, SSM_CW), F32), zero8))
        out_r = pltpu.make_async_copy(s_re, sre_hbm.at[q], sems.at[0])
        out_i = pltpu.make_async_copy(s_im, sim_hbm.at[q], sems.at[1])
        out_r.start()
        out_i.start()

        def y_step(r, c):
            rows = pl.ds(pl.multiple_of(r * rb, rb), rb)
            y = _dot(s_re[rows, :].astype(BF16), cmr_ref[0], NN) - _dot(s_im[rows, :].astype(BF16), cmi_ref[0], NN)
            y_ref[rows, :] = y + d_ref[...] * u_ref[rows, :]
            return c

        lax.fori_loop(0, L // rb, y_step, 0)
        out_r.wait()
        out_i.wait()

    chunk = lambda rows, cols: pl.BlockSpec((rows, cols), lambda q: (0, q))
    mat = lambda r, c: pl.BlockSpec((1, r, c), lambda q: (q, 0, 0))
    anyspec = pl.BlockSpec(memory_space=pl.ANY)
    states = jax.ShapeDtypeStruct((SSM_CHUNKS, L, SSM_CW), F32)
    return pl.pallas_call(
        body, name="ssm_fwd", grid=(SSM_CHUNKS,),
        in_specs=[chunk(L, SSM_CU), chunk(1, SSM_CW), chunk(1, SSM_CW), mat(SSM_CU, SSM_CW), mat(SSM_CU, SSM_CW),
                  mat(SSM_CW, SSM_CU), mat(SSM_CW, SSM_CU), chunk(1, SSM_CU)],
        out_specs=[chunk(L, SSM_CU), anyspec, anyspec],
        out_shape=[jax.ShapeDtypeStruct((L, SSM_WIDTH), F32), states, states],
        scratch_shapes=[pltpu.VMEM((L, SSM_CW), F32), pltpu.VMEM((L, SSM_CW), F32), pltpu.SemaphoreType.DMA((2,))],
        compiler_params=_params("arbitrary", vmem=VMEM_BIG))(u, a_re, a_im, bb_re, bb_im, cm_re, cm_im, d_skip)


def _ssm_bwd_call(dy, u, s_re_all, s_im_all, a_re, a_im, bb_re, bb_im, cm_re, cm_im, d_skip):
    L = u.shape[0]
    seg = L // 8
    rb = _fit(L, SSM_ROWS)

    def body(dy_ref, u_ref, sre_hbm, sim_hbm, ar_ref, ai_ref, bbr_ref, bbi_ref, cmr_ref, cmi_ref, d_ref,
             du_ref, dbbr_ref, dbbi_ref, dcmr_ref, dcmi_ref, dar_ref, dai_ref, dd_ref,
             g_re, g_im, s_re, s_im, sems):
        q = pl.program_id(0)
        in_r = pltpu.make_async_copy(sre_hbm.at[q], s_re, sems.at[0])
        in_i = pltpu.make_async_copy(sim_hbm.at[q], s_im, sems.at[1])
        in_r.start()
        in_i.start()

        def ds_step(r, c):
            rows = pl.ds(pl.multiple_of(r * rb, rb), rb)
            dyb = dy_ref[rows, :].astype(BF16)
            g_re[rows, :] = _dot(dyb, cmr_ref[0], NT)
            g_im[rows, :] = -_dot(dyb, cmi_ref[0], NT)
            return c

        lax.fori_loop(0, L // rb, ds_step, 0)
        ar1, ai1 = ar_ref[...], ai_ref[...]
        ar = jnp.broadcast_to(ar1, (8, SSM_CW))
        nai = jnp.broadcast_to(-ai1, (8, SSM_CW))

        def local(kk, c):
            k = seg - 1 - kk
            nr, ni = _cmul(ar, nai, c[0], c[1])
            nr = nr + g_re[_tile(k), :]
            ni = ni + g_im[_tile(k), :]
            g_re[_tile(k), :] = nr
            g_im[_tile(k), :] = ni
            return nr, ni

        zero8 = jnp.zeros((8, SSM_CW), F32)
        lax.fori_loop(0, seg, local, (zero8, zero8))
        pr, pi = _power(ar1, -ai1, seg)
        head_r = g_re[pl.ds(0, 8), :]
        head_i = g_im[pl.ds(0, 8), :]
        fr = jnp.zeros((1, SSM_CW), F32)
        fi = jnp.zeros((1, SSM_CW), F32)
        rows_r, rows_i = [fr], [fi]
        for j in range(6, -1, -1):
            tr, ti = _cmul(pr, pi, fr, fi)
            fr, fi = head_r[j + 1:j + 2] + tr, head_i[j + 1:j + 2] + ti
            rows_r.insert(0, fr)
            rows_i.insert(0, fi)
        f_r = jnp.concatenate(rows_r, axis=0)
        f_i = jnp.concatenate(rows_i, axis=0)
        in_r.wait()
        in_i.wait()

        def fixed(k, wr, wi):
            xr, xi = _cmul(wr, wi, f_r, f_i)
            gr = g_re[_tile(k), :] + xr
            gi = g_im[_tile(k), :] + xi
            g_re[_tile(k), :] = gr
            g_im[_tile(k), :] = gi
            return gr, gi

        def fix(kk, c):
            k = seg - 1 - kk
            wr, wi = _cmul(c[0], c[1], ar, nai)
            gr, gi = fixed(k, wr, wi)
            pr_, pi_ = s_re[_tile(k - 1), :], s_im[_tile(k - 1), :]
            return wr, wi, c[2] + gr * pr_ + gi * pi_, c[3] + gi * pr_ - gr * pi_

        wr, wi, acc_r, acc_i = lax.fori_loop(0, seg - 1, fix, (jnp.ones((8, SSM_CW), F32), zero8, zero8, zero8))
        wr, wi = _cmul(wr, wi, ar, nai)
        gr, gi = fixed(0, wr, wi)
        row8 = lax.broadcasted_iota(jnp.int32, (8, SSM_CW), 0)
        pr_ = jnp.where(row8 > 0, pltpu.roll(s_re[pl.ds((seg - 1) * 8, 8), :], 1, 0), 0.0)
        pi_ = jnp.where(row8 > 0, pltpu.roll(s_im[pl.ds((seg - 1) * 8, 8), :], 1, 0), 0.0)
        acc_r = acc_r + gr * pr_ + gi * pi_
        acc_i = acc_i + gi * pr_ - gr * pi_
        dar_ref[...] = jnp.sum(acc_r, axis=0, keepdims=True)
        dai_ref[...] = jnp.sum(acc_i, axis=0, keepdims=True)

        dbbr_ref[...] = jnp.zeros((1, SSM_CU, SSM_CW), F32)
        dbbi_ref[...] = jnp.zeros((1, SSM_CU, SSM_CW), F32)
        dcmr_ref[...] = jnp.zeros((1, SSM_CW, SSM_CU), F32)
        dcmi_ref[...] = jnp.zeros((1, SSM_CW, SSM_CU), F32)
        dd_ref[...] = jnp.zeros((1, SSM_CU), F32)

        def grad_step(r, c):
            rows = pl.ds(pl.multiple_of(r * rb, rb), rb)
            ub, dyv = u_ref[rows, :], dy_ref[rows, :]
            ubb, dyb = ub.astype(BF16), dyv.astype(BF16)
            grb, gib = g_re[rows, :].astype(BF16), g_im[rows, :].astype(BF16)
            dbbr_ref[0] += _dot(ubb, grb, TN)
            dbbi_ref[0] += _dot(ubb, gib, TN)
            dcmr_ref[0] += _dot(s_re[rows, :].astype(BF16), dyb, TN)
            dcmi_ref[0] -= _dot(s_im[rows, :].astype(BF16), dyb, TN)
            du_ref[rows, :] = _dot(grb, bbr_ref[0], NT) + _dot(gib, bbi_ref[0], NT) + d_ref[...] * dyv
            dd_ref[...] += jnp.sum(dyv * ub, axis=0, keepdims=True)
            return c

        lax.fori_loop(0, L // rb, grad_step, 0)

    chunk = lambda rows, cols: pl.BlockSpec((rows, cols), lambda q: (0, q))
    mat = lambda r, c: pl.BlockSpec((1, r, c), lambda q: (q, 0, 0))
    anyspec = pl.BlockSpec(memory_space=pl.ANY)
    big = lambda: pltpu.VMEM((L, SSM_CW), F32)
    return pl.pallas_call(
        body, name="ssm_bwd", grid=(SSM_CHUNKS,),
        in_specs=[chunk(L, SSM_CU), chunk(L, SSM_CU), anyspec, anyspec, chunk(1, SSM_CW), chunk(1, SSM_CW),
                  mat(SSM_CU, SSM_CW), mat(SSM_CU, SSM_CW), mat(SSM_CW, SSM_CU), mat(SSM_CW, SSM_CU), chunk(1, SSM_CU)],
        out_specs=[chunk(L, SSM_CU), mat(SSM_CU, SSM_CW), mat(SSM_CU, SSM_CW), mat(SSM_CW, SSM_CU), mat(SSM_CW, SSM_CU),
                   chunk(1, SSM_CW), chunk(1, SSM_CW), chunk(1, SSM_CU)],
        out_shape=[jax.ShapeDtypeStruct((L, SSM_WIDTH), F32),
                   jax.ShapeDtypeStruct((SSM_CHUNKS, SSM_CU, SSM_CW), F32), jax.ShapeDtypeStruct((SSM_CHUNKS, SSM_CU, SSM_CW), F32),
                   jax.ShapeDtypeStruct((SSM_CHUNKS, SSM_CW, SSM_CU), F32), jax.ShapeDtypeStruct((SSM_CHUNKS, SSM_CW, SSM_CU), F32),
                   jax.ShapeDtypeStruct((1, SSM_NSTATE), F32), jax.ShapeDtypeStruct((1, SSM_NSTATE), F32),
                   jax.ShapeDtypeStruct((1, SSM_WIDTH), F32)],
        scratch_shapes=[big(), big(), big(), big(), pltpu.SemaphoreType.DMA((2,))],
        compiler_params=_params("arbitrary", vmem=VMEM_BIG))(
            dy, u, s_re_all, s_im_all, a_re, a_im, bb_re, bb_im, cm_re, cm_im, d_skip)


def _place():
    return lax.axis_index("x"), lax.axis_index("y"), lax.axis_index("c")


def _all_gather_call(blocks, name, direct=False):
    n = len(blocks)

    def body(*refs):
        if direct:
            start, finish = _exchange_phases(refs[:n], refs[n:2 * n], *refs[2 * n:], same_source=True)
            start()
        else:
            start, forward, finish = _gather_phases(refs[:n], refs[n:2 * n], *refs[2 * n:])
            start()
            forward()
        finish()

    return pl.pallas_call(
        body, name=name, in_specs=[ANY_SPEC] * n, out_specs=[ANY_SPEC] * n,
        out_shape=[jax.ShapeDtypeStruct((N_DEV,) + b.shape, b.dtype) for b in blocks],
        scratch_shapes=_comm_sems(n))(*blocks)


def _comm_sems(n):
    return [pltpu.SemaphoreType.DMA((7 * n,)), pltpu.SemaphoreType.DMA((7 * n,)), pltpu.SemaphoreType.DMA((n,))]


def _gather_phases(x_refs, out_refs, send_sems, recv_sems, local_sems):
    x, y, c = _place()
    me, sibling = (x, y, c), (x, y, 1 - c)
    chips = [(1 - x, y), (x, 1 - y), (1 - x, 1 - y)]
    n = len(x_refs)

    def copy(k, a, blk, to, from_input=False):
        slot = out_refs[a].at[4 * blk[0] + 2 * blk[1] + blk[2]]
        return pltpu.make_async_remote_copy(
            src_ref=x_refs[a] if from_input else slot, dst_ref=slot,
            send_sem=send_sems.at[k * n + a], recv_sem=recv_sems.at[k * n + a], device_id=to, device_id_type=MESH_ID)

    mine = [pltpu.make_async_copy(x_refs[a], out_refs[a].at[4 * x + 2 * y + c], local_sems.at[a]) for a in range(n)]
    first, passed = [], []
    for a in range(n):
        first.append(copy(0, a, me, sibling, True))
        first += [copy(1 + j, a, me, (*chip, c), True) for j, chip in enumerate(chips)]
        passed += [copy(4 + j, a, (*chip, c), sibling) for j, chip in enumerate(chips)]

    def start():
        for cp in mine + first:
            cp.start()

    def forward():
        for j, chip in enumerate(chips):
            for a in range(n):
                copy(1 + j, a, (*chip, c), me).wait_recv()
                passed[3 * a + j].start()

    def finish():
        for a in range(n):
            copy(0, a, sibling, me).wait_recv()
            for j, chip in enumerate(chips):
                copy(4 + j, a, (*chip, 1 - c), me).wait_recv()
        for cp in first + passed:
            cp.wait_send()
        for cp in mine:
            cp.wait()

    return start, forward, finish


def _exchange_phases(p_refs, out_refs, send_sems, recv_sems, local_sems, same_source=False):
    x, y, c = _place()
    me = 4 * x + 2 * y + c
    n = len(p_refs)

    def flip(k):
        px = 1 - x if k & 4 else x
        py = 1 - y if k & 2 else y
        pc = 1 - c if k & 1 else c
        return (px, py, pc), 4 * px + 2 * py + pc

    def source(a, slot):
        return p_refs[a] if same_source else p_refs[a].at[slot]

    def copy(k, a, landing):
        peer, peer_slot = flip(k)
        return pltpu.make_async_remote_copy(
            src_ref=source(a, peer_slot), dst_ref=out_refs[a].at[peer_slot if landing else me],
            send_sem=send_sems.at[(k - 1) * n + a], recv_sem=recv_sems.at[(k - 1) * n + a],
            device_id=peer, device_id_type=MESH_ID)

    mine = [pltpu.make_async_copy(source(a, me), out_refs[a].at[me], local_sems.at[a]) for a in range(n)]
    sends = [copy(k, a, False) for k in range(1, N_DEV) for a in range(n)]

    def start():
        for cp in mine + sends:
            cp.start()

    def finish():
        for k in range(1, N_DEV):
            for a in range(n):
                copy(k, a, True).wait_recv()
        for cp in sends:
            cp.wait_send()
        for cp in mine:
            cp.wait()

    return start, finish


def _adam_math(g, w, m, v):
    c1 = 1.0 / (1.0 - ADAM_B1 ** ADAM_STEP)
    c2 = 1.0 / (1.0 - ADAM_B2 ** ADAM_STEP)
    m_new = ADAM_B1 * m + (1.0 - ADAM_B1) * g
    v_new = ADAM_B2 * v + (1.0 - ADAM_B2) * (g * g)
    delta = -ADAM_LR * ((m_new * c1) / (jnp.sqrt(v_new * c2) + ADAM_EPS) + ADAM_WD * w)
    return g, delta, m_new, v_new


def _sum_slices(s_ref):
    g = s_ref[0].astype(F32)
    for k in range(1, N_DEV):
        g = g + s_ref[k].astype(F32)
    return g


def _adam_call(slices, w, m, v, name):
    d1, rest = w.shape[1], w.shape[2:]
    zeros = (0,) * len(rest)
    by_lanes = len(rest) == 1 and d1 > 256 and d1 % 16 != 0
    if by_lanes:
        tile = _fit(rest[0], 256)
        steps = rest[0] // tile
        own = pl.BlockSpec((1, d1, tile), lambda i: (0, 0, i))
        sl = pl.BlockSpec((N_DEV, 1, d1, tile), lambda i: (0, 0, 0, i))
    else:
        tile = _fit(d1, 256, 16) if len(rest) == 1 else _fit(d1, 8, 8)
        steps = d1 // tile
        own = pl.BlockSpec((1, tile) + rest, lambda i: (0, i) + zeros)
        sl = pl.BlockSpec((N_DEV, 1, tile) + rest, lambda i: (0, 0, i) + zeros)

    def body(s_ref, w_ref, m_ref, v_ref, g_ref, d_ref, mo_ref, vo_ref):
        g_ref[...], d_ref[...], mo_ref[...], vo_ref[...] = _adam_math(_sum_slices(s_ref), w_ref[...], m_ref[...], v_ref[...])

    out = jax.ShapeDtypeStruct(w.shape, F32)
    return pl.pallas_call(
        body, name=name, grid=(steps,), in_specs=[sl, own, own, own],
        out_specs=[own, own, own, own], out_shape=[out, out, out, out],
        compiler_params=_params("parallel"))(slices, w, m, v)


def _adam_small_call(rows_all, row_params, slices, params):
    nr, n = len(row_params), len(row_params) + len(params)

    def row_sum(rows_ref, a, width):
        g = rows_ref[0, pl.ds(a, 1), pl.ds(0, width)]
        for k in range(1, N_DEV):
            g = g + rows_ref[k, pl.ds(a, 1), pl.ds(0, width)]
        return g

    def body(rows_ref, *refs):
        slice_refs, wmv, outs = refs[:n - nr], refs[n - nr:n - nr + 3 * n], refs[n - nr + 3 * n:]
        outs[4 * n][...] = row_sum(rows_ref, nr, LANES)
        for a in range(n):
            w_ref, m_ref, v_ref = wmv[3 * a:3 * a + 3]
            if a < nr:
                g = row_sum(rows_ref, a, w_ref.shape[1])
            else:
                g = _sum_slices(slice_refs[a - nr])
            res = _adam_math(g, w_ref[...], m_ref[...], v_ref[...])
            for r in range(4):
                outs[4 * a + r][...] = res[r]

    every = list(row_params) + list(params)
    flat = pl.pallas_call(
        body, name="adam_small",
        out_shape=[jax.ShapeDtypeStruct(w.shape, F32) for w, _, _ in every for _ in range(4)]
        + [jax.ShapeDtypeStruct((1, LANES), F32)],
    )(rows_all, *slices, *[t for wmv in every for t in wmv])
    return [flat[4 * a:4 * a + 4] for a in range(n)], flat[4 * n][0, 0]


BIG = (("w_in", 1024, 404, 1), ("w_uq", 384, 96, 1), ("w_uk", 256, 64, 1), ("w_uv", 256, 64, 1),
       ("w_glu", 64, 512, 0), ("w_branch_attn", 512, 128, 1), ("w_branch_ssm", 512, 128, 1),
       ("w_out", 128, 1024, 0), ("w_up", 1024, 704, 1), ("w_down", 352, 1024, 0), ("conv_w", 3, 704, 1))
BIG_MIX, BIG_FFN = BIG[:8], BIG[8:]
GRADS_EARLY, GRADS_LATE = BIG[8:] + BIG[4:8], BIG[:4]
SMALL = (("mix_norm_pre", (1024,)), ("q_norm", (384,)), ("kv_norm", (256,)), ("ssm_lambda_re", (32, 64)),
         ("ssm_lambda_im", (32, 64)), ("ssm_log_dt", (32,)), ("ssm_b_re", (32, 64, 16)), ("ssm_b_im", (32, 64, 16)),
         ("ssm_c_re", (32, 16, 64)), ("ssm_c_im", (32, 16, 64)), ("ssm_d", (32, 16)), ("b_glu", (512,)),
         ("b_gate", (2048,)), ("mix_norm_post", (1024,)), ("ffn_norm_pre", (1024,)), ("conv_b", (5632,)),
         ("ffn_norm_post", (1024,)))


TRANSPOSED = ("w_in", "w_uq", "w_uk", "w_uv", "w_up")


def _stored(name, arr):
    return jnp.swapaxes(arr, 1, 2) if name in TRANSPOSED else arr


def _to_slices(name, full, rows, cols, axis):
    if name in TRANSPOSED:
        return full.T.reshape(N_DEV, cols, rows)
    if axis == 1:
        return full.reshape(rows, N_DEV, cols).transpose(1, 0, 2)
    return full.reshape(N_DEV, rows, cols)


def _from_slices(name, parts, rows, cols, axis):
    if name in TRANSPOSED:
        return parts.reshape(N_DEV * cols, rows)
    if axis == 1:
        return parts.transpose(1, 0, 2).reshape(rows, N_DEV * cols)
    return parts.reshape(N_DEV * rows, cols)


def _head_unpad_cols(w, width):
    k = w.shape[0]
    return w.reshape(k, N_HEADS, LANES)[:, :, :width].reshape(k, N_HEADS * width)


def _time_perm(a, L):
    return a.reshape(8, L // 8, a.shape[-1]).transpose(1, 0, 2).reshape(L, a.shape[-1])


def _time_unperm(a, L):
    return a.reshape(L // 8, 8, a.shape[-1]).transpose(1, 0, 2).reshape(L, a.shape[-1])


def _block_diag(w, rows_first):
    eye = jnp.eye(8, dtype=w.dtype)
    g = w.reshape(SSM_CHUNKS, 8, w.shape[1], w.shape[2])
    return jnp.einsum("qgrc,gk->qgrkc", g, eye).reshape(SSM_CHUNKS, 8 * w.shape[1], 8 * w.shape[2])


def _block_diag_t(m, r, c):
    eye = jnp.eye(8, dtype=m.dtype)
    return jnp.einsum("qgrkc,gk->qgrc", m.reshape(SSM_CHUNKS, 8, r, 8, c), eye).reshape(SSM_GROUPS, r, c)


def kernel(x, positions, mix_norm_pre, w_in, q_norm, w_uq, kv_norm, w_uk, w_uv, ssm_lambda_re, ssm_lambda_im, ssm_log_dt, ssm_b_re, ssm_b_im, ssm_c_re, ssm_c_im, ssm_d, w_glu, b_glu, w_branch_attn, w_branch_ssm, b_gate, w_out, mix_norm_post, ffn_norm_pre, w_up, conv_w, conv_b, w_down, ffn_norm_post, loss_target, m_mix_norm_pre, m_w_in, m_q_norm, m_w_uq, m_kv_norm, m_w_uk, m_w_uv, m_ssm_lambda_re, m_ssm_lambda_im, m_ssm_log_dt, m_ssm_b_re, m_ssm_b_im, m_ssm_c_re, m_ssm_c_im, m_ssm_d, m_w_glu, m_b_glu, m_w_branch_attn, m_w_branch_ssm, m_b_gate, m_w_out, m_mix_norm_post, m_ffn_norm_pre, m_w_up, m_conv_w, m_conv_b, m_w_down, m_ffn_norm_post, v_mix_norm_pre, v_w_in, v_q_norm, v_w_uq, v_kv_norm, v_w_uk, v_w_uv, v_ssm_lambda_re, v_ssm_lambda_im, v_ssm_log_dt, v_ssm_b_re, v_ssm_b_im, v_ssm_c_re, v_ssm_c_im, v_ssm_d, v_w_glu, v_b_glu, v_w_branch_attn, v_w_branch_ssm, v_b_gate, v_w_out, v_mix_norm_post, v_ffn_norm_pre, v_w_up, v_conv_w, v_conv_b, v_w_down, v_ffn_norm_post):
    given = dict(locals())
    L = x.shape[1]
    xs = x[0]
    target = loss_target[0]

    def shard_bits(group):
        return [given[name][0] if name == "conv_w" else _stored(name, given[name])[0].astype(BF16) for name, _, _, _ in group]

    W = {}

    def unpack_weights(gathered, group):
        for (name, rows, cols, axis), parts in zip(group, gathered):
            W[name] = _from_slices(name, parts, rows, cols, axis)

    unpack_weights(_all_gather_call(shard_bits(BIG_MIX[:1]), "gather_w_in"), BIG_MIX[:1])

    wit = W["w_in"]
    zero_rows = lambda r: jnp.zeros((r, D_MODEL), BF16)
    kr_end = P_KR + QK_ROPE
    w_in_pt = jnp.concatenate(
        [wit[:P_KR], zero_rows(QK_NOPE), wit[P_KR:kr_end], zero_rows(LANES - QK_HEAD), wit[kr_end:]], axis=0)

    hn1 = _rms_fwd_call(xs, mix_norm_pre, "rms_pre")
    proj, *gathered_mix = _mm(hn1, w_in_pt, "mm_in", tb=True, tn=1664, gather=shard_bits(BIG_MIX[1:]))
    unpack_weights(gathered_mix, BIG_MIX[1:])
    head_rows = lambda wt, width: jnp.pad(wt.reshape(N_HEADS, width, wt.shape[1]), ((0, 0), (0, LANES - width), (0, 0)))
    w_uq_pt = head_rows(W["w_uq"], QK_HEAD).reshape(HEAD_PAD, Q_RANK)
    w_kv_pt = jnp.stack([head_rows(W["w_uk"], QK_NOPE), head_rows(W["w_uv"], V_HEAD)], axis=1
                        ).reshape(2 * HEAD_PAD, KV_RANK)
    w_ba_p = jnp.pad(W["w_branch_attn"].reshape(N_HEADS, V_HEAD, D_MODEL), ((0, 0), (0, LANES - V_HEAD), (0, 0))
                     ).reshape(HEAD_PAD, D_MODEL)
    half = jnp.arange(QK_ROPE // 2, dtype=F32)
    inv_freq = ROPE_THETA ** (-2.0 * half / QK_ROPE)
    inv_freq = jnp.pad(jnp.concatenate([inv_freq, inv_freq]), (QK_NOPE, LANES - QK_HEAD)).reshape(1, LANES)
    pos_col = positions.astype(F32).reshape(L, 1)
    qn, ckvn, q_r, kv_r, cosf, sinf = _mla_proj_call(proj, q_norm, kv_norm, w_uq_pt, w_kv_pt, pos_col, inv_freq)
    attn, lse, *gathered_ffn = _attn_fwd_call(q_r, kv_r, shard_bits(BIG_FFN))
    unpack_weights(gathered_ffn, BIG_FFN)

    col = lambda a: a.reshape(SSM_NSTATE, -1)
    lr_c, li_c = col(ssm_lambda_re[0]), col(ssm_lambda_im[0])
    ldt_c = col(jnp.broadcast_to(ssm_log_dt[0][:, None], (SSM_GROUPS, SSM_STATE)))
    br_c, bi_c = col(ssm_b_re[0]), col(ssm_b_im[0])
    a_re_c, a_im_c, bb_re_c, bb_im_c = _disc_call(lr_c, li_c, ldt_c, br_c, bi_c)
    a_re, a_im = a_re_c.reshape(1, SSM_NSTATE), a_im_c.reshape(1, SSM_NSTATE)
    to_bb = lambda b: _block_diag(b.reshape(SSM_GROUPS, SSM_STATE, SSM_GROUP).transpose(0, 2, 1), True).astype(BF16)
    bb_re, bb_im = to_bb(bb_re_c), to_bb(bb_im_c)
    to_cm = lambda c_: _block_diag(c_[0].transpose(0, 2, 1), True).astype(BF16)
    cm_re, cm_im = to_cm(ssm_c_re), to_cm(ssm_c_im)
    d_skip = ssm_d.reshape(1, SSM_WIDTH)
    u_p = _time_perm(proj[:, P_U:P_GATE], L)
    y1, s_re, s_im = _ssm_fwd_call(u_p, a_re, a_im, bb_re, bb_im, cm_re, cm_im, d_skip)
    w_glu_b = W["w_glu"]
    ssm_p = _glu_call(y1, w_glu_b, b_glu)
    ssm = _time_unperm(ssm_p, L)

    pa = _mm(attn, w_ba_p, "mm_ba")
    ps = _mm(ssm, W["w_branch_ssm"], "mm_bs")
    merged = _merge_call(proj, b_gate, pa, ps)
    o = _mm(merged, W["w_out"], "mm_out")
    x2, hn2 = _post_mix_call(o, xs, mix_norm_post, ffn_norm_pre)
    h = _mm(hn2, W["w_up"], "mm_up", tb=True, tn=1408)
    cw = W["conv_w"]
    act = _conv_act_call(h, cw, conv_b)
    ff = _mm(act, W["w_down"], "mm_down", tn=1024, tk=1408)
    loss_row, dy, dff, g_ffn_norm_post = _ffn_out_call(ff, x2, target, ffn_norm_post)

    da = _mm(dff, W["w_down"], "mm_down_dx", tb=True, tn=1408)
    g_w_down = _mm_tn(act, dff, "mm_down_dw", tm=1408)
    dgate, dval, dcw_g, dcw_v, dcb_g, dcb_v = _conv_act_bwd_call(da, h, cw, conv_b)
    g_conv_w = jnp.concatenate([dcw_g, dcw_v], axis=1)
    g_conv_b = jnp.concatenate([dcb_g, dcb_v], axis=1)
    dh = _conv_t_call(dgate, dval, cw)
    dhn2 = _mm(dh, W["w_up"], "mm_up_dx", tn=1024, tk=1408)
    g_w_up = _mm_tn(hn2, dh, "mm_up_dw")
    dx2, do, g_ffn_norm_pre, g_mix_norm_post = _post_bwd_call(x2, dhn2, dy, o, ffn_norm_pre, mix_norm_post)
    dmerged = _mm(do, W["w_out"], "mm_out_dx", tb=True)
    g_w_out = _mm_tn(merged, do, "mm_out_dw")
    dpa, dps, dl0, dl1, db0, db1 = _merge_bwd_call(dmerged, proj, b_gate, pa, ps)
    g_b_gate = jnp.concatenate([db0, db1], axis=1)
    dattn = _mm(dpa, w_ba_p, "mm_ba_dx", tb=True, out_dtype=BF16)
    g_w_ba = _mm_tn(attn, dpa, "mm_ba_dw").reshape(N_HEADS, LANES, D_MODEL)[:, :V_HEAD].reshape(N_HEADS * V_HEAD, D_MODEL)
    dssm = _mm(dps, W["w_branch_ssm"], "mm_bs_dx", tb=True)
    g_w_bs = _mm_tn(ssm, dps, "mm_bs_dw")

    dy1, g_w_glu, g_b_glu = _glu_bwd_call(_time_perm(dssm, L), y1, w_glu_b, b_glu)
    du_p, dbb_re, dbb_im, dcm_re, dcm_im, da_re, da_im, g_ssm_d = _ssm_bwd_call(
        dy1, u_p, s_re, s_im, a_re, a_im, bb_re, bb_im, cm_re, cm_im, d_skip)
    du = _time_unperm(du_p, L)
    from_bb = lambda m: col(_block_diag_t(m, SSM_GROUP, SSM_STATE).transpose(0, 2, 1))
    dlr, dli, dldt, dbr, dbi = _disc_bwd_call(
        lr_c, li_c, ldt_c, br_c, bi_c, da_re.reshape(SSM_NSTATE, 1), da_im.reshape(SSM_NSTATE, 1), from_bb(dbb_re), from_bb(dbb_im))
    g_c_re = _block_diag_t(dcm_re, SSM_STATE, SSM_GROUP).transpose(0, 2, 1)
    g_c_im = _block_diag_t(dcm_im, SSM_STATE, SSM_GROUP).transpose(0, 2, 1)

    def grad_slices(group, grads):
        return [_to_slices(name, grads[name], rows, cols, axis) for name, rows, cols, axis in group]

    early_grads = {"w_up": g_w_up, "w_down": g_w_down, "conv_w": g_conv_w, "w_glu": g_w_glu.astype(BF16),
                   "w_branch_attn": g_w_ba, "w_branch_ssm": g_w_bs, "w_out": g_w_out}
    ssm_partials = {"ssm_lambda_re": dlr, "ssm_lambda_im": dli, "ssm_b_re": dbr, "ssm_b_im": dbi,
                    "ssm_c_re": g_c_re, "ssm_c_im": g_c_im, "ssm_d": g_ssm_d}
    ssm_shapes = [(name, shp) for name, shp in SMALL if name in ssm_partials]
    dq, dkv, *landed = _attn_bwd_call(
        q_r, kv_r, attn, dattn, lse, grad_slices(GRADS_EARLY, early_grads),
        [ssm_partials[name].reshape(-1, LANES) if len(shp) == 3 else ssm_partials[name].reshape((1,) + shp)
         for name, shp in ssm_shapes])
    received_early = landed[:len(GRADS_EARLY)]
    ssm_all = {name: got.reshape((N_DEV, 1) + shp) for (name, shp), got in zip(ssm_shapes, landed[len(GRADS_EARLY):])}
    dq_p, dkv_p, dlatent, g_q_norm, g_kv_norm = _mla_proj_bwd_call(
        dq, dkv, cosf, sinf, proj, q_norm, kv_norm, w_uq_pt, w_kv_pt)
    g_w_uq = _head_unpad_cols(_mm_tn(qn, dq_p, "mm_uq_dw"), QK_HEAD)
    g_w_kv = _mm_tn(ckvn, dkv_p, "mm_ukv_dw").reshape(KV_RANK, N_HEADS, 2, LANES)
    g_w_uk = g_w_kv[:, :, 0, :QK_NOPE].reshape(KV_RANK, N_HEADS * QK_NOPE)
    g_w_uv = g_w_kv[:, :, 1, :V_HEAD].reshape(KV_RANK, N_HEADS * V_HEAD)
    dproj = jnp.concatenate([dlatent, du.astype(BF16), dl0, dl1], axis=1)
    g_w_in_p = _mm_tn(hn1, dproj, "mm_in_dw", tk=1024)
    g_w_in = jnp.concatenate([g_w_in_p[:, :P_KR], g_w_in_p[:, P_KR + QK_NOPE:P_KR + QK_HEAD], g_w_in_p[:, P_U:]], axis=1)
    late_grads = {"w_in": g_w_in, "w_uq": g_w_uq, "w_uk": g_w_uk, "w_uv": g_w_uv}
    dhn1, *received_late = _mm(dproj, w_in_pt, "mm_in_dx", tk=1664, exchange=grad_slices(GRADS_LATE, late_grads))
    grad_x, g_mix_norm_pre = _pre_bwd_call(xs, dhn1, dx2, mix_norm_pre)

    results = {}
    for group, received in ((GRADS_EARLY, received_early), (GRADS_LATE, received_late)):
        for (name, _, _, _), rec in zip(group, received):
            stored = [_stored(name, given[prefix + name]) for prefix in ("", "m_", "v_")]
            results[name] = [_stored(name, r) for r in _adam_call(rec[:, None], *stored, "adam_" + name)]

    vec_grads = {"mix_norm_pre": g_mix_norm_pre, "q_norm": g_q_norm, "kv_norm": g_kv_norm,
                 "ssm_log_dt": jnp.sum(dldt.reshape(SSM_GROUPS, SSM_STATE), axis=1),
                 "b_glu": g_b_glu, "b_gate": g_b_gate, "mix_norm_post": g_mix_norm_post,
                 "ffn_norm_pre": g_ffn_norm_pre, "conv_b": g_conv_b, "ffn_norm_post": g_ffn_norm_post}
    vec_names = [name for name, _ in SMALL if name in vec_grads]
    width = max(shp[0] for name, shp in SMALL if name in vec_grads)
    rows = [jnp.pad(vec_grads[name].reshape(1, -1), ((0, 0), (0, width - vec_grads[name].size))) for name in vec_names]
    rows.append(jnp.pad(loss_row, ((0, 0), (0, width - LANES))))
    rows.append(jnp.zeros((-len(rows) % 8, width), F32))
    rows_all, = _all_gather_call([jnp.concatenate(rows, axis=0)], "gather_small_grads", direct=True)
    wmv = lambda name: (given[name], given["m_" + name], given["v_" + name])
    few = ["ssm_lambda_re", "ssm_lambda_im", "ssm_d"]
    small_results, loss = _adam_small_call(
        rows_all, [wmv(n) for n in vec_names], [ssm_all[n] for n in few], [wmv(n) for n in few])
    results.update(zip(vec_names + few, small_results))
    for name in ("ssm_b_re", "ssm_b_im", "ssm_c_re", "ssm_c_im"):
        results[name] = _adam_call(ssm_all[name], *wmv(name), "adam_" + name)

    order = ["mix_norm_pre", "w_in", "q_norm", "w_uq", "kv_norm", "w_uk", "w_uv", "ssm_lambda_re", "ssm_lambda_im",
             "ssm_log_dt", "ssm_b_re", "ssm_b_im", "ssm_c_re", "ssm_c_im", "ssm_d", "w_glu", "b_glu", "w_branch_attn",
             "w_branch_ssm", "b_gate", "w_out", "mix_norm_post", "ffn_norm_pre", "w_up", "conv_w", "conv_b", "w_down",
             "ffn_norm_post"]
    outs = [loss, grad_x[None]]
    for kind in range(4):
        outs += [results[name][kind] for name in order]
    return tuple(outs)
```

```python
import math

import jax
import jax.numpy as jnp
from jax import lax
from jax.experimental import pallas as pl
from jax.experimental.pallas import tpu as pltpu

F32 = jnp.float32
BF16 = jnp.bfloat16
MESH_ID = pl.DeviceIdType.MESH

N_DEV = 8
LANES = 128
D_MODEL = 1024
N_HEADS = 8
QK_NOPE = 64
QK_ROPE = 32
QK_HEAD = QK_NOPE + QK_ROPE
V_HEAD = 64
Q_RANK = 384
KV_RANK = 256
ROPE_THETA = 10000.0
SSM_WIDTH = 512
SSM_GROUP = 16
SSM_GROUPS = 32
SSM_STATE = 64
SSM_NSTATE = SSM_GROUPS * SSM_STATE
SSM_CHUNKS = 4
D_FF = 2816
EPS = 1e-6
ADAM_LR, ADAM_B1, ADAM_B2, ADAM_EPS, ADAM_WD, ADAM_STEP = 0.001, 0.9, 0.999, 1e-08, 0.01, 10

P_CQ, P_CKV, P_KR, P_U, P_GATE = 0, 384, 640, 768, 1280
HEAD_PAD = N_HEADS * LANES

VMEM_BIG = 52 * 1024 * 1024

_GELU_C0 = math.sqrt(2.0 / math.pi)
_GELU_C1 = 0.044715
NEG = -1e30


def _fit(n, pref, mult=LANES):
    if n <= pref:
        return n
    t = (pref // mult) * mult
    while t > 0 and n % t:
        t -= mult
    assert t > 0, (n, pref, mult)
    return t


def _gelu(x):
    return x * (0.5 * (1.0 + jnp.tanh(_GELU_C0 * x * (1.0 + _GELU_C1 * (x * x)))))


def _gelu_and_grad(x):
    x2 = x * x
    t = jnp.tanh(_GELU_C0 * x * (1.0 + _GELU_C1 * x2))
    half = 0.5 * (1.0 + t)
    return x * half, half + 0.5 * x * (1.0 - t * t) * _GELU_C0 * (1.0 + 3.0 * _GELU_C1 * x2)


def _sigmoid(x):
    return 1.0 / (1.0 + jnp.exp(-x))


def _dot(a, b, dims):
    return lax.dot_general(a, b, (dims, ((), ())), preferred_element_type=F32)


NN = ((1,), (0,))
NT = ((1,), (1,))
TN = ((0,), (0,))


def _params(*sem, vmem=None):
    return pltpu.CompilerParams(dimension_semantics=tuple(sem), vmem_limit_bytes=vmem)


def _mm(a, b, name, tb=False, out_dtype=F32, tm=1024, tn=1024, tk=1024, gather=()):
    M, K = a.shape
    if tb:
        N, K2 = b.shape
    else:
        K2, N = b.shape
    assert K == K2, (a.shape, b.shape, tb)
    tm, tn, tk = _fit(M, tm), _fit(N, tn), _fit(K, tk)
    nk = K // tk
    grid = (M // tm, N // tn, nk)
    steps = grid[0] * grid[1] * grid[2]
    dims = NT if tb else NN
    n = len(gather)

    def body(a_ref, b_ref, *refs):
        o_ref, scratch = refs[n], refs[2 * n + 1:]
        step = (pl.program_id(0) * grid[1] + pl.program_id(1)) * grid[2] + pl.program_id(2)
        if n:
            start, forward, finish = _gather_phases(refs[:n], refs[n + 1:2 * n + 1], *scratch[-3:])
            pl.when(step == 0)(start)
            pl.when(step == steps // 2)(forward)
        part = _dot(a_ref[...].astype(BF16), b_ref[...].astype(BF16), dims)
        if nk == 1:
            o_ref[...] = part.astype(out_dtype)
        else:
            acc_ref = scratch[0]
            k = pl.program_id(2)

            @pl.when(k == 0)
            def _():
                acc_ref[...] = part

            @pl.when(k > 0)
            def _():
                acc_ref[...] += part

            @pl.when(k == nk - 1)
            def _():
                o_ref[...] = acc_ref[...].astype(out_dtype)
        if n:
            pl.when(step == steps - 1)(finish)

    a_spec = pl.BlockSpec((tm, tk), lambda i, j, k: (i, k))
    b_spec = pl.BlockSpec((tn, tk), lambda i, j, k: (j, k)) if tb else pl.BlockSpec((tk, tn), lambda i, j, k: (k, j))
    landed = [jax.ShapeDtypeStruct((N_DEV,) + p.shape, p.dtype) for p in gather]
    out = pl.pallas_call(
        body, name=name, grid=grid,
        in_specs=[a_spec, b_spec] + [ANY_SPEC] * n,
        out_specs=[pl.BlockSpec((tm, tn), lambda i, j, k: (i, j))] + [ANY_SPEC] * n,
        out_shape=[jax.ShapeDtypeStruct((M, N), out_dtype)] + landed,
        scratch_shapes=([] if nk == 1 else [pltpu.VMEM((tm, tn), F32)]) + (_comm_sems(n) if n else []),
        compiler_params=_params(*(("arbitrary",) * 3 if n else ("parallel", "parallel", "arbitrary")), vmem=VMEM_BIG),
    )(a, b, *gather)
    return out if n else out[0]


def _mm_rows(a, b, name, epilogue, rows_in, vecs_in, rows_out, vecs_out, tb=False, tk=1024):
    M, K = a.shape
    N = b.shape[0] if tb else b.shape[1]
    tm, tk = _fit(M, 512), _fit(K, tk)
    nk = K // tk
    nr, nv, nro = len(rows_in), len(vecs_in), len(rows_out)

    def body(a_ref, b_ref, *refs):
        ins, outs, acc_ref = refs[:nr + nv], refs[nr + nv:nr + nv + nro + len(vecs_out)], refs[-1]
        i, k = pl.program_id(0), pl.program_id(1)
        part = _dot(a_ref[...], b_ref[...], NT if tb else NN)

        def finish(product):
            res = epilogue(product, *[r[...] for r in ins])
            for ref, val in zip(outs[:nro], res[:nro]):
                ref[...] = val.astype(ref.dtype)
            for ref, val in zip(outs[nro:], res[nro:]):
                _acc(ref, i == 0, val)

        if nk == 1:
            finish(part)
        else:
            @pl.when(k == 0)
            def _():
                acc_ref[...] = part

            @pl.when(jnp.logical_and(k > 0, k < nk - 1))
            def _():
                acc_ref[...] += part

            @pl.when(k == nk - 1)
            def _():
                finish(acc_ref[...] + part)

    row = lambda w: pl.BlockSpec((tm, w), lambda i, k: (i, 0))
    vec = lambda w: pl.BlockSpec((1, w), lambda i, k: (0, 0))
    b_spec = pl.BlockSpec((N, tk), lambda i, k: (0, k)) if tb else pl.BlockSpec((tk, N), lambda i, k: (k, 0))
    return pl.pallas_call(
        body, name=name, grid=(M // tm, nk),
        in_specs=[pl.BlockSpec((tm, tk), lambda i, k: (i, k)), b_spec] + [row(r.shape[1]) for r in rows_in]
        + [vec(v.shape[1]) for v in vecs_in],
        out_specs=[row(w) for w, _ in rows_out] + [vec(w) for w in vecs_out],
        out_shape=[jax.ShapeDtypeStruct((M, w), dt) for w, dt in rows_out] + [jax.ShapeDtypeStruct((1, w), F32) for w in vecs_out],
        scratch_shapes=[pltpu.VMEM((tm, N), F32)],
        compiler_params=_params("arbitrary", "arbitrary", vmem=VMEM_BIG))(a, b, *rows_in, *vecs_in)


def _mm_in_dx_call(dproj, w_in_pt, x, dx2, g_pre, exchange):
    L, K = dproj.shape
    N = w_in_pt.shape[1]
    tm, tk = _fit(L, 512), _fit(K, 1664)
    nm, nk = L // tm, K // tk
    n = len(exchange)

    def body(a_ref, b_ref, x_ref, dx2_ref, g_ref, *refs):
        parts, (gx_ref, dg_ref), got = refs[:n], refs[n:n + 2], refs[n + 2:2 * n + 2]
        acc_ref = refs[2 * n + 2]
        i, k = pl.program_id(0), pl.program_id(1)
        start, finish = _exchange_phases(parts, got, *refs[2 * n + 3:])
        pl.when(jnp.logical_and(i == 0, k == 0))(start)
        part = _dot(a_ref[...], b_ref[...], NN)

        @pl.when(k == 0)
        def _():
            acc_ref[...] = part

        @pl.when(jnp.logical_and(k > 0, k < nk - 1))
        def _():
            acc_ref[...] += part

        @pl.when(k == nk - 1)
        def _():
            d1, dg = _rms_bwd(x_ref[...], g_ref[...], acc_ref[...] + part)
            gx_ref[...] = dx2_ref[...] + d1
            _acc(dg_ref, i == 0, dg)

        pl.when(jnp.logical_and(i == nm - 1, k == nk - 1))(finish)

    assert nk >= 2
    rows = lambda: pl.BlockSpec((tm, N), lambda i, k: (i, 0))
    return pl.pallas_call(
        body, name="mm_in_dx", grid=(nm, nk),
        in_specs=[pl.BlockSpec((tm, tk), lambda i, k: (i, k)), pl.BlockSpec((tk, N), lambda i, k: (k, 0)),
                  rows(), rows(), pl.BlockSpec((1, N), lambda i, k: (0, 0))] + [ANY_SPEC] * n,
        out_specs=[rows(), pl.BlockSpec((1, N), lambda i, k: (0, 0))] + [ANY_SPEC] * n,
        out_shape=[jax.ShapeDtypeStruct((L, N), F32), jax.ShapeDtypeStruct((1, N), F32)]
        + [jax.ShapeDtypeStruct(p.shape, p.dtype) for p in exchange],
        scratch_shapes=[pltpu.VMEM((tm, N), F32)] + _comm_sems(n),
        compiler_params=_params("arbitrary", "arbitrary", vmem=VMEM_BIG))(dproj, w_in_pt, x, dx2, g_pre, *exchange)


TN_CHUNK = 512


def _mm_tn(a, b, name, tm=512, tk=512):
    K, M = a.shape
    K2, N = b.shape
    assert K == K2, (a.shape, b.shape)
    tm, tk, cn = _fit(M, tm), _fit(K, tk), _fit(N, TN_CHUNK)
    nk = K // tk

    def body(a_ref, b_ref, o_ref, acc_ref):
        k = pl.program_id(1)

        @pl.when(k == 0)
        def _():
            acc_ref[...] = jnp.zeros((tm, N), F32)

        at = a_ref[...].astype(BF16).T
        for c in range(N // cn):
            cols = slice(c * cn, (c + 1) * cn)
            acc_ref[:, cols] += _dot(at, b_ref[:, cols].astype(BF16), NN)

        @pl.when(k == nk - 1)
        def _():
            o_ref[...] = acc_ref[...].astype(BF16)

    return pl.pallas_call(
        body, name=name, grid=(M // tm, nk),
        in_specs=[pl.BlockSpec((tk, tm), lambda i, k: (k, i)), pl.BlockSpec((tk, N), lambda i, k: (k, 0))],
        out_specs=pl.BlockSpec((tm, N), lambda i, k: (i, 0)),
        out_shape=jax.ShapeDtypeStruct((M, N), BF16),
        scratch_shapes=[pltpu.VMEM((tm, N), F32)],
        compiler_params=_params("parallel", "arbitrary", vmem=VMEM_BIG))(a, b)


def _row(tl, n, col=0):
    return pl.BlockSpec((tl, n), lambda i: (i, col))


def _full(shape):
    return pl.BlockSpec(shape, lambda i: (0,) * len(shape))


def _rms(x, g):
    r = lax.rsqrt(jnp.mean(x * x, axis=-1, keepdims=True) + EPS)
    return x * r * g


def _rms_bwd(x, g, dy):
    n = x.shape[-1]
    r = lax.rsqrt(jnp.mean(x * x, axis=-1, keepdims=True) + EPS)
    gy = dy * g
    dx = r * gy - x * (r * r * r * (1.0 / n)) * jnp.sum(x * gy, axis=-1, keepdims=True)
    return dx, jnp.sum(dy * x * r, axis=0, keepdims=True)


def _acc(ref, first, val):
    @pl.when(first)
    def _():
        ref[...] = val

    @pl.when(jnp.logical_not(first))
    def _():
        ref[...] += val


def _rms_fwd_call(x, g, name):
    L, n = x.shape
    tl = _fit(L, 512)

    def body(x_ref, g_ref, o_ref):
        o_ref[...] = _rms(x_ref[...], g_ref[...]).astype(BF16)

    return pl.pallas_call(
        body, name=name, grid=(L // tl,), in_specs=[_row(tl, n), _full((1, n))], out_specs=_row(tl, n),
        out_shape=jax.ShapeDtypeStruct((L, n), BF16), compiler_params=_params("parallel"))(x, g)


def _rope_lanes(shape):
    lane = lax.broadcasted_iota(jnp.int32, shape, 1)
    return lane, jnp.logical_and(lane >= QK_NOPE, lane < QK_HEAD)


def _rope_apply(x, cosf, sinf, lane):
    rot = jnp.where(lane < QK_NOPE + QK_ROPE // 2, -pltpu.roll(x, LANES - QK_ROPE // 2, 1), pltpu.roll(x, QK_ROPE // 2, 1))
    return x * cosf + rot * sinf


def _rope_apply_t(dy, cosf, sinf, lane, is_rope):
    g = dy * sinf
    rot_t = jnp.where(lane < QK_NOPE + QK_ROPE // 2, pltpu.roll(g, LANES - QK_ROPE // 2, 1), -pltpu.roll(g, QK_ROPE // 2, 1))
    return dy * cosf + jnp.where(is_rope, rot_t, 0.0)


def _mla_proj_call(proj, q_norm, kv_norm, w_uq_pt, w_kv_pt, pos_col, inv_freq):
    L = proj.shape[0]
    tl = _fit(L, 512)

    def body(p_ref, gq_ref, gk_ref, wq_ref, wkv_ref, pos_ref, f_ref, qn_ref, kn_ref, qo_ref, kvo_ref, cos_ref, sin_ref):
        qn = _rms(p_ref[:, P_CQ:P_CKV], gq_ref[...]).astype(BF16)
        kn = _rms(p_ref[:, P_CKV:P_KR], gk_ref[...]).astype(BF16)
        qn_ref[...] = qn
        kn_ref[...] = kn
        q_pad = _dot(qn, wq_ref[...], NT)
        kv_pad = _dot(kn, wkv_ref[...], NT)
        lane, is_rope = _rope_lanes((tl, LANES))
        ang = pos_ref[...] * f_ref[...]
        cosf = jnp.where(is_rope, jnp.cos(ang), jnp.where(lane < QK_NOPE, 1.0, 0.0))
        sinf = jnp.where(is_rope, jnp.sin(ang), 0.0)
        cos_ref[...] = cosf
        sin_ref[...] = sinf
        kr = _rope_apply(p_ref[:, P_KR:P_U], cosf, sinf, lane)
        for h in range(N_HEADS):
            qh = _rope_apply(q_pad[:, h * LANES:(h + 1) * LANES], cosf, sinf, lane)
            qo_ref[:, h * LANES:(h + 1) * LANES] = (qh * Q_PRESCALE).astype(BF16)
            kvo_ref[:, 2 * h * LANES:(2 * h + 1) * LANES] = (kv_pad[:, 2 * h * LANES:(2 * h + 1) * LANES] + kr).astype(BF16)
            vh = jnp.where(lane == V_HEAD, 1.0, kv_pad[:, (2 * h + 1) * LANES:(2 * h + 2) * LANES])
            kvo_ref[:, (2 * h + 1) * LANES:(2 * h + 2) * LANES] = vh.astype(BF16)

    shape = lambda n, dt: jax.ShapeDtypeStruct((L, n), dt)
    return pl.pallas_call(
        body, name="mla_proj", grid=(L // tl,),
        in_specs=[_row(tl, P_U), _full((1, Q_RANK)), _full((1, KV_RANK)), _full((HEAD_PAD, Q_RANK)),
                  _full((2 * HEAD_PAD, KV_RANK)), _row(tl, 1), _full((1, LANES))],
        out_specs=[_row(tl, Q_RANK), _row(tl, KV_RANK), _row(tl, HEAD_PAD), _row(tl, 2 * HEAD_PAD), _row(tl, LANES), _row(tl, LANES)],
        out_shape=[shape(Q_RANK, BF16), shape(KV_RANK, BF16), shape(HEAD_PAD, BF16), shape(2 * HEAD_PAD, BF16),
                   shape(LANES, F32), shape(LANES, F32)],
        compiler_params=_params("parallel"))(proj, q_norm, kv_norm, w_uq_pt, w_kv_pt, pos_col, inv_freq)


def _mla_proj_bwd_call(dq, dkv, cosf, sinf, proj, q_norm, kv_norm, w_uq_pt, w_kv_pt):
    L = dq.shape[0]
    tl = _fit(L, 512)

    def body(dq_ref, dkv_ref, cos_ref, sin_ref, p_ref, gq_ref, gk_ref, wq_ref, wkv_ref,
             dqo_ref, dkvo_ref, d_ref, dgq_ref, dgk_ref):
        first = pl.program_id(0) == 0
        lane, is_rope = _rope_lanes((tl, LANES))
        cosf, sinf = cos_ref[...], sin_ref[...]
        dk_sum = jnp.zeros((tl, LANES), F32)
        for h in range(N_HEADS):
            dqo_ref[:, h * LANES:(h + 1) * LANES] = _rope_apply_t(dq_ref[:, h * LANES:(h + 1) * LANES], cosf, sinf, lane, is_rope).astype(BF16)
            dk_sum = dk_sum + dkv_ref[:, 2 * h * LANES:(2 * h + 1) * LANES]
        dkvo_ref[...] = dkv_ref[...].astype(BF16)
        dqn = _dot(dqo_ref[...], wq_ref[...], NN)
        dkn = _dot(dkvo_ref[...], wkv_ref[...], NN)
        dcq, dgq = _rms_bwd(p_ref[:, P_CQ:P_CKV], gq_ref[...], dqn)
        dckv, dgk = _rms_bwd(p_ref[:, P_CKV:P_KR], gk_ref[...], dkn)
        d_ref[:, P_CQ:P_CKV] = dcq.astype(BF16)
        d_ref[:, P_CKV:P_KR] = dckv.astype(BF16)
        d_ref[:, P_KR:P_U] = _rope_apply_t(dk_sum, cosf, sinf, lane, is_rope).astype(BF16)
        _acc(dgq_ref, first, dgq)
        _acc(dgk_ref, first, dgk)

    shape = lambda n: jax.ShapeDtypeStruct((L, n), BF16)
    return pl.pallas_call(
        body, name="mla_proj_bwd", grid=(L // tl,),
        in_specs=[_row(tl, HEAD_PAD), _row(tl, 2 * HEAD_PAD), _row(tl, LANES), _row(tl, LANES), _row(tl, P_KR),
                  _full((1, Q_RANK)), _full((1, KV_RANK)), _full((HEAD_PAD, Q_RANK)), _full((2 * HEAD_PAD, KV_RANK))],
        out_specs=[_row(tl, HEAD_PAD), _row(tl, 2 * HEAD_PAD), _row(tl, P_U), _full((1, Q_RANK)), _full((1, KV_RANK))],
        out_shape=[shape(HEAD_PAD), shape(2 * HEAD_PAD), shape(P_U), jax.ShapeDtypeStruct((1, Q_RANK), F32),
                   jax.ShapeDtypeStruct((1, KV_RANK), F32)],
        compiler_params=_params("arbitrary"))(dq, dkv, cosf, sinf, proj, q_norm, kv_norm, w_uq_pt, w_kv_pt)


GATE_TILE = 256
GATE_ROWS = 1024


def _merge_call(proj, b_gate, pa, ps):
    L = proj.shape[0]
    tl = _fit(L, GATE_ROWS)
    nc = D_MODEL // GATE_TILE
    g0, g1 = P_GATE // GATE_TILE, (P_GATE + D_MODEL) // GATE_TILE

    def body(l0_ref, l1_ref, b0_ref, b1_ref, pa_ref, ps_ref, o_ref):
        s0 = _sigmoid(l0_ref[...] + b0_ref[...])
        s1 = _sigmoid(l1_ref[...] + b1_ref[...])
        o_ref[...] = (s0 * pa_ref[...] + s1 * ps_ref[...]).astype(BF16)

    blk = lambda off: pl.BlockSpec((tl, GATE_TILE), lambda i, j: (i, off + j))
    bias = lambda off: pl.BlockSpec((1, GATE_TILE), lambda i, j: (0, off + j))
    return pl.pallas_call(
        body, name="merge", grid=(L // tl, nc),
        in_specs=[blk(g0), blk(g1), bias(0), bias(nc), blk(0), blk(0)],
        out_specs=blk(0), out_shape=jax.ShapeDtypeStruct((L, D_MODEL), BF16),
        compiler_params=_params("parallel", "parallel"))(proj, proj, b_gate, b_gate, pa, ps)


def _merge_bwd_call(dm, proj, b_gate, pa, ps):
    L = proj.shape[0]
    tl = _fit(L, GATE_ROWS)
    nc = D_MODEL // GATE_TILE
    g0, g1 = P_GATE // GATE_TILE, (P_GATE + D_MODEL) // GATE_TILE

    def body(dm_ref, l0_ref, l1_ref, b0_ref, b1_ref, pa_ref, ps_ref, dpa_ref, dps_ref, dl0_ref, dl1_ref, db0_ref, db1_ref):
        first = pl.program_id(1) == 0
        dm_ = dm_ref[...]
        s0 = _sigmoid(l0_ref[...] + b0_ref[...])
        s1 = _sigmoid(l1_ref[...] + b1_ref[...])
        dpa_ref[...] = (dm_ * s0).astype(BF16)
        dps_ref[...] = (dm_ * s1).astype(BF16)
        dl0 = dm_ * pa_ref[...] * s0 * (1.0 - s0)
        dl1 = dm_ * ps_ref[...] * s1 * (1.0 - s1)
        dl0_ref[...] = dl0.astype(BF16)
        dl1_ref[...] = dl1.astype(BF16)
        _acc(db0_ref, first, jnp.sum(dl0, axis=0, keepdims=True))
        _acc(db1_ref, first, jnp.sum(dl1, axis=0, keepdims=True))

    blk = lambda off: pl.BlockSpec((tl, GATE_TILE), lambda j, i: (i, off + j))
    bias = lambda off: pl.BlockSpec((1, GATE_TILE), lambda j, i: (0, off + j))
    act = jax.ShapeDtypeStruct((L, D_MODEL), BF16)
    vec = jax.ShapeDtypeStruct((1, D_MODEL), F32)
    return pl.pallas_call(
        body, name="merge_bwd", grid=(nc, L // tl),
        in_specs=[blk(0), blk(g0), blk(g1), bias(0), bias(nc), blk(0), blk(0)],
        out_specs=[blk(0), blk(0), blk(0), blk(0), bias(0), bias(0)],
        out_shape=[act, act, act, act, vec, vec],
        compiler_params=_params("parallel", "arbitrary"))(dm, proj, proj, b_gate, b_gate, pa, ps)


def _post_mix_rows(o, x, g_post, g_fpre):
    x2 = x + _rms(o, g_post)
    return o, x2, _rms(x2, g_fpre)


def _ffn_out_rows(ff, x2, target, g_fpost):
    n = ff.shape[-1]
    err = x2 + _rms(ff, g_fpost) - target
    part = 0.5 * jnp.sum(jnp.sum(err * err, axis=-1, keepdims=True) * (1.0 / n), axis=0, keepdims=True)
    dy = err * (1.0 / n)
    dff, dg = _rms_bwd(ff, g_fpost, dy)
    return dy, dff, jnp.broadcast_to(part, (1, LANES)), dg


def _post_bwd_rows(dhn2, x2, dy, o, g_fpre, g_post):
    d1, dgf = _rms_bwd(x2, g_fpre, dhn2)
    dx2 = dy + d1
    do, dgp = _rms_bwd(o, g_post, dx2)
    return dx2, do, dgf, dgp


CONV_TILE = 256
CONV_WIDE = 1408
HALO = 16


def _conv3(w, b, x0, x1, x2):
    return b + w[2:3] * x0 + w[1:2] * x1 + w[0:1] * x2


def _down(x, by):
    return pltpu.roll(x, by, 0)


def _edge_down(edge, before, by):
    r = lax.broadcasted_iota(jnp.int32, edge.shape, 0)
    return jnp.where(r < by, pltpu.roll(before, by, 0), pltpu.roll(edge, by, 0))


def _edge_up(edge, after, by):
    r = lax.broadcasted_iota(jnp.int32, edge.shape, 0)
    return jnp.where(r >= HALO - by, pltpu.roll(after, HALO - by, 0), pltpu.roll(edge, HALO - by, 0))


def _gated(w_g, b_g, w_v, b_v, hg, hv, g1, g2, v1, v2):
    return _conv3(w_g, b_g, hg, g1, g2), _conv3(w_v, b_v, hv, v1, v2)


def _conv_specs(tl, tc, rows_inner):
    nh = tl // HALO
    if rows_inner:
        ij = lambda f: (lambda j, i: f(i, j))
    else:
        ij = lambda f: f
    cur = lambda off: pl.BlockSpec((tl, tc), ij(lambda i, j: (i, off + j)))
    prev = lambda off: pl.BlockSpec((HALO, tc), ij(lambda i, j: (jnp.maximum(i * nh - 1, 0), off + j)))
    par = lambda rows, off: pl.BlockSpec((rows, tc), ij(lambda i, j: (0, off + j)))
    return cur, prev, par


def _conv_act_call(h, conv_w, conv_b):
    L = h.shape[0]
    tl = _fit(L, 256)
    nc = D_FF // CONV_WIDE
    cur, prev, par = _conv_specs(tl, CONV_WIDE, False)

    def body(hg_ref, hv_ref, pg_ref, pv_ref, wg_ref, wv_ref, bg_ref, bv_ref, a_ref):
        not_first = (pl.program_id(0) > 0).astype(F32)
        par = (wg_ref[...], bg_ref[...], wv_ref[...], bv_ref[...])
        hg, hv = hg_ref[...], hv_ref[...]
        gate, val = _gated(*par, hg, hv, _down(hg, 1), _down(hg, 2), _down(hv, 1), _down(hv, 2))
        a_ref[...] = (_gelu(gate) * val).astype(BF16)
        eg, ev, bg, bv = hg[:HALO], hv[:HALO], pg_ref[...] * not_first, pv_ref[...] * not_first
        gate, val = _gated(*par, eg, ev, _edge_down(eg, bg, 1), _edge_down(eg, bg, 2),
                           _edge_down(ev, bv, 1), _edge_down(ev, bv, 2))
        a_ref[:HALO, :] = (_gelu(gate) * val).astype(BF16)

    return pl.pallas_call(
        body, name="conv_act", grid=(L // tl, nc),
        in_specs=[cur(0), cur(nc), prev(0), prev(nc), par(3, 0), par(3, nc), par(1, 0), par(1, nc)],
        out_specs=cur(0), out_shape=jax.ShapeDtypeStruct((L, D_FF), BF16),
        compiler_params=_params("parallel", "parallel"))(h, h, h, h, conv_w, conv_w, conv_b, conv_b)


def _conv_act_bwd_call(da, h, conv_w, conv_b):
    L = h.shape[0]
    tl = _fit(L, 512)
    nc = D_FF // CONV_TILE
    cur, prev, par = _conv_specs(tl, CONV_TILE, True)

    def body(da_ref, hg_ref, hv_ref, pg_ref, pv_ref, wg_ref, wv_ref, bg_ref, bv_ref,
             dg_ref, dv_ref, dwg_ref, dwv_ref, dbg_ref, dbv_ref):
        first = pl.program_id(1) == 0
        not_first = (pl.program_id(1) > 0).astype(F32)
        par = (wg_ref[...], bg_ref[...], wv_ref[...], bv_ref[...])
        col = lambda t: jnp.sum(t, axis=0, keepdims=True)

        def grads(da_, hg, hv, g1, g2, v1, v2):
            gate, val = _gated(*par, hg, hv, g1, g2, v1, v2)
            act, slope = _gelu_and_grad(gate)
            dgate = da_ * val * slope
            dval = da_ * act
            sums = (jnp.concatenate([col(dgate * g2), col(dgate * g1), col(dgate * hg)], axis=0),
                    jnp.concatenate([col(dval * v2), col(dval * v1), col(dval * hv)], axis=0), col(dgate), col(dval))
            return dgate, dval, sums

        da_, hg, hv = da_ref[...], hg_ref[...], hv_ref[...]
        shifted = (_down(hg, 1), _down(hg, 2), _down(hv, 1), _down(hv, 2))
        dgate, dval, whole = grads(da_, hg, hv, *shifted)
        dg_ref[...] = dgate.astype(BF16)
        dv_ref[...] = dval.astype(BF16)
        edge = lambda t: t[:HALO]
        _, _, wrapped = grads(edge(da_), edge(hg), edge(hv), *[edge(s) for s in shifted])
        eg, ev, bg, bv = edge(hg), edge(hv), pg_ref[...] * not_first, pv_ref[...] * not_first
        dgate, dval, fixed = grads(edge(da_), eg, ev, _edge_down(eg, bg, 1), _edge_down(eg, bg, 2),
                                   _edge_down(ev, bv, 1), _edge_down(ev, bv, 2))
        dg_ref[:HALO, :] = dgate.astype(BF16)
        dv_ref[:HALO, :] = dval.astype(BF16)
        for ref, a, b, c in zip((dwg_ref, dwv_ref, dbg_ref, dbv_ref), whole, wrapped, fixed):
            _acc(ref, first, a - b + c)

    act = jax.ShapeDtypeStruct((L, D_FF), BF16)
    w3 = jax.ShapeDtypeStruct((3, D_FF), F32)
    w1 = jax.ShapeDtypeStruct((1, D_FF), F32)
    return pl.pallas_call(
        body, name="conv_act_bwd", grid=(nc, L // tl),
        in_specs=[cur(0), cur(0), cur(nc), prev(0), prev(nc), par(3, 0), par(3, nc), par(1, 0), par(1, nc)],
        out_specs=[cur(0), cur(0), par(3, 0), par(3, 0), par(1, 0), par(1, 0)],
        out_shape=[act, act, w3, w3, w1, w1],
        compiler_params=_params("parallel", "arbitrary"))(da, h, h, h, h, conv_w, conv_w, conv_b, conv_b)


def _conv_t_call(dgate, dval, conv_w):
    L = dgate.shape[0]
    tl = _fit(L, 512)
    nc = D_FF // CONV_WIDE
    nh = tl // HALO

    def body(dg_ref, dv_ref, ng_ref, nv_ref, w_ref, o_ref):
        not_last = (pl.program_id(0) < L // tl - 1).astype(F32)

        def emit(d_ref, n_ref):
            c = d_ref[...].astype(F32)
            w = w_ref[...]
            o_ref[...] = _conv3(w, 0.0, c, pltpu.roll(c, tl - 1, 0), pltpu.roll(c, tl - 2, 0)).astype(BF16)
            edge, after = c[tl - HALO:], n_ref[...].astype(F32) * not_last
            o_ref[tl - HALO:, :] = _conv3(w, 0.0, edge, _edge_up(edge, after, 1), _edge_up(edge, after, 2)).astype(BF16)

        pl.when(pl.program_id(1) < nc)(lambda: emit(dg_ref, ng_ref))
        pl.when(pl.program_id(1) >= nc)(lambda: emit(dv_ref, nv_ref))

    gate_col = lambda j: jnp.minimum(j, nc - 1)
    val_col = lambda j: jnp.maximum(j - nc, 0)
    after_row = lambda i: jnp.minimum((i + 1) * nh, L // HALO - 1)
    tile = lambda col: pl.BlockSpec((tl, CONV_WIDE), lambda i, j: (i, col(j)))
    after = lambda col: pl.BlockSpec((HALO, CONV_WIDE), lambda i, j: (after_row(i), col(j)))
    return pl.pallas_call(
        body, name="conv_t", grid=(L // tl, 2 * nc),
        in_specs=[tile(gate_col), tile(val_col), after(gate_col), after(val_col), pl.BlockSpec((3, CONV_WIDE), lambda i, j: (0, j))],
        out_specs=pl.BlockSpec((tl, CONV_WIDE), lambda i, j: (i, j)),
        out_shape=jax.ShapeDtypeStruct((L, 2 * D_FF), BF16),
        compiler_params=_params("parallel", "parallel"))(dgate, dval, dgate, dval, conv_w)


def _glu_call(y1, w_glu, b_glu):
    L, n = y1.shape
    tl = _fit(L, 512)

    def body(y_ref, w_ref, b_ref, o_ref):
        y2 = _gelu(y_ref[...])
        z = _dot(y2.astype(BF16), w_ref[...], NN) + b_ref[...]
        o_ref[...] = (y2 * _sigmoid(z)).astype(BF16)

    return pl.pallas_call(
        body, name="glu", grid=(L // tl,), in_specs=[_row(tl, n), _full((n, n)), _full((1, n))],
        out_specs=_row(tl, n), out_shape=jax.ShapeDtypeStruct((L, n), BF16),
        compiler_params=_params("parallel"))(y1, w_glu, b_glu)


def _glu_bwd_call(dout, y1, w_glu, b_glu):
    L, n = y1.shape
    tl = _fit(L, 512)

    def body(do_ref, y_ref, w_ref, b_ref, dy_ref, dw_ref, db_ref):
        first = pl.program_id(0) == 0
        y1_ = y_ref[...]
        y2, slope = _gelu_and_grad(y1_)
        y2b = y2.astype(BF16)
        w = w_ref[...]
        sg = _sigmoid(_dot(y2b, w, NN) + b_ref[...])
        dout_ = do_ref[...].astype(F32)
        dz = dout_ * y2 * sg * (1.0 - sg)
        dzb = dz.astype(BF16)
        dy2 = dout_ * sg + _dot(dzb, w, NT)
        dy_ref[...] = dy2 * slope
        _acc(dw_ref, first, _dot(y2b, dzb, TN))
        _acc(db_ref, first, jnp.sum(dz, axis=0, keepdims=True))

    return pl.pallas_call(
        body, name="glu_bwd", grid=(L // tl,),
        in_specs=[_row(tl, n), _row(tl, n), _full((n, n)), _full((1, n))],
        out_specs=[_row(tl, n), _full((n, n)), _full((1, n))],
        out_shape=[jax.ShapeDtypeStruct((L, n), F32), jax.ShapeDtypeStruct((n, n), F32), jax.ShapeDtypeStruct((1, n), F32)],
        compiler_params=_params("arbitrary"))(dout, y1, w_glu, b_glu)


ATTN_TILE = 1024
ATTN_SCALE = 1.0 / math.sqrt(QK_HEAD)


ATTN_HEADS = 2
ATTN_GROUPS = N_HEADS // ATTN_HEADS
LOG2E = 1.0 / math.log(2.0)
Q_PRESCALE = ATTN_SCALE * LOG2E
ANY_SPEC = pl.BlockSpec(memory_space=pl.ANY)


def _attn_fwd_call(q, kv, blocks):
    L = q.shape[0]
    t = _fit(L, ATTN_TILE)
    nq = L // t
    n = len(blocks)

    def body(q_ref, kv_ref, *refs):
        blk_refs, (o_ref, lse_ref), gat_refs = refs[:n], refs[n:n + 2], refs[n + 2:2 * n + 2]
        m_s, acc_s, send_sems, recv_sems, local_sems = refs[2 * n + 2:]
        g, i = pl.program_id(0), pl.program_id(1)
        start, forward, finish = _gather_phases(blk_refs, gat_refs, send_sems, recv_sems, local_sems)
        pl.when(jnp.logical_and(g == 0, i == 0))(start)
        m_s[...] = jnp.full((ATTN_HEADS, t, 1), NEG, F32)
        acc_s[...] = jnp.zeros((ATTN_HEADS, t, LANES), F32)
        below = lax.broadcasted_iota(jnp.int32, (t, t), 1) <= lax.broadcasted_iota(jnp.int32, (t, t), 0)

        def block_step(kb, on_diagonal):
            rows = pl.ds(pl.multiple_of(kb * t, t), t)
            for a in range(ATTN_HEADS):
                s = _dot(q_ref[:, a * LANES:(a + 1) * LANES], kv_ref[rows, 2 * a * LANES:(2 * a + 1) * LANES], NT)
                if on_diagonal:
                    s = jnp.where(below, s, NEG)
                m_prev = m_s[a]
                m_new = jnp.maximum(m_prev, jnp.max(s, axis=1, keepdims=True))
                p = jnp.exp2(s - m_new)
                pv = _dot(p.astype(BF16), kv_ref[rows, (2 * a + 1) * LANES:(2 * a + 2) * LANES], NN)
                acc_s[a] = jnp.exp2(m_prev - m_new) * acc_s[a] + pv
                m_s[a] = m_new

        def step(kb, carry):
            block_step(kb, False)
            return carry

        lax.fori_loop(0, i, step, 0)
        block_step(i, True)
        lane = lax.broadcasted_iota(jnp.int32, (t, LANES), 1)
        for a in range(ATTN_HEADS):
            acc = acc_s[a]
            l = jnp.sum(jnp.where(lane == V_HEAD, acc, 0.0), axis=1, keepdims=True)
            o_ref[:, a * LANES:(a + 1) * LANES] = (acc / l).astype(BF16)
            lse_ref[a] = m_s[a] + jnp.log(l) * LOG2E
        pl.when(jnp.logical_and(g == (3 * ATTN_GROUPS) // 4, i == 0))(forward)
        pl.when(jnp.logical_and(g == ATTN_GROUPS - 1, i == nq - 1))(finish)

    gw = ATTN_HEADS * LANES
    return pl.pallas_call(
        body, name="attn_fwd", grid=(ATTN_GROUPS, nq),
        in_specs=[pl.BlockSpec((t, gw), lambda g, i: (i, g)),
                  pl.BlockSpec((L, 2 * gw), lambda g, i: (0, g))] + [ANY_SPEC] * n,
        out_specs=[pl.BlockSpec((t, gw), lambda g, i: (i, g)),
                   pl.BlockSpec((ATTN_HEADS, t, 1), lambda g, i: (g, i, 0))] + [ANY_SPEC] * n,
        out_shape=[jax.ShapeDtypeStruct((L, HEAD_PAD), BF16), jax.ShapeDtypeStruct((N_HEADS, L, 1), F32)]
        + [jax.ShapeDtypeStruct((N_DEV,) + b.shape, b.dtype) for b in blocks],
        scratch_shapes=[pltpu.VMEM((ATTN_HEADS, t, 1), F32), pltpu.VMEM((ATTN_HEADS, t, LANES), F32)] + _comm_sems(n),
        compiler_params=_params("arbitrary", "arbitrary", vmem=VMEM_BIG))(q, kv, *blocks)


def _attn_bwd_call(q, kv, o, do, lse, parts, blocks):
    L = q.shape[0]
    t = _fit(L, ATTN_TILE)
    nq = L // t
    n1, n = len(parts), len(parts) + len(blocks)

    def body(q_ref, do_ref, o_ref, lse_ref, kv_ref, *refs):
        in_refs, (dq_ref, dkv_ref), out_refs = refs[:n], refs[n:n + 2], refs[n + 2:2 * n + 2]
        dk_s, dv_s = refs[2 * n + 2:2 * n + 4]
        g, j = pl.program_id(0), pl.program_id(1)
        start, finish = _exchange_phases(in_refs[:n1], out_refs[:n1], *refs[2 * n + 4:2 * n + 7])
        start_blocks, finish_blocks = _exchange_phases(in_refs[n1:], out_refs[n1:], *refs[2 * n + 7:], same_source=True)

        @pl.when(jnp.logical_and(g == 0, j == 0))
        def _():
            start()
            start_blocks()

        @pl.when(j == 0)
        def _():
            dq_ref[...] = jnp.zeros((L, ATTN_HEADS * LANES), F32)

        dk_s[...] = jnp.zeros((ATTN_HEADS, t, LANES), F32)
        dv_s[...] = jnp.zeros((ATTN_HEADS, t, LANES), F32)
        below = lax.broadcasted_iota(jnp.int32, (t, t), 1) <= lax.broadcasted_iota(jnp.int32, (t, t), 0)

        def block_step(i, on_diagonal):
            rows = pl.ds(pl.multiple_of(i * t, t), t)
            for a in range(ATTN_HEADS):
                lanes = slice(a * LANES, (a + 1) * LANES)
                qi = q_ref[rows, lanes]
                doi = do_ref[rows, lanes]
                kblk = kv_ref[:, 2 * a * LANES:(2 * a + 1) * LANES]
                delta = jnp.sum(doi.astype(F32) * o_ref[rows, lanes].astype(F32), axis=1, keepdims=True)
                s = _dot(qi, kblk, NT)
                if on_diagonal:
                    s = jnp.where(below, s, NEG)
                p = jnp.exp2(s - lse_ref[a, rows, :])
                dv_s[a] += _dot(p.astype(BF16), doi, TN)
                ds = (p * (_dot(doi, kv_ref[:, (2 * a + 1) * LANES:(2 * a + 2) * LANES], NT) - delta)).astype(BF16)
                dk_s[a] += _dot(ds, qi, TN)
                dq_ref[rows, lanes] += _dot(ds, kblk, NN) * ATTN_SCALE

        def step(i, carry):
            block_step(i, False)
            return carry

        block_step(j, True)
        lax.fori_loop(j + 1, nq, step, 0)
        for a in range(ATTN_HEADS):
            dkv_ref[:, 2 * a * LANES:(2 * a + 1) * LANES] = dk_s[a] * (1.0 / LOG2E)
            dkv_ref[:, (2 * a + 1) * LANES:(2 * a + 2) * LANES] = dv_s[a]
        @pl.when(jnp.logical_and(g == ATTN_GROUPS - 1, j == nq - 1))
        def _():
            finish()
            finish_blocks()

    gw = ATTN_HEADS * LANES
    whole = lambda: pl.BlockSpec((L, gw), lambda g, j: (0, g))
    acc = pltpu.VMEM((ATTN_HEADS, t, LANES), F32)
    return pl.pallas_call(
        body, name="attn_bwd", grid=(ATTN_GROUPS, nq),
        in_specs=[whole(), whole(), whole(), pl.BlockSpec((ATTN_HEADS, L, 1), lambda g, j: (g, 0, 0)),
                  pl.BlockSpec((t, 2 * gw), lambda g, j: (j, g))] + [ANY_SPEC] * n,
        out_specs=[whole(), pl.BlockSpec((t, 2 * gw), lambda g, j: (j, g))] + [ANY_SPEC] * n,
        out_shape=[jax.ShapeDtypeStruct((L, HEAD_PAD), F32), jax.ShapeDtypeStruct((L, 2 * HEAD_PAD), F32)]
        + [jax.ShapeDtypeStruct(p.shape, p.dtype) for p in parts]
        + [jax.ShapeDtypeStruct((N_DEV,) + b.shape, b.dtype) for b in blocks],
        scratch_shapes=[acc, acc] + _comm_sems(n1) + _comm_sems(n - n1),
        compiler_params=_params("arbitrary", "arbitrary", vmem=VMEM_BIG))(q, do, o, lse, kv, *parts, *blocks)


def _disc(lr, li, ldt, br, bi):
    dt = jnp.exp(ldt)
    mag = jnp.exp(lr * dt)
    ang = li * dt
    a_re, a_im = mag * jnp.cos(ang), mag * jnp.sin(ang)
    den = lr * lr + li * li
    n_re, n_im = a_re - 1.0, a_im
    z_re = (n_re * lr + n_im * li) / den
    z_im = (n_im * lr - n_re * li) / den
    return a_re, a_im, z_re * br - z_im * bi, z_re * bi + z_im * br


def _disc_call(lr, li, ldt, br, bi):
    def body(lr_ref, li_ref, ldt_ref, br_ref, bi_ref, ar_ref, ai_ref, bbr_ref, bbi_ref):
        ar_ref[...], ai_ref[...], bbr_ref[...], bbi_ref[...] = _disc(
            lr_ref[...], li_ref[...], ldt_ref[...], br_ref[...], bi_ref[...])

    c1 = jax.ShapeDtypeStruct((SSM_NSTATE, 1), F32)
    c16 = jax.ShapeDtypeStruct((SSM_NSTATE, SSM_GROUP), F32)
    return pl.pallas_call(body, name="ssm_disc", out_shape=[c1, c1, c16, c16])(lr, li, ldt, br, bi)


def _disc_bwd_call(lr, li, ldt, br, bi, dar, dai, dbbr, dbbi):
    def body(lr_ref, li_ref, ldt_ref, br_ref, bi_ref, dar_ref, dai_ref, dbbr_ref, dbbi_ref,
             dlr_ref, dli_ref, dldt_ref, dbr_ref, dbi_ref):
        _, vjp = jax.vjp(_disc, lr_ref[...], li_ref[...], ldt_ref[...], br_ref[...], bi_ref[...])
        dlr_ref[...], dli_ref[...], dldt_ref[...], dbr_ref[...], dbi_ref[...] = vjp(
            (dar_ref[...], dai_ref[...], dbbr_ref[...], dbbi_ref[...]))

    c1 = jax.ShapeDtypeStruct((SSM_NSTATE, 1), F32)
    c16 = jax.ShapeDtypeStruct((SSM_NSTATE, SSM_GROUP), F32)
    return pl.pallas_call(body, name="ssm_disc_bwd", out_shape=[c1, c1, c1, c16, c16])(
        lr, li, ldt, br, bi, dar, dai, dbbr, dbbi)


SSM_ROWS = 512
SSM_CW = SSM_NSTATE // SSM_CHUNKS
SSM_CU = SSM_WIDTH // SSM_CHUNKS


def _cmul(ar, ai, br, bi):
    return ar * br - ai * bi, ar * bi + ai * br


def _power(ar1, ai1, n):
    def step(_, c):
        return _cmul(c[0], c[1], ar1, ai1)

    return lax.fori_loop(0, n, step, (jnp.ones_like(ar1), jnp.zeros_like(ar1)))


def _tile(k):
    return pl.ds(pl.multiple_of(k * 8, 8), 8)


def _ssm_fwd_call(u, a_re, a_im, bb_re, bb_im, cm_re, cm_im, d_skip):
    L = u.shape[0]
    seg = L // 8
    rb = _fit(L, SSM_ROWS)

    def body(u_ref, ar_ref, ai_ref, bbr_ref, bbi_ref, cmr_ref, cmi_ref, d_ref, y_ref, sre_hbm, sim_hbm,
             s_re, s_im, sems):
        q = pl.program_id(0)

        def bu_step(r, c):
            rows = pl.ds(pl.multiple_of(r * rb, rb), rb)
            ub = u_ref[rows, :].astype(BF16)
            s_re[rows, :] = _dot(ub, bbr_ref[0], NN)
            s_im[rows, :] = _dot(ub, bbi_ref[0], NN)
            return c

        lax.fori_loop(0, L // rb, bu_step, 0)
        ar1, ai1 = ar_ref[...], ai_ref[...]
        ar = jnp.broadcast_to(ar1, (8, SSM_CW))
        ai = jnp.broadcast_to(ai1, (8, SSM_CW))

        def local(k, c):
            nr, ni = _cmul(ar, ai, c[0], c[1])
            nr = nr + s_re[_tile(k), :]
            ni = ni + s_im[_tile(k), :]
            s_re[_tile(k), :] = nr
            s_im[_tile(k), :] = ni
            return nr, ni

        zero8 = jnp.zeros((8, SSM_CW), F32)
        lax.fori_loop(0, seg, local, (zero8, zero8))
        pr, pi = _power(ar1, ai1, seg)
        end_r = s_re[pl.ds((seg - 1) * 8, 8), :]
        end_i = s_im[pl.ds((seg - 1) * 8, 8), :]
        er = jnp.zeros((1, SSM_CW), F32)
        ei = jnp.zeros((1, SSM_CW), F32)
        rows_r, rows_i = [er], [ei]
        for j in range(7):
            tr, ti = _cmul(pr, pi, er, ei)
            er, ei = end_r[j:j + 1] + tr, end_i[j:j + 1] + ti
            rows_r.append(er)
            rows_i.append(ei)
        e_r = jnp.concatenate(rows_r, axis=0)
        e_i = jnp.concatenate(rows_i, axis=0)

        def fix(k, c):
            wr, wi = _cmul(c[0], c[1], ar, ai)
            fr, fi = _cmul(wr, wi, e_r, e_i)
            s_re[_tile(k), :] += fr
            s_im[_tile(k), :] += fi
            return wr, wi

        lax.fori_loop(0, seg, fix, (jnp.ones((8, SSM_CW), F32), zero8))
        out_r = pltpu.make_async_copy(s_re, sre_hbm.at[q], sems.at[0])
        out_i = pltpu.make_async_copy(s_im, sim_hbm.at[q], sems.at[1])
        out_r.start()
        out_i.start()

        def y_step(r, c):
            rows = pl.ds(pl.multiple_of(r * rb, rb), rb)
            y = _dot(s_re[rows, :].astype(BF16), cmr_ref[0], NN) - _dot(s_im[rows, :].astype(BF16), cmi_ref[0], NN)
            y_ref[rows, :] = y + d_ref[...] * u_ref[rows, :]
            return c

        lax.fori_loop(0, L // rb, y_step, 0)
        out_r.wait()
        out_i.wait()

    chunk = lambda rows, cols: pl.BlockSpec((rows, cols), lambda q: (0, q))
    mat = lambda r, c: pl.BlockSpec((1, r, c), lambda q: (q, 0, 0))
    anyspec = pl.BlockSpec(memory_space=pl.ANY)
    states = jax.ShapeDtypeStruct((SSM_CHUNKS, L, SSM_CW), F32)
    return pl.pallas_call(
        body, name="ssm_fwd", grid=(SSM_CHUNKS,),
        in_specs=[chunk(L, SSM_CU), chunk(1, SSM_CW), chunk(1, SSM_CW), mat(SSM_CU, SSM_CW), mat(SSM_CU, SSM_CW),
                  mat(SSM_CW, SSM_CU), mat(SSM_CW, SSM_CU), chunk(1, SSM_CU)],
        out_specs=[chunk(L, SSM_CU), anyspec, anyspec],
        out_shape=[jax.ShapeDtypeStruct((L, SSM_WIDTH), F32), states, states],
        scratch_shapes=[pltpu.VMEM((L, SSM_CW), F32), pltpu.VMEM((L, SSM_CW), F32), pltpu.SemaphoreType.DMA((2,))],
        compiler_params=_params("arbitrary", vmem=VMEM_BIG))(u, a_re, a_im, bb_re, bb_im, cm_re, cm_im, d_skip)


def _ssm_bwd_call(dy, u, s_re_all, s_im_all, a_re, a_im, bb_re, bb_im, cm_re, cm_im, d_skip):
    L = u.shape[0]
    seg = L // 8
    rb = _fit(L, SSM_ROWS)

    def body(dy_ref, u_ref, sre_hbm, sim_hbm, ar_ref, ai_ref, bbr_ref, bbi_ref, cmr_ref, cmi_ref, d_ref,
             du_ref, dbbr_ref, dbbi_ref, dcmr_ref, dcmi_ref, dar_ref, dai_ref, dd_ref,
             g_re, g_im, s_re, s_im, sems):
        q = pl.program_id(0)
        in_r = pltpu.make_async_copy(sre_hbm.at[q], s_re, sems.at[0])
        in_i = pltpu.make_async_copy(sim_hbm.at[q], s_im, sems.at[1])
        in_r.start()
        in_i.start()

        def ds_step(r, c):
            rows = pl.ds(pl.multiple_of(r * rb, rb), rb)
            dyb = dy_ref[rows, :].astype(BF16)
            g_re[rows, :] = _dot(dyb, cmr_ref[0], NT)
            g_im[rows, :] = -_dot(dyb, cmi_ref[0], NT)
            return c

        lax.fori_loop(0, L // rb, ds_step, 0)
        ar1, ai1 = ar_ref[...], ai_ref[...]
        ar = jnp.broadcast_to(ar1, (8, SSM_CW))
        nai = jnp.broadcast_to(-ai1, (8, SSM_CW))

        def local(kk, c):
            k = seg - 1 - kk
            nr, ni = _cmul(ar, nai, c[0], c[1])
            nr = nr + g_re[_tile(k), :]
            ni = ni + g_im[_tile(k), :]
            g_re[_tile(k), :] = nr
            g_im[_tile(k), :] = ni
            return nr, ni

        zero8 = jnp.zeros((8, SSM_CW), F32)
        lax.fori_loop(0, seg, local, (zero8, zero8))
        pr, pi = _power(ar1, -ai1, seg)
        head_r = g_re[pl.ds(0, 8), :]
        head_i = g_im[pl.ds(0, 8), :]
        fr = jnp.zeros((1, SSM_CW), F32)
        fi = jnp.zeros((1, SSM_CW), F32)
        rows_r, rows_i = [fr], [fi]
        for j in range(6, -1, -1):
            tr, ti = _cmul(pr, pi, fr, fi)
            fr, fi = head_r[j + 1:j + 2] + tr, head_i[j + 1:j + 2] + ti
            rows_r.insert(0, fr)
            rows_i.insert(0, fi)
        f_r = jnp.concatenate(rows_r, axis=0)
        f_i = jnp.concatenate(rows_i, axis=0)
        in_r.wait()
        in_i.wait()

        def fixed(k, wr, wi):
            xr, xi = _cmul(wr, wi, f_r, f_i)
            gr = g_re[_tile(k), :] + xr
            gi = g_im[_tile(k), :] + xi
            g_re[_tile(k), :] = gr
            g_im[_tile(k), :] = gi
            return gr, gi

        def fix(kk, c):
            k = seg - 1 - kk
            wr, wi = _cmul(c[0], c[1], ar, nai)
            gr, gi = fixed(k, wr, wi)
            pr_, pi_ = s_re[_tile(k - 1), :], s_im[_tile(k - 1), :]
            return wr, wi, c[2] + gr * pr_ + gi * pi_, c[3] + gi * pr_ - gr * pi_

        wr, wi, acc_r, acc_i = lax.fori_loop(0, seg - 1, fix, (jnp.ones((8, SSM_CW), F32), zero8, zero8, zero8))
        wr, wi = _cmul(wr, wi, ar, nai)
        gr, gi = fixed(0, wr, wi)
        row8 = lax.broadcasted_iota(jnp.int32, (8, SSM_CW), 0)
        pr_ = jnp.where(row8 > 0, pltpu.roll(s_re[pl.ds((seg - 1) * 8, 8), :], 1, 0), 0.0)
        pi_ = jnp.where(row8 > 0, pltpu.roll(s_im[pl.ds((seg - 1) * 8, 8), :], 1, 0), 0.0)
        acc_r = acc_r + gr * pr_ + gi * pi_
        acc_i = acc_i + gi * pr_ - gr * pi_
        dar_ref[...] = jnp.sum(acc_r, axis=0, keepdims=True)
        dai_ref[...] = jnp.sum(acc_i, axis=0, keepdims=True)

        dbbr_ref[...] = jnp.zeros((1, SSM_CU, SSM_CW), F32)
        dbbi_ref[...] = jnp.zeros((1, SSM_CU, SSM_CW), F32)
        dcmr_ref[...] = jnp.zeros((1, SSM_CW, SSM_CU), F32)
        dcmi_ref[...] = jnp.zeros((1, SSM_CW, SSM_CU), F32)
        dd_ref[...] = jnp.zeros((1, SSM_CU), F32)

        def grad_step(r, c):
            rows = pl.ds(pl.multiple_of(r * rb, rb), rb)
            ub, dyv = u_ref[rows, :], dy_ref[rows, :]
            ubb, dyb = ub.astype(BF16), dyv.astype(BF16)
            grb, gib = g_re[rows, :].astype(BF16), g_im[rows, :].astype(BF16)
            dbbr_ref[0] += _dot(ubb, grb, TN)
            dbbi_ref[0] += _dot(ubb, gib, TN)
            dcmr_ref[0] += _dot(s_re[rows, :].astype(BF16), dyb, TN)
            dcmi_ref[0] -= _dot(s_im[rows, :].astype(BF16), dyb, TN)
            du_ref[rows, :] = _dot(grb, bbr_ref[0], NT) + _dot(gib, bbi_ref[0], NT) + d_ref[...] * dyv
            dd_ref[...] += jnp.sum(dyv * ub, axis=0, keepdims=True)
            return c

        lax.fori_loop(0, L // rb, grad_step, 0)

    chunk = lambda rows, cols: pl.BlockSpec((rows, cols), lambda q: (0, q))
    mat = lambda r, c: pl.BlockSpec((1, r, c), lambda q: (q, 0, 0))
    anyspec = pl.BlockSpec(memory_space=pl.ANY)
    big = lambda: pltpu.VMEM((L, SSM_CW), F32)
    return pl.pallas_call(
        body, name="ssm_bwd", grid=(SSM_CHUNKS,),
        in_specs=[chunk(L, SSM_CU), chunk(L, SSM_CU), anyspec, anyspec, chunk(1, SSM_CW), chunk(1, SSM_CW),
                  mat(SSM_CU, SSM_CW), mat(SSM_CU, SSM_CW), mat(SSM_CW, SSM_CU), mat(SSM_CW, SSM_CU), chunk(1, SSM_CU)],
        out_specs=[chunk(L, SSM_CU), mat(SSM_CU, SSM_CW), mat(SSM_CU, SSM_CW), mat(SSM_CW, SSM_CU), mat(SSM_CW, SSM_CU),
                   chunk(1, SSM_CW), chunk(1, SSM_CW), chunk(1, SSM_CU)],
        out_shape=[jax.ShapeDtypeStruct((L, SSM_WIDTH), F32),
                   jax.ShapeDtypeStruct((SSM_CHUNKS, SSM_CU, SSM_CW), F32), jax.ShapeDtypeStruct((SSM_CHUNKS, SSM_CU, SSM_CW), F32),
                   jax.ShapeDtypeStruct((SSM_CHUNKS, SSM_CW, SSM_CU), F32), jax.ShapeDtypeStruct((SSM_CHUNKS, SSM_CW, SSM_CU), F32),
                   jax.ShapeDtypeStruct((1, SSM_NSTATE), F32), jax.ShapeDtypeStruct((1, SSM_NSTATE), F32),
                   jax.ShapeDtypeStruct((1, SSM_WIDTH), F32)],
        scratch_shapes=[big(), big(), big(), big(), pltpu.SemaphoreType.DMA((2,))],
        compiler_params=_params("arbitrary", vmem=VMEM_BIG))(
            dy, u, s_re_all, s_im_all, a_re, a_im, bb_re, bb_im, cm_re, cm_im, d_skip)


def _place():
    return lax.axis_index("x"), lax.axis_index("y"), lax.axis_index("c")


def _all_gather_call(blocks, name, direct=False):
    n = len(blocks)

    def body(*refs):
        if direct:
            start, finish = _exchange_phases(refs[:n], refs[n:2 * n], *refs[2 * n:], same_source=True)
            start()
        else:
            start, forward, finish = _gather_phases(refs[:n], refs[n:2 * n], *refs[2 * n:])
            start()
            forward()
        finish()

    return pl.pallas_call(
        body, name=name, in_specs=[ANY_SPEC] * n, out_specs=[ANY_SPEC] * n,
        out_shape=[jax.ShapeDtypeStruct((N_DEV,) + b.shape, b.dtype) for b in blocks],
        scratch_shapes=_comm_sems(n))(*blocks)


def _comm_sems(n):
    return [pltpu.SemaphoreType.DMA((7 * n,)), pltpu.SemaphoreType.DMA((7 * n,)), pltpu.SemaphoreType.DMA((n,))]


def _gather_phases(x_refs, out_refs, send_sems, recv_sems, local_sems):
    x, y, c = _place()
    me, sibling = (x, y, c), (x, y, 1 - c)
    chips = [(1 - x, y), (x, 1 - y), (1 - x, 1 - y)]
    n = len(x_refs)

    def copy(k, a, blk, to, from_input=False):
        slot = out_refs[a].at[4 * blk[0] + 2 * blk[1] + blk[2]]
        return pltpu.make_async_remote_copy(
            src_ref=x_refs[a] if from_input else slot, dst_ref=slot,
            send_sem=send_sems.at[k * n + a], recv_sem=recv_sems.at[k * n + a], device_id=to, device_id_type=MESH_ID)

    mine = [pltpu.make_async_copy(x_refs[a], out_refs[a].at[4 * x + 2 * y + c], local_sems.at[a]) for a in range(n)]
    first, passed = [], []
    for a in range(n):
        first.append(copy(0, a, me, sibling, True))
        first += [copy(1 + j, a, me, (*chip, c), True) for j, chip in enumerate(chips)]
        passed += [copy(4 + j, a, (*chip, c), sibling) for j, chip in enumerate(chips)]

    def start():
        for cp in mine + first:
            cp.start()

    def forward():
        for j, chip in enumerate(chips):
            for a in range(n):
                copy(1 + j, a, (*chip, c), me).wait_recv()
                passed[3 * a + j].start()

    def finish():
        for a in range(n):
            copy(0, a, sibling, me).wait_recv()
            for j, chip in enumerate(chips):
                copy(4 + j, a, (*chip, 1 - c), me).wait_recv()
        for cp in first + passed:
            cp.wait_send()
        for cp in mine:
            cp.wait()

    return start, forward, finish


def _exchange_phases(p_refs, out_refs, send_sems, recv_sems, local_sems, same_source=False):
    x, y, c = _place()
    me = 4 * x + 2 * y + c
    n = len(p_refs)

    def flip(k):
        px = 1 - x if k & 4 else x
        py = 1 - y if k & 2 else y
        pc = 1 - c if k & 1 else c
        return (px, py, pc), 4 * px + 2 * py + pc

    def source(a, slot):
        return p_refs[a] if same_source else p_refs[a].at[slot]

    def copy(k, a, landing):
        peer, peer_slot = flip(k)
        return pltpu.make_async_remote_copy(
            src_ref=source(a, peer_slot), dst_ref=out_refs[a].at[peer_slot if landing else me],
            send_sem=send_sems.at[(k - 1) * n + a], recv_sem=recv_sems.at[(k - 1) * n + a],
            device_id=peer, device_id_type=MESH_ID)

    mine = [pltpu.make_async_copy(source(a, me), out_refs[a].at[me], local_sems.at[a]) for a in range(n)]
    sends = [copy(k, a, False) for k in range(1, N_DEV) for a in range(n)]

    def start():
        for cp in mine + sends:
            cp.start()

    def finish():
        for k in range(1, N_DEV):
            for a in range(n):
                copy(k, a, True).wait_recv()
        for cp in sends:
            cp.wait_send()
        for cp in mine:
            cp.wait()

    return start, finish


def _adam_math(g, w, m, v):
    c1 = 1.0 / (1.0 - ADAM_B1 ** ADAM_STEP)
    c2 = 1.0 / (1.0 - ADAM_B2 ** ADAM_STEP)
    m_new = ADAM_B1 * m + (1.0 - ADAM_B1) * g
    v_new = ADAM_B2 * v + (1.0 - ADAM_B2) * (g * g)
    delta = -ADAM_LR * ((m_new * c1) / (jnp.sqrt(v_new * c2) + ADAM_EPS) + ADAM_WD * w)
    return g, delta, m_new, v_new


def _sum_slices(s_ref):
    g = s_ref[0].astype(F32)
    for k in range(1, N_DEV):
        g = g + s_ref[k].astype(F32)
    return g


def _adam_call(slices, w, m, v, name):
    d1, rest = w.shape[1], w.shape[2:]
    zeros = (0,) * len(rest)
    by_lanes = len(rest) == 1 and d1 > 256 and d1 % 16 != 0
    if by_lanes:
        tile = _fit(rest[0], 256)
        steps = rest[0] // tile
        own = pl.BlockSpec((1, d1, tile), lambda i: (0, 0, i))
        sl = pl.BlockSpec((N_DEV, 1, d1, tile), lambda i: (0, 0, 0, i))
    else:
        tile = _fit(d1, 256, 16) if len(rest) == 1 else _fit(d1, 8, 8)
        steps = d1 // tile
        own = pl.BlockSpec((1, tile) + rest, lambda i: (0, i) + zeros)
        sl = pl.BlockSpec((N_DEV, 1, tile) + rest, lambda i: (0, 0, i) + zeros)

    def body(s_ref, w_ref, m_ref, v_ref, g_ref, d_ref, mo_ref, vo_ref):
        g_ref[...], d_ref[...], mo_ref[...], vo_ref[...] = _adam_math(_sum_slices(s_ref), w_ref[...], m_ref[...], v_ref[...])

    out = jax.ShapeDtypeStruct(w.shape, F32)
    return pl.pallas_call(
        body, name=name, grid=(steps,), in_specs=[sl, own, own, own],
        out_specs=[own, own, own, own], out_shape=[out, out, out, out],
        compiler_params=_params("parallel"))(slices, w, m, v)


def _adam_small_call(rows_all, row_params, slices, params):
    nr, n = len(row_params), len(row_params) + len(params)

    def row_sum(rows_ref, a, width):
        g = rows_ref[0, pl.ds(a, 1), pl.ds(0, width)]
        for k in range(1, N_DEV):
            g = g + rows_ref[k, pl.ds(a, 1), pl.ds(0, width)]
        return g

    def body(rows_ref, *refs):
        slice_refs, wmv, outs = refs[:n - nr], refs[n - nr:n - nr + 3 * n], refs[n - nr + 3 * n:]
        outs[4 * n][...] = row_sum(rows_ref, nr, LANES)
        for a in range(n):
            w_ref, m_ref, v_ref = wmv[3 * a:3 * a + 3]
            if a < nr:
                g = row_sum(rows_ref, a, w_ref.shape[1])
            else:
                g = _sum_slices(slice_refs[a - nr])
            res = _adam_math(g, w_ref[...], m_ref[...], v_ref[...])
            for r in range(4):
                outs[4 * a + r][...] = res[r]

    every = list(row_params) + list(params)
    flat = pl.pallas_call(
        body, name="adam_small",
        out_shape=[jax.ShapeDtypeStruct(w.shape, F32) for w, _, _ in every for _ in range(4)]
        + [jax.ShapeDtypeStruct((1, LANES), F32)],
    )(rows_all, *slices, *[t for wmv in every for t in wmv])
    return [flat[4 * a:4 * a + 4] for a in range(n)], flat[4 * n][0, 0]


BIG = (("w_in", 1024, 404, 1), ("w_uq", 384, 96, 1), ("w_uk", 256, 64, 1), ("w_uv", 256, 64, 1),
       ("w_glu", 64, 512, 0), ("w_branch_attn", 512, 128, 1), ("w_branch_ssm", 512, 128, 1),
       ("w_out", 128, 1024, 0), ("w_up", 1024, 704, 1), ("w_down", 352, 1024, 0), ("conv_w", 3, 704, 1))
BIG_MIX, BIG_FFN = BIG[:8], BIG[8:]
GRADS_EARLY, GRADS_LATE = BIG[8:] + BIG[4:8], BIG[:4]
SMALL = (("mix_norm_pre", (1024,)), ("q_norm", (384,)), ("kv_norm", (256,)), ("ssm_lambda_re", (32, 64)),
         ("ssm_lambda_im", (32, 64)), ("ssm_log_dt", (32,)), ("ssm_b_re", (32, 64, 16)), ("ssm_b_im", (32, 64, 16)),
         ("ssm_c_re", (32, 16, 64)), ("ssm_c_im", (32, 16, 64)), ("ssm_d", (32, 16)), ("b_glu", (512,)),
         ("b_gate", (2048,)), ("mix_norm_post", (1024,)), ("ffn_norm_pre", (1024,)), ("conv_b", (5632,)),
         ("ffn_norm_post", (1024,)))


TRANSPOSED = ("w_in", "w_uq", "w_uk", "w_uv", "w_up")


def _stored(name, arr):
    return jnp.swapaxes(arr, 1, 2) if name in TRANSPOSED else arr


def _to_slices(name, full, rows, cols, axis):
    if name in TRANSPOSED:
        return full.T.reshape(N_DEV, cols, rows)
    if axis == 1:
        return full.reshape(rows, N_DEV, cols).transpose(1, 0, 2)
    return full.reshape(N_DEV, rows, cols)


def _from_slices(name, parts, rows, cols, axis):
    if name in TRANSPOSED:
        return parts.reshape(N_DEV * cols, rows)
    if axis == 1:
        return parts.transpose(1, 0, 2).reshape(rows, N_DEV * cols)
    return parts.reshape(N_DEV * rows, cols)


def _head_unpad_cols(w, width):
    k = w.shape[0]
    return w.reshape(k, N_HEADS, LANES)[:, :, :width].reshape(k, N_HEADS * width)


def _time_perm(a, L):
    return a.reshape(8, L // 8, a.shape[-1]).transpose(1, 0, 2).reshape(L, a.shape[-1])


def _time_unperm(a, L):
    return a.reshape(L // 8, 8, a.shape[-1]).transpose(1, 0, 2).reshape(L, a.shape[-1])


def _block_diag(w, rows_first):
    eye = jnp.eye(8, dtype=w.dtype)
    g = w.reshape(SSM_CHUNKS, 8, w.shape[1], w.shape[2])
    return jnp.einsum("qgrc,gk->qgrkc", g, eye).reshape(SSM_CHUNKS, 8 * w.shape[1], 8 * w.shape[2])


def _block_diag_t(m, r, c):
    eye = jnp.eye(8, dtype=m.dtype)
    return jnp.einsum("qgrkc,gk->qgrc", m.reshape(SSM_CHUNKS, 8, r, 8, c), eye).reshape(SSM_GROUPS, r, c)


def kernel(x, positions, mix_norm_pre, w_in, q_norm, w_uq, kv_norm, w_uk, w_uv, ssm_lambda_re, ssm_lambda_im, ssm_log_dt, ssm_b_re, ssm_b_im, ssm_c_re, ssm_c_im, ssm_d, w_glu, b_glu, w_branch_attn, w_branch_ssm, b_gate, w_out, mix_norm_post, ffn_norm_pre, w_up, conv_w, conv_b, w_down, ffn_norm_post, loss_target, m_mix_norm_pre, m_w_in, m_q_norm, m_w_uq, m_kv_norm, m_w_uk, m_w_uv, m_ssm_lambda_re, m_ssm_lambda_im, m_ssm_log_dt, m_ssm_b_re, m_ssm_b_im, m_ssm_c_re, m_ssm_c_im, m_ssm_d, m_w_glu, m_b_glu, m_w_branch_attn, m_w_branch_ssm, m_b_gate, m_w_out, m_mix_norm_post, m_ffn_norm_pre, m_w_up, m_conv_w, m_conv_b, m_w_down, m_ffn_norm_post, v_mix_norm_pre, v_w_in, v_q_norm, v_w_uq, v_kv_norm, v_w_uk, v_w_uv, v_ssm_lambda_re, v_ssm_lambda_im, v_ssm_log_dt, v_ssm_b_re, v_ssm_b_im, v_ssm_c_re, v_ssm_c_im, v_ssm_d, v_w_glu, v_b_glu, v_w_branch_attn, v_w_branch_ssm, v_b_gate, v_w_out, v_mix_norm_post, v_ffn_norm_pre, v_w_up, v_conv_w, v_conv_b, v_w_down, v_ffn_norm_post):
    given = dict(locals())
    L = x.shape[1]
    xs = x[0]
    target = loss_target[0]

    def shard_bits(group):
        return [given[name][0] if name == "conv_w" else _stored(name, given[name])[0].astype(BF16) for name, _, _, _ in group]

    W = {}

    def unpack_weights(gathered, group):
        for (name, rows, cols, axis), parts in zip(group, gathered):
            W[name] = _from_slices(name, parts, rows, cols, axis)

    unpack_weights(_all_gather_call(shard_bits(BIG_MIX[:1]), "gather_w_in"), BIG_MIX[:1])

    wit = W["w_in"]
    zero_rows = lambda r: jnp.zeros((r, D_MODEL), BF16)
    kr_end = P_KR + QK_ROPE
    w_in_pt = jnp.concatenate(
        [wit[:P_KR], zero_rows(QK_NOPE), wit[P_KR:kr_end], zero_rows(LANES - QK_HEAD), wit[kr_end:]], axis=0)

    hn1 = _rms_fwd_call(xs, mix_norm_pre, "rms_pre")
    proj, *gathered_mix = _mm(hn1, w_in_pt, "mm_in", tb=True, tn=1664, gather=shard_bits(BIG_MIX[1:]))
    unpack_weights(gathered_mix, BIG_MIX[1:])
    head_rows = lambda wt, width: jnp.pad(wt.reshape(N_HEADS, width, wt.shape[1]), ((0, 0), (0, LANES - width), (0, 0)))
    w_uq_pt = head_rows(W["w_uq"], QK_HEAD).reshape(HEAD_PAD, Q_RANK)
    w_kv_pt = jnp.stack([head_rows(W["w_uk"], QK_NOPE), head_rows(W["w_uv"], V_HEAD)], axis=1
                        ).reshape(2 * HEAD_PAD, KV_RANK)
    w_ba_p = jnp.pad(W["w_branch_attn"].reshape(N_HEADS, V_HEAD, D_MODEL), ((0, 0), (0, LANES - V_HEAD), (0, 0))
                     ).reshape(HEAD_PAD, D_MODEL)
    half = jnp.arange(QK_ROPE // 2, dtype=F32)
    inv_freq = ROPE_THETA ** (-2.0 * half / QK_ROPE)
    inv_freq = jnp.pad(jnp.concatenate([inv_freq, inv_freq]), (QK_NOPE, LANES - QK_HEAD)).reshape(1, LANES)
    pos_col = positions.astype(F32).reshape(L, 1)
    qn, ckvn, q_r, kv_r, cosf, sinf = _mla_proj_call(proj, q_norm, kv_norm, w_uq_pt, w_kv_pt, pos_col, inv_freq)
    attn, lse, *gathered_ffn = _attn_fwd_call(q_r, kv_r, shard_bits(BIG_FFN))
    unpack_weights(gathered_ffn, BIG_FFN)

    col = lambda a: a.reshape(SSM_NSTATE, -1)
    lr_c, li_c = col(ssm_lambda_re[0]), col(ssm_lambda_im[0])
    ldt_c = col(jnp.broadcast_to(ssm_log_dt[0][:, None], (SSM_GROUPS, SSM_STATE)))
    br_c, bi_c = col(ssm_b_re[0]), col(ssm_b_im[0])
    a_re_c, a_im_c, bb_re_c, bb_im_c = _disc_call(lr_c, li_c, ldt_c, br_c, bi_c)
    a_re, a_im = a_re_c.reshape(1, SSM_NSTATE), a_im_c.reshape(1, SSM_NSTATE)
    to_bb = lambda b: _block_diag(b.reshape(SSM_GROUPS, SSM_STATE, SSM_GROUP).transpose(0, 2, 1), True).astype(BF16)
    bb_re, bb_im = to_bb(bb_re_c), to_bb(bb_im_c)
    to_cm = lambda c_: _block_diag(c_[0].transpose(0, 2, 1), True).astype(BF16)
    cm_re, cm_im = to_cm(ssm_c_re), to_cm(ssm_c_im)
    d_skip = ssm_d.reshape(1, SSM_WIDTH)
    u_p = _time_perm(proj[:, P_U:P_GATE], L)
    y1, s_re, s_im = _ssm_fwd_call(u_p, a_re, a_im, bb_re, bb_im, cm_re, cm_im, d_skip)
    w_glu_b = W["w_glu"]
    ssm_p = _glu_call(y1, w_glu_b, b_glu)
    ssm = _time_unperm(ssm_p, L)

    pa = _mm(attn, w_ba_p, "mm_ba")
    ps = _mm(ssm, W["w_branch_ssm"], "mm_bs")
    merged = _merge_call(proj, b_gate, pa, ps)
    wide = lambda dt: (D_MODEL, dt)
    o, x2, hn2 = _mm_rows(merged, W["w_out"], "mm_out", _post_mix_rows, [xs], [mix_norm_post, ffn_norm_pre],
                          [wide(F32), wide(F32), wide(BF16)], [])
    h = _mm(hn2, W["w_up"], "mm_up", tb=True, tn=1408)
    cw = W["conv_w"]
    act = _conv_act_call(h, cw, conv_b)
    dy, dff, loss_row, g_ffn_norm_post = _mm_rows(
        act, W["w_down"], "mm_down", _ffn_out_rows, [x2, target], [ffn_norm_post], [wide(F32), wide(BF16)],
        [LANES, D_MODEL], tk=1408)

    da = _mm(dff, W["w_down"], "mm_down_dx", tb=True, tn=1408)
    g_w_down = _mm_tn(act, dff, "mm_down_dw", tm=1408)
    dgate, dval, dcw_g, dcw_v, dcb_g, dcb_v = _conv_act_bwd_call(da, h, cw, conv_b)
    g_conv_w = jnp.concatenate([dcw_g, dcw_v], axis=1)
    g_conv_b = jnp.concatenate([dcb_g, dcb_v], axis=1)
    dh = _conv_t_call(dgate, dval, cw)
    dx2, do, g_ffn_norm_pre, g_mix_norm_post = _mm_rows(
        dh, W["w_up"], "mm_up_dx", _post_bwd_rows, [x2, dy, o], [ffn_norm_pre, mix_norm_post], [wide(F32), wide(BF16)],
        [D_MODEL, D_MODEL], tk=1408)
    g_w_up = _mm_tn(hn2, dh, "mm_up_dw")
    dmerged = _mm(do, W["w_out"], "mm_out_dx", tb=True)
    g_w_out = _mm_tn(merged, do, "mm_out_dw")
    dpa, dps, dl0, dl1, db0, db1 = _merge_bwd_call(dmerged, proj, b_gate, pa, ps)
    g_b_gate = jnp.concatenate([db0, db1], axis=1)
    dattn = _mm(dpa, w_ba_p, "mm_ba_dx", tb=True, out_dtype=BF16)
    g_w_ba = _mm_tn(attn, dpa, "mm_ba_dw").reshape(N_HEADS, LANES, D_MODEL)[:, :V_HEAD].reshape(N_HEADS * V_HEAD, D_MODEL)
    dssm = _mm(dps, W["w_branch_ssm"], "mm_bs_dx", tb=True)
    g_w_bs = _mm_tn(ssm, dps, "mm_bs_dw")

    dy1, g_w_glu, g_b_glu = _glu_bwd_call(_time_perm(dssm, L), y1, w_glu_b, b_glu)
    du_p, dbb_re, dbb_im, dcm_re, dcm_im, da_re, da_im, g_ssm_d = _ssm_bwd_call(
        dy1, u_p, s_re, s_im, a_re, a_im, bb_re, bb_im, cm_re, cm_im, d_skip)
    du = _time_unperm(du_p, L)
    from_bb = lambda m: col(_block_diag_t(m, SSM_GROUP, SSM_STATE).transpose(0, 2, 1))
    dlr, dli, dldt, dbr, dbi = _disc_bwd_call(
        lr_c, li_c, ldt_c, br_c, bi_c, da_re.reshape(SSM_NSTATE, 1), da_im.reshape(SSM_NSTATE, 1), from_bb(dbb_re), from_bb(dbb_im))
    g_c_re = _block_diag_t(dcm_re, SSM_STATE, SSM_GROUP).transpose(0, 2, 1)
    g_c_im = _block_diag_t(dcm_im, SSM_STATE, SSM_GROUP).transpose(0, 2, 1)

    def grad_slices(group, grads):
        return [_to_slices(name, grads[name], rows, cols, axis) for name, rows, cols, axis in group]

    early_grads = {"w_up": g_w_up, "w_down": g_w_down, "conv_w": g_conv_w, "w_glu": g_w_glu.astype(BF16),
                   "w_branch_attn": g_w_ba, "w_branch_ssm": g_w_bs, "w_out": g_w_out}
    ssm_partials = {"ssm_lambda_re": dlr, "ssm_lambda_im": dli, "ssm_b_re": dbr, "ssm_b_im": dbi,
                    "ssm_c_re": g_c_re, "ssm_c_im": g_c_im, "ssm_d": g_ssm_d}
    ssm_shapes = [(name, shp) for name, shp in SMALL if name in ssm_partials]
    dq, dkv, *landed = _attn_bwd_call(
        q_r, kv_r, attn, dattn, lse, grad_slices(GRADS_EARLY, early_grads),
        [ssm_partials[name].reshape(-1, LANES) if len(shp) == 3 else ssm_partials[name].reshape((1,) + shp)
         for name, shp in ssm_shapes])
    received_early = landed[:len(GRADS_EARLY)]
    ssm_all = {name: got.reshape((N_DEV, 1) + shp) for (name, shp), got in zip(ssm_shapes, landed[len(GRADS_EARLY):])}
    dq_p, dkv_p, dlatent, g_q_norm, g_kv_norm = _mla_proj_bwd_call(
        dq, dkv, cosf, sinf, proj, q_norm, kv_norm, w_uq_pt, w_kv_pt)
    g_w_uq = _head_unpad_cols(_mm_tn(qn, dq_p, "mm_uq_dw"), QK_HEAD)
    g_w_kv = _mm_tn(ckvn, dkv_p, "mm_ukv_dw").reshape(KV_RANK, N_HEADS, 2, LANES)
    g_w_uk = g_w_kv[:, :, 0, :QK_NOPE].reshape(KV_RANK, N_HEADS * QK_NOPE)
    g_w_uv = g_w_kv[:, :, 1, :V_HEAD].reshape(KV_RANK, N_HEADS * V_HEAD)
    dproj = jnp.concatenate([dlatent, du.astype(BF16), dl0, dl1], axis=1)
    g_w_in_p = _mm_tn(hn1, dproj, "mm_in_dw", tk=1024)
    g_w_in = jnp.concatenate([g_w_in_p[:, :P_KR], g_w_in_p[:, P_KR + QK_NOPE:P_KR + QK_HEAD], g_w_in_p[:, P_U:]], axis=1)
    late_grads = {"w_in": g_w_in, "w_uq": g_w_uq, "w_uk": g_w_uk, "w_uv": g_w_uv}
    grad_x, g_mix_norm_pre, *received_late = _mm_in_dx_call(
        dproj, w_in_pt, xs, dx2, mix_norm_pre, grad_slices(GRADS_LATE, late_grads))

    results = {}
    for group, received in ((GRADS_EARLY, received_early), (GRADS_LATE, received_late)):
        for (name, _, _, _), rec in zip(group, received):
            stored = [_stored(name, given[prefix + name]) for prefix in ("", "m_", "v_")]
            results[name] = [_stored(name, r) for r in _adam_call(rec[:, None], *stored, "adam_" + name)]

    vec_grads = {"mix_norm_pre": g_mix_norm_pre, "q_norm": g_q_norm, "kv_norm": g_kv_norm,
                 "ssm_log_dt": jnp.sum(dldt.reshape(SSM_GROUPS, SSM_STATE), axis=1),
                 "b_glu": g_b_glu, "b_gate": g_b_gate, "mix_norm_post": g_mix_norm_post,
                 "ffn_norm_pre": g_ffn_norm_pre, "conv_b": g_conv_b, "ffn_norm_post": g_ffn_norm_post}
    vec_names = [name for name, _ in SMALL if name in vec_grads]
    width = max(shp[0] for name, shp in SMALL if name in vec_grads)
    rows = [jnp.pad(vec_grads[name].reshape(1, -1), ((0, 0), (0, width - vec_grads[name].size))) for name in vec_names]
    rows.append(jnp.pad(loss_row, ((0, 0), (0, width - LANES))))
    rows.append(jnp.zeros((-len(rows) % 8, width), F32))
    rows_all, = _all_gather_call([jnp.concatenate(rows, axis=0)], "gather_small_grads", direct=True)
    wmv = lambda name: (given[name], given["m_" + name], given["v_" + name])
    few = ["ssm_lambda_re", "ssm_lambda_im", "ssm_d"]
    small_results, loss = _adam_small_call(
        rows_all, [wmv(n) for n in vec_names], [ssm_all[n] for n in few], [wmv(n) for n in few])
    results.update(zip(vec_names + few, small_results))
    for name in ("ssm_b_re", "ssm_b_im", "ssm_c_re", "ssm_c_im"):
        results[name] = _adam_call(ssm_all[name], *wmv(name), "adam_" + name)

    order = ["mix_norm_pre", "w_in", "q_norm", "w_uq", "kv_norm", "w_uk", "w_uv", "ssm_lambda_re", "ssm_lambda_im",
             "ssm_log_dt", "ssm_b_re", "ssm_b_im", "ssm_c_re", "ssm_c_im", "ssm_d", "w_glu", "b_glu", "w_branch_attn",
             "w_branch_ssm", "b_gate", "w_out", "mix_norm_post", "ffn_norm_pre", "w_up", "conv_w", "conv_b", "w_down",
             "ffn_norm_post"]
    outs = [loss, grad_x[None]]
    for kind in range(4):
        outs += [results[name][kind] for name in order]
    return tuple(outs)
```

```python
import math

import jax
import jax.numpy as jnp
from jax import lax
from jax.experimental import pallas as pl
from jax.experimental.pallas import tpu as pltpu

F32 = jnp.float32
BF16 = jnp.bfloat16
MESH_ID = pl.DeviceIdType.MESH

N_DEV = 8
LANES = 128
D_MODEL = 1024
N_HEADS = 8
QK_NOPE = 64
QK_ROPE = 32
QK_HEAD = QK_NOPE + QK_ROPE
V_HEAD = 64
Q_RANK = 384
KV_RANK = 256
ROPE_THETA = 10000.0
SSM_WIDTH = 512
SSM_GROUP = 16
SSM_GROUPS = 32
SSM_STATE = 64
SSM_NSTATE = SSM_GROUPS * SSM_STATE
SSM_CHUNKS = 4
D_FF = 2816
EPS = 1e-6
ADAM_LR, ADAM_B1, ADAM_B2, ADAM_EPS, ADAM_WD, ADAM_STEP = 0.001, 0.9, 0.999, 1e-08, 0.01, 10

P_CQ, P_CKV, P_KR, P_U, P_GATE = 0, 384, 640, 768, 1280
HEAD_PAD = N_HEADS * LANES

VMEM_BIG = 52 * 1024 * 1024

_GELU_C0 = math.sqrt(2.0 / math.pi)
_GELU_C1 = 0.044715
NEG = -1e30


def _fit(n, pref, mult=LANES):
    if n <= pref:
        return n
    t = (pref // mult) * mult
    while t > 0 and n % t:
        t -= mult
    assert t > 0, (n, pref, mult)
    return t


def _gelu(x):
    return x * (0.5 * (1.0 + jnp.tanh(_GELU_C0 * x * (1.0 + _GELU_C1 * (x * x)))))


def _gelu_and_grad(x):
    x2 = x * x
    t = jnp.tanh(_GELU_C0 * x * (1.0 + _GELU_C1 * x2))
    half = 0.5 * (1.0 + t)
    return x * half, half + 0.5 * x * (1.0 - t * t) * _GELU_C0 * (1.0 + 3.0 * _GELU_C1 * x2)


def _sigmoid(x):
    return 1.0 / (1.0 + jnp.exp(-x))


def _dot(a, b, dims):
    return lax.dot_general(a, b, (dims, ((), ())), preferred_element_type=F32)


NN = ((1,), (0,))
NT = ((1,), (1,))
TN = ((0,), (0,))


def _params(*sem, vmem=None):
    return pltpu.CompilerParams(dimension_semantics=tuple(sem), vmem_limit_bytes=vmem)


def _mm(a, b, name, tb=False, out_dtype=F32, tm=1024, tn=1024, tk=1024, gather=()):
    M, K = a.shape
    if tb:
        N, K2 = b.shape
    else:
        K2, N = b.shape
    assert K == K2, (a.shape, b.shape, tb)
    tm, tn, tk = _fit(M, tm), _fit(N, tn), _fit(K, tk)
    nk = K // tk
    grid = (M // tm, N // tn, nk)
    steps = grid[0] * grid[1] * grid[2]
    dims = NT if tb else NN
    n = len(gather)

    def body(a_ref, b_ref, *refs):
        o_ref, scratch = refs[n], refs[2 * n + 1:]
        step = (pl.program_id(0) * grid[1] + pl.program_id(1)) * grid[2] + pl.program_id(2)
        if n:
            start, forward, finish = _gather_phases(refs[:n], refs[n + 1:2 * n + 1], *scratch[-3:])
            pl.when(step == 0)(start)
            pl.when(step == steps // 2)(forward)
        part = _dot(a_ref[...].astype(BF16), b_ref[...].astype(BF16), dims)
        if nk == 1:
            o_ref[...] = part.astype(out_dtype)
        else:
            acc_ref = scratch[0]
            k = pl.program_id(2)

            @pl.when(k == 0)
            def _():
                acc_ref[...] = part

            @pl.when(k > 0)
            def _():
                acc_ref[...] += part

            @pl.when(k == nk - 1)
            def _():
                o_ref[...] = acc_ref[...].astype(out_dtype)
        if n:
            pl.when(step == steps - 1)(finish)

    a_spec = pl.BlockSpec((tm, tk), lambda i, j, k: (i, k))
    b_spec = pl.BlockSpec((tn, tk), lambda i, j, k: (j, k)) if tb else pl.BlockSpec((tk, tn), lambda i, j, k: (k, j))
    landed = [jax.ShapeDtypeStruct((N_DEV,) + p.shape, p.dtype) for p in gather]
    out = pl.pallas_call(
        body, name=name, grid=grid,
        in_specs=[a_spec, b_spec] + [ANY_SPEC] * n,
        out_specs=[pl.BlockSpec((tm, tn), lambda i, j, k: (i, j))] + [ANY_SPEC] * n,
        out_shape=[jax.ShapeDtypeStruct((M, N), out_dtype)] + landed,
        scratch_shapes=([] if nk == 1 else [pltpu.VMEM((tm, tn), F32)]) + (_comm_sems(n) if n else []),
        compiler_params=_params(*(("arbitrary",) * 3 if n else ("parallel", "parallel", "arbitrary")), vmem=VMEM_BIG),
    )(a, b, *gather)
    return out if n else out[0]


def _mm_rows(a, b, name, epilogue, rows_in, vecs_in, rows_out, vecs_out, tb=False, tk=1024):
    M, K = a.shape
    N = b.shape[0] if tb else b.shape[1]
    tm, tk = _fit(M, 512), _fit(K, tk)
    nk = K // tk
    nr, nv, nro = len(rows_in), len(vecs_in), len(rows_out)

    def body(a_ref, b_ref, *refs):
        ins, outs, acc_ref = refs[:nr + nv], refs[nr + nv:nr + nv + nro + len(vecs_out)], refs[-1]
        i, k = pl.program_id(0), pl.program_id(1)
        part = _dot(a_ref[...], b_ref[...], NT if tb else NN)

        def finish(product):
            res = epilogue(product, *[r[...] for r in ins])
            for ref, val in zip(outs[:nro], res[:nro]):
                ref[...] = val.astype(ref.dtype)
            for ref, val in zip(outs[nro:], res[nro:]):
                _acc(ref, i == 0, val)

        if nk == 1:
            finish(part)
        else:
            @pl.when(k == 0)
            def _():
                acc_ref[...] = part

            @pl.when(jnp.logical_and(k > 0, k < nk - 1))
            def _():
                acc_ref[...] += part

            @pl.when(k == nk - 1)
            def _():
                finish(acc_ref[...] + part)

    row = lambda w: pl.BlockSpec((tm, w), lambda i, k: (i, 0))
    vec = lambda w: pl.BlockSpec((1, w), lambda i, k: (0, 0))
    b_spec = pl.BlockSpec((N, tk), lambda i, k: (0, k)) if tb else pl.BlockSpec((tk, N), lambda i, k: (k, 0))
    return pl.pallas_call(
        body, name=name, grid=(M // tm, nk),
        in_specs=[pl.BlockSpec((tm, tk), lambda i, k: (i, k)), b_spec] + [row(r.shape[1]) for r in rows_in]
        + [vec(v.shape[1]) for v in vecs_in],
        out_specs=[row(w) for w, _ in rows_out] + [vec(w) for w in vecs_out],
        out_shape=[jax.ShapeDtypeStruct((M, w), dt) for w, dt in rows_out] + [jax.ShapeDtypeStruct((1, w), F32) for w in vecs_out],
        scratch_shapes=[pltpu.VMEM((tm, N), F32)],
        compiler_params=_params("arbitrary", "arbitrary", vmem=VMEM_BIG))(a, b, *rows_in, *vecs_in)


def _mm_in_dx_call(dproj, w_in_pt, x, dx2, g_pre, exchange):
    L, K = dproj.shape
    N = w_in_pt.shape[1]
    tm, tk = _fit(L, 512), _fit(K, 1664)
    nm, nk = L // tm, K // tk
    n = len(exchange)

    def body(a_ref, b_ref, x_ref, dx2_ref, g_ref, *refs):
        parts, (gx_ref, dg_ref), got = refs[:n], refs[n:n + 2], refs[n + 2:2 * n + 2]
        acc_ref = refs[2 * n + 2]
        i, k = pl.program_id(0), pl.program_id(1)
        start, finish = _exchange_phases(parts, got, *refs[2 * n + 3:])
        pl.when(jnp.logical_and(i == 0, k == 0))(start)
        part = _dot(a_ref[...], b_ref[...], NN)

        @pl.when(k == 0)
        def _():
            acc_ref[...] = part

        @pl.when(jnp.logical_and(k > 0, k < nk - 1))
        def _():
            acc_ref[...] += part

        @pl.when(k == nk - 1)
        def _():
            d1, dg = _rms_bwd(x_ref[...], g_ref[...], acc_ref[...] + part)
            gx_ref[...] = dx2_ref[...] + d1
            _acc(dg_ref, i == 0, dg)

        pl.when(jnp.logical_and(i == nm - 1, k == nk - 1))(finish)

    assert nk >= 2
    rows = lambda: pl.BlockSpec((tm, N), lambda i, k: (i, 0))
    return pl.pallas_call(
        body, name="mm_in_dx", grid=(nm, nk),
        in_specs=[pl.BlockSpec((tm, tk), lambda i, k: (i, k)), pl.BlockSpec((tk, N), lambda i, k: (k, 0)),
                  rows(), rows(), pl.BlockSpec((1, N), lambda i, k: (0, 0))] + [ANY_SPEC] * n,
        out_specs=[rows(), pl.BlockSpec((1, N), lambda i, k: (0, 0))] + [ANY_SPEC] * n,
        out_shape=[jax.ShapeDtypeStruct((L, N), F32), jax.ShapeDtypeStruct((1, N), F32)]
        + [jax.ShapeDtypeStruct(p.shape, p.dtype) for p in exchange],
        scratch_shapes=[pltpu.VMEM((tm, N), F32)] + _comm_sems(n),
        compiler_params=_params("arbitrary", "arbitrary", vmem=VMEM_BIG))(dproj, w_in_pt, x, dx2, g_pre, *exchange)


TN_CHUNK = 512


def _mm_tn(a, b, name, tm=512, tk=512):
    K, M = a.shape
    K2, N = b.shape
    assert K == K2, (a.shape, b.shape)
    tm, tk, cn = _fit(M, tm), _fit(K, tk), _fit(N, TN_CHUNK)
    nk = K // tk

    def body(a_ref, b_ref, o_ref, acc_ref):
        k = pl.program_id(1)

        @pl.when(k == 0)
        def _():
            acc_ref[...] = jnp.zeros((tm, N), F32)

        at = a_ref[...].astype(BF16).T
        for c in range(N // cn):
            cols = slice(c * cn, (c + 1) * cn)
            acc_ref[:, cols] += _dot(at, b_ref[:, cols].astype(BF16), NN)

        @pl.when(k == nk - 1)
        def _():
            o_ref[...] = acc_ref[...].astype(BF16)

    return pl.pallas_call(
        body, name=name, grid=(M // tm, nk),
        in_specs=[pl.BlockSpec((tk, tm), lambda i, k: (k, i)), pl.BlockSpec((tk, N), lambda i, k: (k, 0))],
        out_specs=pl.BlockSpec((tm, N), lambda i, k: (i, 0)),
        out_shape=jax.ShapeDtypeStruct((M, N), BF16),
        scratch_shapes=[pltpu.VMEM((tm, N), F32)],
        compiler_params=_params("parallel", "arbitrary", vmem=VMEM_BIG))(a, b)


def _row(tl, n, col=0):
    return pl.BlockSpec((tl, n), lambda i: (i, col))


def _full(shape):
    return pl.BlockSpec(shape, lambda i: (0,) * len(shape))


def _rms(x, g):
    r = lax.rsqrt(jnp.mean(x * x, axis=-1, keepdims=True) + EPS)
    return x * r * g


def _rms_bwd(x, g, dy):
    n = x.shape[-1]
    r = lax.rsqrt(jnp.mean(x * x, axis=-1, keepdims=True) + EPS)
    gy = dy * g
    dx = r * gy - x * (r * r * r * (1.0 / n)) * jnp.sum(x * gy, axis=-1, keepdims=True)
    return dx, jnp.sum(dy * x * r, axis=0, keepdims=True)


def _acc(ref, first, val):
    @pl.when(first)
    def _():
        ref[...] = val

    @pl.when(jnp.logical_not(first))
    def _():
        ref[...] += val


def _rms_fwd_call(x, g, name):
    L, n = x.shape
    tl = _fit(L, 512)

    def body(x_ref, g_ref, o_ref):
        o_ref[...] = _rms(x_ref[...], g_ref[...]).astype(BF16)

    return pl.pallas_call(
        body, name=name, grid=(L // tl,), in_specs=[_row(tl, n), _full((1, n))], out_specs=_row(tl, n),
        out_shape=jax.ShapeDtypeStruct((L, n), BF16), compiler_params=_params("parallel"))(x, g)


def _rope_lanes(shape):
    lane = lax.broadcasted_iota(jnp.int32, shape, 1)
    return lane, jnp.logical_and(lane >= QK_NOPE, lane < QK_HEAD)


def _rope_apply(x, cosf, sinf, lane):
    rot = jnp.where(lane < QK_NOPE + QK_ROPE // 2, -pltpu.roll(x, LANES - QK_ROPE // 2, 1), pltpu.roll(x, QK_ROPE // 2, 1))
    return x * cosf + rot * sinf


def _rope_apply_t(dy, cosf, sinf, lane, is_rope):
    g = dy * sinf
    rot_t = jnp.where(lane < QK_NOPE + QK_ROPE // 2, pltpu.roll(g, LANES - QK_ROPE // 2, 1), -pltpu.roll(g, QK_ROPE // 2, 1))
    return dy * cosf + jnp.where(is_rope, rot_t, 0.0)


def _mla_proj_call(proj, q_norm, kv_norm, w_uq_pt, w_kv_pt, pos_col, inv_freq):
    L = proj.shape[0]
    tl = _fit(L, 512)

    def body(p_ref, gq_ref, gk_ref, wq_ref, wkv_ref, pos_ref, f_ref, qn_ref, kn_ref, qo_ref, kvo_ref, cos_ref, sin_ref):
        qn = _rms(p_ref[:, P_CQ:P_CKV], gq_ref[...]).astype(BF16)
        kn = _rms(p_ref[:, P_CKV:P_KR], gk_ref[...]).astype(BF16)
        qn_ref[...] = qn
        kn_ref[...] = kn
        q_pad = _dot(qn, wq_ref[...], NT)
        kv_pad = _dot(kn, wkv_ref[...], NT)
        lane, is_rope = _rope_lanes((tl, LANES))
        ang = pos_ref[...] * f_ref[...]
        cosf = jnp.where(is_rope, jnp.cos(ang), jnp.where(lane < QK_NOPE, 1.0, 0.0))
        sinf = jnp.where(is_rope, jnp.sin(ang), 0.0)
        cos_ref[...] = cosf
        sin_ref[...] = sinf
        kr = _rope_apply(p_ref[:, P_KR:P_U], cosf, sinf, lane)
        for h in range(N_HEADS):
            qh = _rope_apply(q_pad[:, h * LANES:(h + 1) * LANES], cosf, sinf, lane)
            qo_ref[:, h * LANES:(h + 1) * LANES] = (qh * Q_PRESCALE).astype(BF16)
            kvo_ref[:, 2 * h * LANES:(2 * h + 1) * LANES] = (kv_pad[:, 2 * h * LANES:(2 * h + 1) * LANES] + kr).astype(BF16)
            vh = jnp.where(lane == V_HEAD, 1.0, kv_pad[:, (2 * h + 1) * LANES:(2 * h + 2) * LANES])
            kvo_ref[:, (2 * h + 1) * LANES:(2 * h + 2) * LANES] = vh.astype(BF16)

    shape = lambda n, dt: jax.ShapeDtypeStruct((L, n), dt)
    return pl.pallas_call(
        body, name="mla_proj", grid=(L // tl,),
        in_specs=[_row(tl, P_U), _full((1, Q_RANK)), _full((1, KV_RANK)), _full((HEAD_PAD, Q_RANK)),
                  _full((2 * HEAD_PAD, KV_RANK)), _row(tl, 1), _full((1, LANES))],
        out_specs=[_row(tl, Q_RANK), _row(tl, KV_RANK), _row(tl, HEAD_PAD), _row(tl, 2 * HEAD_PAD), _row(tl, LANES), _row(tl, LANES)],
        out_shape=[shape(Q_RANK, BF16), shape(KV_RANK, BF16), shape(HEAD_PAD, BF16), shape(2 * HEAD_PAD, BF16),
                   shape(LANES, F32), shape(LANES, F32)],
        compiler_params=_params("parallel"))(proj, q_norm, kv_norm, w_uq_pt, w_kv_pt, pos_col, inv_freq)


def _mla_proj_bwd_call(dq, dkv, cosf, sinf, proj, q_norm, kv_norm, w_uq_pt, w_kv_pt):
    L = dq.shape[0]
    tl = _fit(L, 512)

    def body(dq_ref, dkv_ref, cos_ref, sin_ref, p_ref, gq_ref, gk_ref, wq_ref, wkv_ref,
             dqo_ref, dkvo_ref, d_ref, dgq_ref, dgk_ref):
        first = pl.program_id(0) == 0
        lane, is_rope = _rope_lanes((tl, LANES))
        cosf, sinf = cos_ref[...], sin_ref[...]
        dk_sum = jnp.zeros((tl, LANES), F32)
        for h in range(N_HEADS):
            dqo_ref[:, h * LANES:(h + 1) * LANES] = _rope_apply_t(dq_ref[:, h * LANES:(h + 1) * LANES], cosf, sinf, lane, is_rope).astype(BF16)
            dk_sum = dk_sum + dkv_ref[:, 2 * h * LANES:(2 * h + 1) * LANES]
        dkvo_ref[...] = dkv_ref[...].astype(BF16)
        dqn = _dot(dqo_ref[...], wq_ref[...], NN)
        dkn = _dot(dkvo_ref[...], wkv_ref[...], NN)
        dcq, dgq = _rms_bwd(p_ref[:, P_CQ:P_CKV], gq_ref[...], dqn)
        dckv, dgk = _rms_bwd(p_ref[:, P_CKV:P_KR], gk_ref[...], dkn)
        d_ref[:, P_CQ:P_CKV] = dcq.astype(BF16)
        d_ref[:, P_CKV:P_KR] = dckv.astype(BF16)
        d_ref[:, P_KR:P_U] = _rope_apply_t(dk_sum, cosf, sinf, lane, is_rope).astype(BF16)
        _acc(dgq_ref, first, dgq)
        _acc(dgk_ref, first, dgk)

    shape = lambda n: jax.ShapeDtypeStruct((L, n), BF16)
    return pl.pallas_call(
        body, name="mla_proj_bwd", grid=(L // tl,),
        in_specs=[_row(tl, HEAD_PAD), _row(tl, 2 * HEAD_PAD), _row(tl, LANES), _row(tl, LANES), _row(tl, P_KR),
                  _full((1, Q_RANK)), _full((1, KV_RANK)), _full((HEAD_PAD, Q_RANK)), _full((2 * HEAD_PAD, KV_RANK))],
        out_specs=[_row(tl, HEAD_PAD), _row(tl, 2 * HEAD_PAD), _row(tl, P_U), _full((1, Q_RANK)), _full((1, KV_RANK))],
        out_shape=[shape(HEAD_PAD), shape(2 * HEAD_PAD), shape(P_U), jax.ShapeDtypeStruct((1, Q_RANK), F32),
                   jax.ShapeDtypeStruct((1, KV_RANK), F32)],
        compiler_params=_params("arbitrary"))(dq, dkv, cosf, sinf, proj, q_norm, kv_norm, w_uq_pt, w_kv_pt)


GATE_TILE = 256
GATE_ROWS = 1024


def _merge_call(proj, b_gate, pa, ps):
    L = proj.shape[0]
    tl = _fit(L, GATE_ROWS)
    nc = D_MODEL // GATE_TILE
    g0, g1 = P_GATE // GATE_TILE, (P_GATE + D_MODEL) // GATE_TILE

    def body(l0_ref, l1_ref, b0_ref, b1_ref, pa_ref, ps_ref, o_ref):
        s0 = _sigmoid(l0_ref[...] + b0_ref[...])
        s1 = _sigmoid(l1_ref[...] + b1_ref[...])
        o_ref[...] = (s0 * pa_ref[...] + s1 * ps_ref[...]).astype(BF16)

    blk = lambda off: pl.BlockSpec((tl, GATE_TILE), lambda i, j: (i, off + j))
    bias = lambda off: pl.BlockSpec((1, GATE_TILE), lambda i, j: (0, off + j))
    return pl.pallas_call(
        body, name="merge", grid=(L // tl, nc),
        in_specs=[blk(g0), blk(g1), bias(0), bias(nc), blk(0), blk(0)],
        out_specs=blk(0), out_shape=jax.ShapeDtypeStruct((L, D_MODEL), BF16),
        compiler_params=_params("parallel", "parallel"))(proj, proj, b_gate, b_gate, pa, ps)


def _merge_bwd_call(dm, proj, b_gate, pa, ps):
    L = proj.shape[0]
    tl = _fit(L, GATE_ROWS)
    nc = D_MODEL // GATE_TILE
    g0, g1 = P_GATE // GATE_TILE, (P_GATE + D_MODEL) // GATE_TILE

    def body(dm_ref, l0_ref, l1_ref, b0_ref, b1_ref, pa_ref, ps_ref, dpa_ref, dps_ref, dl0_ref, dl1_ref, db0_ref, db1_ref):
        first = pl.program_id(1) == 0
        dm_ = dm_ref[...]
        s0 = _sigmoid(l0_ref[...] + b0_ref[...])
        s1 = _sigmoid(l1_ref[...] + b1_ref[...])
        dpa_ref[...] = (dm_ * s0).astype(BF16)
        dps_ref[...] = (dm_ * s1).astype(BF16)
        dl0 = dm_ * pa_ref[...] * s0 * (1.0 - s0)
        dl1 = dm_ * ps_ref[...] * s1 * (1.0 - s1)
        dl0_ref[...] = dl0.astype(BF16)
        dl1_ref[...] = dl1.astype(BF16)
        _acc(db0_ref, first, jnp.sum(dl0, axis=0, keepdims=True))
        _acc(db1_ref, first, jnp.sum(dl1, axis=0, keepdims=True))

    blk = lambda off: pl.BlockSpec((tl, GATE_TILE), lambda j, i: (i, off + j))
    bias = lambda off: pl.BlockSpec((1, GATE_TILE), lambda j, i: (0, off + j))
    act = jax.ShapeDtypeStruct((L, D_MODEL), BF16)
    vec = jax.ShapeDtypeStruct((1, D_MODEL), F32)
    return pl.pallas_call(
        body, name="merge_bwd", grid=(nc, L // tl),
        in_specs=[blk(0), blk(g0), blk(g1), bias(0), bias(nc), blk(0), blk(0)],
        out_specs=[blk(0), blk(0), blk(0), blk(0), bias(0), bias(0)],
        out_shape=[act, act, act, act, vec, vec],
        compiler_params=_params("parallel", "arbitrary"))(dm, proj, proj, b_gate, b_gate, pa, ps)


def _post_mix_rows(o, x, g_post, g_fpre):
    x2 = x + _rms(o, g_post)
    return o, x2, _rms(x2, g_fpre)


def _ffn_out_rows(ff, x2, target, g_fpost):
    n = ff.shape[-1]
    err = x2 + _rms(ff, g_fpost) - target
    part = 0.5 * jnp.sum(jnp.sum(err * err, axis=-1, keepdims=True) * (1.0 / n), axis=0, keepdims=True)
    dy = err * (1.0 / n)
    dff, dg = _rms_bwd(ff, g_fpost, dy)
    return dy, dff, jnp.broadcast_to(part, (1, LANES)), dg


def _post_bwd_rows(dhn2, x2, dy, o, g_fpre, g_post):
    d1, dgf = _rms_bwd(x2, g_fpre, dhn2)
    dx2 = dy + d1
    do, dgp = _rms_bwd(o, g_post, dx2)
    return dx2, do, dgf, dgp


CONV_TILE = 256
CONV_WIDE = 1408
HALO = 16


def _conv3(w, b, x0, x1, x2):
    return b + w[2:3] * x0 + w[1:2] * x1 + w[0:1] * x2


def _down(x, by):
    return pltpu.roll(x, by, 0)


def _edge_down(edge, before, by):
    r = lax.broadcasted_iota(jnp.int32, edge.shape, 0)
    return jnp.where(r < by, pltpu.roll(before, by, 0), pltpu.roll(edge, by, 0))


def _edge_up(edge, after, by):
    r = lax.broadcasted_iota(jnp.int32, edge.shape, 0)
    return jnp.where(r >= HALO - by, pltpu.roll(after, HALO - by, 0), pltpu.roll(edge, HALO - by, 0))


def _gated(w_g, b_g, w_v, b_v, hg, hv, g1, g2, v1, v2):
    return _conv3(w_g, b_g, hg, g1, g2), _conv3(w_v, b_v, hv, v1, v2)


def _conv_specs(tl, tc, rows_inner):
    nh = tl // HALO
    if rows_inner:
        ij = lambda f: (lambda j, i: f(i, j))
    else:
        ij = lambda f: f
    cur = lambda off: pl.BlockSpec((tl, tc), ij(lambda i, j: (i, off + j)))
    prev = lambda off: pl.BlockSpec((HALO, tc), ij(lambda i, j: (jnp.maximum(i * nh - 1, 0), off + j)))
    par = lambda rows, off: pl.BlockSpec((rows, tc), ij(lambda i, j: (0, off + j)))
    return cur, prev, par


def _conv_act_call(h, conv_w, conv_b):
    L = h.shape[0]
    tl = _fit(L, 256)
    nc = D_FF // CONV_WIDE
    cur, prev, par = _conv_specs(tl, CONV_WIDE, False)

    def body(hg_ref, hv_ref, pg_ref, pv_ref, wg_ref, wv_ref, bg_ref, bv_ref, a_ref):
        not_first = (pl.program_id(0) > 0).astype(F32)
        par = (wg_ref[...], bg_ref[...], wv_ref[...], bv_ref[...])
        hg, hv = hg_ref[...], hv_ref[...]
        gate, val = _gated(*par, hg, hv, _down(hg, 1), _down(hg, 2), _down(hv, 1), _down(hv, 2))
        a_ref[...] = (_gelu(gate) * val).astype(BF16)
        eg, ev, bg, bv = hg[:HALO], hv[:HALO], pg_ref[...] * not_first, pv_ref[...] * not_first
        gate, val = _gated(*par, eg, ev, _edge_down(eg, bg, 1), _edge_down(eg, bg, 2),
                           _edge_down(ev, bv, 1), _edge_down(ev, bv, 2))
        a_ref[:HALO, :] = (_gelu(gate) * val).astype(BF16)

    return pl.pallas_call(
        body, name="conv_act", grid=(L // tl, nc),
        in_specs=[cur(0), cur(nc), prev(0), prev(nc), par(3, 0), par(3, nc), par(1, 0), par(1, nc)],
        out_specs=cur(0), out_shape=jax.ShapeDtypeStruct((L, D_FF), BF16),
        compiler_params=_params("parallel", "parallel"))(h, h, h, h, conv_w, conv_w, conv_b, conv_b)


def _conv_act_bwd_call(da, h, conv_w, conv_b):
    L = h.shape[0]
    tl = _fit(L, 512)
    nc = D_FF // CONV_TILE
    cur, prev, par = _conv_specs(tl, CONV_TILE, True)

    def body(da_ref, hg_ref, hv_ref, pg_ref, pv_ref, wg_ref, wv_ref, bg_ref, bv_ref,
             dg_ref, dv_ref, dwg_ref, dwv_ref, dbg_ref, dbv_ref):
        first = pl.program_id(1) == 0
        not_first = (pl.program_id(1) > 0).astype(F32)
        par = (wg_ref[...], bg_ref[...], wv_ref[...], bv_ref[...])
        col = lambda t: jnp.sum(t, axis=0, keepdims=True)

        def grads(da_, hg, hv, g1, g2, v1, v2):
            gate, val = _gated(*par, hg, hv, g1, g2, v1, v2)
            act, slope = _gelu_and_grad(gate)
            dgate = da_ * val * slope
            dval = da_ * act
            sums = (jnp.concatenate([col(dgate * g2), col(dgate * g1), col(dgate * hg)], axis=0),
                    jnp.concatenate([col(dval * v2), col(dval * v1), col(dval * hv)], axis=0), col(dgate), col(dval))
            return dgate, dval, sums

        da_, hg, hv = da_ref[...], hg_ref[...], hv_ref[...]
        shifted = (_down(hg, 1), _down(hg, 2), _down(hv, 1), _down(hv, 2))
        dgate, dval, whole = grads(da_, hg, hv, *shifted)
        dg_ref[...] = dgate.astype(BF16)
        dv_ref[...] = dval.astype(BF16)
        edge = lambda t: t[:HALO]
        _, _, wrapped = grads(edge(da_), edge(hg), edge(hv), *[edge(s) for s in shifted])
        eg, ev, bg, bv = edge(hg), edge(hv), pg_ref[...] * not_first, pv_ref[...] * not_first
        dgate, dval, fixed = grads(edge(da_), eg, ev, _edge_down(eg, bg, 1), _edge_down(eg, bg, 2),
                                   _edge_down(ev, bv, 1), _edge_down(ev, bv, 2))
        dg_ref[:HALO, :] = dgate.astype(BF16)
        dv_ref[:HALO, :] = dval.astype(BF16)
        for ref, a, b, c in zip((dwg_ref, dwv_ref, dbg_ref, dbv_ref), whole, wrapped, fixed):
            _acc(ref, first, a - b + c)

    act = jax.ShapeDtypeStruct((L, D_FF), BF16)
    w3 = jax.ShapeDtypeStruct((3, D_FF), F32)
    w1 = jax.ShapeDtypeStruct((1, D_FF), F32)
    return pl.pallas_call(
        body, name="conv_act_bwd", grid=(nc, L // tl),
        in_specs=[cur(0), cur(0), cur(nc), prev(0), prev(nc), par(3, 0), par(3, nc), par(1, 0), par(1, nc)],
        out_specs=[cur(0), cur(0), par(3, 0), par(3, 0), par(1, 0), par(1, 0)],
        out_shape=[act, act, w3, w3, w1, w1],
        compiler_params=_params("parallel", "arbitrary"))(da, h, h, h, h, conv_w, conv_w, conv_b, conv_b)


def _conv_t_call(dgate, dval, conv_w):
    L = dgate.shape[0]
    tl = _fit(L, 512)
    nc = D_FF // CONV_WIDE
    nh = tl // HALO

    def body(dg_ref, dv_ref, ng_ref, nv_ref, w_ref, o_ref):
        not_last = (pl.program_id(0) < L // tl - 1).astype(F32)

        def emit(d_ref, n_ref):
            c = d_ref[...].astype(F32)
            w = w_ref[...]
            o_ref[...] = _conv3(w, 0.0, c, pltpu.roll(c, tl - 1, 0), pltpu.roll(c, tl - 2, 0)).astype(BF16)
            edge, after = c[tl - HALO:], n_ref[...].astype(F32) * not_last
            o_ref[tl - HALO:, :] = _conv3(w, 0.0, edge, _edge_up(edge, after, 1), _edge_up(edge, after, 2)).astype(BF16)

        pl.when(pl.program_id(1) < nc)(lambda: emit(dg_ref, ng_ref))
        pl.when(pl.program_id(1) >= nc)(lambda: emit(dv_ref, nv_ref))

    gate_col = lambda j: jnp.minimum(j, nc - 1)
    val_col = lambda j: jnp.maximum(j - nc, 0)
    after_row = lambda i: jnp.minimum((i + 1) * nh, L // HALO - 1)
    tile = lambda col: pl.BlockSpec((tl, CONV_WIDE), lambda i, j: (i, col(j)))
    after = lambda col: pl.BlockSpec((HALO, CONV_WIDE), lambda i, j: (after_row(i), col(j)))
    return pl.pallas_call(
        body, name="conv_t", grid=(L // tl, 2 * nc),
        in_specs=[tile(gate_col), tile(val_col), after(gate_col), after(val_col), pl.BlockSpec((3, CONV_WIDE), lambda i, j: (0, j))],
        out_specs=pl.BlockSpec((tl, CONV_WIDE), lambda i, j: (i, j)),
        out_shape=jax.ShapeDtypeStruct((L, 2 * D_FF), BF16),
        compiler_params=_params("parallel", "parallel"))(dgate, dval, dgate, dval, conv_w)


def _glu_call(y1, w_glu, b_glu):
    L, n = y1.shape
    tl = _fit(L, 512)

    def body(y_ref, w_ref, b_ref, o_ref):
        y2 = _gelu(y_ref[...])
        z = _dot(y2.astype(BF16), w_ref[...], NN) + b_ref[...]
        o_ref[...] = (y2 * _sigmoid(z)).astype(BF16)

    return pl.pallas_call(
        body, name="glu", grid=(L // tl,), in_specs=[_row(tl, n), _full((n, n)), _full((1, n))],
        out_specs=_row(tl, n), out_shape=jax.ShapeDtypeStruct((L, n), BF16),
        compiler_params=_params("parallel"))(y1, w_glu, b_glu)


def _glu_bwd_call(dout, y1, w_glu, b_glu):
    L, n = y1.shape
    tl = _fit(L, 512)

    def body(do_ref, y_ref, w_ref, b_ref, dy_ref, dw_ref, db_ref):
        first = pl.program_id(0) == 0
        y1_ = y_ref[...]
        y2, slope = _gelu_and_grad(y1_)
        y2b = y2.astype(BF16)
        w = w_ref[...]
        sg = _sigmoid(_dot(y2b, w, NN) + b_ref[...])
        dout_ = do_ref[...].astype(F32)
        dz = dout_ * y2 * sg * (1.0 - sg)
        dzb = dz.astype(BF16)
        dy2 = dout_ * sg + _dot(dzb, w, NT)
        dy_ref[...] = dy2 * slope
        _acc(dw_ref, first, _dot(y2b, dzb, TN))
        _acc(db_ref, first, jnp.sum(dz, axis=0, keepdims=True))

    return pl.pallas_call(
        body, name="glu_bwd", grid=(L // tl,),
        in_specs=[_row(tl, n), _row(tl, n), _full((n, n)), _full((1, n))],
        out_specs=[_row(tl, n), _full((n, n)), _full((1, n))],
        out_shape=[jax.ShapeDtypeStruct((L, n), F32), jax.ShapeDtypeStruct((n, n), F32), jax.ShapeDtypeStruct((1, n), F32)],
        compiler_params=_params("arbitrary"))(dout, y1, w_glu, b_glu)


ATTN_TILE = 1024
ATTN_SCALE = 1.0 / math.sqrt(QK_HEAD)


ATTN_HEADS = 2
ATTN_GROUPS = N_HEADS // ATTN_HEADS
LOG2E = 1.0 / math.log(2.0)
Q_PRESCALE = ATTN_SCALE * LOG2E
ANY_SPEC = pl.BlockSpec(memory_space=pl.ANY)


def _attn_fwd_call(q, kv, blocks):
    L = q.shape[0]
    t = _fit(L, ATTN_TILE)
    nq = L // t
    n = len(blocks)

    def body(q_ref, kv_ref, *refs):
        blk_refs, (o_ref, lse_ref), gat_refs = refs[:n], refs[n:n + 2], refs[n + 2:2 * n + 2]
        m_s, acc_s, send_sems, recv_sems, local_sems = refs[2 * n + 2:]
        g, i = pl.program_id(0), pl.program_id(1)
        start, forward, finish = _gather_phases(blk_refs, gat_refs, send_sems, recv_sems, local_sems)
        pl.when(jnp.logical_and(g == 0, i == 0))(start)
        m_s[...] = jnp.full((ATTN_HEADS, t, 1), NEG, F32)
        acc_s[...] = jnp.zeros((ATTN_HEADS, t, LANES), F32)
        below = lax.broadcasted_iota(jnp.int32, (t, t), 1) <= lax.broadcasted_iota(jnp.int32, (t, t), 0)

        def block_step(kb, on_diagonal):
            rows = pl.ds(pl.multiple_of(kb * t, t), t)
            for a in range(ATTN_HEADS):
                s = _dot(q_ref[:, a * LANES:(a + 1) * LANES], kv_ref[rows, 2 * a * LANES:(2 * a + 1) * LANES], NT)
                if on_diagonal:
                    s = jnp.where(below, s, NEG)
                m_prev = m_s[a]
                m_new = jnp.maximum(m_prev, jnp.max(s, axis=1, keepdims=True))
                p = jnp.exp2(s - m_new)
                pv = _dot(p.astype(BF16), kv_ref[rows, (2 * a + 1) * LANES:(2 * a + 2) * LANES], NN)
                acc_s[a] = jnp.exp2(m_prev - m_new) * acc_s[a] + pv
                m_s[a] = m_new

        def step(kb, carry):
            block_step(kb, False)
            return carry

        lax.fori_loop(0, i, step, 0)
        block_step(i, True)
        lane = lax.broadcasted_iota(jnp.int32, (t, LANES), 1)
        for a in range(ATTN_HEADS):
            acc = acc_s[a]
            l = jnp.sum(jnp.where(lane == V_HEAD, acc, 0.0), axis=1, keepdims=True)
            o_ref[:, a * LANES:(a + 1) * LANES] = (acc / l).astype(BF16)
            lse_ref[a] = m_s[a] + jnp.log(l) * LOG2E
        pl.when(jnp.logical_and(g == (3 * ATTN_GROUPS) // 4, i == 0))(forward)
        pl.when(jnp.logical_and(g == ATTN_GROUPS - 1, i == nq - 1))(finish)

    gw = ATTN_HEADS * LANES
    return pl.pallas_call(
        body, name="attn_fwd", grid=(ATTN_GROUPS, nq),
        in_specs=[pl.BlockSpec((t, gw), lambda g, i: (i, g)),
                  pl.BlockSpec((L, 2 * gw), lambda g, i: (0, g))] + [ANY_SPEC] * n,
        out_specs=[pl.BlockSpec((t, gw), lambda g, i: (i, g)),
                   pl.BlockSpec((ATTN_HEADS, t, 1), lambda g, i: (g, i, 0))] + [ANY_SPEC] * n,
        out_shape=[jax.ShapeDtypeStruct((L, HEAD_PAD), BF16), jax.ShapeDtypeStruct((N_HEADS, L, 1), F32)]
        + [jax.ShapeDtypeStruct((N_DEV,) + b.shape, b.dtype) for b in blocks],
        scratch_shapes=[pltpu.VMEM((ATTN_HEADS, t, 1), F32), pltpu.VMEM((ATTN_HEADS, t, LANES), F32)] + _comm_sems(n),
        compiler_params=_params("arbitrary", "arbitrary", vmem=VMEM_BIG))(q, kv, *blocks)


def _attn_bwd_call(q, kv, o, do, lse, parts, blocks):
    L = q.shape[0]
    t = _fit(L, ATTN_TILE)
    nq = L // t
    n1, n = len(parts), len(parts) + len(blocks)

    def body(q_ref, do_ref, o_ref, lse_ref, kv_ref, *refs):
        in_refs, (dq_ref, dkv_ref), out_refs = refs[:n], refs[n:n + 2], refs[n + 2:2 * n + 2]
        dk_s, dv_s = refs[2 * n + 2:2 * n + 4]
        g, j = pl.program_id(0), pl.program_id(1)
        start, finish = _exchange_phases(in_refs[:n1], out_refs[:n1], *refs[2 * n + 4:2 * n + 7])
        start_blocks, finish_blocks = _exchange_phases(in_refs[n1:], out_refs[n1:], *refs[2 * n + 7:], same_source=True)

        @pl.when(jnp.logical_and(g == 0, j == 0))
        def _():
            start()
            start_blocks()

        @pl.when(j == 0)
        def _():
            dq_ref[...] = jnp.zeros((L, ATTN_HEADS * LANES), F32)

        dk_s[...] = jnp.zeros((ATTN_HEADS, t, LANES), F32)
        dv_s[...] = jnp.zeros((ATTN_HEADS, t, LANES), F32)
        below = lax.broadcasted_iota(jnp.int32, (t, t), 1) <= lax.broadcasted_iota(jnp.int32, (t, t), 0)

        def block_step(i, on_diagonal):
            rows = pl.ds(pl.multiple_of(i * t, t), t)
            for a in range(ATTN_HEADS):
                lanes = slice(a * LANES, (a + 1) * LANES)
                qi = q_ref[rows, lanes]
                doi = do_ref[rows, lanes]
                kblk = kv_ref[:, 2 * a * LANES:(2 * a + 1) * LANES]
                delta = jnp.sum(doi.astype(F32) * o_ref[rows, lanes].astype(F32), axis=1, keepdims=True)
                s = _dot(qi, kblk, NT)
                if on_diagonal:
                    s = jnp.where(below, s, NEG)
                p = jnp.exp2(s - lse_ref[a, rows, :])
                dv_s[a] += _dot(p.astype(BF16), doi, TN)
                ds = (p * (_dot(doi, kv_ref[:, (2 * a + 1) * LANES:(2 * a + 2) * LANES], NT) - delta)).astype(BF16)
                dk_s[a] += _dot(ds, qi, TN)
                dq_ref[rows, lanes] += _dot(ds, kblk, NN) * ATTN_SCALE

        def step(i, carry):
            block_step(i, False)
            return carry

        block_step(j, True)
        lax.fori_loop(j + 1, nq, step, 0)
        for a in range(ATTN_HEADS):
            dkv_ref[:, 2 * a * LANES:(2 * a + 1) * LANES] = dk_s[a] * (1.0 / LOG2E)
            dkv_ref[:, (2 * a + 1) * LANES:(2 * a + 2) * LANES] = dv_s[a]
        @pl.when(jnp.logical_and(g == ATTN_GROUPS - 1, j == nq - 1))
        def _():
            finish()
            finish_blocks()

    gw = ATTN_HEADS * LANES
    whole = lambda: pl.BlockSpec((L, gw), lambda g, j: (0, g))
    acc = pltpu.VMEM((ATTN_HEADS, t, LANES), F32)
    return pl.pallas_call(
        body, name="attn_bwd", grid=(ATTN_GROUPS, nq),
        in_specs=[whole(), whole(), whole(), pl.BlockSpec((ATTN_HEADS, L, 1), lambda g, j: (g, 0, 0)),
                  pl.BlockSpec((t, 2 * gw), lambda g, j: (j, g))] + [ANY_SPEC] * n,
        out_specs=[whole(), pl.BlockSpec((t, 2 * gw), lambda g, j: (j, g))] + [ANY_SPEC] * n,
        out_shape=[jax.ShapeDtypeStruct((L, HEAD_PAD), F32), jax.ShapeDtypeStruct((L, 2 * HEAD_PAD), F32)]
        + [jax.ShapeDtypeStruct(p.shape, p.dtype) for p in parts]
        + [jax.ShapeDtypeStruct((N_DEV,) + b.shape, b.dtype) for b in blocks],
        scratch_shapes=[acc, acc] + _comm_sems(n1) + _comm_sems(n - n1),
        compiler_params=_params("arbitrary", "arbitrary", vmem=VMEM_BIG))(q, do, o, lse, kv, *parts, *blocks)


def _disc(lr, li, ldt, br, bi):
    dt = jnp.exp(ldt)
    mag = jnp.exp(lr * dt)
    ang = li * dt
    a_re, a_im = mag * jnp.cos(ang), mag * jnp.sin(ang)
    den = lr * lr + li * li
    n_re, n_im = a_re - 1.0, a_im
    z_re = (n_re * lr + n_im * li) / den
    z_im = (n_im * lr - n_re * li) / den
    return a_re, a_im, z_re * br - z_im * bi, z_re * bi + z_im * br


def _disc_call(lr, li, ldt, br, bi):
    def body(lr_ref, li_ref, ldt_ref, br_ref, bi_ref, ar_ref, ai_ref, bbr_ref, bbi_ref):
        ar_ref[...], ai_ref[...], bbr_ref[...], bbi_ref[...] = _disc(
            lr_ref[...], li_ref[...], ldt_ref[...], br_ref[...], bi_ref[...])

    c1 = jax.ShapeDtypeStruct((SSM_NSTATE, 1), F32)
    c16 = jax.ShapeDtypeStruct((SSM_NSTATE, SSM_GROUP), F32)
    return pl.pallas_call(body, name="ssm_disc", out_shape=[c1, c1, c16, c16])(lr, li, ldt, br, bi)


def _disc_bwd_call(lr, li, ldt, br, bi, dar, dai, dbbr, dbbi):
    def body(lr_ref, li_ref, ldt_ref, br_ref, bi_ref, dar_ref, dai_ref, dbbr_ref, dbbi_ref,
             dlr_ref, dli_ref, dldt_ref, dbr_ref, dbi_ref):
        _, vjp = jax.vjp(_disc, lr_ref[...], li_ref[...], ldt_ref[...], br_ref[...], bi_ref[...])
        dlr_ref[...], dli_ref[...], dldt_ref[...], dbr_ref[...], dbi_ref[...] = vjp(
            (dar_ref[...], dai_ref[...], dbbr_ref[...], dbbi_ref[...]))

    c1 = jax.ShapeDtypeStruct((SSM_NSTATE, 1), F32)
    c16 = jax.ShapeDtypeStruct((SSM_NSTATE, SSM_GROUP), F32)
    return pl.pallas_call(body, name="ssm_disc_bwd", out_shape=[c1, c1, c1, c16, c16])(
        lr, li, ldt, br, bi, dar, dai, dbbr, dbbi)


SSM_ROWS = 512
SSM_CW = SSM_NSTATE // SSM_CHUNKS
SSM_CU = SSM_WIDTH // SSM_CHUNKS


def _cmul(ar, ai, br, bi):
    return ar * br - ai * bi, ar * bi + ai * br


def _power(ar1, ai1, n):
    def step(_, c):
        return _cmul(c[0], c[1], ar1, ai1)

    return lax.fori_loop(0, n, step, (jnp.ones_like(ar1), jnp.zeros_like(ar1)))


def _tile(k):
    return pl.ds(pl.multiple_of(k * 8, 8), 8)


def _ssm_fwd_call(u, a_re, a_im, bb_re, bb_im, cm_re, cm_im, d_skip):
    L = u.shape[0]
    seg = L // 8
    rb = _fit(L, SSM_ROWS)

    def body(u_ref, ar_ref, ai_ref, bbr_ref, bbi_ref, cmr_ref, cmi_ref, d_ref, y_ref, sre_hbm, sim_hbm,
             s_re, s_im, sems):
        q = pl.program_id(0)

        def bu_step(r, c):
            rows = pl.ds(pl.multiple_of(r * rb, rb), rb)
            ub = u_ref[rows, :].astype(BF16)
            s_re[rows, :] = _dot(ub, bbr_ref[0], NN)
            s_im[rows, :] = _dot(ub, bbi_ref[0], NN)
            return c

        lax.fori_loop(0, L // rb, bu_step, 0)
        ar1, ai1 = ar_ref[...], ai_ref[...]
        ar = jnp.broadcast_to(ar1, (8, SSM_CW))
        ai = jnp.broadcast_to(ai1, (8, SSM_CW))

        def local(k, c):
            nr, ni = _cmul(ar, ai, c[0], c[1])
            nr = nr + s_re[_tile(k), :]
            ni = ni + s_im[_tile(k), :]
            s_re[_tile(k), :] = nr
            s_im[_tile(k), :] = ni
            return nr, ni

        zero8 = jnp.zeros((8, SSM_CW), F32)
        lax.fori_loop(0, seg, local, (zero8, zero8))
        pr, pi = _power(ar1, ai1, seg)
        end_r = s_re[pl.ds((seg - 1) * 8, 8), :]
        end_i = s_im[pl.ds((seg - 1) * 8, 8), :]
        er = jnp.zeros((1, SSM_CW), F32)
        ei = jnp.zeros((1, SSM_CW), F32)
        rows_r, rows_i = [er], [ei]
        for j in range(7):
            tr, ti = _cmul(pr, pi, er, ei)
            er, ei = end_r[j:j + 1] + tr, end_i[j:j + 1] + ti
            rows_r.append(er)
            rows_i.append(ei)
        e_r = jnp.concatenate(rows_r, axis=0)
        e_i = jnp.concatenate(rows_i, axis=0)

        def fix(k, c):
            wr, wi = _cmul(c[0], c[1], ar, ai)
            fr, fi = _cmul(wr, wi, e_r, e_i)
            s_re[_tile(k), :] += fr
            s_im[_tile(k), :] += fi
            return wr, wi

        lax.fori_loop(0, seg, fix, (jnp.ones((8, SSM_CW), F32), zero8))
        out_r = pltpu.make_async_copy(s_re, sre_hbm.at[q], sems.at[0])
        out_i = pltpu.make_async_copy(s_im, sim_hbm.at[q], sems.at[1])
        out_r.start()
        out_i.start()

        def y_step(r, c):
            rows = pl.ds(pl.multiple_of(r * rb, rb), rb)
            y = _dot(s_re[rows, :].astype(BF16), cmr_ref[0], NN) - _dot(s_im[rows, :].astype(BF16), cmi_ref[0], NN)
            y_ref[rows, :] = y + d_ref[...] * u_ref[rows, :]
            return c

        lax.fori_loop(0, L // rb, y_step, 0)
        out_r.wait()
        out_i.wait()

    chunk = lambda rows, cols: pl.BlockSpec((rows, cols), lambda q: (0, q))
    mat = lambda r, c: pl.BlockSpec((1, r, c), lambda q: (q, 0, 0))
    anyspec = pl.BlockSpec(memory_space=pl.ANY)
    states = jax.ShapeDtypeStruct((SSM_CHUNKS, L, SSM_CW), F32)
    return pl.pallas_call(
        body, name="ssm_fwd", grid=(SSM_CHUNKS,),
        in_specs=[chunk(L, SSM_CU), chunk(1, SSM_CW), chunk(1, SSM_CW), mat(SSM_CU, SSM_CW), mat(SSM_CU, SSM_CW),
                  mat(SSM_CW, SSM_CU), mat(SSM_CW, SSM_CU), chunk(1, SSM_CU)],
        out_specs=[chunk(L, SSM_CU), anyspec, anyspec],
        out_shape=[jax.ShapeDtypeStruct((L, SSM_WIDTH), F32), states, states],
        scratch_shapes=[pltpu.VMEM((L, SSM_CW), F32), pltpu.VMEM((L, SSM_CW), F32), pltpu.SemaphoreType.DMA((2,))],
        compiler_params=_params("arbitrary", vmem=VMEM_BIG))(u, a_re, a_im, bb_re, bb_im, cm_re, cm_im, d_skip)


def _ssm_bwd_call(dy, u, s_re_all, s_im_all, a_re, a_im, bb_re, bb_im, cm_re, cm_im, d_skip):
    L = u.shape[0]
    seg = L // 8
    rb = _fit(L, SSM_ROWS)

    def body(dy_ref, u_ref, sre_hbm, sim_hbm, ar_ref, ai_ref, bbr_ref, bbi_ref, cmr_ref, cmi_ref, d_ref,
             du_ref, dbbr_ref, dbbi_ref, dcmr_ref, dcmi_ref, dar_ref, dai_ref, dd_ref,
             g_re, g_im, s_re, s_im, sems):
        q = pl.program_id(0)
        in_r = pltpu.make_async_copy(sre_hbm.at[q], s_re, sems.at[0])
        in_i = pltpu.make_async_copy(sim_hbm.at[q], s_im, sems.at[1])
        in_r.start()
        in_i.start()

        def ds_step(r, c):
            rows = pl.ds(pl.multiple_of(r * rb, rb), rb)
            dyb = dy_ref[rows, :].astype(BF16)
            g_re[rows, :] = _dot(dyb, cmr_ref[0], NT)
            g_im[rows, :] = -_dot(dyb, cmi_ref[0], NT)
            return c

        lax.fori_loop(0, L // rb, ds_step, 0)
        ar1, ai1 = ar_ref[...], ai_ref[...]
        ar = jnp.broadcast_to(ar1, (8, SSM_CW))
        nai = jnp.broadcast_to(-ai1, (8, SSM_CW))

        def local(kk, c):
            k = seg - 1 - kk
            nr, ni = _cmul(ar, nai, c[0], c[1])
            nr = nr + g_re[_tile(k), :]
            ni = ni + g_im[_tile(k), :]
            g_re[_tile(k), :] = nr
            g_im[_tile(k), :] = ni
            return nr, ni

        zero8 = jnp.zeros((8, SSM_CW), F32)
        lax.fori_loop(0, seg, local, (zero8, zero8))
        pr, pi = _power(ar1, -ai1, seg)
        head_r = g_re[pl.ds(0, 8), :]
        head_i = g_im[pl.ds(0, 8), :]
        fr = jnp.zeros((1, SSM_CW), F32)
        fi = jnp.zeros((1, SSM_CW), F32)
        rows_r, rows_i = [fr], [fi]
        for j in range(6, -1, -1):
            tr, ti = _cmul(pr, pi, fr, fi)
            fr, fi = head_r[j + 1:j + 2] + tr, head_i[j + 1:j + 2] + ti
            rows_r.insert(0, fr)
            rows_i.insert(0, fi)
        f_r = jnp.concatenate(rows_r, axis=0)
        f_i = jnp.concatenate(rows_i, axis=0)
        in_r.wait()
        in_i.wait()

        def fixed(k, wr, wi):
            xr, xi = _cmul(wr, wi, f_r, f_i)
            gr = g_re[_tile(k), :] + xr
            gi = g_im[_tile(k), :] + xi
            g_re[_tile(k), :] = gr
            g_im[_tile(k), :] = gi
            return gr, gi

        def fix(kk, c):
            k = seg - 1 - kk
            wr, wi = _cmul(c[0], c[1], ar, nai)
            gr, gi = fixed(k, wr, wi)
            pr_, pi_ = s_re[_tile(k - 1), :], s_im[_tile(k - 1), :]
            return wr, wi, c[2] + gr * pr_ + gi * pi_, c[3] + gi * pr_ - gr * pi_

        wr, wi, acc_r, acc_i = lax.fori_loop(0, seg - 1, fix, (jnp.ones((8, SSM_CW), F32), zero8, zero8, zero8))
        wr, wi = _cmul(wr, wi, ar, nai)
        gr, gi = fixed(0, wr, wi)
        row8 = lax.broadcasted_iota(jnp.int32, (8, SSM_CW), 0)
        pr_ = jnp.where(row8 > 0, pltpu.roll(s_re[pl.ds((seg - 1) * 8, 8), :], 1, 0), 0.0)
        pi_ = jnp.where(row8 > 0, pltpu.roll(s_im[pl.ds((seg - 1) * 8, 8), :], 1, 0), 0.0)
        acc_r = acc_r + gr * pr_ + gi * pi_
        acc_i = acc_i + gi * pr_ - gr * pi_
        dar_ref[...] = jnp.sum(acc_r, axis=0, keepdims=True)
        dai_ref[...] = jnp.sum(acc_i, axis=0, keepdims=True)

        dbbr_ref[...] = jnp.zeros((1, SSM_CU, SSM_CW), F32)
        dbbi_ref[...] = jnp.zeros((1, SSM_CU, SSM_CW), F32)
        dcmr_ref[...] = jnp.zeros((1, SSM_CW, SSM_CU), F32)
        dcmi_ref[...] = jnp.zeros((1, SSM_CW, SSM_CU), F32)
        dd_ref[...] = jnp.zeros((1, SSM_CU), F32)

        def grad_step(r, c):
            rows = pl.ds(pl.multiple_of(r * rb, rb), rb)
            ub, dyv = u_ref[rows, :], dy_ref[rows, :]
            ubb, dyb = ub.astype(BF16), dyv.astype(BF16)
            grb, gib = g_re[rows, :].astype(BF16), g_im[rows, :].astype(BF16)
            dbbr_ref[0] += _dot(ubb, grb, TN)
            dbbi_ref[0] += _dot(ubb, gib, TN)
            dcmr_ref[0] += _dot(s_re[rows, :].astype(BF16), dyb, TN)
            dcmi_ref[0] -= _dot(s_im[rows, :].astype(BF16), dyb, TN)
            du_ref[rows, :] = _dot(grb, bbr_ref[0], NT) + _dot(gib, bbi_ref[0], NT) + d_ref[...] * dyv
            dd_ref[...] += jnp.sum(dyv * ub, axis=0, keepdims=True)
            return c

        lax.fori_loop(0, L // rb, grad_step, 0)

    chunk = lambda rows, cols: pl.BlockSpec((rows, cols), lambda q: (0, q))
    mat = lambda r, c: pl.BlockSpec((1, r, c), lambda q: (q, 0, 0))
    anyspec = pl.BlockSpec(memory_space=pl.ANY)
    big = lambda: pltpu.VMEM((L, SSM_CW), F32)
    return pl.pallas_call(
        body, name="ssm_bwd", grid=(SSM_CHUNKS,),
        in_specs=[chunk(L, SSM_CU), chunk(L, SSM_CU), anyspec, anyspec, chunk(1, SSM_CW), chunk(1, SSM_CW),
                  mat(SSM_CU, SSM_CW), mat(SSM_CU, SSM_CW), mat(SSM_CW, SSM_CU), mat(SSM_CW, SSM_CU), chunk(1, SSM_CU)],
        out_specs=[chunk(L, SSM_CU), mat(SSM_CU, SSM_CW), mat(SSM_CU, SSM_CW), mat(SSM_CW, SSM_CU), mat(SSM_CW, SSM_CU),
                   chunk(1, SSM_CW), chunk(1, SSM_CW), chunk(1, SSM_CU)],
        out_shape=[jax.ShapeDtypeStruct((L, SSM_WIDTH), F32),
                   jax.ShapeDtypeStruct((SSM_CHUNKS, SSM_CU, SSM_CW), F32), jax.ShapeDtypeStruct((SSM_CHUNKS, SSM_CU, SSM_CW), F32),
                   jax.ShapeDtypeStruct((SSM_CHUNKS, SSM_CW, SSM_CU), F32), jax.ShapeDtypeStruct((SSM_CHUNKS, SSM_CW, SSM_CU), F32),
                   jax.ShapeDtypeStruct((1, SSM_NSTATE), F32), jax.ShapeDtypeStruct((1, SSM_NSTATE), F32),
                   jax.ShapeDtypeStruct((1, SSM_WIDTH), F32)],
        scratch_shapes=[big(), big(), big(), big(), pltpu.SemaphoreType.DMA((2,))],
        compiler_params=_params("arbitrary", vmem=VMEM_BIG))(
            dy, u, s_re_all, s_im_all, a_re, a_im, bb_re, bb_im, cm_re, cm_im, d_skip)


def _place():
    return lax.axis_index("x"), lax.axis_index("y"), lax.axis_index("c")


def _all_gather_call(blocks, name, direct=False):
    n = len(blocks)

    def body(*refs):
        if direct:
            start, finish = _exchange_phases(refs[:n], refs[n:2 * n], *refs[2 * n:], same_source=True)
            start()
        else:
            start, forward, finish = _gather_phases(refs[:n], refs[n:2 * n], *refs[2 * n:])
            start()
            forward()
        finish()

    return pl.pallas_call(
        body, name=name, in_specs=[ANY_SPEC] * n, out_specs=[ANY_SPEC] * n,
        out_shape=[jax.ShapeDtypeStruct((N_DEV,) + b.shape, b.dtype) for b in blocks],
        scratch_shapes=_comm_sems(n))(*blocks)


def _comm_sems(n):
    return [pltpu.SemaphoreType.DMA((7 * n,)), pltpu.SemaphoreType.DMA((7 * n,)), pltpu.SemaphoreType.DMA((n,))]


def _gather_phases(x_refs, out_refs, send_sems, recv_sems, local_sems):
    x, y, c = _place()
    me, sibling = (x, y, c), (x, y, 1 - c)
    chips = [(1 - x, y), (x, 1 - y), (1 - x, 1 - y)]
    n = len(x_refs)

    def copy(k, a, blk, to, from_input=False):
        slot = out_refs[a].at[4 * blk[0] + 2 * blk[1] + blk[2]]
        return pltpu.make_async_remote_copy(
            src_ref=x_refs[a] if from_input else slot, dst_ref=slot,
            send_sem=send_sems.at[k * n + a], recv_sem=recv_sems.at[k * n + a], device_id=to, device_id_type=MESH_ID)

    mine = [pltpu.make_async_copy(x_refs[a], out_refs[a].at[4 * x + 2 * y + c], local_sems.at[a]) for a in range(n)]
    first, passed = [], []
    for a in range(n):
        first.append(copy(0, a, me, sibling, True))
        first += [copy(1 + j, a, me, (*chip, c), True) for j, chip in enumerate(chips)]
        passed += [copy(4 + j, a, (*chip, c), sibling) for j, chip in enumerate(chips)]

    def start():
        for cp in mine + first:
            cp.start()

    def forward():
        for j, chip in enumerate(chips):
            for a in range(n):
                copy(1 + j, a, (*chip, c), me).wait_recv()
                passed[3 * a + j].start()

    def finish():
        for a in range(n):
            copy(0, a, sibling, me).wait_recv()
            for j, chip in enumerate(chips):
                copy(4 + j, a, (*chip, 1 - c), me).wait_recv()
        for cp in first + passed:
            cp.wait_send()
        for cp in mine:
            cp.wait()

    return start, forward, finish


def _exchange_phases(p_refs, out_refs, send_sems, recv_sems, local_sems, same_source=False):
    x, y, c = _place()
    me = 4 * x + 2 * y + c
    n = len(p_refs)

    def flip(k):
        px = 1 - x if k & 4 else x
        py = 1 - y if k & 2 else y
        pc = 1 - c if k & 1 else c
        return (px, py, pc), 4 * px + 2 * py + pc

    def source(a, slot):
        return p_refs[a] if same_source else p_refs[a].at[slot]

    def copy(k, a, landing):
        peer, peer_slot = flip(k)
        return pltpu.make_async_remote_copy(
            src_ref=source(a, peer_slot), dst_ref=out_refs[a].at[peer_slot if landing else me],
            send_sem=send_sems.at[(k - 1) * n + a], recv_sem=recv_sems.at[(k - 1) * n + a],
            device_id=peer, device_id_type=MESH_ID)

    mine = [pltpu.make_async_copy(source(a, me), out_refs[a].at[me], local_sems.at[a]) for a in range(n)]
    sends = [copy(k, a, False) for k in range(1, N_DEV) for a in range(n)]

    def start():
        for cp in mine + sends:
            cp.start()

    def finish():
        for k in range(1, N_DEV):
            for a in range(n):
                copy(k, a, True).wait_recv()
        for cp in sends:
            cp.wait_send()
        for cp in mine:
            cp.wait()

    return start, finish


def _adam_math(g, w, m, v):
    c1 = 1.0 / (1.0 - ADAM_B1 ** ADAM_STEP)
    c2 = 1.0 / (1.0 - ADAM_B2 ** ADAM_STEP)
    m_new = ADAM_B1 * m + (1.0 - ADAM_B1) * g
    v_new = ADAM_B2 * v + (1.0 - ADAM_B2) * (g * g)
    delta = -ADAM_LR * ((m_new * c1) / (jnp.sqrt(v_new * c2) + ADAM_EPS) + ADAM_WD * w)
    return g, delta, m_new, v_new


def _sum_slices(s_ref):
    g = s_ref[0].astype(F32)
    for k in range(1, N_DEV):
        g = g + s_ref[k].astype(F32)
    return g


def _adam_call(slices, w, m, v, name):
    d1, rest = w.shape[1], w.shape[2:]
    zeros = (0,) * len(rest)
    by_lanes = len(rest) == 1 and d1 > 256 and d1 % 16 != 0
    if by_lanes:
        tile = _fit(rest[0], 256)
        steps = rest[0] // tile
        own = pl.BlockSpec((1, d1, tile), lambda i: (0, 0, i))
        sl = pl.BlockSpec((N_DEV, 1, d1, tile), lambda i: (0, 0, 0, i))
    else:
        tile = _fit(d1, 256, 16) if len(rest) == 1 else _fit(d1, 8, 8)
        steps = d1 // tile
        own = pl.BlockSpec((1, tile) + rest, lambda i: (0, i) + zeros)
        sl = pl.BlockSpec((N_DEV, 1, tile) + rest, lambda i: (0, 0, i) + zeros)

    def body(s_ref, w_ref, m_ref, v_ref, g_ref, d_ref, mo_ref, vo_ref):
        g_ref[...], d_ref[...], mo_ref[...], vo_ref[...] = _adam_math(_sum_slices(s_ref), w_ref[...], m_ref[...], v_ref[...])

    out = jax.ShapeDtypeStruct(w.shape, F32)
    return pl.pallas_call(
        body, name=name, grid=(steps,), in_specs=[sl, own, own, own],
        out_specs=[own, own, own, own], out_shape=[out, out, out, out],
        compiler_params=_params("parallel"))(slices, w, m, v)


def _adam_small_call(rows_all, row_params, slices, params):
    nr, n = len(row_params), len(row_params) + len(params)

    def row_sum(rows_ref, a, width):
        g = rows_ref[0, pl.ds(a, 1), pl.ds(0, width)]
        for k in range(1, N_DEV):
            g = g + rows_ref[k, pl.ds(a, 1), pl.ds(0, width)]
        return g

    def body(rows_ref, *refs):
        slice_refs, wmv, outs = refs[:n - nr], refs[n - nr:n - nr + 3 * n], refs[n - nr + 3 * n:]
        outs[4 * n][...] = row_sum(rows_ref, nr, LANES)
        for a in range(n):
            w_ref, m_ref, v_ref = wmv[3 * a:3 * a + 3]
            if a < nr:
                g = row_sum(rows_ref, a, w_ref.shape[1])
            else:
                g = _sum_slices(slice_refs[a - nr])
            res = _adam_math(g, w_ref[...], m_ref[...], v_ref[...])
            for r in range(4):
                outs[4 * a + r][...] = res[r]

    every = list(row_params) + list(params)
    flat = pl.pallas_call(
        body, name="adam_small",
        out_shape=[jax.ShapeDtypeStruct(w.shape, F32) for w, _, _ in every for _ in range(4)]
        + [jax.ShapeDtypeStruct((1, LANES), F32)],
        compiler_params=pltpu.CompilerParams(vmem_limit_bytes=VMEM_BIG),
    )(rows_all, *slices, *[t for wmv in every for t in wmv])
    return [flat[4 * a:4 * a + 4] for a in range(n)], flat[4 * n][0, 0]


BIG = (("w_in", 1024, 404, 1), ("w_uq", 384, 96, 1), ("w_uk", 256, 64, 1), ("w_uv", 256, 64, 1),
       ("w_glu", 64, 512, 0), ("w_branch_attn", 512, 128, 1), ("w_branch_ssm", 512, 128, 1),
       ("w_out", 128, 1024, 0), ("w_up", 1024, 704, 1), ("w_down", 352, 1024, 0), ("conv_w", 3, 704, 1))
BIG_MIX, BIG_FFN = BIG[:8], BIG[8:]
GRADS_EARLY, GRADS_LATE = BIG[8:] + BIG[4:8], BIG[:4]
SMALL = (("mix_norm_pre", (1024,)), ("q_norm", (384,)), ("kv_norm", (256,)), ("ssm_lambda_re", (32, 64)),
         ("ssm_lambda_im", (32, 64)), ("ssm_log_dt", (32,)), ("ssm_b_re", (32, 64, 16)), ("ssm_b_im", (32, 64, 16)),
         ("ssm_c_re", (32, 16, 64)), ("ssm_c_im", (32, 16, 64)), ("ssm_d", (32, 16)), ("b_glu", (512,)),
         ("b_gate", (2048,)), ("mix_norm_post", (1024,)), ("ffn_norm_pre", (1024,)), ("conv_b", (5632,)),
         ("ffn_norm_post", (1024,)))


TRANSPOSED = ("w_in", "w_uq", "w_uk", "w_uv", "w_up")


STORED_SWAP = {**{name: (1, 2) for name in TRANSPOSED}, "ssm_b_re": (2, 3), "ssm_b_im": (2, 3), "ssm_d": (1, 2)}


def _stored(name, arr):
    return jnp.swapaxes(arr, *STORED_SWAP[name]) if name in STORED_SWAP else arr


def _to_slices(name, full, rows, cols, axis):
    if name in TRANSPOSED:
        return full.T.reshape(N_DEV, cols, rows)
    if axis == 1:
        return full.reshape(rows, N_DEV, cols).transpose(1, 0, 2)
    return full.reshape(N_DEV, rows, cols)


def _from_slices(name, parts, rows, cols, axis):
    if name in TRANSPOSED:
        return parts.reshape(N_DEV * cols, rows)
    if axis == 1:
        return parts.transpose(1, 0, 2).reshape(rows, N_DEV * cols)
    return parts.reshape(N_DEV * rows, cols)


def _head_unpad_cols(w, width):
    k = w.shape[0]
    return w.reshape(k, N_HEADS, LANES)[:, :, :width].reshape(k, N_HEADS * width)


def _time_perm(a, L):
    return a.reshape(8, L // 8, a.shape[-1]).transpose(1, 0, 2).reshape(L, a.shape[-1])


def _time_unperm(a, L):
    return a.reshape(L // 8, 8, a.shape[-1]).transpose(1, 0, 2).reshape(L, a.shape[-1])


def _block_diag(w, rows_first):
    eye = jnp.eye(8, dtype=w.dtype)
    g = w.reshape(SSM_CHUNKS, 8, w.shape[1], w.shape[2])
    return jnp.einsum("qgrc,gk->qgrkc", g, eye).reshape(SSM_CHUNKS, 8 * w.shape[1], 8 * w.shape[2])


def _block_diag_t(m, r, c):
    eye = jnp.eye(8, dtype=m.dtype)
    return jnp.einsum("qgrkc,gk->qgrc", m.reshape(SSM_CHUNKS, 8, r, 8, c), eye).reshape(SSM_GROUPS, r, c)


def kernel(x, positions, mix_norm_pre, w_in, q_norm, w_uq, kv_norm, w_uk, w_uv, ssm_lambda_re, ssm_lambda_im, ssm_log_dt, ssm_b_re, ssm_b_im, ssm_c_re, ssm_c_im, ssm_d, w_glu, b_glu, w_branch_attn, w_branch_ssm, b_gate, w_out, mix_norm_post, ffn_norm_pre, w_up, conv_w, conv_b, w_down, ffn_norm_post, loss_target, m_mix_norm_pre, m_w_in, m_q_norm, m_w_uq, m_kv_norm, m_w_uk, m_w_uv, m_ssm_lambda_re, m_ssm_lambda_im, m_ssm_log_dt, m_ssm_b_re, m_ssm_b_im, m_ssm_c_re, m_ssm_c_im, m_ssm_d, m_w_glu, m_b_glu, m_w_branch_attn, m_w_branch_ssm, m_b_gate, m_w_out, m_mix_norm_post, m_ffn_norm_pre, m_w_up, m_conv_w, m_conv_b, m_w_down, m_ffn_norm_post, v_mix_norm_pre, v_w_in, v_q_norm, v_w_uq, v_kv_norm, v_w_uk, v_w_uv, v_ssm_lambda_re, v_ssm_lambda_im, v_ssm_log_dt, v_ssm_b_re, v_ssm_b_im, v_ssm_c_re, v_ssm_c_im, v_ssm_d, v_w_glu, v_b_glu, v_w_branch_attn, v_w_branch_ssm, v_b_gate, v_w_out, v_mix_norm_post, v_ffn_norm_pre, v_w_up, v_conv_w, v_conv_b, v_w_down, v_ffn_norm_post):
    given = dict(locals())
    L = x.shape[1]
    xs = x[0]
    target = loss_target[0]

    def shard_bits(group):
        return [given[name][0] if name == "conv_w" else _stored(name, given[name])[0].astype(BF16) for name, _, _, _ in group]

    W = {}

    def unpack_weights(gathered, group):
        for (name, rows, cols, axis), parts in zip(group, gathered):
            W[name] = _from_slices(name, parts, rows, cols, axis)

    unpack_weights(_all_gather_call(shard_bits(BIG_MIX[:1]), "gather_w_in"), BIG_MIX[:1])

    wit = W["w_in"]
    zero_rows = lambda r: jnp.zeros((r, D_MODEL), BF16)
    kr_end = P_KR + QK_ROPE
    w_in_pt = jnp.concatenate(
        [wit[:P_KR], zero_rows(QK_NOPE), wit[P_KR:kr_end], zero_rows(LANES - QK_HEAD), wit[kr_end:]], axis=0)

    hn1 = _rms_fwd_call(xs, mix_norm_pre, "rms_pre")
    proj, *gathered_mix = _mm(hn1, w_in_pt, "mm_in", tb=True, tn=1664, gather=shard_bits(BIG_MIX[1:]))
    unpack_weights(gathered_mix, BIG_MIX[1:])
    head_rows = lambda wt, width: jnp.pad(wt.reshape(N_HEADS, width, wt.shape[1]), ((0, 0), (0, LANES - width), (0, 0)))
    w_uq_pt = head_rows(W["w_uq"], QK_HEAD).reshape(HEAD_PAD, Q_RANK)
    w_kv_pt = jnp.stack([head_rows(W["w_uk"], QK_NOPE), head_rows(W["w_uv"], V_HEAD)], axis=1
                        ).reshape(2 * HEAD_PAD, KV_RANK)
    w_ba_p = jnp.pad(W["w_branch_attn"].reshape(N_HEADS, V_HEAD, D_MODEL), ((0, 0), (0, LANES - V_HEAD), (0, 0))
                     ).reshape(HEAD_PAD, D_MODEL)
    half = jnp.arange(QK_ROPE // 2, dtype=F32)
    inv_freq = ROPE_THETA ** (-2.0 * half / QK_ROPE)
    inv_freq = jnp.pad(jnp.concatenate([inv_freq, inv_freq]), (QK_NOPE, LANES - QK_HEAD)).reshape(1, LANES)
    pos_col = positions.astype(F32).reshape(L, 1)
    qn, ckvn, q_r, kv_r, cosf, sinf = _mla_proj_call(proj, q_norm, kv_norm, w_uq_pt, w_kv_pt, pos_col, inv_freq)
    attn, lse, *gathered_ffn = _attn_fwd_call(q_r, kv_r, shard_bits(BIG_FFN))
    unpack_weights(gathered_ffn, BIG_FFN)

    col = lambda a: a.reshape(SSM_NSTATE, -1)
    lr_c, li_c = col(ssm_lambda_re[0]), col(ssm_lambda_im[0])
    ldt_c = col(jnp.broadcast_to(ssm_log_dt[0][:, None], (SSM_GROUPS, SSM_STATE)))
    br_c, bi_c = col(ssm_b_re[0]), col(ssm_b_im[0])
    a_re_c, a_im_c, bb_re_c, bb_im_c = _disc_call(lr_c, li_c, ldt_c, br_c, bi_c)
    a_re, a_im = a_re_c.reshape(1, SSM_NSTATE), a_im_c.reshape(1, SSM_NSTATE)
    to_bb = lambda b: _block_diag(b.reshape(SSM_GROUPS, SSM_STATE, SSM_GROUP).transpose(0, 2, 1), True).astype(BF16)
    bb_re, bb_im = to_bb(bb_re_c), to_bb(bb_im_c)
    to_cm = lambda c_: _block_diag(c_[0].transpose(0, 2, 1), True).astype(BF16)
    cm_re, cm_im = to_cm(ssm_c_re), to_cm(ssm_c_im)
    d_skip = ssm_d.reshape(1, SSM_WIDTH)
    u_p = _time_perm(proj[:, P_U:P_GATE], L)
    y1, s_re, s_im = _ssm_fwd_call(u_p, a_re, a_im, bb_re, bb_im, cm_re, cm_im, d_skip)
    w_glu_b = W["w_glu"]
    ssm_p = _glu_call(y1, w_glu_b, b_glu)
    ssm = _time_unperm(ssm_p, L)

    pa = _mm(attn, w_ba_p, "mm_ba")
    ps = _mm(ssm, W["w_branch_ssm"], "mm_bs")
    merged = _merge_call(proj, b_gate, pa, ps)
    wide = lambda dt: (D_MODEL, dt)
    o, x2, hn2 = _mm_rows(merged, W["w_out"], "mm_out", _post_mix_rows, [xs], [mix_norm_post, ffn_norm_pre],
                          [wide(F32), wide(F32), wide(BF16)], [])
    h = _mm(hn2, W["w_up"], "mm_up", tb=True, tn=1408)
    cw = W["conv_w"]
    act = _conv_act_call(h, cw, conv_b)
    dy, dff, loss_row, g_ffn_norm_post = _mm_rows(
        act, W["w_down"], "mm_down", _ffn_out_rows, [x2, target], [ffn_norm_post], [wide(F32), wide(BF16)],
        [LANES, D_MODEL], tk=1408)

    da = _mm(dff, W["w_down"], "mm_down_dx", tb=True, tn=1408)
    g_w_down = _mm_tn(act, dff, "mm_down_dw", tm=1408)
    dgate, dval, dcw_g, dcw_v, dcb_g, dcb_v = _conv_act_bwd_call(da, h, cw, conv_b)
    g_conv_w = jnp.concatenate([dcw_g, dcw_v], axis=1)
    g_conv_b = jnp.concatenate([dcb_g, dcb_v], axis=1)
    dh = _conv_t_call(dgate, dval, cw)
    dx2, do, g_ffn_norm_pre, g_mix_norm_post = _mm_rows(
        dh, W["w_up"], "mm_up_dx", _post_bwd_rows, [x2, dy, o], [ffn_norm_pre, mix_norm_post], [wide(F32), wide(BF16)],
        [D_MODEL, D_MODEL], tk=1408)
    g_w_up = _mm_tn(hn2, dh, "mm_up_dw")
    dmerged = _mm(do, W["w_out"], "mm_out_dx", tb=True)
    g_w_out = _mm_tn(merged, do, "mm_out_dw")
    dpa, dps, dl0, dl1, db0, db1 = _merge_bwd_call(dmerged, proj, b_gate, pa, ps)
    g_b_gate = jnp.concatenate([db0, db1], axis=1)
    dattn = _mm(dpa, w_ba_p, "mm_ba_dx", tb=True, out_dtype=BF16)
    g_w_ba = _mm_tn(attn, dpa, "mm_ba_dw").reshape(N_HEADS, LANES, D_MODEL)[:, :V_HEAD].reshape(N_HEADS * V_HEAD, D_MODEL)
    dssm = _mm(dps, W["w_branch_ssm"], "mm_bs_dx", tb=True)
    g_w_bs = _mm_tn(ssm, dps, "mm_bs_dw")

    dy1, g_w_glu, g_b_glu = _glu_bwd_call(_time_perm(dssm, L), y1, w_glu_b, b_glu)
    du_p, dbb_re, dbb_im, dcm_re, dcm_im, da_re, da_im, g_ssm_d = _ssm_bwd_call(
        dy1, u_p, s_re, s_im, a_re, a_im, bb_re, bb_im, cm_re, cm_im, d_skip)
    du = _time_unperm(du_p, L)
    from_bb = lambda m: col(_block_diag_t(m, SSM_GROUP, SSM_STATE).transpose(0, 2, 1))
    dlr, dli, dldt, dbr, dbi = _disc_bwd_call(
        lr_c, li_c, ldt_c, br_c, bi_c, da_re.reshape(SSM_NSTATE, 1), da_im.reshape(SSM_NSTATE, 1), from_bb(dbb_re), from_bb(dbb_im))
    g_c_re = _block_diag_t(dcm_re, SSM_STATE, SSM_GROUP).transpose(0, 2, 1)
    g_c_im = _block_diag_t(dcm_im, SSM_STATE, SSM_GROUP).transpose(0, 2, 1)

    def grad_slices(group, grads):
        return [_to_slices(name, grads[name], rows, cols, axis) for name, rows, cols, axis in group]

    early_grads = {"w_up": g_w_up, "w_down": g_w_down, "conv_w": g_conv_w, "w_glu": g_w_glu.astype(BF16),
                   "w_branch_attn": g_w_ba, "w_branch_ssm": g_w_bs, "w_out": g_w_out}
    b_stored = lambda d: d.reshape(SSM_GROUPS, SSM_STATE, SSM_GROUP).transpose(0, 2, 1)
    per_state = lambda d: d.reshape(SSM_GROUPS, SSM_STATE)
    ssm_partials = {"ssm_lambda_re": per_state(dlr), "ssm_lambda_im": per_state(dli),
                    "ssm_b_re": b_stored(dbr), "ssm_b_im": b_stored(dbi),
                    "ssm_c_re": g_c_re, "ssm_c_im": g_c_im, "ssm_d": g_ssm_d.reshape(SSM_GROUPS, SSM_GROUP).T}
    ssm_shapes = [(name, ssm_partials[name].shape) for name, _ in SMALL if name in ssm_partials]
    dq, dkv, *landed = _attn_bwd_call(
        q_r, kv_r, attn, dattn, lse, grad_slices(GRADS_EARLY, early_grads),
        [ssm_partials[name].reshape(-1, LANES) if len(shp) == 3 else ssm_partials[name].reshape((1,) + shp)
         for name, shp in ssm_shapes])
    received_early = landed[:len(GRADS_EARLY)]
    ssm_all = {name: got.reshape((N_DEV, 1) + shp) for (name, shp), got in zip(ssm_shapes, landed[len(GRADS_EARLY):])}
    dq_p, dkv_p, dlatent, g_q_norm, g_kv_norm = _mla_proj_bwd_call(
        dq, dkv, cosf, sinf, proj, q_norm, kv_norm, w_uq_pt, w_kv_pt)
    g_w_uq = _head_unpad_cols(_mm_tn(qn, dq_p, "mm_uq_dw"), QK_HEAD)
    g_w_kv = _mm_tn(ckvn, dkv_p, "mm_ukv_dw").reshape(KV_RANK, N_HEADS, 2, LANES)
    g_w_uk = g_w_kv[:, :, 0, :QK_NOPE].reshape(KV_RANK, N_HEADS * QK_NOPE)
    g_w_uv = g_w_kv[:, :, 1, :V_HEAD].reshape(KV_RANK, N_HEADS * V_HEAD)
    dproj = jnp.concatenate([dlatent, du.astype(BF16), dl0, dl1], axis=1)
    g_w_in_p = _mm_tn(hn1, dproj, "mm_in_dw", tk=1024)
    g_w_in = jnp.concatenate([g_w_in_p[:, :P_KR], g_w_in_p[:, P_KR + QK_NOPE:P_KR + QK_HEAD], g_w_in_p[:, P_U:]], axis=1)
    late_grads = {"w_in": g_w_in, "w_uq": g_w_uq, "w_uk": g_w_uk, "w_uv": g_w_uv}
    grad_x, g_mix_norm_pre, *received_late = _mm_in_dx_call(
        dproj, w_in_pt, xs, dx2, mix_norm_pre, grad_slices(GRADS_LATE, late_grads))

    results = {}
    wmv = lambda name: tuple(_stored(name, given[prefix + name]) for prefix in ("", "m_", "v_"))
    unstored = lambda name, res: [_stored(name, r) for r in res]
    whole = ("w_uq", "w_uk", "w_uv", "w_glu", "w_branch_attn", "w_branch_ssm", "conv_w")
    landed_small = dict(ssm_all)
    for group, received in ((GRADS_EARLY, received_early), (GRADS_LATE, received_late)):
        for (name, _, _, _), rec in zip(group, received):
            if name in whole:
                landed_small[name] = rec[:, None]
            else:
                results[name] = unstored(name, _adam_call(rec[:, None], *wmv(name), "adam_" + name))

    vec_grads = {"mix_norm_pre": g_mix_norm_pre, "q_norm": g_q_norm, "kv_norm": g_kv_norm,
                 "ssm_log_dt": jnp.sum(dldt.reshape(SSM_GROUPS, SSM_STATE), axis=1),
                 "b_glu": g_b_glu, "b_gate": g_b_gate, "mix_norm_post": g_mix_norm_post,
                 "ffn_norm_pre": g_ffn_norm_pre, "ffn_norm_post": g_ffn_norm_post}
    vec_names = [name for name, _ in SMALL if name in vec_grads]
    width = max(shp[0] for name, shp in SMALL if name in vec_grads)
    rows = [jnp.pad(vec_grads[name].reshape(1, -1), ((0, 0), (0, width - vec_grads[name].size))) for name in vec_names]
    rows.append(jnp.pad(loss_row, ((0, 0), (0, width - LANES))))
    rows.append(jnp.zeros((-len(rows) % 8, width), F32))
    rows_all, landed_small["conv_b"] = _all_gather_call(
        [jnp.concatenate(rows, axis=0), g_conv_b], "gather_small_grads", direct=True)
    others = ["conv_b", "ssm_lambda_re", "ssm_lambda_im", "ssm_d"] + list(whole)
    small_results, loss = _adam_small_call(
        rows_all, [wmv(n) for n in vec_names], [landed_small[n] for n in others], [wmv(n) for n in others])
    for name, res in zip(vec_names + others, small_results):
        results[name] = unstored(name, res)
    for name in ("ssm_b_re", "ssm_b_im", "ssm_c_re", "ssm_c_im"):
        results[name] = unstored(name, _adam_call(landed_small[name], *wmv(name), "adam_" + name))

    order = ["mix_norm_pre", "w_in", "q_norm", "w_uq", "kv_norm", "w_uk", "w_uv", "ssm_lambda_re", "ssm_lambda_im",
             "ssm_log_dt", "ssm_b_re", "ssm_b_im", "ssm_c_re", "ssm_c_im", "ssm_d", "w_glu", "b_glu", "w_branch_attn",
             "w_branch_ssm", "b_gate", "w_out", "mix_norm_post", "ffn_norm_pre", "w_up", "conv_w", "conv_b", "w_down",
             "ffn_norm_post"]
    outs = [loss, grad_x[None]]
    for kind in range(4):
        outs += [results[name][kind] for name in order]
    return tuple(outs)
```

```python
import math

import jax
import jax.numpy as jnp
from jax import lax
from jax.experimental import pallas as pl
from jax.experimental.pallas import tpu as pltpu

F32 = jnp.float32
BF16 = jnp.bfloat16
MESH_ID = pl.DeviceIdType.MESH

N_DEV = 8
LANES = 128
D_MODEL = 1024
N_HEADS = 8
QK_NOPE = 64
QK_ROPE = 32
QK_HEAD = QK_NOPE + QK_ROPE
V_HEAD = 64
Q_RANK = 384
KV_RANK = 256
ROPE_THETA = 10000.0
SSM_WIDTH = 512
SSM_GROUP = 16
SSM_GROUPS = 32
SSM_STATE = 64
SSM_NSTATE = SSM_GROUPS * SSM_STATE
SSM_CHUNKS = 4
D_FF = 2816
EPS = 1e-6
ADAM_LR, ADAM_B1, ADAM_B2, ADAM_EPS, ADAM_WD, ADAM_STEP = 0.001, 0.9, 0.999, 1e-08, 0.01, 10

P_CQ, P_CKV, P_KR, P_U, P_GATE = 0, 384, 640, 768, 1280
HEAD_PAD = N_HEADS * LANES

VMEM_BIG = 52 * 1024 * 1024

_GELU_C0 = math.sqrt(2.0 / math.pi)
_GELU_C1 = 0.044715
NEG = -1e30


def _fit(n, pref, mult=LANES):
    if n <= pref:
        return n
    t = (pref // mult) * mult
    while t > 0 and n % t:
        t -= mult
    assert t > 0, (n, pref, mult)
    return t


def _gelu(x):
    return x * (0.5 * (1.0 + jnp.tanh(_GELU_C0 * x * (1.0 + _GELU_C1 * (x * x)))))


def _gelu_and_grad(x):
    x2 = x * x
    t = jnp.tanh(_GELU_C0 * x * (1.0 + _GELU_C1 * x2))
    half = 0.5 * (1.0 + t)
    return x * half, half + 0.5 * x * (1.0 - t * t) * _GELU_C0 * (1.0 + 3.0 * _GELU_C1 * x2)


def _sigmoid(x):
    return 1.0 / (1.0 + jnp.exp(-x))


def _dot(a, b, dims):
    return lax.dot_general(a, b, (dims, ((), ())), preferred_element_type=F32)


NN = ((1,), (0,))
NT = ((1,), (1,))
TN = ((0,), (0,))


def _params(*sem, vmem=None):
    return pltpu.CompilerParams(dimension_semantics=tuple(sem), vmem_limit_bytes=vmem)


def _mm(a, b, name, tb=False, out_dtype=F32, tm=1024, tn=1024, tk=1024, gather=()):
    M, K = a.shape
    if tb:
        N, K2 = b.shape
    else:
        K2, N = b.shape
    assert K == K2, (a.shape, b.shape, tb)
    tm, tn, tk = _fit(M, tm), _fit(N, tn), _fit(K, tk)
    nk = K // tk
    grid = (M // tm, N // tn, nk)
    steps = grid[0] * grid[1] * grid[2]
    dims = NT if tb else NN
    n = len(gather)

    def body(a_ref, b_ref, *refs):
        o_ref, scratch = refs[n], refs[2 * n + 1:]
        step = (pl.program_id(0) * grid[1] + pl.program_id(1)) * grid[2] + pl.program_id(2)
        if n:
            start, forward, finish = _gather_phases(refs[:n], refs[n + 1:2 * n + 1], *scratch[-3:])
            pl.when(step == 0)(start)
            pl.when(step == steps // 2)(forward)
        part = _dot(a_ref[...].astype(BF16), b_ref[...].astype(BF16), dims)
        if nk == 1:
            o_ref[...] = part.astype(out_dtype)
        else:
            acc_ref = scratch[0]
            k = pl.program_id(2)

            @pl.when(k == 0)
            def _():
                acc_ref[...] = part

            @pl.when(k > 0)
            def _():
                acc_ref[...] += part

            @pl.when(k == nk - 1)
            def _():
                o_ref[...] = acc_ref[...].astype(out_dtype)
        if n:
            pl.when(step == steps - 1)(finish)

    a_spec = pl.BlockSpec((tm, tk), lambda i, j, k: (i, k))
    b_spec = pl.BlockSpec((tn, tk), lambda i, j, k: (j, k)) if tb else pl.BlockSpec((tk, tn), lambda i, j, k: (k, j))
    landed = [jax.ShapeDtypeStruct((N_DEV,) + p.shape, p.dtype) for p in gather]
    out = pl.pallas_call(
        body, name=name, grid=grid,
        in_specs=[a_spec, b_spec] + [ANY_SPEC] * n,
        out_specs=[pl.BlockSpec((tm, tn), lambda i, j, k: (i, j))] + [ANY_SPEC] * n,
        out_shape=[jax.ShapeDtypeStruct((M, N), out_dtype)] + landed,
        scratch_shapes=([] if nk == 1 else [pltpu.VMEM((tm, tn), F32)]) + (_comm_sems(n) if n else []),
        compiler_params=_params(*(("arbitrary",) * 3 if n else ("parallel", "parallel", "arbitrary")), vmem=VMEM_BIG),
    )(a, b, *gather)
    return out if n else out[0]


def _mm_rows(a, b, name, epilogue, rows_in, vecs_in, rows_out, vecs_out, tb=False, tk=1024):
    M, K = a.shape
    N = b.shape[0] if tb else b.shape[1]
    tm, tk = _fit(M, 512), _fit(K, tk)
    nk = K // tk
    nr, nv, nro = len(rows_in), len(vecs_in), len(rows_out)

    def body(a_ref, b_ref, *refs):
        ins, outs, acc_ref = refs[:nr + nv], refs[nr + nv:nr + nv + nro + len(vecs_out)], refs[-1]
        i, k = pl.program_id(0), pl.program_id(1)
        part = _dot(a_ref[...], b_ref[...], NT if tb else NN)

        def finish(product):
            res = epilogue(product, *[r[...] for r in ins])
            for ref, val in zip(outs[:nro], res[:nro]):
                ref[...] = val.astype(ref.dtype)
            for ref, val in zip(outs[nro:], res[nro:]):
                _acc(ref, i == 0, val)

        if nk == 1:
            finish(part)
        else:
            @pl.when(k == 0)
            def _():
                acc_ref[...] = part

            @pl.when(jnp.logical_and(k > 0, k < nk - 1))
            def _():
                acc_ref[...] += part

            @pl.when(k == nk - 1)
            def _():
                finish(acc_ref[...] + part)

    row = lambda w: pl.BlockSpec((tm, w), lambda i, k: (i, 0))
    vec = lambda w: pl.BlockSpec((1, w), lambda i, k: (0, 0))
    b_spec = pl.BlockSpec((N, tk), lambda i, k: (0, k)) if tb else pl.BlockSpec((tk, N), lambda i, k: (k, 0))
    return pl.pallas_call(
        body, name=name, grid=(M // tm, nk),
        in_specs=[pl.BlockSpec((tm, tk), lambda i, k: (i, k)), b_spec] + [row(r.shape[1]) for r in rows_in]
        + [vec(v.shape[1]) for v in vecs_in],
        out_specs=[row(w) for w, _ in rows_out] + [vec(w) for w in vecs_out],
        out_shape=[jax.ShapeDtypeStruct((M, w), dt) for w, dt in rows_out] + [jax.ShapeDtypeStruct((1, w), F32) for w in vecs_out],
        scratch_shapes=[pltpu.VMEM((tm, N), F32)],
        compiler_params=_params("arbitrary", "arbitrary", vmem=VMEM_BIG))(a, b, *rows_in, *vecs_in)


def _mm_in_dx_call(dproj, w_in_pt, x, dx2, g_pre, exchange):
    L, K = dproj.shape
    N = w_in_pt.shape[1]
    tm, tk = _fit(L, 512), _fit(K, 1664)
    nm, nk = L // tm, K // tk
    n = len(exchange)

    def body(a_ref, b_ref, x_ref, dx2_ref, g_ref, *refs):
        parts, (gx_ref, dg_ref), got = refs[:n], refs[n:n + 2], refs[n + 2:2 * n + 2]
        acc_ref = refs[2 * n + 2]
        i, k = pl.program_id(0), pl.program_id(1)
        start, finish = _exchange_phases(parts, got, *refs[2 * n + 3:])
        pl.when(jnp.logical_and(i == 0, k == 0))(start)
        part = _dot(a_ref[...], b_ref[...], NN)

        @pl.when(k == 0)
        def _():
            acc_ref[...] = part

        @pl.when(jnp.logical_and(k > 0, k < nk - 1))
        def _():
            acc_ref[...] += part

        @pl.when(k == nk - 1)
        def _():
            d1, dg = _rms_bwd(x_ref[...], g_ref[...], acc_ref[...] + part)
            gx_ref[...] = dx2_ref[...] + d1
            _acc(dg_ref, i == 0, dg)

        pl.when(jnp.logical_and(i == nm - 1, k == nk - 1))(finish)

    assert nk >= 2
    rows = lambda: pl.BlockSpec((tm, N), lambda i, k: (i, 0))
    return pl.pallas_call(
        body, name="mm_in_dx", grid=(nm, nk),
        in_specs=[pl.BlockSpec((tm, tk), lambda i, k: (i, k)), pl.BlockSpec((tk, N), lambda i, k: (k, 0)),
                  rows(), rows(), pl.BlockSpec((1, N), lambda i, k: (0, 0))] + [ANY_SPEC] * n,
        out_specs=[rows(), pl.BlockSpec((1, N), lambda i, k: (0, 0))] + [ANY_SPEC] * n,
        out_shape=[jax.ShapeDtypeStruct((L, N), F32), jax.ShapeDtypeStruct((1, N), F32)]
        + [jax.ShapeDtypeStruct(p.shape, p.dtype) for p in exchange],
        scratch_shapes=[pltpu.VMEM((tm, N), F32)] + _comm_sems(n),
        compiler_params=_params("arbitrary", "arbitrary", vmem=VMEM_BIG))(dproj, w_in_pt, x, dx2, g_pre, *exchange)


TN_CHUNK = 512


def _mm_tn(a, b, name, tm=512, tk=512):
    K, M = a.shape
    K2, N = b.shape
    assert K == K2, (a.shape, b.shape)
    tm, tk, cn = _fit(M, tm), _fit(K, tk), _fit(N, TN_CHUNK)
    nk = K // tk

    def body(a_ref, b_ref, o_ref, acc_ref):
        k = pl.program_id(1)

        @pl.when(k == 0)
        def _():
            acc_ref[...] = jnp.zeros((tm, N), F32)

        at = a_ref[...].astype(BF16).T
        for c in range(N // cn):
            cols = slice(c * cn, (c + 1) * cn)
            acc_ref[:, cols] += _dot(at, b_ref[:, cols].astype(BF16), NN)

        @pl.when(k == nk - 1)
        def _():
            o_ref[...] = acc_ref[...].astype(BF16)

    return pl.pallas_call(
        body, name=name, grid=(M // tm, nk),
        in_specs=[pl.BlockSpec((tk, tm), lambda i, k: (k, i)), pl.BlockSpec((tk, N), lambda i, k: (k, 0))],
        out_specs=pl.BlockSpec((tm, N), lambda i, k: (i, 0)),
        out_shape=jax.ShapeDtypeStruct((M, N), BF16),
        scratch_shapes=[pltpu.VMEM((tm, N), F32)],
        compiler_params=_params("parallel", "arbitrary", vmem=VMEM_BIG))(a, b)


def _row(tl, n, col=0):
    return pl.BlockSpec((tl, n), lambda i: (i, col))


def _full(shape):
    return pl.BlockSpec(shape, lambda i: (0,) * len(shape))


def _rms(x, g):
    r = lax.rsqrt(jnp.mean(x * x, axis=-1, keepdims=True) + EPS)
    return x * r * g


def _rms_bwd(x, g, dy):
    n = x.shape[-1]
    r = lax.rsqrt(jnp.mean(x * x, axis=-1, keepdims=True) + EPS)
    gy = dy * g
    dx = r * gy - x * (r * r * r * (1.0 / n)) * jnp.sum(x * gy, axis=-1, keepdims=True)
    return dx, jnp.sum(dy * x * r, axis=0, keepdims=True)


def _acc(ref, first, val):
    @pl.when(first)
    def _():
        ref[...] = val

    @pl.when(jnp.logical_not(first))
    def _():
        ref[...] += val


def _rms_fwd_call(x, g, name):
    L, n = x.shape
    tl = _fit(L, 512)

    def body(x_ref, g_ref, o_ref):
        o_ref[...] = _rms(x_ref[...], g_ref[...]).astype(BF16)

    return pl.pallas_call(
        body, name=name, grid=(L // tl,), in_specs=[_row(tl, n), _full((1, n))], out_specs=_row(tl, n),
        out_shape=jax.ShapeDtypeStruct((L, n), BF16), compiler_params=_params("parallel"))(x, g)


def _rope_lanes(shape):
    lane = lax.broadcasted_iota(jnp.int32, shape, 1)
    return lane, jnp.logical_and(lane >= QK_NOPE, lane < QK_HEAD)


def _rope_apply(x, cosf, sinf, lane):
    rot = jnp.where(lane < QK_NOPE + QK_ROPE // 2, -pltpu.roll(x, LANES - QK_ROPE // 2, 1), pltpu.roll(x, QK_ROPE // 2, 1))
    return x * cosf + rot * sinf


def _rope_apply_t(dy, cosf, sinf, lane, is_rope):
    g = dy * sinf
    rot_t = jnp.where(lane < QK_NOPE + QK_ROPE // 2, pltpu.roll(g, LANES - QK_ROPE // 2, 1), -pltpu.roll(g, QK_ROPE // 2, 1))
    return dy * cosf + jnp.where(is_rope, rot_t, 0.0)


def _mla_proj_call(proj, q_norm, kv_norm, w_uq_pt, w_kv_pt, pos_col, inv_freq):
    L = proj.shape[0]
    tl = _fit(L, 512)

    def body(p_ref, gq_ref, gk_ref, wq_ref, wkv_ref, pos_ref, f_ref, qn_ref, kn_ref, qo_ref, kvo_ref, cos_ref, sin_ref):
        qn = _rms(p_ref[:, P_CQ:P_CKV], gq_ref[...]).astype(BF16)
        kn = _rms(p_ref[:, P_CKV:P_KR], gk_ref[...]).astype(BF16)
        qn_ref[...] = qn
        kn_ref[...] = kn
        q_pad = _dot(qn, wq_ref[...], NT)
        kv_pad = _dot(kn, wkv_ref[...], NT)
        lane, is_rope = _rope_lanes((tl, LANES))
        ang = pos_ref[...] * f_ref[...]
        cosf = jnp.where(is_rope, jnp.cos(ang), jnp.where(lane < QK_NOPE, 1.0, 0.0))
        sinf = jnp.where(is_rope, jnp.sin(ang), 0.0)
        cos_ref[...] = cosf
        sin_ref[...] = sinf
        kr = _rope_apply(p_ref[:, P_KR:P_U], cosf, sinf, lane)
        for h in range(N_HEADS):
            qh = _rope_apply(q_pad[:, h * LANES:(h + 1) * LANES], cosf, sinf, lane)
            qo_ref[:, h * LANES:(h + 1) * LANES] = (qh * Q_PRESCALE).astype(BF16)
            kvo_ref[:, 2 * h * LANES:(2 * h + 1) * LANES] = (kv_pad[:, 2 * h * LANES:(2 * h + 1) * LANES] + kr).astype(BF16)
            vh = jnp.where(lane == V_HEAD, 1.0, kv_pad[:, (2 * h + 1) * LANES:(2 * h + 2) * LANES])
            kvo_ref[:, (2 * h + 1) * LANES:(2 * h + 2) * LANES] = vh.astype(BF16)

    shape = lambda n, dt: jax.ShapeDtypeStruct((L, n), dt)
    return pl.pallas_call(
        body, name="mla_proj", grid=(L // tl,),
        in_specs=[_row(tl, P_U), _full((1, Q_RANK)), _full((1, KV_RANK)), _full((HEAD_PAD, Q_RANK)),
                  _full((2 * HEAD_PAD, KV_RANK)), _row(tl, 1), _full((1, LANES))],
        out_specs=[_row(tl, Q_RANK), _row(tl, KV_RANK), _row(tl, HEAD_PAD), _row(tl, 2 * HEAD_PAD), _row(tl, LANES), _row(tl, LANES)],
        out_shape=[shape(Q_RANK, BF16), shape(KV_RANK, BF16), shape(HEAD_PAD, BF16), shape(2 * HEAD_PAD, BF16),
                   shape(LANES, F32), shape(LANES, F32)],
        compiler_params=_params("parallel"))(proj, q_norm, kv_norm, w_uq_pt, w_kv_pt, pos_col, inv_freq)


def _mla_proj_bwd_call(dq, dkv, cosf, sinf, proj, q_norm, kv_norm, w_uq_pt, w_kv_pt):
    L = dq.shape[0]
    tl = _fit(L, 512)

    def body(dq_ref, dkv_ref, cos_ref, sin_ref, p_ref, gq_ref, gk_ref, wq_ref, wkv_ref,
             dqo_ref, dkvo_ref, d_ref, dgq_ref, dgk_ref):
        first = pl.program_id(0) == 0
        lane, is_rope = _rope_lanes((tl, LANES))
        cosf, sinf = cos_ref[...], sin_ref[...]
        dk_sum = jnp.zeros((tl, LANES), F32)
        for h in range(N_HEADS):
            dqo_ref[:, h * LANES:(h + 1) * LANES] = _rope_apply_t(dq_ref[:, h * LANES:(h + 1) * LANES], cosf, sinf, lane, is_rope).astype(BF16)
            dk_sum = dk_sum + dkv_ref[:, 2 * h * LANES:(2 * h + 1) * LANES]
        dkvo_ref[...] = dkv_ref[...].astype(BF16)
        dqn = _dot(dqo_ref[...], wq_ref[...], NN)
        dkn = _dot(dkvo_ref[...], wkv_ref[...], NN)
        dcq, dgq = _rms_bwd(p_ref[:, P_CQ:P_CKV], gq_ref[...], dqn)
        dckv, dgk = _rms_bwd(p_ref[:, P_CKV:P_KR], gk_ref[...], dkn)
        d_ref[:, P_CQ:P_CKV] = dcq.astype(BF16)
        d_ref[:, P_CKV:P_KR] = dckv.astype(BF16)
        d_ref[:, P_KR:P_U] = _rope_apply_t(dk_sum, cosf, sinf, lane, is_rope).astype(BF16)
        _acc(dgq_ref, first, dgq)
        _acc(dgk_ref, first, dgk)

    shape = lambda n: jax.ShapeDtypeStruct((L, n), BF16)
    return pl.pallas_call(
        body, name="mla_proj_bwd", grid=(L // tl,),
        in_specs=[_row(tl, HEAD_PAD), _row(tl, 2 * HEAD_PAD), _row(tl, LANES), _row(tl, LANES), _row(tl, P_KR),
                  _full((1, Q_RANK)), _full((1, KV_RANK)), _full((HEAD_PAD, Q_RANK)), _full((2 * HEAD_PAD, KV_RANK))],
        out_specs=[_row(tl, HEAD_PAD), _row(tl, 2 * HEAD_PAD), _row(tl, P_U), _full((1, Q_RANK)), _full((1, KV_RANK))],
        out_shape=[shape(HEAD_PAD), shape(2 * HEAD_PAD), shape(P_U), jax.ShapeDtypeStruct((1, Q_RANK), F32),
                   jax.ShapeDtypeStruct((1, KV_RANK), F32)],
        compiler_params=_params("arbitrary"))(dq, dkv, cosf, sinf, proj, q_norm, kv_norm, w_uq_pt, w_kv_pt)


GATE_TILE = 256
GATE_ROWS = 1024


def _merge_call(proj, b_gate, pa, ps):
    L = proj.shape[0]
    tl = _fit(L, GATE_ROWS)
    nc = D_MODEL // GATE_TILE
    g0, g1 = P_GATE // GATE_TILE, (P_GATE + D_MODEL) // GATE_TILE

    def body(l0_ref, l1_ref, b0_ref, b1_ref, pa_ref, ps_ref, o_ref):
        s0 = _sigmoid(l0_ref[...] + b0_ref[...])
        s1 = _sigmoid(l1_ref[...] + b1_ref[...])
        o_ref[...] = (s0 * pa_ref[...] + s1 * ps_ref[...]).astype(BF16)

    blk = lambda off: pl.BlockSpec((tl, GATE_TILE), lambda i, j: (i, off + j))
    bias = lambda off: pl.BlockSpec((1, GATE_TILE), lambda i, j: (0, off + j))
    return pl.pallas_call(
        body, name="merge", grid=(L // tl, nc),
        in_specs=[blk(g0), blk(g1), bias(0), bias(nc), blk(0), blk(0)],
        out_specs=blk(0), out_shape=jax.ShapeDtypeStruct((L, D_MODEL), BF16),
        compiler_params=_params("parallel", "parallel"))(proj, proj, b_gate, b_gate, pa, ps)


def _merge_bwd_call(dm, proj, b_gate, pa, ps):
    L = proj.shape[0]
    tl = _fit(L, GATE_ROWS)
    nc = D_MODEL // GATE_TILE
    g0, g1 = P_GATE // GATE_TILE, (P_GATE + D_MODEL) // GATE_TILE

    def body(dm_ref, l0_ref, l1_ref, b0_ref, b1_ref, pa_ref, ps_ref, dpa_ref, dps_ref, dl0_ref, dl1_ref, db0_ref, db1_ref):
        first = pl.program_id(1) == 0
        dm_ = dm_ref[...]
        s0 = _sigmoid(l0_ref[...] + b0_ref[...])
        s1 = _sigmoid(l1_ref[...] + b1_ref[...])
        dpa_ref[...] = (dm_ * s0).astype(BF16)
        dps_ref[...] = (dm_ * s1).astype(BF16)
        dl0 = dm_ * pa_ref[...] * s0 * (1.0 - s0)
        dl1 = dm_ * ps_ref[...] * s1 * (1.0 - s1)
        dl0_ref[...] = dl0.astype(BF16)
        dl1_ref[...] = dl1.astype(BF16)
        _acc(db0_ref, first, jnp.sum(dl0, axis=0, keepdims=True))
        _acc(db1_ref, first, jnp.sum(dl1, axis=0, keepdims=True))

    blk = lambda off: pl.BlockSpec((tl, GATE_TILE), lambda j, i: (i, off + j))
    bias = lambda off: pl.BlockSpec((1, GATE_TILE), lambda j, i: (0, off + j))
    act = jax.ShapeDtypeStruct((L, D_MODEL), BF16)
    vec = jax.ShapeDtypeStruct((1, D_MODEL), F32)
    return pl.pallas_call(
        body, name="merge_bwd", grid=(nc, L // tl),
        in_specs=[blk(0), blk(g0), blk(g1), bias(0), bias(nc), blk(0), blk(0)],
        out_specs=[blk(0), blk(0), blk(0), blk(0), bias(0), bias(0)],
        out_shape=[act, act, act, act, vec, vec],
        compiler_params=_params("parallel", "arbitrary"))(dm, proj, proj, b_gate, b_gate, pa, ps)


def _post_mix_rows(o, x, g_post, g_fpre):
    x2 = x + _rms(o, g_post)
    return o, x2, _rms(x2, g_fpre)


def _ffn_out_rows(ff, x2, target, g_fpost):
    n = ff.shape[-1]
    err = x2 + _rms(ff, g_fpost) - target
    part = 0.5 * jnp.sum(jnp.sum(err * err, axis=-1, keepdims=True) * (1.0 / n), axis=0, keepdims=True)
    dy = err * (1.0 / n)
    dff, dg = _rms_bwd(ff, g_fpost, dy)
    return dy, dff, jnp.broadcast_to(part, (1, LANES)), dg


def _post_bwd_rows(dhn2, x2, dy, o, g_fpre, g_post):
    d1, dgf = _rms_bwd(x2, g_fpre, dhn2)
    dx2 = dy + d1
    do, dgp = _rms_bwd(o, g_post, dx2)
    return dx2, do, dgf, dgp


CONV_TILE = 256
CONV_WIDE = 1408
HALO = 16


def _conv3(w, b, x0, x1, x2):
    return b + w[2:3] * x0 + w[1:2] * x1 + w[0:1] * x2


def _down(x, by):
    return pltpu.roll(x, by, 0)


def _edge_down(edge, before, by):
    r = lax.broadcasted_iota(jnp.int32, edge.shape, 0)
    return jnp.where(r < by, pltpu.roll(before, by, 0), pltpu.roll(edge, by, 0))


def _edge_up(edge, after, by):
    r = lax.broadcasted_iota(jnp.int32, edge.shape, 0)
    return jnp.where(r >= HALO - by, pltpu.roll(after, HALO - by, 0), pltpu.roll(edge, HALO - by, 0))


def _gated(w_g, b_g, w_v, b_v, hg, hv, g1, g2, v1, v2):
    return _conv3(w_g, b_g, hg, g1, g2), _conv3(w_v, b_v, hv, v1, v2)


def _conv_specs(tl, tc, rows_inner):
    nh = tl // HALO
    if rows_inner:
        ij = lambda f: (lambda j, i: f(i, j))
    else:
        ij = lambda f: f
    cur = lambda off: pl.BlockSpec((tl, tc), ij(lambda i, j: (i, off + j)))
    prev = lambda off: pl.BlockSpec((HALO, tc), ij(lambda i, j: (jnp.maximum(i * nh - 1, 0), off + j)))
    par = lambda rows, off: pl.BlockSpec((rows, tc), ij(lambda i, j: (0, off + j)))
    return cur, prev, par


def _conv_act_call(h, conv_w, conv_b):
    L = h.shape[0]
    tl = _fit(L, 256)
    nc = D_FF // CONV_WIDE
    cur, prev, par = _conv_specs(tl, CONV_WIDE, False)

    def body(hg_ref, hv_ref, pg_ref, pv_ref, wg_ref, wv_ref, bg_ref, bv_ref, a_ref):
        not_first = (pl.program_id(0) > 0).astype(F32)
        par = (wg_ref[...], bg_ref[...], wv_ref[...], bv_ref[...])
        hg, hv = hg_ref[...], hv_ref[...]
        gate, val = _gated(*par, hg, hv, _down(hg, 1), _down(hg, 2), _down(hv, 1), _down(hv, 2))
        a_ref[...] = (_gelu(gate) * val).astype(BF16)
        eg, ev, bg, bv = hg[:HALO], hv[:HALO], pg_ref[...] * not_first, pv_ref[...] * not_first
        gate, val = _gated(*par, eg, ev, _edge_down(eg, bg, 1), _edge_down(eg, bg, 2),
                           _edge_down(ev, bv, 1), _edge_down(ev, bv, 2))
        a_ref[:HALO, :] = (_gelu(gate) * val).astype(BF16)

    return pl.pallas_call(
        body, name="conv_act", grid=(L // tl, nc),
        in_specs=[cur(0), cur(nc), prev(0), prev(nc), par(3, 0), par(3, nc), par(1, 0), par(1, nc)],
        out_specs=cur(0), out_shape=jax.ShapeDtypeStruct((L, D_FF), BF16),
        compiler_params=_params("parallel", "parallel"))(h, h, h, h, conv_w, conv_w, conv_b, conv_b)


def _conv_act_bwd_call(da, h, conv_w, conv_b):
    L = h.shape[0]
    tl = _fit(L, 512)
    nc = D_FF // CONV_TILE
    cur, prev, par = _conv_specs(tl, CONV_TILE, True)

    def body(da_ref, hg_ref, hv_ref, pg_ref, pv_ref, wg_ref, wv_ref, bg_ref, bv_ref,
             dg_ref, dv_ref, dwg_ref, dwv_ref, dbg_ref, dbv_ref):
        first = pl.program_id(1) == 0
        not_first = (pl.program_id(1) > 0).astype(F32)
        par = (wg_ref[...], bg_ref[...], wv_ref[...], bv_ref[...])
        col = lambda t: jnp.sum(t, axis=0, keepdims=True)

        def grads(da_, hg, hv, g1, g2, v1, v2):
            gate, val = _gated(*par, hg, hv, g1, g2, v1, v2)
            act, slope = _gelu_and_grad(gate)
            dgate = da_ * val * slope
            dval = da_ * act
            sums = (jnp.concatenate([col(dgate * g2), col(dgate * g1), col(dgate * hg)], axis=0),
                    jnp.concatenate([col(dval * v2), col(dval * v1), col(dval * hv)], axis=0), col(dgate), col(dval))
            return dgate, dval, sums

        da_, hg, hv = da_ref[...], hg_ref[...], hv_ref[...]
        shifted = (_down(hg, 1), _down(hg, 2), _down(hv, 1), _down(hv, 2))
        dgate, dval, whole = grads(da_, hg, hv, *shifted)
        dg_ref[...] = dgate.astype(BF16)
        dv_ref[...] = dval.astype(BF16)
        edge = lambda t: t[:HALO]
        _, _, wrapped = grads(edge(da_), edge(hg), edge(hv), *[edge(s) for s in shifted])
        eg, ev, bg, bv = edge(hg), edge(hv), pg_ref[...] * not_first, pv_ref[...] * not_first
        dgate, dval, fixed = grads(edge(da_), eg, ev, _edge_down(eg, bg, 1), _edge_down(eg, bg, 2),
                                   _edge_down(ev, bv, 1), _edge_down(ev, bv, 2))
        dg_ref[:HALO, :] = dgate.astype(BF16)
        dv_ref[:HALO, :] = dval.astype(BF16)
        for ref, a, b, c in zip((dwg_ref, dwv_ref, dbg_ref, dbv_ref), whole, wrapped, fixed):
            _acc(ref, first, a - b + c)

    act = jax.ShapeDtypeStruct((L, D_FF), BF16)
    w3 = jax.ShapeDtypeStruct((3, D_FF), F32)
    w1 = jax.ShapeDtypeStruct((1, D_FF), F32)
    return pl.pallas_call(
        body, name="conv_act_bwd", grid=(nc, L // tl),
        in_specs=[cur(0), cur(0), cur(nc), prev(0), prev(nc), par(3, 0), par(3, nc), par(1, 0), par(1, nc)],
        out_specs=[cur(0), cur(0), par(3, 0), par(3, 0), par(1, 0), par(1, 0)],
        out_shape=[act, act, w3, w3, w1, w1],
        compiler_params=_params("parallel", "arbitrary"))(da, h, h, h, h, conv_w, conv_w, conv_b, conv_b)


def _conv_t_call(dgate, dval, conv_w):
    L = dgate.shape[0]
    tl = _fit(L, 512)
    nc = D_FF // CONV_WIDE
    nh = tl // HALO

    def body(dg_ref, dv_ref, ng_ref, nv_ref, w_ref, o_ref):
        not_last = (pl.program_id(0) < L // tl - 1).astype(F32)

        def emit(d_ref, n_ref):
            c = d_ref[...].astype(F32)
            w = w_ref[...]
            o_ref[...] = _conv3(w, 0.0, c, pltpu.roll(c, tl - 1, 0), pltpu.roll(c, tl - 2, 0)).astype(BF16)
            edge, after = c[tl - HALO:], n_ref[...].astype(F32) * not_last
            o_ref[tl - HALO:, :] = _conv3(w, 0.0, edge, _edge_up(edge, after, 1), _edge_up(edge, after, 2)).astype(BF16)

        pl.when(pl.program_id(1) < nc)(lambda: emit(dg_ref, ng_ref))
        pl.when(pl.program_id(1) >= nc)(lambda: emit(dv_ref, nv_ref))

    gate_col = lambda j: jnp.minimum(j, nc - 1)
    val_col = lambda j: jnp.maximum(j - nc, 0)
    after_row = lambda i: jnp.minimum((i + 1) * nh, L // HALO - 1)
    tile = lambda col: pl.BlockSpec((tl, CONV_WIDE), lambda i, j: (i, col(j)))
    after = lambda col: pl.BlockSpec((HALO, CONV_WIDE), lambda i, j: (after_row(i), col(j)))
    return pl.pallas_call(
        body, name="conv_t", grid=(L // tl, 2 * nc),
        in_specs=[tile(gate_col), tile(val_col), after(gate_col), after(val_col), pl.BlockSpec((3, CONV_WIDE), lambda i, j: (0, j))],
        out_specs=pl.BlockSpec((tl, CONV_WIDE), lambda i, j: (i, j)),
        out_shape=jax.ShapeDtypeStruct((L, 2 * D_FF), BF16),
        compiler_params=_params("parallel", "parallel"))(dgate, dval, dgate, dval, conv_w)


def _glu_call(y1, w_glu, b_glu):
    L, n = y1.shape
    tl = _fit(L, 512)

    def body(y_ref, w_ref, b_ref, o_ref):
        y2 = _gelu(y_ref[...])
        z = _dot(y2.astype(BF16), w_ref[...], NN) + b_ref[...]
        o_ref[...] = (y2 * _sigmoid(z)).astype(BF16)

    return pl.pallas_call(
        body, name="glu", grid=(L // tl,), in_specs=[_row(tl, n), _full((n, n)), _full((1, n))],
        out_specs=_row(tl, n), out_shape=jax.ShapeDtypeStruct((L, n), BF16),
        compiler_params=_params("parallel"))(y1, w_glu, b_glu)


def _glu_bwd_call(dout, y1, w_glu, b_glu):
    L, n = y1.shape
    tl = _fit(L, 512)

    def body(do_ref, y_ref, w_ref, b_ref, dy_ref, dw_ref, db_ref):
        first = pl.program_id(0) == 0
        y1_ = y_ref[...]
        y2, slope = _gelu_and_grad(y1_)
        y2b = y2.astype(BF16)
        w = w_ref[...]
        sg = _sigmoid(_dot(y2b, w, NN) + b_ref[...])
        dout_ = do_ref[...].astype(F32)
        dz = dout_ * y2 * sg * (1.0 - sg)
        dzb = dz.astype(BF16)
        dy2 = dout_ * sg + _dot(dzb, w, NT)
        dy_ref[...] = dy2 * slope
        _acc(dw_ref, first, _dot(y2b, dzb, TN))
        _acc(db_ref, first, jnp.sum(dz, axis=0, keepdims=True))

    return pl.pallas_call(
        body, name="glu_bwd", grid=(L // tl,),
        in_specs=[_row(tl, n), _row(tl, n), _full((n, n)), _full((1, n))],
        out_specs=[_row(tl, n), _full((n, n)), _full((1, n))],
        out_shape=[jax.ShapeDtypeStruct((L, n), F32), jax.ShapeDtypeStruct((n, n), F32), jax.ShapeDtypeStruct((1, n), F32)],
        compiler_params=_params("arbitrary"))(dout, y1, w_glu, b_glu)


ATTN_TILE = 1024
ATTN_SCALE = 1.0 / math.sqrt(QK_HEAD)


ATTN_HEADS = 2
ATTN_GROUPS = N_HEADS // ATTN_HEADS
LOG2E = 1.0 / math.log(2.0)
Q_PRESCALE = ATTN_SCALE * LOG2E
ANY_SPEC = pl.BlockSpec(memory_space=pl.ANY)


def _attn_fwd_call(q, kv, blocks):
    L = q.shape[0]
    t = _fit(L, ATTN_TILE)
    nq = L // t
    n = len(blocks)

    def body(q_ref, kv_ref, *refs):
        blk_refs, (o_ref, lse_ref), gat_refs = refs[:n], refs[n:n + 2], refs[n + 2:2 * n + 2]
        m_s, acc_s, send_sems, recv_sems, local_sems = refs[2 * n + 2:]
        g, i = pl.program_id(0), pl.program_id(1)
        start, forward, finish = _gather_phases(blk_refs, gat_refs, send_sems, recv_sems, local_sems)
        pl.when(jnp.logical_and(g == 0, i == 0))(start)
        m_s[...] = jnp.full((ATTN_HEADS, t, 1), NEG, F32)
        acc_s[...] = jnp.zeros((ATTN_HEADS, t, LANES), F32)
        below = lax.broadcasted_iota(jnp.int32, (t, t), 1) <= lax.broadcasted_iota(jnp.int32, (t, t), 0)

        def block_step(kb, on_diagonal):
            rows = pl.ds(pl.multiple_of(kb * t, t), t)
            for a in range(ATTN_HEADS):
                s = _dot(q_ref[:, a * LANES:(a + 1) * LANES], kv_ref[rows, 2 * a * LANES:(2 * a + 1) * LANES], NT)
                if on_diagonal:
                    s = jnp.where(below, s, NEG)
                m_prev = m_s[a]
                m_new = jnp.maximum(m_prev, jnp.max(s, axis=1, keepdims=True))
                p = jnp.exp2(s - m_new)
                pv = _dot(p.astype(BF16), kv_ref[rows, (2 * a + 1) * LANES:(2 * a + 2) * LANES], NN)
                acc_s[a] = jnp.exp2(m_prev - m_new) * acc_s[a] + pv
                m_s[a] = m_new

        def step(kb, carry):
            block_step(kb, False)
            return carry

        lax.fori_loop(0, i, step, 0)
        block_step(i, True)
        lane = lax.broadcasted_iota(jnp.int32, (t, LANES), 1)
        for a in range(ATTN_HEADS):
            acc = acc_s[a]
            l = jnp.sum(jnp.where(lane == V_HEAD, acc, 0.0), axis=1, keepdims=True)
            o_ref[:, a * LANES:(a + 1) * LANES] = (acc / l).astype(BF16)
            lse_ref[a] = m_s[a] + jnp.log(l) * LOG2E
        pl.when(jnp.logical_and(g == (3 * ATTN_GROUPS) // 4, i == 0))(forward)
        pl.when(jnp.logical_and(g == ATTN_GROUPS - 1, i == nq - 1))(finish)

    gw = ATTN_HEADS * LANES
    return pl.pallas_call(
        body, name="attn_fwd", grid=(ATTN_GROUPS, nq),
        in_specs=[pl.BlockSpec((t, gw), lambda g, i: (i, g)),
                  pl.BlockSpec((L, 2 * gw), lambda g, i: (0, g))] + [ANY_SPEC] * n,
        out_specs=[pl.BlockSpec((t, gw), lambda g, i: (i, g)),
                   pl.BlockSpec((ATTN_HEADS, t, 1), lambda g, i: (g, i, 0))] + [ANY_SPEC] * n,
        out_shape=[jax.ShapeDtypeStruct((L, HEAD_PAD), BF16), jax.ShapeDtypeStruct((N_HEADS, L, 1), F32)]
        + [jax.ShapeDtypeStruct((N_DEV,) + b.shape, b.dtype) for b in blocks],
        scratch_shapes=[pltpu.VMEM((ATTN_HEADS, t, 1), F32), pltpu.VMEM((ATTN_HEADS, t, LANES), F32)] + _comm_sems(n),
        compiler_params=_params("arbitrary", "arbitrary", vmem=VMEM_BIG))(q, kv, *blocks)


def _attn_bwd_call(q, kv, o, do, lse, parts, blocks):
    L = q.shape[0]
    t = _fit(L, ATTN_TILE)
    nq = L // t
    n1, n = len(parts), len(parts) + len(blocks)

    def body(q_ref, do_ref, o_ref, lse_ref, kv_ref, *refs):
        in_refs, (dq_ref, dkv_ref), out_refs = refs[:n], refs[n:n + 2], refs[n + 2:2 * n + 2]
        dk_s, dv_s = refs[2 * n + 2:2 * n + 4]
        g, j = pl.program_id(0), pl.program_id(1)
        start, finish = _exchange_phases(in_refs[:n1], out_refs[:n1], *refs[2 * n + 4:2 * n + 7])
        start_blocks, finish_blocks = _exchange_phases(in_refs[n1:], out_refs[n1:], *refs[2 * n + 7:], same_source=True)

        @pl.when(jnp.logical_and(g == 0, j == 0))
        def _():
            start()
            start_blocks()

        @pl.when(j == 0)
        def _():
            dq_ref[...] = jnp.zeros((L, ATTN_HEADS * LANES), F32)

        dk_s[...] = jnp.zeros((ATTN_HEADS, t, LANES), F32)
        dv_s[...] = jnp.zeros((ATTN_HEADS, t, LANES), F32)
        below = lax.broadcasted_iota(jnp.int32, (t, t), 1) <= lax.broadcasted_iota(jnp.int32, (t, t), 0)

        def block_step(i, on_diagonal):
            rows = pl.ds(pl.multiple_of(i * t, t), t)
            for a in range(ATTN_HEADS):
                lanes = slice(a * LANES, (a + 1) * LANES)
                qi = q_ref[rows, lanes]
                doi = do_ref[rows, lanes]
                kblk = kv_ref[:, 2 * a * LANES:(2 * a + 1) * LANES]
                delta = jnp.sum(doi.astype(F32) * o_ref[rows, lanes].astype(F32), axis=1, keepdims=True)
                s = _dot(qi, kblk, NT)
                if on_diagonal:
                    s = jnp.where(below, s, NEG)
                p = jnp.exp2(s - lse_ref[a, rows, :])
                dv_s[a] += _dot(p.astype(BF16), doi, TN)
                ds = (p * (_dot(doi, kv_ref[:, (2 * a + 1) * LANES:(2 * a + 2) * LANES], NT) - delta)).astype(BF16)
                dk_s[a] += _dot(ds, qi, TN)
                dq_ref[rows, lanes] += _dot(ds, kblk, NN) * ATTN_SCALE

        def step(i, carry):
            block_step(i, False)
            return carry

        block_step(j, True)
        lax.fori_loop(j + 1, nq, step, 0)
        for a in range(ATTN_HEADS):
            dkv_ref[:, 2 * a * LANES:(2 * a + 1) * LANES] = dk_s[a] * (1.0 / LOG2E)
            dkv_ref[:, (2 * a + 1) * LANES:(2 * a + 2) * LANES] = dv_s[a]
        @pl.when(jnp.logical_and(g == ATTN_GROUPS - 1, j == nq - 1))
        def _():
            finish()
            finish_blocks()

    gw = ATTN_HEADS * LANES
    whole = lambda: pl.BlockSpec((L, gw), lambda g, j: (0, g))
    acc = pltpu.VMEM((ATTN_HEADS, t, LANES), F32)
    return pl.pallas_call(
        body, name="attn_bwd", grid=(ATTN_GROUPS, nq),
        in_specs=[whole(), whole(), whole(), pl.BlockSpec((ATTN_HEADS, L, 1), lambda g, j: (g, 0, 0)),
                  pl.BlockSpec((t, 2 * gw), lambda g, j: (j, g))] + [ANY_SPEC] * n,
        out_specs=[whole(), pl.BlockSpec((t, 2 * gw), lambda g, j: (j, g))] + [ANY_SPEC] * n,
        out_shape=[jax.ShapeDtypeStruct((L, HEAD_PAD), F32), jax.ShapeDtypeStruct((L, 2 * HEAD_PAD), F32)]
        + [jax.ShapeDtypeStruct(p.shape, p.dtype) for p in parts]
        + [jax.ShapeDtypeStruct((N_DEV,) + b.shape, b.dtype) for b in blocks],
        scratch_shapes=[acc, acc] + _comm_sems(n1) + _comm_sems(n - n1),
        compiler_params=_params("arbitrary", "arbitrary", vmem=VMEM_BIG))(q, do, o, lse, kv, *parts, *blocks)


def _disc(lr, li, ldt, br, bi):
    dt = jnp.exp(ldt)
    mag = jnp.exp(lr * dt)
    ang = li * dt
    a_re, a_im = mag * jnp.cos(ang), mag * jnp.sin(ang)
    den = lr * lr + li * li
    n_re, n_im = a_re - 1.0, a_im
    z_re = (n_re * lr + n_im * li) / den
    z_im = (n_im * lr - n_re * li) / den
    return a_re, a_im, z_re * br - z_im * bi, z_re * bi + z_im * br


def _disc_call(lr, li, ldt, br, bi):
    def body(lr_ref, li_ref, ldt_ref, br_ref, bi_ref, ar_ref, ai_ref, bbr_ref, bbi_ref):
        ar_ref[...], ai_ref[...], bbr_ref[...], bbi_ref[...] = _disc(
            lr_ref[...], li_ref[...], ldt_ref[...], br_ref[...], bi_ref[...])

    c1 = jax.ShapeDtypeStruct((SSM_NSTATE, 1), F32)
    c16 = jax.ShapeDtypeStruct((SSM_NSTATE, SSM_GROUP), F32)
    return pl.pallas_call(body, name="ssm_disc", out_shape=[c1, c1, c16, c16])(lr, li, ldt, br, bi)


def _disc_bwd_call(lr, li, ldt, br, bi, dar, dai, dbbr, dbbi):
    def body(lr_ref, li_ref, ldt_ref, br_ref, bi_ref, dar_ref, dai_ref, dbbr_ref, dbbi_ref,
             dlr_ref, dli_ref, dldt_ref, dbr_ref, dbi_ref):
        _, vjp = jax.vjp(_disc, lr_ref[...], li_ref[...], ldt_ref[...], br_ref[...], bi_ref[...])
        dlr_ref[...], dli_ref[...], dldt_ref[...], dbr_ref[...], dbi_ref[...] = vjp(
            (dar_ref[...], dai_ref[...], dbbr_ref[...], dbbi_ref[...]))

    c1 = jax.ShapeDtypeStruct((SSM_NSTATE, 1), F32)
    c16 = jax.ShapeDtypeStruct((SSM_NSTATE, SSM_GROUP), F32)
    return pl.pallas_call(body, name="ssm_disc_bwd", out_shape=[c1, c1, c1, c16, c16])(
        lr, li, ldt, br, bi, dar, dai, dbbr, dbbi)


SSM_ROWS = 512
SSM_CW = SSM_NSTATE // SSM_CHUNKS
SSM_CU = SSM_WIDTH // SSM_CHUNKS


def _cmul(ar, ai, br, bi):
    return ar * br - ai * bi, ar * bi + ai * br


def _power(ar1, ai1, n):
    def step(_, c):
        return _cmul(c[0], c[1], ar1, ai1)

    return lax.fori_loop(0, n, step, (jnp.ones_like(ar1), jnp.zeros_like(ar1)))


def _tile(k):
    return pl.ds(pl.multiple_of(k * 8, 8), 8)


def _ssm_fwd_call(u, a_re, a_im, bb_re, bb_im, cm_re, cm_im, d_skip):
    L = u.shape[0]
    seg = L // 8
    rb = _fit(L, SSM_ROWS)

    def body(u_ref, ar_ref, ai_ref, bbr_ref, bbi_ref, cmr_ref, cmi_ref, d_ref, y_ref, sre_hbm, sim_hbm,
             s_re, s_im, sems):
        q = pl.program_id(0)

        def bu_step(r, c):
            rows = pl.ds(pl.multiple_of(r * rb, rb), rb)
            ub = u_ref[rows, :].astype(BF16)
            s_re[rows, :] = _dot(ub, bbr_ref[0], NN)
            s_im[rows, :] = _dot(ub, bbi_ref[0], NN)
            return c

        lax.fori_loop(0, L // rb, bu_step, 0)
        ar1, ai1 = ar_ref[...], ai_ref[...]
        ar = jnp.broadcast_to(ar1, (8, SSM_CW))
        ai = jnp.broadcast_to(ai1, (8, SSM_CW))

        def local(k, c):
            nr, ni = _cmul(ar, ai, c[0], c[1])
            nr = nr + s_re[_tile(k), :]
            ni = ni + s_im[_tile(k), :]
            s_re[_tile(k), :] = nr
            s_im[_tile(k), :] = ni
            return nr, ni

        zero8 = jnp.zeros((8, SSM_CW), F32)
        lax.fori_loop(0, seg, local, (zero8, zero8))
        pr, pi = _power(ar1, ai1, seg)
        end_r = s_re[pl.ds((seg - 1) * 8, 8), :]
        end_i = s_im[pl.ds((seg - 1) * 8, 8), :]
        er = jnp.zeros((1, SSM_CW), F32)
        ei = jnp.zeros((1, SSM_CW), F32)
        rows_r, rows_i = [er], [ei]
        for j in range(7):
            tr, ti = _cmul(pr, pi, er, ei)
            er, ei = end_r[j:j + 1] + tr, end_i[j:j + 1] + ti
            rows_r.append(er)
            rows_i.append(ei)
        e_r = jnp.concatenate(rows_r, axis=0)
        e_i = jnp.concatenate(rows_i, axis=0)

        def fix(k, c):
            wr, wi = _cmul(c[0], c[1], ar, ai)
            fr, fi = _cmul(wr, wi, e_r, e_i)
            s_re[_tile(k), :] += fr
            s_im[_tile(k), :] += fi
            return wr, wi

        lax.fori_loop(0, seg, fix, (jnp.ones((8, SSM_CW), F32), zero8))
        out_r = pltpu.make_async_copy(s_re, sre_hbm.at[q], sems.at[0])
        out_i = pltpu.make_async_copy(s_im, sim_hbm.at[q], sems.at[1])
        out_r.start()
        out_i.start()

        def y_step(r, c):
            rows = pl.ds(pl.multiple_of(r * rb, rb), rb)
            y = _dot(s_re[rows, :].astype(BF16), cmr_ref[0], NN) - _dot(s_im[rows, :].astype(BF16), cmi_ref[0], NN)
            y_ref[rows, :] = y + d_ref[...] * u_ref[rows, :]
            return c

        lax.fori_loop(0, L // rb, y_step, 0)
        out_r.wait()
        out_i.wait()

    chunk = lambda rows, cols: pl.BlockSpec((rows, cols), lambda q: (0, q))
    mat = lambda r, c: pl.BlockSpec((1, r, c), lambda q: (q, 0, 0))
    anyspec = pl.BlockSpec(memory_space=pl.ANY)
    states = jax.ShapeDtypeStruct((SSM_CHUNKS, L, SSM_CW), F32)
    return pl.pallas_call(
        body, name="ssm_fwd", grid=(SSM_CHUNKS,),
        in_specs=[chunk(L, SSM_CU), chunk(1, SSM_CW), chunk(1, SSM_CW), mat(SSM_CU, SSM_CW), mat(SSM_CU, SSM_CW),
                  mat(SSM_CW, SSM_CU), mat(SSM_CW, SSM_CU), chunk(1, SSM_CU)],
        out_specs=[chunk(L, SSM_CU), anyspec, anyspec],
        out_shape=[jax.ShapeDtypeStruct((L, SSM_WIDTH), F32), states, states],
        scratch_shapes=[pltpu.VMEM((L, SSM_CW), F32), pltpu.VMEM((L, SSM_CW), F32), pltpu.SemaphoreType.DMA((2,))],
        compiler_params=_params("arbitrary", vmem=VMEM_BIG))(u, a_re, a_im, bb_re, bb_im, cm_re, cm_im, d_skip)


def _ssm_bwd_call(dy, u, s_re_all, s_im_all, a_re, a_im, bb_re, bb_im, cm_re, cm_im, d_skip):
    L = u.shape[0]
    seg = L // 8
    rb = _fit(L, SSM_ROWS)

    def body(dy_ref, u_ref, sre_hbm, sim_hbm, ar_ref, ai_ref, bbr_ref, bbi_ref, cmr_ref, cmi_ref, d_ref,
             du_ref, dbbr_ref, dbbi_ref, dcmr_ref, dcmi_ref, dar_ref, dai_ref, dd_ref,
             g_re, g_im, s_re, s_im, sems):
        q = pl.program_id(0)
        in_r = pltpu.make_async_copy(sre_hbm.at[q], s_re, sems.at[0])
        in_i = pltpu.make_async_copy(sim_hbm.at[q], s_im, sems.at[1])
        in_r.start()
        in_i.start()

        def ds_step(r, c):
            rows = pl.ds(pl.multiple_of(r * rb, rb), rb)
            dyb = dy_ref[rows, :].astype(BF16)
            g_re[rows, :] = _dot(dyb, cmr_ref[0], NT)
            g_im[rows, :] = -_dot(dyb, cmi_ref[0], NT)
            return c

        lax.fori_loop(0, L // rb, ds_step, 0)
        ar1, ai1 = ar_ref[...], ai_ref[...]
        ar = jnp.broadcast_to(ar1, (8, SSM_CW))
        nai = jnp.broadcast_to(-ai1, (8, SSM_CW))

        def local(kk, c):
            k = seg - 1 - kk
            nr, ni = _cmul(ar, nai, c[0], c[1])
            nr = nr + g_re[_tile(k), :]
            ni = ni + g_im[_tile(k), :]
            g_re[_tile(k), :] = nr
            g_im[_tile(k), :] = ni
            return nr, ni

        zero8 = jnp.zeros((8, SSM_CW), F32)
        lax.fori_loop(0, seg, local, (zero8, zero8))
        pr, pi = _power(ar1, -ai1, seg)
        head_r = g_re[pl.ds(0, 8), :]
        head_i = g_im[pl.ds(0, 8), :]
        fr = jnp.zeros((1, SSM_CW), F32)
        fi = jnp.zeros((1, SSM_CW), F32)
        rows_r, rows_i = [fr], [fi]
        for j in range(6, -1, -1):
            tr, ti = _cmul(pr, pi, fr, fi)
            fr, fi = head_r[j + 1:j + 2] + tr, head_i[j + 1:j + 2] + ti
            rows_r.insert(0, fr)
            rows_i.insert(0, fi)
        f_r = jnp.concatenate(rows_r, axis=0)
        f_i = jnp.concatenate(rows_i, axis=0)
        in_r.wait()
        in_i.wait()

        def fixed(k, wr, wi):
            xr, xi = _cmul(wr, wi, f_r, f_i)
            gr = g_re[_tile(k), :] + xr
            gi = g_im[_tile(k), :] + xi
            g_re[_tile(k), :] = gr
            g_im[_tile(k), :] = gi
            return gr, gi

        def fix(kk, c):
            k = seg - 1 - kk
            wr, wi = _cmul(c[0], c[1], ar, nai)
            gr, gi = fixed(k, wr, wi)
            pr_, pi_ = s_re[_tile(k - 1), :], s_im[_tile(k - 1), :]
            return wr, wi, c[2] + gr * pr_ + gi * pi_, c[3] + gi * pr_ - gr * pi_

        wr, wi, acc_r, acc_i = lax.fori_loop(0, seg - 1, fix, (jnp.ones((8, SSM_CW), F32), zero8, zero8, zero8))
        wr, wi = _cmul(wr, wi, ar, nai)
        gr, gi = fixed(0, wr, wi)
        row8 = lax.broadcasted_iota(jnp.int32, (8, SSM_CW), 0)
        pr_ = jnp.where(row8 > 0, pltpu.roll(s_re[pl.ds((seg - 1) * 8, 8), :], 1, 0), 0.0)
        pi_ = jnp.where(row8 > 0, pltpu.roll(s_im[pl.ds((seg - 1) * 8, 8), :], 1, 0), 0.0)
        acc_r = acc_r + gr * pr_ + gi * pi_
        acc_i = acc_i + gi * pr_ - gr * pi_
        dar_ref[...] = jnp.sum(acc_r, axis=0, keepdims=True)
        dai_ref[...] = jnp.sum(acc_i, axis=0, keepdims=True)

        dbbr_ref[...] = jnp.zeros((1, SSM_CU, SSM_CW), F32)
        dbbi_ref[...] = jnp.zeros((1, SSM_CU, SSM_CW), F32)
        dcmr_ref[...] = jnp.zeros((1, SSM_CW, SSM_CU), F32)
        dcmi_ref[...] = jnp.zeros((1, SSM_CW, SSM_CU), F32)
        dd_ref[...] = jnp.zeros((1, SSM_CU), F32)

        def grad_step(r, c):
            rows = pl.ds(pl.multiple_of(r * rb, rb), rb)
            ub, dyv = u_ref[rows, :], dy_ref[rows, :]
            ubb, dyb = ub.astype(BF16), dyv.astype(BF16)
            grb, gib = g_re[rows, :].astype(BF16), g_im[rows, :].astype(BF16)
            dbbr_ref[0] += _dot(ubb, grb, TN)
            dbbi_ref[0] += _dot(ubb, gib, TN)
            dcmr_ref[0] += _dot(s_re[rows, :].astype(BF16), dyb, TN)
            dcmi_ref[0] -= _dot(s_im[rows, :].astype(BF16), dyb, TN)
            du_ref[rows, :] = _dot(grb, bbr_ref[0], NT) + _dot(gib, bbi_ref[0], NT) + d_ref[...] * dyv
            dd_ref[...] += jnp.sum(dyv * ub, axis=0, keepdims=True)
            return c

        lax.fori_loop(0, L // rb, grad_step, 0)

    chunk = lambda rows, cols: pl.BlockSpec((rows, cols), lambda q: (0, q))
    mat = lambda r, c: pl.BlockSpec((1, r, c), lambda q: (q, 0, 0))
    anyspec = pl.BlockSpec(memory_space=pl.ANY)
    big = lambda: pltpu.VMEM((L, SSM_CW), F32)
    return pl.pallas_call(
        body, name="ssm_bwd", grid=(SSM_CHUNKS,),
        in_specs=[chunk(L, SSM_CU), chunk(L, SSM_CU), anyspec, anyspec, chunk(1, SSM_CW), chunk(1, SSM_CW),
                  mat(SSM_CU, SSM_CW), mat(SSM_CU, SSM_CW), mat(SSM_CW, SSM_CU), mat(SSM_CW, SSM_CU), chunk(1, SSM_CU)],
        out_specs=[chunk(L, SSM_CU), mat(SSM_CU, SSM_CW), mat(SSM_CU, SSM_CW), mat(SSM_CW, SSM_CU), mat(SSM_CW, SSM_CU),
                   chunk(1, SSM_CW), chunk(1, SSM_CW), chunk(1, SSM_CU)],
        out_shape=[jax.ShapeDtypeStruct((L, SSM_WIDTH), F32),
                   jax.ShapeDtypeStruct((SSM_CHUNKS, SSM_CU, SSM_CW), F32), jax.ShapeDtypeStruct((SSM_CHUNKS, SSM_CU, SSM_CW), F32),
                   jax.ShapeDtypeStruct((SSM_CHUNKS, SSM_CW, SSM_CU), F32), jax.ShapeDtypeStruct((SSM_CHUNKS, SSM_CW, SSM_CU), F32),
                   jax.ShapeDtypeStruct((1, SSM_NSTATE), F32), jax.ShapeDtypeStruct((1, SSM_NSTATE), F32),
                   jax.ShapeDtypeStruct((1, SSM_WIDTH), F32)],
        scratch_shapes=[big(), big(), big(), big(), pltpu.SemaphoreType.DMA((2,))],
        compiler_params=_params("arbitrary", vmem=VMEM_BIG))(
            dy, u, s_re_all, s_im_all, a_re, a_im, bb_re, bb_im, cm_re, cm_im, d_skip)


def _place():
    return lax.axis_index("x"), lax.axis_index("y"), lax.axis_index("c")


def _all_gather_call(blocks, name, direct=False):
    n = len(blocks)

    def body(*refs):
        if direct:
            start, finish = _exchange_phases(refs[:n], refs[n:2 * n], *refs[2 * n:], same_source=True)
            start()
        else:
            start, forward, finish = _gather_phases(refs[:n], refs[n:2 * n], *refs[2 * n:])
            start()
            forward()
        finish()

    return pl.pallas_call(
        body, name=name, in_specs=[ANY_SPEC] * n, out_specs=[ANY_SPEC] * n,
        out_shape=[jax.ShapeDtypeStruct((N_DEV,) + b.shape, b.dtype) for b in blocks],
        scratch_shapes=_comm_sems(n))(*blocks)


def _comm_sems(n):
    return [pltpu.SemaphoreType.DMA((7 * n,)), pltpu.SemaphoreType.DMA((7 * n,)), pltpu.SemaphoreType.DMA((n,))]


def _gather_phases(x_refs, out_refs, send_sems, recv_sems, local_sems):
    x, y, c = _place()
    me, sibling = (x, y, c), (x, y, 1 - c)
    chips = [(1 - x, y), (x, 1 - y), (1 - x, 1 - y)]
    n = len(x_refs)

    def copy(k, a, blk, to, from_input=False):
        slot = out_refs[a].at[4 * blk[0] + 2 * blk[1] + blk[2]]
        return pltpu.make_async_remote_copy(
            src_ref=x_refs[a] if from_input else slot, dst_ref=slot,
            send_sem=send_sems.at[k * n + a], recv_sem=recv_sems.at[k * n + a], device_id=to, device_id_type=MESH_ID)

    mine = [pltpu.make_async_copy(x_refs[a], out_refs[a].at[4 * x + 2 * y + c], local_sems.at[a]) for a in range(n)]
    first, passed = [], []
    for a in range(n):
        first.append(copy(0, a, me, sibling, True))
        first += [copy(1 + j, a, me, (*chip, c), True) for j, chip in enumerate(chips)]
        passed += [copy(4 + j, a, (*chip, c), sibling) for j, chip in enumerate(chips)]

    def start():
        for cp in mine + first:
            cp.start()

    def forward():
        for j, chip in enumerate(chips):
            for a in range(n):
                copy(1 + j, a, (*chip, c), me).wait_recv()
                passed[3 * a + j].start()

    def finish():
        for a in range(n):
            copy(0, a, sibling, me).wait_recv()
            for j, chip in enumerate(chips):
                copy(4 + j, a, (*chip, 1 - c), me).wait_recv()
        for cp in first + passed:
            cp.wait_send()
        for cp in mine:
            cp.wait()

    return start, forward, finish


def _exchange_phases(p_refs, out_refs, send_sems, recv_sems, local_sems, same_source=False):
    x, y, c = _place()
    me = 4 * x + 2 * y + c
    n = len(p_refs)

    def flip(k):
        px = 1 - x if k & 4 else x
        py = 1 - y if k & 2 else y
        pc = 1 - c if k & 1 else c
        return (px, py, pc), 4 * px + 2 * py + pc

    def source(a, slot):
        return p_refs[a] if same_source else p_refs[a].at[slot]

    def copy(k, a, landing):
        peer, peer_slot = flip(k)
        return pltpu.make_async_remote_copy(
            src_ref=source(a, peer_slot), dst_ref=out_refs[a].at[peer_slot if landing else me],
            send_sem=send_sems.at[(k - 1) * n + a], recv_sem=recv_sems.at[(k - 1) * n + a],
            device_id=peer, device_id_type=MESH_ID)

    mine = [pltpu.make_async_copy(source(a, me), out_refs[a].at[me], local_sems.at[a]) for a in range(n)]
    sends = [copy(k, a, False) for k in range(1, N_DEV) for a in range(n)]

    def start():
        for cp in mine + sends:
            cp.start()

    def finish():
        for k in range(1, N_DEV):
            for a in range(n):
                copy(k, a, True).wait_recv()
        for cp in sends:
            cp.wait_send()
        for cp in mine:
            cp.wait()

    return start, finish


def _adam_math(g, w, m, v):
    c1 = 1.0 / (1.0 - ADAM_B1 ** ADAM_STEP)
    c2 = 1.0 / (1.0 - ADAM_B2 ** ADAM_STEP)
    m_new = ADAM_B1 * m + (1.0 - ADAM_B1) * g
    v_new = ADAM_B2 * v + (1.0 - ADAM_B2) * (g * g)
    delta = -ADAM_LR * ((m_new * c1) / (jnp.sqrt(v_new * c2) + ADAM_EPS) + ADAM_WD * w)
    return g, delta, m_new, v_new


def _sum_slices(s_ref):
    g = s_ref[0].astype(F32)
    for k in range(1, N_DEV):
        g = g + s_ref[k].astype(F32)
    return g


def _adam_call(slices, w, m, v, name):
    d1, rest = w.shape[1], w.shape[2:]
    zeros = (0,) * len(rest)
    by_lanes = len(rest) == 1 and d1 > 256 and d1 % 16 != 0
    if by_lanes:
        tile = _fit(rest[0], 256)
        steps = rest[0] // tile
        own = pl.BlockSpec((1, d1, tile), lambda i: (0, 0, i))
        sl = pl.BlockSpec((N_DEV, 1, d1, tile), lambda i: (0, 0, 0, i))
    else:
        tile = _fit(d1, 256, 16) if len(rest) == 1 else _fit(d1, 8, 8)
        steps = d1 // tile
        own = pl.BlockSpec((1, tile) + rest, lambda i: (0, i) + zeros)
        sl = pl.BlockSpec((N_DEV, 1, tile) + rest, lambda i: (0, 0, i) + zeros)

    def body(s_ref, w_ref, m_ref, v_ref, g_ref, d_ref, mo_ref, vo_ref):
        g_ref[...], d_ref[...], mo_ref[...], vo_ref[...] = _adam_math(_sum_slices(s_ref), w_ref[...], m_ref[...], v_ref[...])

    out = jax.ShapeDtypeStruct(w.shape, F32)
    return pl.pallas_call(
        body, name=name, grid=(steps,), in_specs=[sl, own, own, own],
        out_specs=[own, own, own, own], out_shape=[out, out, out, out],
        compiler_params=_params("parallel"))(slices, w, m, v)


def _adam_small_call(rows_all, row_params, slices, params):
    nr, n = len(row_params), len(row_params) + len(params)

    def row_sum(rows_ref, a, width):
        g = rows_ref[0, pl.ds(a, 1), pl.ds(0, width)]
        for k in range(1, N_DEV):
            g = g + rows_ref[k, pl.ds(a, 1), pl.ds(0, width)]
        return g

    def body(rows_ref, *refs):
        slice_refs, wmv, outs = refs[:n - nr], refs[n - nr:n - nr + 3 * n], refs[n - nr + 3 * n:]
        outs[4 * n][...] = row_sum(rows_ref, nr, LANES)
        for a in range(n):
            w_ref, m_ref, v_ref = wmv[3 * a:3 * a + 3]
            if a < nr:
                g = row_sum(rows_ref, a, w_ref.shape[1])
            else:
                g = _sum_slices(slice_refs[a - nr])
            res = _adam_math(g, w_ref[...], m_ref[...], v_ref[...])
            for r in range(4):
                outs[4 * a + r][...] = res[r]

    every = list(row_params) + list(params)
    flat = pl.pallas_call(
        body, name="adam_small",
        out_shape=[jax.ShapeDtypeStruct(w.shape, F32) for w, _, _ in every for _ in range(4)]
        + [jax.ShapeDtypeStruct((1, LANES), F32)],
        compiler_params=pltpu.CompilerParams(vmem_limit_bytes=VMEM_BIG),
    )(rows_all, *slices, *[t for wmv in every for t in wmv])
    return [flat[4 * a:4 * a + 4] for a in range(n)], flat[4 * n][0, 0]


BIG = (("w_in", 1024, 404, 1), ("w_uq", 384, 96, 1), ("w_uk", 256, 64, 1), ("w_uv", 256, 64, 1),
       ("w_glu", 64, 512, 0), ("w_branch_attn", 512, 128, 1), ("w_branch_ssm", 512, 128, 1),
       ("w_out", 128, 1024, 0), ("w_up", 1024, 704, 1), ("w_down", 352, 1024, 0), ("conv_w", 3, 704, 1))
BIG_MIX, BIG_FFN = BIG[:8], BIG[8:]
GRADS_EARLY, GRADS_LATE = BIG[8:] + BIG[4:8], BIG[:4]
SMALL = (("mix_norm_pre", (1024,)), ("q_norm", (384,)), ("kv_norm", (256,)), ("ssm_lambda_re", (32, 64)),
         ("ssm_lambda_im", (32, 64)), ("ssm_log_dt", (32,)), ("ssm_b_re", (32, 64, 16)), ("ssm_b_im", (32, 64, 16)),
         ("ssm_c_re", (32, 16, 64)), ("ssm_c_im", (32, 16, 64)), ("ssm_d", (32, 16)), ("b_glu", (512,)),
         ("b_gate", (2048,)), ("mix_norm_post", (1024,)), ("ffn_norm_pre", (1024,)), ("conv_b", (5632,)),
         ("ffn_norm_post", (1024,)))


TRANSPOSED = ("w_in", "w_uq", "w_uk", "w_uv", "w_up")


STORED_SWAP = {**{name: (1, 2) for name in TRANSPOSED}, "ssm_b_re": (2, 3), "ssm_b_im": (2, 3), "ssm_d": (1, 2)}


def _stored(name, arr):
    return jnp.swapaxes(arr, *STORED_SWAP[name]) if name in STORED_SWAP else arr


def _to_slices(name, full, rows, cols, axis):
    if name in TRANSPOSED:
        return full.reshape(N_DEV, cols, rows)
    if axis == 1:
        return full.reshape(rows, N_DEV, cols).transpose(1, 0, 2)
    return full.reshape(N_DEV, rows, cols)


def _from_slices(name, parts, rows, cols, axis):
    if name in TRANSPOSED:
        return parts.reshape(N_DEV * cols, rows)
    if axis == 1:
        return parts.transpose(1, 0, 2).reshape(rows, N_DEV * cols)
    return parts.reshape(N_DEV * rows, cols)


def _time_perm(a, L):
    return a.reshape(8, L // 8, a.shape[-1]).transpose(1, 0, 2).reshape(L, a.shape[-1])


def _time_unperm(a, L):
    return a.reshape(L // 8, 8, a.shape[-1]).transpose(1, 0, 2).reshape(L, a.shape[-1])


def _block_diag(w, rows_first):
    eye = jnp.eye(8, dtype=w.dtype)
    g = w.reshape(SSM_CHUNKS, 8, w.shape[1], w.shape[2])
    return jnp.einsum("qgrc,gk->qgrkc", g, eye).reshape(SSM_CHUNKS, 8 * w.shape[1], 8 * w.shape[2])


def _block_diag_t(m, r, c):
    eye = jnp.eye(8, dtype=m.dtype)
    return jnp.einsum("qgrkc,gk->qgrc", m.reshape(SSM_CHUNKS, 8, r, 8, c), eye).reshape(SSM_GROUPS, r, c)


def kernel(x, positions, mix_norm_pre, w_in, q_norm, w_uq, kv_norm, w_uk, w_uv, ssm_lambda_re, ssm_lambda_im, ssm_log_dt, ssm_b_re, ssm_b_im, ssm_c_re, ssm_c_im, ssm_d, w_glu, b_glu, w_branch_attn, w_branch_ssm, b_gate, w_out, mix_norm_post, ffn_norm_pre, w_up, conv_w, conv_b, w_down, ffn_norm_post, loss_target, m_mix_norm_pre, m_w_in, m_q_norm, m_w_uq, m_kv_norm, m_w_uk, m_w_uv, m_ssm_lambda_re, m_ssm_lambda_im, m_ssm_log_dt, m_ssm_b_re, m_ssm_b_im, m_ssm_c_re, m_ssm_c_im, m_ssm_d, m_w_glu, m_b_glu, m_w_branch_attn, m_w_branch_ssm, m_b_gate, m_w_out, m_mix_norm_post, m_ffn_norm_pre, m_w_up, m_conv_w, m_conv_b, m_w_down, m_ffn_norm_post, v_mix_norm_pre, v_w_in, v_q_norm, v_w_uq, v_kv_norm, v_w_uk, v_w_uv, v_ssm_lambda_re, v_ssm_lambda_im, v_ssm_log_dt, v_ssm_b_re, v_ssm_b_im, v_ssm_c_re, v_ssm_c_im, v_ssm_d, v_w_glu, v_b_glu, v_w_branch_attn, v_w_branch_ssm, v_b_gate, v_w_out, v_mix_norm_post, v_ffn_norm_pre, v_w_up, v_conv_w, v_conv_b, v_w_down, v_ffn_norm_post):
    given = dict(locals())
    L = x.shape[1]
    xs = x[0]
    target = loss_target[0]

    def shard_bits(group):
        return [given[name][0] if name == "conv_w" else _stored(name, given[name])[0].astype(BF16) for name, _, _, _ in group]

    W = {}

    def unpack_weights(gathered, group):
        for (name, rows, cols, axis), parts in zip(group, gathered):
            W[name] = _from_slices(name, parts, rows, cols, axis)

    unpack_weights(_all_gather_call(shard_bits(BIG_MIX[:1]), "gather_w_in"), BIG_MIX[:1])

    wit = W["w_in"]
    zero_rows = lambda r: jnp.zeros((r, D_MODEL), BF16)
    kr_end = P_KR + QK_ROPE
    w_in_pt = jnp.concatenate(
        [wit[:P_KR], zero_rows(QK_NOPE), wit[P_KR:kr_end], zero_rows(LANES - QK_HEAD), wit[kr_end:]], axis=0)

    hn1 = _rms_fwd_call(xs, mix_norm_pre, "rms_pre")
    proj, *gathered_mix = _mm(hn1, w_in_pt, "mm_in", tb=True, tn=1664, gather=shard_bits(BIG_MIX[1:]))
    unpack_weights(gathered_mix, BIG_MIX[1:])
    head_rows = lambda wt, width: jnp.pad(wt.reshape(N_HEADS, width, wt.shape[1]), ((0, 0), (0, LANES - width), (0, 0)))
    w_uq_pt = head_rows(W["w_uq"], QK_HEAD).reshape(HEAD_PAD, Q_RANK)
    w_kv_pt = jnp.stack([head_rows(W["w_uk"], QK_NOPE), head_rows(W["w_uv"], V_HEAD)], axis=1
                        ).reshape(2 * HEAD_PAD, KV_RANK)
    w_ba_p = jnp.pad(W["w_branch_attn"].reshape(N_HEADS, V_HEAD, D_MODEL), ((0, 0), (0, LANES - V_HEAD), (0, 0))
                     ).reshape(HEAD_PAD, D_MODEL)
    half = jnp.arange(QK_ROPE // 2, dtype=F32)
    inv_freq = ROPE_THETA ** (-2.0 * half / QK_ROPE)
    inv_freq = jnp.pad(jnp.concatenate([inv_freq, inv_freq]), (QK_NOPE, LANES - QK_HEAD)).reshape(1, LANES)
    pos_col = positions.astype(F32).reshape(L, 1)
    qn, ckvn, q_r, kv_r, cosf, sinf = _mla_proj_call(proj, q_norm, kv_norm, w_uq_pt, w_kv_pt, pos_col, inv_freq)
    attn, lse, *gathered_ffn = _attn_fwd_call(q_r, kv_r, shard_bits(BIG_FFN))
    unpack_weights(gathered_ffn, BIG_FFN)

    col = lambda a: a.reshape(SSM_NSTATE, -1)
    lr_c, li_c = col(ssm_lambda_re[0]), col(ssm_lambda_im[0])
    ldt_c = col(jnp.broadcast_to(ssm_log_dt[0][:, None], (SSM_GROUPS, SSM_STATE)))
    br_c, bi_c = col(ssm_b_re[0]), col(ssm_b_im[0])
    a_re_c, a_im_c, bb_re_c, bb_im_c = _disc_call(lr_c, li_c, ldt_c, br_c, bi_c)
    a_re, a_im = a_re_c.reshape(1, SSM_NSTATE), a_im_c.reshape(1, SSM_NSTATE)
    to_bb = lambda b: _block_diag(b.reshape(SSM_GROUPS, SSM_STATE, SSM_GROUP).transpose(0, 2, 1), True).astype(BF16)
    bb_re, bb_im = to_bb(bb_re_c), to_bb(bb_im_c)
    to_cm = lambda c_: _block_diag(c_[0].transpose(0, 2, 1), True).astype(BF16)
    cm_re, cm_im = to_cm(ssm_c_re), to_cm(ssm_c_im)
    d_skip = ssm_d.reshape(1, SSM_WIDTH)
    u_p = _time_perm(proj[:, P_U:P_GATE], L)
    y1, s_re, s_im = _ssm_fwd_call(u_p, a_re, a_im, bb_re, bb_im, cm_re, cm_im, d_skip)
    w_glu_b = W["w_glu"]
    ssm_p = _glu_call(y1, w_glu_b, b_glu)
    ssm = _time_unperm(ssm_p, L)

    pa = _mm(attn, w_ba_p, "mm_ba")
    ps = _mm(ssm, W["w_branch_ssm"], "mm_bs")
    merged = _merge_call(proj, b_gate, pa, ps)
    wide = lambda dt: (D_MODEL, dt)
    o, x2, hn2 = _mm_rows(merged, W["w_out"], "mm_out", _post_mix_rows, [xs], [mix_norm_post, ffn_norm_pre],
                          [wide(F32), wide(F32), wide(BF16)], [])
    h = _mm(hn2, W["w_up"], "mm_up", tb=True, tn=1408)
    cw = W["conv_w"]
    act = _conv_act_call(h, cw, conv_b)
    dy, dff, loss_row, g_ffn_norm_post = _mm_rows(
        act, W["w_down"], "mm_down", _ffn_out_rows, [x2, target], [ffn_norm_post], [wide(F32), wide(BF16)],
        [LANES, D_MODEL], tk=1408)

    da = _mm(dff, W["w_down"], "mm_down_dx", tb=True, tn=1408)
    g_w_down = _mm_tn(act, dff, "mm_down_dw", tm=1408)
    dgate, dval, dcw_g, dcw_v, dcb_g, dcb_v = _conv_act_bwd_call(da, h, cw, conv_b)
    g_conv_w = jnp.concatenate([dcw_g, dcw_v], axis=1)
    g_conv_b = jnp.concatenate([dcb_g, dcb_v], axis=1)
    dh = _conv_t_call(dgate, dval, cw)
    dx2, do, g_ffn_norm_pre, g_mix_norm_post = _mm_rows(
        dh, W["w_up"], "mm_up_dx", _post_bwd_rows, [x2, dy, o], [ffn_norm_pre, mix_norm_post], [wide(F32), wide(BF16)],
        [D_MODEL, D_MODEL], tk=1408)
    g_w_up = _mm_tn(dh, hn2, "mm_up_dw", tm=1408)
    dmerged = _mm(do, W["w_out"], "mm_out_dx", tb=True)
    g_w_out = _mm_tn(merged, do, "mm_out_dw")
    dpa, dps, dl0, dl1, db0, db1 = _merge_bwd_call(dmerged, proj, b_gate, pa, ps)
    g_b_gate = jnp.concatenate([db0, db1], axis=1)
    dattn = _mm(dpa, w_ba_p, "mm_ba_dx", tb=True, out_dtype=BF16)
    g_w_ba = _mm_tn(attn, dpa, "mm_ba_dw").reshape(N_HEADS, LANES, D_MODEL)[:, :V_HEAD].reshape(N_HEADS * V_HEAD, D_MODEL)
    dssm = _mm(dps, W["w_branch_ssm"], "mm_bs_dx", tb=True)
    g_w_bs = _mm_tn(ssm, dps, "mm_bs_dw")

    dy1, g_w_glu, g_b_glu = _glu_bwd_call(_time_perm(dssm, L), y1, w_glu_b, b_glu)
    du_p, dbb_re, dbb_im, dcm_re, dcm_im, da_re, da_im, g_ssm_d = _ssm_bwd_call(
        dy1, u_p, s_re, s_im, a_re, a_im, bb_re, bb_im, cm_re, cm_im, d_skip)
    du = _time_unperm(du_p, L)
    from_bb = lambda m: col(_block_diag_t(m, SSM_GROUP, SSM_STATE).transpose(0, 2, 1))
    dlr, dli, dldt, dbr, dbi = _disc_bwd_call(
        lr_c, li_c, ldt_c, br_c, bi_c, da_re.reshape(SSM_NSTATE, 1), da_im.reshape(SSM_NSTATE, 1), from_bb(dbb_re), from_bb(dbb_im))
    g_c_re = _block_diag_t(dcm_re, SSM_STATE, SSM_GROUP).transpose(0, 2, 1)
    g_c_im = _block_diag_t(dcm_im, SSM_STATE, SSM_GROUP).transpose(0, 2, 1)

    def grad_slices(group, grads):
        return [_to_slices(name, grads[name], rows, cols, axis) for name, rows, cols, axis in group]

    early_grads = {"w_up": g_w_up, "w_down": g_w_down, "conv_w": g_conv_w, "w_glu": g_w_glu.astype(BF16),
                   "w_branch_attn": g_w_ba, "w_branch_ssm": g_w_bs, "w_out": g_w_out}
    b_stored = lambda d: d.reshape(SSM_GROUPS, SSM_STATE, SSM_GROUP).transpose(0, 2, 1)
    per_state = lambda d: d.reshape(SSM_GROUPS, SSM_STATE)
    ssm_partials = {"ssm_lambda_re": per_state(dlr), "ssm_lambda_im": per_state(dli),
                    "ssm_b_re": b_stored(dbr), "ssm_b_im": b_stored(dbi),
                    "ssm_c_re": g_c_re, "ssm_c_im": g_c_im, "ssm_d": g_ssm_d.reshape(SSM_GROUPS, SSM_GROUP).T}
    ssm_shapes = [(name, ssm_partials[name].shape) for name, _ in SMALL if name in ssm_partials]
    dq, dkv, *landed = _attn_bwd_call(
        q_r, kv_r, attn, dattn, lse, grad_slices(GRADS_EARLY, early_grads),
        [ssm_partials[name].reshape(-1, LANES) if len(shp) == 3 else ssm_partials[name].reshape((1,) + shp)
         for name, shp in ssm_shapes])
    received_early = landed[:len(GRADS_EARLY)]
    ssm_all = {name: got.reshape((N_DEV, 1) + shp) for (name, shp), got in zip(ssm_shapes, landed[len(GRADS_EARLY):])}
    dq_p, dkv_p, dlatent, g_q_norm, g_kv_norm = _mla_proj_bwd_call(
        dq, dkv, cosf, sinf, proj, q_norm, kv_norm, w_uq_pt, w_kv_pt)
    g_w_uq = _mm_tn(dq_p, qn, "mm_uq_dw").reshape(N_HEADS, LANES, Q_RANK)[:, :QK_HEAD].reshape(N_HEADS * QK_HEAD, Q_RANK)
    g_w_kv = _mm_tn(dkv_p, ckvn, "mm_ukv_dw").reshape(N_HEADS, 2, LANES, KV_RANK)
    g_w_uk = g_w_kv[:, 0, :QK_NOPE].reshape(N_HEADS * QK_NOPE, KV_RANK)
    g_w_uv = g_w_kv[:, 1, :V_HEAD].reshape(N_HEADS * V_HEAD, KV_RANK)
    dproj = jnp.concatenate([dlatent, du.astype(BF16), dl0, dl1], axis=1)
    g_w_in_pt = _mm_tn(dproj, hn1, "mm_in_dw", tm=1664)
    g_w_in = jnp.concatenate([g_w_in_pt[:P_KR], g_w_in_pt[P_KR + QK_NOPE:P_KR + QK_HEAD], g_w_in_pt[P_U:]], axis=0)
    late_grads = {"w_in": g_w_in, "w_uq": g_w_uq, "w_uk": g_w_uk, "w_uv": g_w_uv}
    grad_x, g_mix_norm_pre, *received_late = _mm_in_dx_call(
        dproj, w_in_pt, xs, dx2, mix_norm_pre, grad_slices(GRADS_LATE, late_grads))

    results = {}
    wmv = lambda name: tuple(_stored(name, given[prefix + name]) for prefix in ("", "m_", "v_"))
    unstored = lambda name, res: [_stored(name, r) for r in res]
    whole = ("w_uq", "w_uk", "w_uv", "w_glu", "w_branch_attn", "w_branch_ssm", "conv_w")
    landed_small = dict(ssm_all)
    for group, received in ((GRADS_EARLY, received_early), (GRADS_LATE, received_late)):
        for (name, _, _, _), rec in zip(group, received):
            if name in whole:
                landed_small[name] = rec[:, None]
            else:
                results[name] = unstored(name, _adam_call(rec[:, None], *wmv(name), "adam_" + name))

    vec_grads = {"mix_norm_pre": g_mix_norm_pre, "q_norm": g_q_norm, "kv_norm": g_kv_norm,
                 "ssm_log_dt": jnp.sum(dldt.reshape(SSM_GROUPS, SSM_STATE), axis=1),
                 "b_glu": g_b_glu, "b_gate": g_b_gate, "mix_norm_post": g_mix_norm_post,
                 "ffn_norm_pre": g_ffn_norm_pre, "ffn_norm_post": g_ffn_norm_post}
    vec_names = [name for name, _ in SMALL if name in vec_grads]
    width = max(shp[0] for name, shp in SMALL if name in vec_grads)
    rows = [jnp.pad(vec_grads[name].reshape(1, -1), ((0, 0), (0, width - vec_grads[name].size))) for name in vec_names]
    rows.append(jnp.pad(loss_row, ((0, 0), (0, width - LANES))))
    rows.append(jnp.zeros((-len(rows) % 8, width), F32))
    rows_all, landed_small["conv_b"] = _all_gather_call(
        [jnp.concatenate(rows, axis=0), g_conv_b], "gather_small_grads", direct=True)
    others = ["conv_b"] + [name for name, _ in ssm_shapes] + list(whole)
    small_results, loss = _adam_small_call(
        rows_all, [wmv(n) for n in vec_names], [landed_small[n] for n in others], [wmv(n) for n in others])
    for name, res in zip(vec_names + others, small_results):
        results[name] = unstored(name, res)

    order = ["mix_norm_pre", "w_in", "q_norm", "w_uq", "kv_norm", "w_uk", "w_uv", "ssm_lambda_re", "ssm_lambda_im",
             "ssm_log_dt", "ssm_b_re", "ssm_b_im", "ssm_c_re", "ssm_c_im", "ssm_d", "w_glu", "b_glu", "w_branch_attn",
             "w_branch_ssm", "b_gate", "w_out", "mix_norm_post", "ffn_norm_pre", "w_up", "conv_w", "conv_b", "w_down",
             "ffn_norm_post"]
    outs = [loss, grad_x[None]]
    for kind in range(4):
        outs += [results[name][kind] for name in order]
    return tuple(outs)
```

```python
import math

import jax
import jax.numpy as jnp
from jax import lax
from jax.experimental import pallas as pl
from jax.experimental.pallas import tpu as pltpu

F32 = jnp.float32
BF16 = jnp.bfloat16
MESH_ID = pl.DeviceIdType.MESH

N_DEV = 8
LANES = 128
D_MODEL = 1024
N_HEADS = 8
QK_NOPE = 64
QK_ROPE = 32
QK_HEAD = QK_NOPE + QK_ROPE
V_HEAD = 64
Q_RANK = 384
KV_RANK = 256
ROPE_THETA = 10000.0
SSM_WIDTH = 512
SSM_GROUP = 16
SSM_GROUPS = 32
SSM_STATE = 64
SSM_NSTATE = SSM_GROUPS * SSM_STATE
SSM_CHUNKS = 4
D_FF = 2816
EPS = 1e-6
ADAM_LR, ADAM_B1, ADAM_B2, ADAM_EPS, ADAM_WD, ADAM_STEP = 0.001, 0.9, 0.999, 1e-08, 0.01, 10

P_CQ, P_CKV, P_KR, P_U, P_GATE = 0, 384, 640, 768, 1280
HEAD_PAD = N_HEADS * LANES

VMEM_BIG = 52 * 1024 * 1024

_GELU_C0 = math.sqrt(2.0 / math.pi)
_GELU_C1 = 0.044715
NEG = -1e30


def _fit(n, pref, mult=LANES):
    if n <= pref:
        return n
    t = (pref // mult) * mult
    while t > 0 and n % t:
        t -= mult
    assert t > 0, (n, pref, mult)
    return t


def _gelu(x):
    return x * (0.5 * (1.0 + jnp.tanh(_GELU_C0 * x * (1.0 + _GELU_C1 * (x * x)))))


def _gelu_and_grad(x):
    x2 = x * x
    t = jnp.tanh(_GELU_C0 * x * (1.0 + _GELU_C1 * x2))
    half = 0.5 * (1.0 + t)
    return x * half, half + 0.5 * x * (1.0 - t * t) * _GELU_C0 * (1.0 + 3.0 * _GELU_C1 * x2)


def _sigmoid(x):
    return 1.0 / (1.0 + jnp.exp(-x))


def _dot(a, b, dims):
    return lax.dot_general(a, b, (dims, ((), ())), preferred_element_type=F32)


NN = ((1,), (0,))
NT = ((1,), (1,))
TN = ((0,), (0,))


def _params(*sem, vmem=None):
    return pltpu.CompilerParams(dimension_semantics=tuple(sem), vmem_limit_bytes=vmem)


def _mm(a, b, name, tb=False, out_dtype=F32, tm=1024, tn=1024, tk=1024, gather=()):
    M, K = a.shape
    if tb:
        N, K2 = b.shape
    else:
        K2, N = b.shape
    assert K == K2, (a.shape, b.shape, tb)
    tm, tn, tk = _fit(M, tm), _fit(N, tn), _fit(K, tk)
    nk = K // tk
    grid = (M // tm, N // tn, nk)
    steps = grid[0] * grid[1] * grid[2]
    dims = NT if tb else NN
    n = len(gather)

    def body(a_ref, b_ref, *refs):
        o_ref, scratch = refs[n], refs[2 * n + 1:]
        step = (pl.program_id(0) * grid[1] + pl.program_id(1)) * grid[2] + pl.program_id(2)
        if n:
            start, forward, finish = _gather_phases(refs[:n], refs[n + 1:2 * n + 1], *scratch[-3:])
            pl.when(step == 0)(start)
            pl.when(step == steps // 2)(forward)
        part = _dot(a_ref[...].astype(BF16), b_ref[...].astype(BF16), dims)
        if nk == 1:
            o_ref[...] = part.astype(out_dtype)
        else:
            acc_ref = scratch[0]
            k = pl.program_id(2)

            @pl.when(k == 0)
            def _():
                acc_ref[...] = part

            @pl.when(k > 0)
            def _():
                acc_ref[...] += part

            @pl.when(k == nk - 1)
            def _():
                o_ref[...] = acc_ref[...].astype(out_dtype)
        if n:
            pl.when(step == steps - 1)(finish)

    a_spec = pl.BlockSpec((tm, tk), lambda i, j, k: (i, k))
    b_spec = pl.BlockSpec((tn, tk), lambda i, j, k: (j, k)) if tb else pl.BlockSpec((tk, tn), lambda i, j, k: (k, j))
    landed = [jax.ShapeDtypeStruct((N_DEV,) + p.shape, p.dtype) for p in gather]
    out = pl.pallas_call(
        body, name=name, grid=grid,
        in_specs=[a_spec, b_spec] + [ANY_SPEC] * n,
        out_specs=[pl.BlockSpec((tm, tn), lambda i, j, k: (i, j))] + [ANY_SPEC] * n,
        out_shape=[jax.ShapeDtypeStruct((M, N), out_dtype)] + landed,
        scratch_shapes=([] if nk == 1 else [pltpu.VMEM((tm, tn), F32)]) + (_comm_sems(n) if n else []),
        compiler_params=_params(*(("arbitrary",) * 3 if n else ("parallel", "parallel", "arbitrary")), vmem=VMEM_BIG),
    )(a, b, *gather)
    return out if n else out[0]


def _mm_rows(a, b, name, epilogue, rows_in, vecs_in, rows_out, vecs_out, tb=False, tk=1024):
    M, K = a.shape
    N = b.shape[0] if tb else b.shape[1]
    tm, tk = _fit(M, 512), _fit(K, tk)
    nk = K // tk
    nr, nv, nro = len(rows_in), len(vecs_in), len(rows_out)

    def body(a_ref, b_ref, *refs):
        ins, outs, acc_ref = refs[:nr + nv], refs[nr + nv:nr + nv + nro + len(vecs_out)], refs[-1]
        i, k = pl.program_id(0), pl.program_id(1)
        part = _dot(a_ref[...], b_ref[...], NT if tb else NN)

        def finish(product):
            res = epilogue(product, *[r[...] for r in ins])
            for ref, val in zip(outs[:nro], res[:nro]):
                ref[...] = val.astype(ref.dtype)
            for ref, val in zip(outs[nro:], res[nro:]):
                _acc(ref, i == 0, val)

        if nk == 1:
            finish(part)
        else:
            @pl.when(k == 0)
            def _():
                acc_ref[...] = part

            @pl.when(jnp.logical_and(k > 0, k < nk - 1))
            def _():
                acc_ref[...] += part

            @pl.when(k == nk - 1)
            def _():
                finish(acc_ref[...] + part)

    row = lambda w: pl.BlockSpec((tm, w), lambda i, k: (i, 0))
    vec = lambda w: pl.BlockSpec((1, w), lambda i, k: (0, 0))
    b_spec = pl.BlockSpec((N, tk), lambda i, k: (0, k)) if tb else pl.BlockSpec((tk, N), lambda i, k: (k, 0))
    return pl.pallas_call(
        body, name=name, grid=(M // tm, nk),
        in_specs=[pl.BlockSpec((tm, tk), lambda i, k: (i, k)), b_spec] + [row(r.shape[1]) for r in rows_in]
        + [vec(v.shape[1]) for v in vecs_in],
        out_specs=[row(w) for w, _ in rows_out] + [vec(w) for w in vecs_out],
        out_shape=[jax.ShapeDtypeStruct((M, w), dt) for w, dt in rows_out] + [jax.ShapeDtypeStruct((1, w), F32) for w in vecs_out],
        scratch_shapes=[pltpu.VMEM((tm, N), F32)],
        compiler_params=_params("arbitrary", "arbitrary", vmem=VMEM_BIG))(a, b, *rows_in, *vecs_in)


def _mm_in_dx_call(dproj, w_in_pt, x, dx2, g_pre, exchange):
    L, K = dproj.shape
    N = w_in_pt.shape[1]
    tm, tk = _fit(L, 512), _fit(K, 1664)
    nm, nk = L // tm, K // tk
    n = len(exchange)

    def body(a_ref, b_ref, x_ref, dx2_ref, g_ref, *refs):
        parts, (gx_ref, dg_ref), got = refs[:n], refs[n:n + 2], refs[n + 2:2 * n + 2]
        acc_ref = refs[2 * n + 2]
        i, k = pl.program_id(0), pl.program_id(1)
        start, finish = _exchange_phases(parts, got, *refs[2 * n + 3:])
        pl.when(jnp.logical_and(i == 0, k == 0))(start)
        part = _dot(a_ref[...], b_ref[...], NN)

        @pl.when(k == 0)
        def _():
            acc_ref[...] = part

        @pl.when(jnp.logical_and(k > 0, k < nk - 1))
        def _():
            acc_ref[...] += part

        @pl.when(k == nk - 1)
        def _():
            d1, dg = _rms_bwd(x_ref[...], g_ref[...], acc_ref[...] + part)
            gx_ref[...] = dx2_ref[...] + d1
            _acc(dg_ref, i == 0, dg)

        pl.when(jnp.logical_and(i == nm - 1, k == nk - 1))(finish)

    assert nk >= 2
    rows = lambda: pl.BlockSpec((tm, N), lambda i, k: (i, 0))
    return pl.pallas_call(
        body, name="mm_in_dx", grid=(nm, nk),
        in_specs=[pl.BlockSpec((tm, tk), lambda i, k: (i, k)), pl.BlockSpec((tk, N), lambda i, k: (k, 0)),
                  rows(), rows(), pl.BlockSpec((1, N), lambda i, k: (0, 0))] + [ANY_SPEC] * n,
        out_specs=[rows(), pl.BlockSpec((1, N), lambda i, k: (0, 0))] + [ANY_SPEC] * n,
        out_shape=[jax.ShapeDtypeStruct((L, N), F32), jax.ShapeDtypeStruct((1, N), F32)]
        + [jax.ShapeDtypeStruct(p.shape, p.dtype) for p in exchange],
        scratch_shapes=[pltpu.VMEM((tm, N), F32)] + _comm_sems(n),
        compiler_params=_params("arbitrary", "arbitrary", vmem=VMEM_BIG))(dproj, w_in_pt, x, dx2, g_pre, *exchange)


TN_CHUNK = 512


def _mm_tn(a, b, name, tm=512, tk=512):
    K, M = a.shape
    K2, N = b.shape
    assert K == K2, (a.shape, b.shape)
    tm, tk, cn = _fit(M, tm), _fit(K, tk), _fit(N, TN_CHUNK)
    nk = K // tk

    def body(a_ref, b_ref, o_ref, acc_ref):
        k = pl.program_id(1)

        @pl.when(k == 0)
        def _():
            acc_ref[...] = jnp.zeros((tm, N), F32)

        at = a_ref[...].astype(BF16).T
        for c in range(N // cn):
            cols = slice(c * cn, (c + 1) * cn)
            acc_ref[:, cols] += _dot(at, b_ref[:, cols].astype(BF16), NN)

        @pl.when(k == nk - 1)
        def _():
            o_ref[...] = acc_ref[...].astype(BF16)

    return pl.pallas_call(
        body, name=name, grid=(M // tm, nk),
        in_specs=[pl.BlockSpec((tk, tm), lambda i, k: (k, i)), pl.BlockSpec((tk, N), lambda i, k: (k, 0))],
        out_specs=pl.BlockSpec((tm, N), lambda i, k: (i, 0)),
        out_shape=jax.ShapeDtypeStruct((M, N), BF16),
        scratch_shapes=[pltpu.VMEM((tm, N), F32)],
        compiler_params=_params("parallel", "arbitrary", vmem=VMEM_BIG))(a, b)


def _row(tl, n, col=0):
    return pl.BlockSpec((tl, n), lambda i: (i, col))


def _full(shape):
    return pl.BlockSpec(shape, lambda i: (0,) * len(shape))


def _rms(x, g):
    r = lax.rsqrt(jnp.mean(x * x, axis=-1, keepdims=True) + EPS)
    return x * r * g


def _rms_bwd(x, g, dy):
    n = x.shape[-1]
    r = lax.rsqrt(jnp.mean(x * x, axis=-1, keepdims=True) + EPS)
    gy = dy * g
    dx = r * gy - x * (r * r * r * (1.0 / n)) * jnp.sum(x * gy, axis=-1, keepdims=True)
    return dx, jnp.sum(dy * x * r, axis=0, keepdims=True)


def _acc(ref, first, val):
    @pl.when(first)
    def _():
        ref[...] = val

    @pl.when(jnp.logical_not(first))
    def _():
        ref[...] += val


def _rms_fwd_call(x, g, name, gather):
    L, n = x.shape
    tl = _fit(L, 512)
    steps, na = L // tl, len(gather)

    def body(x_ref, g_ref, *refs):
        o_ref = refs[na]
        start, forward, finish = _gather_phases(refs[:na], refs[na + 1:2 * na + 1], *refs[2 * na + 1:])
        i = pl.program_id(0)
        pl.when(i == 0)(start)
        pl.when(i == steps // 2)(forward)
        o_ref[...] = _rms(x_ref[...], g_ref[...]).astype(BF16)
        pl.when(i == steps - 1)(finish)

    return pl.pallas_call(
        body, name=name, grid=(steps,), in_specs=[_row(tl, n), _full((1, n))] + [ANY_SPEC] * na,
        out_specs=[_row(tl, n)] + [ANY_SPEC] * na,
        out_shape=[jax.ShapeDtypeStruct((L, n), BF16)] + [jax.ShapeDtypeStruct((N_DEV,) + b.shape, b.dtype) for b in gather],
        scratch_shapes=_comm_sems(na), compiler_params=_params("arbitrary"))(x, g, *gather)


def _rope_lanes(shape):
    lane = lax.broadcasted_iota(jnp.int32, shape, 1)
    return lane, jnp.logical_and(lane >= QK_NOPE, lane < QK_HEAD)


def _rope_apply(x, cosf, sinf, lane):
    rot = jnp.where(lane < QK_NOPE + QK_ROPE // 2, -pltpu.roll(x, LANES - QK_ROPE // 2, 1), pltpu.roll(x, QK_ROPE // 2, 1))
    return x * cosf + rot * sinf


def _rope_apply_t(dy, cosf, sinf, lane, is_rope):
    g = dy * sinf
    rot_t = jnp.where(lane < QK_NOPE + QK_ROPE // 2, pltpu.roll(g, LANES - QK_ROPE // 2, 1), -pltpu.roll(g, QK_ROPE // 2, 1))
    return dy * cosf + jnp.where(is_rope, rot_t, 0.0)


def _mla_proj_call(proj, q_norm, kv_norm, w_uq_pt, w_kv_pt, pos_col, inv_freq):
    L = proj.shape[0]
    tl = _fit(L, 512)

    def body(p_ref, gq_ref, gk_ref, wq_ref, wkv_ref, pos_ref, f_ref, qn_ref, kn_ref, qo_ref, kvo_ref, cos_ref, sin_ref):
        qn = _rms(p_ref[:, P_CQ:P_CKV], gq_ref[...]).astype(BF16)
        kn = _rms(p_ref[:, P_CKV:P_KR], gk_ref[...]).astype(BF16)
        qn_ref[...] = qn
        kn_ref[...] = kn
        q_pad = _dot(qn, wq_ref[...], NT)
        kv_pad = _dot(kn, wkv_ref[...], NT)
        lane, is_rope = _rope_lanes((tl, LANES))
        ang = pos_ref[...] * f_ref[...]
        cosf = jnp.where(is_rope, jnp.cos(ang), jnp.where(lane < QK_NOPE, 1.0, 0.0))
        sinf = jnp.where(is_rope, jnp.sin(ang), 0.0)
        cos_ref[...] = cosf
        sin_ref[...] = sinf
        kr = _rope_apply(p_ref[:, P_KR:P_U], cosf, sinf, lane)
        for h in range(N_HEADS):
            qh = _rope_apply(q_pad[:, h * LANES:(h + 1) * LANES], cosf, sinf, lane)
            qo_ref[:, h * LANES:(h + 1) * LANES] = (qh * Q_PRESCALE).astype(BF16)
            kvo_ref[:, 2 * h * LANES:(2 * h + 1) * LANES] = (kv_pad[:, 2 * h * LANES:(2 * h + 1) * LANES] + kr).astype(BF16)
            vh = jnp.where(lane == V_HEAD, 1.0, kv_pad[:, (2 * h + 1) * LANES:(2 * h + 2) * LANES])
            kvo_ref[:, (2 * h + 1) * LANES:(2 * h + 2) * LANES] = vh.astype(BF16)

    shape = lambda n, dt: jax.ShapeDtypeStruct((L, n), dt)
    return pl.pallas_call(
        body, name="mla_proj", grid=(L // tl,),
        in_specs=[_row(tl, P_U), _full((1, Q_RANK)), _full((1, KV_RANK)), _full((HEAD_PAD, Q_RANK)),
                  _full((2 * HEAD_PAD, KV_RANK)), _row(tl, 1), _full((1, LANES))],
        out_specs=[_row(tl, Q_RANK), _row(tl, KV_RANK), _row(tl, HEAD_PAD), _row(tl, 2 * HEAD_PAD), _row(tl, LANES), _row(tl, LANES)],
        out_shape=[shape(Q_RANK, BF16), shape(KV_RANK, BF16), shape(HEAD_PAD, BF16), shape(2 * HEAD_PAD, BF16),
                   shape(LANES, F32), shape(LANES, F32)],
        compiler_params=_params("parallel"))(proj, q_norm, kv_norm, w_uq_pt, w_kv_pt, pos_col, inv_freq)


def _mla_proj_bwd_call(dq, dkv, cosf, sinf, proj, q_norm, kv_norm, w_uq_pt, w_kv_pt):
    L = dq.shape[0]
    tl = _fit(L, 512)

    def body(dq_ref, dkv_ref, cos_ref, sin_ref, p_ref, gq_ref, gk_ref, wq_ref, wkv_ref,
             dqo_ref, dkvo_ref, d_ref, dgq_ref, dgk_ref):
        first = pl.program_id(0) == 0
        lane, is_rope = _rope_lanes((tl, LANES))
        cosf, sinf = cos_ref[...], sin_ref[...]
        dk_sum = jnp.zeros((tl, LANES), F32)
        for h in range(N_HEADS):
            dqo_ref[:, h * LANES:(h + 1) * LANES] = _rope_apply_t(dq_ref[:, h * LANES:(h + 1) * LANES], cosf, sinf, lane, is_rope).astype(BF16)
            dk_sum = dk_sum + dkv_ref[:, 2 * h * LANES:(2 * h + 1) * LANES]
        dkvo_ref[...] = dkv_ref[...].astype(BF16)
        dqn = _dot(dqo_ref[...], wq_ref[...], NN)
        dkn = _dot(dkvo_ref[...], wkv_ref[...], NN)
        dcq, dgq = _rms_bwd(p_ref[:, P_CQ:P_CKV], gq_ref[...], dqn)
        dckv, dgk = _rms_bwd(p_ref[:, P_CKV:P_KR], gk_ref[...], dkn)
        d_ref[:, P_CQ:P_CKV] = dcq.astype(BF16)
        d_ref[:, P_CKV:P_KR] = dckv.astype(BF16)
        d_ref[:, P_KR:P_U] = _rope_apply_t(dk_sum, cosf, sinf, lane, is_rope).astype(BF16)
        _acc(dgq_ref, first, dgq)
        _acc(dgk_ref, first, dgk)

    shape = lambda n: jax.ShapeDtypeStruct((L, n), BF16)
    return pl.pallas_call(
        body, name="mla_proj_bwd", grid=(L // tl,),
        in_specs=[_row(tl, HEAD_PAD), _row(tl, 2 * HEAD_PAD), _row(tl, LANES), _row(tl, LANES), _row(tl, P_KR),
                  _full((1, Q_RANK)), _full((1, KV_RANK)), _full((HEAD_PAD, Q_RANK)), _full((2 * HEAD_PAD, KV_RANK))],
        out_specs=[_row(tl, HEAD_PAD), _row(tl, 2 * HEAD_PAD), _row(tl, P_U), _full((1, Q_RANK)), _full((1, KV_RANK))],
        out_shape=[shape(HEAD_PAD), shape(2 * HEAD_PAD), shape(P_U), jax.ShapeDtypeStruct((1, Q_RANK), F32),
                   jax.ShapeDtypeStruct((1, KV_RANK), F32)],
        compiler_params=_params("arbitrary"))(dq, dkv, cosf, sinf, proj, q_norm, kv_norm, w_uq_pt, w_kv_pt)


GATE_TILE = 256
GATE_ROWS = 1024


def _merge_call(proj, b_gate, pa, ps):
    L = proj.shape[0]
    tl = _fit(L, GATE_ROWS)
    nc = D_MODEL // GATE_TILE
    g0, g1 = P_GATE // GATE_TILE, (P_GATE + D_MODEL) // GATE_TILE

    def body(l0_ref, l1_ref, b0_ref, b1_ref, pa_ref, ps_ref, o_ref):
        s0 = _sigmoid(l0_ref[...] + b0_ref[...])
        s1 = _sigmoid(l1_ref[...] + b1_ref[...])
        o_ref[...] = (s0 * pa_ref[...] + s1 * ps_ref[...]).astype(BF16)

    blk = lambda off: pl.BlockSpec((tl, GATE_TILE), lambda i, j: (i, off + j))
    bias = lambda off: pl.BlockSpec((1, GATE_TILE), lambda i, j: (0, off + j))
    return pl.pallas_call(
        body, name="merge", grid=(L // tl, nc),
        in_specs=[blk(g0), blk(g1), bias(0), bias(nc), blk(0), blk(0)],
        out_specs=blk(0), out_shape=jax.ShapeDtypeStruct((L, D_MODEL), BF16),
        compiler_params=_params("parallel", "parallel"))(proj, proj, b_gate, b_gate, pa, ps)


def _merge_bwd_call(dm, proj, b_gate, pa, ps):
    L = proj.shape[0]
    tl = _fit(L, GATE_ROWS)
    nc = D_MODEL // GATE_TILE
    g0, g1 = P_GATE // GATE_TILE, (P_GATE + D_MODEL) // GATE_TILE

    def body(dm_ref, l0_ref, l1_ref, b0_ref, b1_ref, pa_ref, ps_ref, dpa_ref, dps_ref, dl0_ref, dl1_ref, db0_ref, db1_ref):
        first = pl.program_id(1) == 0
        dm_ = dm_ref[...]
        s0 = _sigmoid(l0_ref[...] + b0_ref[...])
        s1 = _sigmoid(l1_ref[...] + b1_ref[...])
        dpa_ref[...] = (dm_ * s0).astype(BF16)
        dps_ref[...] = (dm_ * s1).astype(BF16)
        dl0 = dm_ * pa_ref[...] * s0 * (1.0 - s0)
        dl1 = dm_ * ps_ref[...] * s1 * (1.0 - s1)
        dl0_ref[...] = dl0.astype(BF16)
        dl1_ref[...] = dl1.astype(BF16)
        _acc(db0_ref, first, jnp.sum(dl0, axis=0, keepdims=True))
        _acc(db1_ref, first, jnp.sum(dl1, axis=0, keepdims=True))

    blk = lambda off: pl.BlockSpec((tl, GATE_TILE), lambda j, i: (i, off + j))
    bias = lambda off: pl.BlockSpec((1, GATE_TILE), lambda j, i: (0, off + j))
    act = jax.ShapeDtypeStruct((L, D_MODEL), BF16)
    vec = jax.ShapeDtypeStruct((1, D_MODEL), F32)
    return pl.pallas_call(
        body, name="merge_bwd", grid=(nc, L // tl),
        in_specs=[blk(0), blk(g0), blk(g1), bias(0), bias(nc), blk(0), blk(0)],
        out_specs=[blk(0), blk(0), blk(0), blk(0), bias(0), bias(0)],
        out_shape=[act, act, act, act, vec, vec],
        compiler_params=_params("parallel", "arbitrary"))(dm, proj, proj, b_gate, b_gate, pa, ps)


def _post_mix_rows(o, x, g_post, g_fpre):
    x2 = x + _rms(o, g_post)
    return o, x2, _rms(x2, g_fpre)


def _ffn_out_rows(ff, x2, target, g_fpost):
    n = ff.shape[-1]
    err = x2 + _rms(ff, g_fpost) - target
    part = 0.5 * jnp.sum(jnp.sum(err * err, axis=-1, keepdims=True) * (1.0 / n), axis=0, keepdims=True)
    dy = err * (1.0 / n)
    dff, dg = _rms_bwd(ff, g_fpost, dy)
    return dy, dff, jnp.broadcast_to(part, (1, LANES)), dg


def _post_bwd_rows(dhn2, x2, dy, o, g_fpre, g_post):
    d1, dgf = _rms_bwd(x2, g_fpre, dhn2)
    dx2 = dy + d1
    do, dgp = _rms_bwd(o, g_post, dx2)
    return dx2, do, dgf, dgp


CONV_TILE = 256
CONV_WIDE = 1408
HALO = 16


def _conv3(w, b, x0, x1, x2):
    return b + w[2:3] * x0 + w[1:2] * x1 + w[0:1] * x2


def _down(x, by):
    return pltpu.roll(x, by, 0)


def _edge_down(edge, before, by):
    r = lax.broadcasted_iota(jnp.int32, edge.shape, 0)
    return jnp.where(r < by, pltpu.roll(before, by, 0), pltpu.roll(edge, by, 0))


def _edge_up(edge, after, by):
    r = lax.broadcasted_iota(jnp.int32, edge.shape, 0)
    return jnp.where(r >= HALO - by, pltpu.roll(after, HALO - by, 0), pltpu.roll(edge, HALO - by, 0))


def _gated(w_g, b_g, w_v, b_v, hg, hv, g1, g2, v1, v2):
    return _conv3(w_g, b_g, hg, g1, g2), _conv3(w_v, b_v, hv, v1, v2)


def _conv_specs(tl, tc, rows_inner):
    nh = tl // HALO
    if rows_inner:
        ij = lambda f: (lambda j, i: f(i, j))
    else:
        ij = lambda f: f
    cur = lambda off: pl.BlockSpec((tl, tc), ij(lambda i, j: (i, off + j)))
    prev = lambda off: pl.BlockSpec((HALO, tc), ij(lambda i, j: (jnp.maximum(i * nh - 1, 0), off + j)))
    par = lambda rows, off: pl.BlockSpec((rows, tc), ij(lambda i, j: (0, off + j)))
    return cur, prev, par


def _conv_act_call(h, conv_w, conv_b):
    L = h.shape[0]
    tl = _fit(L, 256)
    nc = D_FF // CONV_WIDE
    cur, prev, par = _conv_specs(tl, CONV_WIDE, False)

    def body(hg_ref, hv_ref, pg_ref, pv_ref, wg_ref, wv_ref, bg_ref, bv_ref, a_ref):
        not_first = (pl.program_id(0) > 0).astype(F32)
        par = (wg_ref[...], bg_ref[...], wv_ref[...], bv_ref[...])
        hg, hv = hg_ref[...], hv_ref[...]
        gate, val = _gated(*par, hg, hv, _down(hg, 1), _down(hg, 2), _down(hv, 1), _down(hv, 2))
        a_ref[...] = (_gelu(gate) * val).astype(BF16)
        eg, ev, bg, bv = hg[:HALO], hv[:HALO], pg_ref[...] * not_first, pv_ref[...] * not_first
        gate, val = _gated(*par, eg, ev, _edge_down(eg, bg, 1), _edge_down(eg, bg, 2),
                           _edge_down(ev, bv, 1), _edge_down(ev, bv, 2))
        a_ref[:HALO, :] = (_gelu(gate) * val).astype(BF16)

    return pl.pallas_call(
        body, name="conv_act", grid=(L // tl, nc),
        in_specs=[cur(0), cur(nc), prev(0), prev(nc), par(3, 0), par(3, nc), par(1, 0), par(1, nc)],
        out_specs=cur(0), out_shape=jax.ShapeDtypeStruct((L, D_FF), BF16),
        compiler_params=_params("parallel", "parallel"))(h, h, h, h, conv_w, conv_w, conv_b, conv_b)


def _conv_act_bwd_call(da, h, conv_w, conv_b):
    L = h.shape[0]
    tl = _fit(L, 512)
    nc = D_FF // CONV_TILE
    cur, prev, par = _conv_specs(tl, CONV_TILE, True)

    def body(da_ref, hg_ref, hv_ref, pg_ref, pv_ref, wg_ref, wv_ref, bg_ref, bv_ref,
             dg_ref, dv_ref, dwg_ref, dwv_ref, dbg_ref, dbv_ref):
        first = pl.program_id(1) == 0
        not_first = (pl.program_id(1) > 0).astype(F32)
        par = (wg_ref[...], bg_ref[...], wv_ref[...], bv_ref[...])
        col = lambda t: jnp.sum(t, axis=0, keepdims=True)

        def grads(da_, hg, hv, g1, g2, v1, v2):
            gate, val = _gated(*par, hg, hv, g1, g2, v1, v2)
            act, slope = _gelu_and_grad(gate)
            dgate = da_ * val * slope
            dval = da_ * act
            sums = (jnp.concatenate([col(dgate * g2), col(dgate * g1), col(dgate * hg)], axis=0),
                    jnp.concatenate([col(dval * v2), col(dval * v1), col(dval * hv)], axis=0), col(dgate), col(dval))
            return dgate, dval, sums

        da_, hg, hv = da_ref[...], hg_ref[...], hv_ref[...]
        shifted = (_down(hg, 1), _down(hg, 2), _down(hv, 1), _down(hv, 2))
        dgate, dval, whole = grads(da_, hg, hv, *shifted)
        dg_ref[...] = dgate.astype(BF16)
        dv_ref[...] = dval.astype(BF16)
        edge = lambda t: t[:HALO]
        _, _, wrapped = grads(edge(da_), edge(hg), edge(hv), *[edge(s) for s in shifted])
        eg, ev, bg, bv = edge(hg), edge(hv), pg_ref[...] * not_first, pv_ref[...] * not_first
        dgate, dval, fixed = grads(edge(da_), eg, ev, _edge_down(eg, bg, 1), _edge_down(eg, bg, 2),
                                   _edge_down(ev, bv, 1), _edge_down(ev, bv, 2))
        dg_ref[:HALO, :] = dgate.astype(BF16)
        dv_ref[:HALO, :] = dval.astype(BF16)
        for ref, a, b, c in zip((dwg_ref, dwv_ref, dbg_ref, dbv_ref), whole, wrapped, fixed):
            _acc(ref, first, a - b + c)

    act = jax.ShapeDtypeStruct((L, D_FF), BF16)
    w3 = jax.ShapeDtypeStruct((3, D_FF), F32)
    w1 = jax.ShapeDtypeStruct((1, D_FF), F32)
    return pl.pallas_call(
        body, name="conv_act_bwd", grid=(nc, L // tl),
        in_specs=[cur(0), cur(0), cur(nc), prev(0), prev(nc), par(3, 0), par(3, nc), par(1, 0), par(1, nc)],
        out_specs=[cur(0), cur(0), par(3, 0), par(3, 0), par(1, 0), par(1, 0)],
        out_shape=[act, act, w3, w3, w1, w1],
        compiler_params=_params("parallel", "arbitrary"))(da, h, h, h, h, conv_w, conv_w, conv_b, conv_b)


def _conv_t_call(dgate, dval, conv_w):
    L = dgate.shape[0]
    tl = _fit(L, 512)
    nc = D_FF // CONV_WIDE
    nh = tl // HALO

    def body(dg_ref, dv_ref, ng_ref, nv_ref, w_ref, o_ref):
        not_last = (pl.program_id(0) < L // tl - 1).astype(F32)

        def emit(d_ref, n_ref):
            c = d_ref[...].astype(F32)
            w = w_ref[...]
            o_ref[...] = _conv3(w, 0.0, c, pltpu.roll(c, tl - 1, 0), pltpu.roll(c, tl - 2, 0)).astype(BF16)
            edge, after = c[tl - HALO:], n_ref[...].astype(F32) * not_last
            o_ref[tl - HALO:, :] = _conv3(w, 0.0, edge, _edge_up(edge, after, 1), _edge_up(edge, after, 2)).astype(BF16)

        pl.when(pl.program_id(1) < nc)(lambda: emit(dg_ref, ng_ref))
        pl.when(pl.program_id(1) >= nc)(lambda: emit(dv_ref, nv_ref))

    gate_col = lambda j: jnp.minimum(j, nc - 1)
    val_col = lambda j: jnp.maximum(j - nc, 0)
    after_row = lambda i: jnp.minimum((i + 1) * nh, L // HALO - 1)
    tile = lambda col: pl.BlockSpec((tl, CONV_WIDE), lambda i, j: (i, col(j)))
    after = lambda col: pl.BlockSpec((HALO, CONV_WIDE), lambda i, j: (after_row(i), col(j)))
    return pl.pallas_call(
        body, name="conv_t", grid=(L // tl, 2 * nc),
        in_specs=[tile(gate_col), tile(val_col), after(gate_col), after(val_col), pl.BlockSpec((3, CONV_WIDE), lambda i, j: (0, j))],
        out_specs=pl.BlockSpec((tl, CONV_WIDE), lambda i, j: (i, j)),
        out_shape=jax.ShapeDtypeStruct((L, 2 * D_FF), BF16),
        compiler_params=_params("parallel", "parallel"))(dgate, dval, dgate, dval, conv_w)


def _glu_call(y1, w_glu, b_glu):
    L, n = y1.shape
    tl = _fit(L, 512)

    def body(y_ref, w_ref, b_ref, o_ref):
        y2 = _gelu(y_ref[...])
        z = _dot(y2.astype(BF16), w_ref[...], NN) + b_ref[...]
        o_ref[...] = (y2 * _sigmoid(z)).astype(BF16)

    return pl.pallas_call(
        body, name="glu", grid=(L // tl,), in_specs=[_row(tl, n), _full((n, n)), _full((1, n))],
        out_specs=_row(tl, n), out_shape=jax.ShapeDtypeStruct((L, n), BF16),
        compiler_params=_params("parallel"))(y1, w_glu, b_glu)


def _glu_bwd_call(dout, y1, w_glu, b_glu):
    L, n = y1.shape
    tl = _fit(L, 512)

    def body(do_ref, y_ref, w_ref, b_ref, dy_ref, dw_ref, db_ref):
        first = pl.program_id(0) == 0
        y1_ = y_ref[...]
        y2, slope = _gelu_and_grad(y1_)
        y2b = y2.astype(BF16)
        w = w_ref[...]
        sg = _sigmoid(_dot(y2b, w, NN) + b_ref[...])
        dout_ = do_ref[...].astype(F32)
        dz = dout_ * y2 * sg * (1.0 - sg)
        dzb = dz.astype(BF16)
        dy2 = dout_ * sg + _dot(dzb, w, NT)
        dy_ref[...] = dy2 * slope
        _acc(dw_ref, first, _dot(y2b, dzb, TN))
        _acc(db_ref, first, jnp.sum(dz, axis=0, keepdims=True))

    return pl.pallas_call(
        body, name="glu_bwd", grid=(L // tl,),
        in_specs=[_row(tl, n), _row(tl, n), _full((n, n)), _full((1, n))],
        out_specs=[_row(tl, n), _full((n, n)), _full((1, n))],
        out_shape=[jax.ShapeDtypeStruct((L, n), F32), jax.ShapeDtypeStruct((n, n), F32), jax.ShapeDtypeStruct((1, n), F32)],
        compiler_params=_params("arbitrary"))(dout, y1, w_glu, b_glu)


ATTN_TILE = 1024
ATTN_SCALE = 1.0 / math.sqrt(QK_HEAD)


ATTN_HEADS = 2
ATTN_GROUPS = N_HEADS // ATTN_HEADS
LOG2E = 1.0 / math.log(2.0)
Q_PRESCALE = ATTN_SCALE * LOG2E
ANY_SPEC = pl.BlockSpec(memory_space=pl.ANY)


def _attn_fwd_call(q, kv, blocks):
    L = q.shape[0]
    t = _fit(L, ATTN_TILE)
    nq = L // t
    n = len(blocks)

    def body(q_ref, kv_ref, *refs):
        blk_refs, (o_ref, lse_ref), gat_refs = refs[:n], refs[n:n + 2], refs[n + 2:2 * n + 2]
        m_s, acc_s, send_sems, recv_sems, local_sems = refs[2 * n + 2:]
        g, i = pl.program_id(0), pl.program_id(1)
        start, forward, finish = _gather_phases(blk_refs, gat_refs, send_sems, recv_sems, local_sems)
        pl.when(jnp.logical_and(g == 0, i == 0))(start)
        m_s[...] = jnp.full((ATTN_HEADS, t, 1), NEG, F32)
        acc_s[...] = jnp.zeros((ATTN_HEADS, t, LANES), F32)
        below = lax.broadcasted_iota(jnp.int32, (t, t), 1) <= lax.broadcasted_iota(jnp.int32, (t, t), 0)

        def block_step(kb, on_diagonal):
            rows = pl.ds(pl.multiple_of(kb * t, t), t)
            for a in range(ATTN_HEADS):
                s = _dot(q_ref[:, a * LANES:(a + 1) * LANES], kv_ref[rows, 2 * a * LANES:(2 * a + 1) * LANES], NT)
                if on_diagonal:
                    s = jnp.where(below, s, NEG)
                m_prev = m_s[a]
                m_new = jnp.maximum(m_prev, jnp.max(s, axis=1, keepdims=True))
                p = jnp.exp2(s - m_new)
                pv = _dot(p.astype(BF16), kv_ref[rows, (2 * a + 1) * LANES:(2 * a + 2) * LANES], NN)
                acc_s[a] = jnp.exp2(m_prev - m_new) * acc_s[a] + pv
                m_s[a] = m_new

        def step(kb, carry):
            block_step(kb, False)
            return carry

        lax.fori_loop(0, i, step, 0)
        block_step(i, True)
        lane = lax.broadcasted_iota(jnp.int32, (t, LANES), 1)
        for a in range(ATTN_HEADS):
            acc = acc_s[a]
            l = jnp.sum(jnp.where(lane == V_HEAD, acc, 0.0), axis=1, keepdims=True)
            o_ref[:, a * LANES:(a + 1) * LANES] = (acc / l).astype(BF16)
            lse_ref[a] = m_s[a] + jnp.log(l) * LOG2E
        pl.when(jnp.logical_and(g == (3 * ATTN_GROUPS) // 4, i == 0))(forward)
        pl.when(jnp.logical_and(g == ATTN_GROUPS - 1, i == nq - 1))(finish)

    gw = ATTN_HEADS * LANES
    return pl.pallas_call(
        body, name="attn_fwd", grid=(ATTN_GROUPS, nq),
        in_specs=[pl.BlockSpec((t, gw), lambda g, i: (i, g)),
                  pl.BlockSpec((L, 2 * gw), lambda g, i: (0, g))] + [ANY_SPEC] * n,
        out_specs=[pl.BlockSpec((t, gw), lambda g, i: (i, g)),
                   pl.BlockSpec((ATTN_HEADS, t, 1), lambda g, i: (g, i, 0))] + [ANY_SPEC] * n,
        out_shape=[jax.ShapeDtypeStruct((L, HEAD_PAD), BF16), jax.ShapeDtypeStruct((N_HEADS, L, 1), F32)]
        + [jax.ShapeDtypeStruct((N_DEV,) + b.shape, b.dtype) for b in blocks],
        scratch_shapes=[pltpu.VMEM((ATTN_HEADS, t, 1), F32), pltpu.VMEM((ATTN_HEADS, t, LANES), F32)] + _comm_sems(n),
        compiler_params=_params("arbitrary", "arbitrary", vmem=VMEM_BIG))(q, kv, *blocks)


def _attn_bwd_call(q, kv, o, do, lse, parts, blocks):
    L = q.shape[0]
    t = _fit(L, ATTN_TILE)
    nq = L // t
    n1, n = len(parts), len(parts) + len(blocks)

    def body(q_ref, do_ref, o_ref, lse_ref, kv_ref, *refs):
        in_refs, (dq_ref, dkv_ref), out_refs = refs[:n], refs[n:n + 2], refs[n + 2:2 * n + 2]
        dk_s, dv_s = refs[2 * n + 2:2 * n + 4]
        g, j = pl.program_id(0), pl.program_id(1)
        start, finish = _exchange_phases(in_refs[:n1], out_refs[:n1], *refs[2 * n + 4:2 * n + 7])
        start_blocks, finish_blocks = _exchange_phases(in_refs[n1:], out_refs[n1:], *refs[2 * n + 7:], same_source=True)

        @pl.when(jnp.logical_and(g == 0, j == 0))
        def _():
            start()
            start_blocks()

        @pl.when(j == 0)
        def _():
            dq_ref[...] = jnp.zeros((L, ATTN_HEADS * LANES), F32)

        dk_s[...] = jnp.zeros((ATTN_HEADS, t, LANES), F32)
        dv_s[...] = jnp.zeros((ATTN_HEADS, t, LANES), F32)
        below = lax.broadcasted_iota(jnp.int32, (t, t), 1) <= lax.broadcasted_iota(jnp.int32, (t, t), 0)

        def block_step(i, on_diagonal):
            rows = pl.ds(pl.multiple_of(i * t, t), t)
            for a in range(ATTN_HEADS):
                lanes = slice(a * LANES, (a + 1) * LANES)
                qi = q_ref[rows, lanes]
                doi = do_ref[rows, lanes]
                kblk = kv_ref[:, 2 * a * LANES:(2 * a + 1) * LANES]
                delta = jnp.sum(doi.astype(F32) * o_ref[rows, lanes].astype(F32), axis=1, keepdims=True)
                s = _dot(qi, kblk, NT)
                if on_diagonal:
                    s = jnp.where(below, s, NEG)
                p = jnp.exp2(s - lse_ref[a, rows, :])
                dv_s[a] += _dot(p.astype(BF16), doi, TN)
                ds = (p * (_dot(doi, kv_ref[:, (2 * a + 1) * LANES:(2 * a + 2) * LANES], NT) - delta)).astype(BF16)
                dk_s[a] += _dot(ds, qi, TN)
                dq_ref[rows, lanes] += _dot(ds, kblk, NN) * ATTN_SCALE

        def step(i, carry):
            block_step(i, False)
            return carry

        block_step(j, True)
        lax.fori_loop(j + 1, nq, step, 0)
        for a in range(ATTN_HEADS):
            dkv_ref[:, 2 * a * LANES:(2 * a + 1) * LANES] = dk_s[a] * (1.0 / LOG2E)
            dkv_ref[:, (2 * a + 1) * LANES:(2 * a + 2) * LANES] = dv_s[a]
        @pl.when(jnp.logical_and(g == ATTN_GROUPS - 1, j == nq - 1))
        def _():
            finish()
            finish_blocks()

    gw = ATTN_HEADS * LANES
    whole = lambda: pl.BlockSpec((L, gw), lambda g, j: (0, g))
    acc = pltpu.VMEM((ATTN_HEADS, t, LANES), F32)
    return pl.pallas_call(
        body, name="attn_bwd", grid=(ATTN_GROUPS, nq),
        in_specs=[whole(), whole(), whole(), pl.BlockSpec((ATTN_HEADS, L, 1), lambda g, j: (g, 0, 0)),
                  pl.BlockSpec((t, 2 * gw), lambda g, j: (j, g))] + [ANY_SPEC] * n,
        out_specs=[whole(), pl.BlockSpec((t, 2 * gw), lambda g, j: (j, g))] + [ANY_SPEC] * n,
        out_shape=[jax.ShapeDtypeStruct((L, HEAD_PAD), F32), jax.ShapeDtypeStruct((L, 2 * HEAD_PAD), F32)]
        + [jax.ShapeDtypeStruct(p.shape, p.dtype) for p in parts]
        + [jax.ShapeDtypeStruct((N_DEV,) + b.shape, b.dtype) for b in blocks],
        scratch_shapes=[acc, acc] + _comm_sems(n1) + _comm_sems(n - n1),
        compiler_params=_params("arbitrary", "arbitrary", vmem=VMEM_BIG))(q, do, o, lse, kv, *parts, *blocks)


def _disc(lr, li, ldt, br, bi):
    dt = jnp.exp(ldt)
    mag = jnp.exp(lr * dt)
    ang = li * dt
    a_re, a_im = mag * jnp.cos(ang), mag * jnp.sin(ang)
    den = lr * lr + li * li
    n_re, n_im = a_re - 1.0, a_im
    z_re = (n_re * lr + n_im * li) / den
    z_im = (n_im * lr - n_re * li) / den
    return a_re, a_im, z_re * br - z_im * bi, z_re * bi + z_im * br


def _disc_call(lr, li, ldt, br, bi):
    def body(lr_ref, li_ref, ldt_ref, br_ref, bi_ref, ar_ref, ai_ref, bbr_ref, bbi_ref):
        ar_ref[...], ai_ref[...], bbr_ref[...], bbi_ref[...] = _disc(
            lr_ref[...], li_ref[...], ldt_ref[...], br_ref[...], bi_ref[...])

    c1 = jax.ShapeDtypeStruct((SSM_NSTATE, 1), F32)
    c16 = jax.ShapeDtypeStruct((SSM_NSTATE, SSM_GROUP), F32)
    return pl.pallas_call(body, name="ssm_disc", out_shape=[c1, c1, c16, c16])(lr, li, ldt, br, bi)


def _disc_bwd_call(lr, li, ldt, br, bi, dar, dai, dbbr, dbbi):
    def body(lr_ref, li_ref, ldt_ref, br_ref, bi_ref, dar_ref, dai_ref, dbbr_ref, dbbi_ref,
             dlr_ref, dli_ref, dldt_ref, dbr_ref, dbi_ref):
        _, vjp = jax.vjp(_disc, lr_ref[...], li_ref[...], ldt_ref[...], br_ref[...], bi_ref[...])
        dlr_ref[...], dli_ref[...], dldt_ref[...], dbr_ref[...], dbi_ref[...] = vjp(
            (dar_ref[...], dai_ref[...], dbbr_ref[...], dbbi_ref[...]))

    c1 = jax.ShapeDtypeStruct((SSM_NSTATE, 1), F32)
    c16 = jax.ShapeDtypeStruct((SSM_NSTATE, SSM_GROUP), F32)
    return pl.pallas_call(body, name="ssm_disc_bwd", out_shape=[c1, c1, c1, c16, c16])(
        lr, li, ldt, br, bi, dar, dai, dbbr, dbbi)


SSM_ROWS = 512
SSM_CW = SSM_NSTATE // SSM_CHUNKS
SSM_CU = SSM_WIDTH // SSM_CHUNKS


def _cmul(ar, ai, br, bi):
    return ar * br - ai * bi, ar * bi + ai * br


def _power(ar1, ai1, n):
    res, base = None, (ar1, ai1)
    while n:
        if n & 1:
            res = base if res is None else _cmul(res[0], res[1], base[0], base[1])
        n >>= 1
        if n:
            base = _cmul(base[0], base[1], base[0], base[1])
    return res


def _tile(k):
    return pl.ds(pl.multiple_of(k * 8, 8), 8)


def _ssm_fwd_call(u, a_re, a_im, bb_re, bb_im, cm_re, cm_im, d_skip):
    L = u.shape[0]
    seg = L // 8
    rb = _fit(L, SSM_ROWS)

    def body(u_ref, ar_ref, ai_ref, bbr_ref, bbi_ref, cmr_ref, cmi_ref, d_ref, y_ref, sre_hbm, sim_hbm,
             s_re, s_im, sems):
        q = pl.program_id(0)

        def bu_step(r, c):
            rows = pl.ds(pl.multiple_of(r * rb, rb), rb)
            ub = u_ref[rows, :].astype(BF16)
            s_re[rows, :] = _dot(ub, bbr_ref[0], NN)
            s_im[rows, :] = _dot(ub, bbi_ref[0], NN)
            return c

        lax.fori_loop(0, L // rb, bu_step, 0)
        ar1, ai1 = ar_ref[...], ai_ref[...]
        ar = jnp.broadcast_to(ar1, (8, SSM_CW))
        ai = jnp.broadcast_to(ai1, (8, SSM_CW))

        def local(k, c):
            nr, ni = _cmul(ar, ai, c[0], c[1])
            nr = nr + s_re[_tile(k), :]
            ni = ni + s_im[_tile(k), :]
            s_re[_tile(k), :] = nr
            s_im[_tile(k), :] = ni
            return nr, ni

        zero8 = jnp.zeros((8, SSM_CW), F32)
        lax.fori_loop(0, seg, local, (zero8, zero8))
        pr, pi = _power(ar1, ai1, seg)
        end_r = s_re[pl.ds((seg - 1) * 8, 8), :]
        end_i = s_im[pl.ds((seg - 1) * 8, 8), :]
        er = jnp.zeros((1, SSM_CW), F32)
        ei = jnp.zeros((1, SSM_CW), F32)
        rows_r, rows_i = [er], [ei]
        for j in range(7):
            tr, ti = _cmul(pr, pi, er, ei)
            er, ei = end_r[j:j + 1] + tr, end_i[j:j + 1] + ti
            rows_r.append(er)
            rows_i.append(ei)
        e_r = jnp.concatenate(rows_r, axis=0)
        e_i = jnp.concatenate(rows_i, axis=0)

        def fix(k, c):
            wr, wi = _cmul(c[0], c[1], ar, ai)
            fr, fi = _cmul(wr, wi, e_r, e_i)
            s_re[_tile(k), :] += fr
            s_im[_tile(k), :] += fi
            return wr, wi

        lax.fori_loop(0, seg, fix, (jnp.ones((8, SSM_CW), F32), zero8))
        out_r = pltpu.make_async_copy(s_re, sre_hbm.at[q], sems.at[0])
        out_i = pltpu.make_async_copy(s_im, sim_hbm.at[q], sems.at[1])
        out_r.start()
        out_i.start()

        def y_step(r, c):
            rows = pl.ds(pl.multiple_of(r * rb, rb), rb)
            y = _dot(s_re[rows, :].astype(BF16), cmr_ref[0], NN) - _dot(s_im[rows, :].astype(BF16), cmi_ref[0], NN)
            y_ref[rows, :] = y + d_ref[...] * u_ref[rows, :]
            return c

        lax.fori_loop(0, L // rb, y_step, 0)
        out_r.wait()
        out_i.wait()

    chunk = lambda rows, cols: pl.BlockSpec((rows, cols), lambda q: (0, q))
    mat = lambda r, c: pl.BlockSpec((1, r, c), lambda q: (q, 0, 0))
    anyspec = pl.BlockSpec(memory_space=pl.ANY)
    states = jax.ShapeDtypeStruct((SSM_CHUNKS, L, SSM_CW), F32)
    return pl.pallas_call(
        body, name="ssm_fwd", grid=(SSM_CHUNKS,),
        in_specs=[chunk(L, SSM_CU), chunk(1, SSM_CW), chunk(1, SSM_CW), mat(SSM_CU, SSM_CW), mat(SSM_CU, SSM_CW),
                  mat(SSM_CW, SSM_CU), mat(SSM_CW, SSM_CU), chunk(1, SSM_CU)],
        out_specs=[chunk(L, SSM_CU), anyspec, anyspec],
        out_shape=[jax.ShapeDtypeStruct((L, SSM_WIDTH), F32), states, states],
        scratch_shapes=[pltpu.VMEM((L, SSM_CW), F32), pltpu.VMEM((L, SSM_CW), F32), pltpu.SemaphoreType.DMA((2,))],
        compiler_params=_params("arbitrary", vmem=VMEM_BIG))(u, a_re, a_im, bb_re, bb_im, cm_re, cm_im, d_skip)


def _ssm_bwd_call(dy, u, s_re_all, s_im_all, a_re, a_im, bb_re, bb_im, cm_re, cm_im, d_skip):
    L = u.shape[0]
    seg = L // 8
    rb = _fit(L, SSM_ROWS)

    def body(dy_ref, u_ref, sre_hbm, sim_hbm, ar_ref, ai_ref, bbr_ref, bbi_ref, cmr_ref, cmi_ref, d_ref,
             du_ref, dbbr_ref, dbbi_ref, dcmr_ref, dcmi_ref, dar_ref, dai_ref, dd_ref,
             g_re, g_im, s_re, s_im, sems):
        q = pl.program_id(0)
        in_r = pltpu.make_async_copy(sre_hbm.at[q], s_re, sems.at[0])
        in_i = pltpu.make_async_copy(sim_hbm.at[q], s_im, sems.at[1])
        in_r.start()
        in_i.start()

        def ds_step(r, c):
            rows = pl.ds(pl.multiple_of(r * rb, rb), rb)
            dyb = dy_ref[rows, :].astype(BF16)
            g_re[rows, :] = _dot(dyb, cmr_ref[0], NT)
            g_im[rows, :] = -_dot(dyb, cmi_ref[0], NT)
            return c

        lax.fori_loop(0, L // rb, ds_step, 0)
        ar1, ai1 = ar_ref[...], ai_ref[...]
        ar = jnp.broadcast_to(ar1, (8, SSM_CW))
        nai = jnp.broadcast_to(-ai1, (8, SSM_CW))

        def local(kk, c):
            k = seg - 1 - kk
            nr, ni = _cmul(ar, nai, c[0], c[1])
            nr = nr + g_re[_tile(k), :]
            ni = ni + g_im[_tile(k), :]
            g_re[_tile(k), :] = nr
            g_im[_tile(k), :] = ni
            return nr, ni

        zero8 = jnp.zeros((8, SSM_CW), F32)
        lax.fori_loop(0, seg, local, (zero8, zero8))
        pr, pi = _power(ar1, -ai1, seg)
        head_r = g_re[pl.ds(0, 8), :]
        head_i = g_im[pl.ds(0, 8), :]
        fr = jnp.zeros((1, SSM_CW), F32)
        fi = jnp.zeros((1, SSM_CW), F32)
        rows_r, rows_i = [fr], [fi]
        for j in range(6, -1, -1):
            tr, ti = _cmul(pr, pi, fr, fi)
            fr, fi = head_r[j + 1:j + 2] + tr, head_i[j + 1:j + 2] + ti
            rows_r.insert(0, fr)
            rows_i.insert(0, fi)
        f_r = jnp.concatenate(rows_r, axis=0)
        f_i = jnp.concatenate(rows_i, axis=0)
        in_r.wait()
        in_i.wait()

        def fixed(k, wr, wi):
            xr, xi = _cmul(wr, wi, f_r, f_i)
            gr = g_re[_tile(k), :] + xr
            gi = g_im[_tile(k), :] + xi
            g_re[_tile(k), :] = gr
            g_im[_tile(k), :] = gi
            return gr, gi

        def fix(kk, c):
            k = seg - 1 - kk
            wr, wi = _cmul(c[0], c[1], ar, nai)
            gr, gi = fixed(k, wr, wi)
            pr_, pi_ = s_re[_tile(k - 1), :], s_im[_tile(k - 1), :]
            return wr, wi, c[2] + gr * pr_ + gi * pi_, c[3] + gi * pr_ - gr * pi_

        wr, wi, acc_r, acc_i = lax.fori_loop(0, seg - 1, fix, (jnp.ones((8, SSM_CW), F32), zero8, zero8, zero8))
        wr, wi = _cmul(wr, wi, ar, nai)
        gr, gi = fixed(0, wr, wi)
        row8 = lax.broadcasted_iota(jnp.int32, (8, SSM_CW), 0)
        pr_ = jnp.where(row8 > 0, pltpu.roll(s_re[pl.ds((seg - 1) * 8, 8), :], 1, 0), 0.0)
        pi_ = jnp.where(row8 > 0, pltpu.roll(s_im[pl.ds((seg - 1) * 8, 8), :], 1, 0), 0.0)
        acc_r = acc_r + gr * pr_ + gi * pi_
        acc_i = acc_i + gi * pr_ - gr * pi_
        dar_ref[...] = jnp.sum(acc_r, axis=0, keepdims=True)
        dai_ref[...] = jnp.sum(acc_i, axis=0, keepdims=True)

        dbbr_ref[...] = jnp.zeros((1, SSM_CU, SSM_CW), F32)
        dbbi_ref[...] = jnp.zeros((1, SSM_CU, SSM_CW), F32)
        dcmr_ref[...] = jnp.zeros((1, SSM_CW, SSM_CU), F32)
        dcmi_ref[...] = jnp.zeros((1, SSM_CW, SSM_CU), F32)
        dd_ref[...] = jnp.zeros((1, SSM_CU), F32)

        def grad_step(r, c):
            rows = pl.ds(pl.multiple_of(r * rb, rb), rb)
            ub, dyv = u_ref[rows, :], dy_ref[rows, :]
            ubb, dyb = ub.astype(BF16), dyv.astype(BF16)
            grb, gib = g_re[rows, :].astype(BF16), g_im[rows, :].astype(BF16)
            dbbr_ref[0] += _dot(ubb, grb, TN)
            dbbi_ref[0] += _dot(ubb, gib, TN)
            dcmr_ref[0] += _dot(s_re[rows, :].astype(BF16), dyb, TN)
            dcmi_ref[0] -= _dot(s_im[rows, :].astype(BF16), dyb, TN)
            du_ref[rows, :] = _dot(grb, bbr_ref[0], NT) + _dot(gib, bbi_ref[0], NT) + d_ref[...] * dyv
            dd_ref[...] += jnp.sum(dyv * ub, axis=0, keepdims=True)
            return c

        lax.fori_loop(0, L // rb, grad_step, 0)

    chunk = lambda rows, cols: pl.BlockSpec((rows, cols), lambda q: (0, q))
    mat = lambda r, c: pl.BlockSpec((1, r, c), lambda q: (q, 0, 0))
    anyspec = pl.BlockSpec(memory_space=pl.ANY)
    big = lambda: pltpu.VMEM((L, SSM_CW), F32)
    return pl.pallas_call(
        body, name="ssm_bwd", grid=(SSM_CHUNKS,),
        in_specs=[chunk(L, SSM_CU), chunk(L, SSM_CU), anyspec, anyspec, chunk(1, SSM_CW), chunk(1, SSM_CW),
                  mat(SSM_CU, SSM_CW), mat(SSM_CU, SSM_CW), mat(SSM_CW, SSM_CU), mat(SSM_CW, SSM_CU), chunk(1, SSM_CU)],
        out_specs=[chunk(L, SSM_CU), mat(SSM_CU, SSM_CW), mat(SSM_CU, SSM_CW), mat(SSM_CW, SSM_CU), mat(SSM_CW, SSM_CU),
                   chunk(1, SSM_CW), chunk(1, SSM_CW), chunk(1, SSM_CU)],
        out_shape=[jax.ShapeDtypeStruct((L, SSM_WIDTH), F32),
                   jax.ShapeDtypeStruct((SSM_CHUNKS, SSM_CU, SSM_CW), F32), jax.ShapeDtypeStruct((SSM_CHUNKS, SSM_CU, SSM_CW), F32),
                   jax.ShapeDtypeStruct((SSM_CHUNKS, SSM_CW, SSM_CU), F32), jax.ShapeDtypeStruct((SSM_CHUNKS, SSM_CW, SSM_CU), F32),
                   jax.ShapeDtypeStruct((1, SSM_NSTATE), F32), jax.ShapeDtypeStruct((1, SSM_NSTATE), F32),
                   jax.ShapeDtypeStruct((1, SSM_WIDTH), F32)],
        scratch_shapes=[big(), big(), big(), big(), pltpu.SemaphoreType.DMA((2,))],
        compiler_params=_params("arbitrary", vmem=VMEM_BIG))(
            dy, u, s_re_all, s_im_all, a_re, a_im, bb_re, bb_im, cm_re, cm_im, d_skip)


def _place():
    return lax.axis_index("x"), lax.axis_index("y"), lax.axis_index("c")


def _small_gather_call(blocks, name):
    n = len(blocks)

    def body(*refs):
        start, finish = _exchange_phases(refs[:n], refs[n:2 * n], *refs[2 * n:], same_source=True)
        start()
        finish()

    return pl.pallas_call(
        body, name=name, in_specs=[ANY_SPEC] * n, out_specs=[ANY_SPEC] * n,
        out_shape=[jax.ShapeDtypeStruct((N_DEV,) + b.shape, b.dtype) for b in blocks],
        scratch_shapes=_comm_sems(n))(*blocks)


def _comm_sems(n):
    return [pltpu.SemaphoreType.DMA((7 * n,)), pltpu.SemaphoreType.DMA((7 * n,)), pltpu.SemaphoreType.DMA((n,))]


def _gather_phases(x_refs, out_refs, send_sems, recv_sems, local_sems):
    x, y, c = _place()
    me, sibling = (x, y, c), (x, y, 1 - c)
    chips = [(1 - x, y), (x, 1 - y), (1 - x, 1 - y)]
    n = len(x_refs)

    def copy(k, a, blk, to, from_input=False):
        slot = out_refs[a].at[4 * blk[0] + 2 * blk[1] + blk[2]]
        return pltpu.make_async_remote_copy(
            src_ref=x_refs[a] if from_input else slot, dst_ref=slot,
            send_sem=send_sems.at[k * n + a], recv_sem=recv_sems.at[k * n + a], device_id=to, device_id_type=MESH_ID)

    mine = [pltpu.make_async_copy(x_refs[a], out_refs[a].at[4 * x + 2 * y + c], local_sems.at[a]) for a in range(n)]
    first, passed = [], []
    for a in range(n):
        first.append(copy(0, a, me, sibling, True))
        first += [copy(1 + j, a, me, (*chip, c), True) for j, chip in enumerate(chips)]
        passed += [copy(4 + j, a, (*chip, c), sibling) for j, chip in enumerate(chips)]

    def start():
        for cp in mine + first:
            cp.start()

    def forward():
        for j, chip in enumerate(chips):
            for a in range(n):
                copy(1 + j, a, (*chip, c), me).wait_recv()
                passed[3 * a + j].start()

    def finish():
        for a in range(n):
            copy(0, a, sibling, me).wait_recv()
            for j, chip in enumerate(chips):
                copy(4 + j, a, (*chip, 1 - c), me).wait_recv()
        for cp in first + passed:
            cp.wait_send()
        for cp in mine:
            cp.wait()

    return start, forward, finish


def _exchange_phases(p_refs, out_refs, send_sems, recv_sems, local_sems, same_source=False):
    x, y, c = _place()
    me = 4 * x + 2 * y + c
    n = len(p_refs)

    def flip(k):
        px = 1 - x if k & 4 else x
        py = 1 - y if k & 2 else y
        pc = 1 - c if k & 1 else c
        return (px, py, pc), 4 * px + 2 * py + pc

    def source(a, slot):
        return p_refs[a] if same_source else p_refs[a].at[slot]

    def copy(k, a, landing):
        peer, peer_slot = flip(k)
        return pltpu.make_async_remote_copy(
            src_ref=source(a, peer_slot), dst_ref=out_refs[a].at[peer_slot if landing else me],
            send_sem=send_sems.at[(k - 1) * n + a], recv_sem=recv_sems.at[(k - 1) * n + a],
            device_id=peer, device_id_type=MESH_ID)

    mine = [pltpu.make_async_copy(source(a, me), out_refs[a].at[me], local_sems.at[a]) for a in range(n)]
    sends = [copy(k, a, False) for k in range(1, N_DEV) for a in range(n)]

    def start():
        for cp in mine + sends:
            cp.start()

    def finish():
        for k in range(1, N_DEV):
            for a in range(n):
                copy(k, a, True).wait_recv()
        for cp in sends:
            cp.wait_send()
        for cp in mine:
            cp.wait()

    return start, finish


def _adam_math(g, w, m, v):
    c1 = 1.0 / (1.0 - ADAM_B1 ** ADAM_STEP)
    c2 = 1.0 / (1.0 - ADAM_B2 ** ADAM_STEP)
    m_new = ADAM_B1 * m + (1.0 - ADAM_B1) * g
    v_new = ADAM_B2 * v + (1.0 - ADAM_B2) * (g * g)
    delta = -ADAM_LR * ((m_new * c1) / (jnp.sqrt(v_new * c2) + ADAM_EPS) + ADAM_WD * w)
    return g, delta, m_new, v_new


def _sum_slices(s_ref):
    g = s_ref[0].astype(F32)
    for k in range(1, N_DEV):
        g = g + s_ref[k].astype(F32)
    return g


def _adam_call(slices, w, m, v, name):
    d1, rest = w.shape[1], w.shape[2:]
    zeros = (0,) * len(rest)
    by_lanes = len(rest) == 1 and d1 > 256 and d1 % 16 != 0
    if by_lanes:
        tile = _fit(rest[0], 256)
        steps = rest[0] // tile
        own = pl.BlockSpec((1, d1, tile), lambda i: (0, 0, i))
        sl = pl.BlockSpec((N_DEV, 1, d1, tile), lambda i: (0, 0, 0, i))
    else:
        tile = _fit(d1, 256, 16) if len(rest) == 1 else _fit(d1, 8, 8)
        steps = d1 // tile
        own = pl.BlockSpec((1, tile) + rest, lambda i: (0, i) + zeros)
        sl = pl.BlockSpec((N_DEV, 1, tile) + rest, lambda i: (0, 0, i) + zeros)

    def body(s_ref, w_ref, m_ref, v_ref, g_ref, d_ref, mo_ref, vo_ref):
        g_ref[...], d_ref[...], mo_ref[...], vo_ref[...] = _adam_math(_sum_slices(s_ref), w_ref[...], m_ref[...], v_ref[...])

    out = jax.ShapeDtypeStruct(w.shape, F32)
    return pl.pallas_call(
        body, name=name, grid=(steps,), in_specs=[sl, own, own, own],
        out_specs=[own, own, own, own], out_shape=[out, out, out, out],
        compiler_params=_params("parallel"))(slices, w, m, v)


def _adam_small_call(rows_all, row_params, slices, params):
    nr, n = len(row_params), len(row_params) + len(params)

    def row_sum(rows_ref, a, width):
        g = rows_ref[0, pl.ds(a, 1), pl.ds(0, width)]
        for k in range(1, N_DEV):
            g = g + rows_ref[k, pl.ds(a, 1), pl.ds(0, width)]
        return g

    def body(rows_ref, *refs):
        slice_refs, wmv, outs = refs[:n - nr], refs[n - nr:n - nr + 3 * n], refs[n - nr + 3 * n:]
        outs[4 * n][...] = row_sum(rows_ref, nr, LANES)
        for a in range(n):
            w_ref, m_ref, v_ref = wmv[3 * a:3 * a + 3]
            if a < nr:
                g = row_sum(rows_ref, a, w_ref.shape[1])
            else:
                g = _sum_slices(slice_refs[a - nr])
            res = _adam_math(g, w_ref[...], m_ref[...], v_ref[...])
            for r in range(4):
                outs[4 * a + r][...] = res[r]

    every = list(row_params) + list(params)
    flat = pl.pallas_call(
        body, name="adam_small",
        out_shape=[jax.ShapeDtypeStruct(w.shape, F32) for w, _, _ in every for _ in range(4)]
        + [jax.ShapeDtypeStruct((1, LANES), F32)],
        compiler_params=pltpu.CompilerParams(vmem_limit_bytes=VMEM_BIG),
    )(rows_all, *slices, *[t for wmv in every for t in wmv])
    return [flat[4 * a:4 * a + 4] for a in range(n)], flat[4 * n][0, 0]


BIG = (("w_in", 1024, 404, 1), ("w_uq", 384, 96, 1), ("w_uk", 256, 64, 1), ("w_uv", 256, 64, 1),
       ("w_glu", 64, 512, 0), ("w_branch_attn", 512, 128, 1), ("w_branch_ssm", 512, 128, 1),
       ("w_out", 128, 1024, 0), ("w_up", 1024, 704, 1), ("w_down", 352, 1024, 0), ("conv_w", 3, 704, 1))
BIG_MIX, BIG_FFN = BIG[:8], BIG[8:]
GRADS_EARLY, GRADS_LATE = BIG[8:] + BIG[4:8], BIG[:4]
SMALL = (("mix_norm_pre", (1024,)), ("q_norm", (384,)), ("kv_norm", (256,)), ("ssm_lambda_re", (32, 64)),
         ("ssm_lambda_im", (32, 64)), ("ssm_log_dt", (32,)), ("ssm_b_re", (32, 64, 16)), ("ssm_b_im", (32, 64, 16)),
         ("ssm_c_re", (32, 16, 64)), ("ssm_c_im", (32, 16, 64)), ("ssm_d", (32, 16)), ("b_glu", (512,)),
         ("b_gate", (2048,)), ("mix_norm_post", (1024,)), ("ffn_norm_pre", (1024,)), ("conv_b", (5632,)),
         ("ffn_norm_post", (1024,)))


TRANSPOSED = ("w_in", "w_uq", "w_uk", "w_uv", "w_up")


STORED_SWAP = {**{name: (1, 2) for name in TRANSPOSED}, "ssm_b_re": (2, 3), "ssm_b_im": (2, 3), "ssm_d": (1, 2)}


def _stored(name, arr):
    return jnp.swapaxes(arr, *STORED_SWAP[name]) if name in STORED_SWAP else arr


def _to_slices(name, full, rows, cols, axis):
    if name in TRANSPOSED:
        return full.reshape(N_DEV, cols, rows)
    if axis == 1:
        return full.reshape(rows, N_DEV, cols).transpose(1, 0, 2)
    return full.reshape(N_DEV, rows, cols)


def _from_slices(name, parts, rows, cols, axis):
    if name in TRANSPOSED:
        return parts.reshape(N_DEV * cols, rows)
    if axis == 1:
        return parts.transpose(1, 0, 2).reshape(rows, N_DEV * cols)
    return parts.reshape(N_DEV * rows, cols)


def _time_perm(a, L):
    return a.reshape(8, L // 8, a.shape[-1]).transpose(1, 0, 2).reshape(L, a.shape[-1])


def _time_unperm(a, L):
    return a.reshape(L // 8, 8, a.shape[-1]).transpose(1, 0, 2).reshape(L, a.shape[-1])


def _block_diag(w, rows_first):
    eye = jnp.eye(8, dtype=w.dtype)
    g = w.reshape(SSM_CHUNKS, 8, w.shape[1], w.shape[2])
    return jnp.einsum("qgrc,gk->qgrkc", g, eye).reshape(SSM_CHUNKS, 8 * w.shape[1], 8 * w.shape[2])


def _block_diag_t(m, r, c):
    eye = jnp.eye(8, dtype=m.dtype)
    return jnp.einsum("qgrkc,gk->qgrc", m.reshape(SSM_CHUNKS, 8, r, 8, c), eye).reshape(SSM_GROUPS, r, c)


def kernel(x, positions, mix_norm_pre, w_in, q_norm, w_uq, kv_norm, w_uk, w_uv, ssm_lambda_re, ssm_lambda_im, ssm_log_dt, ssm_b_re, ssm_b_im, ssm_c_re, ssm_c_im, ssm_d, w_glu, b_glu, w_branch_attn, w_branch_ssm, b_gate, w_out, mix_norm_post, ffn_norm_pre, w_up, conv_w, conv_b, w_down, ffn_norm_post, loss_target, m_mix_norm_pre, m_w_in, m_q_norm, m_w_uq, m_kv_norm, m_w_uk, m_w_uv, m_ssm_lambda_re, m_ssm_lambda_im, m_ssm_log_dt, m_ssm_b_re, m_ssm_b_im, m_ssm_c_re, m_ssm_c_im, m_ssm_d, m_w_glu, m_b_glu, m_w_branch_attn, m_w_branch_ssm, m_b_gate, m_w_out, m_mix_norm_post, m_ffn_norm_pre, m_w_up, m_conv_w, m_conv_b, m_w_down, m_ffn_norm_post, v_mix_norm_pre, v_w_in, v_q_norm, v_w_uq, v_kv_norm, v_w_uk, v_w_uv, v_ssm_lambda_re, v_ssm_lambda_im, v_ssm_log_dt, v_ssm_b_re, v_ssm_b_im, v_ssm_c_re, v_ssm_c_im, v_ssm_d, v_w_glu, v_b_glu, v_w_branch_attn, v_w_branch_ssm, v_b_gate, v_w_out, v_mix_norm_post, v_ffn_norm_pre, v_w_up, v_conv_w, v_conv_b, v_w_down, v_ffn_norm_post):
    given = dict(locals())
    L = x.shape[1]
    xs = x[0]
    target = loss_target[0]

    def shard_bits(group):
        return [given[name][0] if name == "conv_w" else _stored(name, given[name])[0].astype(BF16) for name, _, _, _ in group]

    W = {}

    def unpack_weights(gathered, group):
        for (name, rows, cols, axis), parts in zip(group, gathered):
            W[name] = _from_slices(name, parts, rows, cols, axis)

    hn1, *gathered_w_in = _rms_fwd_call(xs, mix_norm_pre, "rms_pre", shard_bits(BIG_MIX[:1]))
    unpack_weights(gathered_w_in, BIG_MIX[:1])

    wit = W["w_in"]
    zero_rows = lambda r: jnp.zeros((r, D_MODEL), BF16)
    kr_end = P_KR + QK_ROPE
    w_in_pt = jnp.concatenate(
        [wit[:P_KR], zero_rows(QK_NOPE), wit[P_KR:kr_end], zero_rows(LANES - QK_HEAD), wit[kr_end:]], axis=0)

    proj, *gathered_mix = _mm(hn1, w_in_pt, "mm_in", tb=True, tn=1664, gather=shard_bits(BIG_MIX[1:]))
    unpack_weights(gathered_mix, BIG_MIX[1:])
    head_rows = lambda wt, width: jnp.pad(wt.reshape(N_HEADS, width, wt.shape[1]), ((0, 0), (0, LANES - width), (0, 0)))
    w_uq_pt = head_rows(W["w_uq"], QK_HEAD).reshape(HEAD_PAD, Q_RANK)
    w_kv_pt = jnp.stack([head_rows(W["w_uk"], QK_NOPE), head_rows(W["w_uv"], V_HEAD)], axis=1
                        ).reshape(2 * HEAD_PAD, KV_RANK)
    w_ba_p = jnp.pad(W["w_branch_attn"].reshape(N_HEADS, V_HEAD, D_MODEL), ((0, 0), (0, LANES - V_HEAD), (0, 0))
                     ).reshape(HEAD_PAD, D_MODEL)
    half = jnp.arange(QK_ROPE // 2, dtype=F32)
    inv_freq = ROPE_THETA ** (-2.0 * half / QK_ROPE)
    inv_freq = jnp.pad(jnp.concatenate([inv_freq, inv_freq]), (QK_NOPE, LANES - QK_HEAD)).reshape(1, LANES)
    pos_col = positions.astype(F32).reshape(L, 1)
    qn, ckvn, q_r, kv_r, cosf, sinf = _mla_proj_call(proj, q_norm, kv_norm, w_uq_pt, w_kv_pt, pos_col, inv_freq)
    attn, lse, *gathered_ffn = _attn_fwd_call(q_r, kv_r, shard_bits(BIG_FFN))
    unpack_weights(gathered_ffn, BIG_FFN)

    col = lambda a: a.reshape(SSM_NSTATE, -1)
    lr_c, li_c = col(ssm_lambda_re[0]), col(ssm_lambda_im[0])
    ldt_c = col(jnp.broadcast_to(ssm_log_dt[0][:, None], (SSM_GROUPS, SSM_STATE)))
    br_c, bi_c = col(ssm_b_re[0]), col(ssm_b_im[0])
    a_re_c, a_im_c, bb_re_c, bb_im_c = _disc_call(lr_c, li_c, ldt_c, br_c, bi_c)
    a_re, a_im = a_re_c.reshape(1, SSM_NSTATE), a_im_c.reshape(1, SSM_NSTATE)
    to_bb = lambda b: _block_diag(b.reshape(SSM_GROUPS, SSM_STATE, SSM_GROUP).transpose(0, 2, 1), True).astype(BF16)
    bb_re, bb_im = to_bb(bb_re_c), to_bb(bb_im_c)
    to_cm = lambda c_: _block_diag(c_[0].transpose(0, 2, 1), True).astype(BF16)
    cm_re, cm_im = to_cm(ssm_c_re), to_cm(ssm_c_im)
    d_skip = ssm_d.reshape(1, SSM_WIDTH)
    u_p = _time_perm(proj[:, P_U:P_GATE], L)
    y1, s_re, s_im = _ssm_fwd_call(u_p, a_re, a_im, bb_re, bb_im, cm_re, cm_im, d_skip)
    w_glu_b = W["w_glu"]
    ssm_p = _glu_call(y1, w_glu_b, b_glu)
    ssm = _time_unperm(ssm_p, L)

    pa = _mm(attn, w_ba_p, "mm_ba")
    ps = _mm(ssm, W["w_branch_ssm"], "mm_bs")
    merged = _merge_call(proj, b_gate, pa, ps)
    wide = lambda dt: (D_MODEL, dt)
    o, x2, hn2 = _mm_rows(merged, W["w_out"], "mm_out", _post_mix_rows, [xs], [mix_norm_post, ffn_norm_pre],
                          [wide(F32), wide(F32), wide(BF16)], [])
    h = _mm(hn2, W["w_up"], "mm_up", tb=True, tn=1408)
    cw = W["conv_w"]
    act = _conv_act_call(h, cw, conv_b)
    dy, dff, loss_row, g_ffn_norm_post = _mm_rows(
        act, W["w_down"], "mm_down", _ffn_out_rows, [x2, target], [ffn_norm_post], [wide(F32), wide(BF16)],
        [LANES, D_MODEL], tk=1408)

    da = _mm(dff, W["w_down"], "mm_down_dx", tb=True, tn=1408)
    g_w_down = _mm_tn(act, dff, "mm_down_dw", tm=1408)
    dgate, dval, dcw_g, dcw_v, dcb_g, dcb_v = _conv_act_bwd_call(da, h, cw, conv_b)
    g_conv_w = jnp.concatenate([dcw_g, dcw_v], axis=1)
    g_conv_b = jnp.concatenate([dcb_g, dcb_v], axis=1)
    dh = _conv_t_call(dgate, dval, cw)
    dx2, do, g_ffn_norm_pre, g_mix_norm_post = _mm_rows(
        dh, W["w_up"], "mm_up_dx", _post_bwd_rows, [x2, dy, o], [ffn_norm_pre, mix_norm_post], [wide(F32), wide(BF16)],
        [D_MODEL, D_MODEL], tk=1408)
    g_w_up = _mm_tn(dh, hn2, "mm_up_dw", tm=1408)
    dmerged = _mm(do, W["w_out"], "mm_out_dx", tb=True)
    g_w_out = _mm_tn(merged, do, "mm_out_dw")
    dpa, dps, dl0, dl1, db0, db1 = _merge_bwd_call(dmerged, proj, b_gate, pa, ps)
    g_b_gate = jnp.concatenate([db0, db1], axis=1)
    dattn = _mm(dpa, w_ba_p, "mm_ba_dx", tb=True, out_dtype=BF16)
    g_w_ba = _mm_tn(attn, dpa, "mm_ba_dw").reshape(N_HEADS, LANES, D_MODEL)[:, :V_HEAD].reshape(N_HEADS * V_HEAD, D_MODEL)
    dssm = _mm(dps, W["w_branch_ssm"], "mm_bs_dx", tb=True)
    g_w_bs = _mm_tn(ssm, dps, "mm_bs_dw")

    dy1, g_w_glu, g_b_glu = _glu_bwd_call(_time_perm(dssm, L), y1, w_glu_b, b_glu)
    du_p, dbb_re, dbb_im, dcm_re, dcm_im, da_re, da_im, g_ssm_d = _ssm_bwd_call(
        dy1, u_p, s_re, s_im, a_re, a_im, bb_re, bb_im, cm_re, cm_im, d_skip)
    du = _time_unperm(du_p, L)
    from_bb = lambda m: col(_block_diag_t(m, SSM_GROUP, SSM_STATE).transpose(0, 2, 1))
    dlr, dli, dldt, dbr, dbi = _disc_bwd_call(
        lr_c, li_c, ldt_c, br_c, bi_c, da_re.reshape(SSM_NSTATE, 1), da_im.reshape(SSM_NSTATE, 1), from_bb(dbb_re), from_bb(dbb_im))
    g_c_re = _block_diag_t(dcm_re, SSM_STATE, SSM_GROUP).transpose(0, 2, 1)
    g_c_im = _block_diag_t(dcm_im, SSM_STATE, SSM_GROUP).transpose(0, 2, 1)

    def grad_slices(group, grads):
        return [_to_slices(name, grads[name], rows, cols, axis) for name, rows, cols, axis in group]

    early_grads = {"w_up": g_w_up, "w_down": g_w_down, "conv_w": g_conv_w, "w_glu": g_w_glu.astype(BF16),
                   "w_branch_attn": g_w_ba, "w_branch_ssm": g_w_bs, "w_out": g_w_out}
    b_stored = lambda d: d.reshape(SSM_GROUPS, SSM_STATE, SSM_GROUP).transpose(0, 2, 1)
    per_state = lambda d: d.reshape(SSM_GROUPS, SSM_STATE)
    ssm_partials = {"ssm_lambda_re": per_state(dlr), "ssm_lambda_im": per_state(dli),
                    "ssm_b_re": b_stored(dbr), "ssm_b_im": b_stored(dbi),
                    "ssm_c_re": g_c_re, "ssm_c_im": g_c_im, "ssm_d": g_ssm_d.reshape(SSM_GROUPS, SSM_GROUP).T}
    ssm_shapes = [(name, ssm_partials[name].shape) for name, _ in SMALL if name in ssm_partials]
    dq, dkv, *landed = _attn_bwd_call(
        q_r, kv_r, attn, dattn, lse, grad_slices(GRADS_EARLY, early_grads),
        [ssm_partials[name].reshape(-1, LANES) if len(shp) == 3 else ssm_partials[name].reshape((1,) + shp)
         for name, shp in ssm_shapes])
    received_early = landed[:len(GRADS_EARLY)]
    ssm_all = {name: got.reshape((N_DEV, 1) + shp) for (name, shp), got in zip(ssm_shapes, landed[len(GRADS_EARLY):])}
    dq_p, dkv_p, dlatent, g_q_norm, g_kv_norm = _mla_proj_bwd_call(
        dq, dkv, cosf, sinf, proj, q_norm, kv_norm, w_uq_pt, w_kv_pt)
    g_w_uq = _mm_tn(dq_p, qn, "mm_uq_dw").reshape(N_HEADS, LANES, Q_RANK)[:, :QK_HEAD].reshape(N_HEADS * QK_HEAD, Q_RANK)
    g_w_kv = _mm_tn(ckvn, dkv_p, "mm_ukv_dw").T.reshape(N_HEADS, 2, LANES, KV_RANK)
    g_w_uk = g_w_kv[:, 0, :QK_NOPE].reshape(N_HEADS * QK_NOPE, KV_RANK)
    g_w_uv = g_w_kv[:, 1, :V_HEAD].reshape(N_HEADS * V_HEAD, KV_RANK)
    dproj = jnp.concatenate([dlatent, du.astype(BF16), dl0, dl1], axis=1)
    g_w_in_pt = _mm_tn(dproj, hn1, "mm_in_dw", tm=1664)
    g_w_in = jnp.concatenate([g_w_in_pt[:P_KR], g_w_in_pt[P_KR + QK_NOPE:P_KR + QK_HEAD], g_w_in_pt[P_U:]], axis=0)
    late_grads = {"w_in": g_w_in, "w_uq": g_w_uq, "w_uk": g_w_uk, "w_uv": g_w_uv}
    grad_x, g_mix_norm_pre, *received_late = _mm_in_dx_call(
        dproj, w_in_pt, xs, dx2, mix_norm_pre, grad_slices(GRADS_LATE, late_grads))

    results = {}
    wmv = lambda name: tuple(_stored(name, given[prefix + name]) for prefix in ("", "m_", "v_"))
    unstored = lambda name, res: [_stored(name, r) for r in res]
    whole = ("w_uq", "w_uk", "w_uv", "w_glu", "w_branch_attn", "w_branch_ssm", "conv_w")
    landed_small = dict(ssm_all)
    for group, received in ((GRADS_EARLY, received_early), (GRADS_LATE, received_late)):
        for (name, _, _, _), rec in zip(group, received):
            if name in whole:
                landed_small[name] = rec[:, None]
            else:
                results[name] = unstored(name, _adam_call(rec[:, None], *wmv(name), "adam_" + name))

    vec_grads = {"mix_norm_pre": g_mix_norm_pre, "q_norm": g_q_norm, "kv_norm": g_kv_norm,
                 "ssm_log_dt": jnp.sum(dldt.reshape(SSM_GROUPS, SSM_STATE), axis=1),
                 "b_glu": g_b_glu, "b_gate": g_b_gate, "mix_norm_post": g_mix_norm_post,
                 "ffn_norm_pre": g_ffn_norm_pre, "ffn_norm_post": g_ffn_norm_post}
    vec_names = [name for name, _ in SMALL if name in vec_grads]
    width = max(shp[0] for name, shp in SMALL if name in vec_grads)
    rows = [jnp.pad(vec_grads[name].reshape(1, -1), ((0, 0), (0, width - vec_grads[name].size))) for name in vec_names]
    rows.append(jnp.pad(loss_row, ((0, 0), (0, width - LANES))))
    rows.append(jnp.zeros((-len(rows) % 8, width), F32))
    rows_all, landed_small["conv_b"] = _small_gather_call([jnp.concatenate(rows, axis=0), g_conv_b], "gather_small_grads")
    others = ["conv_b"] + [name for name, _ in ssm_shapes] + list(whole)
    small_results, loss = _adam_small_call(
        rows_all, [wmv(n) for n in vec_names], [landed_small[n] for n in others], [wmv(n) for n in others])
    for name, res in zip(vec_names + others, small_results):
        results[name] = unstored(name, res)

    order = ["mix_norm_pre", "w_in", "q_norm", "w_uq", "kv_norm", "w_uk", "w_uv", "ssm_lambda_re", "ssm_lambda_im",
             "ssm_log_dt", "ssm_b_re", "ssm_b_im", "ssm_c_re", "ssm_c_im", "ssm_d", "w_glu", "b_glu", "w_branch_attn",
             "w_branch_ssm", "b_gate", "w_out", "mix_norm_post", "ffn_norm_pre", "w_up", "conv_w", "conv_b", "w_down",
             "ffn_norm_post"]
    outs = [loss, grad_x[None]]
    for kind in range(4):
        outs += [results[name][kind] for name in order]
    return tuple(outs)
```

```python
import math

import jax
import jax.numpy as jnp
from jax import lax
from jax.experimental import pallas as pl
from jax.experimental.pallas import tpu as pltpu

F32 = jnp.float32
BF16 = jnp.bfloat16
MESH_ID = pl.DeviceIdType.MESH

N_DEV = 8
LANES = 128
D_MODEL = 1024
N_HEADS = 8
QK_NOPE = 64
QK_ROPE = 32
QK_HEAD = QK_NOPE + QK_ROPE
V_HEAD = 64
Q_RANK = 384
KV_RANK = 256
ROPE_THETA = 10000.0
SSM_WIDTH = 512
SSM_GROUP = 16
SSM_GROUPS = 32
SSM_STATE = 64
SSM_NSTATE = SSM_GROUPS * SSM_STATE
SSM_CHUNKS = 4
D_FF = 2816
EPS = 1e-6
ADAM_LR, ADAM_B1, ADAM_B2, ADAM_EPS, ADAM_WD, ADAM_STEP = 0.001, 0.9, 0.999, 1e-08, 0.01, 10

P_CQ, P_CKV, P_KR, P_U, P_GATE = 0, 384, 640, 768, 1280
HEAD_PAD = N_HEADS * LANES

VMEM_BIG = 52 * 1024 * 1024

_GELU_C0 = math.sqrt(2.0 / math.pi)
_GELU_C1 = 0.044715
NEG = -1e30


def _fit(n, pref, mult=LANES):
    if n <= pref:
        return n
    t = (pref // mult) * mult
    while t > 0 and n % t:
        t -= mult
    assert t > 0, (n, pref, mult)
    return t


def _gelu(x):
    return x * (0.5 * (1.0 + jnp.tanh(_GELU_C0 * x * (1.0 + _GELU_C1 * (x * x)))))


def _gelu_and_grad(x):
    x2 = x * x
    t = jnp.tanh(_GELU_C0 * x * (1.0 + _GELU_C1 * x2))
    half = 0.5 * (1.0 + t)
    return x * half, half + 0.5 * x * (1.0 - t * t) * _GELU_C0 * (1.0 + 3.0 * _GELU_C1 * x2)


def _sigmoid(x):
    return 1.0 / (1.0 + jnp.exp(-x))


def _dot(a, b, dims):
    return lax.dot_general(a, b, (dims, ((), ())), preferred_element_type=F32)


NN = ((1,), (0,))
NT = ((1,), (1,))
TN = ((0,), (0,))


def _params(*sem, vmem=None):
    return pltpu.CompilerParams(dimension_semantics=tuple(sem), vmem_limit_bytes=vmem)


def _mm(a, b, name, tb=False, out_dtype=F32, tm=1024, tn=1024, tk=1024, gather=()):
    M, K = a.shape
    if tb:
        N, K2 = b.shape
    else:
        K2, N = b.shape
    assert K == K2, (a.shape, b.shape, tb)
    tm, tn, tk = _fit(M, tm), _fit(N, tn), _fit(K, tk)
    nk = K // tk
    grid = (M // tm, N // tn, nk)
    steps = grid[0] * grid[1] * grid[2]
    dims = NT if tb else NN
    n = len(gather)

    def body(a_ref, b_ref, *refs):
        o_ref, scratch = refs[n], refs[2 * n + 1:]
        step = (pl.program_id(0) * grid[1] + pl.program_id(1)) * grid[2] + pl.program_id(2)
        if n:
            start, forward, finish = _gather_phases(refs[:n], refs[n + 1:2 * n + 1], *scratch[-3:])
            pl.when(step == 0)(start)
            pl.when(step == steps // 2)(forward)
        part = _dot(a_ref[...].astype(BF16), b_ref[...].astype(BF16), dims)
        if nk == 1:
            o_ref[...] = part.astype(out_dtype)
        else:
            acc_ref = scratch[0]
            k = pl.program_id(2)

            @pl.when(k == 0)
            def _():
                acc_ref[...] = part

            @pl.when(k > 0)
            def _():
                acc_ref[...] += part

            @pl.when(k == nk - 1)
            def _():
                o_ref[...] = acc_ref[...].astype(out_dtype)
        if n:
            pl.when(step == steps - 1)(finish)

    a_spec = pl.BlockSpec((tm, tk), lambda i, j, k: (i, k))
    b_spec = pl.BlockSpec((tn, tk), lambda i, j, k: (j, k)) if tb else pl.BlockSpec((tk, tn), lambda i, j, k: (k, j))
    landed = [jax.ShapeDtypeStruct((N_DEV,) + p.shape, p.dtype) for p in gather]
    out = pl.pallas_call(
        body, name=name, grid=grid,
        in_specs=[a_spec, b_spec] + [ANY_SPEC] * n,
        out_specs=[pl.BlockSpec((tm, tn), lambda i, j, k: (i, j))] + [ANY_SPEC] * n,
        out_shape=[jax.ShapeDtypeStruct((M, N), out_dtype)] + landed,
        scratch_shapes=([] if nk == 1 else [pltpu.VMEM((tm, tn), F32)]) + (_comm_sems(n) if n else []),
        compiler_params=_params(*(("arbitrary",) * 3 if n else ("parallel", "parallel", "arbitrary")), vmem=VMEM_BIG),
    )(a, b, *gather)
    return out if n else out[0]


def _mm_rows(a, b, name, epilogue, rows_in, vecs_in, rows_out, vecs_out, tb=False, tk=1024):
    M, K = a.shape
    N = b.shape[0] if tb else b.shape[1]
    tm, tk = _fit(M, 512), _fit(K, tk)
    nk = K // tk
    nr, nv, nro = len(rows_in), len(vecs_in), len(rows_out)

    def body(a_ref, b_ref, *refs):
        ins, outs, acc_ref = refs[:nr + nv], refs[nr + nv:nr + nv + nro + len(vecs_out)], refs[-1]
        i, k = pl.program_id(0), pl.program_id(1)
        part = _dot(a_ref[...], b_ref[...], NT if tb else NN)

        def finish(product):
            res = epilogue(product, *[r[...] for r in ins])
            for ref, val in zip(outs[:nro], res[:nro]):
                ref[...] = val.astype(ref.dtype)
            for ref, val in zip(outs[nro:], res[nro:]):
                _acc(ref, i == 0, val)

        if nk == 1:
            finish(part)
        else:
            @pl.when(k == 0)
            def _():
                acc_ref[...] = part

            @pl.when(jnp.logical_and(k > 0, k < nk - 1))
            def _():
                acc_ref[...] += part

            @pl.when(k == nk - 1)
            def _():
                finish(acc_ref[...] + part)

    row = lambda w: pl.BlockSpec((tm, w), lambda i, k: (i, 0))
    vec = lambda w: pl.BlockSpec((1, w), lambda i, k: (0, 0))
    b_spec = pl.BlockSpec((N, tk), lambda i, k: (0, k)) if tb else pl.BlockSpec((tk, N), lambda i, k: (k, 0))
    return pl.pallas_call(
        body, name=name, grid=(M // tm, nk),
        in_specs=[pl.BlockSpec((tm, tk), lambda i, k: (i, k)), b_spec] + [row(r.shape[1]) for r in rows_in]
        + [vec(v.shape[1]) for v in vecs_in],
        out_specs=[row(w) for w, _ in rows_out] + [vec(w) for w in vecs_out],
        out_shape=[jax.ShapeDtypeStruct((M, w), dt) for w, dt in rows_out] + [jax.ShapeDtypeStruct((1, w), F32) for w in vecs_out],
        scratch_shapes=[pltpu.VMEM((tm, N), F32)],
        compiler_params=_params("arbitrary", "arbitrary", vmem=VMEM_BIG))(a, b, *rows_in, *vecs_in)


def _mm_in_dx_call(dproj, w_in_pt, x, dx2, g_pre, exchange):
    L, K = dproj.shape
    N = w_in_pt.shape[1]
    tm, tk = _fit(L, 512), _fit(K, 1664)
    nm, nk = L // tm, K // tk
    n = len(exchange)

    def body(a_ref, b_ref, x_ref, dx2_ref, g_ref, *refs):
        parts, (gx_ref, dg_ref), got = refs[:n], refs[n:n + 2], refs[n + 2:2 * n + 2]
        acc_ref = refs[2 * n + 2]
        i, k = pl.program_id(0), pl.program_id(1)
        start, finish = _exchange_phases(parts, got, *refs[2 * n + 3:])
        pl.when(jnp.logical_and(i == 0, k == 0))(start)
        part = _dot(a_ref[...], b_ref[...], NN)

        @pl.when(k == 0)
        def _():
            acc_ref[...] = part

        @pl.when(jnp.logical_and(k > 0, k < nk - 1))
        def _():
            acc_ref[...] += part

        @pl.when(k == nk - 1)
        def _():
            d1, dg = _rms_bwd(x_ref[...], g_ref[...], acc_ref[...] + part)
            gx_ref[...] = dx2_ref[...] + d1
            _acc(dg_ref, i == 0, dg)

        pl.when(jnp.logical_and(i == nm - 1, k == nk - 1))(finish)

    assert nk >= 2
    rows = lambda: pl.BlockSpec((tm, N), lambda i, k: (i, 0))
    return pl.pallas_call(
        body, name="mm_in_dx", grid=(nm, nk),
        in_specs=[pl.BlockSpec((tm, tk), lambda i, k: (i, k)), pl.BlockSpec((tk, N), lambda i, k: (k, 0)),
                  rows(), rows(), pl.BlockSpec((1, N), lambda i, k: (0, 0))] + [ANY_SPEC] * n,
        out_specs=[rows(), pl.BlockSpec((1, N), lambda i, k: (0, 0))] + [ANY_SPEC] * n,
        out_shape=[jax.ShapeDtypeStruct((L, N), F32), jax.ShapeDtypeStruct((1, N), F32)]
        + [jax.ShapeDtypeStruct(p.shape, p.dtype) for p in exchange],
        scratch_shapes=[pltpu.VMEM((tm, N), F32)] + _comm_sems(n),
        compiler_params=_params("arbitrary", "arbitrary", vmem=VMEM_BIG))(dproj, w_in_pt, x, dx2, g_pre, *exchange)


TN_CHUNK = 512


def _mm_tn(a, b, name, tm=512, tk=512, exchange=()):
    K, M = a.shape
    K2, N = b.shape
    assert K == K2, (a.shape, b.shape)
    tm, tk, cn = _fit(M, tm), _fit(K, tk), _fit(N, TN_CHUNK)
    nm, nk = M // tm, K // tk
    n = len(exchange)

    def body(a_ref, b_ref, *refs):
        o_ref, acc_ref = refs[n], refs[2 * n + 1]
        i, k = pl.program_id(0), pl.program_id(1)
        if n:
            start, finish = _exchange_phases(refs[:n], refs[n + 1:2 * n + 1], *refs[2 * n + 2:])
            pl.when(jnp.logical_and(i == 0, k == 0))(start)

        @pl.when(k == 0)
        def _():
            acc_ref[...] = jnp.zeros((tm, N), F32)

        at = a_ref[...].astype(BF16).T
        for c in range(N // cn):
            cols = slice(c * cn, (c + 1) * cn)
            acc_ref[:, cols] += _dot(at, b_ref[:, cols].astype(BF16), NN)

        @pl.when(k == nk - 1)
        def _():
            o_ref[...] = acc_ref[...].astype(BF16)

        if n:
            pl.when(jnp.logical_and(i == nm - 1, k == nk - 1))(finish)

    out = pl.pallas_call(
        body, name=name, grid=(nm, nk),
        in_specs=[pl.BlockSpec((tk, tm), lambda i, k: (k, i)), pl.BlockSpec((tk, N), lambda i, k: (k, 0))] + [ANY_SPEC] * n,
        out_specs=[pl.BlockSpec((tm, N), lambda i, k: (i, 0))] + [ANY_SPEC] * n,
        out_shape=[jax.ShapeDtypeStruct((M, N), BF16)] + [jax.ShapeDtypeStruct(p.shape, p.dtype) for p in exchange],
        scratch_shapes=[pltpu.VMEM((tm, N), F32)] + (_comm_sems(n) if n else []),
        compiler_params=_params("arbitrary" if n else "parallel", "arbitrary", vmem=VMEM_BIG))(a, b, *exchange)
    return out if n else out[0]


def _row(tl, n, col=0):
    return pl.BlockSpec((tl, n), lambda i: (i, col))


def _full(shape):
    return pl.BlockSpec(shape, lambda i: (0,) * len(shape))


def _rms(x, g):
    r = lax.rsqrt(jnp.mean(x * x, axis=-1, keepdims=True) + EPS)
    return x * r * g


def _rms_bwd(x, g, dy):
    n = x.shape[-1]
    r = lax.rsqrt(jnp.mean(x * x, axis=-1, keepdims=True) + EPS)
    gy = dy * g
    dx = r * gy - x * (r * r * r * (1.0 / n)) * jnp.sum(x * gy, axis=-1, keepdims=True)
    return dx, jnp.sum(dy * x * r, axis=0, keepdims=True)


def _acc(ref, first, val):
    @pl.when(first)
    def _():
        ref[...] = val

    @pl.when(jnp.logical_not(first))
    def _():
        ref[...] += val


def _rms_fwd_call(x, g, name, gather):
    L, n = x.shape
    tl = _fit(L, 512)
    steps, na = L // tl, len(gather)

    def body(x_ref, g_ref, *refs):
        o_ref = refs[na]
        start, forward, finish = _gather_phases(refs[:na], refs[na + 1:2 * na + 1], *refs[2 * na + 1:])
        i = pl.program_id(0)
        pl.when(i == 0)(start)
        pl.when(i == steps // 2)(forward)
        o_ref[...] = _rms(x_ref[...], g_ref[...]).astype(BF16)
        pl.when(i == steps - 1)(finish)

    return pl.pallas_call(
        body, name=name, grid=(steps,), in_specs=[_row(tl, n), _full((1, n))] + [ANY_SPEC] * na,
        out_specs=[_row(tl, n)] + [ANY_SPEC] * na,
        out_shape=[jax.ShapeDtypeStruct((L, n), BF16)] + [jax.ShapeDtypeStruct((N_DEV,) + b.shape, b.dtype) for b in gather],
        scratch_shapes=_comm_sems(na), compiler_params=_params("arbitrary"))(x, g, *gather)


def _rope_lanes(shape):
    lane = lax.broadcasted_iota(jnp.int32, shape, 1)
    return lane, jnp.logical_and(lane >= QK_NOPE, lane < QK_HEAD)


def _rope_apply(x, cosf, sinf, lane):
    rot = jnp.where(lane < QK_NOPE + QK_ROPE // 2, -pltpu.roll(x, LANES - QK_ROPE // 2, 1), pltpu.roll(x, QK_ROPE // 2, 1))
    return x * cosf + rot * sinf


def _rope_apply_t(dy, cosf, sinf, lane, is_rope):
    g = dy * sinf
    rot_t = jnp.where(lane < QK_NOPE + QK_ROPE // 2, pltpu.roll(g, LANES - QK_ROPE // 2, 1), -pltpu.roll(g, QK_ROPE // 2, 1))
    return dy * cosf + jnp.where(is_rope, rot_t, 0.0)


def _mla_proj_call(proj, q_norm, kv_norm, w_uq_pt, w_kv_pt, pos_col, inv_freq):
    L = proj.shape[0]
    tl = _fit(L, 512)

    def body(p_ref, gq_ref, gk_ref, wq_ref, wkv_ref, pos_ref, f_ref, qn_ref, kn_ref, qo_ref, kvo_ref, cos_ref, sin_ref):
        qn = _rms(p_ref[:, P_CQ:P_CKV], gq_ref[...]).astype(BF16)
        kn = _rms(p_ref[:, P_CKV:P_KR], gk_ref[...]).astype(BF16)
        qn_ref[...] = qn
        kn_ref[...] = kn
        q_pad = _dot(qn, wq_ref[...], NT)
        kv_pad = _dot(kn, wkv_ref[...], NT)
        lane, is_rope = _rope_lanes((tl, LANES))
        ang = pos_ref[...] * f_ref[...]
        cosf = jnp.where(is_rope, jnp.cos(ang), jnp.where(lane < QK_NOPE, 1.0, 0.0))
        sinf = jnp.where(is_rope, jnp.sin(ang), 0.0)
        cos_ref[...] = cosf
        sin_ref[...] = sinf
        kr = _rope_apply(p_ref[:, P_KR:P_U], cosf, sinf, lane)
        for h in range(N_HEADS):
            qh = _rope_apply(q_pad[:, h * LANES:(h + 1) * LANES], cosf, sinf, lane)
            qo_ref[:, h * LANES:(h + 1) * LANES] = (qh * Q_PRESCALE).astype(BF16)
            kvo_ref[:, 2 * h * LANES:(2 * h + 1) * LANES] = (kv_pad[:, 2 * h * LANES:(2 * h + 1) * LANES] + kr).astype(BF16)
            vh = jnp.where(lane == V_HEAD, 1.0, kv_pad[:, (2 * h + 1) * LANES:(2 * h + 2) * LANES])
            kvo_ref[:, (2 * h + 1) * LANES:(2 * h + 2) * LANES] = vh.astype(BF16)

    shape = lambda n, dt: jax.ShapeDtypeStruct((L, n), dt)
    return pl.pallas_call(
        body, name="mla_proj", grid=(L // tl,),
        in_specs=[_row(tl, P_U), _full((1, Q_RANK)), _full((1, KV_RANK)), _full((HEAD_PAD, Q_RANK)),
                  _full((2 * HEAD_PAD, KV_RANK)), _row(tl, 1), _full((1, LANES))],
        out_specs=[_row(tl, Q_RANK), _row(tl, KV_RANK), _row(tl, HEAD_PAD), _row(tl, 2 * HEAD_PAD), _row(tl, LANES), _row(tl, LANES)],
        out_shape=[shape(Q_RANK, BF16), shape(KV_RANK, BF16), shape(HEAD_PAD, BF16), shape(2 * HEAD_PAD, BF16),
                   shape(LANES, F32), shape(LANES, F32)],
        compiler_params=_params("parallel"))(proj, q_norm, kv_norm, w_uq_pt, w_kv_pt, pos_col, inv_freq)


def _mla_proj_bwd_call(dq, dkv, cosf, sinf, proj, q_norm, kv_norm, w_uq_pt, w_kv_pt):
    L = dq.shape[0]
    tl = _fit(L, 512)

    def body(dq_ref, dkv_ref, cos_ref, sin_ref, p_ref, gq_ref, gk_ref, wq_ref, wkv_ref,
             dqo_ref, dkvo_ref, d_ref, dgq_ref, dgk_ref):
        first = pl.program_id(0) == 0
        lane, is_rope = _rope_lanes((tl, LANES))
        cosf, sinf = cos_ref[...], sin_ref[...]
        dk_sum = jnp.zeros((tl, LANES), F32)
        for h in range(N_HEADS):
            dqo_ref[:, h * LANES:(h + 1) * LANES] = _rope_apply_t(dq_ref[:, h * LANES:(h + 1) * LANES], cosf, sinf, lane, is_rope).astype(BF16)
            dk_sum = dk_sum + dkv_ref[:, 2 * h * LANES:(2 * h + 1) * LANES]
        dkvo_ref[...] = dkv_ref[...].astype(BF16)
        dqn = _dot(dqo_ref[...], wq_ref[...], NN)
        dkn = _dot(dkvo_ref[...], wkv_ref[...], NN)
        dcq, dgq = _rms_bwd(p_ref[:, P_CQ:P_CKV], gq_ref[...], dqn)
        dckv, dgk = _rms_bwd(p_ref[:, P_CKV:P_KR], gk_ref[...], dkn)
        d_ref[:, P_CQ:P_CKV] = dcq.astype(BF16)
        d_ref[:, P_CKV:P_KR] = dckv.astype(BF16)
        d_ref[:, P_KR:P_U] = _rope_apply_t(dk_sum, cosf, sinf, lane, is_rope).astype(BF16)
        _acc(dgq_ref, first, dgq)
        _acc(dgk_ref, first, dgk)

    shape = lambda n: jax.ShapeDtypeStruct((L, n), BF16)
    return pl.pallas_call(
        body, name="mla_proj_bwd", grid=(L // tl,),
        in_specs=[_row(tl, HEAD_PAD), _row(tl, 2 * HEAD_PAD), _row(tl, LANES), _row(tl, LANES), _row(tl, P_KR),
                  _full((1, Q_RANK)), _full((1, KV_RANK)), _full((HEAD_PAD, Q_RANK)), _full((2 * HEAD_PAD, KV_RANK))],
        out_specs=[_row(tl, HEAD_PAD), _row(tl, 2 * HEAD_PAD), _row(tl, P_U), _full((1, Q_RANK)), _full((1, KV_RANK))],
        out_shape=[shape(HEAD_PAD), shape(2 * HEAD_PAD), shape(P_U), jax.ShapeDtypeStruct((1, Q_RANK), F32),
                   jax.ShapeDtypeStruct((1, KV_RANK), F32)],
        compiler_params=_params("arbitrary"))(dq, dkv, cosf, sinf, proj, q_norm, kv_norm, w_uq_pt, w_kv_pt)


GATE_TILE = 256
GATE_ROWS = 1024


def _merge_call(proj, b_gate, pa, ps):
    L = proj.shape[0]
    tl = _fit(L, GATE_ROWS)
    nc = D_MODEL // GATE_TILE
    g0, g1 = P_GATE // GATE_TILE, (P_GATE + D_MODEL) // GATE_TILE

    def body(l0_ref, l1_ref, b0_ref, b1_ref, pa_ref, ps_ref, o_ref):
        s0 = _sigmoid(l0_ref[...] + b0_ref[...])
        s1 = _sigmoid(l1_ref[...] + b1_ref[...])
        o_ref[...] = (s0 * pa_ref[...] + s1 * ps_ref[...]).astype(BF16)

    blk = lambda off: pl.BlockSpec((tl, GATE_TILE), lambda i, j: (i, off + j))
    bias = lambda off: pl.BlockSpec((1, GATE_TILE), lambda i, j: (0, off + j))
    return pl.pallas_call(
        body, name="merge", grid=(L // tl, nc),
        in_specs=[blk(g0), blk(g1), bias(0), bias(nc), blk(0), blk(0)],
        out_specs=blk(0), out_shape=jax.ShapeDtypeStruct((L, D_MODEL), BF16),
        compiler_params=_params("parallel", "parallel"))(proj, proj, b_gate, b_gate, pa, ps)


def _merge_bwd_call(dm, proj, b_gate, pa, ps):
    L = proj.shape[0]
    tl = _fit(L, GATE_ROWS)
    nc = D_MODEL // GATE_TILE
    g0, g1 = P_GATE // GATE_TILE, (P_GATE + D_MODEL) // GATE_TILE

    def body(dm_ref, l0_ref, l1_ref, b0_ref, b1_ref, pa_ref, ps_ref, dpa_ref, dps_ref, dl0_ref, dl1_ref, db0_ref, db1_ref):
        first = pl.program_id(1) == 0
        dm_ = dm_ref[...]
        s0 = _sigmoid(l0_ref[...] + b0_ref[...])
        s1 = _sigmoid(l1_ref[...] + b1_ref[...])
        dpa_ref[...] = (dm_ * s0).astype(BF16)
        dps_ref[...] = (dm_ * s1).astype(BF16)
        dl0 = dm_ * pa_ref[...] * s0 * (1.0 - s0)
        dl1 = dm_ * ps_ref[...] * s1 * (1.0 - s1)
        dl0_ref[...] = dl0.astype(BF16)
        dl1_ref[...] = dl1.astype(BF16)
        _acc(db0_ref, first, jnp.sum(dl0, axis=0, keepdims=True))
        _acc(db1_ref, first, jnp.sum(dl1, axis=0, keepdims=True))

    blk = lambda off: pl.BlockSpec((tl, GATE_TILE), lambda j, i: (i, off + j))
    bias = lambda off: pl.BlockSpec((1, GATE_TILE), lambda j, i: (0, off + j))
    act = jax.ShapeDtypeStruct((L, D_MODEL), BF16)
    vec = jax.ShapeDtypeStruct((1, D_MODEL), F32)
    return pl.pallas_call(
        body, name="merge_bwd", grid=(nc, L // tl),
        in_specs=[blk(0), blk(g0), blk(g1), bias(0), bias(nc), blk(0), blk(0)],
        out_specs=[blk(0), blk(0), blk(0), blk(0), bias(0), bias(0)],
        out_shape=[act, act, act, act, vec, vec],
        compiler_params=_params("parallel", "arbitrary"))(dm, proj, proj, b_gate, b_gate, pa, ps)


def _post_mix_rows(o, x, g_post, g_fpre):
    x2 = x + _rms(o, g_post)
    return o, x2, _rms(x2, g_fpre)


def _ffn_out_rows(ff, x2, target, g_fpost):
    n = ff.shape[-1]
    err = x2 + _rms(ff, g_fpost) - target
    part = 0.5 * jnp.sum(jnp.sum(err * err, axis=-1, keepdims=True) * (1.0 / n), axis=0, keepdims=True)
    dy = err * (1.0 / n)
    dff, dg = _rms_bwd(ff, g_fpost, dy)
    return dy, dff, jnp.broadcast_to(part, (1, LANES)), dg


def _post_bwd_rows(dhn2, x2, dy, o, g_fpre, g_post):
    d1, dgf = _rms_bwd(x2, g_fpre, dhn2)
    dx2 = dy + d1
    do, dgp = _rms_bwd(o, g_post, dx2)
    return dx2, do, dgf, dgp


CONV_TILE = 256
CONV_WIDE = 1408
HALO = 16


def _conv3(w, b, x0, x1, x2):
    return b + w[2:3] * x0 + w[1:2] * x1 + w[0:1] * x2


def _down(x, by):
    return pltpu.roll(x, by, 0)


def _edge_down(edge, before, by):
    r = lax.broadcasted_iota(jnp.int32, edge.shape, 0)
    return jnp.where(r < by, pltpu.roll(before, by, 0), pltpu.roll(edge, by, 0))


def _edge_up(edge, after, by):
    r = lax.broadcasted_iota(jnp.int32, edge.shape, 0)
    return jnp.where(r >= HALO - by, pltpu.roll(after, HALO - by, 0), pltpu.roll(edge, HALO - by, 0))


def _gated(w_g, b_g, w_v, b_v, hg, hv, g1, g2, v1, v2):
    return _conv3(w_g, b_g, hg, g1, g2), _conv3(w_v, b_v, hv, v1, v2)


def _conv_specs(tl, tc, rows_inner):
    nh = tl // HALO
    if rows_inner:
        ij = lambda f: (lambda j, i: f(i, j))
    else:
        ij = lambda f: f
    cur = lambda off: pl.BlockSpec((tl, tc), ij(lambda i, j: (i, off + j)))
    prev = lambda off: pl.BlockSpec((HALO, tc), ij(lambda i, j: (jnp.maximum(i * nh - 1, 0), off + j)))
    par = lambda rows, off: pl.BlockSpec((rows, tc), ij(lambda i, j: (0, off + j)))
    return cur, prev, par


def _conv_act_call(h, conv_w, conv_b):
    L = h.shape[0]
    tl = _fit(L, 256)
    nc = D_FF // CONV_WIDE
    cur, prev, par = _conv_specs(tl, CONV_WIDE, False)

    def body(hg_ref, hv_ref, pg_ref, pv_ref, wg_ref, wv_ref, bg_ref, bv_ref, a_ref):
        not_first = (pl.program_id(0) > 0).astype(F32)
        par = (wg_ref[...], bg_ref[...], wv_ref[...], bv_ref[...])
        hg, hv = hg_ref[...], hv_ref[...]
        gate, val = _gated(*par, hg, hv, _down(hg, 1), _down(hg, 2), _down(hv, 1), _down(hv, 2))
        a_ref[...] = (_gelu(gate) * val).astype(BF16)
        eg, ev, bg, bv = hg[:HALO], hv[:HALO], pg_ref[...] * not_first, pv_ref[...] * not_first
        gate, val = _gated(*par, eg, ev, _edge_down(eg, bg, 1), _edge_down(eg, bg, 2),
                           _edge_down(ev, bv, 1), _edge_down(ev, bv, 2))
        a_ref[:HALO, :] = (_gelu(gate) * val).astype(BF16)

    return pl.pallas_call(
        body, name="conv_act", grid=(L // tl, nc),
        in_specs=[cur(0), cur(nc), prev(0), prev(nc), par(3, 0), par(3, nc), par(1, 0), par(1, nc)],
        out_specs=cur(0), out_shape=jax.ShapeDtypeStruct((L, D_FF), BF16),
        compiler_params=_params("parallel", "parallel"))(h, h, h, h, conv_w, conv_w, conv_b, conv_b)


def _conv_act_bwd_call(da, h, conv_w, conv_b):
    L = h.shape[0]
    tl = _fit(L, 512)
    nc = D_FF // CONV_TILE
    cur, prev, par = _conv_specs(tl, CONV_TILE, True)

    def body(da_ref, hg_ref, hv_ref, pg_ref, pv_ref, wg_ref, wv_ref, bg_ref, bv_ref,
             dg_ref, dv_ref, dwg_ref, dwv_ref, dbg_ref, dbv_ref):
        first = pl.program_id(1) == 0
        not_first = (pl.program_id(1) > 0).astype(F32)
        par = (wg_ref[...], bg_ref[...], wv_ref[...], bv_ref[...])
        col = lambda t: jnp.sum(t, axis=0, keepdims=True)

        def grads(da_, hg, hv, g1, g2, v1, v2):
            gate, val = _gated(*par, hg, hv, g1, g2, v1, v2)
            act, slope = _gelu_and_grad(gate)
            dgate = da_ * val * slope
            dval = da_ * act
            sums = (jnp.concatenate([col(dgate * g2), col(dgate * g1), col(dgate * hg)], axis=0),
                    jnp.concatenate([col(dval * v2), col(dval * v1), col(dval * hv)], axis=0), col(dgate), col(dval))
            return dgate, dval, sums

        da_, hg, hv = da_ref[...], hg_ref[...], hv_ref[...]
        shifted = (_down(hg, 1), _down(hg, 2), _down(hv, 1), _down(hv, 2))
        dgate, dval, whole = grads(da_, hg, hv, *shifted)
        dg_ref[...] = dgate.astype(BF16)
        dv_ref[...] = dval.astype(BF16)
        edge = lambda t: t[:HALO]
        _, _, wrapped = grads(edge(da_), edge(hg), edge(hv), *[edge(s) for s in shifted])
        eg, ev, bg, bv = edge(hg), edge(hv), pg_ref[...] * not_first, pv_ref[...] * not_first
        dgate, dval, fixed = grads(edge(da_), eg, ev, _edge_down(eg, bg, 1), _edge_down(eg, bg, 2),
                                   _edge_down(ev, bv, 1), _edge_down(ev, bv, 2))
        dg_ref[:HALO, :] = dgate.astype(BF16)
        dv_ref[:HALO, :] = dval.astype(BF16)
        for ref, a, b, c in zip((dwg_ref, dwv_ref, dbg_ref, dbv_ref), whole, wrapped, fixed):
            _acc(ref, first, a - b + c)

    act = jax.ShapeDtypeStruct((L, D_FF), BF16)
    w3 = jax.ShapeDtypeStruct((3, D_FF), F32)
    w1 = jax.ShapeDtypeStruct((1, D_FF), F32)
    return pl.pallas_call(
        body, name="conv_act_bwd", grid=(nc, L // tl),
        in_specs=[cur(0), cur(0), cur(nc), prev(0), prev(nc), par(3, 0), par(3, nc), par(1, 0), par(1, nc)],
        out_specs=[cur(0), cur(0), par(3, 0), par(3, 0), par(1, 0), par(1, 0)],
        out_shape=[act, act, w3, w3, w1, w1],
        compiler_params=_params("parallel", "arbitrary"))(da, h, h, h, h, conv_w, conv_w, conv_b, conv_b)


def _conv_t_call(dgate, dval, conv_w):
    L = dgate.shape[0]
    tl = _fit(L, 512)
    nc = D_FF // CONV_WIDE
    nh = tl // HALO

    def body(dg_ref, dv_ref, ng_ref, nv_ref, w_ref, o_ref):
        not_last = (pl.program_id(0) < L // tl - 1).astype(F32)

        def emit(d_ref, n_ref):
            c = d_ref[...].astype(F32)
            w = w_ref[...]
            o_ref[...] = _conv3(w, 0.0, c, pltpu.roll(c, tl - 1, 0), pltpu.roll(c, tl - 2, 0)).astype(BF16)
            edge, after = c[tl - HALO:], n_ref[...].astype(F32) * not_last
            o_ref[tl - HALO:, :] = _conv3(w, 0.0, edge, _edge_up(edge, after, 1), _edge_up(edge, after, 2)).astype(BF16)

        pl.when(pl.program_id(1) < nc)(lambda: emit(dg_ref, ng_ref))
        pl.when(pl.program_id(1) >= nc)(lambda: emit(dv_ref, nv_ref))

    gate_col = lambda j: jnp.minimum(j, nc - 1)
    val_col = lambda j: jnp.maximum(j - nc, 0)
    after_row = lambda i: jnp.minimum((i + 1) * nh, L // HALO - 1)
    tile = lambda col: pl.BlockSpec((tl, CONV_WIDE), lambda i, j: (i, col(j)))
    after = lambda col: pl.BlockSpec((HALO, CONV_WIDE), lambda i, j: (after_row(i), col(j)))
    return pl.pallas_call(
        body, name="conv_t", grid=(L // tl, 2 * nc),
        in_specs=[tile(gate_col), tile(val_col), after(gate_col), after(val_col), pl.BlockSpec((3, CONV_WIDE), lambda i, j: (0, j))],
        out_specs=pl.BlockSpec((tl, CONV_WIDE), lambda i, j: (i, j)),
        out_shape=jax.ShapeDtypeStruct((L, 2 * D_FF), BF16),
        compiler_params=_params("parallel", "parallel"))(dgate, dval, dgate, dval, conv_w)


def _glu_call(y1, w_glu, b_glu):
    L, n = y1.shape
    tl = _fit(L, 512)

    def body(y_ref, w_ref, b_ref, o_ref):
        y2 = _gelu(y_ref[...])
        z = _dot(y2.astype(BF16), w_ref[...], NN) + b_ref[...]
        o_ref[...] = (y2 * _sigmoid(z)).astype(BF16)

    return pl.pallas_call(
        body, name="glu", grid=(L // tl,), in_specs=[_row(tl, n), _full((n, n)), _full((1, n))],
        out_specs=_row(tl, n), out_shape=jax.ShapeDtypeStruct((L, n), BF16),
        compiler_params=_params("parallel"))(y1, w_glu, b_glu)


def _glu_bwd_call(dout, y1, w_glu, b_glu):
    L, n = y1.shape
    tl = _fit(L, 512)

    def body(do_ref, y_ref, w_ref, b_ref, dy_ref, dw_ref, db_ref):
        first = pl.program_id(0) == 0
        y1_ = y_ref[...]
        y2, slope = _gelu_and_grad(y1_)
        y2b = y2.astype(BF16)
        w = w_ref[...]
        sg = _sigmoid(_dot(y2b, w, NN) + b_ref[...])
        dout_ = do_ref[...].astype(F32)
        dz = dout_ * y2 * sg * (1.0 - sg)
        dzb = dz.astype(BF16)
        dy2 = dout_ * sg + _dot(dzb, w, NT)
        dy_ref[...] = dy2 * slope
        _acc(dw_ref, first, _dot(y2b, dzb, TN))
        _acc(db_ref, first, jnp.sum(dz, axis=0, keepdims=True))

    return pl.pallas_call(
        body, name="glu_bwd", grid=(L // tl,),
        in_specs=[_row(tl, n), _row(tl, n), _full((n, n)), _full((1, n))],
        out_specs=[_row(tl, n), _full((n, n)), _full((1, n))],
        out_shape=[jax.ShapeDtypeStruct((L, n), F32), jax.ShapeDtypeStruct((n, n), F32), jax.ShapeDtypeStruct((1, n), F32)],
        compiler_params=_params("arbitrary"))(dout, y1, w_glu, b_glu)


ATTN_TILE = 1024
ATTN_SCALE = 1.0 / math.sqrt(QK_HEAD)


ATTN_HEADS = 2
ATTN_GROUPS = N_HEADS // ATTN_HEADS
LOG2E = 1.0 / math.log(2.0)
Q_PRESCALE = ATTN_SCALE * LOG2E
ANY_SPEC = pl.BlockSpec(memory_space=pl.ANY)


def _attn_fwd_call(q, kv, blocks):
    L = q.shape[0]
    t = _fit(L, ATTN_TILE)
    nq = L // t
    n = len(blocks)

    def body(q_ref, kv_ref, *refs):
        blk_refs, (o_ref, lse_ref), gat_refs = refs[:n], refs[n:n + 2], refs[n + 2:2 * n + 2]
        m_s, acc_s, send_sems, recv_sems, local_sems = refs[2 * n + 2:]
        g, i = pl.program_id(0), pl.program_id(1)
        start, forward, finish = _gather_phases(blk_refs, gat_refs, send_sems, recv_sems, local_sems)
        pl.when(jnp.logical_and(g == 0, i == 0))(start)
        m_s[...] = jnp.full((ATTN_HEADS, t, 1), NEG, F32)
        acc_s[...] = jnp.zeros((ATTN_HEADS, t, LANES), F32)
        below = lax.broadcasted_iota(jnp.int32, (t, t), 1) <= lax.broadcasted_iota(jnp.int32, (t, t), 0)

        def block_step(kb, on_diagonal):
            rows = pl.ds(pl.multiple_of(kb * t, t), t)
            for a in range(ATTN_HEADS):
                s = _dot(q_ref[:, a * LANES:(a + 1) * LANES], kv_ref[rows, 2 * a * LANES:(2 * a + 1) * LANES], NT)
                if on_diagonal:
                    s = jnp.where(below, s, NEG)
                m_prev = m_s[a]
                m_new = jnp.maximum(m_prev, jnp.max(s, axis=1, keepdims=True))
                p = jnp.exp2(s - m_new)
                pv = _dot(p.astype(BF16), kv_ref[rows, (2 * a + 1) * LANES:(2 * a + 2) * LANES], NN)
                acc_s[a] = jnp.exp2(m_prev - m_new) * acc_s[a] + pv
                m_s[a] = m_new

        def step(kb, carry):
            block_step(kb, False)
            return carry

        lax.fori_loop(0, i, step, 0)
        block_step(i, True)
        lane = lax.broadcasted_iota(jnp.int32, (t, LANES), 1)
        for a in range(ATTN_HEADS):
            acc = acc_s[a]
            l = jnp.sum(jnp.where(lane == V_HEAD, acc, 0.0), axis=1, keepdims=True)
            o_ref[:, a * LANES:(a + 1) * LANES] = (acc / l).astype(BF16)
            lse_ref[a] = m_s[a] + jnp.log(l) * LOG2E
        pl.when(jnp.logical_and(g == (3 * ATTN_GROUPS) // 4, i == 0))(forward)
        pl.when(jnp.logical_and(g == ATTN_GROUPS - 1, i == nq - 1))(finish)

    gw = ATTN_HEADS * LANES
    return pl.pallas_call(
        body, name="attn_fwd", grid=(ATTN_GROUPS, nq),
        in_specs=[pl.BlockSpec((t, gw), lambda g, i: (i, g)),
                  pl.BlockSpec((L, 2 * gw), lambda g, i: (0, g))] + [ANY_SPEC] * n,
        out_specs=[pl.BlockSpec((t, gw), lambda g, i: (i, g)),
                   pl.BlockSpec((ATTN_HEADS, t, 1), lambda g, i: (g, i, 0))] + [ANY_SPEC] * n,
        out_shape=[jax.ShapeDtypeStruct((L, HEAD_PAD), BF16), jax.ShapeDtypeStruct((N_HEADS, L, 1), F32)]
        + [jax.ShapeDtypeStruct((N_DEV,) + b.shape, b.dtype) for b in blocks],
        scratch_shapes=[pltpu.VMEM((ATTN_HEADS, t, 1), F32), pltpu.VMEM((ATTN_HEADS, t, LANES), F32)] + _comm_sems(n),
        compiler_params=_params("arbitrary", "arbitrary", vmem=VMEM_BIG))(q, kv, *blocks)


def _attn_bwd_call(q, kv, o, do, lse, parts, blocks):
    L = q.shape[0]
    t = _fit(L, ATTN_TILE)
    nq = L // t
    n1, n = len(parts), len(parts) + len(blocks)

    def body(q_ref, do_ref, o_ref, lse_ref, kv_ref, *refs):
        in_refs, (dq_ref, dkv_ref), out_refs = refs[:n], refs[n:n + 2], refs[n + 2:2 * n + 2]
        dk_s, dv_s = refs[2 * n + 2:2 * n + 4]
        g, j = pl.program_id(0), pl.program_id(1)
        start, finish = _exchange_phases(in_refs[:n1], out_refs[:n1], *refs[2 * n + 4:2 * n + 7])
        start_blocks, finish_blocks = _exchange_phases(in_refs[n1:], out_refs[n1:], *refs[2 * n + 7:], same_source=True)

        @pl.when(jnp.logical_and(g == 0, j == 0))
        def _():
            start()
            start_blocks()

        @pl.when(j == 0)
        def _():
            dq_ref[...] = jnp.zeros((L, ATTN_HEADS * LANES), F32)

        dk_s[...] = jnp.zeros((ATTN_HEADS, t, LANES), F32)
        dv_s[...] = jnp.zeros((ATTN_HEADS, t, LANES), F32)
        below = lax.broadcasted_iota(jnp.int32, (t, t), 1) <= lax.broadcasted_iota(jnp.int32, (t, t), 0)

        def block_step(i, on_diagonal):
            rows = pl.ds(pl.multiple_of(i * t, t), t)
            for a in range(ATTN_HEADS):
                lanes = slice(a * LANES, (a + 1) * LANES)
                qi = q_ref[rows, lanes]
                doi = do_ref[rows, lanes]
                kblk = kv_ref[:, 2 * a * LANES:(2 * a + 1) * LANES]
                delta = jnp.sum(doi.astype(F32) * o_ref[rows, lanes].astype(F32), axis=1, keepdims=True)
                s = _dot(qi, kblk, NT)
                if on_diagonal:
                    s = jnp.where(below, s, NEG)
                p = jnp.exp2(s - lse_ref[a, rows, :])
                dv_s[a] += _dot(p.astype(BF16), doi, TN)
                ds = (p * (_dot(doi, kv_ref[:, (2 * a + 1) * LANES:(2 * a + 2) * LANES], NT) - delta)).astype(BF16)
                dk_s[a] += _dot(ds, qi, TN)
                dq_ref[rows, lanes] += _dot(ds, kblk, NN) * ATTN_SCALE

        def step(i, carry):
            block_step(i, False)
            return carry

        block_step(j, True)
        lax.fori_loop(j + 1, nq, step, 0)
        for a in range(ATTN_HEADS):
            dkv_ref[:, 2 * a * LANES:(2 * a + 1) * LANES] = dk_s[a] * (1.0 / LOG2E)
            dkv_ref[:, (2 * a + 1) * LANES:(2 * a + 2) * LANES] = dv_s[a]
        @pl.when(jnp.logical_and(g == ATTN_GROUPS - 1, j == nq - 1))
        def _():
            finish()
            finish_blocks()

    gw = ATTN_HEADS * LANES
    whole = lambda: pl.BlockSpec((L, gw), lambda g, j: (0, g))
    acc = pltpu.VMEM((ATTN_HEADS, t, LANES), F32)
    return pl.pallas_call(
        body, name="attn_bwd", grid=(ATTN_GROUPS, nq),
        in_specs=[whole(), whole(), whole(), pl.BlockSpec((ATTN_HEADS, L, 1), lambda g, j: (g, 0, 0)),
                  pl.BlockSpec((t, 2 * gw), lambda g, j: (j, g))] + [ANY_SPEC] * n,
        out_specs=[whole(), pl.BlockSpec((t, 2 * gw), lambda g, j: (j, g))] + [ANY_SPEC] * n,
        out_shape=[jax.ShapeDtypeStruct((L, HEAD_PAD), F32), jax.ShapeDtypeStruct((L, 2 * HEAD_PAD), F32)]
        + [jax.ShapeDtypeStruct(p.shape, p.dtype) for p in parts]
        + [jax.ShapeDtypeStruct((N_DEV,) + b.shape, b.dtype) for b in blocks],
        scratch_shapes=[acc, acc] + _comm_sems(n1) + _comm_sems(n - n1),
        compiler_params=_params("arbitrary", "arbitrary", vmem=VMEM_BIG))(q, do, o, lse, kv, *parts, *blocks)


def _disc(lr, li, ldt, br, bi):
    dt = jnp.exp(ldt)
    mag = jnp.exp(lr * dt)
    ang = li * dt
    a_re, a_im = mag * jnp.cos(ang), mag * jnp.sin(ang)
    den = lr * lr + li * li
    n_re, n_im = a_re - 1.0, a_im
    z_re = (n_re * lr + n_im * li) / den
    z_im = (n_im * lr - n_re * li) / den
    return a_re, a_im, z_re * br - z_im * bi, z_re * bi + z_im * br


def _disc_call(lr, li, ldt, br, bi):
    def body(lr_ref, li_ref, ldt_ref, br_ref, bi_ref, ar_ref, ai_ref, bbr_ref, bbi_ref):
        ar_ref[...], ai_ref[...], bbr_ref[...], bbi_ref[...] = _disc(
            lr_ref[...], li_ref[...], ldt_ref[...], br_ref[...], bi_ref[...])

    c1 = jax.ShapeDtypeStruct((SSM_NSTATE, 1), F32)
    c16 = jax.ShapeDtypeStruct((SSM_NSTATE, SSM_GROUP), F32)
    return pl.pallas_call(body, name="ssm_disc", out_shape=[c1, c1, c16, c16])(lr, li, ldt, br, bi)


def _disc_bwd_call(lr, li, ldt, br, bi, dar, dai, dbbr, dbbi):
    def body(lr_ref, li_ref, ldt_ref, br_ref, bi_ref, dar_ref, dai_ref, dbbr_ref, dbbi_ref,
             dlr_ref, dli_ref, dldt_ref, dbr_ref, dbi_ref):
        _, vjp = jax.vjp(_disc, lr_ref[...], li_ref[...], ldt_ref[...], br_ref[...], bi_ref[...])
        dlr_ref[...], dli_ref[...], dldt_ref[...], dbr_ref[...], dbi_ref[...] = vjp(
            (dar_ref[...], dai_ref[...], dbbr_ref[...], dbbi_ref[...]))

    c1 = jax.ShapeDtypeStruct((SSM_NSTATE, 1), F32)
    c16 = jax.ShapeDtypeStruct((SSM_NSTATE, SSM_GROUP), F32)
    return pl.pallas_call(body, name="ssm_disc_bwd", out_shape=[c1, c1, c1, c16, c16])(
        lr, li, ldt, br, bi, dar, dai, dbbr, dbbi)


SSM_ROWS = 512
SSM_CW = SSM_NSTATE // SSM_CHUNKS
SSM_CU = SSM_WIDTH // SSM_CHUNKS


def _cmul(ar, ai, br, bi):
    return ar * br - ai * bi, ar * bi + ai * br


def _power(ar1, ai1, n):
    res, base = None, (ar1, ai1)
    while n:
        if n & 1:
            res = base if res is None else _cmul(res[0], res[1], base[0], base[1])
        n >>= 1
        if n:
            base = _cmul(base[0], base[1], base[0], base[1])
    return res


def _tile(k):
    return pl.ds(pl.multiple_of(k * 8, 8), 8)


def _ssm_fwd_call(u, a_re, a_im, bb_re, bb_im, cm_re, cm_im, d_skip):
    L = u.shape[0]
    seg = L // 8
    rb = _fit(L, SSM_ROWS)

    def body(u_ref, ar_ref, ai_ref, bbr_ref, bbi_ref, cmr_ref, cmi_ref, d_ref, y_ref, sre_hbm, sim_hbm,
             s_re, s_im, sems):
        q = pl.program_id(0)

        def bu_step(r, c):
            rows = pl.ds(pl.multiple_of(r * rb, rb), rb)
            ub = u_ref[rows, :].astype(BF16)
            s_re[rows, :] = _dot(ub, bbr_ref[0], NN)
            s_im[rows, :] = _dot(ub, bbi_ref[0], NN)
            return c

        lax.fori_loop(0, L // rb, bu_step, 0)
        ar1, ai1 = ar_ref[...], ai_ref[...]
        ar = jnp.broadcast_to(ar1, (8, SSM_CW))
        ai = jnp.broadcast_to(ai1, (8, SSM_CW))

        def local(k, c):
            nr, ni = _cmul(ar, ai, c[0], c[1])
            nr = nr + s_re[_tile(k), :]
            ni = ni + s_im[_tile(k), :]
            s_re[_tile(k), :] = nr
            s_im[_tile(k), :] = ni
            return nr, ni

        zero8 = jnp.zeros((8, SSM_CW), F32)
        lax.fori_loop(0, seg, local, (zero8, zero8))
        pr, pi = _power(ar1, ai1, seg)
        end_r = s_re[pl.ds((seg - 1) * 8, 8), :]
        end_i = s_im[pl.ds((seg - 1) * 8, 8), :]
        er = jnp.zeros((1, SSM_CW), F32)
        ei = jnp.zeros((1, SSM_CW), F32)
        rows_r, rows_i = [er], [ei]
        for j in range(7):
            tr, ti = _cmul(pr, pi, er, ei)
            er, ei = end_r[j:j + 1] + tr, end_i[j:j + 1] + ti
            rows_r.append(er)
            rows_i.append(ei)
        e_r = jnp.concatenate(rows_r, axis=0)
        e_i = jnp.concatenate(rows_i, axis=0)

        def fix(k, c):
            wr, wi = _cmul(c[0], c[1], ar, ai)
            fr, fi = _cmul(wr, wi, e_r, e_i)
            s_re[_tile(k), :] += fr
            s_im[_tile(k), :] += fi
            return wr, wi

        lax.fori_loop(0, seg, fix, (jnp.ones((8, SSM_CW), F32), zero8))
        out_r = pltpu.make_async_copy(s_re, sre_hbm.at[q], sems.at[0])
        out_i = pltpu.make_async_copy(s_im, sim_hbm.at[q], sems.at[1])
        out_r.start()
        out_i.start()

        def y_step(r, c):
            rows = pl.ds(pl.multiple_of(r * rb, rb), rb)
            y = _dot(s_re[rows, :].astype(BF16), cmr_ref[0], NN) - _dot(s_im[rows, :].astype(BF16), cmi_ref[0], NN)
            y_ref[rows, :] = y + d_ref[...] * u_ref[rows, :]
            return c

        lax.fori_loop(0, L // rb, y_step, 0)
        out_r.wait()
        out_i.wait()

    chunk = lambda rows, cols: pl.BlockSpec((rows, cols), lambda q: (0, q))
    mat = lambda r, c: pl.BlockSpec((1, r, c), lambda q: (q, 0, 0))
    anyspec = pl.BlockSpec(memory_space=pl.ANY)
    states = jax.ShapeDtypeStruct((SSM_CHUNKS, L, SSM_CW), F32)
    return pl.pallas_call(
        body, name="ssm_fwd", grid=(SSM_CHUNKS,),
        in_specs=[chunk(L, SSM_CU), chunk(1, SSM_CW), chunk(1, SSM_CW), mat(SSM_CU, SSM_CW), mat(SSM_CU, SSM_CW),
                  mat(SSM_CW, SSM_CU), mat(SSM_CW, SSM_CU), chunk(1, SSM_CU)],
        out_specs=[chunk(L, SSM_CU), anyspec, anyspec],
        out_shape=[jax.ShapeDtypeStruct((L, SSM_WIDTH), F32), states, states],
        scratch_shapes=[pltpu.VMEM((L, SSM_CW), F32), pltpu.VMEM((L, SSM_CW), F32), pltpu.SemaphoreType.DMA((2,))],
        compiler_params=_params("arbitrary", vmem=VMEM_BIG))(u, a_re, a_im, bb_re, bb_im, cm_re, cm_im, d_skip)


def _ssm_bwd_call(dy, u, s_re_all, s_im_all, a_re, a_im, bb_re, bb_im, cm_re, cm_im, d_skip):
    L = u.shape[0]
    seg = L // 8
    rb = _fit(L, SSM_ROWS)

    def body(dy_ref, u_ref, sre_hbm, sim_hbm, ar_ref, ai_ref, bbr_ref, bbi_ref, cmr_ref, cmi_ref, d_ref,
             du_ref, dbbr_ref, dbbi_ref, dcmr_ref, dcmi_ref, dar_ref, dai_ref, dd_ref,
             g_re, g_im, s_re, s_im, sems):
        q = pl.program_id(0)
        in_r = pltpu.make_async_copy(sre_hbm.at[q], s_re, sems.at[0])
        in_i = pltpu.make_async_copy(sim_hbm.at[q], s_im, sems.at[1])
        in_r.start()
        in_i.start()

        def ds_step(r, c):
            rows = pl.ds(pl.multiple_of(r * rb, rb), rb)
            dyb = dy_ref[rows, :].astype(BF16)
            g_re[rows, :] = _dot(dyb, cmr_ref[0], NT)
            g_im[rows, :] = -_dot(dyb, cmi_ref[0], NT)
            return c

        lax.fori_loop(0, L // rb, ds_step, 0)
        ar1, ai1 = ar_ref[...], ai_ref[...]
        ar = jnp.broadcast_to(ar1, (8, SSM_CW))
        nai = jnp.broadcast_to(-ai1, (8, SSM_CW))

        def local(kk, c):
            k = seg - 1 - kk
            nr, ni = _cmul(ar, nai, c[0], c[1])
            nr = nr + g_re[_tile(k), :]
            ni = ni + g_im[_tile(k), :]
            g_re[_tile(k), :] = nr
            g_im[_tile(k), :] = ni
            return nr, ni

        zero8 = jnp.zeros((8, SSM_CW), F32)
        lax.fori_loop(0, seg, local, (zero8, zero8))
        pr, pi = _power(ar1, -ai1, seg)
        head_r = g_re[pl.ds(0, 8), :]
        head_i = g_im[pl.ds(0, 8), :]
        fr = jnp.zeros((1, SSM_CW), F32)
        fi = jnp.zeros((1, SSM_CW), F32)
        rows_r, rows_i = [fr], [fi]
        for j in range(6, -1, -1):
            tr, ti = _cmul(pr, pi, fr, fi)
            fr, fi = head_r[j + 1:j + 2] + tr, head_i[j + 1:j + 2] + ti
            rows_r.insert(0, fr)
            rows_i.insert(0, fi)
        f_r = jnp.concatenate(rows_r, axis=0)
        f_i = jnp.concatenate(rows_i, axis=0)
        in_r.wait()
        in_i.wait()

        def fixed(k, wr, wi):
            xr, xi = _cmul(wr, wi, f_r, f_i)
            gr = g_re[_tile(k), :] + xr
            gi = g_im[_tile(k), :] + xi
            g_re[_tile(k), :] = gr
            g_im[_tile(k), :] = gi
            return gr, gi

        def fix(kk, c):
            k = seg - 1 - kk
            wr, wi = _cmul(c[0], c[1], ar, nai)
            gr, gi = fixed(k, wr, wi)
            pr_, pi_ = s_re[_tile(k - 1), :], s_im[_tile(k - 1), :]
            return wr, wi, c[2] + gr * pr_ + gi * pi_, c[3] + gi * pr_ - gr * pi_

        wr, wi, acc_r, acc_i = lax.fori_loop(0, seg - 1, fix, (jnp.ones((8, SSM_CW), F32), zero8, zero8, zero8))
        wr, wi = _cmul(wr, wi, ar, nai)
        gr, gi = fixed(0, wr, wi)
        row8 = lax.broadcasted_iota(jnp.int32, (8, SSM_CW), 0)
        pr_ = jnp.where(row8 > 0, pltpu.roll(s_re[pl.ds((seg - 1) * 8, 8), :], 1, 0), 0.0)
        pi_ = jnp.where(row8 > 0, pltpu.roll(s_im[pl.ds((seg - 1) * 8, 8), :], 1, 0), 0.0)
        acc_r = acc_r + gr * pr_ + gi * pi_
        acc_i = acc_i + gi * pr_ - gr * pi_
        dar_ref[...] = jnp.sum(acc_r, axis=0, keepdims=True)
        dai_ref[...] = jnp.sum(acc_i, axis=0, keepdims=True)

        dbbr_ref[...] = jnp.zeros((1, SSM_CU, SSM_CW), F32)
        dbbi_ref[...] = jnp.zeros((1, SSM_CU, SSM_CW), F32)
        dcmr_ref[...] = jnp.zeros((1, SSM_CW, SSM_CU), F32)
        dcmi_ref[...] = jnp.zeros((1, SSM_CW, SSM_CU), F32)
        dd_ref[...] = jnp.zeros((1, SSM_CU), F32)

        def grad_step(r, c):
            rows = pl.ds(pl.multiple_of(r * rb, rb), rb)
            ub, dyv = u_ref[rows, :], dy_ref[rows, :]
            ubb, dyb = ub.astype(BF16), dyv.astype(BF16)
            grb, gib = g_re[rows, :].astype(BF16), g_im[rows, :].astype(BF16)
            dbbr_ref[0] += _dot(ubb, grb, TN)
            dbbi_ref[0] += _dot(ubb, gib, TN)
            dcmr_ref[0] += _dot(s_re[rows, :].astype(BF16), dyb, TN)
            dcmi_ref[0] -= _dot(s_im[rows, :].astype(BF16), dyb, TN)
            du_ref[rows, :] = _dot(grb, bbr_ref[0], NT) + _dot(gib, bbi_ref[0], NT) + d_ref[...] * dyv
            dd_ref[...] += jnp.sum(dyv * ub, axis=0, keepdims=True)
            return c

        lax.fori_loop(0, L // rb, grad_step, 0)

    chunk = lambda rows, cols: pl.BlockSpec((rows, cols), lambda q: (0, q))
    mat = lambda r, c: pl.BlockSpec((1, r, c), lambda q: (q, 0, 0))
    anyspec = pl.BlockSpec(memory_space=pl.ANY)
    big = lambda: pltpu.VMEM((L, SSM_CW), F32)
    return pl.pallas_call(
        body, name="ssm_bwd", grid=(SSM_CHUNKS,),
        in_specs=[chunk(L, SSM_CU), chunk(L, SSM_CU), anyspec, anyspec, chunk(1, SSM_CW), chunk(1, SSM_CW),
                  mat(SSM_CU, SSM_CW), mat(SSM_CU, SSM_CW), mat(SSM_CW, SSM_CU), mat(SSM_CW, SSM_CU), chunk(1, SSM_CU)],
        out_specs=[chunk(L, SSM_CU), mat(SSM_CU, SSM_CW), mat(SSM_CU, SSM_CW), mat(SSM_CW, SSM_CU), mat(SSM_CW, SSM_CU),
                   chunk(1, SSM_CW), chunk(1, SSM_CW), chunk(1, SSM_CU)],
        out_shape=[jax.ShapeDtypeStruct((L, SSM_WIDTH), F32),
                   jax.ShapeDtypeStruct((SSM_CHUNKS, SSM_CU, SSM_CW), F32), jax.ShapeDtypeStruct((SSM_CHUNKS, SSM_CU, SSM_CW), F32),
                   jax.ShapeDtypeStruct((SSM_CHUNKS, SSM_CW, SSM_CU), F32), jax.ShapeDtypeStruct((SSM_CHUNKS, SSM_CW, SSM_CU), F32),
                   jax.ShapeDtypeStruct((1, SSM_NSTATE), F32), jax.ShapeDtypeStruct((1, SSM_NSTATE), F32),
                   jax.ShapeDtypeStruct((1, SSM_WIDTH), F32)],
        scratch_shapes=[big(), big(), big(), big(), pltpu.SemaphoreType.DMA((2,))],
        compiler_params=_params("arbitrary", vmem=VMEM_BIG))(
            dy, u, s_re_all, s_im_all, a_re, a_im, bb_re, bb_im, cm_re, cm_im, d_skip)


def _place():
    return lax.axis_index("x"), lax.axis_index("y"), lax.axis_index("c")


def _small_gather_call(blocks, name):
    n = len(blocks)

    def body(*refs):
        start, finish = _exchange_phases(refs[:n], refs[n:2 * n], *refs[2 * n:], same_source=True)
        start()
        finish()

    return pl.pallas_call(
        body, name=name, in_specs=[ANY_SPEC] * n, out_specs=[ANY_SPEC] * n,
        out_shape=[jax.ShapeDtypeStruct((N_DEV,) + b.shape, b.dtype) for b in blocks],
        scratch_shapes=_comm_sems(n))(*blocks)


def _comm_sems(n):
    return [pltpu.SemaphoreType.DMA((7 * n,)), pltpu.SemaphoreType.DMA((7 * n,)), pltpu.SemaphoreType.DMA((n,))]


def _gather_phases(x_refs, out_refs, send_sems, recv_sems, local_sems):
    x, y, c = _place()
    me, sibling = (x, y, c), (x, y, 1 - c)
    chips = [(1 - x, y), (x, 1 - y), (1 - x, 1 - y)]
    n = len(x_refs)

    def copy(k, a, blk, to, from_input=False):
        slot = out_refs[a].at[4 * blk[0] + 2 * blk[1] + blk[2]]
        return pltpu.make_async_remote_copy(
            src_ref=x_refs[a] if from_input else slot, dst_ref=slot,
            send_sem=send_sems.at[k * n + a], recv_sem=recv_sems.at[k * n + a], device_id=to, device_id_type=MESH_ID)

    mine = [pltpu.make_async_copy(x_refs[a], out_refs[a].at[4 * x + 2 * y + c], local_sems.at[a]) for a in range(n)]
    first, passed = [], []
    for a in range(n):
        first.append(copy(0, a, me, sibling, True))
        first += [copy(1 + j, a, me, (*chip, c), True) for j, chip in enumerate(chips)]
        passed += [copy(4 + j, a, (*chip, c), sibling) for j, chip in enumerate(chips)]

    def start():
        for cp in mine + first:
            cp.start()

    def forward():
        for j, chip in enumerate(chips):
            for a in range(n):
                copy(1 + j, a, (*chip, c), me).wait_recv()
                passed[3 * a + j].start()

    def finish():
        for a in range(n):
            copy(0, a, sibling, me).wait_recv()
            for j, chip in enumerate(chips):
                copy(4 + j, a, (*chip, 1 - c), me).wait_recv()
        for cp in first + passed:
            cp.wait_send()
        for cp in mine:
            cp.wait()

    return start, forward, finish


def _exchange_phases(p_refs, out_refs, send_sems, recv_sems, local_sems, same_source=False):
    x, y, c = _place()
    me = 4 * x + 2 * y + c
    n = len(p_refs)

    def flip(k):
        px = 1 - x if k & 4 else x
        py = 1 - y if k & 2 else y
        pc = 1 - c if k & 1 else c
        return (px, py, pc), 4 * px + 2 * py + pc

    def source(a, slot):
        return p_refs[a] if same_source else p_refs[a].at[slot]

    def copy(k, a, landing):
        peer, peer_slot = flip(k)
        return pltpu.make_async_remote_copy(
            src_ref=source(a, peer_slot), dst_ref=out_refs[a].at[peer_slot if landing else me],
            send_sem=send_sems.at[(k - 1) * n + a], recv_sem=recv_sems.at[(k - 1) * n + a],
            device_id=peer, device_id_type=MESH_ID)

    mine = [pltpu.make_async_copy(source(a, me), out_refs[a].at[me], local_sems.at[a]) for a in range(n)]
    sends = [copy(k, a, False) for k in range(1, N_DEV) for a in range(n)]

    def start():
        for cp in mine + sends:
            cp.start()

    def finish():
        for k in range(1, N_DEV):
            for a in range(n):
                copy(k, a, True).wait_recv()
        for cp in sends:
            cp.wait_send()
        for cp in mine:
            cp.wait()

    return start, finish


def _adam_math(g, w, m, v):
    c1 = 1.0 / (1.0 - ADAM_B1 ** ADAM_STEP)
    c2 = 1.0 / (1.0 - ADAM_B2 ** ADAM_STEP)
    m_new = ADAM_B1 * m + (1.0 - ADAM_B1) * g
    v_new = ADAM_B2 * v + (1.0 - ADAM_B2) * (g * g)
    delta = -ADAM_LR * ((m_new * c1) / (jnp.sqrt(v_new * c2) + ADAM_EPS) + ADAM_WD * w)
    return g, delta, m_new, v_new


def _sum_slices(s_ref):
    g = s_ref[0].astype(F32)
    for k in range(1, N_DEV):
        g = g + s_ref[k].astype(F32)
    return g


def _adam_call(slices, w, m, v, name):
    d1, rest = w.shape[1], w.shape[2:]
    zeros = (0,) * len(rest)
    by_lanes = len(rest) == 1 and d1 > 256 and d1 % 16 != 0
    if by_lanes:
        tile = _fit(rest[0], 256)
        steps = rest[0] // tile
        own = pl.BlockSpec((1, d1, tile), lambda i: (0, 0, i))
        sl = pl.BlockSpec((N_DEV, 1, d1, tile), lambda i: (0, 0, 0, i))
    else:
        tile = _fit(d1, 256, 16) if len(rest) == 1 else _fit(d1, 8, 8)
        steps = d1 // tile
        own = pl.BlockSpec((1, tile) + rest, lambda i: (0, i) + zeros)
        sl = pl.BlockSpec((N_DEV, 1, tile) + rest, lambda i: (0, 0, i) + zeros)

    def body(s_ref, w_ref, m_ref, v_ref, g_ref, d_ref, mo_ref, vo_ref):
        g_ref[...], d_ref[...], mo_ref[...], vo_ref[...] = _adam_math(_sum_slices(s_ref), w_ref[...], m_ref[...], v_ref[...])

    out = jax.ShapeDtypeStruct(w.shape, F32)
    return pl.pallas_call(
        body, name=name, grid=(steps,), in_specs=[sl, own, own, own],
        out_specs=[own, own, own, own], out_shape=[out, out, out, out],
        compiler_params=_params("parallel"))(slices, w, m, v)


def _adam_small_call(rows_all, row_params, slices, params):
    nr, n = len(row_params), len(row_params) + len(params)

    def row_sum(rows_ref, a, width):
        g = rows_ref[0, pl.ds(a, 1), pl.ds(0, width)]
        for k in range(1, N_DEV):
            g = g + rows_ref[k, pl.ds(a, 1), pl.ds(0, width)]
        return g

    def body(rows_ref, *refs):
        slice_refs, wmv, outs = refs[:n - nr], refs[n - nr:n - nr + 3 * n], refs[n - nr + 3 * n:]
        outs[4 * n][...] = row_sum(rows_ref, nr, LANES)
        for a in range(n):
            w_ref, m_ref, v_ref = wmv[3 * a:3 * a + 3]
            if a < nr:
                g = row_sum(rows_ref, a, w_ref.shape[1])
            else:
                g = _sum_slices(slice_refs[a - nr])
            res = _adam_math(g, w_ref[...], m_ref[...], v_ref[...])
            for r in range(4):
                outs[4 * a + r][...] = res[r]

    every = list(row_params) + list(params)
    flat = pl.pallas_call(
        body, name="adam_small",
        out_shape=[jax.ShapeDtypeStruct(w.shape, F32) for w, _, _ in every for _ in range(4)]
        + [jax.ShapeDtypeStruct((1, LANES), F32)],
        compiler_params=pltpu.CompilerParams(vmem_limit_bytes=VMEM_BIG),
    )(rows_all, *slices, *[t for wmv in every for t in wmv])
    return [flat[4 * a:4 * a + 4] for a in range(n)], flat[4 * n][0, 0]


BIG = (("w_in", 1024, 404, 1), ("w_uq", 384, 96, 1), ("w_uk", 256, 64, 1), ("w_uv", 256, 64, 1),
       ("w_glu", 64, 512, 0), ("w_branch_attn", 512, 128, 1), ("w_branch_ssm", 512, 128, 1),
       ("w_out", 128, 1024, 0), ("w_up", 1024, 704, 1), ("w_down", 352, 1024, 0), ("conv_w", 3, 704, 1))
GATHER_FIRST, GATHER_PROJ, GATHER_LATER = BIG[:1], BIG[1:4], BIG[8:] + BIG[4:8]
GRADS_EARLY, GRADS_PROJ, GRADS_LAST = BIG[8:] + BIG[4:8], BIG[1:4], BIG[:1]
SMALL = (("mix_norm_pre", (1024,)), ("q_norm", (384,)), ("kv_norm", (256,)), ("ssm_lambda_re", (32, 64)),
         ("ssm_lambda_im", (32, 64)), ("ssm_log_dt", (32,)), ("ssm_b_re", (32, 64, 16)), ("ssm_b_im", (32, 64, 16)),
         ("ssm_c_re", (32, 16, 64)), ("ssm_c_im", (32, 16, 64)), ("ssm_d", (32, 16)), ("b_glu", (512,)),
         ("b_gate", (2048,)), ("mix_norm_post", (1024,)), ("ffn_norm_pre", (1024,)), ("conv_b", (5632,)),
         ("ffn_norm_post", (1024,)))


TRANSPOSED = ("w_in", "w_uq", "w_uk", "w_uv", "w_up")


STORED_SWAP = {**{name: (1, 2) for name in TRANSPOSED}, "ssm_b_re": (2, 3), "ssm_b_im": (2, 3), "ssm_d": (1, 2)}


def _stored(name, arr):
    return jnp.swapaxes(arr, *STORED_SWAP[name]) if name in STORED_SWAP else arr


def _to_slices(name, full, rows, cols, axis):
    if name in TRANSPOSED:
        return full.reshape(N_DEV, cols, rows)
    if axis == 1:
        return full.reshape(rows, N_DEV, cols).transpose(1, 0, 2)
    return full.reshape(N_DEV, rows, cols)


def _from_slices(name, parts, rows, cols, axis):
    if name in TRANSPOSED:
        return parts.reshape(N_DEV * cols, rows)
    if axis == 1:
        return parts.transpose(1, 0, 2).reshape(rows, N_DEV * cols)
    return parts.reshape(N_DEV * rows, cols)


def _time_perm(a, L):
    return a.reshape(8, L // 8, a.shape[-1]).transpose(1, 0, 2).reshape(L, a.shape[-1])


def _time_unperm(a, L):
    return a.reshape(L // 8, 8, a.shape[-1]).transpose(1, 0, 2).reshape(L, a.shape[-1])


def _block_diag(w, rows_first):
    eye = jnp.eye(8, dtype=w.dtype)
    g = w.reshape(SSM_CHUNKS, 8, w.shape[1], w.shape[2])
    return jnp.einsum("qgrc,gk->qgrkc", g, eye).reshape(SSM_CHUNKS, 8 * w.shape[1], 8 * w.shape[2])


def _block_diag_t(m, r, c):
    eye = jnp.eye(8, dtype=m.dtype)
    return jnp.einsum("qgrkc,gk->qgrc", m.reshape(SSM_CHUNKS, 8, r, 8, c), eye).reshape(SSM_GROUPS, r, c)


def kernel(x, positions, mix_norm_pre, w_in, q_norm, w_uq, kv_norm, w_uk, w_uv, ssm_lambda_re, ssm_lambda_im, ssm_log_dt, ssm_b_re, ssm_b_im, ssm_c_re, ssm_c_im, ssm_d, w_glu, b_glu, w_branch_attn, w_branch_ssm, b_gate, w_out, mix_norm_post, ffn_norm_pre, w_up, conv_w, conv_b, w_down, ffn_norm_post, loss_target, m_mix_norm_pre, m_w_in, m_q_norm, m_w_uq, m_kv_norm, m_w_uk, m_w_uv, m_ssm_lambda_re, m_ssm_lambda_im, m_ssm_log_dt, m_ssm_b_re, m_ssm_b_im, m_ssm_c_re, m_ssm_c_im, m_ssm_d, m_w_glu, m_b_glu, m_w_branch_attn, m_w_branch_ssm, m_b_gate, m_w_out, m_mix_norm_post, m_ffn_norm_pre, m_w_up, m_conv_w, m_conv_b, m_w_down, m_ffn_norm_post, v_mix_norm_pre, v_w_in, v_q_norm, v_w_uq, v_kv_norm, v_w_uk, v_w_uv, v_ssm_lambda_re, v_ssm_lambda_im, v_ssm_log_dt, v_ssm_b_re, v_ssm_b_im, v_ssm_c_re, v_ssm_c_im, v_ssm_d, v_w_glu, v_b_glu, v_w_branch_attn, v_w_branch_ssm, v_b_gate, v_w_out, v_mix_norm_post, v_ffn_norm_pre, v_w_up, v_conv_w, v_conv_b, v_w_down, v_ffn_norm_post):
    given = dict(locals())
    L = x.shape[1]
    xs = x[0]
    target = loss_target[0]

    def shard_bits(group):
        return [given[name][0] if name == "conv_w" else _stored(name, given[name])[0].astype(BF16) for name, _, _, _ in group]

    W = {}

    def unpack_weights(gathered, group):
        for (name, rows, cols, axis), parts in zip(group, gathered):
            W[name] = _from_slices(name, parts, rows, cols, axis)

    hn1, *gathered_w_in = _rms_fwd_call(xs, mix_norm_pre, "rms_pre", shard_bits(GATHER_FIRST))
    unpack_weights(gathered_w_in, GATHER_FIRST)

    wit = W["w_in"]
    zero_rows = lambda r: jnp.zeros((r, D_MODEL), BF16)
    kr_end = P_KR + QK_ROPE
    w_in_pt = jnp.concatenate(
        [wit[:P_KR], zero_rows(QK_NOPE), wit[P_KR:kr_end], zero_rows(LANES - QK_HEAD), wit[kr_end:]], axis=0)

    proj, *gathered_proj = _mm(hn1, w_in_pt, "mm_in", tb=True, tn=1664, gather=shard_bits(GATHER_PROJ))
    unpack_weights(gathered_proj, GATHER_PROJ)
    head_rows = lambda wt, width: jnp.pad(wt.reshape(N_HEADS, width, wt.shape[1]), ((0, 0), (0, LANES - width), (0, 0)))
    w_uq_pt = head_rows(W["w_uq"], QK_HEAD).reshape(HEAD_PAD, Q_RANK)
    w_kv_pt = jnp.stack([head_rows(W["w_uk"], QK_NOPE), head_rows(W["w_uv"], V_HEAD)], axis=1
                        ).reshape(2 * HEAD_PAD, KV_RANK)
    half = jnp.arange(QK_ROPE // 2, dtype=F32)
    inv_freq = ROPE_THETA ** (-2.0 * half / QK_ROPE)
    inv_freq = jnp.pad(jnp.concatenate([inv_freq, inv_freq]), (QK_NOPE, LANES - QK_HEAD)).reshape(1, LANES)
    pos_col = positions.astype(F32).reshape(L, 1)
    qn, ckvn, q_r, kv_r, cosf, sinf = _mla_proj_call(proj, q_norm, kv_norm, w_uq_pt, w_kv_pt, pos_col, inv_freq)
    attn, lse, *gathered_later = _attn_fwd_call(q_r, kv_r, shard_bits(GATHER_LATER))
    unpack_weights(gathered_later, GATHER_LATER)
    w_ba_p = jnp.pad(W["w_branch_attn"].reshape(N_HEADS, V_HEAD, D_MODEL), ((0, 0), (0, LANES - V_HEAD), (0, 0))
                     ).reshape(HEAD_PAD, D_MODEL)

    col = lambda a: a.reshape(SSM_NSTATE, -1)
    lr_c, li_c = col(ssm_lambda_re[0]), col(ssm_lambda_im[0])
    ldt_c = col(jnp.broadcast_to(ssm_log_dt[0][:, None], (SSM_GROUPS, SSM_STATE)))
    br_c, bi_c = col(ssm_b_re[0]), col(ssm_b_im[0])
    a_re_c, a_im_c, bb_re_c, bb_im_c = _disc_call(lr_c, li_c, ldt_c, br_c, bi_c)
    a_re, a_im = a_re_c.reshape(1, SSM_NSTATE), a_im_c.reshape(1, SSM_NSTATE)
    to_bb = lambda b: _block_diag(b.reshape(SSM_GROUPS, SSM_STATE, SSM_GROUP).transpose(0, 2, 1), True).astype(BF16)
    bb_re, bb_im = to_bb(bb_re_c), to_bb(bb_im_c)
    to_cm = lambda c_: _block_diag(c_[0].transpose(0, 2, 1), True).astype(BF16)
    cm_re, cm_im = to_cm(ssm_c_re), to_cm(ssm_c_im)
    d_skip = ssm_d.reshape(1, SSM_WIDTH)
    u_p = _time_perm(proj[:, P_U:P_GATE], L)
    y1, s_re, s_im = _ssm_fwd_call(u_p, a_re, a_im, bb_re, bb_im, cm_re, cm_im, d_skip)
    w_glu_b = W["w_glu"]
    ssm_p = _glu_call(y1, w_glu_b, b_glu)
    ssm = _time_unperm(ssm_p, L)

    pa = _mm(attn, w_ba_p, "mm_ba")
    ps = _mm(ssm, W["w_branch_ssm"], "mm_bs")
    merged = _merge_call(proj, b_gate, pa, ps)
    wide = lambda dt: (D_MODEL, dt)
    o, x2, hn2 = _mm_rows(merged, W["w_out"], "mm_out", _post_mix_rows, [xs], [mix_norm_post, ffn_norm_pre],
                          [wide(F32), wide(F32), wide(BF16)], [])
    h = _mm(hn2, W["w_up"], "mm_up", tb=True, tn=1408)
    cw = W["conv_w"]
    act = _conv_act_call(h, cw, conv_b)
    dy, dff, loss_row, g_ffn_norm_post = _mm_rows(
        act, W["w_down"], "mm_down", _ffn_out_rows, [x2, target], [ffn_norm_post], [wide(F32), wide(BF16)],
        [LANES, D_MODEL], tk=1408)

    da = _mm(dff, W["w_down"], "mm_down_dx", tb=True, tn=1408)
    g_w_down = _mm_tn(act, dff, "mm_down_dw", tm=1408)
    dgate, dval, dcw_g, dcw_v, dcb_g, dcb_v = _conv_act_bwd_call(da, h, cw, conv_b)
    g_conv_w = jnp.concatenate([dcw_g, dcw_v], axis=1)
    g_conv_b = jnp.concatenate([dcb_g, dcb_v], axis=1)
    dh = _conv_t_call(dgate, dval, cw)
    dx2, do, g_ffn_norm_pre, g_mix_norm_post = _mm_rows(
        dh, W["w_up"], "mm_up_dx", _post_bwd_rows, [x2, dy, o], [ffn_norm_pre, mix_norm_post], [wide(F32), wide(BF16)],
        [D_MODEL, D_MODEL], tk=1408)
    g_w_up = _mm_tn(dh, hn2, "mm_up_dw", tm=1408)
    dmerged = _mm(do, W["w_out"], "mm_out_dx", tb=True)
    g_w_out = _mm_tn(merged, do, "mm_out_dw")
    dpa, dps, dl0, dl1, db0, db1 = _merge_bwd_call(dmerged, proj, b_gate, pa, ps)
    g_b_gate = jnp.concatenate([db0, db1], axis=1)
    dattn = _mm(dpa, w_ba_p, "mm_ba_dx", tb=True, out_dtype=BF16)
    g_w_ba = _mm_tn(attn, dpa, "mm_ba_dw").reshape(N_HEADS, LANES, D_MODEL)[:, :V_HEAD].reshape(N_HEADS * V_HEAD, D_MODEL)
    dssm = _mm(dps, W["w_branch_ssm"], "mm_bs_dx", tb=True)
    g_w_bs = _mm_tn(ssm, dps, "mm_bs_dw")

    dy1, g_w_glu, g_b_glu = _glu_bwd_call(_time_perm(dssm, L), y1, w_glu_b, b_glu)
    du_p, dbb_re, dbb_im, dcm_re, dcm_im, da_re, da_im, g_ssm_d = _ssm_bwd_call(
        dy1, u_p, s_re, s_im, a_re, a_im, bb_re, bb_im, cm_re, cm_im, d_skip)
    du = _time_unperm(du_p, L)
    from_bb = lambda m: col(_block_diag_t(m, SSM_GROUP, SSM_STATE).transpose(0, 2, 1))
    dlr, dli, dldt, dbr, dbi = _disc_bwd_call(
        lr_c, li_c, ldt_c, br_c, bi_c, da_re.reshape(SSM_NSTATE, 1), da_im.reshape(SSM_NSTATE, 1), from_bb(dbb_re), from_bb(dbb_im))
    g_c_re = _block_diag_t(dcm_re, SSM_STATE, SSM_GROUP).transpose(0, 2, 1)
    g_c_im = _block_diag_t(dcm_im, SSM_STATE, SSM_GROUP).transpose(0, 2, 1)

    def grad_slices(group, grads):
        return [_to_slices(name, grads[name], rows, cols, axis) for name, rows, cols, axis in group]

    early_grads = {"w_up": g_w_up, "w_down": g_w_down, "conv_w": g_conv_w, "w_glu": g_w_glu.astype(BF16),
                   "w_branch_attn": g_w_ba, "w_branch_ssm": g_w_bs, "w_out": g_w_out}
    b_stored = lambda d: d.reshape(SSM_GROUPS, SSM_STATE, SSM_GROUP).transpose(0, 2, 1)
    per_state = lambda d: d.reshape(SSM_GROUPS, SSM_STATE)
    ssm_partials = {"ssm_lambda_re": per_state(dlr), "ssm_lambda_im": per_state(dli),
                    "ssm_b_re": b_stored(dbr), "ssm_b_im": b_stored(dbi),
                    "ssm_c_re": g_c_re, "ssm_c_im": g_c_im, "ssm_d": g_ssm_d.reshape(SSM_GROUPS, SSM_GROUP).T}
    ssm_shapes = [(name, ssm_partials[name].shape) for name, _ in SMALL if name in ssm_partials]
    dq, dkv, *landed = _attn_bwd_call(
        q_r, kv_r, attn, dattn, lse, grad_slices(GRADS_EARLY, early_grads),
        [ssm_partials[name].reshape(-1, LANES) if len(shp) == 3 else ssm_partials[name].reshape((1,) + shp)
         for name, shp in ssm_shapes])
    received_early = landed[:len(GRADS_EARLY)]
    ssm_all = {name: got.reshape((N_DEV, 1) + shp) for (name, shp), got in zip(ssm_shapes, landed[len(GRADS_EARLY):])}
    dq_p, dkv_p, dlatent, g_q_norm, g_kv_norm = _mla_proj_bwd_call(
        dq, dkv, cosf, sinf, proj, q_norm, kv_norm, w_uq_pt, w_kv_pt)
    g_w_uq = _mm_tn(dq_p, qn, "mm_uq_dw").reshape(N_HEADS, LANES, Q_RANK)[:, :QK_HEAD].reshape(N_HEADS * QK_HEAD, Q_RANK)
    g_w_kv = _mm_tn(ckvn, dkv_p, "mm_ukv_dw").T.reshape(N_HEADS, 2, LANES, KV_RANK)
    g_w_uk = g_w_kv[:, 0, :QK_NOPE].reshape(N_HEADS * QK_NOPE, KV_RANK)
    g_w_uv = g_w_kv[:, 1, :V_HEAD].reshape(N_HEADS * V_HEAD, KV_RANK)
    dproj = jnp.concatenate([dlatent, du.astype(BF16), dl0, dl1], axis=1)
    proj_grads = {"w_uq": g_w_uq, "w_uk": g_w_uk, "w_uv": g_w_uv}
    g_w_in_pt, *received_proj = _mm_tn(dproj, hn1, "mm_in_dw", tm=1664, exchange=grad_slices(GRADS_PROJ, proj_grads))
    g_w_in = jnp.concatenate([g_w_in_pt[:P_KR], g_w_in_pt[P_KR + QK_NOPE:P_KR + QK_HEAD], g_w_in_pt[P_U:]], axis=0)
    grad_x, g_mix_norm_pre, *received_last = _mm_in_dx_call(
        dproj, w_in_pt, xs, dx2, mix_norm_pre, grad_slices(GRADS_LAST, {"w_in": g_w_in}))

    results = {}
    wmv = lambda name: tuple(_stored(name, given[prefix + name]) for prefix in ("", "m_", "v_"))
    unstored = lambda name, res: [_stored(name, r) for r in res]
    whole = ("w_uq", "w_uk", "w_uv", "w_glu", "w_branch_attn", "w_branch_ssm", "conv_w")
    landed_small = dict(ssm_all)
    for group, received in ((GRADS_EARLY, received_early), (GRADS_PROJ, received_proj), (GRADS_LAST, received_last)):
        for (name, _, _, _), rec in zip(group, received):
            if name in whole:
                landed_small[name] = rec[:, None]
            else:
                results[name] = unstored(name, _adam_call(rec[:, None], *wmv(name), "adam_" + name))

    vec_grads = {"mix_norm_pre": g_mix_norm_pre, "q_norm": g_q_norm, "kv_norm": g_kv_norm,
                 "ssm_log_dt": jnp.sum(dldt.reshape(SSM_GROUPS, SSM_STATE), axis=1),
                 "b_glu": g_b_glu, "b_gate": g_b_gate, "mix_norm_post": g_mix_norm_post,
                 "ffn_norm_pre": g_ffn_norm_pre, "ffn_norm_post": g_ffn_norm_post}
    vec_names = [name for name, _ in SMALL if name in vec_grads]
    width = max(shp[0] for name, shp in SMALL if name in vec_grads)
    rows = [jnp.pad(vec_grads[name].reshape(1, -1), ((0, 0), (0, width - vec_grads[name].size))) for name in vec_names]
    rows.append(jnp.pad(loss_row, ((0, 0), (0, width - LANES))))
    rows.append(jnp.zeros((-len(rows) % 8, width), F32))
    rows_all, landed_small["conv_b"] = _small_gather_call([jnp.concatenate(rows, axis=0), g_conv_b], "gather_small_grads")
    others = ["conv_b"] + [name for name, _ in ssm_shapes] + list(whole)
    small_results, loss = _adam_small_call(
        rows_all, [wmv(n) for n in vec_names], [landed_small[n] for n in others], [wmv(n) for n in others])
    for name, res in zip(vec_names + others, small_results):
        results[name] = unstored(name, res)

    order = ["mix_norm_pre", "w_in", "q_norm", "w_uq", "kv_norm", "w_uk", "w_uv", "ssm_lambda_re", "ssm_lambda_im",
             "ssm_log_dt", "ssm_b_re", "ssm_b_im", "ssm_c_re", "ssm_c_im", "ssm_d", "w_glu", "b_glu", "w_branch_attn",
             "w_branch_ssm", "b_gate", "w_out", "mix_norm_post", "ffn_norm_pre", "w_up", "conv_w", "conv_b", "w_down",
             "ffn_norm_post"]
    outs = [loss, grad_x[None]]
    for kind in range(4):
        outs += [results[name][kind] for name in order]
    return tuple(outs)
```

```python
import math

import jax
import jax.numpy as jnp
from jax import lax
from jax.experimental import pallas as pl
from jax.experimental.pallas import tpu as pltpu

F32 = jnp.float32
BF16 = jnp.bfloat16
MESH_ID = pl.DeviceIdType.MESH

N_DEV = 8
LANES = 128
D_MODEL = 1024
N_HEADS = 8
QK_NOPE = 64
QK_ROPE = 32
QK_HEAD = QK_NOPE + QK_ROPE
V_HEAD = 64
Q_RANK = 384
KV_RANK = 256
ROPE_THETA = 10000.0
SSM_WIDTH = 512
SSM_GROUP = 16
SSM_GROUPS = 32
SSM_STATE = 64
SSM_NSTATE = SSM_GROUPS * SSM_STATE
SSM_CHUNKS = 4
D_FF = 2816
EPS = 1e-6
ADAM_LR, ADAM_B1, ADAM_B2, ADAM_EPS, ADAM_WD, ADAM_STEP = 0.001, 0.9, 0.999, 1e-08, 0.01, 10

P_CQ, P_CKV, P_KR, P_U, P_GATE = 0, 384, 640, 768, 1280
HEAD_PAD = N_HEADS * LANES

VMEM_BIG = 52 * 1024 * 1024

_GELU_C0 = math.sqrt(2.0 / math.pi)
_GELU_C1 = 0.044715
NEG = -1e30


def _fit(n, pref, mult=LANES):
    if n <= pref:
        return n
    t = (pref // mult) * mult
    while t > 0 and n % t:
        t -= mult
    assert t > 0, (n, pref, mult)
    return t


def _gelu(x):
    return x * (0.5 * (1.0 + jnp.tanh(_GELU_C0 * x * (1.0 + _GELU_C1 * (x * x)))))


def _gelu_and_grad(x):
    x2 = x * x
    t = jnp.tanh(_GELU_C0 * x * (1.0 + _GELU_C1 * x2))
    half = 0.5 * (1.0 + t)
    return x * half, half + 0.5 * x * (1.0 - t * t) * _GELU_C0 * (1.0 + 3.0 * _GELU_C1 * x2)


def _sigmoid(x):
    return 1.0 / (1.0 + jnp.exp(-x))


def _dot(a, b, dims):
    return lax.dot_general(a, b, (dims, ((), ())), preferred_element_type=F32)


NN = ((1,), (0,))
NT = ((1,), (1,))
TN = ((0,), (0,))


def _params(*sem, vmem=None):
    return pltpu.CompilerParams(dimension_semantics=tuple(sem), vmem_limit_bytes=vmem)


def _mm(a, b, name, tb=False, out_dtype=F32, tm=1024, tn=1024, tk=1024, gather=()):
    M, K = a.shape
    if tb:
        N, K2 = b.shape
    else:
        K2, N = b.shape
    assert K == K2, (a.shape, b.shape, tb)
    tm, tn, tk = _fit(M, tm), _fit(N, tn), _fit(K, tk)
    nk = K // tk
    grid = (M // tm, N // tn, nk)
    steps = grid[0] * grid[1] * grid[2]
    dims = NT if tb else NN
    n = len(gather)

    def body(a_ref, b_ref, *refs):
        o_ref, scratch = refs[n], refs[2 * n + 1:]
        step = (pl.program_id(0) * grid[1] + pl.program_id(1)) * grid[2] + pl.program_id(2)
        if n:
            start, forward, finish = _gather_phases(refs[:n], refs[n + 1:2 * n + 1], *scratch[-3:])
            pl.when(step == 0)(start)
            pl.when(step == steps // 2)(forward)
        part = _dot(a_ref[...].astype(BF16), b_ref[...].astype(BF16), dims)
        if nk == 1:
            o_ref[...] = part.astype(out_dtype)
        else:
            acc_ref = scratch[0]
            k = pl.program_id(2)

            @pl.when(k == 0)
            def _():
                acc_ref[...] = part

            @pl.when(k > 0)
            def _():
                acc_ref[...] += part

            @pl.when(k == nk - 1)
            def _():
                o_ref[...] = acc_ref[...].astype(out_dtype)
        if n:
            pl.when(step == steps - 1)(finish)

    a_spec = pl.BlockSpec((tm, tk), lambda i, j, k: (i, k))
    b_spec = pl.BlockSpec((tn, tk), lambda i, j, k: (j, k)) if tb else pl.BlockSpec((tk, tn), lambda i, j, k: (k, j))
    landed = [jax.ShapeDtypeStruct((N_DEV,) + p.shape, p.dtype) for p in gather]
    out = pl.pallas_call(
        body, name=name, grid=grid,
        in_specs=[a_spec, b_spec] + [ANY_SPEC] * n,
        out_specs=[pl.BlockSpec((tm, tn), lambda i, j, k: (i, j))] + [ANY_SPEC] * n,
        out_shape=[jax.ShapeDtypeStruct((M, N), out_dtype)] + landed,
        scratch_shapes=([] if nk == 1 else [pltpu.VMEM((tm, tn), F32)]) + (_comm_sems(n) if n else []),
        compiler_params=_params(*(("arbitrary",) * 3 if n else ("parallel", "parallel", "arbitrary")), vmem=VMEM_BIG),
    )(a, b, *gather)
    return out if n else out[0]


def _mm_rows(a, b, name, epilogue, rows_in, vecs_in, rows_out, vecs_out, tb=False, tk=1024):
    M, K = a.shape
    N = b.shape[0] if tb else b.shape[1]
    tm, tk = _fit(M, 512), _fit(K, tk)
    nk = K // tk
    nr, nv, nro = len(rows_in), len(vecs_in), len(rows_out)

    def body(a_ref, b_ref, *refs):
        ins, outs, acc_ref = refs[:nr + nv], refs[nr + nv:nr + nv + nro + len(vecs_out)], refs[-1]
        i, k = pl.program_id(0), pl.program_id(1)
        part = _dot(a_ref[...], b_ref[...], NT if tb else NN)

        def finish(product):
            res = epilogue(product, *[r[...] for r in ins])
            for ref, val in zip(outs[:nro], res[:nro]):
                ref[...] = val.astype(ref.dtype)
            for ref, val in zip(outs[nro:], res[nro:]):
                _acc(ref, i == 0, val)

        if nk == 1:
            finish(part)
        else:
            @pl.when(k == 0)
            def _():
                acc_ref[...] = part

            @pl.when(jnp.logical_and(k > 0, k < nk - 1))
            def _():
                acc_ref[...] += part

            @pl.when(k == nk - 1)
            def _():
                finish(acc_ref[...] + part)

    row = lambda w: pl.BlockSpec((tm, w), lambda i, k: (i, 0))
    vec = lambda w: pl.BlockSpec((1, w), lambda i, k: (0, 0))
    b_spec = pl.BlockSpec((N, tk), lambda i, k: (0, k)) if tb else pl.BlockSpec((tk, N), lambda i, k: (k, 0))
    return pl.pallas_call(
        body, name=name, grid=(M // tm, nk),
        in_specs=[pl.BlockSpec((tm, tk), lambda i, k: (i, k)), b_spec] + [row(r.shape[1]) for r in rows_in]
        + [vec(v.shape[1]) for v in vecs_in],
        out_specs=[row(w) for w, _ in rows_out] + [vec(w) for w in vecs_out],
        out_shape=[jax.ShapeDtypeStruct((M, w), dt) for w, dt in rows_out] + [jax.ShapeDtypeStruct((1, w), F32) for w in vecs_out],
        scratch_shapes=[pltpu.VMEM((tm, N), F32)],
        compiler_params=_params("arbitrary", "arbitrary", vmem=VMEM_BIG))(a, b, *rows_in, *vecs_in)


def _mm_in_dx_call(dproj, w_in_pt, x, dx2, g_pre, exchange):
    L, K = dproj.shape
    N = w_in_pt.shape[1]
    tm, tk = _fit(L, 512), _fit(K, 1664)
    nm, nk = L // tm, K // tk
    n = len(exchange)

    def body(a_ref, b_ref, x_ref, dx2_ref, g_ref, *refs):
        parts, (gx_ref, dg_ref), got = refs[:n], refs[n:n + 2], refs[n + 2:2 * n + 2]
        acc_ref = refs[2 * n + 2]
        i, k = pl.program_id(0), pl.program_id(1)
        start, finish = _exchange_phases(parts, got, *refs[2 * n + 3:])
        pl.when(jnp.logical_and(i == 0, k == 0))(start)
        part = _dot(a_ref[...], b_ref[...], NN)

        @pl.when(k == 0)
        def _():
            acc_ref[...] = part

        @pl.when(jnp.logical_and(k > 0, k < nk - 1))
        def _():
            acc_ref[...] += part

        @pl.when(k == nk - 1)
        def _():
            d1, dg = _rms_bwd(x_ref[...], g_ref[...], acc_ref[...] + part)
            gx_ref[...] = dx2_ref[...] + d1
            _acc(dg_ref, i == 0, dg)

        pl.when(jnp.logical_and(i == nm - 1, k == nk - 1))(finish)

    assert nk >= 2
    rows = lambda: pl.BlockSpec((tm, N), lambda i, k: (i, 0))
    return pl.pallas_call(
        body, name="mm_in_dx", grid=(nm, nk),
        in_specs=[pl.BlockSpec((tm, tk), lambda i, k: (i, k)), pl.BlockSpec((tk, N), lambda i, k: (k, 0)),
                  rows(), rows(), pl.BlockSpec((1, N), lambda i, k: (0, 0))] + [ANY_SPEC] * n,
        out_specs=[rows(), pl.BlockSpec((1, N), lambda i, k: (0, 0))] + [ANY_SPEC] * n,
        out_shape=[jax.ShapeDtypeStruct((L, N), F32), jax.ShapeDtypeStruct((1, N), F32)]
        + [jax.ShapeDtypeStruct(p.shape, p.dtype) for p in exchange],
        scratch_shapes=[pltpu.VMEM((tm, N), F32)] + _comm_sems(n),
        compiler_params=_params("arbitrary", "arbitrary", vmem=VMEM_BIG))(dproj, w_in_pt, x, dx2, g_pre, *exchange)


TN_CHUNK = 512


def _mm_tn(a, b, name, tm=512, tk=1024, exchange=()):
    K, M = a.shape
    K2, N = b.shape
    assert K == K2, (a.shape, b.shape)
    tm, tk, cn = _fit(M, tm), _fit(K, tk), _fit(N, TN_CHUNK)
    nm, nk = M // tm, K // tk
    n = len(exchange)

    def body(a_ref, b_ref, *refs):
        o_ref, acc_ref = refs[n], refs[2 * n + 1]
        i, k = pl.program_id(0), pl.program_id(1)
        if n:
            start, finish = _exchange_phases(refs[:n], refs[n + 1:2 * n + 1], *refs[2 * n + 2:])
            pl.when(jnp.logical_and(i == 0, k == 0))(start)

        @pl.when(k == 0)
        def _():
            acc_ref[...] = jnp.zeros((tm, N), F32)

        at = a_ref[...].astype(BF16).T
        for c in range(N // cn):
            cols = slice(c * cn, (c + 1) * cn)
            acc_ref[:, cols] += _dot(at, b_ref[:, cols].astype(BF16), NN)

        @pl.when(k == nk - 1)
        def _():
            o_ref[...] = acc_ref[...].astype(BF16)

        if n:
            pl.when(jnp.logical_and(i == nm - 1, k == nk - 1))(finish)

    out = pl.pallas_call(
        body, name=name, grid=(nm, nk),
        in_specs=[pl.BlockSpec((tk, tm), lambda i, k: (k, i)), pl.BlockSpec((tk, N), lambda i, k: (k, 0))] + [ANY_SPEC] * n,
        out_specs=[pl.BlockSpec((tm, N), lambda i, k: (i, 0))] + [ANY_SPEC] * n,
        out_shape=[jax.ShapeDtypeStruct((M, N), BF16)] + [jax.ShapeDtypeStruct(p.shape, p.dtype) for p in exchange],
        scratch_shapes=[pltpu.VMEM((tm, N), F32)] + (_comm_sems(n) if n else []),
        compiler_params=_params("arbitrary" if n else "parallel", "arbitrary", vmem=VMEM_BIG))(a, b, *exchange)
    return out if n else out[0]


def _row(tl, n, col=0):
    return pl.BlockSpec((tl, n), lambda i: (i, col))


def _full(shape):
    return pl.BlockSpec(shape, lambda i: (0,) * len(shape))


def _rms(x, g):
    r = lax.rsqrt(jnp.mean(x * x, axis=-1, keepdims=True) + EPS)
    return x * r * g


def _rms_bwd(x, g, dy):
    n = x.shape[-1]
    r = lax.rsqrt(jnp.mean(x * x, axis=-1, keepdims=True) + EPS)
    gy = dy * g
    dx = r * gy - x * (r * r * r * (1.0 / n)) * jnp.sum(x * gy, axis=-1, keepdims=True)
    return dx, jnp.sum(dy * x * r, axis=0, keepdims=True)


def _acc(ref, first, val):
    @pl.when(first)
    def _():
        ref[...] = val

    @pl.when(jnp.logical_not(first))
    def _():
        ref[...] += val


def _rms_fwd_call(x, g, name, gather):
    L, n = x.shape
    tl = _fit(L, 512)
    steps, na = L // tl, len(gather)

    def body(x_ref, g_ref, *refs):
        o_ref = refs[na]
        start, forward, finish = _gather_phases(refs[:na], refs[na + 1:2 * na + 1], *refs[2 * na + 1:])
        i = pl.program_id(0)
        pl.when(i == 0)(start)
        pl.when(i == steps // 2)(forward)
        o_ref[...] = _rms(x_ref[...], g_ref[...]).astype(BF16)
        pl.when(i == steps - 1)(finish)

    return pl.pallas_call(
        body, name=name, grid=(steps,), in_specs=[_row(tl, n), _full((1, n))] + [ANY_SPEC] * na,
        out_specs=[_row(tl, n)] + [ANY_SPEC] * na,
        out_shape=[jax.ShapeDtypeStruct((L, n), BF16)] + [jax.ShapeDtypeStruct((N_DEV,) + b.shape, b.dtype) for b in gather],
        scratch_shapes=_comm_sems(na), compiler_params=_params("arbitrary"))(x, g, *gather)


def _rope_lanes(shape):
    lane = lax.broadcasted_iota(jnp.int32, shape, 1)
    return lane, jnp.logical_and(lane >= QK_NOPE, lane < QK_HEAD)


def _rope_apply(x, cosf, sinf, lane):
    rot = jnp.where(lane < QK_NOPE + QK_ROPE // 2, -pltpu.roll(x, LANES - QK_ROPE // 2, 1), pltpu.roll(x, QK_ROPE // 2, 1))
    return x * cosf + rot * sinf


def _rope_apply_t(dy, cosf, sinf, lane, is_rope):
    g = dy * sinf
    rot_t = jnp.where(lane < QK_NOPE + QK_ROPE // 2, pltpu.roll(g, LANES - QK_ROPE // 2, 1), -pltpu.roll(g, QK_ROPE // 2, 1))
    return dy * cosf + jnp.where(is_rope, rot_t, 0.0)


def _mla_proj_call(proj, q_norm, kv_norm, w_uq_pt, w_kv_pt, pos_col, inv_freq):
    L = proj.shape[0]
    tl = _fit(L, 512)

    def body(p_ref, gq_ref, gk_ref, wq_ref, wkv_ref, pos_ref, f_ref, qn_ref, kn_ref, qo_ref, kvo_ref, cos_ref, sin_ref):
        qn = _rms(p_ref[:, P_CQ:P_CKV], gq_ref[...]).astype(BF16)
        kn = _rms(p_ref[:, P_CKV:P_KR], gk_ref[...]).astype(BF16)
        qn_ref[...] = qn
        kn_ref[...] = kn
        q_pad = _dot(qn, wq_ref[...], NT)
        kv_pad = _dot(kn, wkv_ref[...], NT)
        lane, is_rope = _rope_lanes((tl, LANES))
        ang = pos_ref[...] * f_ref[...]
        cosf = jnp.where(is_rope, jnp.cos(ang), jnp.where(lane < QK_NOPE, 1.0, 0.0))
        sinf = jnp.where(is_rope, jnp.sin(ang), 0.0)
        cos_ref[...] = cosf
        sin_ref[...] = sinf
        kr = _rope_apply(p_ref[:, P_KR:P_U], cosf, sinf, lane)
        for h in range(N_HEADS):
            qh = _rope_apply(q_pad[:, h * LANES:(h + 1) * LANES], cosf, sinf, lane)
            qo_ref[:, h * LANES:(h + 1) * LANES] = (qh * Q_PRESCALE).astype(BF16)
            kvo_ref[:, 2 * h * LANES:(2 * h + 1) * LANES] = (kv_pad[:, 2 * h * LANES:(2 * h + 1) * LANES] + kr).astype(BF16)
            vh = jnp.where(lane == V_HEAD, 1.0, kv_pad[:, (2 * h + 1) * LANES:(2 * h + 2) * LANES])
            kvo_ref[:, (2 * h + 1) * LANES:(2 * h + 2) * LANES] = vh.astype(BF16)

    shape = lambda n, dt: jax.ShapeDtypeStruct((L, n), dt)
    return pl.pallas_call(
        body, name="mla_proj", grid=(L // tl,),
        in_specs=[_row(tl, P_U), _full((1, Q_RANK)), _full((1, KV_RANK)), _full((HEAD_PAD, Q_RANK)),
                  _full((2 * HEAD_PAD, KV_RANK)), _row(tl, 1), _full((1, LANES))],
        out_specs=[_row(tl, Q_RANK), _row(tl, KV_RANK), _row(tl, HEAD_PAD), _row(tl, 2 * HEAD_PAD), _row(tl, LANES), _row(tl, LANES)],
        out_shape=[shape(Q_RANK, BF16), shape(KV_RANK, BF16), shape(HEAD_PAD, BF16), shape(2 * HEAD_PAD, BF16),
                   shape(LANES, F32), shape(LANES, F32)],
        compiler_params=_params("parallel"))(proj, q_norm, kv_norm, w_uq_pt, w_kv_pt, pos_col, inv_freq)


def _mla_proj_bwd_call(dq, dkv, cosf, sinf, proj, q_norm, kv_norm, w_uq_pt, w_kv_pt):
    L = dq.shape[0]
    tl = _fit(L, 512)

    def body(dq_ref, dkv_ref, cos_ref, sin_ref, p_ref, gq_ref, gk_ref, wq_ref, wkv_ref,
             dqo_ref, dkvo_ref, d_ref, dgq_ref, dgk_ref):
        first = pl.program_id(0) == 0
        lane, is_rope = _rope_lanes((tl, LANES))
        cosf, sinf = cos_ref[...], sin_ref[...]
        dk_sum = jnp.zeros((tl, LANES), F32)
        for h in range(N_HEADS):
            dqo_ref[:, h * LANES:(h + 1) * LANES] = _rope_apply_t(dq_ref[:, h * LANES:(h + 1) * LANES], cosf, sinf, lane, is_rope).astype(BF16)
            dk_sum = dk_sum + dkv_ref[:, 2 * h * LANES:(2 * h + 1) * LANES]
        dkvo_ref[...] = dkv_ref[...].astype(BF16)
        dqn = _dot(dqo_ref[...], wq_ref[...], NN)
        dkn = _dot(dkvo_ref[...], wkv_ref[...], NN)
        dcq, dgq = _rms_bwd(p_ref[:, P_CQ:P_CKV], gq_ref[...], dqn)
        dckv, dgk = _rms_bwd(p_ref[:, P_CKV:P_KR], gk_ref[...], dkn)
        d_ref[:, P_CQ:P_CKV] = dcq.astype(BF16)
        d_ref[:, P_CKV:P_KR] = dckv.astype(BF16)
        d_ref[:, P_KR:P_U] = _rope_apply_t(dk_sum, cosf, sinf, lane, is_rope).astype(BF16)
        _acc(dgq_ref, first, dgq)
        _acc(dgk_ref, first, dgk)

    shape = lambda n: jax.ShapeDtypeStruct((L, n), BF16)
    return pl.pallas_call(
        body, name="mla_proj_bwd", grid=(L // tl,),
        in_specs=[_row(tl, HEAD_PAD), _row(tl, 2 * HEAD_PAD), _row(tl, LANES), _row(tl, LANES), _row(tl, P_KR),
                  _full((1, Q_RANK)), _full((1, KV_RANK)), _full((HEAD_PAD, Q_RANK)), _full((2 * HEAD_PAD, KV_RANK))],
        out_specs=[_row(tl, HEAD_PAD), _row(tl, 2 * HEAD_PAD), _row(tl, P_U), _full((1, Q_RANK)), _full((1, KV_RANK))],
        out_shape=[shape(HEAD_PAD), shape(2 * HEAD_PAD), shape(P_U), jax.ShapeDtypeStruct((1, Q_RANK), F32),
                   jax.ShapeDtypeStruct((1, KV_RANK), F32)],
        compiler_params=_params("arbitrary"))(dq, dkv, cosf, sinf, proj, q_norm, kv_norm, w_uq_pt, w_kv_pt)


GATE_TILE = 256
GATE_ROWS = 1024


def _merge_call(proj, b_gate, pa, ps):
    L = proj.shape[0]
    tl = _fit(L, GATE_ROWS)
    nc = D_MODEL // GATE_TILE
    g0, g1 = P_GATE // GATE_TILE, (P_GATE + D_MODEL) // GATE_TILE

    def body(l0_ref, l1_ref, b0_ref, b1_ref, pa_ref, ps_ref, o_ref):
        s0 = _sigmoid(l0_ref[...] + b0_ref[...])
        s1 = _sigmoid(l1_ref[...] + b1_ref[...])
        o_ref[...] = (s0 * pa_ref[...] + s1 * ps_ref[...]).astype(BF16)

    blk = lambda off: pl.BlockSpec((tl, GATE_TILE), lambda i, j: (i, off + j))
    bias = lambda off: pl.BlockSpec((1, GATE_TILE), lambda i, j: (0, off + j))
    return pl.pallas_call(
        body, name="merge", grid=(L // tl, nc),
        in_specs=[blk(g0), blk(g1), bias(0), bias(nc), blk(0), blk(0)],
        out_specs=blk(0), out_shape=jax.ShapeDtypeStruct((L, D_MODEL), BF16),
        compiler_params=_params("parallel", "parallel"))(proj, proj, b_gate, b_gate, pa, ps)


def _merge_bwd_call(dm, proj, b_gate, pa, ps):
    L = proj.shape[0]
    tl = _fit(L, GATE_ROWS)
    nc = D_MODEL // GATE_TILE
    g0, g1 = P_GATE // GATE_TILE, (P_GATE + D_MODEL) // GATE_TILE

    def body(dm_ref, l0_ref, l1_ref, b0_ref, b1_ref, pa_ref, ps_ref, dpa_ref, dps_ref, dl0_ref, dl1_ref, db0_ref, db1_ref):
        first = pl.program_id(1) == 0
        dm_ = dm_ref[...]
        s0 = _sigmoid(l0_ref[...] + b0_ref[...])
        s1 = _sigmoid(l1_ref[...] + b1_ref[...])
        dpa_ref[...] = (dm_ * s0).astype(BF16)
        dps_ref[...] = (dm_ * s1).astype(BF16)
        dl0 = dm_ * pa_ref[...] * s0 * (1.0 - s0)
        dl1 = dm_ * ps_ref[...] * s1 * (1.0 - s1)
        dl0_ref[...] = dl0.astype(BF16)
        dl1_ref[...] = dl1.astype(BF16)
        _acc(db0_ref, first, jnp.sum(dl0, axis=0, keepdims=True))
        _acc(db1_ref, first, jnp.sum(dl1, axis=0, keepdims=True))

    blk = lambda off: pl.BlockSpec((tl, GATE_TILE), lambda j, i: (i, off + j))
    bias = lambda off: pl.BlockSpec((1, GATE_TILE), lambda j, i: (0, off + j))
    act = jax.ShapeDtypeStruct((L, D_MODEL), BF16)
    vec = jax.ShapeDtypeStruct((1, D_MODEL), F32)
    return pl.pallas_call(
        body, name="merge_bwd", grid=(nc, L // tl),
        in_specs=[blk(0), blk(g0), blk(g1), bias(0), bias(nc), blk(0), blk(0)],
        out_specs=[blk(0), blk(0), blk(0), blk(0), bias(0), bias(0)],
        out_shape=[act, act, act, act, vec, vec],
        compiler_params=_params("parallel", "arbitrary"))(dm, proj, proj, b_gate, b_gate, pa, ps)


def _post_mix_rows(o, x, g_post, g_fpre):
    x2 = x + _rms(o, g_post)
    return o, x2, _rms(x2, g_fpre)


def _ffn_out_rows(ff, x2, target, g_fpost):
    n = ff.shape[-1]
    err = x2 + _rms(ff, g_fpost) - target
    part = 0.5 * jnp.sum(jnp.sum(err * err, axis=-1, keepdims=True) * (1.0 / n), axis=0, keepdims=True)
    dy = err * (1.0 / n)
    dff, dg = _rms_bwd(ff, g_fpost, dy)
    return dy, dff, jnp.broadcast_to(part, (1, LANES)), dg


def _post_bwd_rows(dhn2, x2, dy, o, g_fpre, g_post):
    d1, dgf = _rms_bwd(x2, g_fpre, dhn2)
    dx2 = dy + d1
    do, dgp = _rms_bwd(o, g_post, dx2)
    return dx2, do, dgf, dgp


CONV_TILE = 256
CONV_WIDE = 1408
HALO = 16


def _conv3(w, b, x0, x1, x2):
    return b + w[2:3] * x0 + w[1:2] * x1 + w[0:1] * x2


def _down(x, by):
    return pltpu.roll(x, by, 0)


def _edge_down(edge, before, by):
    r = lax.broadcasted_iota(jnp.int32, edge.shape, 0)
    return jnp.where(r < by, pltpu.roll(before, by, 0), pltpu.roll(edge, by, 0))


def _edge_up(edge, after, by):
    r = lax.broadcasted_iota(jnp.int32, edge.shape, 0)
    return jnp.where(r >= HALO - by, pltpu.roll(after, HALO - by, 0), pltpu.roll(edge, HALO - by, 0))


def _gated(w_g, b_g, w_v, b_v, hg, hv, g1, g2, v1, v2):
    return _conv3(w_g, b_g, hg, g1, g2), _conv3(w_v, b_v, hv, v1, v2)


def _conv_specs(tl, tc, rows_inner):
    nh = tl // HALO
    if rows_inner:
        ij = lambda f: (lambda j, i: f(i, j))
    else:
        ij = lambda f: f
    cur = lambda off: pl.BlockSpec((tl, tc), ij(lambda i, j: (i, off + j)))
    prev = lambda off: pl.BlockSpec((HALO, tc), ij(lambda i, j: (jnp.maximum(i * nh - 1, 0), off + j)))
    par = lambda rows, off: pl.BlockSpec((rows, tc), ij(lambda i, j: (0, off + j)))
    return cur, prev, par


def _conv_act_call(h, conv_w, conv_b):
    L = h.shape[0]
    tl = _fit(L, 256)
    nc = D_FF // CONV_WIDE
    cur, prev, par = _conv_specs(tl, CONV_WIDE, False)

    def body(hg_ref, hv_ref, pg_ref, pv_ref, wg_ref, wv_ref, bg_ref, bv_ref, a_ref):
        not_first = (pl.program_id(0) > 0).astype(F32)
        par = (wg_ref[...], bg_ref[...], wv_ref[...], bv_ref[...])
        hg, hv = hg_ref[...], hv_ref[...]
        gate, val = _gated(*par, hg, hv, _down(hg, 1), _down(hg, 2), _down(hv, 1), _down(hv, 2))
        a_ref[...] = (_gelu(gate) * val).astype(BF16)
        eg, ev, bg, bv = hg[:HALO], hv[:HALO], pg_ref[...] * not_first, pv_ref[...] * not_first
        gate, val = _gated(*par, eg, ev, _edge_down(eg, bg, 1), _edge_down(eg, bg, 2),
                           _edge_down(ev, bv, 1), _edge_down(ev, bv, 2))
        a_ref[:HALO, :] = (_gelu(gate) * val).astype(BF16)

    return pl.pallas_call(
        body, name="conv_act", grid=(L // tl, nc),
        in_specs=[cur(0), cur(nc), prev(0), prev(nc), par(3, 0), par(3, nc), par(1, 0), par(1, nc)],
        out_specs=cur(0), out_shape=jax.ShapeDtypeStruct((L, D_FF), BF16),
        compiler_params=_params("parallel", "parallel"))(h, h, h, h, conv_w, conv_w, conv_b, conv_b)


def _conv_act_bwd_call(da, h, conv_w, conv_b):
    L = h.shape[0]
    tl = _fit(L, 512)
    nc = D_FF // CONV_TILE
    cur, prev, par = _conv_specs(tl, CONV_TILE, True)

    def body(da_ref, hg_ref, hv_ref, pg_ref, pv_ref, wg_ref, wv_ref, bg_ref, bv_ref,
             dg_ref, dv_ref, dwg_ref, dwv_ref, dbg_ref, dbv_ref):
        first = pl.program_id(1) == 0
        not_first = (pl.program_id(1) > 0).astype(F32)
        par = (wg_ref[...], bg_ref[...], wv_ref[...], bv_ref[...])
        col = lambda t: jnp.sum(t, axis=0, keepdims=True)

        def grads(da_, hg, hv, g1, g2, v1, v2):
            gate, val = _gated(*par, hg, hv, g1, g2, v1, v2)
            act, slope = _gelu_and_grad(gate)
            dgate = da_ * val * slope
            dval = da_ * act
            sums = (jnp.concatenate([col(dgate * g2), col(dgate * g1), col(dgate * hg)], axis=0),
                    jnp.concatenate([col(dval * v2), col(dval * v1), col(dval * hv)], axis=0), col(dgate), col(dval))
            return dgate, dval, sums

        da_, hg, hv = da_ref[...], hg_ref[...], hv_ref[...]
        shifted = (_down(hg, 1), _down(hg, 2), _down(hv, 1), _down(hv, 2))
        dgate, dval, whole = grads(da_, hg, hv, *shifted)
        dg_ref[...] = dgate.astype(BF16)
        dv_ref[...] = dval.astype(BF16)
        edge = lambda t: t[:HALO]
        _, _, wrapped = grads(edge(da_), edge(hg), edge(hv), *[edge(s) for s in shifted])
        eg, ev, bg, bv = edge(hg), edge(hv), pg_ref[...] * not_first, pv_ref[...] * not_first
        dgate, dval, fixed = grads(edge(da_), eg, ev, _edge_down(eg, bg, 1), _edge_down(eg, bg, 2),
                                   _edge_down(ev, bv, 1), _edge_down(ev, bv, 2))
        dg_ref[:HALO, :] = dgate.astype(BF16)
        dv_ref[:HALO, :] = dval.astype(BF16)
        for ref, a, b, c in zip((dwg_ref, dwv_ref, dbg_ref, dbv_ref), whole, wrapped, fixed):
            _acc(ref, first, a - b + c)

    act = jax.ShapeDtypeStruct((L, D_FF), BF16)
    w3 = jax.ShapeDtypeStruct((3, D_FF), F32)
    w1 = jax.ShapeDtypeStruct((1, D_FF), F32)
    return pl.pallas_call(
        body, name="conv_act_bwd", grid=(nc, L // tl),
        in_specs=[cur(0), cur(0), cur(nc), prev(0), prev(nc), par(3, 0), par(3, nc), par(1, 0), par(1, nc)],
        out_specs=[cur(0), cur(0), par(3, 0), par(3, 0), par(1, 0), par(1, 0)],
        out_shape=[act, act, w3, w3, w1, w1],
        compiler_params=_params("parallel", "arbitrary"))(da, h, h, h, h, conv_w, conv_w, conv_b, conv_b)


def _conv_t_call(dgate, dval, conv_w):
    L = dgate.shape[0]
    tl = _fit(L, 512)
    nc = D_FF // CONV_WIDE
    nh = tl // HALO

    def body(dg_ref, dv_ref, ng_ref, nv_ref, w_ref, o_ref):
        not_last = (pl.program_id(0) < L // tl - 1).astype(F32)

        def emit(d_ref, n_ref):
            c = d_ref[...].astype(F32)
            w = w_ref[...]
            o_ref[...] = _conv3(w, 0.0, c, pltpu.roll(c, tl - 1, 0), pltpu.roll(c, tl - 2, 0)).astype(BF16)
            edge, after = c[tl - HALO:], n_ref[...].astype(F32) * not_last
            o_ref[tl - HALO:, :] = _conv3(w, 0.0, edge, _edge_up(edge, after, 1), _edge_up(edge, after, 2)).astype(BF16)

        pl.when(pl.program_id(1) < nc)(lambda: emit(dg_ref, ng_ref))
        pl.when(pl.program_id(1) >= nc)(lambda: emit(dv_ref, nv_ref))

    gate_col = lambda j: jnp.minimum(j, nc - 1)
    val_col = lambda j: jnp.maximum(j - nc, 0)
    after_row = lambda i: jnp.minimum((i + 1) * nh, L // HALO - 1)
    tile = lambda col: pl.BlockSpec((tl, CONV_WIDE), lambda i, j: (i, col(j)))
    after = lambda col: pl.BlockSpec((HALO, CONV_WIDE), lambda i, j: (after_row(i), col(j)))
    return pl.pallas_call(
        body, name="conv_t", grid=(L // tl, 2 * nc),
        in_specs=[tile(gate_col), tile(val_col), after(gate_col), after(val_col), pl.BlockSpec((3, CONV_WIDE), lambda i, j: (0, j))],
        out_specs=pl.BlockSpec((tl, CONV_WIDE), lambda i, j: (i, j)),
        out_shape=jax.ShapeDtypeStruct((L, 2 * D_FF), BF16),
        compiler_params=_params("parallel", "parallel"))(dgate, dval, dgate, dval, conv_w)


def _glu_call(y1, w_glu, b_glu):
    L, n = y1.shape
    tl = _fit(L, 512)

    def body(y_ref, w_ref, b_ref, o_ref):
        y2 = _gelu(y_ref[...])
        z = _dot(y2.astype(BF16), w_ref[...], NN) + b_ref[...]
        o_ref[...] = (y2 * _sigmoid(z)).astype(BF16)

    return pl.pallas_call(
        body, name="glu", grid=(L // tl,), in_specs=[_row(tl, n), _full((n, n)), _full((1, n))],
        out_specs=_row(tl, n), out_shape=jax.ShapeDtypeStruct((L, n), BF16),
        compiler_params=_params("parallel"))(y1, w_glu, b_glu)


def _glu_bwd_call(dout, y1, w_glu, b_glu):
    L, n = y1.shape
    tl = _fit(L, 512)

    def body(do_ref, y_ref, w_ref, b_ref, dy_ref, dw_ref, db_ref):
        first = pl.program_id(0) == 0
        y1_ = y_ref[...]
        y2, slope = _gelu_and_grad(y1_)
        y2b = y2.astype(BF16)
        w = w_ref[...]
        sg = _sigmoid(_dot(y2b, w, NN) + b_ref[...])
        dout_ = do_ref[...].astype(F32)
        dz = dout_ * y2 * sg * (1.0 - sg)
        dzb = dz.astype(BF16)
        dy2 = dout_ * sg + _dot(dzb, w, NT)
        dy_ref[...] = dy2 * slope
        _acc(dw_ref, first, _dot(y2b, dzb, TN))
        _acc(db_ref, first, jnp.sum(dz, axis=0, keepdims=True))

    return pl.pallas_call(
        body, name="glu_bwd", grid=(L // tl,),
        in_specs=[_row(tl, n), _row(tl, n), _full((n, n)), _full((1, n))],
        out_specs=[_row(tl, n), _full((n, n)), _full((1, n))],
        out_shape=[jax.ShapeDtypeStruct((L, n), F32), jax.ShapeDtypeStruct((n, n), F32), jax.ShapeDtypeStruct((1, n), F32)],
        compiler_params=_params("arbitrary"))(dout, y1, w_glu, b_glu)


ATTN_TILE = 1024
ATTN_SCALE = 1.0 / math.sqrt(QK_HEAD)


ATTN_HEADS = 2
ATTN_GROUPS = N_HEADS // ATTN_HEADS
LOG2E = 1.0 / math.log(2.0)
Q_PRESCALE = ATTN_SCALE * LOG2E
ANY_SPEC = pl.BlockSpec(memory_space=pl.ANY)


def _attn_fwd_call(q, kv, blocks):
    L = q.shape[0]
    t = _fit(L, ATTN_TILE)
    nq = L // t
    n = len(blocks)

    def body(q_ref, kv_ref, *refs):
        blk_refs, (o_ref, lse_ref), gat_refs = refs[:n], refs[n:n + 2], refs[n + 2:2 * n + 2]
        m_s, acc_s, send_sems, recv_sems, local_sems = refs[2 * n + 2:]
        g, i = pl.program_id(0), pl.program_id(1)
        start, forward, finish = _gather_phases(blk_refs, gat_refs, send_sems, recv_sems, local_sems)
        pl.when(jnp.logical_and(g == 0, i == 0))(start)
        m_s[...] = jnp.full((ATTN_HEADS, t, 1), NEG, F32)
        acc_s[...] = jnp.zeros((ATTN_HEADS, t, LANES), F32)
        below = lax.broadcasted_iota(jnp.int32, (t, t), 1) <= lax.broadcasted_iota(jnp.int32, (t, t), 0)

        def block_step(kb, on_diagonal):
            rows = pl.ds(pl.multiple_of(kb * t, t), t)
            for a in range(ATTN_HEADS):
                s = _dot(q_ref[:, a * LANES:(a + 1) * LANES], kv_ref[rows, 2 * a * LANES:(2 * a + 1) * LANES], NT)
                if on_diagonal:
                    s = jnp.where(below, s, NEG)
                m_prev = m_s[a]
                m_new = jnp.maximum(m_prev, jnp.max(s, axis=1, keepdims=True))
                p = jnp.exp2(s - m_new)
                pv = _dot(p.astype(BF16), kv_ref[rows, (2 * a + 1) * LANES:(2 * a + 2) * LANES], NN)
                acc_s[a] = jnp.exp2(m_prev - m_new) * acc_s[a] + pv
                m_s[a] = m_new

        def step(kb, carry):
            block_step(kb, False)
            return carry

        lax.fori_loop(0, i, step, 0)
        block_step(i, True)
        lane = lax.broadcasted_iota(jnp.int32, (t, LANES), 1)
        for a in range(ATTN_HEADS):
            acc = acc_s[a]
            l = jnp.sum(jnp.where(lane == V_HEAD, acc, 0.0), axis=1, keepdims=True)
            o_ref[:, a * LANES:(a + 1) * LANES] = (acc / l).astype(BF16)
            lse_ref[a] = m_s[a] + jnp.log(l) * LOG2E
        pl.when(jnp.logical_and(g == (3 * ATTN_GROUPS) // 4, i == 0))(forward)
        pl.when(jnp.logical_and(g == ATTN_GROUPS - 1, i == nq - 1))(finish)

    gw = ATTN_HEADS * LANES
    return pl.pallas_call(
        body, name="attn_fwd", grid=(ATTN_GROUPS, nq),
        in_specs=[pl.BlockSpec((t, gw), lambda g, i: (i, g)),
                  pl.BlockSpec((L, 2 * gw), lambda g, i: (0, g))] + [ANY_SPEC] * n,
        out_specs=[pl.BlockSpec((t, gw), lambda g, i: (i, g)),
                   pl.BlockSpec((ATTN_HEADS, t, 1), lambda g, i: (g, i, 0))] + [ANY_SPEC] * n,
        out_shape=[jax.ShapeDtypeStruct((L, HEAD_PAD), BF16), jax.ShapeDtypeStruct((N_HEADS, L, 1), F32)]
        + [jax.ShapeDtypeStruct((N_DEV,) + b.shape, b.dtype) for b in blocks],
        scratch_shapes=[pltpu.VMEM((ATTN_HEADS, t, 1), F32), pltpu.VMEM((ATTN_HEADS, t, LANES), F32)] + _comm_sems(n),
        compiler_params=_params("arbitrary", "arbitrary", vmem=VMEM_BIG))(q, kv, *blocks)


def _attn_bwd_call(q, kv, o, do, lse, parts, blocks):
    L = q.shape[0]
    t = _fit(L, ATTN_TILE)
    nq = L // t
    n1, n = len(parts), len(parts) + len(blocks)

    def body(q_ref, do_ref, o_ref, lse_ref, kv_ref, *refs):
        in_refs, (dq_ref, dkv_ref), out_refs = refs[:n], refs[n:n + 2], refs[n + 2:2 * n + 2]
        dk_s, dv_s = refs[2 * n + 2:2 * n + 4]
        g, j = pl.program_id(0), pl.program_id(1)
        start, finish = _exchange_phases(in_refs[:n1], out_refs[:n1], *refs[2 * n + 4:2 * n + 7])
        start_blocks, finish_blocks = _exchange_phases(in_refs[n1:], out_refs[n1:], *refs[2 * n + 7:], same_source=True)

        @pl.when(jnp.logical_and(g == 0, j == 0))
        def _():
            start()
            start_blocks()

        @pl.when(j == 0)
        def _():
            dq_ref[...] = jnp.zeros((L, ATTN_HEADS * LANES), F32)

        dk_s[...] = jnp.zeros((ATTN_HEADS, t, LANES), F32)
        dv_s[...] = jnp.zeros((ATTN_HEADS, t, LANES), F32)
        below = lax.broadcasted_iota(jnp.int32, (t, t), 1) <= lax.broadcasted_iota(jnp.int32, (t, t), 0)

        def block_step(i, on_diagonal):
            rows = pl.ds(pl.multiple_of(i * t, t), t)
            for a in range(ATTN_HEADS):
                lanes = slice(a * LANES, (a + 1) * LANES)
                qi = q_ref[rows, lanes]
                doi = do_ref[rows, lanes]
                kblk = kv_ref[:, 2 * a * LANES:(2 * a + 1) * LANES]
                delta = jnp.sum(doi.astype(F32) * o_ref[rows, lanes].astype(F32), axis=1, keepdims=True)
                s = _dot(qi, kblk, NT)
                if on_diagonal:
                    s = jnp.where(below, s, NEG)
                p = jnp.exp2(s - lse_ref[a, rows, :])
                dv_s[a] += _dot(p.astype(BF16), doi, TN)
                ds = (p * (_dot(doi, kv_ref[:, (2 * a + 1) * LANES:(2 * a + 2) * LANES], NT) - delta)).astype(BF16)
                dk_s[a] += _dot(ds, qi, TN)
                dq_ref[rows, lanes] += _dot(ds, kblk, NN) * ATTN_SCALE

        def step(i, carry):
            block_step(i, False)
            return carry

        block_step(j, True)
        lax.fori_loop(j + 1, nq, step, 0)
        for a in range(ATTN_HEADS):
            dkv_ref[:, 2 * a * LANES:(2 * a + 1) * LANES] = dk_s[a] * (1.0 / LOG2E)
            dkv_ref[:, (2 * a + 1) * LANES:(2 * a + 2) * LANES] = dv_s[a]
        @pl.when(jnp.logical_and(g == ATTN_GROUPS - 1, j == nq - 1))
        def _():
            finish()
            finish_blocks()

    gw = ATTN_HEADS * LANES
    whole = lambda: pl.BlockSpec((L, gw), lambda g, j: (0, g))
    acc = pltpu.VMEM((ATTN_HEADS, t, LANES), F32)
    return pl.pallas_call(
        body, name="attn_bwd", grid=(ATTN_GROUPS, nq),
        in_specs=[whole(), whole(), whole(), pl.BlockSpec((ATTN_HEADS, L, 1), lambda g, j: (g, 0, 0)),
                  pl.BlockSpec((t, 2 * gw), lambda g, j: (j, g))] + [ANY_SPEC] * n,
        out_specs=[whole(), pl.BlockSpec((t, 2 * gw), lambda g, j: (j, g))] + [ANY_SPEC] * n,
        out_shape=[jax.ShapeDtypeStruct((L, HEAD_PAD), F32), jax.ShapeDtypeStruct((L, 2 * HEAD_PAD), F32)]
        + [jax.ShapeDtypeStruct(p.shape, p.dtype) for p in parts]
        + [jax.ShapeDtypeStruct((N_DEV,) + b.shape, b.dtype) for b in blocks],
        scratch_shapes=[acc, acc] + _comm_sems(n1) + _comm_sems(n - n1),
        compiler_params=_params("arbitrary", "arbitrary", vmem=VMEM_BIG))(q, do, o, lse, kv, *parts, *blocks)


def _disc(lr, li, ldt, br, bi):
    dt = jnp.exp(ldt)
    mag = jnp.exp(lr * dt)
    ang = li * dt
    a_re, a_im = mag * jnp.cos(ang), mag * jnp.sin(ang)
    den = lr * lr + li * li
    n_re, n_im = a_re - 1.0, a_im
    z_re = (n_re * lr + n_im * li) / den
    z_im = (n_im * lr - n_re * li) / den
    return a_re, a_im, z_re * br - z_im * bi, z_re * bi + z_im * br


def _disc_call(lr, li, ldt, br, bi):
    def body(lr_ref, li_ref, ldt_ref, br_ref, bi_ref, ar_ref, ai_ref, bbr_ref, bbi_ref):
        ar_ref[...], ai_ref[...], bbr_ref[...], bbi_ref[...] = _disc(
            lr_ref[...], li_ref[...], ldt_ref[...], br_ref[...], bi_ref[...])

    c1 = jax.ShapeDtypeStruct((SSM_NSTATE, 1), F32)
    c16 = jax.ShapeDtypeStruct((SSM_NSTATE, SSM_GROUP), F32)
    return pl.pallas_call(body, name="ssm_disc", out_shape=[c1, c1, c16, c16])(lr, li, ldt, br, bi)


def _disc_bwd_call(lr, li, ldt, br, bi, dar, dai, dbbr, dbbi):
    def body(lr_ref, li_ref, ldt_ref, br_ref, bi_ref, dar_ref, dai_ref, dbbr_ref, dbbi_ref,
             dlr_ref, dli_ref, dldt_ref, dbr_ref, dbi_ref):
        _, vjp = jax.vjp(_disc, lr_ref[...], li_ref[...], ldt_ref[...], br_ref[...], bi_ref[...])
        dlr_ref[...], dli_ref[...], dldt_ref[...], dbr_ref[...], dbi_ref[...] = vjp(
            (dar_ref[...], dai_ref[...], dbbr_ref[...], dbbi_ref[...]))

    c1 = jax.ShapeDtypeStruct((SSM_NSTATE, 1), F32)
    c16 = jax.ShapeDtypeStruct((SSM_NSTATE, SSM_GROUP), F32)
    return pl.pallas_call(body, name="ssm_disc_bwd", out_shape=[c1, c1, c1, c16, c16])(
        lr, li, ldt, br, bi, dar, dai, dbbr, dbbi)


SSM_ROWS = 512
SSM_CW = SSM_NSTATE // SSM_CHUNKS
SSM_CU = SSM_WIDTH // SSM_CHUNKS


def _cmul(ar, ai, br, bi):
    return ar * br - ai * bi, ar * bi + ai * br


def _power(ar1, ai1, n):
    res, base = None, (ar1, ai1)
    while n:
        if n & 1:
            res = base if res is None else _cmul(res[0], res[1], base[0], base[1])
        n >>= 1
        if n:
            base = _cmul(base[0], base[1], base[0], base[1])
    return res


def _tile(k):
    return pl.ds(pl.multiple_of(k * 8, 8), 8)


def _ssm_fwd_call(u, a_re, a_im, bb_re, bb_im, cm_re, cm_im, d_skip):
    L = u.shape[0]
    seg = L // 8
    rb = _fit(L, SSM_ROWS)

    def body(u_ref, ar_ref, ai_ref, bbr_ref, bbi_ref, cmr_ref, cmi_ref, d_ref, y_ref, sre_hbm, sim_hbm,
             s_re, s_im, sems):
        q = pl.program_id(0)

        def bu_step(r, c):
            rows = pl.ds(pl.multiple_of(r * rb, rb), rb)
            ub = u_ref[rows, :].astype(BF16)
            s_re[rows, :] = _dot(ub, bbr_ref[0], NN)
            s_im[rows, :] = _dot(ub, bbi_ref[0], NN)
            return c

        lax.fori_loop(0, L // rb, bu_step, 0)
        ar1, ai1 = ar_ref[...], ai_ref[...]
        ar = jnp.broadcast_to(ar1, (8, SSM_CW))
        ai = jnp.broadcast_to(ai1, (8, SSM_CW))

        def local(k, c):
            nr, ni = _cmul(ar, ai, c[0], c[1])
            nr = nr + s_re[_tile(k), :]
            ni = ni + s_im[_tile(k), :]
            s_re[_tile(k), :] = nr
            s_im[_tile(k), :] = ni
            return nr, ni

        zero8 = jnp.zeros((8, SSM_CW), F32)
        lax.fori_loop(0, seg, local, (zero8, zero8))
        pr, pi = _power(ar1, ai1, seg)
        end_r = s_re[pl.ds((seg - 1) * 8, 8), :]
        end_i = s_im[pl.ds((seg - 1) * 8, 8), :]
        er = jnp.zeros((1, SSM_CW), F32)
        ei = jnp.zeros((1, SSM_CW), F32)
        rows_r, rows_i = [er], [ei]
        for j in range(7):
            tr, ti = _cmul(pr, pi, er, ei)
            er, ei = end_r[j:j + 1] + tr, end_i[j:j + 1] + ti
            rows_r.append(er)
            rows_i.append(ei)
        e_r = jnp.concatenate(rows_r, axis=0)
        e_i = jnp.concatenate(rows_i, axis=0)

        def fix(k, c):
            wr, wi = _cmul(c[0], c[1], ar, ai)
            fr, fi = _cmul(wr, wi, e_r, e_i)
            s_re[_tile(k), :] += fr
            s_im[_tile(k), :] += fi
            return wr, wi

        lax.fori_loop(0, seg, fix, (jnp.ones((8, SSM_CW), F32), zero8))
        out_r = pltpu.make_async_copy(s_re, sre_hbm.at[q], sems.at[0])
        out_i = pltpu.make_async_copy(s_im, sim_hbm.at[q], sems.at[1])
        out_r.start()
        out_i.start()

        def y_step(r, c):
            rows = pl.ds(pl.multiple_of(r * rb, rb), rb)
            y = _dot(s_re[rows, :].astype(BF16), cmr_ref[0], NN) - _dot(s_im[rows, :].astype(BF16), cmi_ref[0], NN)
            y_ref[rows, :] = y + d_ref[...] * u_ref[rows, :]
            return c

        lax.fori_loop(0, L // rb, y_step, 0)
        out_r.wait()
        out_i.wait()

    chunk = lambda rows, cols: pl.BlockSpec((rows, cols), lambda q: (0, q))
    mat = lambda r, c: pl.BlockSpec((1, r, c), lambda q: (q, 0, 0))
    anyspec = pl.BlockSpec(memory_space=pl.ANY)
    states = jax.ShapeDtypeStruct((SSM_CHUNKS, L, SSM_CW), F32)
    return pl.pallas_call(
        body, name="ssm_fwd", grid=(SSM_CHUNKS,),
        in_specs=[chunk(L, SSM_CU), chunk(1, SSM_CW), chunk(1, SSM_CW), mat(SSM_CU, SSM_CW), mat(SSM_CU, SSM_CW),
                  mat(SSM_CW, SSM_CU), mat(SSM_CW, SSM_CU), chunk(1, SSM_CU)],
        out_specs=[chunk(L, SSM_CU), anyspec, anyspec],
        out_shape=[jax.ShapeDtypeStruct((L, SSM_WIDTH), F32), states, states],
        scratch_shapes=[pltpu.VMEM((L, SSM_CW), F32), pltpu.VMEM((L, SSM_CW), F32), pltpu.SemaphoreType.DMA((2,))],
        compiler_params=_params("arbitrary", vmem=VMEM_BIG))(u, a_re, a_im, bb_re, bb_im, cm_re, cm_im, d_skip)


def _ssm_bwd_call(dy, u, s_re_all, s_im_all, a_re, a_im, bb_re, bb_im, cm_re, cm_im, d_skip):
    L = u.shape[0]
    seg = L // 8
    rb = _fit(L, SSM_ROWS)

    def body(dy_ref, u_ref, sre_hbm, sim_hbm, ar_ref, ai_ref, bbr_ref, bbi_ref, cmr_ref, cmi_ref, d_ref,
             du_ref, dbbr_ref, dbbi_ref, dcmr_ref, dcmi_ref, dar_ref, dai_ref, dd_ref,
             g_re, g_im, s_re, s_im, sems):
        q = pl.program_id(0)
        in_r = pltpu.make_async_copy(sre_hbm.at[q], s_re, sems.at[0])
        in_i = pltpu.make_async_copy(sim_hbm.at[q], s_im, sems.at[1])
        in_r.start()
        in_i.start()

        def ds_step(r, c):
            rows = pl.ds(pl.multiple_of(r * rb, rb), rb)
            dyb = dy_ref[rows, :].astype(BF16)
            g_re[rows, :] = _dot(dyb, cmr_ref[0], NT)
            g_im[rows, :] = -_dot(dyb, cmi_ref[0], NT)
            return c

        lax.fori_loop(0, L // rb, ds_step, 0)
        ar1, ai1 = ar_ref[...], ai_ref[...]
        ar = jnp.broadcast_to(ar1, (8, SSM_CW))
        nai = jnp.broadcast_to(-ai1, (8, SSM_CW))

        def local(kk, c):
            k = seg - 1 - kk
            nr, ni = _cmul(ar, nai, c[0], c[1])
            nr = nr + g_re[_tile(k), :]
            ni = ni + g_im[_tile(k), :]
            g_re[_tile(k), :] = nr
            g_im[_tile(k), :] = ni
            return nr, ni

        zero8 = jnp.zeros((8, SSM_CW), F32)
        lax.fori_loop(0, seg, local, (zero8, zero8))
        pr, pi = _power(ar1, -ai1, seg)
        head_r = g_re[pl.ds(0, 8), :]
        head_i = g_im[pl.ds(0, 8), :]
        fr = jnp.zeros((1, SSM_CW), F32)
        fi = jnp.zeros((1, SSM_CW), F32)
        rows_r, rows_i = [fr], [fi]
        for j in range(6, -1, -1):
            tr, ti = _cmul(pr, pi, fr, fi)
            fr, fi = head_r[j + 1:j + 2] + tr, head_i[j + 1:j + 2] + ti
            rows_r.insert(0, fr)
            rows_i.insert(0, fi)
        f_r = jnp.concatenate(rows_r, axis=0)
        f_i = jnp.concatenate(rows_i, axis=0)
        in_r.wait()
        in_i.wait()

        def fixed(k, wr, wi):
            xr, xi = _cmul(wr, wi, f_r, f_i)
            gr = g_re[_tile(k), :] + xr
            gi = g_im[_tile(k), :] + xi
            g_re[_tile(k), :] = gr
            g_im[_tile(k), :] = gi
            return gr, gi

        def fix(kk, c):
            k = seg - 1 - kk
            wr, wi = _cmul(c[0], c[1], ar, nai)
            gr, gi = fixed(k, wr, wi)
            pr_, pi_ = s_re[_tile(k - 1), :], s_im[_tile(k - 1), :]
            return wr, wi, c[2] + gr * pr_ + gi * pi_, c[3] + gi * pr_ - gr * pi_

        wr, wi, acc_r, acc_i = lax.fori_loop(0, seg - 1, fix, (jnp.ones((8, SSM_CW), F32), zero8, zero8, zero8))
        wr, wi = _cmul(wr, wi, ar, nai)
        gr, gi = fixed(0, wr, wi)
        row8 = lax.broadcasted_iota(jnp.int32, (8, SSM_CW), 0)
        pr_ = jnp.where(row8 > 0, pltpu.roll(s_re[pl.ds((seg - 1) * 8, 8), :], 1, 0), 0.0)
        pi_ = jnp.where(row8 > 0, pltpu.roll(s_im[pl.ds((seg - 1) * 8, 8), :], 1, 0), 0.0)
        acc_r = acc_r + gr * pr_ + gi * pi_
        acc_i = acc_i + gi * pr_ - gr * pi_
        dar_ref[...] = jnp.sum(acc_r, axis=0, keepdims=True)
        dai_ref[...] = jnp.sum(acc_i, axis=0, keepdims=True)

        dbbr_ref[...] = jnp.zeros((1, SSM_CU, SSM_CW), F32)
        dbbi_ref[...] = jnp.zeros((1, SSM_CU, SSM_CW), F32)
        dcmr_ref[...] = jnp.zeros((1, SSM_CW, SSM_CU), F32)
        dcmi_ref[...] = jnp.zeros((1, SSM_CW, SSM_CU), F32)
        dd_ref[...] = jnp.zeros((1, SSM_CU), F32)

        def grad_step(r, c):
            rows = pl.ds(pl.multiple_of(r * rb, rb), rb)
            ub, dyv = u_ref[rows, :], dy_ref[rows, :]
            ubb, dyb = ub.astype(BF16), dyv.astype(BF16)
            grb, gib = g_re[rows, :].astype(BF16), g_im[rows, :].astype(BF16)
            dbbr_ref[0] += _dot(ubb, grb, TN)
            dbbi_ref[0] += _dot(ubb, gib, TN)
            dcmr_ref[0] += _dot(s_re[rows, :].astype(BF16), dyb, TN)
            dcmi_ref[0] -= _dot(s_im[rows, :].astype(BF16), dyb, TN)
            du_ref[rows, :] = _dot(grb, bbr_ref[0], NT) + _dot(gib, bbi_ref[0], NT) + d_ref[...] * dyv
            dd_ref[...] += jnp.sum(dyv * ub, axis=0, keepdims=True)
            return c

        lax.fori_loop(0, L // rb, grad_step, 0)

    chunk = lambda rows, cols: pl.BlockSpec((rows, cols), lambda q: (0, q))
    mat = lambda r, c: pl.BlockSpec((1, r, c), lambda q: (q, 0, 0))
    anyspec = pl.BlockSpec(memory_space=pl.ANY)
    big = lambda: pltpu.VMEM((L, SSM_CW), F32)
    return pl.pallas_call(
        body, name="ssm_bwd", grid=(SSM_CHUNKS,),
        in_specs=[chunk(L, SSM_CU), chunk(L, SSM_CU), anyspec, anyspec, chunk(1, SSM_CW), chunk(1, SSM_CW),
                  mat(SSM_CU, SSM_CW), mat(SSM_CU, SSM_CW), mat(SSM_CW, SSM_CU), mat(SSM_CW, SSM_CU), chunk(1, SSM_CU)],
        out_specs=[chunk(L, SSM_CU), mat(SSM_CU, SSM_CW), mat(SSM_CU, SSM_CW), mat(SSM_CW, SSM_CU), mat(SSM_CW, SSM_CU),
                   chunk(1, SSM_CW), chunk(1, SSM_CW), chunk(1, SSM_CU)],
        out_shape=[jax.ShapeDtypeStruct((L, SSM_WIDTH), F32),
                   jax.ShapeDtypeStruct((SSM_CHUNKS, SSM_CU, SSM_CW), F32), jax.ShapeDtypeStruct((SSM_CHUNKS, SSM_CU, SSM_CW), F32),
                   jax.ShapeDtypeStruct((SSM_CHUNKS, SSM_CW, SSM_CU), F32), jax.ShapeDtypeStruct((SSM_CHUNKS, SSM_CW, SSM_CU), F32),
                   jax.ShapeDtypeStruct((1, SSM_NSTATE), F32), jax.ShapeDtypeStruct((1, SSM_NSTATE), F32),
                   jax.ShapeDtypeStruct((1, SSM_WIDTH), F32)],
        scratch_shapes=[big(), big(), big(), big(), pltpu.SemaphoreType.DMA((2,))],
        compiler_params=_params("arbitrary", vmem=VMEM_BIG))(
            dy, u, s_re_all, s_im_all, a_re, a_im, bb_re, bb_im, cm_re, cm_im, d_skip)


def _place():
    return lax.axis_index("x"), lax.axis_index("y"), lax.axis_index("c")


def _small_gather_call(blocks, name):
    n = len(blocks)

    def body(*refs):
        start, finish = _exchange_phases(refs[:n], refs[n:2 * n], *refs[2 * n:], same_source=True)
        start()
        finish()

    return pl.pallas_call(
        body, name=name, in_specs=[ANY_SPEC] * n, out_specs=[ANY_SPEC] * n,
        out_shape=[jax.ShapeDtypeStruct((N_DEV,) + b.shape, b.dtype) for b in blocks],
        scratch_shapes=_comm_sems(n))(*blocks)


def _comm_sems(n):
    return [pltpu.SemaphoreType.DMA((7 * n,)), pltpu.SemaphoreType.DMA((7 * n,)), pltpu.SemaphoreType.DMA((n,))]


def _gather_phases(x_refs, out_refs, send_sems, recv_sems, local_sems):
    x, y, c = _place()
    me, sibling = (x, y, c), (x, y, 1 - c)
    chips = [(1 - x, y), (x, 1 - y), (1 - x, 1 - y)]
    n = len(x_refs)

    def copy(k, a, blk, to, from_input=False):
        slot = out_refs[a].at[4 * blk[0] + 2 * blk[1] + blk[2]]
        return pltpu.make_async_remote_copy(
            src_ref=x_refs[a] if from_input else slot, dst_ref=slot,
            send_sem=send_sems.at[k * n + a], recv_sem=recv_sems.at[k * n + a], device_id=to, device_id_type=MESH_ID)

    mine = [pltpu.make_async_copy(x_refs[a], out_refs[a].at[4 * x + 2 * y + c], local_sems.at[a]) for a in range(n)]
    first, passed = [], []
    for a in range(n):
        first.append(copy(0, a, me, sibling, True))
        first += [copy(1 + j, a, me, (*chip, c), True) for j, chip in enumerate(chips)]
        passed += [copy(4 + j, a, (*chip, c), sibling) for j, chip in enumerate(chips)]

    def start():
        for cp in mine + first:
            cp.start()

    def forward():
        for j, chip in enumerate(chips):
            for a in range(n):
                copy(1 + j, a, (*chip, c), me).wait_recv()
                passed[3 * a + j].start()

    def finish():
        for a in range(n):
            copy(0, a, sibling, me).wait_recv()
            for j, chip in enumerate(chips):
                copy(4 + j, a, (*chip, 1 - c), me).wait_recv()
        for cp in first + passed:
            cp.wait_send()
        for cp in mine:
            cp.wait()

    return start, forward, finish


def _exchange_phases(p_refs, out_refs, send_sems, recv_sems, local_sems, same_source=False):
    x, y, c = _place()
    me = 4 * x + 2 * y + c
    n = len(p_refs)

    def flip(k):
        px = 1 - x if k & 4 else x
        py = 1 - y if k & 2 else y
        pc = 1 - c if k & 1 else c
        return (px, py, pc), 4 * px + 2 * py + pc

    def source(a, slot):
        return p_refs[a] if same_source else p_refs[a].at[slot]

    def copy(k, a, landing):
        peer, peer_slot = flip(k)
        return pltpu.make_async_remote_copy(
            src_ref=source(a, peer_slot), dst_ref=out_refs[a].at[peer_slot if landing else me],
            send_sem=send_sems.at[(k - 1) * n + a], recv_sem=recv_sems.at[(k - 1) * n + a],
            device_id=peer, device_id_type=MESH_ID)

    mine = [pltpu.make_async_copy(source(a, me), out_refs[a].at[me], local_sems.at[a]) for a in range(n)]
    sends = [copy(k, a, False) for k in range(1, N_DEV) for a in range(n)]

    def start():
        for cp in mine + sends:
            cp.start()

    def finish():
        for k in range(1, N_DEV):
            for a in range(n):
                copy(k, a, True).wait_recv()
        for cp in sends:
            cp.wait_send()
        for cp in mine:
            cp.wait()

    return start, finish


def _adam_math(g, w, m, v):
    c1 = 1.0 / (1.0 - ADAM_B1 ** ADAM_STEP)
    c2 = 1.0 / (1.0 - ADAM_B2 ** ADAM_STEP)
    m_new = ADAM_B1 * m + (1.0 - ADAM_B1) * g
    v_new = ADAM_B2 * v + (1.0 - ADAM_B2) * (g * g)
    delta = -ADAM_LR * ((m_new * c1) / (jnp.sqrt(v_new * c2) + ADAM_EPS) + ADAM_WD * w)
    return g, delta, m_new, v_new


def _sum_slices(s_ref):
    g = s_ref[0].astype(F32)
    for k in range(1, N_DEV):
        g = g + s_ref[k].astype(F32)
    return g


def _adam_call(slices, w, m, v, name):
    d1, rest = w.shape[1], w.shape[2:]
    zeros = (0,) * len(rest)
    by_lanes = len(rest) == 1 and d1 > 256 and d1 % 16 != 0
    if by_lanes:
        tile = _fit(rest[0], 256)
        steps = rest[0] // tile
        own = pl.BlockSpec((1, d1, tile), lambda i: (0, 0, i))
        sl = pl.BlockSpec((N_DEV, 1, d1, tile), lambda i: (0, 0, 0, i))
    else:
        tile = _fit(d1, 256, 16) if len(rest) == 1 else _fit(d1, 8, 8)
        steps = d1 // tile
        own = pl.BlockSpec((1, tile) + rest, lambda i: (0, i) + zeros)
        sl = pl.BlockSpec((N_DEV, 1, tile) + rest, lambda i: (0, 0, i) + zeros)

    def body(s_ref, w_ref, m_ref, v_ref, g_ref, d_ref, mo_ref, vo_ref):
        g_ref[...], d_ref[...], mo_ref[...], vo_ref[...] = _adam_math(_sum_slices(s_ref), w_ref[...], m_ref[...], v_ref[...])

    out = jax.ShapeDtypeStruct(w.shape, F32)
    return pl.pallas_call(
        body, name=name, grid=(steps,), in_specs=[sl, own, own, own],
        out_specs=[own, own, own, own], out_shape=[out, out, out, out],
        compiler_params=_params("parallel"))(slices, w, m, v)


def _adam_small_call(rows_all, row_params, slices, params):
    nr, n = len(row_params), len(row_params) + len(params)

    def row_sum(rows_ref, a, width):
        g = rows_ref[0, pl.ds(a, 1), pl.ds(0, width)]
        for k in range(1, N_DEV):
            g = g + rows_ref[k, pl.ds(a, 1), pl.ds(0, width)]
        return g

    def body(rows_ref, *refs):
        slice_refs, wmv, outs = refs[:n - nr], refs[n - nr:n - nr + 3 * n], refs[n - nr + 3 * n:]
        outs[4 * n][...] = row_sum(rows_ref, nr, LANES)
        for a in range(n):
            w_ref, m_ref, v_ref = wmv[3 * a:3 * a + 3]
            if a < nr:
                g = row_sum(rows_ref, a, w_ref.shape[1])
            else:
                g = _sum_slices(slice_refs[a - nr])
            res = _adam_math(g, w_ref[...], m_ref[...], v_ref[...])
            for r in range(4):
                outs[4 * a + r][...] = res[r]

    every = list(row_params) + list(params)
    flat = pl.pallas_call(
        body, name="adam_small",
        out_shape=[jax.ShapeDtypeStruct(w.shape, F32) for w, _, _ in every for _ in range(4)]
        + [jax.ShapeDtypeStruct((1, LANES), F32)],
        compiler_params=pltpu.CompilerParams(vmem_limit_bytes=VMEM_BIG),
    )(rows_all, *slices, *[t for wmv in every for t in wmv])
    return [flat[4 * a:4 * a + 4] for a in range(n)], flat[4 * n][0, 0]


BIG = (("w_in", 1024, 404, 1), ("w_uq", 384, 96, 1), ("w_uk", 256, 64, 1), ("w_uv", 256, 64, 1),
       ("w_glu", 64, 512, 0), ("w_branch_attn", 512, 128, 1), ("w_branch_ssm", 512, 128, 1),
       ("w_out", 128, 1024, 0), ("w_up", 1024, 704, 1), ("w_down", 352, 1024, 0), ("conv_w", 3, 704, 1))
GATHER_FIRST, GATHER_PROJ, GATHER_LATER = BIG[:1], BIG[1:4], BIG[8:] + BIG[4:8]
GRADS_EARLY, GRADS_PROJ, GRADS_LAST = BIG[8:] + BIG[4:8], BIG[1:4], BIG[:1]
SMALL = (("mix_norm_pre", (1024,)), ("q_norm", (384,)), ("kv_norm", (256,)), ("ssm_lambda_re", (32, 64)),
         ("ssm_lambda_im", (32, 64)), ("ssm_log_dt", (32,)), ("ssm_b_re", (32, 64, 16)), ("ssm_b_im", (32, 64, 16)),
         ("ssm_c_re", (32, 16, 64)), ("ssm_c_im", (32, 16, 64)), ("ssm_d", (32, 16)), ("b_glu", (512,)),
         ("b_gate", (2048,)), ("mix_norm_post", (1024,)), ("ffn_norm_pre", (1024,)), ("conv_b", (5632,)),
         ("ffn_norm_post", (1024,)))


TRANSPOSED = ("w_in", "w_uq", "w_uk", "w_uv", "w_up")


STORED_SWAP = {**{name: (1, 2) for name in TRANSPOSED}, "ssm_b_re": (2, 3), "ssm_b_im": (2, 3), "ssm_d": (1, 2)}


def _stored(name, arr):
    return jnp.swapaxes(arr, *STORED_SWAP[name]) if name in STORED_SWAP else arr


def _to_slices(name, full, rows, cols, axis):
    if name in TRANSPOSED:
        return full.reshape(N_DEV, cols, rows)
    if axis == 1:
        return full.reshape(rows, N_DEV, cols).transpose(1, 0, 2)
    return full.reshape(N_DEV, rows, cols)


def _from_slices(name, parts, rows, cols, axis):
    if name in TRANSPOSED:
        return parts.reshape(N_DEV * cols, rows)
    if axis == 1:
        return parts.transpose(1, 0, 2).reshape(rows, N_DEV * cols)
    return parts.reshape(N_DEV * rows, cols)


def _time_perm(a, L):
    return a.reshape(8, L // 8, a.shape[-1]).transpose(1, 0, 2).reshape(L, a.shape[-1])


def _time_unperm(a, L):
    return a.reshape(L // 8, 8, a.shape[-1]).transpose(1, 0, 2).reshape(L, a.shape[-1])


def _block_diag(w, rows_first):
    eye = jnp.eye(8, dtype=w.dtype)
    g = w.reshape(SSM_CHUNKS, 8, w.shape[1], w.shape[2])
    return jnp.einsum("qgrc,gk->qgrkc", g, eye).reshape(SSM_CHUNKS, 8 * w.shape[1], 8 * w.shape[2])


def _block_diag_t(m, r, c):
    eye = jnp.eye(8, dtype=m.dtype)
    return jnp.einsum("qgrkc,gk->qgrc", m.reshape(SSM_CHUNKS, 8, r, 8, c), eye).reshape(SSM_GROUPS, r, c)


def kernel(x, positions, mix_norm_pre, w_in, q_norm, w_uq, kv_norm, w_uk, w_uv, ssm_lambda_re, ssm_lambda_im, ssm_log_dt, ssm_b_re, ssm_b_im, ssm_c_re, ssm_c_im, ssm_d, w_glu, b_glu, w_branch_attn, w_branch_ssm, b_gate, w_out, mix_norm_post, ffn_norm_pre, w_up, conv_w, conv_b, w_down, ffn_norm_post, loss_target, m_mix_norm_pre, m_w_in, m_q_norm, m_w_uq, m_kv_norm, m_w_uk, m_w_uv, m_ssm_lambda_re, m_ssm_lambda_im, m_ssm_log_dt, m_ssm_b_re, m_ssm_b_im, m_ssm_c_re, m_ssm_c_im, m_ssm_d, m_w_glu, m_b_glu, m_w_branch_attn, m_w_branch_ssm, m_b_gate, m_w_out, m_mix_norm_post, m_ffn_norm_pre, m_w_up, m_conv_w, m_conv_b, m_w_down, m_ffn_norm_post, v_mix_norm_pre, v_w_in, v_q_norm, v_w_uq, v_kv_norm, v_w_uk, v_w_uv, v_ssm_lambda_re, v_ssm_lambda_im, v_ssm_log_dt, v_ssm_b_re, v_ssm_b_im, v_ssm_c_re, v_ssm_c_im, v_ssm_d, v_w_glu, v_b_glu, v_w_branch_attn, v_w_branch_ssm, v_b_gate, v_w_out, v_mix_norm_post, v_ffn_norm_pre, v_w_up, v_conv_w, v_conv_b, v_w_down, v_ffn_norm_post):
    given = dict(locals())
    L = x.shape[1]
    xs = x[0]
    target = loss_target[0]

    def shard_bits(group):
        return [given[name][0] if name == "conv_w" else _stored(name, given[name])[0].astype(BF16) for name, _, _, _ in group]

    W = {}

    def unpack_weights(gathered, group):
        for (name, rows, cols, axis), parts in zip(group, gathered):
            W[name] = _from_slices(name, parts, rows, cols, axis)

    hn1, *gathered_w_in = _rms_fwd_call(xs, mix_norm_pre, "rms_pre", shard_bits(GATHER_FIRST))
    unpack_weights(gathered_w_in, GATHER_FIRST)

    wit = W["w_in"]
    zero_rows = lambda r: jnp.zeros((r, D_MODEL), BF16)
    kr_end = P_KR + QK_ROPE
    w_in_pt = jnp.concatenate(
        [wit[:P_KR], zero_rows(QK_NOPE), wit[P_KR:kr_end], zero_rows(LANES - QK_HEAD), wit[kr_end:]], axis=0)

    proj, *gathered_proj = _mm(hn1, w_in_pt, "mm_in", tb=True, tn=1664, gather=shard_bits(GATHER_PROJ))
    unpack_weights(gathered_proj, GATHER_PROJ)
    head_rows = lambda wt, width: jnp.pad(wt.reshape(N_HEADS, width, wt.shape[1]), ((0, 0), (0, LANES - width), (0, 0)))
    w_uq_pt = head_rows(W["w_uq"], QK_HEAD).reshape(HEAD_PAD, Q_RANK)
    w_kv_pt = jnp.stack([head_rows(W["w_uk"], QK_NOPE), head_rows(W["w_uv"], V_HEAD)], axis=1
                        ).reshape(2 * HEAD_PAD, KV_RANK)
    half = jnp.arange(QK_ROPE // 2, dtype=F32)
    inv_freq = ROPE_THETA ** (-2.0 * half / QK_ROPE)
    inv_freq = jnp.pad(jnp.concatenate([inv_freq, inv_freq]), (QK_NOPE, LANES - QK_HEAD)).reshape(1, LANES)
    pos_col = positions.astype(F32).reshape(L, 1)
    qn, ckvn, q_r, kv_r, cosf, sinf = _mla_proj_call(proj, q_norm, kv_norm, w_uq_pt, w_kv_pt, pos_col, inv_freq)
    attn, lse, *gathered_later = _attn_fwd_call(q_r, kv_r, shard_bits(GATHER_LATER))
    unpack_weights(gathered_later, GATHER_LATER)
    w_ba_p = jnp.pad(W["w_branch_attn"].reshape(N_HEADS, V_HEAD, D_MODEL), ((0, 0), (0, LANES - V_HEAD), (0, 0))
                     ).reshape(HEAD_PAD, D_MODEL)

    col = lambda a: a.reshape(SSM_NSTATE, -1)
    lr_c, li_c = col(ssm_lambda_re[0]), col(ssm_lambda_im[0])
    ldt_c = col(jnp.broadcast_to(ssm_log_dt[0][:, None], (SSM_GROUPS, SSM_STATE)))
    br_c, bi_c = col(ssm_b_re[0]), col(ssm_b_im[0])
    a_re_c, a_im_c, bb_re_c, bb_im_c = _disc_call(lr_c, li_c, ldt_c, br_c, bi_c)
    a_re, a_im = a_re_c.reshape(1, SSM_NSTATE), a_im_c.reshape(1, SSM_NSTATE)
    to_bb = lambda b: _block_diag(b.reshape(SSM_GROUPS, SSM_STATE, SSM_GROUP).transpose(0, 2, 1), True).astype(BF16)
    bb_re, bb_im = to_bb(bb_re_c), to_bb(bb_im_c)
    to_cm = lambda c_: _block_diag(c_[0].transpose(0, 2, 1), True).astype(BF16)
    cm_re, cm_im = to_cm(ssm_c_re), to_cm(ssm_c_im)
    d_skip = ssm_d.reshape(1, SSM_WIDTH)
    u_p = _time_perm(proj[:, P_U:P_GATE], L)
    y1, s_re, s_im = _ssm_fwd_call(u_p, a_re, a_im, bb_re, bb_im, cm_re, cm_im, d_skip)
    w_glu_b = W["w_glu"]
    ssm_p = _glu_call(y1, w_glu_b, b_glu)
    ssm = _time_unperm(ssm_p, L)

    pa = _mm(attn, w_ba_p, "mm_ba")
    ps = _mm(ssm, W["w_branch_ssm"], "mm_bs")
    merged = _merge_call(proj, b_gate, pa, ps)
    wide = lambda dt: (D_MODEL, dt)
    o, x2, hn2 = _mm_rows(merged, W["w_out"], "mm_out", _post_mix_rows, [xs], [mix_norm_post, ffn_norm_pre],
                          [wide(F32), wide(F32), wide(BF16)], [])
    h = _mm(hn2, W["w_up"], "mm_up", tb=True, tn=D_FF)
    cw = W["conv_w"]
    act = _conv_act_call(h, cw, conv_b)
    dy, dff, loss_row, g_ffn_norm_post = _mm_rows(
        act, W["w_down"], "mm_down", _ffn_out_rows, [x2, target], [ffn_norm_post], [wide(F32), wide(BF16)],
        [LANES, D_MODEL], tk=1408)

    da = _mm(dff, W["w_down"], "mm_down_dx", tb=True, tn=D_FF)
    g_w_down = _mm_tn(act, dff, "mm_down_dw", tm=1408)
    dgate, dval, dcw_g, dcw_v, dcb_g, dcb_v = _conv_act_bwd_call(da, h, cw, conv_b)
    g_conv_w = jnp.concatenate([dcw_g, dcw_v], axis=1)
    g_conv_b = jnp.concatenate([dcb_g, dcb_v], axis=1)
    dh = _conv_t_call(dgate, dval, cw)
    dx2, do, g_ffn_norm_pre, g_mix_norm_post = _mm_rows(
        dh, W["w_up"], "mm_up_dx", _post_bwd_rows, [x2, dy, o], [ffn_norm_pre, mix_norm_post], [wide(F32), wide(BF16)],
        [D_MODEL, D_MODEL], tk=1408)
    g_w_up = _mm_tn(dh, hn2, "mm_up_dw", tm=1408)
    dmerged = _mm(do, W["w_out"], "mm_out_dx", tb=True)
    g_w_out = _mm_tn(merged, do, "mm_out_dw")
    dpa, dps, dl0, dl1, db0, db1 = _merge_bwd_call(dmerged, proj, b_gate, pa, ps)
    g_b_gate = jnp.concatenate([db0, db1], axis=1)
    dattn = _mm(dpa, w_ba_p, "mm_ba_dx", tb=True, out_dtype=BF16)
    g_w_ba = _mm_tn(attn, dpa, "mm_ba_dw").reshape(N_HEADS, LANES, D_MODEL)[:, :V_HEAD].reshape(N_HEADS * V_HEAD, D_MODEL)
    dssm = _mm(dps, W["w_branch_ssm"], "mm_bs_dx", tb=True)
    g_w_bs = _mm_tn(ssm, dps, "mm_bs_dw")

    dy1, g_w_glu, g_b_glu = _glu_bwd_call(_time_perm(dssm, L), y1, w_glu_b, b_glu)
    du_p, dbb_re, dbb_im, dcm_re, dcm_im, da_re, da_im, g_ssm_d = _ssm_bwd_call(
        dy1, u_p, s_re, s_im, a_re, a_im, bb_re, bb_im, cm_re, cm_im, d_skip)
    du = _time_unperm(du_p, L)
    from_bb = lambda m: col(_block_diag_t(m, SSM_GROUP, SSM_STATE).transpose(0, 2, 1))
    dlr, dli, dldt, dbr, dbi = _disc_bwd_call(
        lr_c, li_c, ldt_c, br_c, bi_c, da_re.reshape(SSM_NSTATE, 1), da_im.reshape(SSM_NSTATE, 1), from_bb(dbb_re), from_bb(dbb_im))
    g_c_re = _block_diag_t(dcm_re, SSM_STATE, SSM_GROUP).transpose(0, 2, 1)
    g_c_im = _block_diag_t(dcm_im, SSM_STATE, SSM_GROUP).transpose(0, 2, 1)

    def grad_slices(group, grads):
        return [_to_slices(name, grads[name], rows, cols, axis) for name, rows, cols, axis in group]

    early_grads = {"w_up": g_w_up, "w_down": g_w_down, "conv_w": g_conv_w, "w_glu": g_w_glu.astype(BF16),
                   "w_branch_attn": g_w_ba, "w_branch_ssm": g_w_bs, "w_out": g_w_out}
    b_stored = lambda d: d.reshape(SSM_GROUPS, SSM_STATE, SSM_GROUP).transpose(0, 2, 1)
    per_state = lambda d: d.reshape(SSM_GROUPS, SSM_STATE)
    ssm_partials = {"ssm_lambda_re": per_state(dlr), "ssm_lambda_im": per_state(dli),
                    "ssm_b_re": b_stored(dbr), "ssm_b_im": b_stored(dbi),
                    "ssm_c_re": g_c_re, "ssm_c_im": g_c_im, "ssm_d": g_ssm_d.reshape(SSM_GROUPS, SSM_GROUP).T}
    ssm_shapes = [(name, ssm_partials[name].shape) for name, _ in SMALL if name in ssm_partials]
    dq, dkv, *landed = _attn_bwd_call(
        q_r, kv_r, attn, dattn, lse, grad_slices(GRADS_EARLY, early_grads),
        [ssm_partials[name].reshape(-1, LANES) if len(shp) == 3 else ssm_partials[name].reshape((1,) + shp)
         for name, shp in ssm_shapes])
    received_early = landed[:len(GRADS_EARLY)]
    ssm_all = {name: got.reshape((N_DEV, 1) + shp) for (name, shp), got in zip(ssm_shapes, landed[len(GRADS_EARLY):])}
    dq_p, dkv_p, dlatent, g_q_norm, g_kv_norm = _mla_proj_bwd_call(
        dq, dkv, cosf, sinf, proj, q_norm, kv_norm, w_uq_pt, w_kv_pt)
    g_w_uq = _mm_tn(dq_p, qn, "mm_uq_dw").reshape(N_HEADS, LANES, Q_RANK)[:, :QK_HEAD].reshape(N_HEADS * QK_HEAD, Q_RANK)
    g_w_kv = _mm_tn(ckvn, dkv_p, "mm_ukv_dw").T.reshape(N_HEADS, 2, LANES, KV_RANK)
    g_w_uk = g_w_kv[:, 0, :QK_NOPE].reshape(N_HEADS * QK_NOPE, KV_RANK)
    g_w_uv = g_w_kv[:, 1, :V_HEAD].reshape(N_HEADS * V_HEAD, KV_RANK)
    dproj = jnp.concatenate([dlatent, du.astype(BF16), dl0, dl1], axis=1)
    proj_grads = {"w_uq": g_w_uq, "w_uk": g_w_uk, "w_uv": g_w_uv}
    g_w_in_pt, *received_proj = _mm_tn(dproj, hn1, "mm_in_dw", tm=1664, exchange=grad_slices(GRADS_PROJ, proj_grads))
    g_w_in = jnp.concatenate([g_w_in_pt[:P_KR], g_w_in_pt[P_KR + QK_NOPE:P_KR + QK_HEAD], g_w_in_pt[P_U:]], axis=0)
    grad_x, g_mix_norm_pre, *received_last = _mm_in_dx_call(
        dproj, w_in_pt, xs, dx2, mix_norm_pre, grad_slices(GRADS_LAST, {"w_in": g_w_in}))

    results = {}
    wmv = lambda name: tuple(_stored(name, given[prefix + name]) for prefix in ("", "m_", "v_"))
    unstored = lambda name, res: [_stored(name, r) for r in res]
    whole = ("w_uq", "w_uk", "w_uv", "w_glu", "w_branch_attn", "w_branch_ssm", "conv_w")
    landed_small = dict(ssm_all)
    for group, received in ((GRADS_EARLY, received_early), (GRADS_PROJ, received_proj), (GRADS_LAST, received_last)):
        for (name, _, _, _), rec in zip(group, received):
            if name in whole:
                landed_small[name] = rec[:, None]
            else:
                results[name] = unstored(name, _adam_call(rec[:, None], *wmv(name), "adam_" + name))

    vec_grads = {"mix_norm_pre": g_mix_norm_pre, "q_norm": g_q_norm, "kv_norm": g_kv_norm,
                 "ssm_log_dt": jnp.sum(dldt.reshape(SSM_GROUPS, SSM_STATE), axis=1),
                 "b_glu": g_b_glu, "b_gate": g_b_gate, "mix_norm_post": g_mix_norm_post,
                 "ffn_norm_pre": g_ffn_norm_pre, "ffn_norm_post": g_ffn_norm_post}
    vec_names = [name for name, _ in SMALL if name in vec_grads]
    width = max(shp[0] for name, shp in SMALL if name in vec_grads)
    rows = [jnp.pad(vec_grads[name].reshape(1, -1), ((0, 0), (0, width - vec_grads[name].size))) for name in vec_names]
    rows.append(jnp.pad(loss_row, ((0, 0), (0, width - LANES))))
    rows.append(jnp.zeros((-len(rows) % 8, width), F32))
    rows_all, landed_small["conv_b"] = _small_gather_call([jnp.concatenate(rows, axis=0), g_conv_b], "gather_small_grads")
    others = ["conv_b"] + [name for name, _ in ssm_shapes] + list(whole)
    small_results, loss = _adam_small_call(
        rows_all, [wmv(n) for n in vec_names], [landed_small[n] for n in others], [wmv(n) for n in others])
    for name, res in zip(vec_names + others, small_results):
        results[name] = unstored(name, res)

    order = ["mix_norm_pre", "w_in", "q_norm", "w_uq", "kv_norm", "w_uk", "w_uv", "ssm_lambda_re", "ssm_lambda_im",
             "ssm_log_dt", "ssm_b_re", "ssm_b_im", "ssm_c_re", "ssm_c_im", "ssm_d", "w_glu", "b_glu", "w_branch_attn",
             "w_branch_ssm", "b_gate", "w_out", "mix_norm_post", "ffn_norm_pre", "w_up", "conv_w", "conv_b", "w_down",
             "ffn_norm_post"]
    outs = [loss, grad_x[None]]
    for kind in range(4):
        outs += [results[name][kind] for name in order]
    return tuple(outs)
```

```python
import math

import jax
import jax.numpy as jnp
from jax import lax
from jax.experimental import pallas as pl
from jax.experimental.pallas import tpu as pltpu

F32 = jnp.float32
BF16 = jnp.bfloat16
MESH_ID = pl.DeviceIdType.MESH

N_DEV = 8
LANES = 128
D_MODEL = 1024
N_HEADS = 8
QK_NOPE = 64
QK_ROPE = 32
QK_HEAD = QK_NOPE + QK_ROPE
V_HEAD = 64
Q_RANK = 384
KV_RANK = 256
ROPE_THETA = 10000.0
SSM_WIDTH = 512
SSM_GROUP = 16
SSM_GROUPS = 32
SSM_STATE = 64
SSM_NSTATE = SSM_GROUPS * SSM_STATE
SSM_CHUNKS = 4
D_FF = 2816
EPS = 1e-6
ADAM_LR, ADAM_B1, ADAM_B2, ADAM_EPS, ADAM_WD, ADAM_STEP = 0.001, 0.9, 0.999, 1e-08, 0.01, 10

P_CQ, P_CKV, P_KR, P_U, P_GATE = 0, 384, 640, 768, 1280
HEAD_PAD = N_HEADS * LANES

VMEM_BIG = 52 * 1024 * 1024

_GELU_C0 = math.sqrt(2.0 / math.pi)
_GELU_C1 = 0.044715
NEG = -1e30


def _fit(n, pref, mult=LANES):
    if n <= pref:
        return n
    t = (pref // mult) * mult
    while t > 0 and n % t:
        t -= mult
    assert t > 0, (n, pref, mult)
    return t


def _gelu(x):
    return x * (0.5 * (1.0 + jnp.tanh(_GELU_C0 * x * (1.0 + _GELU_C1 * (x * x)))))


def _gelu_and_grad(x):
    x2 = x * x
    t = jnp.tanh(_GELU_C0 * x * (1.0 + _GELU_C1 * x2))
    half = 0.5 * (1.0 + t)
    return x * half, half + 0.5 * x * (1.0 - t * t) * _GELU_C0 * (1.0 + 3.0 * _GELU_C1 * x2)


def _sigmoid(x):
    return 1.0 / (1.0 + jnp.exp(-x))


def _dot(a, b, dims):
    return lax.dot_general(a, b, (dims, ((), ())), preferred_element_type=F32)


NN = ((1,), (0,))
NT = ((1,), (1,))
TN = ((0,), (0,))


def _params(*sem, vmem=None):
    return pltpu.CompilerParams(dimension_semantics=tuple(sem), vmem_limit_bytes=vmem)


def _mm(a, b, name, tb=False, out_dtype=F32, tm=1024, tn=1024, tk=1024, gather=()):
    M, K = a.shape
    if tb:
        N, K2 = b.shape
    else:
        K2, N = b.shape
    assert K == K2, (a.shape, b.shape, tb)
    tm, tn, tk = _fit(M, tm), _fit(N, tn), _fit(K, tk)
    nk = K // tk
    grid = (M // tm, N // tn, nk)
    steps = grid[0] * grid[1] * grid[2]
    dims = NT if tb else NN
    n = len(gather)

    def body(a_ref, b_ref, *refs):
        o_ref, scratch = refs[n], refs[2 * n + 1:]
        step = (pl.program_id(0) * grid[1] + pl.program_id(1)) * grid[2] + pl.program_id(2)
        if n:
            start, forward, finish = _gather_phases(refs[:n], refs[n + 1:2 * n + 1], *scratch[-3:])
            pl.when(step == 0)(start)
            pl.when(step == steps // 2)(forward)
        part = _dot(a_ref[...].astype(BF16), b_ref[...].astype(BF16), dims)
        if nk == 1:
            o_ref[...] = part.astype(out_dtype)
        else:
            acc_ref = scratch[0]
            k = pl.program_id(2)

            @pl.when(k == 0)
            def _():
                acc_ref[...] = part

            @pl.when(k > 0)
            def _():
                acc_ref[...] += part

            @pl.when(k == nk - 1)
            def _():
                o_ref[...] = acc_ref[...].astype(out_dtype)
        if n:
            pl.when(step == steps - 1)(finish)

    a_spec = pl.BlockSpec((tm, tk), lambda i, j, k: (i, k))
    b_spec = pl.BlockSpec((tn, tk), lambda i, j, k: (j, k)) if tb else pl.BlockSpec((tk, tn), lambda i, j, k: (k, j))
    landed = [jax.ShapeDtypeStruct((N_DEV,) + p.shape, p.dtype) for p in gather]
    out = pl.pallas_call(
        body, name=name, grid=grid,
        in_specs=[a_spec, b_spec] + [ANY_SPEC] * n,
        out_specs=[pl.BlockSpec((tm, tn), lambda i, j, k: (i, j))] + [ANY_SPEC] * n,
        out_shape=[jax.ShapeDtypeStruct((M, N), out_dtype)] + landed,
        scratch_shapes=([] if nk == 1 else [pltpu.VMEM((tm, tn), F32)]) + (_comm_sems(n) if n else []),
        compiler_params=_params(*(("arbitrary",) * 3 if n else ("parallel", "parallel", "arbitrary")), vmem=VMEM_BIG),
    )(a, b, *gather)
    return out if n else out[0]


def _mm_rows(a, b, name, epilogue, rows_in, vecs_in, rows_out, vecs_out, tb=False, tk=1024):
    M, K = a.shape
    N = b.shape[0] if tb else b.shape[1]
    tm, tk = _fit(M, 512), _fit(K, tk)
    nk = K // tk
    nr, nv, nro = len(rows_in), len(vecs_in), len(rows_out)

    def body(a_ref, b_ref, *refs):
        ins, outs, acc_ref = refs[:nr + nv], refs[nr + nv:nr + nv + nro + len(vecs_out)], refs[-1]
        i, k = pl.program_id(0), pl.program_id(1)
        part = _dot(a_ref[...], b_ref[...], NT if tb else NN)

        def finish(product):
            res = epilogue(product, *[r[...] for r in ins])
            for ref, val in zip(outs[:nro], res[:nro]):
                ref[...] = val.astype(ref.dtype)
            for ref, val in zip(outs[nro:], res[nro:]):
                _acc(ref, i == 0, val)

        if nk == 1:
            finish(part)
        else:
            @pl.when(k == 0)
            def _():
                acc_ref[...] = part

            @pl.when(jnp.logical_and(k > 0, k < nk - 1))
            def _():
                acc_ref[...] += part

            @pl.when(k == nk - 1)
            def _():
                finish(acc_ref[...] + part)

    row = lambda w: pl.BlockSpec((tm, w), lambda i, k: (i, 0))
    vec = lambda w: pl.BlockSpec((1, w), lambda i, k: (0, 0))
    b_spec = pl.BlockSpec((N, tk), lambda i, k: (0, k)) if tb else pl.BlockSpec((tk, N), lambda i, k: (k, 0))
    return pl.pallas_call(
        body, name=name, grid=(M // tm, nk),
        in_specs=[pl.BlockSpec((tm, tk), lambda i, k: (i, k)), b_spec] + [row(r.shape[1]) for r in rows_in]
        + [vec(v.shape[1]) for v in vecs_in],
        out_specs=[row(w) for w, _ in rows_out] + [vec(w) for w in vecs_out],
        out_shape=[jax.ShapeDtypeStruct((M, w), dt) for w, dt in rows_out] + [jax.ShapeDtypeStruct((1, w), F32) for w in vecs_out],
        scratch_shapes=[pltpu.VMEM((tm, N), F32)],
        compiler_params=_params("arbitrary", "arbitrary", vmem=VMEM_BIG))(a, b, *rows_in, *vecs_in)


def _mm_in_dx_call(dproj, w_in_pt, x, dx2, g_pre, exchange):
    L, K = dproj.shape
    N = w_in_pt.shape[1]
    tm, tk = _fit(L, 512), _fit(K, 1664)
    nm, nk = L // tm, K // tk
    n = len(exchange)

    def body(a_ref, b_ref, x_ref, dx2_ref, g_ref, *refs):
        parts, (gx_ref, dg_ref), got = refs[:n], refs[n:n + 2], refs[n + 2:2 * n + 2]
        acc_ref = refs[2 * n + 2]
        i, k = pl.program_id(0), pl.program_id(1)
        start, finish = _exchange_phases(parts, got, *refs[2 * n + 3:])
        pl.when(jnp.logical_and(i == 0, k == 0))(start)
        part = _dot(a_ref[...], b_ref[...], NN)

        @pl.when(k == 0)
        def _():
            acc_ref[...] = part

        @pl.when(jnp.logical_and(k > 0, k < nk - 1))
        def _():
            acc_ref[...] += part

        @pl.when(k == nk - 1)
        def _():
            d1, dg = _rms_bwd(x_ref[...], g_ref[...], acc_ref[...] + part)
            gx_ref[...] = dx2_ref[...] + d1
            _acc(dg_ref, i == 0, dg)

        pl.when(jnp.logical_and(i == nm - 1, k == nk - 1))(finish)

    assert nk >= 2
    rows = lambda: pl.BlockSpec((tm, N), lambda i, k: (i, 0))
    return pl.pallas_call(
        body, name="mm_in_dx", grid=(nm, nk),
        in_specs=[pl.BlockSpec((tm, tk), lambda i, k: (i, k)), pl.BlockSpec((tk, N), lambda i, k: (k, 0)),
                  rows(), rows(), pl.BlockSpec((1, N), lambda i, k: (0, 0))] + [ANY_SPEC] * n,
        out_specs=[rows(), pl.BlockSpec((1, N), lambda i, k: (0, 0))] + [ANY_SPEC] * n,
        out_shape=[jax.ShapeDtypeStruct((L, N), F32), jax.ShapeDtypeStruct((1, N), F32)]
        + [jax.ShapeDtypeStruct(p.shape, p.dtype) for p in exchange],
        scratch_shapes=[pltpu.VMEM((tm, N), F32)] + _comm_sems(n),
        compiler_params=_params("arbitrary", "arbitrary", vmem=VMEM_BIG))(dproj, w_in_pt, x, dx2, g_pre, *exchange)


TN_CHUNK = 512


def _mm_tn(a, b, name, tm=512, tk=1024, exchange=()):
    K, M = a.shape
    K2, N = b.shape
    assert K == K2, (a.shape, b.shape)
    tm, tk, cn = _fit(M, tm), _fit(K, tk), _fit(N, TN_CHUNK)
    nm, nk = M // tm, K // tk
    n = len(exchange)

    def body(a_ref, b_ref, *refs):
        o_ref, acc_ref = refs[n], refs[2 * n + 1]
        i, k = pl.program_id(0), pl.program_id(1)
        if n:
            start, finish = _exchange_phases(refs[:n], refs[n + 1:2 * n + 1], *refs[2 * n + 2:])
            pl.when(jnp.logical_and(i == 0, k == 0))(start)

        @pl.when(k == 0)
        def _():
            acc_ref[...] = jnp.zeros((tm, N), F32)

        at = a_ref[...].astype(BF16).T
        for c in range(N // cn):
            cols = slice(c * cn, (c + 1) * cn)
            acc_ref[:, cols] += _dot(at, b_ref[:, cols].astype(BF16), NN)

        @pl.when(k == nk - 1)
        def _():
            o_ref[...] = acc_ref[...].astype(BF16)

        if n:
            pl.when(jnp.logical_and(i == nm - 1, k == nk - 1))(finish)

    out = pl.pallas_call(
        body, name=name, grid=(nm, nk),
        in_specs=[pl.BlockSpec((tk, tm), lambda i, k: (k, i)), pl.BlockSpec((tk, N), lambda i, k: (k, 0))] + [ANY_SPEC] * n,
        out_specs=[pl.BlockSpec((tm, N), lambda i, k: (i, 0))] + [ANY_SPEC] * n,
        out_shape=[jax.ShapeDtypeStruct((M, N), BF16)] + [jax.ShapeDtypeStruct(p.shape, p.dtype) for p in exchange],
        scratch_shapes=[pltpu.VMEM((tm, N), F32)] + (_comm_sems(n) if n else []),
        compiler_params=_params("arbitrary" if n else "parallel", "arbitrary", vmem=VMEM_BIG))(a, b, *exchange)
    return out if n else out[0]


def _row(tl, n, col=0):
    return pl.BlockSpec((tl, n), lambda i: (i, col))


def _full(shape):
    return pl.BlockSpec(shape, lambda i: (0,) * len(shape))


def _rms(x, g):
    r = lax.rsqrt(jnp.mean(x * x, axis=-1, keepdims=True) + EPS)
    return x * r * g


def _rms_bwd(x, g, dy):
    n = x.shape[-1]
    r = lax.rsqrt(jnp.mean(x * x, axis=-1, keepdims=True) + EPS)
    gy = dy * g
    dx = r * gy - x * (r * r * r * (1.0 / n)) * jnp.sum(x * gy, axis=-1, keepdims=True)
    return dx, jnp.sum(dy * x * r, axis=0, keepdims=True)


def _acc(ref, first, val):
    @pl.when(first)
    def _():
        ref[...] = val

    @pl.when(jnp.logical_not(first))
    def _():
        ref[...] += val


def _rms_fwd_call(x, g, name, gather):
    L, n = x.shape
    tl = _fit(L, 512)
    steps, na = L // tl, len(gather)

    def body(x_ref, g_ref, *refs):
        o_ref = refs[na]
        start, forward, finish = _gather_phases(refs[:na], refs[na + 1:2 * na + 1], *refs[2 * na + 1:])
        i = pl.program_id(0)
        pl.when(i == 0)(start)
        pl.when(i == steps // 2)(forward)
        o_ref[...] = _rms(x_ref[...], g_ref[...]).astype(BF16)
        pl.when(i == steps - 1)(finish)

    return pl.pallas_call(
        body, name=name, grid=(steps,), in_specs=[_row(tl, n), _full((1, n))] + [ANY_SPEC] * na,
        out_specs=[_row(tl, n)] + [ANY_SPEC] * na,
        out_shape=[jax.ShapeDtypeStruct((L, n), BF16)] + [jax.ShapeDtypeStruct((N_DEV,) + b.shape, b.dtype) for b in gather],
        scratch_shapes=_comm_sems(na), compiler_params=_params("arbitrary"))(x, g, *gather)


def _rope_lanes(shape):
    lane = lax.broadcasted_iota(jnp.int32, shape, 1)
    return lane, jnp.logical_and(lane >= QK_NOPE, lane < QK_HEAD)


def _rope_apply(x, cosf, sinf, lane):
    rot = jnp.where(lane < QK_NOPE + QK_ROPE // 2, -pltpu.roll(x, LANES - QK_ROPE // 2, 1), pltpu.roll(x, QK_ROPE // 2, 1))
    return x * cosf + rot * sinf


def _rope_apply_t(dy, cosf, sinf, lane, is_rope):
    g = dy * sinf
    rot_t = jnp.where(lane < QK_NOPE + QK_ROPE // 2, pltpu.roll(g, LANES - QK_ROPE // 2, 1), -pltpu.roll(g, QK_ROPE // 2, 1))
    return dy * cosf + jnp.where(is_rope, rot_t, 0.0)


def _mla_proj_call(proj, q_norm, kv_norm, w_uq_pt, w_kv_pt, pos_col, inv_freq):
    L = proj.shape[0]
    tl = _fit(L, 512)

    def body(p_ref, gq_ref, gk_ref, wq_ref, wkv_ref, pos_ref, f_ref, qn_ref, kn_ref, qo_ref, kvo_ref, cos_ref, sin_ref):
        qn = _rms(p_ref[:, P_CQ:P_CKV], gq_ref[...]).astype(BF16)
        kn = _rms(p_ref[:, P_CKV:P_KR], gk_ref[...]).astype(BF16)
        qn_ref[...] = qn
        kn_ref[...] = kn
        q_pad = _dot(qn, wq_ref[...], NT)
        kv_pad = _dot(kn, wkv_ref[...], NT)
        lane, is_rope = _rope_lanes((tl, LANES))
        ang = pos_ref[...] * f_ref[...]
        cosf = jnp.where(is_rope, jnp.cos(ang), jnp.where(lane < QK_NOPE, 1.0, 0.0))
        sinf = jnp.where(is_rope, jnp.sin(ang), 0.0)
        cos_ref[...] = cosf
        sin_ref[...] = sinf
        kr = _rope_apply(p_ref[:, P_KR:P_U], cosf, sinf, lane)
        for h in range(N_HEADS):
            qh = _rope_apply(q_pad[:, h * LANES:(h + 1) * LANES], cosf, sinf, lane)
            qo_ref[:, h * LANES:(h + 1) * LANES] = (qh * Q_PRESCALE).astype(BF16)
            kvo_ref[:, 2 * h * LANES:(2 * h + 1) * LANES] = (kv_pad[:, 2 * h * LANES:(2 * h + 1) * LANES] + kr).astype(BF16)
            vh = jnp.where(lane == V_HEAD, 1.0, kv_pad[:, (2 * h + 1) * LANES:(2 * h + 2) * LANES])
            kvo_ref[:, (2 * h + 1) * LANES:(2 * h + 2) * LANES] = vh.astype(BF16)

    shape = lambda n, dt: jax.ShapeDtypeStruct((L, n), dt)
    return pl.pallas_call(
        body, name="mla_proj", grid=(L // tl,),
        in_specs=[_row(tl, P_U), _full((1, Q_RANK)), _full((1, KV_RANK)), _full((HEAD_PAD, Q_RANK)),
                  _full((2 * HEAD_PAD, KV_RANK)), _row(tl, 1), _full((1, LANES))],
        out_specs=[_row(tl, Q_RANK), _row(tl, KV_RANK), _row(tl, HEAD_PAD), _row(tl, 2 * HEAD_PAD), _row(tl, LANES), _row(tl, LANES)],
        out_shape=[shape(Q_RANK, BF16), shape(KV_RANK, BF16), shape(HEAD_PAD, BF16), shape(2 * HEAD_PAD, BF16),
                   shape(LANES, F32), shape(LANES, F32)],
        compiler_params=_params("parallel"))(proj, q_norm, kv_norm, w_uq_pt, w_kv_pt, pos_col, inv_freq)


def _mla_proj_bwd_call(dq, dkv, cosf, sinf, proj, q_norm, kv_norm, w_uq_pt, w_kv_pt):
    L = dq.shape[0]
    tl = _fit(L, 512)

    def body(dq_ref, dkv_ref, cos_ref, sin_ref, p_ref, gq_ref, gk_ref, wq_ref, wkv_ref,
             dqo_ref, dkvo_ref, d_ref, dgq_ref, dgk_ref):
        first = pl.program_id(0) == 0
        lane, is_rope = _rope_lanes((tl, LANES))
        cosf, sinf = cos_ref[...], sin_ref[...]
        dk_sum = jnp.zeros((tl, LANES), F32)
        for h in range(N_HEADS):
            dqo_ref[:, h * LANES:(h + 1) * LANES] = _rope_apply_t(dq_ref[:, h * LANES:(h + 1) * LANES], cosf, sinf, lane, is_rope).astype(BF16)
            dk_sum = dk_sum + dkv_ref[:, 2 * h * LANES:(2 * h + 1) * LANES]
        dkvo_ref[...] = dkv_ref[...].astype(BF16)
        dqn = _dot(dqo_ref[...], wq_ref[...], NN)
        dkn = _dot(dkvo_ref[...], wkv_ref[...], NN)
        dcq, dgq = _rms_bwd(p_ref[:, P_CQ:P_CKV], gq_ref[...], dqn)
        dckv, dgk = _rms_bwd(p_ref[:, P_CKV:P_KR], gk_ref[...], dkn)
        d_ref[:, P_CQ:P_CKV] = dcq.astype(BF16)
        d_ref[:, P_CKV:P_KR] = dckv.astype(BF16)
        d_ref[:, P_KR:P_U] = _rope_apply_t(dk_sum, cosf, sinf, lane, is_rope).astype(BF16)
        _acc(dgq_ref, first, dgq)
        _acc(dgk_ref, first, dgk)

    shape = lambda n: jax.ShapeDtypeStruct((L, n), BF16)
    return pl.pallas_call(
        body, name="mla_proj_bwd", grid=(L // tl,),
        in_specs=[_row(tl, HEAD_PAD), _row(tl, 2 * HEAD_PAD), _row(tl, LANES), _row(tl, LANES), _row(tl, P_KR),
                  _full((1, Q_RANK)), _full((1, KV_RANK)), _full((HEAD_PAD, Q_RANK)), _full((2 * HEAD_PAD, KV_RANK))],
        out_specs=[_row(tl, HEAD_PAD), _row(tl, 2 * HEAD_PAD), _row(tl, P_U), _full((1, Q_RANK)), _full((1, KV_RANK))],
        out_shape=[shape(HEAD_PAD), shape(2 * HEAD_PAD), shape(P_U), jax.ShapeDtypeStruct((1, Q_RANK), F32),
                   jax.ShapeDtypeStruct((1, KV_RANK), F32)],
        compiler_params=_params("arbitrary"))(dq, dkv, cosf, sinf, proj, q_norm, kv_norm, w_uq_pt, w_kv_pt)


GATE_TILE = 256
GATE_ROWS = 1024


def _merge_call(proj, b_gate, pa, ps):
    L = proj.shape[0]
    tl = _fit(L, GATE_ROWS)
    nc = D_MODEL // GATE_TILE
    g0, g1 = P_GATE // GATE_TILE, (P_GATE + D_MODEL) // GATE_TILE

    def body(l0_ref, l1_ref, b0_ref, b1_ref, pa_ref, ps_ref, o_ref):
        s0 = _sigmoid(l0_ref[...] + b0_ref[...])
        s1 = _sigmoid(l1_ref[...] + b1_ref[...])
        o_ref[...] = (s0 * pa_ref[...] + s1 * ps_ref[...]).astype(BF16)

    blk = lambda off: pl.BlockSpec((tl, GATE_TILE), lambda i, j: (i, off + j))
    bias = lambda off: pl.BlockSpec((1, GATE_TILE), lambda i, j: (0, off + j))
    return pl.pallas_call(
        body, name="merge", grid=(L // tl, nc),
        in_specs=[blk(g0), blk(g1), bias(0), bias(nc), blk(0), blk(0)],
        out_specs=blk(0), out_shape=jax.ShapeDtypeStruct((L, D_MODEL), BF16),
        compiler_params=_params("parallel", "parallel"))(proj, proj, b_gate, b_gate, pa, ps)


def _merge_bwd_call(dm, proj, b_gate, pa, ps):
    L = proj.shape[0]
    tl = _fit(L, GATE_ROWS)
    nc = D_MODEL // GATE_TILE
    g0, g1 = P_GATE // GATE_TILE, (P_GATE + D_MODEL) // GATE_TILE

    def body(dm_ref, l0_ref, l1_ref, b0_ref, b1_ref, pa_ref, ps_ref, dpa_ref, dps_ref, dl0_ref, dl1_ref, db0_ref, db1_ref):
        first = pl.program_id(1) == 0
        dm_ = dm_ref[...]
        s0 = _sigmoid(l0_ref[...] + b0_ref[...])
        s1 = _sigmoid(l1_ref[...] + b1_ref[...])
        dpa_ref[...] = (dm_ * s0).astype(BF16)
        dps_ref[...] = (dm_ * s1).astype(BF16)
        dl0 = dm_ * pa_ref[...] * s0 * (1.0 - s0)
        dl1 = dm_ * ps_ref[...] * s1 * (1.0 - s1)
        dl0_ref[...] = dl0.astype(BF16)
        dl1_ref[...] = dl1.astype(BF16)
        _acc(db0_ref, first, jnp.sum(dl0, axis=0, keepdims=True))
        _acc(db1_ref, first, jnp.sum(dl1, axis=0, keepdims=True))

    blk = lambda off: pl.BlockSpec((tl, GATE_TILE), lambda j, i: (i, off + j))
    bias = lambda off: pl.BlockSpec((1, GATE_TILE), lambda j, i: (0, off + j))
    act = jax.ShapeDtypeStruct((L, D_MODEL), BF16)
    vec = jax.ShapeDtypeStruct((1, D_MODEL), F32)
    return pl.pallas_call(
        body, name="merge_bwd", grid=(nc, L // tl),
        in_specs=[blk(0), blk(g0), blk(g1), bias(0), bias(nc), blk(0), blk(0)],
        out_specs=[blk(0), blk(0), blk(0), blk(0), bias(0), bias(0)],
        out_shape=[act, act, act, act, vec, vec],
        compiler_params=_params("parallel", "arbitrary"))(dm, proj, proj, b_gate, b_gate, pa, ps)


def _post_mix_rows(o, x, g_post, g_fpre):
    x2 = x + _rms(o, g_post)
    return o, x2, _rms(x2, g_fpre)


def _ffn_out_rows(ff, x2, target, g_fpost):
    n = ff.shape[-1]
    err = x2 + _rms(ff, g_fpost) - target
    part = 0.5 * jnp.sum(jnp.sum(err * err, axis=-1, keepdims=True) * (1.0 / n), axis=0, keepdims=True)
    dy = err * (1.0 / n)
    dff, dg = _rms_bwd(ff, g_fpost, dy)
    return dy, dff, jnp.broadcast_to(part, (1, LANES)), dg


def _post_bwd_rows(dhn2, x2, dy, o, g_fpre, g_post):
    d1, dgf = _rms_bwd(x2, g_fpre, dhn2)
    dx2 = dy + d1
    do, dgp = _rms_bwd(o, g_post, dx2)
    return dx2, do, dgf, dgp


CONV_TILE = 256
CONV_WIDE = 1408
HALO = 16


def _conv3(w, b, x0, x1, x2):
    return b + w[2:3] * x0 + w[1:2] * x1 + w[0:1] * x2


def _down(x, by):
    return pltpu.roll(x, by, 0)


def _edge_down(edge, before, by):
    r = lax.broadcasted_iota(jnp.int32, edge.shape, 0)
    return jnp.where(r < by, pltpu.roll(before, by, 0), pltpu.roll(edge, by, 0))


def _edge_up(edge, after, by):
    r = lax.broadcasted_iota(jnp.int32, edge.shape, 0)
    return jnp.where(r >= HALO - by, pltpu.roll(after, HALO - by, 0), pltpu.roll(edge, HALO - by, 0))


def _gated(w_g, b_g, w_v, b_v, hg, hv, g1, g2, v1, v2):
    return _conv3(w_g, b_g, hg, g1, g2), _conv3(w_v, b_v, hv, v1, v2)


def _conv_specs(tl, tc, rows_inner):
    nh = tl // HALO
    if rows_inner:
        ij = lambda f: (lambda j, i: f(i, j))
    else:
        ij = lambda f: f
    cur = lambda off: pl.BlockSpec((tl, tc), ij(lambda i, j: (i, off + j)))
    prev = lambda off: pl.BlockSpec((HALO, tc), ij(lambda i, j: (jnp.maximum(i * nh - 1, 0), off + j)))
    par = lambda rows, off: pl.BlockSpec((rows, tc), ij(lambda i, j: (0, off + j)))
    return cur, prev, par


def _conv_act_call(h, conv_w, conv_b):
    L = h.shape[0]
    tl = _fit(L, 256)
    nc = D_FF // CONV_WIDE
    cur, prev, par = _conv_specs(tl, CONV_WIDE, False)

    def body(hg_ref, hv_ref, pg_ref, pv_ref, wg_ref, wv_ref, bg_ref, bv_ref, a_ref):
        not_first = (pl.program_id(0) > 0).astype(F32)
        par = (wg_ref[...], bg_ref[...], wv_ref[...], bv_ref[...])
        hg, hv = hg_ref[...], hv_ref[...]
        gate, val = _gated(*par, hg, hv, _down(hg, 1), _down(hg, 2), _down(hv, 1), _down(hv, 2))
        a_ref[...] = (_gelu(gate) * val).astype(BF16)
        eg, ev, bg, bv = hg[:HALO], hv[:HALO], pg_ref[...] * not_first, pv_ref[...] * not_first
        gate, val = _gated(*par, eg, ev, _edge_down(eg, bg, 1), _edge_down(eg, bg, 2),
                           _edge_down(ev, bv, 1), _edge_down(ev, bv, 2))
        a_ref[:HALO, :] = (_gelu(gate) * val).astype(BF16)

    return pl.pallas_call(
        body, name="conv_act", grid=(L // tl, nc),
        in_specs=[cur(0), cur(nc), prev(0), prev(nc), par(3, 0), par(3, nc), par(1, 0), par(1, nc)],
        out_specs=cur(0), out_shape=jax.ShapeDtypeStruct((L, D_FF), BF16),
        compiler_params=_params("parallel", "parallel"))(h, h, h, h, conv_w, conv_w, conv_b, conv_b)


def _conv_act_bwd_call(da, h, conv_w, conv_b):
    L = h.shape[0]
    tl = _fit(L, 512)
    nc = D_FF // CONV_TILE
    cur, prev, par = _conv_specs(tl, CONV_TILE, True)

    def body(da_ref, hg_ref, hv_ref, pg_ref, pv_ref, wg_ref, wv_ref, bg_ref, bv_ref,
             dg_ref, dv_ref, dwg_ref, dwv_ref, dbg_ref, dbv_ref):
        first = pl.program_id(1) == 0
        not_first = (pl.program_id(1) > 0).astype(F32)
        par = (wg_ref[...], bg_ref[...], wv_ref[...], bv_ref[...])
        col = lambda t: jnp.sum(t, axis=0, keepdims=True)

        def grads(da_, hg, hv, g1, g2, v1, v2):
            gate, val = _gated(*par, hg, hv, g1, g2, v1, v2)
            act, slope = _gelu_and_grad(gate)
            dgate = da_ * val * slope
            dval = da_ * act
            sums = (jnp.concatenate([col(dgate * g2), col(dgate * g1), col(dgate * hg)], axis=0),
                    jnp.concatenate([col(dval * v2), col(dval * v1), col(dval * hv)], axis=0), col(dgate), col(dval))
            return dgate, dval, sums

        da_, hg, hv = da_ref[...], hg_ref[...], hv_ref[...]
        shifted = (_down(hg, 1), _down(hg, 2), _down(hv, 1), _down(hv, 2))
        dgate, dval, whole = grads(da_, hg, hv, *shifted)
        dg_ref[...] = dgate.astype(BF16)
        dv_ref[...] = dval.astype(BF16)
        edge = lambda t: t[:HALO]
        _, _, wrapped = grads(edge(da_), edge(hg), edge(hv), *[edge(s) for s in shifted])
        eg, ev, bg, bv = edge(hg), edge(hv), pg_ref[...] * not_first, pv_ref[...] * not_first
        dgate, dval, fixed = grads(edge(da_), eg, ev, _edge_down(eg, bg, 1), _edge_down(eg, bg, 2),
                                   _edge_down(ev, bv, 1), _edge_down(ev, bv, 2))
        dg_ref[:HALO, :] = dgate.astype(BF16)
        dv_ref[:HALO, :] = dval.astype(BF16)
        for ref, a, b, c in zip((dwg_ref, dwv_ref, dbg_ref, dbv_ref), whole, wrapped, fixed):
            _acc(ref, first, a - b + c)

    act = jax.ShapeDtypeStruct((L, D_FF), BF16)
    w3 = jax.ShapeDtypeStruct((3, D_FF), F32)
    w1 = jax.ShapeDtypeStruct((1, D_FF), F32)
    return pl.pallas_call(
        body, name="conv_act_bwd", grid=(nc, L // tl),
        in_specs=[cur(0), cur(0), cur(nc), prev(0), prev(nc), par(3, 0), par(3, nc), par(1, 0), par(1, nc)],
        out_specs=[cur(0), cur(0), par(3, 0), par(3, 0), par(1, 0), par(1, 0)],
        out_shape=[act, act, w3, w3, w1, w1],
        compiler_params=_params("parallel", "arbitrary"))(da, h, h, h, h, conv_w, conv_w, conv_b, conv_b)


def _conv_t_call(dgate, dval, conv_w):
    L = dgate.shape[0]
    tl = _fit(L, 512)
    nc = D_FF // CONV_WIDE
    nh = tl // HALO

    def body(dg_ref, dv_ref, ng_ref, nv_ref, w_ref, o_ref):
        not_last = (pl.program_id(0) < L // tl - 1).astype(F32)

        def emit(d_ref, n_ref):
            c = d_ref[...].astype(F32)
            w = w_ref[...]
            o_ref[...] = _conv3(w, 0.0, c, pltpu.roll(c, tl - 1, 0), pltpu.roll(c, tl - 2, 0)).astype(BF16)
            edge, after = c[tl - HALO:], n_ref[...].astype(F32) * not_last
            o_ref[tl - HALO:, :] = _conv3(w, 0.0, edge, _edge_up(edge, after, 1), _edge_up(edge, after, 2)).astype(BF16)

        pl.when(pl.program_id(1) < nc)(lambda: emit(dg_ref, ng_ref))
        pl.when(pl.program_id(1) >= nc)(lambda: emit(dv_ref, nv_ref))

    gate_col = lambda j: jnp.minimum(j, nc - 1)
    val_col = lambda j: jnp.maximum(j - nc, 0)
    after_row = lambda i: jnp.minimum((i + 1) * nh, L // HALO - 1)
    tile = lambda col: pl.BlockSpec((tl, CONV_WIDE), lambda i, j: (i, col(j)))
    after = lambda col: pl.BlockSpec((HALO, CONV_WIDE), lambda i, j: (after_row(i), col(j)))
    return pl.pallas_call(
        body, name="conv_t", grid=(L // tl, 2 * nc),
        in_specs=[tile(gate_col), tile(val_col), after(gate_col), after(val_col), pl.BlockSpec((3, CONV_WIDE), lambda i, j: (0, j))],
        out_specs=pl.BlockSpec((tl, CONV_WIDE), lambda i, j: (i, j)),
        out_shape=jax.ShapeDtypeStruct((L, 2 * D_FF), BF16),
        compiler_params=_params("parallel", "parallel"))(dgate, dval, dgate, dval, conv_w)


def _glu_call(y1, w_glu, b_glu):
    L, n = y1.shape
    tl = _fit(L, 512)

    def body(y_ref, w_ref, b_ref, o_ref):
        y2 = _gelu(y_ref[...])
        z = _dot(y2.astype(BF16), w_ref[...], NN) + b_ref[...]
        o_ref[...] = (y2 * _sigmoid(z)).astype(BF16)

    return pl.pallas_call(
        body, name="glu", grid=(L // tl,), in_specs=[_row(tl, n), _full((n, n)), _full((1, n))],
        out_specs=_row(tl, n), out_shape=jax.ShapeDtypeStruct((L, n), BF16),
        compiler_params=_params("parallel"))(y1, w_glu, b_glu)


def _glu_bwd_call(dout, y1, w_glu, b_glu):
    L, n = y1.shape
    tl = _fit(L, 512)

    def body(do_ref, y_ref, w_ref, b_ref, dy_ref, dw_ref, db_ref):
        first = pl.program_id(0) == 0
        y1_ = y_ref[...]
        y2, slope = _gelu_and_grad(y1_)
        y2b = y2.astype(BF16)
        w = w_ref[...]
        sg = _sigmoid(_dot(y2b, w, NN) + b_ref[...])
        dout_ = do_ref[...].astype(F32)
        dz = dout_ * y2 * sg * (1.0 - sg)
        dzb = dz.astype(BF16)
        dy2 = dout_ * sg + _dot(dzb, w, NT)
        dy_ref[...] = dy2 * slope
        _acc(dw_ref, first, _dot(y2b, dzb, TN))
        _acc(db_ref, first, jnp.sum(dz, axis=0, keepdims=True))

    return pl.pallas_call(
        body, name="glu_bwd", grid=(L // tl,),
        in_specs=[_row(tl, n), _row(tl, n), _full((n, n)), _full((1, n))],
        out_specs=[_row(tl, n), _full((n, n)), _full((1, n))],
        out_shape=[jax.ShapeDtypeStruct((L, n), F32), jax.ShapeDtypeStruct((n, n), F32), jax.ShapeDtypeStruct((1, n), F32)],
        compiler_params=_params("arbitrary"))(dout, y1, w_glu, b_glu)


ATTN_TILE = 1024
ATTN_SCALE = 1.0 / math.sqrt(QK_HEAD)


ATTN_HEADS = 2
ATTN_GROUPS = N_HEADS // ATTN_HEADS
LOG2E = 1.0 / math.log(2.0)
Q_PRESCALE = ATTN_SCALE * LOG2E
ANY_SPEC = pl.BlockSpec(memory_space=pl.ANY)


def _attn_fwd_call(q, kv, blocks):
    L = q.shape[0]
    t = _fit(L, ATTN_TILE)
    nq = L // t
    n = len(blocks)

    def body(q_ref, kv_ref, *refs):
        blk_refs, (o_ref, lse_ref), gat_refs = refs[:n], refs[n:n + 2], refs[n + 2:2 * n + 2]
        m_s, acc_s, send_sems, recv_sems, local_sems = refs[2 * n + 2:]
        g, i = pl.program_id(0), pl.program_id(1)
        start, forward, finish = _gather_phases(blk_refs, gat_refs, send_sems, recv_sems, local_sems)
        pl.when(jnp.logical_and(g == 0, i == 0))(start)
        m_s[...] = jnp.full((ATTN_HEADS, t, 1), NEG, F32)
        acc_s[...] = jnp.zeros((ATTN_HEADS, t, LANES), F32)
        below = lax.broadcasted_iota(jnp.int32, (t, t), 1) <= lax.broadcasted_iota(jnp.int32, (t, t), 0)

        def block_step(kb, on_diagonal):
            rows = pl.ds(pl.multiple_of(kb * t, t), t)
            for a in range(ATTN_HEADS):
                s = _dot(q_ref[:, a * LANES:(a + 1) * LANES], kv_ref[rows, 2 * a * LANES:(2 * a + 1) * LANES], NT)
                if on_diagonal:
                    s = jnp.where(below, s, NEG)
                m_prev = m_s[a]
                m_new = jnp.maximum(m_prev, jnp.max(s, axis=1, keepdims=True))
                p = jnp.exp2(s - m_new)
                pv = _dot(p.astype(BF16), kv_ref[rows, (2 * a + 1) * LANES:(2 * a + 2) * LANES], NN)
                acc_s[a] = jnp.exp2(m_prev - m_new) * acc_s[a] + pv
                m_s[a] = m_new

        def step(kb, carry):
            block_step(kb, False)
            return carry

        lax.fori_loop(0, i, step, 0)
        block_step(i, True)
        lane = lax.broadcasted_iota(jnp.int32, (t, LANES), 1)
        for a in range(ATTN_HEADS):
            acc = acc_s[a]
            l = jnp.sum(jnp.where(lane == V_HEAD, acc, 0.0), axis=1, keepdims=True)
            o_ref[:, a * LANES:(a + 1) * LANES] = (acc / l).astype(BF16)
            lse_ref[a] = m_s[a] + jnp.log(l) * LOG2E
        pl.when(jnp.logical_and(g == (3 * ATTN_GROUPS) // 4, i == 0))(forward)
        pl.when(jnp.logical_and(g == ATTN_GROUPS - 1, i == nq - 1))(finish)

    gw = ATTN_HEADS * LANES
    return pl.pallas_call(
        body, name="attn_fwd", grid=(ATTN_GROUPS, nq),
        in_specs=[pl.BlockSpec((t, gw), lambda g, i: (i, g)),
                  pl.BlockSpec((L, 2 * gw), lambda g, i: (0, g))] + [ANY_SPEC] * n,
        out_specs=[pl.BlockSpec((t, gw), lambda g, i: (i, g)),
                   pl.BlockSpec((ATTN_HEADS, t, 1), lambda g, i: (g, i, 0))] + [ANY_SPEC] * n,
        out_shape=[jax.ShapeDtypeStruct((L, HEAD_PAD), BF16), jax.ShapeDtypeStruct((N_HEADS, L, 1), F32)]
        + [jax.ShapeDtypeStruct((N_DEV,) + b.shape, b.dtype) for b in blocks],
        scratch_shapes=[pltpu.VMEM((ATTN_HEADS, t, 1), F32), pltpu.VMEM((ATTN_HEADS, t, LANES), F32)] + _comm_sems(n),
        compiler_params=_params("arbitrary", "arbitrary", vmem=VMEM_BIG))(q, kv, *blocks)


def _attn_bwd_call(q, kv, o, do, lse, parts, blocks):
    L = q.shape[0]
    t = _fit(L, ATTN_TILE)
    nq = L // t
    n1, n = len(parts), len(parts) + len(blocks)

    def body(q_ref, do_ref, o_ref, lse_ref, kv_ref, *refs):
        in_refs, (dq_ref, dkv_ref), out_refs = refs[:n], refs[n:n + 2], refs[n + 2:2 * n + 2]
        dk_s, dv_s = refs[2 * n + 2:2 * n + 4]
        g, j = pl.program_id(0), pl.program_id(1)
        start, finish = _exchange_phases(in_refs[:n1], out_refs[:n1], *refs[2 * n + 4:2 * n + 7])
        start_blocks, finish_blocks = _exchange_phases(in_refs[n1:], out_refs[n1:], *refs[2 * n + 7:], same_source=True)

        @pl.when(jnp.logical_and(g == 0, j == 0))
        def _():
            start()
            start_blocks()

        @pl.when(j == 0)
        def _():
            dq_ref[...] = jnp.zeros((L, ATTN_HEADS * LANES), F32)

        dk_s[...] = jnp.zeros((ATTN_HEADS, t, LANES), F32)
        dv_s[...] = jnp.zeros((ATTN_HEADS, t, LANES), F32)
        below = lax.broadcasted_iota(jnp.int32, (t, t), 1) <= lax.broadcasted_iota(jnp.int32, (t, t), 0)

        def block_step(i, on_diagonal):
            rows = pl.ds(pl.multiple_of(i * t, t), t)
            for a in range(ATTN_HEADS):
                lanes = slice(a * LANES, (a + 1) * LANES)
                qi = q_ref[rows, lanes]
                doi = do_ref[rows, lanes]
                kblk = kv_ref[:, 2 * a * LANES:(2 * a + 1) * LANES]
                delta = jnp.sum(doi.astype(F32) * o_ref[rows, lanes].astype(F32), axis=1, keepdims=True)
                s = _dot(qi, kblk, NT)
                if on_diagonal:
                    s = jnp.where(below, s, NEG)
                p = jnp.exp2(s - lse_ref[a, rows, :])
                dv_s[a] += _dot(p.astype(BF16), doi, TN)
                ds = (p * (_dot(doi, kv_ref[:, (2 * a + 1) * LANES:(2 * a + 2) * LANES], NT) - delta)).astype(BF16)
                dk_s[a] += _dot(ds, qi, TN)
                dq_ref[rows, lanes] += _dot(ds, kblk, NN) * ATTN_SCALE

        def step(i, carry):
            block_step(i, False)
            return carry

        block_step(j, True)
        lax.fori_loop(j + 1, nq, step, 0)
        for a in range(ATTN_HEADS):
            dkv_ref[:, 2 * a * LANES:(2 * a + 1) * LANES] = dk_s[a] * (1.0 / LOG2E)
            dkv_ref[:, (2 * a + 1) * LANES:(2 * a + 2) * LANES] = dv_s[a]
        @pl.when(jnp.logical_and(g == ATTN_GROUPS - 1, j == nq - 1))
        def _():
            finish()
            finish_blocks()

    gw = ATTN_HEADS * LANES
    whole = lambda: pl.BlockSpec((L, gw), lambda g, j: (0, g))
    acc = pltpu.VMEM((ATTN_HEADS, t, LANES), F32)
    return pl.pallas_call(
        body, name="attn_bwd", grid=(ATTN_GROUPS, nq),
        in_specs=[whole(), whole(), whole(), pl.BlockSpec((ATTN_HEADS, L, 1), lambda g, j: (g, 0, 0)),
                  pl.BlockSpec((t, 2 * gw), lambda g, j: (j, g))] + [ANY_SPEC] * n,
        out_specs=[whole(), pl.BlockSpec((t, 2 * gw), lambda g, j: (j, g))] + [ANY_SPEC] * n,
        out_shape=[jax.ShapeDtypeStruct((L, HEAD_PAD), F32), jax.ShapeDtypeStruct((L, 2 * HEAD_PAD), F32)]
        + [jax.ShapeDtypeStruct(p.shape, p.dtype) for p in parts]
        + [jax.ShapeDtypeStruct((N_DEV,) + b.shape, b.dtype) for b in blocks],
        scratch_shapes=[acc, acc] + _comm_sems(n1) + _comm_sems(n - n1),
        compiler_params=_params("arbitrary", "arbitrary", vmem=VMEM_BIG))(q, do, o, lse, kv, *parts, *blocks)


def _disc(lr, li, ldt, br, bi):
    dt = jnp.exp(ldt)
    mag = jnp.exp(lr * dt)
    ang = li * dt
    a_re, a_im = mag * jnp.cos(ang), mag * jnp.sin(ang)
    den = lr * lr + li * li
    n_re, n_im = a_re - 1.0, a_im
    z_re = (n_re * lr + n_im * li) / den
    z_im = (n_im * lr - n_re * li) / den
    return a_re, a_im, z_re * br - z_im * bi, z_re * bi + z_im * br


def _disc_call(lr, li, ldt, br, bi):
    def body(lr_ref, li_ref, ldt_ref, br_ref, bi_ref, ar_ref, ai_ref, bbr_ref, bbi_ref):
        ar_ref[...], ai_ref[...], bbr_ref[...], bbi_ref[...] = _disc(
            lr_ref[...], li_ref[...], ldt_ref[...], br_ref[...], bi_ref[...])

    c1 = jax.ShapeDtypeStruct((SSM_NSTATE, 1), F32)
    c16 = jax.ShapeDtypeStruct((SSM_NSTATE, SSM_GROUP), F32)
    return pl.pallas_call(body, name="ssm_disc", out_shape=[c1, c1, c16, c16])(lr, li, ldt, br, bi)


def _disc_bwd_call(lr, li, ldt, br, bi, dar, dai, dbbr, dbbi):
    def body(lr_ref, li_ref, ldt_ref, br_ref, bi_ref, dar_ref, dai_ref, dbbr_ref, dbbi_ref,
             dlr_ref, dli_ref, dldt_ref, dbr_ref, dbi_ref):
        _, vjp = jax.vjp(_disc, lr_ref[...], li_ref[...], ldt_ref[...], br_ref[...], bi_ref[...])
        dlr_ref[...], dli_ref[...], dldt_ref[...], dbr_ref[...], dbi_ref[...] = vjp(
            (dar_ref[...], dai_ref[...], dbbr_ref[...], dbbi_ref[...]))

    c1 = jax.ShapeDtypeStruct((SSM_NSTATE, 1), F32)
    c16 = jax.ShapeDtypeStruct((SSM_NSTATE, SSM_GROUP), F32)
    return pl.pallas_call(body, name="ssm_disc_bwd", out_shape=[c1, c1, c1, c16, c16])(
        lr, li, ldt, br, bi, dar, dai, dbbr, dbbi)


SSM_ROWS = 1024
SSM_CW = SSM_NSTATE // SSM_CHUNKS
SSM_CU = SSM_WIDTH // SSM_CHUNKS


def _cmul(ar, ai, br, bi):
    return ar * br - ai * bi, ar * bi + ai * br


def _power(ar1, ai1, n):
    res, base = None, (ar1, ai1)
    while n:
        if n & 1:
            res = base if res is None else _cmul(res[0], res[1], base[0], base[1])
        n >>= 1
        if n:
            base = _cmul(base[0], base[1], base[0], base[1])
    return res


def _tile(k):
    return pl.ds(pl.multiple_of(k * 8, 8), 8)


def _ssm_fwd_call(u, a_re, a_im, bb_re, bb_im, cm_re, cm_im, d_skip):
    L = u.shape[0]
    seg = L // 8
    rb = _fit(L, SSM_ROWS)

    def body(u_ref, ar_ref, ai_ref, bbr_ref, bbi_ref, cmr_ref, cmi_ref, d_ref, y_ref, sre_hbm, sim_hbm,
             s_re, s_im, sems):
        q = pl.program_id(0)

        def bu_step(r, c):
            rows = pl.ds(pl.multiple_of(r * rb, rb), rb)
            ub = u_ref[rows, :].astype(BF16)
            s_re[rows, :] = _dot(ub, bbr_ref[0], NN)
            s_im[rows, :] = _dot(ub, bbi_ref[0], NN)
            return c

        lax.fori_loop(0, L // rb, bu_step, 0)
        ar1, ai1 = ar_ref[...], ai_ref[...]
        ar = jnp.broadcast_to(ar1, (8, SSM_CW))
        ai = jnp.broadcast_to(ai1, (8, SSM_CW))

        def local(k, c):
            nr, ni = _cmul(ar, ai, c[0], c[1])
            nr = nr + s_re[_tile(k), :]
            ni = ni + s_im[_tile(k), :]
            s_re[_tile(k), :] = nr
            s_im[_tile(k), :] = ni
            return nr, ni

        zero8 = jnp.zeros((8, SSM_CW), F32)
        lax.fori_loop(0, seg, local, (zero8, zero8))
        pr, pi = _power(ar1, ai1, seg)
        end_r = s_re[pl.ds((seg - 1) * 8, 8), :]
        end_i = s_im[pl.ds((seg - 1) * 8, 8), :]
        er = jnp.zeros((1, SSM_CW), F32)
        ei = jnp.zeros((1, SSM_CW), F32)
        rows_r, rows_i = [er], [ei]
        for j in range(7):
            tr, ti = _cmul(pr, pi, er, ei)
            er, ei = end_r[j:j + 1] + tr, end_i[j:j + 1] + ti
            rows_r.append(er)
            rows_i.append(ei)
        e_r = jnp.concatenate(rows_r, axis=0)
        e_i = jnp.concatenate(rows_i, axis=0)

        def fix(k, c):
            wr, wi = _cmul(c[0], c[1], ar, ai)
            fr, fi = _cmul(wr, wi, e_r, e_i)
            s_re[_tile(k), :] += fr
            s_im[_tile(k), :] += fi
            return wr, wi

        lax.fori_loop(0, seg, fix, (jnp.ones((8, SSM_CW), F32), zero8))
        out_r = pltpu.make_async_copy(s_re, sre_hbm.at[q], sems.at[0])
        out_i = pltpu.make_async_copy(s_im, sim_hbm.at[q], sems.at[1])
        out_r.start()
        out_i.start()

        def y_step(r, c):
            rows = pl.ds(pl.multiple_of(r * rb, rb), rb)
            y = _dot(s_re[rows, :].astype(BF16), cmr_ref[0], NN) - _dot(s_im[rows, :].astype(BF16), cmi_ref[0], NN)
            y_ref[rows, :] = y + d_ref[...] * u_ref[rows, :]
            return c

        lax.fori_loop(0, L // rb, y_step, 0)
        out_r.wait()
        out_i.wait()

    chunk = lambda rows, cols: pl.BlockSpec((rows, cols), lambda q: (0, q))
    mat = lambda r, c: pl.BlockSpec((1, r, c), lambda q: (q, 0, 0))
    anyspec = pl.BlockSpec(memory_space=pl.ANY)
    states = jax.ShapeDtypeStruct((SSM_CHUNKS, L, SSM_CW), F32)
    return pl.pallas_call(
        body, name="ssm_fwd", grid=(SSM_CHUNKS,),
        in_specs=[chunk(L, SSM_CU), chunk(1, SSM_CW), chunk(1, SSM_CW), mat(SSM_CU, SSM_CW), mat(SSM_CU, SSM_CW),
                  mat(SSM_CW, SSM_CU), mat(SSM_CW, SSM_CU), chunk(1, SSM_CU)],
        out_specs=[chunk(L, SSM_CU), anyspec, anyspec],
        out_shape=[jax.ShapeDtypeStruct((L, SSM_WIDTH), F32), states, states],
        scratch_shapes=[pltpu.VMEM((L, SSM_CW), F32), pltpu.VMEM((L, SSM_CW), F32), pltpu.SemaphoreType.DMA((2,))],
        compiler_params=_params("arbitrary", vmem=VMEM_BIG))(u, a_re, a_im, bb_re, bb_im, cm_re, cm_im, d_skip)


def _ssm_bwd_call(dy, u, s_re_all, s_im_all, a_re, a_im, bb_re, bb_im, cm_re, cm_im, d_skip):
    L = u.shape[0]
    seg = L // 8
    rb = _fit(L, SSM_ROWS)

    def body(dy_ref, u_ref, sre_hbm, sim_hbm, ar_ref, ai_ref, bbr_ref, bbi_ref, cmr_ref, cmi_ref, d_ref,
             du_ref, dbbr_ref, dbbi_ref, dcmr_ref, dcmi_ref, dar_ref, dai_ref, dd_ref,
             g_re, g_im, s_re, s_im, sems):
        q = pl.program_id(0)
        in_r = pltpu.make_async_copy(sre_hbm.at[q], s_re, sems.at[0])
        in_i = pltpu.make_async_copy(sim_hbm.at[q], s_im, sems.at[1])
        in_r.start()
        in_i.start()

        def ds_step(r, c):
            rows = pl.ds(pl.multiple_of(r * rb, rb), rb)
            dyb = dy_ref[rows, :].astype(BF16)
            g_re[rows, :] = _dot(dyb, cmr_ref[0], NT)
            g_im[rows, :] = -_dot(dyb, cmi_ref[0], NT)
            return c

        lax.fori_loop(0, L // rb, ds_step, 0)
        ar1, ai1 = ar_ref[...], ai_ref[...]
        ar = jnp.broadcast_to(ar1, (8, SSM_CW))
        nai = jnp.broadcast_to(-ai1, (8, SSM_CW))

        def local(kk, c):
            k = seg - 1 - kk
            nr, ni = _cmul(ar, nai, c[0], c[1])
            nr = nr + g_re[_tile(k), :]
            ni = ni + g_im[_tile(k), :]
            g_re[_tile(k), :] = nr
            g_im[_tile(k), :] = ni
            return nr, ni

        zero8 = jnp.zeros((8, SSM_CW), F32)
        lax.fori_loop(0, seg, local, (zero8, zero8))
        pr, pi = _power(ar1, -ai1, seg)
        head_r = g_re[pl.ds(0, 8), :]
        head_i = g_im[pl.ds(0, 8), :]
        fr = jnp.zeros((1, SSM_CW), F32)
        fi = jnp.zeros((1, SSM_CW), F32)
        rows_r, rows_i = [fr], [fi]
        for j in range(6, -1, -1):
            tr, ti = _cmul(pr, pi, fr, fi)
            fr, fi = head_r[j + 1:j + 2] + tr, head_i[j + 1:j + 2] + ti
            rows_r.insert(0, fr)
            rows_i.insert(0, fi)
        f_r = jnp.concatenate(rows_r, axis=0)
        f_i = jnp.concatenate(rows_i, axis=0)
        in_r.wait()
        in_i.wait()

        def fixed(k, wr, wi):
            xr, xi = _cmul(wr, wi, f_r, f_i)
            gr = g_re[_tile(k), :] + xr
            gi = g_im[_tile(k), :] + xi
            g_re[_tile(k), :] = gr
            g_im[_tile(k), :] = gi
            return gr, gi

        def fix(kk, c):
            k = seg - 1 - kk
            wr, wi = _cmul(c[0], c[1], ar, nai)
            gr, gi = fixed(k, wr, wi)
            pr_, pi_ = s_re[_tile(k - 1), :], s_im[_tile(k - 1), :]
            return wr, wi, c[2] + gr * pr_ + gi * pi_, c[3] + gi * pr_ - gr * pi_

        wr, wi, acc_r, acc_i = lax.fori_loop(0, seg - 1, fix, (jnp.ones((8, SSM_CW), F32), zero8, zero8, zero8))
        wr, wi = _cmul(wr, wi, ar, nai)
        gr, gi = fixed(0, wr, wi)
        row8 = lax.broadcasted_iota(jnp.int32, (8, SSM_CW), 0)
        pr_ = jnp.where(row8 > 0, pltpu.roll(s_re[pl.ds((seg - 1) * 8, 8), :], 1, 0), 0.0)
        pi_ = jnp.where(row8 > 0, pltpu.roll(s_im[pl.ds((seg - 1) * 8, 8), :], 1, 0), 0.0)
        acc_r = acc_r + gr * pr_ + gi * pi_
        acc_i = acc_i + gi * pr_ - gr * pi_
        dar_ref[...] = jnp.sum(acc_r, axis=0, keepdims=True)
        dai_ref[...] = jnp.sum(acc_i, axis=0, keepdims=True)

        dbbr_ref[...] = jnp.zeros((1, SSM_CU, SSM_CW), F32)
        dbbi_ref[...] = jnp.zeros((1, SSM_CU, SSM_CW), F32)
        dcmr_ref[...] = jnp.zeros((1, SSM_CW, SSM_CU), F32)
        dcmi_ref[...] = jnp.zeros((1, SSM_CW, SSM_CU), F32)
        dd_ref[...] = jnp.zeros((1, SSM_CU), F32)

        def grad_step(r, c):
            rows = pl.ds(pl.multiple_of(r * rb, rb), rb)
            ub, dyv = u_ref[rows, :], dy_ref[rows, :]
            ubb, dyb = ub.astype(BF16), dyv.astype(BF16)
            grb, gib = g_re[rows, :].astype(BF16), g_im[rows, :].astype(BF16)
            dbbr_ref[0] += _dot(ubb, grb, TN)
            dbbi_ref[0] += _dot(ubb, gib, TN)
            dcmr_ref[0] += _dot(s_re[rows, :].astype(BF16), dyb, TN)
            dcmi_ref[0] -= _dot(s_im[rows, :].astype(BF16), dyb, TN)
            du_ref[rows, :] = _dot(grb, bbr_ref[0], NT) + _dot(gib, bbi_ref[0], NT) + d_ref[...] * dyv
            dd_ref[...] += jnp.sum(dyv * ub, axis=0, keepdims=True)
            return c

        lax.fori_loop(0, L // rb, grad_step, 0)

    chunk = lambda rows, cols: pl.BlockSpec((rows, cols), lambda q: (0, q))
    mat = lambda r, c: pl.BlockSpec((1, r, c), lambda q: (q, 0, 0))
    anyspec = pl.BlockSpec(memory_space=pl.ANY)
    big = lambda: pltpu.VMEM((L, SSM_CW), F32)
    return pl.pallas_call(
        body, name="ssm_bwd", grid=(SSM_CHUNKS,),
        in_specs=[chunk(L, SSM_CU), chunk(L, SSM_CU), anyspec, anyspec, chunk(1, SSM_CW), chunk(1, SSM_CW),
                  mat(SSM_CU, SSM_CW), mat(SSM_CU, SSM_CW), mat(SSM_CW, SSM_CU), mat(SSM_CW, SSM_CU), chunk(1, SSM_CU)],
        out_specs=[chunk(L, SSM_CU), mat(SSM_CU, SSM_CW), mat(SSM_CU, SSM_CW), mat(SSM_CW, SSM_CU), mat(SSM_CW, SSM_CU),
                   chunk(1, SSM_CW), chunk(1, SSM_CW), chunk(1, SSM_CU)],
        out_shape=[jax.ShapeDtypeStruct((L, SSM_WIDTH), F32),
                   jax.ShapeDtypeStruct((SSM_CHUNKS, SSM_CU, SSM_CW), F32), jax.ShapeDtypeStruct((SSM_CHUNKS, SSM_CU, SSM_CW), F32),
                   jax.ShapeDtypeStruct((SSM_CHUNKS, SSM_CW, SSM_CU), F32), jax.ShapeDtypeStruct((SSM_CHUNKS, SSM_CW, SSM_CU), F32),
                   jax.ShapeDtypeStruct((1, SSM_NSTATE), F32), jax.ShapeDtypeStruct((1, SSM_NSTATE), F32),
                   jax.ShapeDtypeStruct((1, SSM_WIDTH), F32)],
        scratch_shapes=[big(), big(), big(), big(), pltpu.SemaphoreType.DMA((2,))],
        compiler_params=_params("arbitrary", vmem=VMEM_BIG))(
            dy, u, s_re_all, s_im_all, a_re, a_im, bb_re, bb_im, cm_re, cm_im, d_skip)


def _place():
    return lax.axis_index("x"), lax.axis_index("y"), lax.axis_index("c")


def _small_gather_call(blocks, name):
    n = len(blocks)

    def body(*refs):
        start, finish = _exchange_phases(refs[:n], refs[n:2 * n], *refs[2 * n:], same_source=True)
        start()
        finish()

    return pl.pallas_call(
        body, name=name, in_specs=[ANY_SPEC] * n, out_specs=[ANY_SPEC] * n,
        out_shape=[jax.ShapeDtypeStruct((N_DEV,) + b.shape, b.dtype) for b in blocks],
        scratch_shapes=_comm_sems(n))(*blocks)


def _comm_sems(n):
    return [pltpu.SemaphoreType.DMA((7 * n,)), pltpu.SemaphoreType.DMA((7 * n,)), pltpu.SemaphoreType.DMA((n,))]


def _gather_phases(x_refs, out_refs, send_sems, recv_sems, local_sems):
    x, y, c = _place()
    me, sibling = (x, y, c), (x, y, 1 - c)
    chips = [(1 - x, y), (x, 1 - y), (1 - x, 1 - y)]
    n = len(x_refs)

    def copy(k, a, blk, to, from_input=False):
        slot = out_refs[a].at[4 * blk[0] + 2 * blk[1] + blk[2]]
        return pltpu.make_async_remote_copy(
            src_ref=x_refs[a] if from_input else slot, dst_ref=slot,
            send_sem=send_sems.at[k * n + a], recv_sem=recv_sems.at[k * n + a], device_id=to, device_id_type=MESH_ID)

    mine = [pltpu.make_async_copy(x_refs[a], out_refs[a].at[4 * x + 2 * y + c], local_sems.at[a]) for a in range(n)]
    first, passed = [], []
    for a in range(n):
        first.append(copy(0, a, me, sibling, True))
        first += [copy(1 + j, a, me, (*chip, c), True) for j, chip in enumerate(chips)]
        passed += [copy(4 + j, a, (*chip, c), sibling) for j, chip in enumerate(chips)]

    def start():
        for cp in mine + first:
            cp.start()

    def forward():
        for j, chip in enumerate(chips):
            for a in range(n):
                copy(1 + j, a, (*chip, c), me).wait_recv()
                passed[3 * a + j].start()

    def finish():
        for a in range(n):
            copy(0, a, sibling, me).wait_recv()
            for j, chip in enumerate(chips):
                copy(4 + j, a, (*chip, 1 - c), me).wait_recv()
        for cp in first + passed:
            cp.wait_send()
        for cp in mine:
            cp.wait()

    return start, forward, finish


def _exchange_phases(p_refs, out_refs, send_sems, recv_sems, local_sems, same_source=False):
    x, y, c = _place()
    me = 4 * x + 2 * y + c
    n = len(p_refs)

    def flip(k):
        px = 1 - x if k & 4 else x
        py = 1 - y if k & 2 else y
        pc = 1 - c if k & 1 else c
        return (px, py, pc), 4 * px + 2 * py + pc

    def source(a, slot):
        return p_refs[a] if same_source else p_refs[a].at[slot]

    def copy(k, a, landing):
        peer, peer_slot = flip(k)
        return pltpu.make_async_remote_copy(
            src_ref=source(a, peer_slot), dst_ref=out_refs[a].at[peer_slot if landing else me],
            send_sem=send_sems.at[(k - 1) * n + a], recv_sem=recv_sems.at[(k - 1) * n + a],
            device_id=peer, device_id_type=MESH_ID)

    mine = [pltpu.make_async_copy(source(a, me), out_refs[a].at[me], local_sems.at[a]) for a in range(n)]
    sends = [copy(k, a, False) for k in range(1, N_DEV) for a in range(n)]

    def start():
        for cp in mine + sends:
            cp.start()

    def finish():
        for k in range(1, N_DEV):
            for a in range(n):
                copy(k, a, True).wait_recv()
        for cp in sends:
            cp.wait_send()
        for cp in mine:
            cp.wait()

    return start, finish


def _adam_math(g, w, m, v):
    c1 = 1.0 / (1.0 - ADAM_B1 ** ADAM_STEP)
    c2 = 1.0 / (1.0 - ADAM_B2 ** ADAM_STEP)
    m_new = ADAM_B1 * m + (1.0 - ADAM_B1) * g
    v_new = ADAM_B2 * v + (1.0 - ADAM_B2) * (g * g)
    delta = -ADAM_LR * ((m_new * c1) / (jnp.sqrt(v_new * c2) + ADAM_EPS) + ADAM_WD * w)
    return g, delta, m_new, v_new


def _sum_slices(s_ref):
    g = s_ref[0].astype(F32)
    for k in range(1, N_DEV):
        g = g + s_ref[k].astype(F32)
    return g


def _adam_call(slices, w, m, v, name):
    d1, rest = w.shape[1], w.shape[2:]
    zeros = (0,) * len(rest)
    by_lanes = len(rest) == 1 and d1 > 256 and d1 % 16 != 0
    if by_lanes:
        tile = _fit(rest[0], 256)
        steps = rest[0] // tile
        own = pl.BlockSpec((1, d1, tile), lambda i: (0, 0, i))
        sl = pl.BlockSpec((N_DEV, 1, d1, tile), lambda i: (0, 0, 0, i))
    else:
        tile = _fit(d1, 256, 16) if len(rest) == 1 else _fit(d1, 8, 8)
        steps = d1 // tile
        own = pl.BlockSpec((1, tile) + rest, lambda i: (0, i) + zeros)
        sl = pl.BlockSpec((N_DEV, 1, tile) + rest, lambda i: (0, 0, i) + zeros)

    def body(s_ref, w_ref, m_ref, v_ref, g_ref, d_ref, mo_ref, vo_ref):
        g_ref[...], d_ref[...], mo_ref[...], vo_ref[...] = _adam_math(_sum_slices(s_ref), w_ref[...], m_ref[...], v_ref[...])

    out = jax.ShapeDtypeStruct(w.shape, F32)
    return pl.pallas_call(
        body, name=name, grid=(steps,), in_specs=[sl, own, own, own],
        out_specs=[own, own, own, own], out_shape=[out, out, out, out],
        compiler_params=_params("parallel"))(slices, w, m, v)


def _adam_small_call(rows_all, row_params, slices, params):
    nr, n = len(row_params), len(row_params) + len(params)

    def row_sum(rows_ref, a, width):
        g = rows_ref[0, pl.ds(a, 1), pl.ds(0, width)]
        for k in range(1, N_DEV):
            g = g + rows_ref[k, pl.ds(a, 1), pl.ds(0, width)]
        return g

    def body(rows_ref, *refs):
        slice_refs, wmv, outs = refs[:n - nr], refs[n - nr:n - nr + 3 * n], refs[n - nr + 3 * n:]
        outs[4 * n][...] = row_sum(rows_ref, nr, LANES)
        for a in range(n):
            w_ref, m_ref, v_ref = wmv[3 * a:3 * a + 3]
            if a < nr:
                g = row_sum(rows_ref, a, w_ref.shape[1])
            else:
                g = _sum_slices(slice_refs[a - nr])
            res = _adam_math(g, w_ref[...], m_ref[...], v_ref[...])
            for r in range(4):
                outs[4 * a + r][...] = res[r]

    every = list(row_params) + list(params)
    flat = pl.pallas_call(
        body, name="adam_small",
        out_shape=[jax.ShapeDtypeStruct(w.shape, F32) for w, _, _ in every for _ in range(4)]
        + [jax.ShapeDtypeStruct((1, LANES), F32)],
        compiler_params=pltpu.CompilerParams(vmem_limit_bytes=VMEM_BIG),
    )(rows_all, *slices, *[t for wmv in every for t in wmv])
    return [flat[4 * a:4 * a + 4] for a in range(n)], flat[4 * n][0, 0]


BIG = (("w_in", 1024, 404, 1), ("w_uq", 384, 96, 1), ("w_uk", 256, 64, 1), ("w_uv", 256, 64, 1),
       ("w_glu", 64, 512, 0), ("w_branch_attn", 512, 128, 1), ("w_branch_ssm", 512, 128, 1),
       ("w_out", 128, 1024, 0), ("w_up", 1024, 704, 1), ("w_down", 352, 1024, 0), ("conv_w", 3, 704, 1))
GATHER_FIRST, GATHER_PROJ, GATHER_LATER = BIG[:1], BIG[1:4], BIG[8:] + BIG[4:8]
GRADS_EARLY, GRADS_PROJ, GRADS_LAST = BIG[8:] + BIG[4:8], BIG[1:4], BIG[:1]
SMALL = (("mix_norm_pre", (1024,)), ("q_norm", (384,)), ("kv_norm", (256,)), ("ssm_lambda_re", (32, 64)),
         ("ssm_lambda_im", (32, 64)), ("ssm_log_dt", (32,)), ("ssm_b_re", (32, 64, 16)), ("ssm_b_im", (32, 64, 16)),
         ("ssm_c_re", (32, 16, 64)), ("ssm_c_im", (32, 16, 64)), ("ssm_d", (32, 16)), ("b_glu", (512,)),
         ("b_gate", (2048,)), ("mix_norm_post", (1024,)), ("ffn_norm_pre", (1024,)), ("conv_b", (5632,)),
         ("ffn_norm_post", (1024,)))


TRANSPOSED = ("w_in", "w_uq", "w_uk", "w_uv", "w_up")


STORED_SWAP = {**{name: (1, 2) for name in TRANSPOSED}, "ssm_b_re": (2, 3), "ssm_b_im": (2, 3), "ssm_d": (1, 2)}


def _stored(name, arr):
    return jnp.swapaxes(arr, *STORED_SWAP[name]) if name in STORED_SWAP else arr


def _to_slices(name, full, rows, cols, axis):
    if name in TRANSPOSED:
        return full.reshape(N_DEV, cols, rows)
    if axis == 1:
        return full.reshape(rows, N_DEV, cols).transpose(1, 0, 2)
    return full.reshape(N_DEV, rows, cols)


def _from_slices(name, parts, rows, cols, axis):
    if name in TRANSPOSED:
        return parts.reshape(N_DEV * cols, rows)
    if axis == 1:
        return parts.transpose(1, 0, 2).reshape(rows, N_DEV * cols)
    return parts.reshape(N_DEV * rows, cols)


def _time_perm(a, L):
    return a.reshape(8, L // 8, a.shape[-1]).transpose(1, 0, 2).reshape(L, a.shape[-1])


def _time_unperm(a, L):
    return a.reshape(L // 8, 8, a.shape[-1]).transpose(1, 0, 2).reshape(L, a.shape[-1])


def _block_diag(w, rows_first):
    eye = jnp.eye(8, dtype=w.dtype)
    g = w.reshape(SSM_CHUNKS, 8, w.shape[1], w.shape[2])
    return jnp.einsum("qgrc,gk->qgrkc", g, eye).reshape(SSM_CHUNKS, 8 * w.shape[1], 8 * w.shape[2])


def _block_diag_t(m, r, c):
    eye = jnp.eye(8, dtype=m.dtype)
    return jnp.einsum("qgrkc,gk->qgrc", m.reshape(SSM_CHUNKS, 8, r, 8, c), eye).reshape(SSM_GROUPS, r, c)


def kernel(x, positions, mix_norm_pre, w_in, q_norm, w_uq, kv_norm, w_uk, w_uv, ssm_lambda_re, ssm_lambda_im, ssm_log_dt, ssm_b_re, ssm_b_im, ssm_c_re, ssm_c_im, ssm_d, w_glu, b_glu, w_branch_attn, w_branch_ssm, b_gate, w_out, mix_norm_post, ffn_norm_pre, w_up, conv_w, conv_b, w_down, ffn_norm_post, loss_target, m_mix_norm_pre, m_w_in, m_q_norm, m_w_uq, m_kv_norm, m_w_uk, m_w_uv, m_ssm_lambda_re, m_ssm_lambda_im, m_ssm_log_dt, m_ssm_b_re, m_ssm_b_im, m_ssm_c_re, m_ssm_c_im, m_ssm_d, m_w_glu, m_b_glu, m_w_branch_attn, m_w_branch_ssm, m_b_gate, m_w_out, m_mix_norm_post, m_ffn_norm_pre, m_w_up, m_conv_w, m_conv_b, m_w_down, m_ffn_norm_post, v_mix_norm_pre, v_w_in, v_q_norm, v_w_uq, v_kv_norm, v_w_uk, v_w_uv, v_ssm_lambda_re, v_ssm_lambda_im, v_ssm_log_dt, v_ssm_b_re, v_ssm_b_im, v_ssm_c_re, v_ssm_c_im, v_ssm_d, v_w_glu, v_b_glu, v_w_branch_attn, v_w_branch_ssm, v_b_gate, v_w_out, v_mix_norm_post, v_ffn_norm_pre, v_w_up, v_conv_w, v_conv_b, v_w_down, v_ffn_norm_post):
    given = dict(locals())
    L = x.shape[1]
    xs = x[0]
    target = loss_target[0]

    def shard_bits(group):
        return [given[name][0] if name == "conv_w" else _stored(name, given[name])[0].astype(BF16) for name, _, _, _ in group]

    W = {}

    def unpack_weights(gathered, group):
        for (name, rows, cols, axis), parts in zip(group, gathered):
            W[name] = _from_slices(name, parts, rows, cols, axis)

    hn1, *gathered_w_in = _rms_fwd_call(xs, mix_norm_pre, "rms_pre", shard_bits(GATHER_FIRST))
    unpack_weights(gathered_w_in, GATHER_FIRST)

    wit = W["w_in"]
    zero_rows = lambda r: jnp.zeros((r, D_MODEL), BF16)
    kr_end = P_KR + QK_ROPE
    w_in_pt = jnp.concatenate(
        [wit[:P_KR], zero_rows(QK_NOPE), wit[P_KR:kr_end], zero_rows(LANES - QK_HEAD), wit[kr_end:]], axis=0)

    proj, *gathered_proj = _mm(hn1, w_in_pt, "mm_in", tb=True, tn=w_in_pt.shape[0], gather=shard_bits(GATHER_PROJ))
    unpack_weights(gathered_proj, GATHER_PROJ)
    head_rows = lambda wt, width: jnp.pad(wt.reshape(N_HEADS, width, wt.shape[1]), ((0, 0), (0, LANES - width), (0, 0)))
    w_uq_pt = head_rows(W["w_uq"], QK_HEAD).reshape(HEAD_PAD, Q_RANK)
    w_kv_pt = jnp.stack([head_rows(W["w_uk"], QK_NOPE), head_rows(W["w_uv"], V_HEAD)], axis=1
                        ).reshape(2 * HEAD_PAD, KV_RANK)
    half = jnp.arange(QK_ROPE // 2, dtype=F32)
    inv_freq = ROPE_THETA ** (-2.0 * half / QK_ROPE)
    inv_freq = jnp.pad(jnp.concatenate([inv_freq, inv_freq]), (QK_NOPE, LANES - QK_HEAD)).reshape(1, LANES)
    pos_col = positions.astype(F32).reshape(L, 1)
    qn, ckvn, q_r, kv_r, cosf, sinf = _mla_proj_call(proj, q_norm, kv_norm, w_uq_pt, w_kv_pt, pos_col, inv_freq)
    attn, lse, *gathered_later = _attn_fwd_call(q_r, kv_r, shard_bits(GATHER_LATER))
    unpack_weights(gathered_later, GATHER_LATER)
    w_ba_p = jnp.pad(W["w_branch_attn"].reshape(N_HEADS, V_HEAD, D_MODEL), ((0, 0), (0, LANES - V_HEAD), (0, 0))
                     ).reshape(HEAD_PAD, D_MODEL)

    col = lambda a: a.reshape(SSM_NSTATE, -1)
    lr_c, li_c = col(ssm_lambda_re[0]), col(ssm_lambda_im[0])
    ldt_c = col(jnp.broadcast_to(ssm_log_dt[0][:, None], (SSM_GROUPS, SSM_STATE)))
    br_c, bi_c = col(ssm_b_re[0]), col(ssm_b_im[0])
    a_re_c, a_im_c, bb_re_c, bb_im_c = _disc_call(lr_c, li_c, ldt_c, br_c, bi_c)
    a_re, a_im = a_re_c.reshape(1, SSM_NSTATE), a_im_c.reshape(1, SSM_NSTATE)
    to_bb = lambda b: _block_diag(b.reshape(SSM_GROUPS, SSM_STATE, SSM_GROUP).transpose(0, 2, 1), True).astype(BF16)
    bb_re, bb_im = to_bb(bb_re_c), to_bb(bb_im_c)
    to_cm = lambda c_: _block_diag(c_[0].transpose(0, 2, 1), True).astype(BF16)
    cm_re, cm_im = to_cm(ssm_c_re), to_cm(ssm_c_im)
    d_skip = ssm_d.reshape(1, SSM_WIDTH)
    u_p = _time_perm(proj[:, P_U:P_GATE], L)
    y1, s_re, s_im = _ssm_fwd_call(u_p, a_re, a_im, bb_re, bb_im, cm_re, cm_im, d_skip)
    w_glu_b = W["w_glu"]
    ssm_p = _glu_call(y1, w_glu_b, b_glu)
    ssm = _time_unperm(ssm_p, L)

    pa = _mm(attn, w_ba_p, "mm_ba")
    ps = _mm(ssm, W["w_branch_ssm"], "mm_bs")
    merged = _merge_call(proj, b_gate, pa, ps)
    wide = lambda dt: (D_MODEL, dt)
    o, x2, hn2 = _mm_rows(merged, W["w_out"], "mm_out", _post_mix_rows, [xs], [mix_norm_post, ffn_norm_pre],
                          [wide(F32), wide(F32), wide(BF16)], [])
    h = _mm(hn2, W["w_up"], "mm_up", tb=True, tn=D_FF)
    cw = W["conv_w"]
    act = _conv_act_call(h, cw, conv_b)
    dy, dff, loss_row, g_ffn_norm_post = _mm_rows(
        act, W["w_down"], "mm_down", _ffn_out_rows, [x2, target], [ffn_norm_post], [wide(F32), wide(BF16)],
        [LANES, D_MODEL], tk=1408)

    da = _mm(dff, W["w_down"], "mm_down_dx", tb=True, tn=D_FF)
    g_w_down = _mm_tn(act, dff, "mm_down_dw", tm=1408)
    dgate, dval, dcw_g, dcw_v, dcb_g, dcb_v = _conv_act_bwd_call(da, h, cw, conv_b)
    g_conv_w = jnp.concatenate([dcw_g, dcw_v], axis=1)
    g_conv_b = jnp.concatenate([dcb_g, dcb_v], axis=1)
    dh = _conv_t_call(dgate, dval, cw)
    dx2, do, g_ffn_norm_pre, g_mix_norm_post = _mm_rows(
        dh, W["w_up"], "mm_up_dx", _post_bwd_rows, [x2, dy, o], [ffn_norm_pre, mix_norm_post], [wide(F32), wide(BF16)],
        [D_MODEL, D_MODEL], tk=1408)
    g_w_up = _mm_tn(dh, hn2, "mm_up_dw", tm=1408)
    dmerged = _mm(do, W["w_out"], "mm_out_dx", tb=True)
    g_w_out = _mm_tn(merged, do, "mm_out_dw")
    dpa, dps, dl0, dl1, db0, db1 = _merge_bwd_call(dmerged, proj, b_gate, pa, ps)
    g_b_gate = jnp.concatenate([db0, db1], axis=1)
    dattn = _mm(dpa, w_ba_p, "mm_ba_dx", tb=True, out_dtype=BF16)
    g_w_ba = _mm_tn(attn, dpa, "mm_ba_dw").reshape(N_HEADS, LANES, D_MODEL)[:, :V_HEAD].reshape(N_HEADS * V_HEAD, D_MODEL)
    dssm = _mm(dps, W["w_branch_ssm"], "mm_bs_dx", tb=True)
    g_w_bs = _mm_tn(ssm, dps, "mm_bs_dw")

    dy1, g_w_glu, g_b_glu = _glu_bwd_call(_time_perm(dssm, L), y1, w_glu_b, b_glu)
    du_p, dbb_re, dbb_im, dcm_re, dcm_im, da_re, da_im, g_ssm_d = _ssm_bwd_call(
        dy1, u_p, s_re, s_im, a_re, a_im, bb_re, bb_im, cm_re, cm_im, d_skip)
    du = _time_unperm(du_p, L)
    from_bb = lambda m: col(_block_diag_t(m, SSM_GROUP, SSM_STATE).transpose(0, 2, 1))
    dlr, dli, dldt, dbr, dbi = _disc_bwd_call(
        lr_c, li_c, ldt_c, br_c, bi_c, da_re.reshape(SSM_NSTATE, 1), da_im.reshape(SSM_NSTATE, 1), from_bb(dbb_re), from_bb(dbb_im))
    g_c_re = _block_diag_t(dcm_re, SSM_STATE, SSM_GROUP).transpose(0, 2, 1)
    g_c_im = _block_diag_t(dcm_im, SSM_STATE, SSM_GROUP).transpose(0, 2, 1)

    def grad_slices(group, grads):
        return [_to_slices(name, grads[name], rows, cols, axis) for name, rows, cols, axis in group]

    early_grads = {"w_up": g_w_up, "w_down": g_w_down, "conv_w": g_conv_w, "w_glu": g_w_glu.astype(BF16),
                   "w_branch_attn": g_w_ba, "w_branch_ssm": g_w_bs, "w_out": g_w_out}
    b_stored = lambda d: d.reshape(SSM_GROUPS, SSM_STATE, SSM_GROUP).transpose(0, 2, 1)
    per_state = lambda d: d.reshape(SSM_GROUPS, SSM_STATE)
    ssm_partials = {"ssm_lambda_re": per_state(dlr), "ssm_lambda_im": per_state(dli),
                    "ssm_b_re": b_stored(dbr), "ssm_b_im": b_stored(dbi),
                    "ssm_c_re": g_c_re, "ssm_c_im": g_c_im, "ssm_d": g_ssm_d.reshape(SSM_GROUPS, SSM_GROUP).T}
    ssm_shapes = [(name, ssm_partials[name].shape) for name, _ in SMALL if name in ssm_partials]
    dq, dkv, *landed = _attn_bwd_call(
        q_r, kv_r, attn, dattn, lse, grad_slices(GRADS_EARLY, early_grads),
        [ssm_partials[name].reshape(-1, LANES) if len(shp) == 3 else ssm_partials[name].reshape((1,) + shp)
         for name, shp in ssm_shapes])
    received_early = landed[:len(GRADS_EARLY)]
    ssm_all = {name: got.reshape((N_DEV, 1) + shp) for (name, shp), got in zip(ssm_shapes, landed[len(GRADS_EARLY):])}
    dq_p, dkv_p, dlatent, g_q_norm, g_kv_norm = _mla_proj_bwd_call(
        dq, dkv, cosf, sinf, proj, q_norm, kv_norm, w_uq_pt, w_kv_pt)
    g_w_uq = _mm_tn(dq_p, qn, "mm_uq_dw").reshape(N_HEADS, LANES, Q_RANK)[:, :QK_HEAD].reshape(N_HEADS * QK_HEAD, Q_RANK)
    g_w_kv = _mm_tn(ckvn, dkv_p, "mm_ukv_dw").T.reshape(N_HEADS, 2, LANES, KV_RANK)
    g_w_uk = g_w_kv[:, 0, :QK_NOPE].reshape(N_HEADS * QK_NOPE, KV_RANK)
    g_w_uv = g_w_kv[:, 1, :V_HEAD].reshape(N_HEADS * V_HEAD, KV_RANK)
    dproj = jnp.concatenate([dlatent, du.astype(BF16), dl0, dl1], axis=1)
    proj_grads = {"w_uq": g_w_uq, "w_uk": g_w_uk, "w_uv": g_w_uv}
    g_w_in_pt, *received_proj = _mm_tn(dproj, hn1, "mm_in_dw", tm=1664, exchange=grad_slices(GRADS_PROJ, proj_grads))
    g_w_in = jnp.concatenate([g_w_in_pt[:P_KR], g_w_in_pt[P_KR + QK_NOPE:P_KR + QK_HEAD], g_w_in_pt[P_U:]], axis=0)
    grad_x, g_mix_norm_pre, *received_last = _mm_in_dx_call(
        dproj, w_in_pt, xs, dx2, mix_norm_pre, grad_slices(GRADS_LAST, {"w_in": g_w_in}))

    results = {}
    wmv = lambda name: tuple(_stored(name, given[prefix + name]) for prefix in ("", "m_", "v_"))
    unstored = lambda name, res: [_stored(name, r) for r in res]
    whole = ("w_uq", "w_uk", "w_uv", "w_glu", "w_branch_attn", "w_branch_ssm", "conv_w")
    landed_small = dict(ssm_all)
    for group, received in ((GRADS_EARLY, received_early), (GRADS_PROJ, received_proj), (GRADS_LAST, received_last)):
        for (name, _, _, _), rec in zip(group, received):
            if name in whole:
                landed_small[name] = rec[:, None]
            else:
                results[name] = unstored(name, _adam_call(rec[:, None], *wmv(name), "adam_" + name))

    vec_grads = {"mix_norm_pre": g_mix_norm_pre, "q_norm": g_q_norm, "kv_norm": g_kv_norm,
                 "ssm_log_dt": jnp.sum(dldt.reshape(SSM_GROUPS, SSM_STATE), axis=1),
                 "b_glu": g_b_glu, "b_gate": g_b_gate, "mix_norm_post": g_mix_norm_post,
                 "ffn_norm_pre": g_ffn_norm_pre, "ffn_norm_post": g_ffn_norm_post}
    vec_names = [name for name, _ in SMALL if name in vec_grads]
    width = max(shp[0] for name, shp in SMALL if name in vec_grads)
    rows = [jnp.pad(vec_grads[name].reshape(1, -1), ((0, 0), (0, width - vec_grads[name].size))) for name in vec_names]
    rows.append(jnp.pad(loss_row, ((0, 0), (0, width - LANES))))
    rows.append(jnp.zeros((-len(rows) % 8, width), F32))
    rows_all, landed_small["conv_b"] = _small_gather_call([jnp.concatenate(rows, axis=0), g_conv_b], "gather_small_grads")
    others = ["conv_b"] + [name for name, _ in ssm_shapes] + list(whole)
    small_results, loss = _adam_small_call(
        rows_all, [wmv(n) for n in vec_names], [landed_small[n] for n in others], [wmv(n) for n in others])
    for name, res in zip(vec_names + others, small_results):
        results[name] = unstored(name, res)

    order = ["mix_norm_pre", "w_in", "q_norm", "w_uq", "kv_norm", "w_uk", "w_uv", "ssm_lambda_re", "ssm_lambda_im",
             "ssm_log_dt", "ssm_b_re", "ssm_b_im", "ssm_c_re", "ssm_c_im", "ssm_d", "w_glu", "b_glu", "w_branch_attn",
             "w_branch_ssm", "b_gate", "w_out", "mix_norm_post", "ffn_norm_pre", "w_up", "conv_w", "conv_b", "w_down",
             "ffn_norm_post"]
    outs = [loss, grad_x[None]]
    for kind in range(4):
        outs += [results[name][kind] for name in order]
    return tuple(outs)
```

```python
import math

import jax
import jax.numpy as jnp
from jax import lax
from jax.experimental import pallas as pl
from jax.experimental.pallas import tpu as pltpu

F32 = jnp.float32
BF16 = jnp.bfloat16
MESH_ID = pl.DeviceIdType.MESH

N_DEV = 8
LANES = 128
D_MODEL = 1024
N_HEADS = 8
QK_NOPE = 64
QK_ROPE = 32
QK_HEAD = QK_NOPE + QK_ROPE
V_HEAD = 64
Q_RANK = 384
KV_RANK = 256
ROPE_THETA = 10000.0
SSM_WIDTH = 512
SSM_GROUP = 16
SSM_GROUPS = 32
SSM_STATE = 64
SSM_NSTATE = SSM_GROUPS * SSM_STATE
SSM_CHUNKS = 4
D_FF = 2816
EPS = 1e-6
ADAM_LR, ADAM_B1, ADAM_B2, ADAM_EPS, ADAM_WD, ADAM_STEP = 0.001, 0.9, 0.999, 1e-08, 0.01, 10

P_CQ, P_CKV, P_KR, P_U, P_GATE = 0, 384, 640, 768, 1280
HEAD_PAD = N_HEADS * LANES

VMEM_BIG = 52 * 1024 * 1024

_GELU_C0 = math.sqrt(2.0 / math.pi)
_GELU_C1 = 0.044715
NEG = -1e30


def _fit(n, pref, mult=LANES):
    if n <= pref:
        return n
    t = (pref // mult) * mult
    while t > 0 and n % t:
        t -= mult
    assert t > 0, (n, pref, mult)
    return t


def _gelu(x):
    return x * (0.5 * (1.0 + jnp.tanh(_GELU_C0 * x * (1.0 + _GELU_C1 * (x * x)))))


def _gelu_and_grad(x):
    x2 = x * x
    t = jnp.tanh(_GELU_C0 * x * (1.0 + _GELU_C1 * x2))
    half = 0.5 * (1.0 + t)
    return x * half, half + 0.5 * x * (1.0 - t * t) * _GELU_C0 * (1.0 + 3.0 * _GELU_C1 * x2)


def _sigmoid(x):
    return 1.0 / (1.0 + jnp.exp(-x))


def _dot(a, b, dims):
    return lax.dot_general(a, b, (dims, ((), ())), preferred_element_type=F32)


NN = ((1,), (0,))
NT = ((1,), (1,))
TN = ((0,), (0,))


def _params(*sem, vmem=None):
    return pltpu.CompilerParams(dimension_semantics=tuple(sem), vmem_limit_bytes=vmem)


def _mm(a, b, name, tb=False, out_dtype=F32, tm=1024, tn=1024, tk=1024, gather=()):
    M, K = a.shape
    if tb:
        N, K2 = b.shape
    else:
        K2, N = b.shape
    assert K == K2, (a.shape, b.shape, tb)
    tm, tn, tk = _fit(M, tm), _fit(N, tn), _fit(K, tk)
    nk = K // tk
    grid = (M // tm, N // tn, nk)
    steps = grid[0] * grid[1] * grid[2]
    dims = NT if tb else NN
    n = len(gather)

    def body(a_ref, b_ref, *refs):
        o_ref, scratch = refs[n], refs[2 * n + 1:]
        step = (pl.program_id(0) * grid[1] + pl.program_id(1)) * grid[2] + pl.program_id(2)
        if n:
            start, forward, finish = _gather_phases(refs[:n], refs[n + 1:2 * n + 1], *scratch[-3:])
            pl.when(step == 0)(start)
            pl.when(step == steps // 2)(forward)
        part = _dot(a_ref[...].astype(BF16), b_ref[...].astype(BF16), dims)
        if nk == 1:
            o_ref[...] = part.astype(out_dtype)
        else:
            acc_ref = scratch[0]
            k = pl.program_id(2)

            @pl.when(k == 0)
            def _():
                acc_ref[...] = part

            @pl.when(k > 0)
            def _():
                acc_ref[...] += part

            @pl.when(k == nk - 1)
            def _():
                o_ref[...] = acc_ref[...].astype(out_dtype)
        if n:
            pl.when(step == steps - 1)(finish)

    a_spec = pl.BlockSpec((tm, tk), lambda i, j, k: (i, k))
    b_spec = pl.BlockSpec((tn, tk), lambda i, j, k: (j, k)) if tb else pl.BlockSpec((tk, tn), lambda i, j, k: (k, j))
    landed = [jax.ShapeDtypeStruct((N_DEV,) + p.shape, p.dtype) for p in gather]
    out = pl.pallas_call(
        body, name=name, grid=grid,
        in_specs=[a_spec, b_spec] + [ANY_SPEC] * n,
        out_specs=[pl.BlockSpec((tm, tn), lambda i, j, k: (i, j))] + [ANY_SPEC] * n,
        out_shape=[jax.ShapeDtypeStruct((M, N), out_dtype)] + landed,
        scratch_shapes=([] if nk == 1 else [pltpu.VMEM((tm, tn), F32)]) + (_comm_sems(n) if n else []),
        compiler_params=_params(*(("arbitrary",) * 3 if n else ("parallel", "parallel", "arbitrary")), vmem=VMEM_BIG),
    )(a, b, *gather)
    return out if n else out[0]


def _mm_rows(a, b, name, epilogue, rows_in, vecs_in, rows_out, vecs_out, tb=False, tk=1024):
    M, K = a.shape
    N = b.shape[0] if tb else b.shape[1]
    tm, tk = _fit(M, 512), _fit(K, tk)
    nk = K // tk
    nr, nv, nro = len(rows_in), len(vecs_in), len(rows_out)

    def body(a_ref, b_ref, *refs):
        ins, outs, acc_ref = refs[:nr + nv], refs[nr + nv:nr + nv + nro + len(vecs_out)], refs[-1]
        i, k = pl.program_id(0), pl.program_id(1)
        part = _dot(a_ref[...], b_ref[...], NT if tb else NN)

        def finish(product):
            res = epilogue(product, *[r[...] for r in ins])
            for ref, val in zip(outs[:nro], res[:nro]):
                ref[...] = val.astype(ref.dtype)
            for ref, val in zip(outs[nro:], res[nro:]):
                _acc(ref, i == 0, val)

        if nk == 1:
            finish(part)
        else:
            @pl.when(k == 0)
            def _():
                acc_ref[...] = part

            @pl.when(jnp.logical_and(k > 0, k < nk - 1))
            def _():
                acc_ref[...] += part

            @pl.when(k == nk - 1)
            def _():
                finish(acc_ref[...] + part)

    row = lambda w: pl.BlockSpec((tm, w), lambda i, k: (i, 0))
    vec = lambda w: pl.BlockSpec((1, w), lambda i, k: (0, 0))
    b_spec = pl.BlockSpec((N, tk), lambda i, k: (0, k)) if tb else pl.BlockSpec((tk, N), lambda i, k: (k, 0))
    return pl.pallas_call(
        body, name=name, grid=(M // tm, nk),
        in_specs=[pl.BlockSpec((tm, tk), lambda i, k: (i, k)), b_spec] + [row(r.shape[1]) for r in rows_in]
        + [vec(v.shape[1]) for v in vecs_in],
        out_specs=[row(w) for w, _ in rows_out] + [vec(w) for w in vecs_out],
        out_shape=[jax.ShapeDtypeStruct((M, w), dt) for w, dt in rows_out] + [jax.ShapeDtypeStruct((1, w), F32) for w in vecs_out],
        scratch_shapes=[pltpu.VMEM((tm, N), F32)],
        compiler_params=_params("arbitrary", "arbitrary", vmem=VMEM_BIG))(a, b, *rows_in, *vecs_in)


def _mm_in_dx_call(dproj, w_in_pt, x, dx2, g_pre, exchange):
    L, K = dproj.shape
    N = w_in_pt.shape[1]
    tm, tk = _fit(L, 512), _fit(K, 1664)
    nm, nk = L // tm, K // tk
    n = len(exchange)

    def body(a_ref, b_ref, x_ref, dx2_ref, g_ref, *refs):
        parts, (gx_ref, dg_ref), got = refs[:n], refs[n:n + 2], refs[n + 2:2 * n + 2]
        acc_ref = refs[2 * n + 2]
        i, k = pl.program_id(0), pl.program_id(1)
        start, finish = _exchange_phases(parts, got, *refs[2 * n + 3:])
        pl.when(jnp.logical_and(i == 0, k == 0))(start)
        part = _dot(a_ref[...], b_ref[...], NN)

        @pl.when(k == 0)
        def _():
            acc_ref[...] = part

        @pl.when(jnp.logical_and(k > 0, k < nk - 1))
        def _():
            acc_ref[...] += part

        @pl.when(k == nk - 1)
        def _():
            d1, dg = _rms_bwd(x_ref[...], g_ref[...], acc_ref[...] + part)
            gx_ref[...] = dx2_ref[...] + d1
            _acc(dg_ref, i == 0, dg)

        pl.when(jnp.logical_and(i == nm - 1, k == nk - 1))(finish)

    assert nk >= 2
    rows = lambda: pl.BlockSpec((tm, N), lambda i, k: (i, 0))
    return pl.pallas_call(
        body, name="mm_in_dx", grid=(nm, nk),
        in_specs=[pl.BlockSpec((tm, tk), lambda i, k: (i, k)), pl.BlockSpec((tk, N), lambda i, k: (k, 0)),
                  rows(), rows(), pl.BlockSpec((1, N), lambda i, k: (0, 0))] + [ANY_SPEC] * n,
        out_specs=[rows(), pl.BlockSpec((1, N), lambda i, k: (0, 0))] + [ANY_SPEC] * n,
        out_shape=[jax.ShapeDtypeStruct((L, N), F32), jax.ShapeDtypeStruct((1, N), F32)]
        + [jax.ShapeDtypeStruct(p.shape, p.dtype) for p in exchange],
        scratch_shapes=[pltpu.VMEM((tm, N), F32)] + _comm_sems(n),
        compiler_params=_params("arbitrary", "arbitrary", vmem=VMEM_BIG))(dproj, w_in_pt, x, dx2, g_pre, *exchange)


TN_CHUNK = 512


def _mm_tn(a, b, name, tm=512, tk=1024, exchange=()):
    K, M = a.shape
    K2, N = b.shape
    assert K == K2, (a.shape, b.shape)
    tm, tk, cn = _fit(M, tm), _fit(K, tk), _fit(N, TN_CHUNK)
    nm, nk = M // tm, K // tk
    n = len(exchange)

    def body(a_ref, b_ref, *refs):
        o_ref, acc_ref = refs[n], refs[2 * n + 1]
        i, k = pl.program_id(0), pl.program_id(1)
        if n:
            start, finish = _exchange_phases(refs[:n], refs[n + 1:2 * n + 1], *refs[2 * n + 2:])
            pl.when(jnp.logical_and(i == 0, k == 0))(start)

        @pl.when(k == 0)
        def _():
            acc_ref[...] = jnp.zeros((tm, N), F32)

        at = a_ref[...].astype(BF16).T
        for c in range(N // cn):
            cols = slice(c * cn, (c + 1) * cn)
            acc_ref[:, cols] += _dot(at, b_ref[:, cols].astype(BF16), NN)

        @pl.when(k == nk - 1)
        def _():
            o_ref[...] = acc_ref[...].astype(BF16)

        if n:
            pl.when(jnp.logical_and(i == nm - 1, k == nk - 1))(finish)

    out = pl.pallas_call(
        body, name=name, grid=(nm, nk),
        in_specs=[pl.BlockSpec((tk, tm), lambda i, k: (k, i)), pl.BlockSpec((tk, N), lambda i, k: (k, 0))] + [ANY_SPEC] * n,
        out_specs=[pl.BlockSpec((tm, N), lambda i, k: (i, 0))] + [ANY_SPEC] * n,
        out_shape=[jax.ShapeDtypeStruct((M, N), BF16)] + [jax.ShapeDtypeStruct(p.shape, p.dtype) for p in exchange],
        scratch_shapes=[pltpu.VMEM((tm, N), F32)] + (_comm_sems(n) if n else []),
        compiler_params=_params("arbitrary" if n else "parallel", "arbitrary", vmem=VMEM_BIG))(a, b, *exchange)
    return out if n else out[0]


def _row(tl, n, col=0):
    return pl.BlockSpec((tl, n), lambda i: (i, col))


def _full(shape):
    return pl.BlockSpec(shape, lambda i: (0,) * len(shape))


def _rms(x, g):
    r = lax.rsqrt(jnp.mean(x * x, axis=-1, keepdims=True) + EPS)
    return x * r * g


def _rms_bwd(x, g, dy):
    n = x.shape[-1]
    r = lax.rsqrt(jnp.mean(x * x, axis=-1, keepdims=True) + EPS)
    gy = dy * g
    dx = r * gy - x * (r * r * r * (1.0 / n)) * jnp.sum(x * gy, axis=-1, keepdims=True)
    return dx, jnp.sum(dy * x * r, axis=0, keepdims=True)


def _acc(ref, first, val):
    @pl.when(first)
    def _():
        ref[...] = val

    @pl.when(jnp.logical_not(first))
    def _():
        ref[...] += val


def _rms_fwd_call(x, g, name, gather):
    L, n = x.shape
    tl = _fit(L, 512)
    steps, na = L // tl, len(gather)

    def body(x_ref, g_ref, *refs):
        o_ref = refs[na]
        start, forward, finish = _gather_phases(refs[:na], refs[na + 1:2 * na + 1], *refs[2 * na + 1:])
        i = pl.program_id(0)
        pl.when(i == 0)(start)
        pl.when(i == steps // 2)(forward)
        o_ref[...] = _rms(x_ref[...], g_ref[...]).astype(BF16)
        pl.when(i == steps - 1)(finish)

    return pl.pallas_call(
        body, name=name, grid=(steps,), in_specs=[_row(tl, n), _full((1, n))] + [ANY_SPEC] * na,
        out_specs=[_row(tl, n)] + [ANY_SPEC] * na,
        out_shape=[jax.ShapeDtypeStruct((L, n), BF16)] + [jax.ShapeDtypeStruct((N_DEV,) + b.shape, b.dtype) for b in gather],
        scratch_shapes=_comm_sems(na), compiler_params=_params("arbitrary"))(x, g, *gather)


def _rope_lanes(shape):
    lane = lax.broadcasted_iota(jnp.int32, shape, 1)
    return lane, jnp.logical_and(lane >= QK_NOPE, lane < QK_HEAD)


def _rope_apply(x, cosf, sinf, lane):
    rot = jnp.where(lane < QK_NOPE + QK_ROPE // 2, -pltpu.roll(x, LANES - QK_ROPE // 2, 1), pltpu.roll(x, QK_ROPE // 2, 1))
    return x * cosf + rot * sinf


def _rope_apply_t(dy, cosf, sinf, lane, is_rope):
    g = dy * sinf
    rot_t = jnp.where(lane < QK_NOPE + QK_ROPE // 2, pltpu.roll(g, LANES - QK_ROPE // 2, 1), -pltpu.roll(g, QK_ROPE // 2, 1))
    return dy * cosf + jnp.where(is_rope, rot_t, 0.0)


def _mla_proj_call(proj, q_norm, kv_norm, w_uq_pt, w_kv_pt, pos_col, inv_freq):
    L = proj.shape[0]
    tl = _fit(L, 512)

    def body(p_ref, gq_ref, gk_ref, wq_ref, wkv_ref, pos_ref, f_ref, qn_ref, kn_ref, qo_ref, kvo_ref, cos_ref, sin_ref):
        qn = _rms(p_ref[:, P_CQ:P_CKV], gq_ref[...]).astype(BF16)
        kn = _rms(p_ref[:, P_CKV:P_KR], gk_ref[...]).astype(BF16)
        qn_ref[...] = qn
        kn_ref[...] = kn
        q_pad = _dot(qn, wq_ref[...], NT)
        kv_pad = _dot(kn, wkv_ref[...], NT)
        lane, is_rope = _rope_lanes((tl, LANES))
        ang = pos_ref[...] * f_ref[...]
        cosf = jnp.where(is_rope, jnp.cos(ang), jnp.where(lane < QK_NOPE, 1.0, 0.0))
        sinf = jnp.where(is_rope, jnp.sin(ang), 0.0)
        cos_ref[...] = cosf
        sin_ref[...] = sinf
        kr = _rope_apply(p_ref[:, P_KR:P_U], cosf, sinf, lane)
        for h in range(N_HEADS):
            qh = _rope_apply(q_pad[:, h * LANES:(h + 1) * LANES], cosf, sinf, lane)
            qo_ref[:, h * LANES:(h + 1) * LANES] = (qh * Q_PRESCALE).astype(BF16)
            kvo_ref[:, 2 * h * LANES:(2 * h + 1) * LANES] = (kv_pad[:, 2 * h * LANES:(2 * h + 1) * LANES] + kr).astype(BF16)
            vh = jnp.where(lane == V_HEAD, 1.0, kv_pad[:, (2 * h + 1) * LANES:(2 * h + 2) * LANES])
            kvo_ref[:, (2 * h + 1) * LANES:(2 * h + 2) * LANES] = vh.astype(BF16)

    shape = lambda n, dt: jax.ShapeDtypeStruct((L, n), dt)
    return pl.pallas_call(
        body, name="mla_proj", grid=(L // tl,),
        in_specs=[_row(tl, P_U), _full((1, Q_RANK)), _full((1, KV_RANK)), _full((HEAD_PAD, Q_RANK)),
                  _full((2 * HEAD_PAD, KV_RANK)), _row(tl, 1), _full((1, LANES))],
        out_specs=[_row(tl, Q_RANK), _row(tl, KV_RANK), _row(tl, HEAD_PAD), _row(tl, 2 * HEAD_PAD), _row(tl, LANES), _row(tl, LANES)],
        out_shape=[shape(Q_RANK, BF16), shape(KV_RANK, BF16), shape(HEAD_PAD, BF16), shape(2 * HEAD_PAD, BF16),
                   shape(LANES, F32), shape(LANES, F32)],
        compiler_params=_params("parallel"))(proj, q_norm, kv_norm, w_uq_pt, w_kv_pt, pos_col, inv_freq)


def _mla_proj_bwd_call(dq, dkv, cosf, sinf, proj, q_norm, kv_norm, w_uq_pt, w_kv_pt):
    L = dq.shape[0]
    tl = _fit(L, 512)

    def body(dq_ref, dkv_ref, cos_ref, sin_ref, p_ref, gq_ref, gk_ref, wq_ref, wkv_ref,
             dqo_ref, dkvo_ref, d_ref, dgq_ref, dgk_ref):
        first = pl.program_id(0) == 0
        lane, is_rope = _rope_lanes((tl, LANES))
        cosf, sinf = cos_ref[...], sin_ref[...]
        dk_sum = jnp.zeros((tl, LANES), F32)
        for h in range(N_HEADS):
            dqo_ref[:, h * LANES:(h + 1) * LANES] = _rope_apply_t(dq_ref[:, h * LANES:(h + 1) * LANES], cosf, sinf, lane, is_rope).astype(BF16)
            dk_sum = dk_sum + dkv_ref[:, 2 * h * LANES:(2 * h + 1) * LANES]
        dkvo_ref[...] = dkv_ref[...].astype(BF16)
        dqn = _dot(dqo_ref[...], wq_ref[...], NN)
        dkn = _dot(dkvo_ref[...], wkv_ref[...], NN)
        dcq, dgq = _rms_bwd(p_ref[:, P_CQ:P_CKV], gq_ref[...], dqn)
        dckv, dgk = _rms_bwd(p_ref[:, P_CKV:P_KR], gk_ref[...], dkn)
        d_ref[:, P_CQ:P_CKV] = dcq.astype(BF16)
        d_ref[:, P_CKV:P_KR] = dckv.astype(BF16)
        d_ref[:, P_KR:P_U] = _rope_apply_t(dk_sum, cosf, sinf, lane, is_rope).astype(BF16)
        _acc(dgq_ref, first, dgq)
        _acc(dgk_ref, first, dgk)

    shape = lambda n: jax.ShapeDtypeStruct((L, n), BF16)
    return pl.pallas_call(
        body, name="mla_proj_bwd", grid=(L // tl,),
        in_specs=[_row(tl, HEAD_PAD), _row(tl, 2 * HEAD_PAD), _row(tl, LANES), _row(tl, LANES), _row(tl, P_KR),
                  _full((1, Q_RANK)), _full((1, KV_RANK)), _full((HEAD_PAD, Q_RANK)), _full((2 * HEAD_PAD, KV_RANK))],
        out_specs=[_row(tl, HEAD_PAD), _row(tl, 2 * HEAD_PAD), _row(tl, P_U), _full((1, Q_RANK)), _full((1, KV_RANK))],
        out_shape=[shape(HEAD_PAD), shape(2 * HEAD_PAD), shape(P_U), jax.ShapeDtypeStruct((1, Q_RANK), F32),
                   jax.ShapeDtypeStruct((1, KV_RANK), F32)],
        compiler_params=_params("arbitrary"))(dq, dkv, cosf, sinf, proj, q_norm, kv_norm, w_uq_pt, w_kv_pt)


GATE_TILE = 256
GATE_ROWS = 1024


def _merge_call(proj, b_gate, pa, ps):
    L = proj.shape[0]
    tl = _fit(L, GATE_ROWS)
    nc = D_MODEL // GATE_TILE
    g0, g1 = P_GATE // GATE_TILE, (P_GATE + D_MODEL) // GATE_TILE

    def body(l0_ref, l1_ref, b0_ref, b1_ref, pa_ref, ps_ref, o_ref):
        s0 = _sigmoid(l0_ref[...] + b0_ref[...])
        s1 = _sigmoid(l1_ref[...] + b1_ref[...])
        o_ref[...] = (s0 * pa_ref[...] + s1 * ps_ref[...]).astype(BF16)

    blk = lambda off: pl.BlockSpec((tl, GATE_TILE), lambda i, j: (i, off + j))
    bias = lambda off: pl.BlockSpec((1, GATE_TILE), lambda i, j: (0, off + j))
    return pl.pallas_call(
        body, name="merge", grid=(L // tl, nc),
        in_specs=[blk(g0), blk(g1), bias(0), bias(nc), blk(0), blk(0)],
        out_specs=blk(0), out_shape=jax.ShapeDtypeStruct((L, D_MODEL), BF16),
        compiler_params=_params("parallel", "parallel"))(proj, proj, b_gate, b_gate, pa, ps)


def _merge_bwd_call(dm, proj, b_gate, pa, ps):
    L = proj.shape[0]
    tl = _fit(L, GATE_ROWS)
    nc = D_MODEL // GATE_TILE
    g0, g1 = P_GATE // GATE_TILE, (P_GATE + D_MODEL) // GATE_TILE

    def body(dm_ref, l0_ref, l1_ref, b0_ref, b1_ref, pa_ref, ps_ref, dpa_ref, dps_ref, dl0_ref, dl1_ref, db0_ref, db1_ref):
        first = pl.program_id(1) == 0
        dm_ = dm_ref[...]
        s0 = _sigmoid(l0_ref[...] + b0_ref[...])
        s1 = _sigmoid(l1_ref[...] + b1_ref[...])
        dpa_ref[...] = (dm_ * s0).astype(BF16)
        dps_ref[...] = (dm_ * s1).astype(BF16)
        dl0 = dm_ * pa_ref[...] * s0 * (1.0 - s0)
        dl1 = dm_ * ps_ref[...] * s1 * (1.0 - s1)
        dl0_ref[...] = dl0.astype(BF16)
        dl1_ref[...] = dl1.astype(BF16)
        _acc(db0_ref, first, jnp.sum(dl0, axis=0, keepdims=True))
        _acc(db1_ref, first, jnp.sum(dl1, axis=0, keepdims=True))

    blk = lambda off: pl.BlockSpec((tl, GATE_TILE), lambda j, i: (i, off + j))
    bias = lambda off: pl.BlockSpec((1, GATE_TILE), lambda j, i: (0, off + j))
    act = jax.ShapeDtypeStruct((L, D_MODEL), BF16)
    vec = jax.ShapeDtypeStruct((1, D_MODEL), F32)
    return pl.pallas_call(
        body, name="merge_bwd", grid=(nc, L // tl),
        in_specs=[blk(0), blk(g0), blk(g1), bias(0), bias(nc), blk(0), blk(0)],
        out_specs=[blk(0), blk(0), blk(0), blk(0), bias(0), bias(0)],
        out_shape=[act, act, act, act, vec, vec],
        compiler_params=_params("parallel", "arbitrary"))(dm, proj, proj, b_gate, b_gate, pa, ps)


def _post_mix_rows(o, x, g_post, g_fpre):
    x2 = x + _rms(o, g_post)
    return o, x2, _rms(x2, g_fpre)


def _ffn_out_rows(ff, x2, target, g_fpost):
    n = ff.shape[-1]
    err = x2 + _rms(ff, g_fpost) - target
    part = 0.5 * jnp.sum(jnp.sum(err * err, axis=-1, keepdims=True) * (1.0 / n), axis=0, keepdims=True)
    dy = err * (1.0 / n)
    dff, dg = _rms_bwd(ff, g_fpost, dy)
    return dy, dff, jnp.broadcast_to(part, (1, LANES)), dg


def _post_bwd_rows(dhn2, x2, dy, o, g_fpre, g_post):
    d1, dgf = _rms_bwd(x2, g_fpre, dhn2)
    dx2 = dy + d1
    do, dgp = _rms_bwd(o, g_post, dx2)
    return dx2, do, dgf, dgp


CONV_TILE = 256
CONV_WIDE = 1408
HALO = 16


def _conv3(w, b, x0, x1, x2):
    return b + w[2:3] * x0 + w[1:2] * x1 + w[0:1] * x2


def _down(x, by):
    return pltpu.roll(x, by, 0)


def _edge_down(edge, before, by):
    r = lax.broadcasted_iota(jnp.int32, edge.shape, 0)
    return jnp.where(r < by, pltpu.roll(before, by, 0), pltpu.roll(edge, by, 0))


def _edge_up(edge, after, by):
    r = lax.broadcasted_iota(jnp.int32, edge.shape, 0)
    return jnp.where(r >= HALO - by, pltpu.roll(after, HALO - by, 0), pltpu.roll(edge, HALO - by, 0))


def _gated(w_g, b_g, w_v, b_v, hg, hv, g1, g2, v1, v2):
    return _conv3(w_g, b_g, hg, g1, g2), _conv3(w_v, b_v, hv, v1, v2)


def _conv_specs(tl, tc, rows_inner):
    nh = tl // HALO
    if rows_inner:
        ij = lambda f: (lambda j, i: f(i, j))
    else:
        ij = lambda f: f
    cur = lambda off: pl.BlockSpec((tl, tc), ij(lambda i, j: (i, off + j)))
    prev = lambda off: pl.BlockSpec((HALO, tc), ij(lambda i, j: (jnp.maximum(i * nh - 1, 0), off + j)))
    par = lambda rows, off: pl.BlockSpec((rows, tc), ij(lambda i, j: (0, off + j)))
    return cur, prev, par


def _conv_act_call(h, conv_w, conv_b):
    L = h.shape[0]
    tl = _fit(L, 512)
    nc = D_FF // CONV_WIDE
    cur, prev, par = _conv_specs(tl, CONV_WIDE, False)

    def body(hg_ref, hv_ref, pg_ref, pv_ref, wg_ref, wv_ref, bg_ref, bv_ref, a_ref):
        not_first = (pl.program_id(0) > 0).astype(F32)
        par = (wg_ref[...], bg_ref[...], wv_ref[...], bv_ref[...])
        hg, hv = hg_ref[...], hv_ref[...]
        gate, val = _gated(*par, hg, hv, _down(hg, 1), _down(hg, 2), _down(hv, 1), _down(hv, 2))
        a_ref[...] = (_gelu(gate) * val).astype(BF16)
        eg, ev, bg, bv = hg[:HALO], hv[:HALO], pg_ref[...] * not_first, pv_ref[...] * not_first
        gate, val = _gated(*par, eg, ev, _edge_down(eg, bg, 1), _edge_down(eg, bg, 2),
                           _edge_down(ev, bv, 1), _edge_down(ev, bv, 2))
        a_ref[:HALO, :] = (_gelu(gate) * val).astype(BF16)

    return pl.pallas_call(
        body, name="conv_act", grid=(L // tl, nc),
        in_specs=[cur(0), cur(nc), prev(0), prev(nc), par(3, 0), par(3, nc), par(1, 0), par(1, nc)],
        out_specs=cur(0), out_shape=jax.ShapeDtypeStruct((L, D_FF), BF16),
        compiler_params=_params("parallel", "parallel", vmem=VMEM_BIG))(h, h, h, h, conv_w, conv_w, conv_b, conv_b)


def _conv_act_bwd_call(da, h, conv_w, conv_b):
    L = h.shape[0]
    tl = _fit(L, 512)
    nc = D_FF // CONV_TILE
    cur, prev, par = _conv_specs(tl, CONV_TILE, True)

    def body(da_ref, hg_ref, hv_ref, pg_ref, pv_ref, wg_ref, wv_ref, bg_ref, bv_ref,
             dg_ref, dv_ref, dwg_ref, dwv_ref, dbg_ref, dbv_ref):
        first = pl.program_id(1) == 0
        not_first = (pl.program_id(1) > 0).astype(F32)
        par = (wg_ref[...], bg_ref[...], wv_ref[...], bv_ref[...])
        col = lambda t: jnp.sum(t, axis=0, keepdims=True)

        def grads(da_, hg, hv, g1, g2, v1, v2):
            gate, val = _gated(*par, hg, hv, g1, g2, v1, v2)
            act, slope = _gelu_and_grad(gate)
            dgate = da_ * val * slope
            dval = da_ * act
            sums = (jnp.concatenate([col(dgate * g2), col(dgate * g1), col(dgate * hg)], axis=0),
                    jnp.concatenate([col(dval * v2), col(dval * v1), col(dval * hv)], axis=0), col(dgate), col(dval))
            return dgate, dval, sums

        da_, hg, hv = da_ref[...], hg_ref[...], hv_ref[...]
        shifted = (_down(hg, 1), _down(hg, 2), _down(hv, 1), _down(hv, 2))
        dgate, dval, whole = grads(da_, hg, hv, *shifted)
        dg_ref[...] = dgate.astype(BF16)
        dv_ref[...] = dval.astype(BF16)
        edge = lambda t: t[:HALO]
        _, _, wrapped = grads(edge(da_), edge(hg), edge(hv), *[edge(s) for s in shifted])
        eg, ev, bg, bv = edge(hg), edge(hv), pg_ref[...] * not_first, pv_ref[...] * not_first
        dgate, dval, fixed = grads(edge(da_), eg, ev, _edge_down(eg, bg, 1), _edge_down(eg, bg, 2),
                                   _edge_down(ev, bv, 1), _edge_down(ev, bv, 2))
        dg_ref[:HALO, :] = dgate.astype(BF16)
        dv_ref[:HALO, :] = dval.astype(BF16)
        for ref, a, b, c in zip((dwg_ref, dwv_ref, dbg_ref, dbv_ref), whole, wrapped, fixed):
            _acc(ref, first, a - b + c)

    act = jax.ShapeDtypeStruct((L, D_FF), BF16)
    w3 = jax.ShapeDtypeStruct((3, D_FF), F32)
    w1 = jax.ShapeDtypeStruct((1, D_FF), F32)
    return pl.pallas_call(
        body, name="conv_act_bwd", grid=(nc, L // tl),
        in_specs=[cur(0), cur(0), cur(nc), prev(0), prev(nc), par(3, 0), par(3, nc), par(1, 0), par(1, nc)],
        out_specs=[cur(0), cur(0), par(3, 0), par(3, 0), par(1, 0), par(1, 0)],
        out_shape=[act, act, w3, w3, w1, w1],
        compiler_params=_params("parallel", "arbitrary"))(da, h, h, h, h, conv_w, conv_w, conv_b, conv_b)


def _conv_t_call(dgate, dval, conv_w):
    L = dgate.shape[0]
    tl = _fit(L, 512)
    nc = D_FF // CONV_WIDE
    nh = tl // HALO

    def body(dg_ref, dv_ref, ng_ref, nv_ref, w_ref, o_ref):
        not_last = (pl.program_id(0) < L // tl - 1).astype(F32)

        def emit(d_ref, n_ref):
            c = d_ref[...].astype(F32)
            w = w_ref[...]
            o_ref[...] = _conv3(w, 0.0, c, pltpu.roll(c, tl - 1, 0), pltpu.roll(c, tl - 2, 0)).astype(BF16)
            edge, after = c[tl - HALO:], n_ref[...].astype(F32) * not_last
            o_ref[tl - HALO:, :] = _conv3(w, 0.0, edge, _edge_up(edge, after, 1), _edge_up(edge, after, 2)).astype(BF16)

        pl.when(pl.program_id(1) < nc)(lambda: emit(dg_ref, ng_ref))
        pl.when(pl.program_id(1) >= nc)(lambda: emit(dv_ref, nv_ref))

    gate_col = lambda j: jnp.minimum(j, nc - 1)
    val_col = lambda j: jnp.maximum(j - nc, 0)
    after_row = lambda i: jnp.minimum((i + 1) * nh, L // HALO - 1)
    tile = lambda col: pl.BlockSpec((tl, CONV_WIDE), lambda i, j: (i, col(j)))
    after = lambda col: pl.BlockSpec((HALO, CONV_WIDE), lambda i, j: (after_row(i), col(j)))
    return pl.pallas_call(
        body, name="conv_t", grid=(L // tl, 2 * nc),
        in_specs=[tile(gate_col), tile(val_col), after(gate_col), after(val_col), pl.BlockSpec((3, CONV_WIDE), lambda i, j: (0, j))],
        out_specs=pl.BlockSpec((tl, CONV_WIDE), lambda i, j: (i, j)),
        out_shape=jax.ShapeDtypeStruct((L, 2 * D_FF), BF16),
        compiler_params=_params("parallel", "parallel"))(dgate, dval, dgate, dval, conv_w)


def _glu_call(y1, w_glu, b_glu):
    L, n = y1.shape
    tl = _fit(L, 512)

    def body(y_ref, w_ref, b_ref, o_ref):
        y2 = _gelu(y_ref[...])
        z = _dot(y2.astype(BF16), w_ref[...], NN) + b_ref[...]
        o_ref[...] = (y2 * _sigmoid(z)).astype(BF16)

    return pl.pallas_call(
        body, name="glu", grid=(L // tl,), in_specs=[_row(tl, n), _full((n, n)), _full((1, n))],
        out_specs=_row(tl, n), out_shape=jax.ShapeDtypeStruct((L, n), BF16),
        compiler_params=_params("parallel"))(y1, w_glu, b_glu)


def _glu_bwd_call(dout, y1, w_glu, b_glu):
    L, n = y1.shape
    tl = _fit(L, 512)

    def body(do_ref, y_ref, w_ref, b_ref, dy_ref, dw_ref, db_ref):
        first = pl.program_id(0) == 0
        y1_ = y_ref[...]
        y2, slope = _gelu_and_grad(y1_)
        y2b = y2.astype(BF16)
        w = w_ref[...]
        sg = _sigmoid(_dot(y2b, w, NN) + b_ref[...])
        dout_ = do_ref[...].astype(F32)
        dz = dout_ * y2 * sg * (1.0 - sg)
        dzb = dz.astype(BF16)
        dy2 = dout_ * sg + _dot(dzb, w, NT)
        dy_ref[...] = dy2 * slope
        _acc(dw_ref, first, _dot(y2b, dzb, TN))
        _acc(db_ref, first, jnp.sum(dz, axis=0, keepdims=True))

    return pl.pallas_call(
        body, name="glu_bwd", grid=(L // tl,),
        in_specs=[_row(tl, n), _row(tl, n), _full((n, n)), _full((1, n))],
        out_specs=[_row(tl, n), _full((n, n)), _full((1, n))],
        out_shape=[jax.ShapeDtypeStruct((L, n), F32), jax.ShapeDtypeStruct((n, n), F32), jax.ShapeDtypeStruct((1, n), F32)],
        compiler_params=_params("arbitrary"))(dout, y1, w_glu, b_glu)


ATTN_TILE = 1024
ATTN_SCALE = 1.0 / math.sqrt(QK_HEAD)


ATTN_HEADS = 2
ATTN_GROUPS = N_HEADS // ATTN_HEADS
LOG2E = 1.0 / math.log(2.0)
Q_PRESCALE = ATTN_SCALE * LOG2E
ANY_SPEC = pl.BlockSpec(memory_space=pl.ANY)


def _attn_fwd_call(q, kv, blocks):
    L = q.shape[0]
    t = _fit(L, ATTN_TILE)
    nq = L // t
    n = len(blocks)

    def body(q_ref, kv_ref, *refs):
        blk_refs, (o_ref, lse_ref), gat_refs = refs[:n], refs[n:n + 2], refs[n + 2:2 * n + 2]
        m_s, acc_s, send_sems, recv_sems, local_sems = refs[2 * n + 2:]
        g, i = pl.program_id(0), pl.program_id(1)
        start, forward, finish = _gather_phases(blk_refs, gat_refs, send_sems, recv_sems, local_sems)
        pl.when(jnp.logical_and(g == 0, i == 0))(start)
        m_s[...] = jnp.full((ATTN_HEADS, t, 1), NEG, F32)
        acc_s[...] = jnp.zeros((ATTN_HEADS, t, LANES), F32)
        below = lax.broadcasted_iota(jnp.int32, (t, t), 1) <= lax.broadcasted_iota(jnp.int32, (t, t), 0)

        def block_step(kb, on_diagonal):
            rows = pl.ds(pl.multiple_of(kb * t, t), t)
            for a in range(ATTN_HEADS):
                s = _dot(q_ref[:, a * LANES:(a + 1) * LANES], kv_ref[rows, 2 * a * LANES:(2 * a + 1) * LANES], NT)
                if on_diagonal:
                    s = jnp.where(below, s, NEG)
                m_prev = m_s[a]
                m_new = jnp.maximum(m_prev, jnp.max(s, axis=1, keepdims=True))
                p = jnp.exp2(s - m_new)
                pv = _dot(p.astype(BF16), kv_ref[rows, (2 * a + 1) * LANES:(2 * a + 2) * LANES], NN)
                acc_s[a] = jnp.exp2(m_prev - m_new) * acc_s[a] + pv
                m_s[a] = m_new

        def step(kb, carry):
            block_step(kb, False)
            return carry

        lax.fori_loop(0, i, step, 0)
        block_step(i, True)
        lane = lax.broadcasted_iota(jnp.int32, (t, LANES), 1)
        for a in range(ATTN_HEADS):
            acc = acc_s[a]
            l = jnp.sum(jnp.where(lane == V_HEAD, acc, 0.0), axis=1, keepdims=True)
            o_ref[:, a * LANES:(a + 1) * LANES] = (acc / l).astype(BF16)
            lse_ref[a] = m_s[a] + jnp.log(l) * LOG2E
        pl.when(jnp.logical_and(g == (3 * ATTN_GROUPS) // 4, i == 0))(forward)
        pl.when(jnp.logical_and(g == ATTN_GROUPS - 1, i == nq - 1))(finish)

    gw = ATTN_HEADS * LANES
    return pl.pallas_call(
        body, name="attn_fwd", grid=(ATTN_GROUPS, nq),
        in_specs=[pl.BlockSpec((t, gw), lambda g, i: (i, g)),
                  pl.BlockSpec((L, 2 * gw), lambda g, i: (0, g))] + [ANY_SPEC] * n,
        out_specs=[pl.BlockSpec((t, gw), lambda g, i: (i, g)),
                   pl.BlockSpec((ATTN_HEADS, t, 1), lambda g, i: (g, i, 0))] + [ANY_SPEC] * n,
        out_shape=[jax.ShapeDtypeStruct((L, HEAD_PAD), BF16), jax.ShapeDtypeStruct((N_HEADS, L, 1), F32)]
        + [jax.ShapeDtypeStruct((N_DEV,) + b.shape, b.dtype) for b in blocks],
        scratch_shapes=[pltpu.VMEM((ATTN_HEADS, t, 1), F32), pltpu.VMEM((ATTN_HEADS, t, LANES), F32)] + _comm_sems(n),
        compiler_params=_params("arbitrary", "arbitrary", vmem=VMEM_BIG))(q, kv, *blocks)


def _attn_bwd_call(q, kv, o, do, lse, parts, blocks):
    L = q.shape[0]
    t = _fit(L, ATTN_TILE)
    nq = L // t
    n1, n = len(parts), len(parts) + len(blocks)

    def body(q_ref, do_ref, o_ref, lse_ref, kv_ref, *refs):
        in_refs, (dq_ref, dkv_ref), out_refs = refs[:n], refs[n:n + 2], refs[n + 2:2 * n + 2]
        dk_s, dv_s = refs[2 * n + 2:2 * n + 4]
        g, j = pl.program_id(0), pl.program_id(1)
        start, finish = _exchange_phases(in_refs[:n1], out_refs[:n1], *refs[2 * n + 4:2 * n + 7])
        start_blocks, finish_blocks = _exchange_phases(in_refs[n1:], out_refs[n1:], *refs[2 * n + 7:], same_source=True)

        @pl.when(jnp.logical_and(g == 0, j == 0))
        def _():
            start()
            start_blocks()

        @pl.when(j == 0)
        def _():
            dq_ref[...] = jnp.zeros((L, ATTN_HEADS * LANES), F32)

        dk_s[...] = jnp.zeros((ATTN_HEADS, t, LANES), F32)
        dv_s[...] = jnp.zeros((ATTN_HEADS, t, LANES), F32)
        below = lax.broadcasted_iota(jnp.int32, (t, t), 1) <= lax.broadcasted_iota(jnp.int32, (t, t), 0)

        def block_step(i, on_diagonal):
            rows = pl.ds(pl.multiple_of(i * t, t), t)
            for a in range(ATTN_HEADS):
                lanes = slice(a * LANES, (a + 1) * LANES)
                qi = q_ref[rows, lanes]
                doi = do_ref[rows, lanes]
                kblk = kv_ref[:, 2 * a * LANES:(2 * a + 1) * LANES]
                delta = jnp.sum(doi.astype(F32) * o_ref[rows, lanes].astype(F32), axis=1, keepdims=True)
                s = _dot(qi, kblk, NT)
                if on_diagonal:
                    s = jnp.where(below, s, NEG)
                p = jnp.exp2(s - lse_ref[a, rows, :])
                dv_s[a] += _dot(p.astype(BF16), doi, TN)
                ds = (p * (_dot(doi, kv_ref[:, (2 * a + 1) * LANES:(2 * a + 2) * LANES], NT) - delta)).astype(BF16)
                dk_s[a] += _dot(ds, qi, TN)
                dq_ref[rows, lanes] += _dot(ds, kblk, NN) * ATTN_SCALE

        def step(i, carry):
            block_step(i, False)
            return carry

        block_step(j, True)
        lax.fori_loop(j + 1, nq, step, 0)
        for a in range(ATTN_HEADS):
            dkv_ref[:, 2 * a * LANES:(2 * a + 1) * LANES] = dk_s[a] * (1.0 / LOG2E)
            dkv_ref[:, (2 * a + 1) * LANES:(2 * a + 2) * LANES] = dv_s[a]
        @pl.when(jnp.logical_and(g == ATTN_GROUPS - 1, j == nq - 1))
        def _():
            finish()
            finish_blocks()

    gw = ATTN_HEADS * LANES
    whole = lambda: pl.BlockSpec((L, gw), lambda g, j: (0, g))
    acc = pltpu.VMEM((ATTN_HEADS, t, LANES), F32)
    return pl.pallas_call(
        body, name="attn_bwd", grid=(ATTN_GROUPS, nq),
        in_specs=[whole(), whole(), whole(), pl.BlockSpec((ATTN_HEADS, L, 1), lambda g, j: (g, 0, 0)),
                  pl.BlockSpec((t, 2 * gw), lambda g, j: (j, g))] + [ANY_SPEC] * n,
        out_specs=[whole(), pl.BlockSpec((t, 2 * gw), lambda g, j: (j, g))] + [ANY_SPEC] * n,
        out_shape=[jax.ShapeDtypeStruct((L, HEAD_PAD), F32), jax.ShapeDtypeStruct((L, 2 * HEAD_PAD), F32)]
        + [jax.ShapeDtypeStruct(p.shape, p.dtype) for p in parts]
        + [jax.ShapeDtypeStruct((N_DEV,) + b.shape, b.dtype) for b in blocks],
        scratch_shapes=[acc, acc] + _comm_sems(n1) + _comm_sems(n - n1),
        compiler_params=_params("arbitrary", "arbitrary", vmem=VMEM_BIG))(q, do, o, lse, kv, *parts, *blocks)


def _disc(lr, li, ldt, br, bi):
    dt = jnp.exp(ldt)
    mag = jnp.exp(lr * dt)
    ang = li * dt
    a_re, a_im = mag * jnp.cos(ang), mag * jnp.sin(ang)
    den = lr * lr + li * li
    n_re, n_im = a_re - 1.0, a_im
    z_re = (n_re * lr + n_im * li) / den
    z_im = (n_im * lr - n_re * li) / den
    return a_re, a_im, z_re * br - z_im * bi, z_re * bi + z_im * br


def _disc_call(lr, li, ldt, br, bi):
    def body(lr_ref, li_ref, ldt_ref, br_ref, bi_ref, ar_ref, ai_ref, bbr_ref, bbi_ref):
        ar_ref[...], ai_ref[...], bbr_ref[...], bbi_ref[...] = _disc(
            lr_ref[...], li_ref[...], ldt_ref[...], br_ref[...], bi_ref[...])

    c1 = jax.ShapeDtypeStruct((SSM_NSTATE, 1), F32)
    c16 = jax.ShapeDtypeStruct((SSM_NSTATE, SSM_GROUP), F32)
    return pl.pallas_call(body, name="ssm_disc", out_shape=[c1, c1, c16, c16])(lr, li, ldt, br, bi)


def _disc_bwd_call(lr, li, ldt, br, bi, dar, dai, dbbr, dbbi):
    def body(lr_ref, li_ref, ldt_ref, br_ref, bi_ref, dar_ref, dai_ref, dbbr_ref, dbbi_ref,
             dlr_ref, dli_ref, dldt_ref, dbr_ref, dbi_ref):
        _, vjp = jax.vjp(_disc, lr_ref[...], li_ref[...], ldt_ref[...], br_ref[...], bi_ref[...])
        dlr_ref[...], dli_ref[...], dldt_ref[...], dbr_ref[...], dbi_ref[...] = vjp(
            (dar_ref[...], dai_ref[...], dbbr_ref[...], dbbi_ref[...]))

    c1 = jax.ShapeDtypeStruct((SSM_NSTATE, 1), F32)
    c16 = jax.ShapeDtypeStruct((SSM_NSTATE, SSM_GROUP), F32)
    return pl.pallas_call(body, name="ssm_disc_bwd", out_shape=[c1, c1, c1, c16, c16])(
        lr, li, ldt, br, bi, dar, dai, dbbr, dbbi)


SSM_ROWS = 1024
SSM_CW = SSM_NSTATE // SSM_CHUNKS
SSM_CU = SSM_WIDTH // SSM_CHUNKS


def _cmul(ar, ai, br, bi):
    return ar * br - ai * bi, ar * bi + ai * br


def _power(ar1, ai1, n):
    res, base = None, (ar1, ai1)
    while n:
        if n & 1:
            res = base if res is None else _cmul(res[0], res[1], base[0], base[1])
        n >>= 1
        if n:
            base = _cmul(base[0], base[1], base[0], base[1])
    return res


def _tile(k):
    return pl.ds(pl.multiple_of(k * 8, 8), 8)


def _ssm_fwd_call(u, a_re, a_im, bb_re, bb_im, cm_re, cm_im, d_skip):
    L = u.shape[0]
    seg = L // 8
    rb = _fit(L, SSM_ROWS)

    def body(u_ref, ar_ref, ai_ref, bbr_ref, bbi_ref, cmr_ref, cmi_ref, d_ref, y_ref, sre_hbm, sim_hbm,
             s_re, s_im, sems):
        q = pl.program_id(0)

        def bu_step(r, c):
            rows = pl.ds(pl.multiple_of(r * rb, rb), rb)
            ub = u_ref[rows, :].astype(BF16)
            s_re[rows, :] = _dot(ub, bbr_ref[0], NN)
            s_im[rows, :] = _dot(ub, bbi_ref[0], NN)
            return c

        lax.fori_loop(0, L // rb, bu_step, 0)
        ar1, ai1 = ar_ref[...], ai_ref[...]
        ar = jnp.broadcast_to(ar1, (8, SSM_CW))
        ai = jnp.broadcast_to(ai1, (8, SSM_CW))

        def local(k, c):
            nr, ni = _cmul(ar, ai, c[0], c[1])
            nr = nr + s_re[_tile(k), :]
            ni = ni + s_im[_tile(k), :]
            s_re[_tile(k), :] = nr
            s_im[_tile(k), :] = ni
            return nr, ni

        zero8 = jnp.zeros((8, SSM_CW), F32)
        lax.fori_loop(0, seg, local, (zero8, zero8))
        pr, pi = _power(ar1, ai1, seg)
        end_r = s_re[pl.ds((seg - 1) * 8, 8), :]
        end_i = s_im[pl.ds((seg - 1) * 8, 8), :]
        er = jnp.zeros((1, SSM_CW), F32)
        ei = jnp.zeros((1, SSM_CW), F32)
        rows_r, rows_i = [er], [ei]
        for j in range(7):
            tr, ti = _cmul(pr, pi, er, ei)
            er, ei = end_r[j:j + 1] + tr, end_i[j:j + 1] + ti
            rows_r.append(er)
            rows_i.append(ei)
        e_r = jnp.concatenate(rows_r, axis=0)
        e_i = jnp.concatenate(rows_i, axis=0)

        def fix(k, c):
            wr, wi = _cmul(c[0], c[1], ar, ai)
            fr, fi = _cmul(wr, wi, e_r, e_i)
            s_re[_tile(k), :] += fr
            s_im[_tile(k), :] += fi
            return wr, wi

        lax.fori_loop(0, seg, fix, (jnp.ones((8, SSM_CW), F32), zero8))
        out_r = pltpu.make_async_copy(s_re, sre_hbm.at[q], sems.at[0])
        out_i = pltpu.make_async_copy(s_im, sim_hbm.at[q], sems.at[1])
        out_r.start()
        out_i.start()

        def y_step(r, c):
            rows = pl.ds(pl.multiple_of(r * rb, rb), rb)
            y = _dot(s_re[rows, :].astype(BF16), cmr_ref[0], NN) - _dot(s_im[rows, :].astype(BF16), cmi_ref[0], NN)
            y_ref[rows, :] = y + d_ref[...] * u_ref[rows, :]
            return c

        lax.fori_loop(0, L // rb, y_step, 0)
        out_r.wait()
        out_i.wait()

    chunk = lambda rows, cols: pl.BlockSpec((rows, cols), lambda q: (0, q))
    mat = lambda r, c: pl.BlockSpec((1, r, c), lambda q: (q, 0, 0))
    anyspec = pl.BlockSpec(memory_space=pl.ANY)
    states = jax.ShapeDtypeStruct((SSM_CHUNKS, L, SSM_CW), F32)
    return pl.pallas_call(
        body, name="ssm_fwd", grid=(SSM_CHUNKS,),
        in_specs=[chunk(L, SSM_CU), chunk(1, SSM_CW), chunk(1, SSM_CW), mat(SSM_CU, SSM_CW), mat(SSM_CU, SSM_CW),
                  mat(SSM_CW, SSM_CU), mat(SSM_CW, SSM_CU), chunk(1, SSM_CU)],
        out_specs=[chunk(L, SSM_CU), anyspec, anyspec],
        out_shape=[jax.ShapeDtypeStruct((L, SSM_WIDTH), F32), states, states],
        scratch_shapes=[pltpu.VMEM((L, SSM_CW), F32), pltpu.VMEM((L, SSM_CW), F32), pltpu.SemaphoreType.DMA((2,))],
        compiler_params=_params("arbitrary", vmem=VMEM_BIG))(u, a_re, a_im, bb_re, bb_im, cm_re, cm_im, d_skip)


def _ssm_bwd_call(dy, u, s_re_all, s_im_all, a_re, a_im, bb_re, bb_im, cm_re, cm_im, d_skip):
    L = u.shape[0]
    seg = L // 8
    rb = _fit(L, SSM_ROWS)

    def body(dy_ref, u_ref, sre_hbm, sim_hbm, ar_ref, ai_ref, bbr_ref, bbi_ref, cmr_ref, cmi_ref, d_ref,
             du_ref, dbbr_ref, dbbi_ref, dcmr_ref, dcmi_ref, dar_ref, dai_ref, dd_ref,
             g_re, g_im, s_re, s_im, sems):
        q = pl.program_id(0)
        in_r = pltpu.make_async_copy(sre_hbm.at[q], s_re, sems.at[0])
        in_i = pltpu.make_async_copy(sim_hbm.at[q], s_im, sems.at[1])
        in_r.start()
        in_i.start()

        def ds_step(r, c):
            rows = pl.ds(pl.multiple_of(r * rb, rb), rb)
            dyb = dy_ref[rows, :].astype(BF16)
            g_re[rows, :] = _dot(dyb, cmr_ref[0], NT)
            g_im[rows, :] = -_dot(dyb, cmi_ref[0], NT)
            return c

        lax.fori_loop(0, L // rb, ds_step, 0)
        ar1, ai1 = ar_ref[...], ai_ref[...]
        ar = jnp.broadcast_to(ar1, (8, SSM_CW))
        nai = jnp.broadcast_to(-ai1, (8, SSM_CW))

        def local(kk, c):
            k = seg - 1 - kk
            nr, ni = _cmul(ar, nai, c[0], c[1])
            nr = nr + g_re[_tile(k), :]
            ni = ni + g_im[_tile(k), :]
            g_re[_tile(k), :] = nr
            g_im[_tile(k), :] = ni
            return nr, ni

        zero8 = jnp.zeros((8, SSM_CW), F32)
        lax.fori_loop(0, seg, local, (zero8, zero8))
        pr, pi = _power(ar1, -ai1, seg)
        head_r = g_re[pl.ds(0, 8), :]
        head_i = g_im[pl.ds(0, 8), :]
        fr = jnp.zeros((1, SSM_CW), F32)
        fi = jnp.zeros((1, SSM_CW), F32)
        rows_r, rows_i = [fr], [fi]
        for j in range(6, -1, -1):
            tr, ti = _cmul(pr, pi, fr, fi)
            fr, fi = head_r[j + 1:j + 2] + tr, head_i[j + 1:j + 2] + ti
            rows_r.insert(0, fr)
            rows_i.insert(0, fi)
        f_r = jnp.concatenate(rows_r, axis=0)
        f_i = jnp.concatenate(rows_i, axis=0)
        in_r.wait()
        in_i.wait()

        def fixed(k, wr, wi):
            xr, xi = _cmul(wr, wi, f_r, f_i)
            gr = g_re[_tile(k), :] + xr
            gi = g_im[_tile(k), :] + xi
            g_re[_tile(k), :] = gr
            g_im[_tile(k), :] = gi
            return gr, gi

        def fix(kk, c):
            k = seg - 1 - kk
            wr, wi = _cmul(c[0], c[1], ar, nai)
            gr, gi = fixed(k, wr, wi)
            pr_, pi_ = s_re[_tile(k - 1), :], s_im[_tile(k - 1), :]
            return wr, wi, c[2] + gr * pr_ + gi * pi_, c[3] + gi * pr_ - gr * pi_

        wr, wi, acc_r, acc_i = lax.fori_loop(0, seg - 1, fix, (jnp.ones((8, SSM_CW), F32), zero8, zero8, zero8))
        wr, wi = _cmul(wr, wi, ar, nai)
        gr, gi = fixed(0, wr, wi)
        row8 = lax.broadcasted_iota(jnp.int32, (8, SSM_CW), 0)
        pr_ = jnp.where(row8 > 0, pltpu.roll(s_re[pl.ds((seg - 1) * 8, 8), :], 1, 0), 0.0)
        pi_ = jnp.where(row8 > 0, pltpu.roll(s_im[pl.ds((seg - 1) * 8, 8), :], 1, 0), 0.0)
        acc_r = acc_r + gr * pr_ + gi * pi_
        acc_i = acc_i + gi * pr_ - gr * pi_
        dar_ref[...] = jnp.sum(acc_r, axis=0, keepdims=True)
        dai_ref[...] = jnp.sum(acc_i, axis=0, keepdims=True)

        dbbr_ref[...] = jnp.zeros((1, SSM_CU, SSM_CW), F32)
        dbbi_ref[...] = jnp.zeros((1, SSM_CU, SSM_CW), F32)
        dcmr_ref[...] = jnp.zeros((1, SSM_CW, SSM_CU), F32)
        dcmi_ref[...] = jnp.zeros((1, SSM_CW, SSM_CU), F32)
        dd_ref[...] = jnp.zeros((1, SSM_CU), F32)

        def grad_step(r, c):
            rows = pl.ds(pl.multiple_of(r * rb, rb), rb)
            ub, dyv = u_ref[rows, :], dy_ref[rows, :]
            ubb, dyb = ub.astype(BF16), dyv.astype(BF16)
            grb, gib = g_re[rows, :].astype(BF16), g_im[rows, :].astype(BF16)
            dbbr_ref[0] += _dot(ubb, grb, TN)
            dbbi_ref[0] += _dot(ubb, gib, TN)
            dcmr_ref[0] += _dot(s_re[rows, :].astype(BF16), dyb, TN)
            dcmi_ref[0] -= _dot(s_im[rows, :].astype(BF16), dyb, TN)
            du_ref[rows, :] = _dot(grb, bbr_ref[0], NT) + _dot(gib, bbi_ref[0], NT) + d_ref[...] * dyv
            dd_ref[...] += jnp.sum(dyv * ub, axis=0, keepdims=True)
            return c

        lax.fori_loop(0, L // rb, grad_step, 0)

    chunk = lambda rows, cols: pl.BlockSpec((rows, cols), lambda q: (0, q))
    mat = lambda r, c: pl.BlockSpec((1, r, c), lambda q: (q, 0, 0))
    anyspec = pl.BlockSpec(memory_space=pl.ANY)
    big = lambda: pltpu.VMEM((L, SSM_CW), F32)
    return pl.pallas_call(
        body, name="ssm_bwd", grid=(SSM_CHUNKS,),
        in_specs=[chunk(L, SSM_CU), chunk(L, SSM_CU), anyspec, anyspec, chunk(1, SSM_CW), chunk(1, SSM_CW),
                  mat(SSM_CU, SSM_CW), mat(SSM_CU, SSM_CW), mat(SSM_CW, SSM_CU), mat(SSM_CW, SSM_CU), chunk(1, SSM_CU)],
        out_specs=[chunk(L, SSM_CU), mat(SSM_CU, SSM_CW), mat(SSM_CU, SSM_CW), mat(SSM_CW, SSM_CU), mat(SSM_CW, SSM_CU),
                   chunk(1, SSM_CW), chunk(1, SSM_CW), chunk(1, SSM_CU)],
        out_shape=[jax.ShapeDtypeStruct((L, SSM_WIDTH), F32),
                   jax.ShapeDtypeStruct((SSM_CHUNKS, SSM_CU, SSM_CW), F32), jax.ShapeDtypeStruct((SSM_CHUNKS, SSM_CU, SSM_CW), F32),
                   jax.ShapeDtypeStruct((SSM_CHUNKS, SSM_CW, SSM_CU), F32), jax.ShapeDtypeStruct((SSM_CHUNKS, SSM_CW, SSM_CU), F32),
                   jax.ShapeDtypeStruct((1, SSM_NSTATE), F32), jax.ShapeDtypeStruct((1, SSM_NSTATE), F32),
                   jax.ShapeDtypeStruct((1, SSM_WIDTH), F32)],
        scratch_shapes=[big(), big(), big(), big(), pltpu.SemaphoreType.DMA((2,))],
        compiler_params=_params("arbitrary", vmem=VMEM_BIG))(
            dy, u, s_re_all, s_im_all, a_re, a_im, bb_re, bb_im, cm_re, cm_im, d_skip)


def _place():
    return lax.axis_index("x"), lax.axis_index("y"), lax.axis_index("c")


def _small_gather_call(blocks, name):
    n = len(blocks)

    def body(*refs):
        start, finish = _exchange_phases(refs[:n], refs[n:2 * n], *refs[2 * n:], same_source=True)
        start()
        finish()

    return pl.pallas_call(
        body, name=name, in_specs=[ANY_SPEC] * n, out_specs=[ANY_SPEC] * n,
        out_shape=[jax.ShapeDtypeStruct((N_DEV,) + b.shape, b.dtype) for b in blocks],
        scratch_shapes=_comm_sems(n))(*blocks)


def _comm_sems(n):
    return [pltpu.SemaphoreType.DMA((7 * n,)), pltpu.SemaphoreType.DMA((7 * n,)), pltpu.SemaphoreType.DMA((n,))]


def _gather_phases(x_refs, out_refs, send_sems, recv_sems, local_sems):
    x, y, c = _place()
    me, sibling = (x, y, c), (x, y, 1 - c)
    chips = [(1 - x, y), (x, 1 - y), (1 - x, 1 - y)]
    n = len(x_refs)

    def copy(k, a, blk, to, from_input=False):
        slot = out_refs[a].at[4 * blk[0] + 2 * blk[1] + blk[2]]
        return pltpu.make_async_remote_copy(
            src_ref=x_refs[a] if from_input else slot, dst_ref=slot,
            send_sem=send_sems.at[k * n + a], recv_sem=recv_sems.at[k * n + a], device_id=to, device_id_type=MESH_ID)

    mine = [pltpu.make_async_copy(x_refs[a], out_refs[a].at[4 * x + 2 * y + c], local_sems.at[a]) for a in range(n)]
    first, passed = [], []
    for a in range(n):
        first.append(copy(0, a, me, sibling, True))
        first += [copy(1 + j, a, me, (*chip, c), True) for j, chip in enumerate(chips)]
        passed += [copy(4 + j, a, (*chip, c), sibling) for j, chip in enumerate(chips)]

    def start():
        for cp in mine + first:
            cp.start()

    def forward():
        for j, chip in enumerate(chips):
            for a in range(n):
                copy(1 + j, a, (*chip, c), me).wait_recv()
                passed[3 * a + j].start()

    def finish():
        for a in range(n):
            copy(0, a, sibling, me).wait_recv()
            for j, chip in enumerate(chips):
                copy(4 + j, a, (*chip, 1 - c), me).wait_recv()
        for cp in first + passed:
            cp.wait_send()
        for cp in mine:
            cp.wait()

    return start, forward, finish


def _exchange_phases(p_refs, out_refs, send_sems, recv_sems, local_sems, same_source=False):
    x, y, c = _place()
    me = 4 * x + 2 * y + c
    n = len(p_refs)

    def flip(k):
        px = 1 - x if k & 4 else x
        py = 1 - y if k & 2 else y
        pc = 1 - c if k & 1 else c
        return (px, py, pc), 4 * px + 2 * py + pc

    def source(a, slot):
        return p_refs[a] if same_source else p_refs[a].at[slot]

    def copy(k, a, landing):
        peer, peer_slot = flip(k)
        return pltpu.make_async_remote_copy(
            src_ref=source(a, peer_slot), dst_ref=out_refs[a].at[peer_slot if landing else me],
            send_sem=send_sems.at[(k - 1) * n + a], recv_sem=recv_sems.at[(k - 1) * n + a],
            device_id=peer, device_id_type=MESH_ID)

    mine = [pltpu.make_async_copy(source(a, me), out_refs[a].at[me], local_sems.at[a]) for a in range(n)]
    sends = [copy(k, a, False) for k in range(1, N_DEV) for a in range(n)]

    def start():
        for cp in mine + sends:
            cp.start()

    def finish():
        for k in range(1, N_DEV):
            for a in range(n):
                copy(k, a, True).wait_recv()
        for cp in sends:
            cp.wait_send()
        for cp in mine:
            cp.wait()

    return start, finish


def _adam_math(g, w, m, v):
    c1 = 1.0 / (1.0 - ADAM_B1 ** ADAM_STEP)
    c2 = 1.0 / (1.0 - ADAM_B2 ** ADAM_STEP)
    m_new = ADAM_B1 * m + (1.0 - ADAM_B1) * g
    v_new = ADAM_B2 * v + (1.0 - ADAM_B2) * (g * g)
    delta = -ADAM_LR * ((m_new * c1) / (jnp.sqrt(v_new * c2) + ADAM_EPS) + ADAM_WD * w)
    return g, delta, m_new, v_new


def _sum_slices(s_ref):
    g = s_ref[0].astype(F32)
    for k in range(1, N_DEV):
        g = g + s_ref[k].astype(F32)
    return g


def _adam_call(slices, w, m, v, name):
    d1, rest = w.shape[1], w.shape[2:]
    zeros = (0,) * len(rest)
    by_lanes = len(rest) == 1 and d1 > 256 and d1 % 16 != 0
    if by_lanes:
        tile = _fit(rest[0], 256)
        steps = rest[0] // tile
        own = pl.BlockSpec((1, d1, tile), lambda i: (0, 0, i))
        sl = pl.BlockSpec((N_DEV, 1, d1, tile), lambda i: (0, 0, 0, i))
    else:
        tile = _fit(d1, 256, 16) if len(rest) == 1 else _fit(d1, 8, 8)
        steps = d1 // tile
        own = pl.BlockSpec((1, tile) + rest, lambda i: (0, i) + zeros)
        sl = pl.BlockSpec((N_DEV, 1, tile) + rest, lambda i: (0, 0, i) + zeros)

    def body(s_ref, w_ref, m_ref, v_ref, g_ref, d_ref, mo_ref, vo_ref):
        g_ref[...], d_ref[...], mo_ref[...], vo_ref[...] = _adam_math(_sum_slices(s_ref), w_ref[...], m_ref[...], v_ref[...])

    out = jax.ShapeDtypeStruct(w.shape, F32)
    return pl.pallas_call(
        body, name=name, grid=(steps,), in_specs=[sl, own, own, own],
        out_specs=[own, own, own, own], out_shape=[out, out, out, out],
        compiler_params=_params("parallel"))(slices, w, m, v)


def _adam_small_call(rows_all, row_params, slices, params):
    nr, n = len(row_params), len(row_params) + len(params)

    def row_sum(rows_ref, a, width):
        g = rows_ref[0, pl.ds(a, 1), pl.ds(0, width)]
        for k in range(1, N_DEV):
            g = g + rows_ref[k, pl.ds(a, 1), pl.ds(0, width)]
        return g

    def body(rows_ref, *refs):
        slice_refs, wmv, outs = refs[:n - nr], refs[n - nr:n - nr + 3 * n], refs[n - nr + 3 * n:]
        outs[4 * n][...] = row_sum(rows_ref, nr, LANES)
        for a in range(n):
            w_ref, m_ref, v_ref = wmv[3 * a:3 * a + 3]
            if a < nr:
                g = row_sum(rows_ref, a, w_ref.shape[1])
            else:
                g = _sum_slices(slice_refs[a - nr])
            res = _adam_math(g, w_ref[...], m_ref[...], v_ref[...])
            for r in range(4):
                outs[4 * a + r][...] = res[r]

    every = list(row_params) + list(params)
    flat = pl.pallas_call(
        body, name="adam_small",
        out_shape=[jax.ShapeDtypeStruct(w.shape, F32) for w, _, _ in every for _ in range(4)]
        + [jax.ShapeDtypeStruct((1, LANES), F32)],
        compiler_params=pltpu.CompilerParams(vmem_limit_bytes=VMEM_BIG),
    )(rows_all, *slices, *[t for wmv in every for t in wmv])
    return [flat[4 * a:4 * a + 4] for a in range(n)], flat[4 * n][0, 0]


BIG = (("w_in", 1024, 404, 1), ("w_uq", 384, 96, 1), ("w_uk", 256, 64, 1), ("w_uv", 256, 64, 1),
       ("w_glu", 64, 512, 0), ("w_branch_attn", 512, 128, 1), ("w_branch_ssm", 512, 128, 1),
       ("w_out", 128, 1024, 0), ("w_up", 1024, 704, 1), ("w_down", 352, 1024, 0), ("conv_w", 3, 704, 1))
GATHER_FIRST, GATHER_PROJ, GATHER_LATER = BIG[:1], BIG[1:4], BIG[8:] + BIG[4:8]
GRADS_EARLY, GRADS_PROJ, GRADS_LAST = BIG[8:] + BIG[4:8], BIG[1:4], BIG[:1]
SMALL = (("mix_norm_pre", (1024,)), ("q_norm", (384,)), ("kv_norm", (256,)), ("ssm_lambda_re", (32, 64)),
         ("ssm_lambda_im", (32, 64)), ("ssm_log_dt", (32,)), ("ssm_b_re", (32, 64, 16)), ("ssm_b_im", (32, 64, 16)),
         ("ssm_c_re", (32, 16, 64)), ("ssm_c_im", (32, 16, 64)), ("ssm_d", (32, 16)), ("b_glu", (512,)),
         ("b_gate", (2048,)), ("mix_norm_post", (1024,)), ("ffn_norm_pre", (1024,)), ("conv_b", (5632,)),
         ("ffn_norm_post", (1024,)))


TRANSPOSED = ("w_in", "w_uq", "w_uk", "w_uv", "w_up")


STORED_SWAP = {**{name: (1, 2) for name in TRANSPOSED}, "ssm_b_re": (2, 3), "ssm_b_im": (2, 3), "ssm_d": (1, 2)}


def _stored(name, arr):
    return jnp.swapaxes(arr, *STORED_SWAP[name]) if name in STORED_SWAP else arr


def _to_slices(name, full, rows, cols, axis):
    if name in TRANSPOSED:
        return full.reshape(N_DEV, cols, rows)
    if axis == 1:
        return full.reshape(rows, N_DEV, cols).transpose(1, 0, 2)
    return full.reshape(N_DEV, rows, cols)


def _from_slices(name, parts, rows, cols, axis):
    if name in TRANSPOSED:
        return parts.reshape(N_DEV * cols, rows)
    if axis == 1:
        return parts.transpose(1, 0, 2).reshape(rows, N_DEV * cols)
    return parts.reshape(N_DEV * rows, cols)


def _time_perm(a, L):
    return a.reshape(8, L // 8, a.shape[-1]).transpose(1, 0, 2).reshape(L, a.shape[-1])


def _time_unperm(a, L):
    return a.reshape(L // 8, 8, a.shape[-1]).transpose(1, 0, 2).reshape(L, a.shape[-1])


def _block_diag(w, rows_first):
    eye = jnp.eye(8, dtype=w.dtype)
    g = w.reshape(SSM_CHUNKS, 8, w.shape[1], w.shape[2])
    return jnp.einsum("qgrc,gk->qgrkc", g, eye).reshape(SSM_CHUNKS, 8 * w.shape[1], 8 * w.shape[2])


def _block_diag_t(m, r, c):
    eye = jnp.eye(8, dtype=m.dtype)
    return jnp.einsum("qgrkc,gk->qgrc", m.reshape(SSM_CHUNKS, 8, r, 8, c), eye).reshape(SSM_GROUPS, r, c)


def kernel(x, positions, mix_norm_pre, w_in, q_norm, w_uq, kv_norm, w_uk, w_uv, ssm_lambda_re, ssm_lambda_im, ssm_log_dt, ssm_b_re, ssm_b_im, ssm_c_re, ssm_c_im, ssm_d, w_glu, b_glu, w_branch_attn, w_branch_ssm, b_gate, w_out, mix_norm_post, ffn_norm_pre, w_up, conv_w, conv_b, w_down, ffn_norm_post, loss_target, m_mix_norm_pre, m_w_in, m_q_norm, m_w_uq, m_kv_norm, m_w_uk, m_w_uv, m_ssm_lambda_re, m_ssm_lambda_im, m_ssm_log_dt, m_ssm_b_re, m_ssm_b_im, m_ssm_c_re, m_ssm_c_im, m_ssm_d, m_w_glu, m_b_glu, m_w_branch_attn, m_w_branch_ssm, m_b_gate, m_w_out, m_mix_norm_post, m_ffn_norm_pre, m_w_up, m_conv_w, m_conv_b, m_w_down, m_ffn_norm_post, v_mix_norm_pre, v_w_in, v_q_norm, v_w_uq, v_kv_norm, v_w_uk, v_w_uv, v_ssm_lambda_re, v_ssm_lambda_im, v_ssm_log_dt, v_ssm_b_re, v_ssm_b_im, v_ssm_c_re, v_ssm_c_im, v_ssm_d, v_w_glu, v_b_glu, v_w_branch_attn, v_w_branch_ssm, v_b_gate, v_w_out, v_mix_norm_post, v_ffn_norm_pre, v_w_up, v_conv_w, v_conv_b, v_w_down, v_ffn_norm_post):
    given = dict(locals())
    L = x.shape[1]
    xs = x[0]
    target = loss_target[0]

    def shard_bits(group):
        return [given[name][0] if name == "conv_w" else _stored(name, given[name])[0].astype(BF16) for name, _, _, _ in group]

    W = {}

    def unpack_weights(gathered, group):
        for (name, rows, cols, axis), parts in zip(group, gathered):
            W[name] = _from_slices(name, parts, rows, cols, axis)

    hn1, *gathered_w_in = _rms_fwd_call(xs, mix_norm_pre, "rms_pre", shard_bits(GATHER_FIRST))
    unpack_weights(gathered_w_in, GATHER_FIRST)

    wit = W["w_in"]
    zero_rows = lambda r: jnp.zeros((r, D_MODEL), BF16)
    kr_end = P_KR + QK_ROPE
    w_in_pt = jnp.concatenate(
        [wit[:P_KR], zero_rows(QK_NOPE), wit[P_KR:kr_end], zero_rows(LANES - QK_HEAD), wit[kr_end:]], axis=0)

    proj, *gathered_proj = _mm(hn1, w_in_pt, "mm_in", tb=True, tn=w_in_pt.shape[0], gather=shard_bits(GATHER_PROJ))
    unpack_weights(gathered_proj, GATHER_PROJ)
    head_rows = lambda wt, width: jnp.pad(wt.reshape(N_HEADS, width, wt.shape[1]), ((0, 0), (0, LANES - width), (0, 0)))
    w_uq_pt = head_rows(W["w_uq"], QK_HEAD).reshape(HEAD_PAD, Q_RANK)
    w_kv_pt = jnp.stack([head_rows(W["w_uk"], QK_NOPE), head_rows(W["w_uv"], V_HEAD)], axis=1
                        ).reshape(2 * HEAD_PAD, KV_RANK)
    half = jnp.arange(QK_ROPE // 2, dtype=F32)
    inv_freq = ROPE_THETA ** (-2.0 * half / QK_ROPE)
    inv_freq = jnp.pad(jnp.concatenate([inv_freq, inv_freq]), (QK_NOPE, LANES - QK_HEAD)).reshape(1, LANES)
    pos_col = positions.astype(F32).reshape(L, 1)
    qn, ckvn, q_r, kv_r, cosf, sinf = _mla_proj_call(proj, q_norm, kv_norm, w_uq_pt, w_kv_pt, pos_col, inv_freq)
    attn, lse, *gathered_later = _attn_fwd_call(q_r, kv_r, shard_bits(GATHER_LATER))
    unpack_weights(gathered_later, GATHER_LATER)
    w_ba_p = jnp.pad(W["w_branch_attn"].reshape(N_HEADS, V_HEAD, D_MODEL), ((0, 0), (0, LANES - V_HEAD), (0, 0))
                     ).reshape(HEAD_PAD, D_MODEL)

    col = lambda a: a.reshape(SSM_NSTATE, -1)
    lr_c, li_c = col(ssm_lambda_re[0]), col(ssm_lambda_im[0])
    ldt_c = col(jnp.broadcast_to(ssm_log_dt[0][:, None], (SSM_GROUPS, SSM_STATE)))
    br_c, bi_c = col(ssm_b_re[0]), col(ssm_b_im[0])
    a_re_c, a_im_c, bb_re_c, bb_im_c = _disc_call(lr_c, li_c, ldt_c, br_c, bi_c)
    a_re, a_im = a_re_c.reshape(1, SSM_NSTATE), a_im_c.reshape(1, SSM_NSTATE)
    to_bb = lambda b: _block_diag(b.reshape(SSM_GROUPS, SSM_STATE, SSM_GROUP).transpose(0, 2, 1), True).astype(BF16)
    bb_re, bb_im = to_bb(bb_re_c), to_bb(bb_im_c)
    to_cm = lambda c_: _block_diag(c_[0].transpose(0, 2, 1), True).astype(BF16)
    cm_re, cm_im = to_cm(ssm_c_re), to_cm(ssm_c_im)
    d_skip = ssm_d.reshape(1, SSM_WIDTH)
    u_p = proj.reshape(8, L // 8, proj.shape[1])[:, :, P_U:P_GATE].transpose(1, 0, 2).reshape(L, SSM_WIDTH)
    y1, s_re, s_im = _ssm_fwd_call(u_p, a_re, a_im, bb_re, bb_im, cm_re, cm_im, d_skip)
    w_glu_b = W["w_glu"]
    ssm_p = _glu_call(y1, w_glu_b, b_glu)
    ssm = _time_unperm(ssm_p, L)

    pa = _mm(attn, w_ba_p, "mm_ba")
    ps = _mm(ssm, W["w_branch_ssm"], "mm_bs")
    merged = _merge_call(proj, b_gate, pa, ps)
    wide = lambda dt: (D_MODEL, dt)
    o, x2, hn2 = _mm_rows(merged, W["w_out"], "mm_out", _post_mix_rows, [xs], [mix_norm_post, ffn_norm_pre],
                          [wide(F32), wide(F32), wide(BF16)], [])
    h = _mm(hn2, W["w_up"], "mm_up", tb=True, tn=D_FF)
    cw = W["conv_w"]
    act = _conv_act_call(h, cw, conv_b)
    dy, dff, loss_row, g_ffn_norm_post = _mm_rows(
        act, W["w_down"], "mm_down", _ffn_out_rows, [x2, target], [ffn_norm_post], [wide(F32), wide(BF16)],
        [LANES, D_MODEL], tk=1408)

    da = _mm(dff, W["w_down"], "mm_down_dx", tb=True, tn=D_FF)
    g_w_down = _mm_tn(act, dff, "mm_down_dw", tm=1408)
    dgate, dval, dcw_g, dcw_v, dcb_g, dcb_v = _conv_act_bwd_call(da, h, cw, conv_b)
    g_conv_w = jnp.concatenate([dcw_g, dcw_v], axis=1)
    g_conv_b = jnp.concatenate([dcb_g, dcb_v], axis=1)
    dh = _conv_t_call(dgate, dval, cw)
    dx2, do, g_ffn_norm_pre, g_mix_norm_post = _mm_rows(
        dh, W["w_up"], "mm_up_dx", _post_bwd_rows, [x2, dy, o], [ffn_norm_pre, mix_norm_post], [wide(F32), wide(BF16)],
        [D_MODEL, D_MODEL], tk=1408)
    g_w_up = _mm_tn(dh, hn2, "mm_up_dw", tm=1408)
    dmerged = _mm(do, W["w_out"], "mm_out_dx", tb=True)
    g_w_out = _mm_tn(merged, do, "mm_out_dw")
    dpa, dps, dl0, dl1, db0, db1 = _merge_bwd_call(dmerged, proj, b_gate, pa, ps)
    g_b_gate = jnp.concatenate([db0, db1], axis=1)
    dattn = _mm(dpa, w_ba_p, "mm_ba_dx", tb=True, out_dtype=BF16)
    g_w_ba = _mm_tn(attn, dpa, "mm_ba_dw").reshape(N_HEADS, LANES, D_MODEL)[:, :V_HEAD].reshape(N_HEADS * V_HEAD, D_MODEL)
    dssm = _mm(dps, W["w_branch_ssm"], "mm_bs_dx", tb=True)
    g_w_bs = _mm_tn(ssm, dps, "mm_bs_dw")

    dy1, g_w_glu, g_b_glu = _glu_bwd_call(_time_perm(dssm, L), y1, w_glu_b, b_glu)
    du_p, dbb_re, dbb_im, dcm_re, dcm_im, da_re, da_im, g_ssm_d = _ssm_bwd_call(
        dy1, u_p, s_re, s_im, a_re, a_im, bb_re, bb_im, cm_re, cm_im, d_skip)
    du = _time_unperm(du_p, L)
    from_bb = lambda m: col(_block_diag_t(m, SSM_GROUP, SSM_STATE).transpose(0, 2, 1))
    dlr, dli, dldt, dbr, dbi = _disc_bwd_call(
        lr_c, li_c, ldt_c, br_c, bi_c, da_re.reshape(SSM_NSTATE, 1), da_im.reshape(SSM_NSTATE, 1), from_bb(dbb_re), from_bb(dbb_im))
    g_c_re = _block_diag_t(dcm_re, SSM_STATE, SSM_GROUP).transpose(0, 2, 1)
    g_c_im = _block_diag_t(dcm_im, SSM_STATE, SSM_GROUP).transpose(0, 2, 1)

    def grad_slices(group, grads):
        return [_to_slices(name, grads[name], rows, cols, axis) for name, rows, cols, axis in group]

    early_grads = {"w_up": g_w_up, "w_down": g_w_down, "conv_w": g_conv_w, "w_glu": g_w_glu.astype(BF16),
                   "w_branch_attn": g_w_ba, "w_branch_ssm": g_w_bs, "w_out": g_w_out}
    b_stored = lambda d: d.reshape(SSM_GROUPS, SSM_STATE, SSM_GROUP).transpose(0, 2, 1)
    per_state = lambda d: d.reshape(SSM_GROUPS, SSM_STATE)
    ssm_partials = {"ssm_lambda_re": per_state(dlr), "ssm_lambda_im": per_state(dli),
                    "ssm_b_re": b_stored(dbr), "ssm_b_im": b_stored(dbi),
                    "ssm_c_re": g_c_re, "ssm_c_im": g_c_im, "ssm_d": g_ssm_d.reshape(SSM_GROUPS, SSM_GROUP).T}
    ssm_shapes = [(name, ssm_partials[name].shape) for name, _ in SMALL if name in ssm_partials]
    dq, dkv, *landed = _attn_bwd_call(
        q_r, kv_r, attn, dattn, lse, grad_slices(GRADS_EARLY, early_grads),
        [ssm_partials[name].reshape(-1, LANES) if len(shp) == 3 else ssm_partials[name].reshape((1,) + shp)
         for name, shp in ssm_shapes])
    received_early = landed[:len(GRADS_EARLY)]
    ssm_all = {name: got.reshape((N_DEV, 1) + shp) for (name, shp), got in zip(ssm_shapes, landed[len(GRADS_EARLY):])}
    dq_p, dkv_p, dlatent, g_q_norm, g_kv_norm = _mla_proj_bwd_call(
        dq, dkv, cosf, sinf, proj, q_norm, kv_norm, w_uq_pt, w_kv_pt)
    g_w_uq = _mm_tn(dq_p, qn, "mm_uq_dw").reshape(N_HEADS, LANES, Q_RANK)[:, :QK_HEAD].reshape(N_HEADS * QK_HEAD, Q_RANK)
    g_w_kv = _mm_tn(ckvn, dkv_p, "mm_ukv_dw").T.reshape(N_HEADS, 2, LANES, KV_RANK)
    g_w_uk = g_w_kv[:, 0, :QK_NOPE].reshape(N_HEADS * QK_NOPE, KV_RANK)
    g_w_uv = g_w_kv[:, 1, :V_HEAD].reshape(N_HEADS * V_HEAD, KV_RANK)
    dproj = jnp.concatenate([dlatent, du.astype(BF16), dl0, dl1], axis=1)
    proj_grads = {"w_uq": g_w_uq, "w_uk": g_w_uk, "w_uv": g_w_uv}
    g_w_in_pt, *received_proj = _mm_tn(dproj, hn1, "mm_in_dw", tm=1664, exchange=grad_slices(GRADS_PROJ, proj_grads))
    g_w_in = jnp.concatenate([g_w_in_pt[:P_KR], g_w_in_pt[P_KR + QK_NOPE:P_KR + QK_HEAD], g_w_in_pt[P_U:]], axis=0)
    grad_x, g_mix_norm_pre, *received_last = _mm_in_dx_call(
        dproj, w_in_pt, xs, dx2, mix_norm_pre, grad_slices(GRADS_LAST, {"w_in": g_w_in}))

    results = {}
    wmv = lambda name: tuple(_stored(name, given[prefix + name]) for prefix in ("", "m_", "v_"))
    unstored = lambda name, res: [_stored(name, r) for r in res]
    whole = ("w_uq", "w_uk", "w_uv", "w_glu", "w_branch_attn", "w_branch_ssm", "conv_w")
    landed_small = dict(ssm_all)
    for group, received in ((GRADS_EARLY, received_early), (GRADS_PROJ, received_proj), (GRADS_LAST, received_last)):
        for (name, _, _, _), rec in zip(group, received):
            if name in whole:
                landed_small[name] = rec[:, None]
            else:
                results[name] = unstored(name, _adam_call(rec[:, None], *wmv(name), "adam_" + name))

    vec_grads = {"mix_norm_pre": g_mix_norm_pre, "q_norm": g_q_norm, "kv_norm": g_kv_norm,
                 "ssm_log_dt": jnp.sum(dldt.reshape(SSM_GROUPS, SSM_STATE), axis=1),
                 "b_glu": g_b_glu, "b_gate": g_b_gate, "mix_norm_post": g_mix_norm_post,
                 "ffn_norm_pre": g_ffn_norm_pre, "ffn_norm_post": g_ffn_norm_post}
    vec_names = [name for name, _ in SMALL if name in vec_grads]
    width = max(shp[0] for name, shp in SMALL if name in vec_grads)
    rows = [jnp.pad(vec_grads[name].reshape(1, -1), ((0, 0), (0, width - vec_grads[name].size))) for name in vec_names]
    rows.append(jnp.pad(loss_row, ((0, 0), (0, width - LANES))))
    rows.append(jnp.zeros((-len(rows) % 8, width), F32))
    rows_all, landed_small["conv_b"] = _small_gather_call([jnp.concatenate(rows, axis=0), g_conv_b], "gather_small_grads")
    others = ["conv_b"] + [name for name, _ in ssm_shapes] + list(whole)
    small_results, loss = _adam_small_call(
        rows_all, [wmv(n) for n in vec_names], [landed_small[n] for n in others], [wmv(n) for n in others])
    for name, res in zip(vec_names + others, small_results):
        results[name] = unstored(name, res)

    order = ["mix_norm_pre", "w_in", "q_norm", "w_uq", "kv_norm", "w_uk", "w_uv", "ssm_lambda_re", "ssm_lambda_im",
             "ssm_log_dt", "ssm_b_re", "ssm_b_im", "ssm_c_re", "ssm_c_im", "ssm_d", "w_glu", "b_glu", "w_branch_attn",
             "w_branch_ssm", "b_gate", "w_out", "mix_norm_post", "ffn_norm_pre", "w_up", "conv_w", "conv_b", "w_down",
             "ffn_norm_post"]
    outs = [loss, grad_x[None]]
    for kind in range(4):
        outs += [results[name][kind] for name in order]
    return tuple(outs)
```

```python
import math

import jax
import jax.numpy as jnp
from jax import lax
from jax.experimental import pallas as pl
from jax.experimental.pallas import tpu as pltpu

F32 = jnp.float32
BF16 = jnp.bfloat16
MESH_ID = pl.DeviceIdType.MESH

N_DEV = 8
LANES = 128
D_MODEL = 1024
N_HEADS = 8
QK_NOPE = 64
QK_ROPE = 32
QK_HEAD = QK_NOPE + QK_ROPE
V_HEAD = 64
Q_RANK = 384
KV_RANK = 256
ROPE_THETA = 10000.0
SSM_WIDTH = 512
SSM_GROUP = 16
SSM_GROUPS = 32
SSM_STATE = 64
SSM_NSTATE = SSM_GROUPS * SSM_STATE
SSM_CHUNKS = 4
D_FF = 2816
EPS = 1e-6
ADAM_LR, ADAM_B1, ADAM_B2, ADAM_EPS, ADAM_WD, ADAM_STEP = 0.001, 0.9, 0.999, 1e-08, 0.01, 10

P_CQ, P_CKV, P_KR, P_U, P_GATE = 0, 384, 640, 768, 1280
HEAD_PAD = N_HEADS * LANES

VMEM_BIG = 52 * 1024 * 1024

_GELU_C0 = math.sqrt(2.0 / math.pi)
_GELU_C1 = 0.044715
NEG = -1e30


def _fit(n, pref, mult=LANES):
    if n <= pref:
        return n
    t = (pref // mult) * mult
    while t > 0 and n % t:
        t -= mult
    assert t > 0, (n, pref, mult)
    return t


def _gelu(x):
    return x * (0.5 * (1.0 + jnp.tanh(_GELU_C0 * x * (1.0 + _GELU_C1 * (x * x)))))


def _gelu_and_grad(x):
    x2 = x * x
    t = jnp.tanh(_GELU_C0 * x * (1.0 + _GELU_C1 * x2))
    half = 0.5 * (1.0 + t)
    return x * half, half + 0.5 * x * (1.0 - t * t) * _GELU_C0 * (1.0 + 3.0 * _GELU_C1 * x2)


def _sigmoid(x):
    return 1.0 / (1.0 + jnp.exp(-x))


def _dot(a, b, dims):
    return lax.dot_general(a, b, (dims, ((), ())), preferred_element_type=F32)


NN = ((1,), (0,))
NT = ((1,), (1,))
TN = ((0,), (0,))


def _params(*sem, vmem=None):
    return pltpu.CompilerParams(dimension_semantics=tuple(sem), vmem_limit_bytes=vmem)


def _mm(a, b, name, tb=False, out_dtype=F32, tm=1024, tn=1024, tk=1024, gather=()):
    M, K = a.shape
    if tb:
        N, K2 = b.shape
    else:
        K2, N = b.shape
    assert K == K2, (a.shape, b.shape, tb)
    tm, tn, tk = _fit(M, tm), _fit(N, tn), _fit(K, tk)
    nk = K // tk
    grid = (M // tm, N // tn, nk)
    steps = grid[0] * grid[1] * grid[2]
    dims = NT if tb else NN
    n = len(gather)

    def body(a_ref, b_ref, *refs):
        o_ref, scratch = refs[n], refs[2 * n + 1:]
        step = (pl.program_id(0) * grid[1] + pl.program_id(1)) * grid[2] + pl.program_id(2)
        if n:
            start, forward, finish = _gather_phases(refs[:n], refs[n + 1:2 * n + 1], *scratch[-3:])
            pl.when(step == 0)(start)
            pl.when(step == steps // 2)(forward)
        part = _dot(a_ref[...].astype(BF16), b_ref[...].astype(BF16), dims)
        if nk == 1:
            o_ref[...] = part.astype(out_dtype)
        else:
            acc_ref = scratch[0]
            k = pl.program_id(2)

            @pl.when(k == 0)
            def _():
                acc_ref[...] = part

            @pl.when(k > 0)
            def _():
                acc_ref[...] += part

            @pl.when(k == nk - 1)
            def _():
                o_ref[...] = acc_ref[...].astype(out_dtype)
        if n:
            pl.when(step == steps - 1)(finish)

    a_spec = pl.BlockSpec((tm, tk), lambda i, j, k: (i, k))
    b_spec = pl.BlockSpec((tn, tk), lambda i, j, k: (j, k)) if tb else pl.BlockSpec((tk, tn), lambda i, j, k: (k, j))
    landed = [jax.ShapeDtypeStruct((N_DEV,) + p.shape, p.dtype) for p in gather]
    out = pl.pallas_call(
        body, name=name, grid=grid,
        in_specs=[a_spec, b_spec] + [ANY_SPEC] * n,
        out_specs=[pl.BlockSpec((tm, tn), lambda i, j, k: (i, j))] + [ANY_SPEC] * n,
        out_shape=[jax.ShapeDtypeStruct((M, N), out_dtype)] + landed,
        scratch_shapes=([] if nk == 1 else [pltpu.VMEM((tm, tn), F32)]) + (_comm_sems(n) if n else []),
        compiler_params=_params(*(("arbitrary",) * 3 if n else ("parallel", "parallel", "arbitrary")), vmem=VMEM_BIG),
    )(a, b, *gather)
    return out if n else out[0]


def _mm_rows(a, b, name, epilogue, rows_in, vecs_in, rows_out, vecs_out, tb=False, tk=1024):
    M, K = a.shape
    N = b.shape[0] if tb else b.shape[1]
    tm, tk = _fit(M, 512), _fit(K, tk)
    nk = K // tk
    nr, nv, nro = len(rows_in), len(vecs_in), len(rows_out)

    def body(a_ref, b_ref, *refs):
        ins, outs, acc_ref = refs[:nr + nv], refs[nr + nv:nr + nv + nro + len(vecs_out)], refs[-1]
        i, k = pl.program_id(0), pl.program_id(1)
        part = _dot(a_ref[...], b_ref[...], NT if tb else NN)

        def finish(product):
            res = epilogue(product, *[r[...] for r in ins])
            for ref, val in zip(outs[:nro], res[:nro]):
                ref[...] = val.astype(ref.dtype)
            for ref, val in zip(outs[nro:], res[nro:]):
                _acc(ref, i == 0, val)

        if nk == 1:
            finish(part)
        else:
            @pl.when(k == 0)
            def _():
                acc_ref[...] = part

            @pl.when(jnp.logical_and(k > 0, k < nk - 1))
            def _():
                acc_ref[...] += part

            @pl.when(k == nk - 1)
            def _():
                finish(acc_ref[...] + part)

    row = lambda w: pl.BlockSpec((tm, w), lambda i, k: (i, 0))
    vec = lambda w: pl.BlockSpec((1, w), lambda i, k: (0, 0))
    b_spec = pl.BlockSpec((N, tk), lambda i, k: (0, k)) if tb else pl.BlockSpec((tk, N), lambda i, k: (k, 0))
    return pl.pallas_call(
        body, name=name, grid=(M // tm, nk),
        in_specs=[pl.BlockSpec((tm, tk), lambda i, k: (i, k)), b_spec] + [row(r.shape[1]) for r in rows_in]
        + [vec(v.shape[1]) for v in vecs_in],
        out_specs=[row(w) for w, _ in rows_out] + [vec(w) for w in vecs_out],
        out_shape=[jax.ShapeDtypeStruct((M, w), dt) for w, dt in rows_out] + [jax.ShapeDtypeStruct((1, w), F32) for w in vecs_out],
        scratch_shapes=[pltpu.VMEM((tm, N), F32)],
        compiler_params=_params("arbitrary", "arbitrary", vmem=VMEM_BIG))(a, b, *rows_in, *vecs_in)


def _mm_in_dx_call(dproj, w_in_pt, x, dx2, g_pre, exchange):
    L, K = dproj.shape
    N = w_in_pt.shape[1]
    tm, tk = _fit(L, 512), _fit(K, 1664)
    nm, nk = L // tm, K // tk
    n = len(exchange)

    def body(a_ref, b_ref, x_ref, dx2_ref, g_ref, *refs):
        parts, (gx_ref, dg_ref), got = refs[:n], refs[n:n + 2], refs[n + 2:2 * n + 2]
        acc_ref = refs[2 * n + 2]
        i, k = pl.program_id(0), pl.program_id(1)
        start, finish = _exchange_phases(parts, got, *refs[2 * n + 3:])
        pl.when(jnp.logical_and(i == 0, k == 0))(start)
        part = _dot(a_ref[...], b_ref[...], NN)

        @pl.when(k == 0)
        def _():
            acc_ref[...] = part

        @pl.when(jnp.logical_and(k > 0, k < nk - 1))
        def _():
            acc_ref[...] += part

        @pl.when(k == nk - 1)
        def _():
            d1, dg = _rms_bwd(x_ref[...], g_ref[...], acc_ref[...] + part)
            gx_ref[...] = dx2_ref[...] + d1
            _acc(dg_ref, i == 0, dg)

        pl.when(jnp.logical_and(i == nm - 1, k == nk - 1))(finish)

    assert nk >= 2
    rows = lambda: pl.BlockSpec((tm, N), lambda i, k: (i, 0))
    return pl.pallas_call(
        body, name="mm_in_dx", grid=(nm, nk),
        in_specs=[pl.BlockSpec((tm, tk), lambda i, k: (i, k)), pl.BlockSpec((tk, N), lambda i, k: (k, 0)),
                  rows(), rows(), pl.BlockSpec((1, N), lambda i, k: (0, 0))] + [ANY_SPEC] * n,
        out_specs=[rows(), pl.BlockSpec((1, N), lambda i, k: (0, 0))] + [ANY_SPEC] * n,
        out_shape=[jax.ShapeDtypeStruct((L, N), F32), jax.ShapeDtypeStruct((1, N), F32)]
        + [jax.ShapeDtypeStruct(p.shape, p.dtype) for p in exchange],
        scratch_shapes=[pltpu.VMEM((tm, N), F32)] + _comm_sems(n),
        compiler_params=_params("arbitrary", "arbitrary", vmem=VMEM_BIG))(dproj, w_in_pt, x, dx2, g_pre, *exchange)


TN_CHUNK = 512


def _mm_tn(a, b, name, tm=512, tk=1024, exchange=()):
    K, M = a.shape
    K2, N = b.shape
    assert K == K2, (a.shape, b.shape)
    tm, tk, cn = _fit(M, tm), _fit(K, tk), _fit(N, TN_CHUNK)
    nm, nk = M // tm, K // tk
    n = len(exchange)

    def body(a_ref, b_ref, *refs):
        o_ref, acc_ref = refs[n], refs[2 * n + 1]
        i, k = pl.program_id(0), pl.program_id(1)
        if n:
            start, finish = _exchange_phases(refs[:n], refs[n + 1:2 * n + 1], *refs[2 * n + 2:])
            pl.when(jnp.logical_and(i == 0, k == 0))(start)

        @pl.when(k == 0)
        def _():
            acc_ref[...] = jnp.zeros((tm, N), F32)

        at = a_ref[...].astype(BF16).T
        for c in range(N // cn):
            cols = slice(c * cn, (c + 1) * cn)
            acc_ref[:, cols] += _dot(at, b_ref[:, cols].astype(BF16), NN)

        @pl.when(k == nk - 1)
        def _():
            o_ref[...] = acc_ref[...].astype(BF16)

        if n:
            pl.when(jnp.logical_and(i == nm - 1, k == nk - 1))(finish)

    out = pl.pallas_call(
        body, name=name, grid=(nm, nk),
        in_specs=[pl.BlockSpec((tk, tm), lambda i, k: (k, i)), pl.BlockSpec((tk, N), lambda i, k: (k, 0))] + [ANY_SPEC] * n,
        out_specs=[pl.BlockSpec((tm, N), lambda i, k: (i, 0))] + [ANY_SPEC] * n,
        out_shape=[jax.ShapeDtypeStruct((M, N), BF16)] + [jax.ShapeDtypeStruct(p.shape, p.dtype) for p in exchange],
        scratch_shapes=[pltpu.VMEM((tm, N), F32)] + (_comm_sems(n) if n else []),
        compiler_params=_params("arbitrary" if n else "parallel", "arbitrary", vmem=VMEM_BIG))(a, b, *exchange)
    return out if n else out[0]


def _row(tl, n, col=0):
    return pl.BlockSpec((tl, n), lambda i: (i, col))


def _full(shape):
    return pl.BlockSpec(shape, lambda i: (0,) * len(shape))


def _rms(x, g):
    r = lax.rsqrt(jnp.mean(x * x, axis=-1, keepdims=True) + EPS)
    return x * r * g


def _rms_bwd(x, g, dy):
    n = x.shape[-1]
    r = lax.rsqrt(jnp.mean(x * x, axis=-1, keepdims=True) + EPS)
    gy = dy * g
    dx = r * gy - x * (r * r * r * (1.0 / n)) * jnp.sum(x * gy, axis=-1, keepdims=True)
    return dx, jnp.sum(dy * x * r, axis=0, keepdims=True)


def _acc(ref, first, val):
    @pl.when(first)
    def _():
        ref[...] = val

    @pl.when(jnp.logical_not(first))
    def _():
        ref[...] += val


def _rms_fwd_call(x, g, name, gather):
    L, n = x.shape
    tl = _fit(L, 512)
    steps, na = L // tl, len(gather)

    def body(x_ref, g_ref, *refs):
        o_ref = refs[na]
        start, forward, finish = _gather_phases(refs[:na], refs[na + 1:2 * na + 1], *refs[2 * na + 1:])
        i = pl.program_id(0)
        pl.when(i == 0)(start)
        pl.when(i == steps // 2)(forward)
        o_ref[...] = _rms(x_ref[...], g_ref[...]).astype(BF16)
        pl.when(i == steps - 1)(finish)

    return pl.pallas_call(
        body, name=name, grid=(steps,), in_specs=[_row(tl, n), _full((1, n))] + [ANY_SPEC] * na,
        out_specs=[_row(tl, n)] + [ANY_SPEC] * na,
        out_shape=[jax.ShapeDtypeStruct((L, n), BF16)] + [jax.ShapeDtypeStruct((N_DEV,) + b.shape, b.dtype) for b in gather],
        scratch_shapes=_comm_sems(na), compiler_params=_params("arbitrary"))(x, g, *gather)


def _rope_lanes(shape):
    lane = lax.broadcasted_iota(jnp.int32, shape, 1)
    return lane, jnp.logical_and(lane >= QK_NOPE, lane < QK_HEAD)


def _rope_apply(x, cosf, sinf, lane):
    rot = jnp.where(lane < QK_NOPE + QK_ROPE // 2, -pltpu.roll(x, LANES - QK_ROPE // 2, 1), pltpu.roll(x, QK_ROPE // 2, 1))
    return x * cosf + rot * sinf


def _rope_apply_t(dy, cosf, sinf, lane, is_rope):
    g = dy * sinf
    rot_t = jnp.where(lane < QK_NOPE + QK_ROPE // 2, pltpu.roll(g, LANES - QK_ROPE // 2, 1), -pltpu.roll(g, QK_ROPE // 2, 1))
    return dy * cosf + jnp.where(is_rope, rot_t, 0.0)


def _mla_proj_call(proj, q_norm, kv_norm, w_uq_pt, w_kv_pt, pos_col, inv_freq):
    L = proj.shape[0]
    tl = _fit(L, 512)

    def body(p_ref, gq_ref, gk_ref, wq_ref, wkv_ref, pos_ref, f_ref, qn_ref, kn_ref, qo_ref, kvo_ref, cos_ref, sin_ref):
        qn = _rms(p_ref[:, P_CQ:P_CKV], gq_ref[...]).astype(BF16)
        kn = _rms(p_ref[:, P_CKV:P_KR], gk_ref[...]).astype(BF16)
        qn_ref[...] = qn
        kn_ref[...] = kn
        q_pad = _dot(qn, wq_ref[...], NT)
        kv_pad = _dot(kn, wkv_ref[...], NT)
        lane, is_rope = _rope_lanes((tl, LANES))
        ang = pos_ref[...] * f_ref[...]
        cosf = jnp.where(is_rope, jnp.cos(ang), jnp.where(lane < QK_NOPE, 1.0, 0.0))
        sinf = jnp.where(is_rope, jnp.sin(ang), 0.0)
        cos_ref[...] = cosf
        sin_ref[...] = sinf
        kr = _rope_apply(p_ref[:, P_KR:P_U], cosf, sinf, lane)
        for h in range(N_HEADS):
            qh = _rope_apply(q_pad[:, h * LANES:(h + 1) * LANES], cosf, sinf, lane)
            qo_ref[:, h * LANES:(h + 1) * LANES] = (qh * Q_PRESCALE).astype(BF16)
            kvo_ref[:, 2 * h * LANES:(2 * h + 1) * LANES] = (kv_pad[:, 2 * h * LANES:(2 * h + 1) * LANES] + kr).astype(BF16)
            vh = jnp.where(lane == V_HEAD, 1.0, kv_pad[:, (2 * h + 1) * LANES:(2 * h + 2) * LANES])
            kvo_ref[:, (2 * h + 1) * LANES:(2 * h + 2) * LANES] = vh.astype(BF16)

    shape = lambda n, dt: jax.ShapeDtypeStruct((L, n), dt)
    return pl.pallas_call(
        body, name="mla_proj", grid=(L // tl,),
        in_specs=[_row(tl, P_U), _full((1, Q_RANK)), _full((1, KV_RANK)), _full((HEAD_PAD, Q_RANK)),
                  _full((2 * HEAD_PAD, KV_RANK)), _row(tl, 1), _full((1, LANES))],
        out_specs=[_row(tl, Q_RANK), _row(tl, KV_RANK), _row(tl, HEAD_PAD), _row(tl, 2 * HEAD_PAD), _row(tl, LANES), _row(tl, LANES)],
        out_shape=[shape(Q_RANK, BF16), shape(KV_RANK, BF16), shape(HEAD_PAD, BF16), shape(2 * HEAD_PAD, BF16),
                   shape(LANES, F32), shape(LANES, F32)],
        compiler_params=_params("parallel"))(proj, q_norm, kv_norm, w_uq_pt, w_kv_pt, pos_col, inv_freq)


def _mla_proj_bwd_call(dq, dkv, cosf, sinf, proj, q_norm, kv_norm, w_uq_pt, w_kv_pt):
    L = dq.shape[0]
    tl = _fit(L, 512)

    def body(dq_ref, dkv_ref, cos_ref, sin_ref, p_ref, gq_ref, gk_ref, wq_ref, wkv_ref,
             dqo_ref, dkvo_ref, d_ref, dgq_ref, dgk_ref):
        first = pl.program_id(0) == 0
        lane, is_rope = _rope_lanes((tl, LANES))
        cosf, sinf = cos_ref[...], sin_ref[...]
        dk_sum = jnp.zeros((tl, LANES), F32)
        for h in range(N_HEADS):
            dqo_ref[:, h * LANES:(h + 1) * LANES] = _rope_apply_t(dq_ref[:, h * LANES:(h + 1) * LANES], cosf, sinf, lane, is_rope).astype(BF16)
            dk_sum = dk_sum + dkv_ref[:, 2 * h * LANES:(2 * h + 1) * LANES]
        dkvo_ref[...] = dkv_ref[...].astype(BF16)
        dqn = _dot(dqo_ref[...], wq_ref[...], NN)
        dkn = _dot(dkvo_ref[...], wkv_ref[...], NN)
        dcq, dgq = _rms_bwd(p_ref[:, P_CQ:P_CKV], gq_ref[...], dqn)
        dckv, dgk = _rms_bwd(p_ref[:, P_CKV:P_KR], gk_ref[...], dkn)
        d_ref[:, P_CQ:P_CKV] = dcq.astype(BF16)
        d_ref[:, P_CKV:P_KR] = dckv.astype(BF16)
        d_ref[:, P_KR:P_U] = _rope_apply_t(dk_sum, cosf, sinf, lane, is_rope).astype(BF16)
        _acc(dgq_ref, first, dgq)
        _acc(dgk_ref, first, dgk)

    shape = lambda n: jax.ShapeDtypeStruct((L, n), BF16)
    return pl.pallas_call(
        body, name="mla_proj_bwd", grid=(L // tl,),
        in_specs=[_row(tl, HEAD_PAD), _row(tl, 2 * HEAD_PAD), _row(tl, LANES), _row(tl, LANES), _row(tl, P_KR),
                  _full((1, Q_RANK)), _full((1, KV_RANK)), _full((HEAD_PAD, Q_RANK)), _full((2 * HEAD_PAD, KV_RANK))],
        out_specs=[_row(tl, HEAD_PAD), _row(tl, 2 * HEAD_PAD), _row(tl, P_U), _full((1, Q_RANK)), _full((1, KV_RANK))],
        out_shape=[shape(HEAD_PAD), shape(2 * HEAD_PAD), shape(P_U), jax.ShapeDtypeStruct((1, Q_RANK), F32),
                   jax.ShapeDtypeStruct((1, KV_RANK), F32)],
        compiler_params=_params("arbitrary"))(dq, dkv, cosf, sinf, proj, q_norm, kv_norm, w_uq_pt, w_kv_pt)


GATE_TILE = 256
GATE_ROWS = 1024


def _merge_call(proj, b_gate, pa, ps):
    L = proj.shape[0]
    tl = _fit(L, GATE_ROWS)
    nc = D_MODEL // GATE_TILE
    g0, g1 = P_GATE // GATE_TILE, (P_GATE + D_MODEL) // GATE_TILE

    def body(l0_ref, l1_ref, b0_ref, b1_ref, pa_ref, ps_ref, o_ref):
        s0 = _sigmoid(l0_ref[...] + b0_ref[...])
        s1 = _sigmoid(l1_ref[...] + b1_ref[...])
        o_ref[...] = (s0 * pa_ref[...] + s1 * ps_ref[...]).astype(BF16)

    blk = lambda off: pl.BlockSpec((tl, GATE_TILE), lambda i, j: (i, off + j))
    bias = lambda off: pl.BlockSpec((1, GATE_TILE), lambda i, j: (0, off + j))
    return pl.pallas_call(
        body, name="merge", grid=(L // tl, nc),
        in_specs=[blk(g0), blk(g1), bias(0), bias(nc), blk(0), blk(0)],
        out_specs=blk(0), out_shape=jax.ShapeDtypeStruct((L, D_MODEL), BF16),
        compiler_params=_params("parallel", "parallel"))(proj, proj, b_gate, b_gate, pa, ps)


def _merge_bwd_call(dm, proj, b_gate, pa, ps):
    L = proj.shape[0]
    tl = _fit(L, GATE_ROWS)
    nc = D_MODEL // GATE_TILE
    g0, g1 = P_GATE // GATE_TILE, (P_GATE + D_MODEL) // GATE_TILE

    def body(dm_ref, l0_ref, l1_ref, b0_ref, b1_ref, pa_ref, ps_ref, dpa_ref, dps_ref, dl0_ref, dl1_ref, db0_ref, db1_ref):
        first = pl.program_id(1) == 0
        dm_ = dm_ref[...]
        s0 = _sigmoid(l0_ref[...] + b0_ref[...])
        s1 = _sigmoid(l1_ref[...] + b1_ref[...])
        dpa_ref[...] = (dm_ * s0).astype(BF16)
        dps_ref[...] = (dm_ * s1).astype(BF16)
        dl0 = dm_ * pa_ref[...] * s0 * (1.0 - s0)
        dl1 = dm_ * ps_ref[...] * s1 * (1.0 - s1)
        dl0_ref[...] = dl0.astype(BF16)
        dl1_ref[...] = dl1.astype(BF16)
        _acc(db0_ref, first, jnp.sum(dl0, axis=0, keepdims=True))
        _acc(db1_ref, first, jnp.sum(dl1, axis=0, keepdims=True))

    blk = lambda off: pl.BlockSpec((tl, GATE_TILE), lambda j, i: (i, off + j))
    bias = lambda off: pl.BlockSpec((1, GATE_TILE), lambda j, i: (0, off + j))
    act = jax.ShapeDtypeStruct((L, D_MODEL), BF16)
    vec = jax.ShapeDtypeStruct((1, D_MODEL), F32)
    return pl.pallas_call(
        body, name="merge_bwd", grid=(nc, L // tl),
        in_specs=[blk(0), blk(g0), blk(g1), bias(0), bias(nc), blk(0), blk(0)],
        out_specs=[blk(0), blk(0), blk(0), blk(0), bias(0), bias(0)],
        out_shape=[act, act, act, act, vec, vec],
        compiler_params=_params("parallel", "arbitrary"))(dm, proj, proj, b_gate, b_gate, pa, ps)


def _post_mix_rows(o, x, g_post, g_fpre):
    x2 = x + _rms(o, g_post)
    return o, x2, _rms(x2, g_fpre)


def _ffn_out_rows(ff, x2, target, g_fpost):
    n = ff.shape[-1]
    err = x2 + _rms(ff, g_fpost) - target
    part = 0.5 * jnp.sum(jnp.sum(err * err, axis=-1, keepdims=True) * (1.0 / n), axis=0, keepdims=True)
    dy = err * (1.0 / n)
    dff, dg = _rms_bwd(ff, g_fpost, dy)
    return dy, dff, jnp.broadcast_to(part, (1, LANES)), dg


def _post_bwd_rows(dhn2, x2, dy, o, g_fpre, g_post):
    d1, dgf = _rms_bwd(x2, g_fpre, dhn2)
    dx2 = dy + d1
    do, dgp = _rms_bwd(o, g_post, dx2)
    return dx2, do, dgf, dgp


CONV_TILE = 256
CONV_WIDE = 1408
HALO = 16


def _conv3(w, b, x0, x1, x2):
    return b + w[2:3] * x0 + w[1:2] * x1 + w[0:1] * x2


def _down(x, by):
    return pltpu.roll(x, by, 0)


def _edge_down(edge, before, by):
    r = lax.broadcasted_iota(jnp.int32, edge.shape, 0)
    return jnp.where(r < by, pltpu.roll(before, by, 0), pltpu.roll(edge, by, 0))


def _edge_up(edge, after, by):
    r = lax.broadcasted_iota(jnp.int32, edge.shape, 0)
    return jnp.where(r >= HALO - by, pltpu.roll(after, HALO - by, 0), pltpu.roll(edge, HALO - by, 0))


def _gated(w_g, b_g, w_v, b_v, hg, hv, g1, g2, v1, v2):
    return _conv3(w_g, b_g, hg, g1, g2), _conv3(w_v, b_v, hv, v1, v2)


def _conv_specs(tl, tc, rows_inner):
    nh = tl // HALO
    if rows_inner:
        ij = lambda f: (lambda j, i: f(i, j))
    else:
        ij = lambda f: f
    cur = lambda off: pl.BlockSpec((tl, tc), ij(lambda i, j: (i, off + j)))
    prev = lambda off: pl.BlockSpec((HALO, tc), ij(lambda i, j: (jnp.maximum(i * nh - 1, 0), off + j)))
    par = lambda rows, off: pl.BlockSpec((rows, tc), ij(lambda i, j: (0, off + j)))
    return cur, prev, par


def _conv_act_call(h, conv_w, conv_b):
    L = h.shape[0]
    tl = _fit(L, 512)
    nc = D_FF // CONV_WIDE
    cur, prev, par = _conv_specs(tl, CONV_WIDE, False)

    def body(hg_ref, hv_ref, pg_ref, pv_ref, wg_ref, wv_ref, bg_ref, bv_ref, a_ref):
        not_first = (pl.program_id(0) > 0).astype(F32)
        par = (wg_ref[...], bg_ref[...], wv_ref[...], bv_ref[...])
        hg, hv = hg_ref[...], hv_ref[...]
        gate, val = _gated(*par, hg, hv, _down(hg, 1), _down(hg, 2), _down(hv, 1), _down(hv, 2))
        a_ref[...] = (_gelu(gate) * val).astype(BF16)
        eg, ev, bg, bv = hg[:HALO], hv[:HALO], pg_ref[...] * not_first, pv_ref[...] * not_first
        gate, val = _gated(*par, eg, ev, _edge_down(eg, bg, 1), _edge_down(eg, bg, 2),
                           _edge_down(ev, bv, 1), _edge_down(ev, bv, 2))
        a_ref[:HALO, :] = (_gelu(gate) * val).astype(BF16)

    return pl.pallas_call(
        body, name="conv_act", grid=(L // tl, nc),
        in_specs=[cur(0), cur(nc), prev(0), prev(nc), par(3, 0), par(3, nc), par(1, 0), par(1, nc)],
        out_specs=cur(0), out_shape=jax.ShapeDtypeStruct((L, D_FF), BF16),
        compiler_params=_params("parallel", "parallel", vmem=VMEM_BIG))(h, h, h, h, conv_w, conv_w, conv_b, conv_b)


def _conv_act_bwd_call(da, h, conv_w, conv_b, exchange):
    L = h.shape[0]
    tl = _fit(L, 512)
    nc = D_FF // CONV_TILE
    cur, prev, par = _conv_specs(tl, CONV_TILE, True)

    n = len(exchange)

    def body(da_ref, hg_ref, hv_ref, pg_ref, pv_ref, wg_ref, wv_ref, bg_ref, bv_ref, *refs):
        dg_ref, dv_ref, dwg_ref, dwv_ref, dbg_ref, dbv_ref = refs[n:n + 6]
        start, finish = _exchange_phases(refs[:n], refs[n + 6:2 * n + 6], *refs[2 * n + 6:])
        pl.when(jnp.logical_and(pl.program_id(0) == 0, pl.program_id(1) == 0))(start)
        first = pl.program_id(1) == 0
        not_first = (pl.program_id(1) > 0).astype(F32)
        par = (wg_ref[...], bg_ref[...], wv_ref[...], bv_ref[...])
        col = lambda t: jnp.sum(t, axis=0, keepdims=True)

        def grads(da_, hg, hv, g1, g2, v1, v2):
            gate, val = _gated(*par, hg, hv, g1, g2, v1, v2)
            act, slope = _gelu_and_grad(gate)
            dgate = da_ * val * slope
            dval = da_ * act
            sums = (jnp.concatenate([col(dgate * g2), col(dgate * g1), col(dgate * hg)], axis=0),
                    jnp.concatenate([col(dval * v2), col(dval * v1), col(dval * hv)], axis=0), col(dgate), col(dval))
            return dgate, dval, sums

        da_, hg, hv = da_ref[...], hg_ref[...], hv_ref[...]
        shifted = (_down(hg, 1), _down(hg, 2), _down(hv, 1), _down(hv, 2))
        dgate, dval, whole = grads(da_, hg, hv, *shifted)
        dg_ref[...] = dgate.astype(BF16)
        dv_ref[...] = dval.astype(BF16)
        edge = lambda t: t[:HALO]
        _, _, wrapped = grads(edge(da_), edge(hg), edge(hv), *[edge(s) for s in shifted])
        eg, ev, bg, bv = edge(hg), edge(hv), pg_ref[...] * not_first, pv_ref[...] * not_first
        dgate, dval, fixed = grads(edge(da_), eg, ev, _edge_down(eg, bg, 1), _edge_down(eg, bg, 2),
                                   _edge_down(ev, bv, 1), _edge_down(ev, bv, 2))
        dg_ref[:HALO, :] = dgate.astype(BF16)
        dv_ref[:HALO, :] = dval.astype(BF16)
        for ref, a, b, c in zip((dwg_ref, dwv_ref, dbg_ref, dbv_ref), whole, wrapped, fixed):
            _acc(ref, first, a - b + c)
        pl.when(jnp.logical_and(pl.program_id(0) == nc - 1, pl.program_id(1) == L // tl - 1))(finish)

    act = jax.ShapeDtypeStruct((L, D_FF), BF16)
    w3 = jax.ShapeDtypeStruct((3, D_FF), F32)
    w1 = jax.ShapeDtypeStruct((1, D_FF), F32)
    return pl.pallas_call(
        body, name="conv_act_bwd", grid=(nc, L // tl),
        in_specs=[cur(0), cur(0), cur(nc), prev(0), prev(nc), par(3, 0), par(3, nc), par(1, 0), par(1, nc)] + [ANY_SPEC] * n,
        out_specs=[cur(0), cur(0), par(3, 0), par(3, 0), par(1, 0), par(1, 0)] + [ANY_SPEC] * n,
        out_shape=[act, act, w3, w3, w1, w1] + [jax.ShapeDtypeStruct(p.shape, p.dtype) for p in exchange],
        scratch_shapes=_comm_sems(n),
        compiler_params=_params("arbitrary", "arbitrary"))(da, h, h, h, h, conv_w, conv_w, conv_b, conv_b, *exchange)


def _conv_t_call(dgate, dval, conv_w):
    L = dgate.shape[0]
    tl = _fit(L, 512)
    nc = D_FF // CONV_WIDE
    nh = tl // HALO

    def body(dg_ref, dv_ref, ng_ref, nv_ref, w_ref, o_ref):
        not_last = (pl.program_id(0) < L // tl - 1).astype(F32)

        def emit(d_ref, n_ref):
            c = d_ref[...].astype(F32)
            w = w_ref[...]
            o_ref[...] = _conv3(w, 0.0, c, pltpu.roll(c, tl - 1, 0), pltpu.roll(c, tl - 2, 0)).astype(BF16)
            edge, after = c[tl - HALO:], n_ref[...].astype(F32) * not_last
            o_ref[tl - HALO:, :] = _conv3(w, 0.0, edge, _edge_up(edge, after, 1), _edge_up(edge, after, 2)).astype(BF16)

        pl.when(pl.program_id(1) < nc)(lambda: emit(dg_ref, ng_ref))
        pl.when(pl.program_id(1) >= nc)(lambda: emit(dv_ref, nv_ref))

    gate_col = lambda j: jnp.minimum(j, nc - 1)
    val_col = lambda j: jnp.maximum(j - nc, 0)
    after_row = lambda i: jnp.minimum((i + 1) * nh, L // HALO - 1)
    tile = lambda col: pl.BlockSpec((tl, CONV_WIDE), lambda i, j: (i, col(j)))
    after = lambda col: pl.BlockSpec((HALO, CONV_WIDE), lambda i, j: (after_row(i), col(j)))
    return pl.pallas_call(
        body, name="conv_t", grid=(L // tl, 2 * nc),
        in_specs=[tile(gate_col), tile(val_col), after(gate_col), after(val_col), pl.BlockSpec((3, CONV_WIDE), lambda i, j: (0, j))],
        out_specs=pl.BlockSpec((tl, CONV_WIDE), lambda i, j: (i, j)),
        out_shape=jax.ShapeDtypeStruct((L, 2 * D_FF), BF16),
        compiler_params=_params("parallel", "parallel"))(dgate, dval, dgate, dval, conv_w)


def _glu_call(y1, w_glu, b_glu):
    L, n = y1.shape
    tl = _fit(L, 512)

    def body(y_ref, w_ref, b_ref, o_ref):
        y2 = _gelu(y_ref[...])
        z = _dot(y2.astype(BF16), w_ref[...], NN) + b_ref[...]
        o_ref[...] = (y2 * _sigmoid(z)).astype(BF16)

    return pl.pallas_call(
        body, name="glu", grid=(L // tl,), in_specs=[_row(tl, n), _full((n, n)), _full((1, n))],
        out_specs=_row(tl, n), out_shape=jax.ShapeDtypeStruct((L, n), BF16),
        compiler_params=_params("parallel"))(y1, w_glu, b_glu)


def _glu_bwd_call(dout, y1, w_glu, b_glu):
    L, n = y1.shape
    tl = _fit(L, 512)

    def body(do_ref, y_ref, w_ref, b_ref, dy_ref, dw_ref, db_ref):
        first = pl.program_id(0) == 0
        y1_ = y_ref[...]
        y2, slope = _gelu_and_grad(y1_)
        y2b = y2.astype(BF16)
        w = w_ref[...]
        sg = _sigmoid(_dot(y2b, w, NN) + b_ref[...])
        dout_ = do_ref[...].astype(F32)
        dz = dout_ * y2 * sg * (1.0 - sg)
        dzb = dz.astype(BF16)
        dy2 = dout_ * sg + _dot(dzb, w, NT)
        dy_ref[...] = dy2 * slope
        _acc(dw_ref, first, _dot(y2b, dzb, TN))
        _acc(db_ref, first, jnp.sum(dz, axis=0, keepdims=True))

    return pl.pallas_call(
        body, name="glu_bwd", grid=(L // tl,),
        in_specs=[_row(tl, n), _row(tl, n), _full((n, n)), _full((1, n))],
        out_specs=[_row(tl, n), _full((n, n)), _full((1, n))],
        out_shape=[jax.ShapeDtypeStruct((L, n), F32), jax.ShapeDtypeStruct((n, n), F32), jax.ShapeDtypeStruct((1, n), F32)],
        compiler_params=_params("arbitrary"))(dout, y1, w_glu, b_glu)


ATTN_TILE = 1024
ATTN_SCALE = 1.0 / math.sqrt(QK_HEAD)


ATTN_HEADS = 2
ATTN_GROUPS = N_HEADS // ATTN_HEADS
LOG2E = 1.0 / math.log(2.0)
Q_PRESCALE = ATTN_SCALE * LOG2E
ANY_SPEC = pl.BlockSpec(memory_space=pl.ANY)


def _attn_fwd_call(q, kv, blocks):
    L = q.shape[0]
    t = _fit(L, ATTN_TILE)
    nq = L // t
    n = len(blocks)

    def body(q_ref, kv_ref, *refs):
        blk_refs, (o_ref, lse_ref), gat_refs = refs[:n], refs[n:n + 2], refs[n + 2:2 * n + 2]
        m_s, acc_s, send_sems, recv_sems, local_sems = refs[2 * n + 2:]
        g, i = pl.program_id(0), pl.program_id(1)
        start, forward, finish = _gather_phases(blk_refs, gat_refs, send_sems, recv_sems, local_sems)
        pl.when(jnp.logical_and(g == 0, i == 0))(start)
        m_s[...] = jnp.full((ATTN_HEADS, t, 1), NEG, F32)
        acc_s[...] = jnp.zeros((ATTN_HEADS, t, LANES), F32)
        below = lax.broadcasted_iota(jnp.int32, (t, t), 1) <= lax.broadcasted_iota(jnp.int32, (t, t), 0)

        def block_step(kb, on_diagonal):
            rows = pl.ds(pl.multiple_of(kb * t, t), t)
            for a in range(ATTN_HEADS):
                s = _dot(q_ref[:, a * LANES:(a + 1) * LANES], kv_ref[rows, 2 * a * LANES:(2 * a + 1) * LANES], NT)
                if on_diagonal:
                    s = jnp.where(below, s, NEG)
                m_prev = m_s[a]
                m_new = jnp.maximum(m_prev, jnp.max(s, axis=1, keepdims=True))
                p = jnp.exp2(s - m_new)
                pv = _dot(p.astype(BF16), kv_ref[rows, (2 * a + 1) * LANES:(2 * a + 2) * LANES], NN)
                acc_s[a] = jnp.exp2(m_prev - m_new) * acc_s[a] + pv
                m_s[a] = m_new

        def step(kb, carry):
            block_step(kb, False)
            return carry

        lax.fori_loop(0, i, step, 0)
        block_step(i, True)
        lane = lax.broadcasted_iota(jnp.int32, (t, LANES), 1)
        for a in range(ATTN_HEADS):
            acc = acc_s[a]
            l = jnp.sum(jnp.where(lane == V_HEAD, acc, 0.0), axis=1, keepdims=True)
            o_ref[:, a * LANES:(a + 1) * LANES] = (acc / l).astype(BF16)
            lse_ref[a] = m_s[a] + jnp.log(l) * LOG2E
        pl.when(jnp.logical_and(g == (3 * ATTN_GROUPS) // 4, i == 0))(forward)
        pl.when(jnp.logical_and(g == ATTN_GROUPS - 1, i == nq - 1))(finish)

    gw = ATTN_HEADS * LANES
    return pl.pallas_call(
        body, name="attn_fwd", grid=(ATTN_GROUPS, nq),
        in_specs=[pl.BlockSpec((t, gw), lambda g, i: (i, g)),
                  pl.BlockSpec((L, 2 * gw), lambda g, i: (0, g))] + [ANY_SPEC] * n,
        out_specs=[pl.BlockSpec((t, gw), lambda g, i: (i, g)),
                   pl.BlockSpec((ATTN_HEADS, t, 1), lambda g, i: (g, i, 0))] + [ANY_SPEC] * n,
        out_shape=[jax.ShapeDtypeStruct((L, HEAD_PAD), BF16), jax.ShapeDtypeStruct((N_HEADS, L, 1), F32)]
        + [jax.ShapeDtypeStruct((N_DEV,) + b.shape, b.dtype) for b in blocks],
        scratch_shapes=[pltpu.VMEM((ATTN_HEADS, t, 1), F32), pltpu.VMEM((ATTN_HEADS, t, LANES), F32)] + _comm_sems(n),
        compiler_params=_params("arbitrary", "arbitrary", vmem=VMEM_BIG))(q, kv, *blocks)


def _attn_bwd_call(q, kv, o, do, lse, parts, blocks):
    L = q.shape[0]
    t = _fit(L, ATTN_TILE)
    nq = L // t
    n1, n = len(parts), len(parts) + len(blocks)

    def body(q_ref, do_ref, o_ref, lse_ref, kv_ref, *refs):
        in_refs, (dq_ref, dkv_ref), out_refs = refs[:n], refs[n:n + 2], refs[n + 2:2 * n + 2]
        dk_s, dv_s = refs[2 * n + 2:2 * n + 4]
        g, j = pl.program_id(0), pl.program_id(1)
        start, finish = _exchange_phases(in_refs[:n1], out_refs[:n1], *refs[2 * n + 4:2 * n + 7])
        start_blocks, finish_blocks = _exchange_phases(in_refs[n1:], out_refs[n1:], *refs[2 * n + 7:], same_source=True)

        @pl.when(jnp.logical_and(g == 0, j == 0))
        def _():
            start()
            start_blocks()

        @pl.when(j == 0)
        def _():
            dq_ref[...] = jnp.zeros((L, ATTN_HEADS * LANES), F32)

        dk_s[...] = jnp.zeros((ATTN_HEADS, t, LANES), F32)
        dv_s[...] = jnp.zeros((ATTN_HEADS, t, LANES), F32)
        below = lax.broadcasted_iota(jnp.int32, (t, t), 1) <= lax.broadcasted_iota(jnp.int32, (t, t), 0)

        def block_step(i, on_diagonal):
            rows = pl.ds(pl.multiple_of(i * t, t), t)
            for a in range(ATTN_HEADS):
                lanes = slice(a * LANES, (a + 1) * LANES)
                qi = q_ref[rows, lanes]
                doi = do_ref[rows, lanes]
                kblk = kv_ref[:, 2 * a * LANES:(2 * a + 1) * LANES]
                delta = jnp.sum(doi.astype(F32) * o_ref[rows, lanes].astype(F32), axis=1, keepdims=True)
                s = _dot(qi, kblk, NT)
                if on_diagonal:
                    s = jnp.where(below, s, NEG)
                p = jnp.exp2(s - lse_ref[a, rows, :])
                dv_s[a] += _dot(p.astype(BF16), doi, TN)
                ds = (p * (_dot(doi, kv_ref[:, (2 * a + 1) * LANES:(2 * a + 2) * LANES], NT) - delta)).astype(BF16)
                dk_s[a] += _dot(ds, qi, TN)
                dq_ref[rows, lanes] += _dot(ds, kblk, NN) * ATTN_SCALE

        def step(i, carry):
            block_step(i, False)
            return carry

        block_step(j, True)
        lax.fori_loop(j + 1, nq, step, 0)
        for a in range(ATTN_HEADS):
            dkv_ref[:, 2 * a * LANES:(2 * a + 1) * LANES] = dk_s[a] * (1.0 / LOG2E)
            dkv_ref[:, (2 * a + 1) * LANES:(2 * a + 2) * LANES] = dv_s[a]
        @pl.when(jnp.logical_and(g == ATTN_GROUPS - 1, j == nq - 1))
        def _():
            finish()
            finish_blocks()

    gw = ATTN_HEADS * LANES
    whole = lambda: pl.BlockSpec((L, gw), lambda g, j: (0, g))
    acc = pltpu.VMEM((ATTN_HEADS, t, LANES), F32)
    return pl.pallas_call(
        body, name="attn_bwd", grid=(ATTN_GROUPS, nq),
        in_specs=[whole(), whole(), whole(), pl.BlockSpec((ATTN_HEADS, L, 1), lambda g, j: (g, 0, 0)),
                  pl.BlockSpec((t, 2 * gw), lambda g, j: (j, g))] + [ANY_SPEC] * n,
        out_specs=[whole(), pl.BlockSpec((t, 2 * gw), lambda g, j: (j, g))] + [ANY_SPEC] * n,
        out_shape=[jax.ShapeDtypeStruct((L, HEAD_PAD), F32), jax.ShapeDtypeStruct((L, 2 * HEAD_PAD), F32)]
        + [jax.ShapeDtypeStruct(p.shape, p.dtype) for p in parts]
        + [jax.ShapeDtypeStruct((N_DEV,) + b.shape, b.dtype) for b in blocks],
        scratch_shapes=[acc, acc] + _comm_sems(n1) + _comm_sems(n - n1),
        compiler_params=_params("arbitrary", "arbitrary", vmem=VMEM_BIG))(q, do, o, lse, kv, *parts, *blocks)


def _disc(lr, li, ldt, br, bi):
    dt = jnp.exp(ldt)
    mag = jnp.exp(lr * dt)
    ang = li * dt
    a_re, a_im = mag * jnp.cos(ang), mag * jnp.sin(ang)
    den = lr * lr + li * li
    n_re, n_im = a_re - 1.0, a_im
    z_re = (n_re * lr + n_im * li) / den
    z_im = (n_im * lr - n_re * li) / den
    return a_re, a_im, z_re * br - z_im * bi, z_re * bi + z_im * br


def _disc_call(lr, li, ldt, br, bi):
    def body(lr_ref, li_ref, ldt_ref, br_ref, bi_ref, ar_ref, ai_ref, bbr_ref, bbi_ref):
        ar_ref[...], ai_ref[...], bbr_ref[...], bbi_ref[...] = _disc(
            lr_ref[...], li_ref[...], ldt_ref[...], br_ref[...], bi_ref[...])

    c1 = jax.ShapeDtypeStruct((SSM_NSTATE, 1), F32)
    c16 = jax.ShapeDtypeStruct((SSM_NSTATE, SSM_GROUP), F32)
    return pl.pallas_call(body, name="ssm_disc", out_shape=[c1, c1, c16, c16])(lr, li, ldt, br, bi)


def _disc_bwd_call(lr, li, ldt, br, bi, dar, dai, dbbr, dbbi):
    def body(lr_ref, li_ref, ldt_ref, br_ref, bi_ref, dar_ref, dai_ref, dbbr_ref, dbbi_ref,
             dlr_ref, dli_ref, dldt_ref, dbr_ref, dbi_ref):
        _, vjp = jax.vjp(_disc, lr_ref[...], li_ref[...], ldt_ref[...], br_ref[...], bi_ref[...])
        dlr_ref[...], dli_ref[...], dldt_ref[...], dbr_ref[...], dbi_ref[...] = vjp(
            (dar_ref[...], dai_ref[...], dbbr_ref[...], dbbi_ref[...]))

    c1 = jax.ShapeDtypeStruct((SSM_NSTATE, 1), F32)
    c16 = jax.ShapeDtypeStruct((SSM_NSTATE, SSM_GROUP), F32)
    return pl.pallas_call(body, name="ssm_disc_bwd", out_shape=[c1, c1, c1, c16, c16])(
        lr, li, ldt, br, bi, dar, dai, dbbr, dbbi)


SSM_ROWS = 1024
SSM_CW = SSM_NSTATE // SSM_CHUNKS
SSM_CU = SSM_WIDTH // SSM_CHUNKS


def _cmul(ar, ai, br, bi):
    return ar * br - ai * bi, ar * bi + ai * br


def _power(ar1, ai1, n):
    res, base = None, (ar1, ai1)
    while n:
        if n & 1:
            res = base if res is None else _cmul(res[0], res[1], base[0], base[1])
        n >>= 1
        if n:
            base = _cmul(base[0], base[1], base[0], base[1])
    return res


def _tile(k):
    return pl.ds(pl.multiple_of(k * 8, 8), 8)


def _ssm_fwd_call(u, a_re, a_im, bb_re, bb_im, cm_re, cm_im, d_skip):
    L = u.shape[0]
    seg = L // 8
    rb = _fit(L, SSM_ROWS)

    def body(u_ref, ar_ref, ai_ref, bbr_ref, bbi_ref, cmr_ref, cmi_ref, d_ref, y_ref, sre_hbm, sim_hbm,
             s_re, s_im, sems):
        q = pl.program_id(0)

        def bu_step(r, c):
            rows = pl.ds(pl.multiple_of(r * rb, rb), rb)
            ub = u_ref[rows, :].astype(BF16)
            s_re[rows, :] = _dot(ub, bbr_ref[0], NN)
            s_im[rows, :] = _dot(ub, bbi_ref[0], NN)
            return c

        lax.fori_loop(0, L // rb, bu_step, 0)
        ar1, ai1 = ar_ref[...], ai_ref[...]
        ar = jnp.broadcast_to(ar1, (8, SSM_CW))
        ai = jnp.broadcast_to(ai1, (8, SSM_CW))

        def local(k, c):
            nr, ni = _cmul(ar, ai, c[0], c[1])
            nr = nr + s_re[_tile(k), :]
            ni = ni + s_im[_tile(k), :]
            s_re[_tile(k), :] = nr
            s_im[_tile(k), :] = ni
            return nr, ni

        zero8 = jnp.zeros((8, SSM_CW), F32)
        lax.fori_loop(0, seg, local, (zero8, zero8))
        pr, pi = _power(ar1, ai1, seg)
        end_r = s_re[pl.ds((seg - 1) * 8, 8), :]
        end_i = s_im[pl.ds((seg - 1) * 8, 8), :]
        er = jnp.zeros((1, SSM_CW), F32)
        ei = jnp.zeros((1, SSM_CW), F32)
        rows_r, rows_i = [er], [ei]
        for j in range(7):
            tr, ti = _cmul(pr, pi, er, ei)
            er, ei = end_r[j:j + 1] + tr, end_i[j:j + 1] + ti
            rows_r.append(er)
            rows_i.append(ei)
        e_r = jnp.concatenate(rows_r, axis=0)
        e_i = jnp.concatenate(rows_i, axis=0)

        def fix(k, c):
            wr, wi = _cmul(c[0], c[1], ar, ai)
            fr, fi = _cmul(wr, wi, e_r, e_i)
            s_re[_tile(k), :] += fr
            s_im[_tile(k), :] += fi
            return wr, wi

        lax.fori_loop(0, seg, fix, (jnp.ones((8, SSM_CW), F32), zero8))
        out_r = pltpu.make_async_copy(s_re, sre_hbm.at[q], sems.at[0])
        out_i = pltpu.make_async_copy(s_im, sim_hbm.at[q], sems.at[1])
        out_r.start()
        out_i.start()

        def y_step(r, c):
            rows = pl.ds(pl.multiple_of(r * rb, rb), rb)
            y = _dot(s_re[rows, :].astype(BF16), cmr_ref[0], NN) - _dot(s_im[rows, :].astype(BF16), cmi_ref[0], NN)
            y_ref[rows, :] = y + d_ref[...] * u_ref[rows, :]
            return c

        lax.fori_loop(0, L // rb, y_step, 0)
        out_r.wait()
        out_i.wait()

    chunk = lambda rows, cols: pl.BlockSpec((rows, cols), lambda q: (0, q))
    mat = lambda r, c: pl.BlockSpec((1, r, c), lambda q: (q, 0, 0))
    anyspec = pl.BlockSpec(memory_space=pl.ANY)
    states = jax.ShapeDtypeStruct((SSM_CHUNKS, L, SSM_CW), F32)
    return pl.pallas_call(
        body, name="ssm_fwd", grid=(SSM_CHUNKS,),
        in_specs=[chunk(L, SSM_CU), chunk(1, SSM_CW), chunk(1, SSM_CW), mat(SSM_CU, SSM_CW), mat(SSM_CU, SSM_CW),
                  mat(SSM_CW, SSM_CU), mat(SSM_CW, SSM_CU), chunk(1, SSM_CU)],
        out_specs=[chunk(L, SSM_CU), anyspec, anyspec],
        out_shape=[jax.ShapeDtypeStruct((L, SSM_WIDTH), F32), states, states],
        scratch_shapes=[pltpu.VMEM((L, SSM_CW), F32), pltpu.VMEM((L, SSM_CW), F32), pltpu.SemaphoreType.DMA((2,))],
        compiler_params=_params("arbitrary", vmem=VMEM_BIG))(u, a_re, a_im, bb_re, bb_im, cm_re, cm_im, d_skip)


def _ssm_bwd_call(dy, u, s_re_all, s_im_all, a_re, a_im, bb_re, bb_im, cm_re, cm_im, d_skip):
    L = u.shape[0]
    seg = L // 8
    rb = _fit(L, SSM_ROWS)

    def body(dy_ref, u_ref, sre_hbm, sim_hbm, ar_ref, ai_ref, bbr_ref, bbi_ref, cmr_ref, cmi_ref, d_ref,
             du_ref, dbbr_ref, dbbi_ref, dcmr_ref, dcmi_ref, dar_ref, dai_ref, dd_ref,
             g_re, g_im, s_re, s_im, sems):
        q = pl.program_id(0)
        in_r = pltpu.make_async_copy(sre_hbm.at[q], s_re, sems.at[0])
        in_i = pltpu.make_async_copy(sim_hbm.at[q], s_im, sems.at[1])
        in_r.start()
        in_i.start()

        def ds_step(r, c):
            rows = pl.ds(pl.multiple_of(r * rb, rb), rb)
            dyb = dy_ref[rows, :].astype(BF16)
            g_re[rows, :] = _dot(dyb, cmr_ref[0], NT)
            g_im[rows, :] = -_dot(dyb, cmi_ref[0], NT)
            return c

        lax.fori_loop(0, L // rb, ds_step, 0)
        ar1, ai1 = ar_ref[...], ai_ref[...]
        ar = jnp.broadcast_to(ar1, (8, SSM_CW))
        nai = jnp.broadcast_to(-ai1, (8, SSM_CW))

        def local(kk, c):
            k = seg - 1 - kk
            nr, ni = _cmul(ar, nai, c[0], c[1])
            nr = nr + g_re[_tile(k), :]
            ni = ni + g_im[_tile(k), :]
            g_re[_tile(k), :] = nr
            g_im[_tile(k), :] = ni
            return nr, ni

        zero8 = jnp.zeros((8, SSM_CW), F32)
        lax.fori_loop(0, seg, local, (zero8, zero8))
        pr, pi = _power(ar1, -ai1, seg)
        head_r = g_re[pl.ds(0, 8), :]
        head_i = g_im[pl.ds(0, 8), :]
        fr = jnp.zeros((1, SSM_CW), F32)
        fi = jnp.zeros((1, SSM_CW), F32)
        rows_r, rows_i = [fr], [fi]
        for j in range(6, -1, -1):
            tr, ti = _cmul(pr, pi, fr, fi)
            fr, fi = head_r[j + 1:j + 2] + tr, head_i[j + 1:j + 2] + ti
            rows_r.insert(0, fr)
            rows_i.insert(0, fi)
        f_r = jnp.concatenate(rows_r, axis=0)
        f_i = jnp.concatenate(rows_i, axis=0)
        in_r.wait()
        in_i.wait()

        def fixed(k, wr, wi):
            xr, xi = _cmul(wr, wi, f_r, f_i)
            gr = g_re[_tile(k), :] + xr
            gi = g_im[_tile(k), :] + xi
            g_re[_tile(k), :] = gr
            g_im[_tile(k), :] = gi
            return gr, gi

        def fix(kk, c):
            k = seg - 1 - kk
            wr, wi = _cmul(c[0], c[1], ar, nai)
            gr, gi = fixed(k, wr, wi)
            pr_, pi_ = s_re[_tile(k - 1), :], s_im[_tile(k - 1), :]
            return wr, wi, c[2] + gr * pr_ + gi * pi_, c[3] + gi * pr_ - gr * pi_

        wr, wi, acc_r, acc_i = lax.fori_loop(0, seg - 1, fix, (jnp.ones((8, SSM_CW), F32), zero8, zero8, zero8))
        wr, wi = _cmul(wr, wi, ar, nai)
        gr, gi = fixed(0, wr, wi)
        row8 = lax.broadcasted_iota(jnp.int32, (8, SSM_CW), 0)
        pr_ = jnp.where(row8 > 0, pltpu.roll(s_re[pl.ds((seg - 1) * 8, 8), :], 1, 0), 0.0)
        pi_ = jnp.where(row8 > 0, pltpu.roll(s_im[pl.ds((seg - 1) * 8, 8), :], 1, 0), 0.0)
        acc_r = acc_r + gr * pr_ + gi * pi_
        acc_i = acc_i + gi * pr_ - gr * pi_
        dar_ref[...] = jnp.sum(acc_r, axis=0, keepdims=True)
        dai_ref[...] = jnp.sum(acc_i, axis=0, keepdims=True)

        dbbr_ref[...] = jnp.zeros((1, SSM_CU, SSM_CW), F32)
        dbbi_ref[...] = jnp.zeros((1, SSM_CU, SSM_CW), F32)
        dcmr_ref[...] = jnp.zeros((1, SSM_CW, SSM_CU), F32)
        dcmi_ref[...] = jnp.zeros((1, SSM_CW, SSM_CU), F32)
        dd_ref[...] = jnp.zeros((1, SSM_CU), F32)

        def grad_step(r, c):
            rows = pl.ds(pl.multiple_of(r * rb, rb), rb)
            ub, dyv = u_ref[rows, :], dy_ref[rows, :]
            ubb, dyb = ub.astype(BF16), dyv.astype(BF16)
            grb, gib = g_re[rows, :].astype(BF16), g_im[rows, :].astype(BF16)
            dbbr_ref[0] += _dot(ubb, grb, TN)
            dbbi_ref[0] += _dot(ubb, gib, TN)
            dcmr_ref[0] += _dot(s_re[rows, :].astype(BF16), dyb, TN)
            dcmi_ref[0] -= _dot(s_im[rows, :].astype(BF16), dyb, TN)
            du_ref[rows, :] = _dot(grb, bbr_ref[0], NT) + _dot(gib, bbi_ref[0], NT) + d_ref[...] * dyv
            dd_ref[...] += jnp.sum(dyv * ub, axis=0, keepdims=True)
            return c

        lax.fori_loop(0, L // rb, grad_step, 0)

    chunk = lambda rows, cols: pl.BlockSpec((rows, cols), lambda q: (0, q))
    mat = lambda r, c: pl.BlockSpec((1, r, c), lambda q: (q, 0, 0))
    anyspec = pl.BlockSpec(memory_space=pl.ANY)
    big = lambda: pltpu.VMEM((L, SSM_CW), F32)
    return pl.pallas_call(
        body, name="ssm_bwd", grid=(SSM_CHUNKS,),
        in_specs=[chunk(L, SSM_CU), chunk(L, SSM_CU), anyspec, anyspec, chunk(1, SSM_CW), chunk(1, SSM_CW),
                  mat(SSM_CU, SSM_CW), mat(SSM_CU, SSM_CW), mat(SSM_CW, SSM_CU), mat(SSM_CW, SSM_CU), chunk(1, SSM_CU)],
        out_specs=[chunk(L, SSM_CU), mat(SSM_CU, SSM_CW), mat(SSM_CU, SSM_CW), mat(SSM_CW, SSM_CU), mat(SSM_CW, SSM_CU),
                   chunk(1, SSM_CW), chunk(1, SSM_CW), chunk(1, SSM_CU)],
        out_shape=[jax.ShapeDtypeStruct((L, SSM_WIDTH), F32),
                   jax.ShapeDtypeStruct((SSM_CHUNKS, SSM_CU, SSM_CW), F32), jax.ShapeDtypeStruct((SSM_CHUNKS, SSM_CU, SSM_CW), F32),
                   jax.ShapeDtypeStruct((SSM_CHUNKS, SSM_CW, SSM_CU), F32), jax.ShapeDtypeStruct((SSM_CHUNKS, SSM_CW, SSM_CU), F32),
                   jax.ShapeDtypeStruct((1, SSM_NSTATE), F32), jax.ShapeDtypeStruct((1, SSM_NSTATE), F32),
                   jax.ShapeDtypeStruct((1, SSM_WIDTH), F32)],
        scratch_shapes=[big(), big(), big(), big(), pltpu.SemaphoreType.DMA((2,))],
        compiler_params=_params("arbitrary", vmem=VMEM_BIG))(
            dy, u, s_re_all, s_im_all, a_re, a_im, bb_re, bb_im, cm_re, cm_im, d_skip)


def _place():
    return lax.axis_index("x"), lax.axis_index("y"), lax.axis_index("c")


def _small_gather_call(blocks, name):
    n = len(blocks)

    def body(*refs):
        start, finish = _exchange_phases(refs[:n], refs[n:2 * n], *refs[2 * n:], same_source=True)
        start()
        finish()

    return pl.pallas_call(
        body, name=name, in_specs=[ANY_SPEC] * n, out_specs=[ANY_SPEC] * n,
        out_shape=[jax.ShapeDtypeStruct((N_DEV,) + b.shape, b.dtype) for b in blocks],
        scratch_shapes=_comm_sems(n))(*blocks)


def _comm_sems(n):
    return [pltpu.SemaphoreType.DMA((7 * n,)), pltpu.SemaphoreType.DMA((7 * n,)), pltpu.SemaphoreType.DMA((n,))]


def _gather_phases(x_refs, out_refs, send_sems, recv_sems, local_sems):
    x, y, c = _place()
    me, sibling = (x, y, c), (x, y, 1 - c)
    chips = [(1 - x, y), (x, 1 - y), (1 - x, 1 - y)]
    n = len(x_refs)

    def copy(k, a, blk, to, from_input=False):
        slot = out_refs[a].at[4 * blk[0] + 2 * blk[1] + blk[2]]
        return pltpu.make_async_remote_copy(
            src_ref=x_refs[a] if from_input else slot, dst_ref=slot,
            send_sem=send_sems.at[k * n + a], recv_sem=recv_sems.at[k * n + a], device_id=to, device_id_type=MESH_ID)

    mine = [pltpu.make_async_copy(x_refs[a], out_refs[a].at[4 * x + 2 * y + c], local_sems.at[a]) for a in range(n)]
    first, passed = [], []
    for a in range(n):
        first.append(copy(0, a, me, sibling, True))
        first += [copy(1 + j, a, me, (*chip, c), True) for j, chip in enumerate(chips)]
        passed += [copy(4 + j, a, (*chip, c), sibling) for j, chip in enumerate(chips)]

    def start():
        for cp in mine + first:
            cp.start()

    def forward():
        for j, chip in enumerate(chips):
            for a in range(n):
                copy(1 + j, a, (*chip, c), me).wait_recv()
                passed[3 * a + j].start()

    def finish():
        for a in range(n):
            copy(0, a, sibling, me).wait_recv()
            for j, chip in enumerate(chips):
                copy(4 + j, a, (*chip, 1 - c), me).wait_recv()
        for cp in first + passed:
            cp.wait_send()
        for cp in mine:
            cp.wait()

    return start, forward, finish


def _exchange_phases(p_refs, out_refs, send_sems, recv_sems, local_sems, same_source=False):
    x, y, c = _place()
    me = 4 * x + 2 * y + c
    n = len(p_refs)

    def flip(k):
        px = 1 - x if k & 4 else x
        py = 1 - y if k & 2 else y
        pc = 1 - c if k & 1 else c
        return (px, py, pc), 4 * px + 2 * py + pc

    def source(a, slot):
        return p_refs[a] if same_source else p_refs[a].at[slot]

    def copy(k, a, landing):
        peer, peer_slot = flip(k)
        return pltpu.make_async_remote_copy(
            src_ref=source(a, peer_slot), dst_ref=out_refs[a].at[peer_slot if landing else me],
            send_sem=send_sems.at[(k - 1) * n + a], recv_sem=recv_sems.at[(k - 1) * n + a],
            device_id=peer, device_id_type=MESH_ID)

    mine = [pltpu.make_async_copy(source(a, me), out_refs[a].at[me], local_sems.at[a]) for a in range(n)]
    sends = [copy(k, a, False) for k in range(1, N_DEV) for a in range(n)]

    def start():
        for cp in mine + sends:
            cp.start()

    def finish():
        for k in range(1, N_DEV):
            for a in range(n):
                copy(k, a, True).wait_recv()
        for cp in sends:
            cp.wait_send()
        for cp in mine:
            cp.wait()

    return start, finish


def _adam_math(g, w, m, v):
    c1 = 1.0 / (1.0 - ADAM_B1 ** ADAM_STEP)
    c2 = 1.0 / (1.0 - ADAM_B2 ** ADAM_STEP)
    m_new = ADAM_B1 * m + (1.0 - ADAM_B1) * g
    v_new = ADAM_B2 * v + (1.0 - ADAM_B2) * (g * g)
    delta = -ADAM_LR * ((m_new * c1) / (jnp.sqrt(v_new * c2) + ADAM_EPS) + ADAM_WD * w)
    return g, delta, m_new, v_new


def _sum_slices(s_ref):
    g = s_ref[0].astype(F32)
    for k in range(1, N_DEV):
        g = g + s_ref[k].astype(F32)
    return g


def _adam_call(slices, w, m, v, name):
    d1, rest = w.shape[1], w.shape[2:]
    zeros = (0,) * len(rest)
    by_lanes = len(rest) == 1 and d1 > 256 and d1 % 16 != 0
    if by_lanes:
        tile = _fit(rest[0], 256)
        steps = rest[0] // tile
        own = pl.BlockSpec((1, d1, tile), lambda i: (0, 0, i))
        sl = pl.BlockSpec((N_DEV, 1, d1, tile), lambda i: (0, 0, 0, i))
    else:
        tile = _fit(d1, 256, 16) if len(rest) == 1 else _fit(d1, 8, 8)
        steps = d1 // tile
        own = pl.BlockSpec((1, tile) + rest, lambda i: (0, i) + zeros)
        sl = pl.BlockSpec((N_DEV, 1, tile) + rest, lambda i: (0, 0, i) + zeros)

    def body(s_ref, w_ref, m_ref, v_ref, g_ref, d_ref, mo_ref, vo_ref):
        g_ref[...], d_ref[...], mo_ref[...], vo_ref[...] = _adam_math(_sum_slices(s_ref), w_ref[...], m_ref[...], v_ref[...])

    out = jax.ShapeDtypeStruct(w.shape, F32)
    return pl.pallas_call(
        body, name=name, grid=(steps,), in_specs=[sl, own, own, own],
        out_specs=[own, own, own, own], out_shape=[out, out, out, out],
        compiler_params=_params("parallel"))(slices, w, m, v)


def _adam_small_call(rows_all, row_params, slices, params):
    nr, n = len(row_params), len(row_params) + len(params)

    def row_sum(rows_ref, a, width):
        g = rows_ref[0, pl.ds(a, 1), pl.ds(0, width)]
        for k in range(1, N_DEV):
            g = g + rows_ref[k, pl.ds(a, 1), pl.ds(0, width)]
        return g

    def body(rows_ref, *refs):
        slice_refs, wmv, outs = refs[:n - nr], refs[n - nr:n - nr + 3 * n], refs[n - nr + 3 * n:]
        outs[4 * n][...] = row_sum(rows_ref, nr, LANES)
        for a in range(n):
            w_ref, m_ref, v_ref = wmv[3 * a:3 * a + 3]
            if a < nr:
                g = row_sum(rows_ref, a, w_ref.shape[1])
            else:
                g = _sum_slices(slice_refs[a - nr])
            res = _adam_math(g, w_ref[...], m_ref[...], v_ref[...])
            for r in range(4):
                outs[4 * a + r][...] = res[r]

    every = list(row_params) + list(params)
    flat = pl.pallas_call(
        body, name="adam_small",
        out_shape=[jax.ShapeDtypeStruct(w.shape, F32) for w, _, _ in every for _ in range(4)]
        + [jax.ShapeDtypeStruct((1, LANES), F32)],
        compiler_params=pltpu.CompilerParams(vmem_limit_bytes=VMEM_BIG),
    )(rows_all, *slices, *[t for wmv in every for t in wmv])
    return [flat[4 * a:4 * a + 4] for a in range(n)], flat[4 * n][0, 0]


BIG = (("w_in", 1024, 404, 1), ("w_uq", 384, 96, 1), ("w_uk", 256, 64, 1), ("w_uv", 256, 64, 1),
       ("w_glu", 64, 512, 0), ("w_branch_attn", 512, 128, 1), ("w_branch_ssm", 512, 128, 1),
       ("w_out", 128, 1024, 0), ("w_up", 1024, 704, 1), ("w_down", 352, 1024, 0), ("conv_w", 3, 704, 1))
GATHER_FIRST, GATHER_PROJ, GATHER_LATER = BIG[:1], BIG[1:4], BIG[8:] + BIG[4:8]
GRADS_EARLY, GRADS_PROJ, GRADS_LAST = BIG[8:9] + BIG[10:] + BIG[4:8], BIG[1:4], BIG[:1]
GRADS_FIRST = BIG[9:10]
SMALL = (("mix_norm_pre", (1024,)), ("q_norm", (384,)), ("kv_norm", (256,)), ("ssm_lambda_re", (32, 64)),
         ("ssm_lambda_im", (32, 64)), ("ssm_log_dt", (32,)), ("ssm_b_re", (32, 64, 16)), ("ssm_b_im", (32, 64, 16)),
         ("ssm_c_re", (32, 16, 64)), ("ssm_c_im", (32, 16, 64)), ("ssm_d", (32, 16)), ("b_glu", (512,)),
         ("b_gate", (2048,)), ("mix_norm_post", (1024,)), ("ffn_norm_pre", (1024,)), ("conv_b", (5632,)),
         ("ffn_norm_post", (1024,)))


TRANSPOSED = ("w_in", "w_uq", "w_uk", "w_uv", "w_up")


STORED_SWAP = {**{name: (1, 2) for name in TRANSPOSED}, "ssm_b_re": (2, 3), "ssm_b_im": (2, 3), "ssm_d": (1, 2)}


def _stored(name, arr):
    return jnp.swapaxes(arr, *STORED_SWAP[name]) if name in STORED_SWAP else arr


def _to_slices(name, full, rows, cols, axis):
    if name in TRANSPOSED:
        return full.reshape(N_DEV, cols, rows)
    if axis == 1:
        return full.reshape(rows, N_DEV, cols).transpose(1, 0, 2)
    return full.reshape(N_DEV, rows, cols)


def _from_slices(name, parts, rows, cols, axis):
    if name in TRANSPOSED:
        return parts.reshape(N_DEV * cols, rows)
    if axis == 1:
        return parts.transpose(1, 0, 2).reshape(rows, N_DEV * cols)
    return parts.reshape(N_DEV * rows, cols)


def _time_perm(a, L):
    return a.reshape(8, L // 8, a.shape[-1]).transpose(1, 0, 2).reshape(L, a.shape[-1])


def _time_unperm(a, L):
    return a.reshape(L // 8, 8, a.shape[-1]).transpose(1, 0, 2).reshape(L, a.shape[-1])


def _block_diag(w, rows_first):
    eye = jnp.eye(8, dtype=w.dtype)
    g = w.reshape(SSM_CHUNKS, 8, w.shape[1], w.shape[2])
    return jnp.einsum("qgrc,gk->qgrkc", g, eye).reshape(SSM_CHUNKS, 8 * w.shape[1], 8 * w.shape[2])


def _block_diag_t(m, r, c):
    eye = jnp.eye(8, dtype=m.dtype)
    return jnp.einsum("qgrkc,gk->qgrc", m.reshape(SSM_CHUNKS, 8, r, 8, c), eye).reshape(SSM_GROUPS, r, c)


def kernel(x, positions, mix_norm_pre, w_in, q_norm, w_uq, kv_norm, w_uk, w_uv, ssm_lambda_re, ssm_lambda_im, ssm_log_dt, ssm_b_re, ssm_b_im, ssm_c_re, ssm_c_im, ssm_d, w_glu, b_glu, w_branch_attn, w_branch_ssm, b_gate, w_out, mix_norm_post, ffn_norm_pre, w_up, conv_w, conv_b, w_down, ffn_norm_post, loss_target, m_mix_norm_pre, m_w_in, m_q_norm, m_w_uq, m_kv_norm, m_w_uk, m_w_uv, m_ssm_lambda_re, m_ssm_lambda_im, m_ssm_log_dt, m_ssm_b_re, m_ssm_b_im, m_ssm_c_re, m_ssm_c_im, m_ssm_d, m_w_glu, m_b_glu, m_w_branch_attn, m_w_branch_ssm, m_b_gate, m_w_out, m_mix_norm_post, m_ffn_norm_pre, m_w_up, m_conv_w, m_conv_b, m_w_down, m_ffn_norm_post, v_mix_norm_pre, v_w_in, v_q_norm, v_w_uq, v_kv_norm, v_w_uk, v_w_uv, v_ssm_lambda_re, v_ssm_lambda_im, v_ssm_log_dt, v_ssm_b_re, v_ssm_b_im, v_ssm_c_re, v_ssm_c_im, v_ssm_d, v_w_glu, v_b_glu, v_w_branch_attn, v_w_branch_ssm, v_b_gate, v_w_out, v_mix_norm_post, v_ffn_norm_pre, v_w_up, v_conv_w, v_conv_b, v_w_down, v_ffn_norm_post):
    given = dict(locals())
    L = x.shape[1]
    xs = x[0]
    target = loss_target[0]

    def shard_bits(group):
        return [given[name][0] if name == "conv_w" else _stored(name, given[name])[0].astype(BF16) for name, _, _, _ in group]

    W = {}

    def unpack_weights(gathered, group):
        for (name, rows, cols, axis), parts in zip(group, gathered):
            W[name] = _from_slices(name, parts, rows, cols, axis)

    hn1, *gathered_w_in = _rms_fwd_call(xs, mix_norm_pre, "rms_pre", shard_bits(GATHER_FIRST))
    unpack_weights(gathered_w_in, GATHER_FIRST)

    wit = W["w_in"]
    zero_rows = lambda r: jnp.zeros((r, D_MODEL), BF16)
    kr_end = P_KR + QK_ROPE
    w_in_pt = jnp.concatenate(
        [wit[:P_KR], zero_rows(QK_NOPE), wit[P_KR:kr_end], zero_rows(LANES - QK_HEAD), wit[kr_end:]], axis=0)

    proj, *gathered_proj = _mm(hn1, w_in_pt, "mm_in", tb=True, tn=w_in_pt.shape[0], gather=shard_bits(GATHER_PROJ))
    unpack_weights(gathered_proj, GATHER_PROJ)
    head_rows = lambda wt, width: jnp.pad(wt.reshape(N_HEADS, width, wt.shape[1]), ((0, 0), (0, LANES - width), (0, 0)))
    w_uq_pt = head_rows(W["w_uq"], QK_HEAD).reshape(HEAD_PAD, Q_RANK)
    w_kv_pt = jnp.stack([head_rows(W["w_uk"], QK_NOPE), head_rows(W["w_uv"], V_HEAD)], axis=1
                        ).reshape(2 * HEAD_PAD, KV_RANK)
    half = jnp.arange(QK_ROPE // 2, dtype=F32)
    inv_freq = ROPE_THETA ** (-2.0 * half / QK_ROPE)
    inv_freq = jnp.pad(jnp.concatenate([inv_freq, inv_freq]), (QK_NOPE, LANES - QK_HEAD)).reshape(1, LANES)
    pos_col = positions.astype(F32).reshape(L, 1)
    qn, ckvn, q_r, kv_r, cosf, sinf = _mla_proj_call(proj, q_norm, kv_norm, w_uq_pt, w_kv_pt, pos_col, inv_freq)
    attn, lse, *gathered_later = _attn_fwd_call(q_r, kv_r, shard_bits(GATHER_LATER))
    unpack_weights(gathered_later, GATHER_LATER)
    w_ba_p = jnp.pad(W["w_branch_attn"].reshape(N_HEADS, V_HEAD, D_MODEL), ((0, 0), (0, LANES - V_HEAD), (0, 0))
                     ).reshape(HEAD_PAD, D_MODEL)

    col = lambda a: a.reshape(SSM_NSTATE, -1)
    lr_c, li_c = col(ssm_lambda_re[0]), col(ssm_lambda_im[0])
    ldt_c = col(jnp.broadcast_to(ssm_log_dt[0][:, None], (SSM_GROUPS, SSM_STATE)))
    br_c, bi_c = col(ssm_b_re[0]), col(ssm_b_im[0])
    a_re_c, a_im_c, bb_re_c, bb_im_c = _disc_call(lr_c, li_c, ldt_c, br_c, bi_c)
    a_re, a_im = a_re_c.reshape(1, SSM_NSTATE), a_im_c.reshape(1, SSM_NSTATE)
    to_bb = lambda b: _block_diag(b.reshape(SSM_GROUPS, SSM_STATE, SSM_GROUP).transpose(0, 2, 1), True).astype(BF16)
    bb_re, bb_im = to_bb(bb_re_c), to_bb(bb_im_c)
    to_cm = lambda c_: _block_diag(c_[0].transpose(0, 2, 1), True).astype(BF16)
    cm_re, cm_im = to_cm(ssm_c_re), to_cm(ssm_c_im)
    d_skip = ssm_d.reshape(1, SSM_WIDTH)
    u_p = proj.reshape(8, L // 8, proj.shape[1])[:, :, P_U:P_GATE].transpose(1, 0, 2).reshape(L, SSM_WIDTH)
    y1, s_re, s_im = _ssm_fwd_call(u_p, a_re, a_im, bb_re, bb_im, cm_re, cm_im, d_skip)
    w_glu_b = W["w_glu"]
    ssm_p = _glu_call(y1, w_glu_b, b_glu)
    ssm = _time_unperm(ssm_p, L)

    pa = _mm(attn, w_ba_p, "mm_ba")
    ps = _mm(ssm, W["w_branch_ssm"], "mm_bs")
    merged = _merge_call(proj, b_gate, pa, ps)
    wide = lambda dt: (D_MODEL, dt)
    o, x2, hn2 = _mm_rows(merged, W["w_out"], "mm_out", _post_mix_rows, [xs], [mix_norm_post, ffn_norm_pre],
                          [wide(F32), wide(F32), wide(BF16)], [])
    h = _mm(hn2, W["w_up"], "mm_up", tb=True, tn=D_FF)
    cw = W["conv_w"]
    act = _conv_act_call(h, cw, conv_b)
    dy, dff, loss_row, g_ffn_norm_post = _mm_rows(
        act, W["w_down"], "mm_down", _ffn_out_rows, [x2, target], [ffn_norm_post], [wide(F32), wide(BF16)],
        [LANES, D_MODEL], tk=1408)

    da = _mm(dff, W["w_down"], "mm_down_dx", tb=True, tn=D_FF)
    g_w_down = _mm_tn(act, dff, "mm_down_dw", tm=1408)
    dgate, dval, dcw_g, dcw_v, dcb_g, dcb_v, *received_first = _conv_act_bwd_call(
        da, h, cw, conv_b, [_to_slices(name, g_w_down, rows, cols, axis) for name, rows, cols, axis in GRADS_FIRST])
    g_conv_w = jnp.concatenate([dcw_g, dcw_v], axis=1)
    g_conv_b = jnp.concatenate([dcb_g, dcb_v], axis=1)
    dh = _conv_t_call(dgate, dval, cw)
    dx2, do, g_ffn_norm_pre, g_mix_norm_post = _mm_rows(
        dh, W["w_up"], "mm_up_dx", _post_bwd_rows, [x2, dy, o], [ffn_norm_pre, mix_norm_post], [wide(F32), wide(BF16)],
        [D_MODEL, D_MODEL], tk=1408)
    g_w_up = _mm_tn(dh, hn2, "mm_up_dw", tm=1408)
    dmerged = _mm(do, W["w_out"], "mm_out_dx", tb=True)
    g_w_out = _mm_tn(merged, do, "mm_out_dw")
    dpa, dps, dl0, dl1, db0, db1 = _merge_bwd_call(dmerged, proj, b_gate, pa, ps)
    g_b_gate = jnp.concatenate([db0, db1], axis=1)
    dattn = _mm(dpa, w_ba_p, "mm_ba_dx", tb=True, out_dtype=BF16)
    g_w_ba = _mm_tn(attn, dpa, "mm_ba_dw").reshape(N_HEADS, LANES, D_MODEL)[:, :V_HEAD].reshape(N_HEADS * V_HEAD, D_MODEL)
    dssm = _mm(dps, W["w_branch_ssm"], "mm_bs_dx", tb=True)
    g_w_bs = _mm_tn(ssm, dps, "mm_bs_dw")

    dy1, g_w_glu, g_b_glu = _glu_bwd_call(_time_perm(dssm, L), y1, w_glu_b, b_glu)
    du_p, dbb_re, dbb_im, dcm_re, dcm_im, da_re, da_im, g_ssm_d = _ssm_bwd_call(
        dy1, u_p, s_re, s_im, a_re, a_im, bb_re, bb_im, cm_re, cm_im, d_skip)
    du = _time_unperm(du_p, L)
    from_bb = lambda m: col(_block_diag_t(m, SSM_GROUP, SSM_STATE).transpose(0, 2, 1))
    dlr, dli, dldt, dbr, dbi = _disc_bwd_call(
        lr_c, li_c, ldt_c, br_c, bi_c, da_re.reshape(SSM_NSTATE, 1), da_im.reshape(SSM_NSTATE, 1), from_bb(dbb_re), from_bb(dbb_im))
    g_c_re = _block_diag_t(dcm_re, SSM_STATE, SSM_GROUP).transpose(0, 2, 1)
    g_c_im = _block_diag_t(dcm_im, SSM_STATE, SSM_GROUP).transpose(0, 2, 1)

    def grad_slices(group, grads):
        return [_to_slices(name, grads[name], rows, cols, axis) for name, rows, cols, axis in group]

    early_grads = {"w_up": g_w_up, "conv_w": g_conv_w, "w_glu": g_w_glu.astype(BF16),
                   "w_branch_attn": g_w_ba, "w_branch_ssm": g_w_bs, "w_out": g_w_out}
    b_stored = lambda d: d.reshape(SSM_GROUPS, SSM_STATE, SSM_GROUP).transpose(0, 2, 1)
    per_state = lambda d: d.reshape(SSM_GROUPS, SSM_STATE)
    ssm_partials = {"ssm_lambda_re": per_state(dlr), "ssm_lambda_im": per_state(dli),
                    "ssm_b_re": b_stored(dbr), "ssm_b_im": b_stored(dbi),
                    "ssm_c_re": g_c_re, "ssm_c_im": g_c_im, "ssm_d": g_ssm_d.reshape(SSM_GROUPS, SSM_GROUP).T}
    ssm_shapes = [(name, ssm_partials[name].shape) for name, _ in SMALL if name in ssm_partials]
    dq, dkv, *landed = _attn_bwd_call(
        q_r, kv_r, attn, dattn, lse, grad_slices(GRADS_EARLY, early_grads),
        [ssm_partials[name].reshape(-1, LANES) if len(shp) == 3 else ssm_partials[name].reshape((1,) + shp)
         for name, shp in ssm_shapes])
    received_early = landed[:len(GRADS_EARLY)]
    ssm_all = {name: got.reshape((N_DEV, 1) + shp) for (name, shp), got in zip(ssm_shapes, landed[len(GRADS_EARLY):])}
    dq_p, dkv_p, dlatent, g_q_norm, g_kv_norm = _mla_proj_bwd_call(
        dq, dkv, cosf, sinf, proj, q_norm, kv_norm, w_uq_pt, w_kv_pt)
    g_w_uq = _mm_tn(dq_p, qn, "mm_uq_dw").reshape(N_HEADS, LANES, Q_RANK)[:, :QK_HEAD].reshape(N_HEADS * QK_HEAD, Q_RANK)
    g_w_kv = _mm_tn(ckvn, dkv_p, "mm_ukv_dw").T.reshape(N_HEADS, 2, LANES, KV_RANK)
    g_w_uk = g_w_kv[:, 0, :QK_NOPE].reshape(N_HEADS * QK_NOPE, KV_RANK)
    g_w_uv = g_w_kv[:, 1, :V_HEAD].reshape(N_HEADS * V_HEAD, KV_RANK)
    dproj = jnp.concatenate([dlatent, du.astype(BF16), dl0, dl1], axis=1)
    proj_grads = {"w_uq": g_w_uq, "w_uk": g_w_uk, "w_uv": g_w_uv}
    g_w_in_pt, *received_proj = _mm_tn(dproj, hn1, "mm_in_dw", tm=1664, exchange=grad_slices(GRADS_PROJ, proj_grads))
    g_w_in = jnp.concatenate([g_w_in_pt[:P_KR], g_w_in_pt[P_KR + QK_NOPE:P_KR + QK_HEAD], g_w_in_pt[P_U:]], axis=0)
    grad_x, g_mix_norm_pre, *received_last = _mm_in_dx_call(
        dproj, w_in_pt, xs, dx2, mix_norm_pre, grad_slices(GRADS_LAST, {"w_in": g_w_in}))

    results = {}
    wmv = lambda name: tuple(_stored(name, given[prefix + name]) for prefix in ("", "m_", "v_"))
    unstored = lambda name, res: [_stored(name, r) for r in res]
    whole = ("w_uq", "w_uk", "w_uv", "w_glu", "w_branch_attn", "w_branch_ssm", "conv_w")
    landed_small = dict(ssm_all)
    for group, received in ((GRADS_FIRST, received_first), (GRADS_EARLY, received_early), (GRADS_PROJ, received_proj),
                            (GRADS_LAST, received_last)):
        for (name, _, _, _), rec in zip(group, received):
            if name in whole:
                landed_small[name] = rec[:, None]
            else:
                results[name] = unstored(name, _adam_call(rec[:, None], *wmv(name), "adam_" + name))

    vec_grads = {"mix_norm_pre": g_mix_norm_pre, "q_norm": g_q_norm, "kv_norm": g_kv_norm,
                 "ssm_log_dt": jnp.sum(dldt.reshape(SSM_GROUPS, SSM_STATE), axis=1),
                 "b_glu": g_b_glu, "b_gate": g_b_gate, "mix_norm_post": g_mix_norm_post,
                 "ffn_norm_pre": g_ffn_norm_pre, "ffn_norm_post": g_ffn_norm_post}
    vec_names = [name for name, _ in SMALL if name in vec_grads]
    width = max(shp[0] for name, shp in SMALL if name in vec_grads)
    rows = [jnp.pad(vec_grads[name].reshape(1, -1), ((0, 0), (0, width - vec_grads[name].size))) for name in vec_names]
    rows.append(jnp.pad(loss_row, ((0, 0), (0, width - LANES))))
    rows.append(jnp.zeros((-len(rows) % 8, width), F32))
    rows_all, landed_small["conv_b"] = _small_gather_call([jnp.concatenate(rows, axis=0), g_conv_b], "gather_small_grads")
    others = ["conv_b"] + [name for name, _ in ssm_shapes] + list(whole)
    small_results, loss = _adam_small_call(
        rows_all, [wmv(n) for n in vec_names], [landed_small[n] for n in others], [wmv(n) for n in others])
    for name, res in zip(vec_names + others, small_results):
        results[name] = unstored(name, res)

    order = ["mix_norm_pre", "w_in", "q_norm", "w_uq", "kv_norm", "w_uk", "w_uv", "ssm_lambda_re", "ssm_lambda_im",
             "ssm_log_dt", "ssm_b_re", "ssm_b_im", "ssm_c_re", "ssm_c_im", "ssm_d", "w_glu", "b_glu", "w_branch_attn",
             "w_branch_ssm", "b_gate", "w_out", "mix_norm_post", "ffn_norm_pre", "w_up", "conv_w", "conv_b", "w_down",
             "ffn_norm_post"]
    outs = [loss, grad_x[None]]
    for kind in range(4):
        outs += [results[name][kind] for name in order]
    return tuple(outs)
```
